```python
import math
import jax, jax.numpy as jnp
from jax import lax
import numpy as np

D_MODEL = 1024
BATCH = 8
SEQ = 4096
DEPTH = 4

D_CONV = D_MODEL
CONV_GROUPS = 8
SHORT_CONV_K = 3
D_RNN = D_MODEL
RNN_HEADS = 4
RNN_HEAD_DIM = D_RNN // RNN_HEADS
RNN_CONV_K = 4
LRU_C = 8.0
N_BRANCH = 2
D_FF = 2816
D_IN = 3 * D_CONV + 2 * D_RNN + N_BRANCH * D_MODEL
SPLITS = (D_CONV, 2 * D_CONV, 3 * D_CONV, 3 * D_CONV + D_RNN, 3 * D_CONV + 2 * D_RNN)
ALPHA = (2.0 * DEPTH) ** 0.25
BETA = (8.0 * DEPTH) ** -0.25
LN_EPS = 1e-5

kernel_name = "hybrid_shortconv_rglru_macaron_deepnorm"


def layer_norm(x, g, b):
    xf = x.astype(jnp.float32)
    mu = jnp.mean(xf, axis=-1, keepdims=True)
    var = jnp.mean(jnp.square(xf - mu), axis=-1, keepdims=True)
    y = (xf - mu) * lax.rsqrt(var + LN_EPS)
    return (y * g.astype(jnp.float32) + b.astype(jnp.float32)).astype(x.dtype)


def causal_depthwise_conv(x, w, k):
    s = x.shape[1]
    xp = jnp.pad(x, ((0, 0), (k - 1, 0), (0, 0)))
    out = xp[:, 0:s, :] * w[0]
    for i in range(1, k):
        out = out + xp[:, i:i + s, :] * w[i]
    return out


def swiglu(x, w1, w2):
    g, u = jnp.split(x @ w1, 2, axis=-1)
    return (jax.nn.silu(g) * u) @ w2


def rg_lru(x, gate_in, gate_rec, a_param):
    s = x.shape[1]
    log_a = -LRU_C * gate_rec * jax.nn.softplus(-a_param)
    a = jnp.exp(log_a)
    mult = jnp.sqrt(-jnp.expm1(2.0 * log_a))
    is_start = (jnp.arange(s) == 0)[None, :, None]
    mult = jnp.where(is_start, 1.0, mult)
    b = x * gate_in * mult

    def combine(left, right):
        a1, b1 = left
        a2, b2 = right
        return a1 * a2, a2 * b1 + b2

    _, h = lax.associative_scan(combine, (a, b), axis=1)
    return h


def hybrid_mixer(x, w_in, b_merge, sc_w, rc_w, rc_b, rg_w, rg_b, a_param,
                 w_out_conv, w_out_rnn, w_o):
    bsz, s, _ = x.shape
    proj = x @ w_in
    b_gate, c_gate, v, x_r, y_r, g_logits = jnp.split(proj, SPLITS, axis=-1)

    y_a = (b_gate * causal_depthwise_conv(c_gate * v, sc_w, SHORT_CONV_K)) @ w_out_conv

    xc = causal_depthwise_conv(x_r, rc_w, RNN_CONV_K) + rc_b
    xh = xc.reshape(bsz, s, RNN_HEADS, RNN_HEAD_DIM)
    gates = jnp.einsum('bshi,ghij->gbshj', xh, rg_w).reshape(2, bsz, s, D_RNN)
    gates = jax.nn.sigmoid((gates + rg_b[:, None, None, :]).astype(jnp.float32))
    h = rg_lru(xc.astype(jnp.float32), gates[0], gates[1],
               a_param.astype(jnp.float32)).astype(x.dtype)
    y_b = (h * jax.nn.gelu(y_r)) @ w_out_rnn

    g_a, g_b = jnp.split(jax.nn.sigmoid(g_logits + b_merge), 2, axis=-1)
    return (g_a * y_a + g_b * y_b) @ w_o


def _fwd_setup_inputs(seed: int = 0) -> dict:
    key = jax.random.key(seed)
    ks = jax.random.split(key, 20)
    f32 = jnp.float32
    nrm = lambda k, shape, scale: jax.random.normal(k, shape, f32) * scale

    x = jax.random.normal(ks[0], (BATCH, SEQ, D_MODEL), f32)
    w_in = nrm(ks[1], (DEPTH, D_MODEL, D_IN), D_MODEL ** -0.5)
    b_merge = nrm(ks[2], (DEPTH, N_BRANCH * D_MODEL), 0.1)
    sc_w = nrm(ks[3], (DEPTH, SHORT_CONV_K, D_CONV), SHORT_CONV_K ** -0.5)
    rc_w = nrm(ks[4], (DEPTH, RNN_CONV_K, D_RNN), RNN_CONV_K ** -0.5)
    rc_b = nrm(ks[5], (DEPTH, D_RNN), 0.02)
    rg_w = nrm(ks[6], (DEPTH, 2, RNN_HEADS, RNN_HEAD_DIM, RNN_HEAD_DIM), RNN_HEAD_DIM ** -0.5)
    rg_b = nrm(ks[7], (DEPTH, 2, D_RNN), 0.1)
    u = jax.random.uniform(ks[8], (DEPTH, D_RNN), f32, 0.9, 0.999)
    s_a = u ** (1.0 / LRU_C)
    a_param = jnp.log(s_a) - jnp.log1p(-s_a)
    w_out_conv = nrm(ks[9], (DEPTH, D_CONV, D_MODEL), D_CONV ** -0.5)
    w_out_rnn = nrm(ks[10], (DEPTH, D_RNN, D_MODEL), D_RNN ** -0.5)
    w_o = nrm(ks[11], (DEPTH, D_MODEL, D_MODEL), BETA * D_MODEL ** -0.5)
    ffn_w1 = nrm(ks[12], (DEPTH, 2, D_MODEL, 2 * D_FF), D_MODEL ** -0.5)
    ffn_w2 = nrm(ks[13], (DEPTH, 2, D_FF, D_MODEL), BETA * D_FF ** -0.5)
    ln_g = 1.0 + nrm(ks[14], (DEPTH, 3, D_MODEL), 0.02)
    ln_b = nrm(ks[15], (DEPTH, 3, D_MODEL), 0.02)
    return {"x": x, "w_in": w_in, "b_merge": b_merge, "sc_w": sc_w, "rc_w": rc_w,
            "rc_b": rc_b, "rg_w": rg_w, "rg_b": rg_b, "a_param": a_param,
            "w_out_conv": w_out_conv, "w_out_rnn": w_out_rnn, "w_o": w_o,
            "ffn_w1": ffn_w1, "ffn_w2": ffn_w2, "ln_g": ln_g, "ln_b": ln_b}


def _fwd_reference(x, w_in, b_merge, sc_w, rc_w, rc_b, rg_w, rg_b, a_param,
              w_out_conv, w_out_rnn, w_o, ffn_w1, ffn_w2, ln_g, ln_b):
    for l in range(DEPTH):
        x = layer_norm(ALPHA * x + 0.5 * swiglu(x, ffn_w1[l, 0], ffn_w2[l, 0]),
                       ln_g[l, 0], ln_b[l, 0])
        mix = hybrid_mixer(x, w_in[l], b_merge[l], sc_w[l], rc_w[l], rc_b[l], rg_w[l],
                           rg_b[l], a_param[l], w_out_conv[l], w_out_rnn[l], w_o[l])
        x = layer_norm(ALPHA * x + mix, ln_g[l, 1], ln_b[l, 1])
        x = layer_norm(ALPHA * x + 0.5 * swiglu(x, ffn_w1[l, 1], ffn_w2[l, 1]),
                       ln_g[l, 2], ln_b[l, 2])
    return x


import jax as _jax
import jax.numpy as _jnp

TWIN_FORMAT = 'train_step'
FWD_PARAMS = ['x', 'w_in', 'b_merge', 'sc_w', 'rc_w', 'rc_b', 'rg_w', 'rg_b', 'a_param', 'w_out_conv', 'w_out_rnn', 'w_o', 'ffn_w1', 'ffn_w2', 'ln_g', 'ln_b']
TWIN_WEIGHTS = ['w_in', 'b_merge', 'sc_w', 'rc_w', 'rc_b', 'rg_w', 'rg_b', 'a_param', 'w_out_conv', 'w_out_rnn', 'w_o', 'ffn_w1', 'ffn_w2', 'ln_g', 'ln_b']
TWIN_DIFF_INPUT = 'x'
TWIN_INPUTS = ['x', 'w_in', 'b_merge', 'sc_w', 'rc_w', 'rc_b', 'rg_w', 'rg_b', 'a_param', 'w_out_conv', 'w_out_rnn', 'w_o', 'ffn_w1', 'ffn_w2', 'ln_g', 'ln_b', 'loss_target', 'm_w_in', 'm_b_merge', 'm_sc_w', 'm_rc_w', 'm_rc_b', 'm_rg_w', 'm_rg_b', 'm_a_param', 'm_w_out_conv', 'm_w_out_rnn', 'm_w_o', 'm_ffn_w1', 'm_ffn_w2', 'm_ln_g', 'm_ln_b', 'v_w_in', 'v_b_merge', 'v_sc_w', 'v_rc_w', 'v_rc_b', 'v_rg_w', 'v_rg_b', 'v_a_param', 'v_w_out_conv', 'v_w_out_rnn', 'v_w_o', 'v_ffn_w1', 'v_ffn_w2', 'v_ln_g', 'v_ln_b']
TWIN_OUTPUTS = ['loss', 'grad_x', 'grad_w_in', 'grad_b_merge', 'grad_sc_w', 'grad_rc_w', 'grad_rc_b', 'grad_rg_w', 'grad_rg_b', 'grad_a_param', 'grad_w_out_conv', 'grad_w_out_rnn', 'grad_w_o', 'grad_ffn_w1', 'grad_ffn_w2', 'grad_ln_g', 'grad_ln_b', 'delta_w_in', 'delta_b_merge', 'delta_sc_w', 'delta_rc_w', 'delta_rc_b', 'delta_rg_w', 'delta_rg_b', 'delta_a_param', 'delta_w_out_conv', 'delta_w_out_rnn', 'delta_w_o', 'delta_ffn_w1', 'delta_ffn_w2', 'delta_ln_g', 'delta_ln_b', 'new_m_w_in', 'new_m_b_merge', 'new_m_sc_w', 'new_m_rc_w', 'new_m_rc_b', 'new_m_rg_w', 'new_m_rg_b', 'new_m_a_param', 'new_m_w_out_conv', 'new_m_w_out_rnn', 'new_m_w_o', 'new_m_ffn_w1', 'new_m_ffn_w2', 'new_m_ln_g', 'new_m_ln_b', 'new_v_w_in', 'new_v_b_merge', 'new_v_sc_w', 'new_v_rc_w', 'new_v_rc_b', 'new_v_rg_w', 'new_v_rg_b', 'new_v_a_param', 'new_v_w_out_conv', 'new_v_w_out_rnn', 'new_v_w_o', 'new_v_ffn_w1', 'new_v_ffn_w2', 'new_v_ln_g', 'new_v_ln_b']
TWIN_LEAF_KINDS = {'loss': 'loss', 'grad_x': 'grad_x', 'grad_w_in': 'grad_w', 'grad_b_merge': 'grad_w', 'grad_sc_w': 'grad_w', 'grad_rc_w': 'grad_w', 'grad_rc_b': 'grad_w', 'grad_rg_w': 'grad_w', 'grad_rg_b': 'grad_w', 'grad_a_param': 'grad_w', 'grad_w_out_conv': 'grad_w', 'grad_w_out_rnn': 'grad_w', 'grad_w_o': 'grad_w', 'grad_ffn_w1': 'grad_w', 'grad_ffn_w2': 'grad_w', 'grad_ln_g': 'grad_w', 'grad_ln_b': 'grad_w', 'delta_w_in': 'delta_w', 'delta_b_merge': 'delta_w', 'delta_sc_w': 'delta_w', 'delta_rc_w': 'delta_w', 'delta_rc_b': 'delta_w', 'delta_rg_w': 'delta_w', 'delta_rg_b': 'delta_w', 'delta_a_param': 'delta_w', 'delta_w_out_conv': 'delta_w', 'delta_w_out_rnn': 'delta_w', 'delta_w_o': 'delta_w', 'delta_ffn_w1': 'delta_w', 'delta_ffn_w2': 'delta_w', 'delta_ln_g': 'delta_w', 'delta_ln_b': 'delta_w', 'new_m_w_in': 'new_m', 'new_m_b_merge': 'new_m', 'new_m_sc_w': 'new_m', 'new_m_rc_w': 'new_m', 'new_m_rc_b': 'new_m', 'new_m_rg_w': 'new_m', 'new_m_rg_b': 'new_m', 'new_m_a_param': 'new_m', 'new_m_w_out_conv': 'new_m', 'new_m_w_out_rnn': 'new_m', 'new_m_w_o': 'new_m', 'new_m_ffn_w1': 'new_m', 'new_m_ffn_w2': 'new_m', 'new_m_ln_g': 'new_m', 'new_m_ln_b': 'new_m', 'new_v_w_in': 'new_v', 'new_v_b_merge': 'new_v', 'new_v_sc_w': 'new_v', 'new_v_rc_w': 'new_v', 'new_v_rc_b': 'new_v', 'new_v_rg_w': 'new_v', 'new_v_rg_b': 'new_v', 'new_v_a_param': 'new_v', 'new_v_w_out_conv': 'new_v', 'new_v_w_out_rnn': 'new_v', 'new_v_w_o': 'new_v', 'new_v_ffn_w1': 'new_v', 'new_v_ffn_w2': 'new_v', 'new_v_ln_g': 'new_v', 'new_v_ln_b': 'new_v'}


def _forward(args):
    return _fwd_reference(*[args[k] for k in FWD_PARAMS])


def _output_shape():
    def fwd():
        inp = _fwd_setup_inputs(0)
        return _fwd_reference(*[inp[k] for k in FWD_PARAMS])
    out = _jax.eval_shape(fwd)
    return out.shape, out.dtype

N_MICROBATCH = 1
ADAM_LR = 0.001
ADAM_B1 = 0.9
ADAM_B2 = 0.999
ADAM_EPS = 1e-08
ADAM_WD = 0.01
ADAM_STEP = 10
PER_EXAMPLE_BATCH_AXIS = {'x': 0, 'loss_target': 0}
SHARED_INPUTS = []
_WEIGHT_DTYPES = {'w_in': _jnp.float32, 'b_merge': _jnp.float32, 'sc_w': _jnp.float32, 'rc_w': _jnp.float32, 'rc_b': _jnp.float32, 'rg_w': _jnp.float32, 'rg_b': _jnp.float32, 'a_param': _jnp.float32, 'w_out_conv': _jnp.float32, 'w_out_rnn': _jnp.float32, 'w_o': _jnp.float32, 'ffn_w1': _jnp.float32, 'ffn_w2': _jnp.float32, 'ln_g': _jnp.float32, 'ln_b': _jnp.float32}
MOMENT_SCALE = {'w_in': 1.788824e-02, 'b_merge': 7.755061e-03, 'sc_w': 2.474875e-02, 'rc_w': 1.307674e-02, 'rc_b': 1.522860e-01, 'rg_w': 4.630662e-03, 'rg_b': 3.812261e-03, 'a_param': 6.137957e-03, 'w_out_conv': 2.480426e-02, 'w_out_rnn': 1.316265e-02, 'w_o': 6.629457e-02, 'ffn_w1': 8.508531e-03, 'ffn_w2': 3.306419e-02, 'ln_g': 9.361811e+00, 'ln_b': 8.028426e-01}


def _to_microbatches(a, axis):
    t = _jnp.moveaxis(a, axis, 0)
    t = t.reshape((N_MICROBATCH, t.shape[0] // N_MICROBATCH) + t.shape[1:])
    return _jnp.moveaxis(t, 1, axis + 1)


def setup_inputs(seed: int = 0) -> dict:
    inp = _fwd_setup_inputs(seed)
    key = _jax.random.fold_in(_jax.random.key(seed), 7919)
    shape, _ = _output_shape()
    out = dict(inp)
    out["loss_target"] = _jax.random.normal(_jax.random.fold_in(key, 0), shape, _jnp.float32)
    for i, name in enumerate(TWIN_WEIGHTS):
        w = inp[name].astype(_jnp.float32)
        if MOMENT_SCALE is None:
            s = _jnp.sqrt(_jnp.mean(_jnp.square(w)) + 1e-30)
        else:
            s = MOMENT_SCALE[name]
        km, kv = _jax.random.split(_jax.random.fold_in(key, i + 1))
        out[name] = w
        out["m_" + name] = s * _jax.random.normal(km, w.shape, _jnp.float32)
        out["v_" + name] = (s * s) * _jax.random.uniform(kv, w.shape, _jnp.float32, 0.5, 1.5)
    if N_MICROBATCH > 1:
        for name, axis in PER_EXAMPLE_BATCH_AXIS.items():
            out[name] = _to_microbatches(out[name], axis)
    return {'x': out['x'], 'w_in': out['w_in'], 'b_merge': out['b_merge'], 'sc_w': out['sc_w'], 'rc_w': out['rc_w'], 'rc_b': out['rc_b'], 'rg_w': out['rg_w'], 'rg_b': out['rg_b'], 'a_param': out['a_param'], 'w_out_conv': out['w_out_conv'], 'w_out_rnn': out['w_out_rnn'], 'w_o': out['w_o'], 'ffn_w1': out['ffn_w1'], 'ffn_w2': out['ffn_w2'], 'ln_g': out['ln_g'], 'ln_b': out['ln_b'], 'loss_target': out['loss_target'], 'm_w_in': out['m_w_in'], 'm_b_merge': out['m_b_merge'], 'm_sc_w': out['m_sc_w'], 'm_rc_w': out['m_rc_w'], 'm_rc_b': out['m_rc_b'], 'm_rg_w': out['m_rg_w'], 'm_rg_b': out['m_rg_b'], 'm_a_param': out['m_a_param'], 'm_w_out_conv': out['m_w_out_conv'], 'm_w_out_rnn': out['m_w_out_rnn'], 'm_w_o': out['m_w_o'], 'm_ffn_w1': out['m_ffn_w1'], 'm_ffn_w2': out['m_ffn_w2'], 'm_ln_g': out['m_ln_g'], 'm_ln_b': out['m_ln_b'], 'v_w_in': out['v_w_in'], 'v_b_merge': out['v_b_merge'], 'v_sc_w': out['v_sc_w'], 'v_rc_w': out['v_rc_w'], 'v_rc_b': out['v_rc_b'], 'v_rg_w': out['v_rg_w'], 'v_rg_b': out['v_rg_b'], 'v_a_param': out['v_a_param'], 'v_w_out_conv': out['v_w_out_conv'], 'v_w_out_rnn': out['v_w_out_rnn'], 'v_w_o': out['v_w_o'], 'v_ffn_w1': out['v_ffn_w1'], 'v_ffn_w2': out['v_ffn_w2'], 'v_ln_g': out['v_ln_g'], 'v_ln_b': out['v_ln_b']}


def _loss(weights, diff, rest, loss_target):
    with _jax.named_scope("forward"):
        args = {**rest, TWIN_DIFF_INPUT: diff, **{k: w.astype(_WEIGHT_DTYPES[k]) for k, w in weights.items()}}
        y = _forward(args)
    with _jax.named_scope("loss_head"):
        err = _jnp.square(y.astype(_jnp.float32) - loss_target)
        return 0.5 * _jnp.sum(_jnp.mean(err, axis=-1)) if err.ndim else 0.5 * err


def _adamw(w, g, m, v):
    m = ADAM_B1 * m + (1.0 - ADAM_B1) * g
    v = ADAM_B2 * v + (1.0 - ADAM_B2) * _jnp.square(g)
    m_hat = m / (1.0 - ADAM_B1 ** ADAM_STEP)
    v_hat = v / (1.0 - ADAM_B2 ** ADAM_STEP)
    delta = -ADAM_LR * (m_hat / (_jnp.sqrt(v_hat) + ADAM_EPS) + ADAM_WD * w)
    return delta, m, v


def reference(x, w_in, b_merge, sc_w, rc_w, rc_b, rg_w, rg_b, a_param, w_out_conv, w_out_rnn, w_o, ffn_w1, ffn_w2, ln_g, ln_b, loss_target, m_w_in, m_b_merge, m_sc_w, m_rc_w, m_rc_b, m_rg_w, m_rg_b, m_a_param, m_w_out_conv, m_w_out_rnn, m_w_o, m_ffn_w1, m_ffn_w2, m_ln_g, m_ln_b, v_w_in, v_b_merge, v_sc_w, v_rc_w, v_rc_b, v_rg_w, v_rg_b, v_a_param, v_w_out_conv, v_w_out_rnn, v_w_o, v_ffn_w1, v_ffn_w2, v_ln_g, v_ln_b):
    given = dict(x=x, w_in=w_in, b_merge=b_merge, sc_w=sc_w, rc_w=rc_w, rc_b=rc_b, rg_w=rg_w, rg_b=rg_b, a_param=a_param, w_out_conv=w_out_conv, w_out_rnn=w_out_rnn, w_o=w_o, ffn_w1=ffn_w1, ffn_w2=ffn_w2, ln_g=ln_g, ln_b=ln_b, loss_target=loss_target, m_w_in=m_w_in, m_b_merge=m_b_merge, m_sc_w=m_sc_w, m_rc_w=m_rc_w, m_rc_b=m_rc_b, m_rg_w=m_rg_w, m_rg_b=m_rg_b, m_a_param=m_a_param, m_w_out_conv=m_w_out_conv, m_w_out_rnn=m_w_out_rnn, m_w_o=m_w_o, m_ffn_w1=m_ffn_w1, m_ffn_w2=m_ffn_w2, m_ln_g=m_ln_g, m_ln_b=m_ln_b, v_w_in=v_w_in, v_b_merge=v_b_merge, v_sc_w=v_sc_w, v_rc_w=v_rc_w, v_rc_b=v_rc_b, v_rg_w=v_rg_w, v_rg_b=v_rg_b, v_a_param=v_a_param, v_w_out_conv=v_w_out_conv, v_w_out_rnn=v_w_out_rnn, v_w_o=v_w_o, v_ffn_w1=v_ffn_w1, v_ffn_w2=v_ffn_w2, v_ln_g=v_ln_g, v_ln_b=v_ln_b)
    weights = {n: given[n] for n in TWIN_WEIGHTS}
    shared = {n: given[n] for n in SHARED_INPUTS}
    per_example = {n: given[n] for n in ['x']}
    grad_fn = _jax.value_and_grad(_loss, argnums=(0, 1))

    def one_microbatch(ex, loss_target):
        ex = dict(ex)
        diff = ex.pop(TWIN_DIFF_INPUT)
        return grad_fn(weights, diff, {**shared, **ex}, loss_target)

    if N_MICROBATCH == 1:
        loss, (grad_w, grad_x) = one_microbatch(per_example, given["loss_target"])
    else:
        def body(carry, xs):
            loss_sum, grad_sum = carry
            l_k, (gw_k, gx_k) = one_microbatch(xs[0], xs[1])
            with _jax.named_scope("update"):
                return (loss_sum + l_k, _jax.tree.map(_jnp.add, grad_sum, gw_k)), gx_k

        init = (_jnp.zeros((), _jnp.float32), _jax.tree.map(_jnp.zeros_like, weights))
        (loss, grad_w), grad_x = _jax.lax.scan(body, init, (per_example, given["loss_target"]))
    with _jax.named_scope("update"):
        delta_w, new_m, new_v = {}, {}, {}
        for n in TWIN_WEIGHTS:
            delta_w[n], new_m[n], new_v[n] = _adamw(weights[n], grad_w[n], given["m_" + n], given["v_" + n])
    return (loss, grad_x, *[grad_w[n] for n in TWIN_WEIGHTS], *[delta_w[n] for n in TWIN_WEIGHTS],
            *[new_m[n] for n in TWIN_WEIGHTS], *[new_v[n] for n in TWIN_WEIGHTS])
```

```python
import functools

import jax
import jax.numpy as jnp
from jax import lax
from jax.experimental import pallas as pl
from jax.experimental.pallas import tpu as pltpu

F32 = jnp.float32
BF16 = jnp.bfloat16
SDS = jax.ShapeDtypeStruct

N_DEV = 8
DEPTH = 4
D = 1024
D_FF = 2816
FS = D_FF // 4
W2S = D_FF // 8
D_IN = 7 * D
WIN_S = D_IN // 8
HEADS = 4
HD = D // HEADS
LRU_C = 8.0
ALPHA = (2.0 * DEPTH) ** 0.25
LN_EPS = 1e-5
ADAM_LR, ADAM_B1, ADAM_B2, ADAM_EPS, ADAM_WD, ADAM_STEP = 0.001, 0.9, 0.999, 1e-08, 0.01, 10

R_SC, R_RC, R_RGB, R_LNG, R_LNB = 0, 3, 7, 9, 12
SMALL_ROWS = 16
G_BM, G_RCB, G_AP = 15, 17, 18
GRAD_ROWS = 24

NN = ((1,), (0,))
NT = ((1,), (1,))
TN = ((0,), (0,))
MESH = pl.DeviceIdType.MESH
ANY = pl.BlockSpec(memory_space=pl.ANY)
VMEM_LIMIT = 52 * 1024 * 1024


def _dot(a, b, dims):
    return lax.dot_general(a, b, (dims, ((), ())), preferred_element_type=F32)


def _cp(*sem):
    return pltpu.CompilerParams(dimension_semantics=sem, vmem_limit_bytes=VMEM_LIMIT)


def _sigmoid(x):
    return 1.0 / (1.0 + jnp.exp(-x))


def _log1p(e):
    u = 1.0 + e
    return jnp.where(u == 1.0, e, jnp.log(u) * e / jnp.where(u == 1.0, 1.0, u - 1.0))


def _softplus(x):
    return jnp.maximum(x, 0.0) + _log1p(jnp.exp(-jnp.abs(x)))


def _neg_expm1(x):
    u = jnp.exp(x)
    um1 = u - 1.0
    safe = jnp.logical_and(u != 1.0, um1 != -1.0)
    r = um1 * x / jnp.where(safe, jnp.log(jnp.where(safe, u, 0.5)), 1.0)
    return -jnp.where(u == 1.0, x, jnp.where(um1 == -1.0, -1.0, r))


def _gelu(y):
    c = 0.7978845608028654
    t = jnp.tanh(c * (y + 0.044715 * y * y * y))
    return 0.5 * y * (1.0 + t), t


def _gelu_grad(y, t):
    c = 0.7978845608028654
    return 0.5 * (1.0 + t) + 0.5 * y * (1.0 - t * t) * c * (1.0 + 3.0 * 0.044715 * y * y)


def _ln_fwd(z, g, b):
    mu = jnp.mean(z, axis=-1, keepdims=True)
    zc = z - mu
    var = jnp.mean(zc * zc, axis=-1, keepdims=True)
    rstd = lax.rsqrt(var + LN_EPS)
    xh = zc * rstd
    return xh * g + b, xh, rstd


def _ln_bwd(dy, xh, rstd, g):
    dxh = dy * g
    m1 = jnp.mean(dxh, axis=-1, keepdims=True)
    m2 = jnp.mean(dxh * xh, axis=-1, keepdims=True)
    return rstd * (dxh - m1 - xh * m2)


def _row_tile(t, want):
    return min(want, t)


def _me():
    return 4 * lax.axis_index("x") + 2 * lax.axis_index("y") + lax.axis_index("c")


def _coords(p):
    return (p // 4, (p // 2) % 2, p % 2)


def _all_to_all_copies(srcs_of, dsts_of, waits, send_sems, recv_sems, loc_sems):
    me = _me()
    n = len(waits)
    own_src, own_dst = srcs_of(me), dsts_of(me)
    local = [pltpu.make_async_copy(own_src[k], own_dst[k], loc_sems.at[k]) for k in range(n)]
    for cp in local:
        cp.start()
    for d in range(1, N_DEV):
        p = (me + d) % N_DEV
        src, dst = srcs_of(p), dsts_of(me)
        for k in range(n):
            pltpu.make_async_remote_copy(
                src_ref=src[k], dst_ref=dst[k], send_sem=send_sems.at[waits[k][1]],
                recv_sem=recv_sems.at[waits[k][1]], device_id=_coords(p), device_id_type=MESH).start()
    done = set()
    for k in range(n):
        ref, s = waits[k]
        if s in done:
            continue
        done.add(s)
        pltpu.make_async_remote_copy(
            src_ref=ref, dst_ref=ref, send_sem=send_sems.at[s], recv_sem=recv_sems.at[s],
            device_id=_coords(me), device_id_type=MESH).wait()
    for cp in local:
        cp.wait()


def gather_layer(l, win_b, w3_b, w1_b, w2_b, rgw_b, small):
    out_shape = (SDS((D, D_IN), BF16), SDS((3, D, D), BF16), SDS((2, N_DEV, D, FS), BF16),
                 SDS((2, D_FF, D), BF16), SDS((2, HEADS, HD, HD), BF16), SDS((SMALL_ROWS, D), F32))

    def body(win, w3, w1, w2, rgw, sm, o_win, o_w3, o_w1, o_w2, o_rgw, o_sm, send_sems, recv_sems, loc_sems):
        def srcs_of(p):
            return [win.at[l], w3.at[l], w1.at[l], w2.at[l], rgw.at[l], sm.at[l]]

        def dsts_of(p):
            return [o_win.at[:, pl.ds(pl.multiple_of(p * WIN_S, 128), WIN_S)],
                    o_w3.at[:, pl.ds(pl.multiple_of(p * 128, 128), 128), :],
                    o_w1.at[:, p],
                    o_w2.at[:, pl.ds(pl.multiple_of(p * W2S, 16), W2S), :],
                    o_rgw.at[:, :, pl.ds(pl.multiple_of(p * 32, 32), 32), :],
                    o_sm.at[:, pl.ds(pl.multiple_of(p * 128, 128), 128)]]

        waits = [(o_win.at[:, pl.ds(0, 7 * WIN_S)], 0), (o_w3.at[:, pl.ds(0, 7 * 128), :], 1),
                 (o_w1.at[:, pl.ds(0, 7)], 2), (o_w2.at[:, pl.ds(0, 7 * W2S), :], 3),
                 (o_rgw.at[:, :, pl.ds(0, 7 * 32), :], 4), (o_sm.at[:, pl.ds(0, 7 * 128)], 5)]
        _all_to_all_copies(srcs_of, dsts_of, waits, send_sems, recv_sems, loc_sems)

    return pl.pallas_call(
        body, name=f"gather_layer{l}", out_shape=out_shape,
        in_specs=[ANY] * 6, out_specs=(ANY,) * 6,
        scratch_shapes=[pltpu.SemaphoreType.DMA((6,)), pltpu.SemaphoreType.DMA((6,)), pltpu.SemaphoreType.DMA((6,))],
        compiler_params=pltpu.CompilerParams(has_side_effects=True),
    )(win_b, w3_b, w1_b, w2_b, rgw_b, small)


def scatter_layer(l, dwin, dwoc, dwor, dwo, dw1a, dw1b, dw2a, dw2b, drgw):
    out_shape = (SDS((N_DEV, D, WIN_S), BF16), SDS((N_DEV, 3, 128, D), BF16), SDS((N_DEV, 2, D, FS), BF16),
                 SDS((N_DEV, 2, W2S, D), BF16), SDS((N_DEV, 2, HEADS, 32, HD), BF16))

    def body(dwin, dwoc, dwor, dwo, dw1a, dw1b, dw2a, dw2b, drgw, l_win, l_w3, l_w1, l_w2, l_rgw,
             send_sems, recv_sems, loc_sems):
        def srcs_of(p):
            rows3 = pl.ds(pl.multiple_of(p * 128, 128), 128)
            rows2 = pl.ds(pl.multiple_of(p * W2S, 16), W2S)
            return [dwin.at[:, pl.ds(pl.multiple_of(p * WIN_S, 128), WIN_S)],
                    dwoc.at[rows3, :], dwor.at[rows3, :], dwo.at[rows3, :],
                    dw1a.at[p], dw1b.at[p], dw2a.at[rows2, :], dw2b.at[rows2, :],
                    drgw.at[:, :, pl.ds(pl.multiple_of(p * 32, 32), 32), :]]

        def dsts_of(p):
            return [l_win.at[p], l_w3.at[p, 0], l_w3.at[p, 1], l_w3.at[p, 2], l_w1.at[p, 0], l_w1.at[p, 1],
                    l_w2.at[p, 0], l_w2.at[p, 1], l_rgw.at[p]]

        lands = [l_win, l_w3, l_w1, l_w2, l_rgw]
        sem_of = [0, 1, 1, 1, 2, 2, 3, 3, 4]
        waits = [(lands[s].at[pl.ds(0, 7)], s) for s in sem_of]
        _all_to_all_copies(srcs_of, dsts_of, waits, send_sems, recv_sems, loc_sems)

    return pl.pallas_call(
        body, name=f"scatter_layer{l}", out_shape=out_shape,
        in_specs=[ANY] * 9, out_specs=(ANY,) * 5,
        scratch_shapes=[pltpu.SemaphoreType.DMA((5,)), pltpu.SemaphoreType.DMA((5,)), pltpu.SemaphoreType.DMA((9,))],
        compiler_params=pltpu.CompilerParams(has_side_effects=True),
    )(dwin, dwoc, dwor, dwo, dw1a, dw1b, dw2a, dw2b, drgw)


def gather_small_grads(gsmall):
    shp = gsmall.shape

    def body(g, land, send_sems, recv_sems, loc_sems):
        _all_to_all_copies(lambda p: [g], lambda p: [land.at[p]], [(land.at[pl.ds(0, 7)], 0)],
                           send_sems, recv_sems, loc_sems)

    return pl.pallas_call(
        body, name="gather_small_grads", out_shape=SDS((N_DEV,) + shp, F32),
        in_specs=[ANY], out_specs=ANY,
        scratch_shapes=[pltpu.SemaphoreType.DMA((1,)), pltpu.SemaphoreType.DMA((1,)), pltpu.SemaphoreType.DMA((1,))],
        compiler_params=pltpu.CompilerParams(has_side_effects=True),
    )(gsmall)


def ffn_up(xb, w1, f):
    t = xb.shape[0]
    tm = _row_tile(t, 512)

    def body(x_ref, wg_ref, wu_ref, g_ref, u_ref, a_ref):
        x = x_ref[...]
        g = _dot(x, wg_ref[...], NN)
        u = _dot(x, wu_ref[...], NN)
        g_ref[...] = g.astype(BF16)
        u_ref[...] = u.astype(BF16)
        a_ref[...] = (g * _sigmoid(g) * u).astype(BF16)

    out = pl.BlockSpec((None, tm, FS), lambda j, i: (j, i, 0))
    return pl.pallas_call(
        body, name=f"ffn_up{f}", grid=(4, t // tm),
        in_specs=[pl.BlockSpec((tm, D), lambda j, i: (i, 0)),
                  pl.BlockSpec((None, None, D, FS), lambda j, i: (f, j, 0, 0)),
                  pl.BlockSpec((None, None, D, FS), lambda j, i: (f, j + 4, 0, 0))],
        out_specs=(out, out, out), out_shape=(SDS((4, t, FS), BF16),) * 3,
        compiler_params=_cp("parallel", "parallel"),
    )(xb, w1, w1)


def ffn_down_ln(a, w2, f, xf, small, s):
    t = xf.shape[0]
    tm = _row_tile(t, 256)

    def body(a_ref, w_ref, x_ref, sm_ref, xo_ref, xb_ref, xh_ref, rs_ref):
        acc = _dot(a_ref[0], w_ref[0:FS, :], NN)
        for j in range(1, 4):
            acc = acc + _dot(a_ref[j], w_ref[j * FS:(j + 1) * FS, :], NN)
        z = ALPHA * x_ref[...] + 0.5 * acc
        y, xh, rstd = _ln_fwd(z, sm_ref[R_LNG + s:R_LNG + s + 1, :], sm_ref[R_LNB + s:R_LNB + s + 1, :])
        xo_ref[...] = y
        xb_ref[...] = y.astype(BF16)
        xh_ref[...] = xh
        rs_ref[...] = rstd

    row = pl.BlockSpec((tm, D), lambda i: (i, 0))
    return pl.pallas_call(
        body, name=f"ffn_down_ln{f}", grid=(t // tm,),
        in_specs=[pl.BlockSpec((4, tm, FS), lambda i: (0, i, 0)),
                  pl.BlockSpec((None, D_FF, D), lambda i: (f, 0, 0)),
                  row, pl.BlockSpec((SMALL_ROWS, D), lambda i: (0, 0))],
        out_specs=(row, row, row, pl.BlockSpec((tm, 1), lambda i: (i, 0))),
        out_shape=(SDS((t, D), F32), SDS((t, D), BF16), SDS((t, D), F32), SDS((t, 1), F32)),
        compiler_params=_cp("parallel"),
    )(a, w2, xf, small)


def ffn_bwd_gates(dxn, xh, rstd, small, s, w2, f, g, u):
    t = dxn.shape[0]
    tm = _row_tile(t, 256)

    def body(dy_ref, xh_ref, rs_ref, sm_ref, w_ref, g_ref, u_ref, dz_ref, df_ref, dgu_ref, dln_ref):
        i = pl.program_id(0)
        dy = dy_ref[...]
        xhat = xh_ref[...]
        dz = _ln_bwd(dy, xhat, rs_ref[...], sm_ref[R_LNG + s:R_LNG + s + 1, :])

        @pl.when(i == 0)
        def _():
            dln_ref[...] = jnp.zeros_like(dln_ref)

        dln_ref[0:1, :] += jnp.sum(dy * xhat, axis=0, keepdims=True)
        dln_ref[1:2, :] += jnp.sum(dy, axis=0, keepdims=True)
        dz_ref[...] = dz
        df = (0.5 * dz).astype(BF16)
        df_ref[...] = df
        for j in range(4):
            da = _dot(df, w_ref[j * FS:(j + 1) * FS, :], NT)
            gg = g_ref[j].astype(F32)
            uu = u_ref[j].astype(F32)
            sg = _sigmoid(gg)
            dgu_ref[j] = (da * uu * (sg * (1.0 + gg * (1.0 - sg)))).astype(BF16)
            dgu_ref[j + 4] = (da * (gg * sg)).astype(BF16)

    row = pl.BlockSpec((tm, D), lambda i: (i, 0))
    gu = pl.BlockSpec((4, tm, FS), lambda i: (0, i, 0))
    return pl.pallas_call(
        body, name=f"ffn_bwd_gates{f}", grid=(t // tm,),
        in_specs=[row, row, pl.BlockSpec((tm, 1), lambda i: (i, 0)), pl.BlockSpec((SMALL_ROWS, D), lambda i: (0, 0)),
                  pl.BlockSpec((None, D_FF, D), lambda i: (f, 0, 0)), gu, gu],
        out_specs=(row, row, pl.BlockSpec((8, tm, FS), lambda i: (0, i, 0)), pl.BlockSpec((2, D), lambda i: (0, 0))),
        out_shape=(SDS((t, D), F32), SDS((t, D), BF16), SDS((8, t, FS), BF16), SDS((2, D), F32)),
        compiler_params=_cp("arbitrary"),
    )(dxn, xh, rstd, small, w2, g, u)


def ffn_bwd_dx(dz, dgu, w1, f):
    t = dz.shape[0]
    tm = _row_tile(t, 1024)

    def body(dz_ref, d_ref, w_ref, dx_ref, acc):
        k = pl.program_id(1)

        @pl.when(k == 0)
        def _():
            acc[...] = ALPHA * dz_ref[...]

        acc[...] += _dot(d_ref[...], w_ref[...], NT)

        @pl.when(k == 7)
        def _():
            dx_ref[...] = acc[...]

    row = pl.BlockSpec((tm, D), lambda i, k: (i, 0))
    return pl.pallas_call(
        body, name=f"ffn_bwd_dx{f}", grid=(t // tm, 8),
        in_specs=[row, pl.BlockSpec((None, tm, FS), lambda i, k: (k, i, 0)),
                  pl.BlockSpec((None, None, D, FS), lambda i, k: (f, k, 0, 0))],
        out_specs=row, out_shape=SDS((t, D), F32),
        scratch_shapes=[pltpu.VMEM((tm, D), F32)],
        compiler_params=_cp("parallel", "arbitrary"),
    )(dz, dgu, w1)


def _mm_tn(name, a, a_spec, b, b_spec, out_sds, out_spec, grid, acc_shape):
    nk = grid[-1]

    def body(a_ref, b_ref, o_ref, acc):
        k = pl.program_id(len(grid) - 1)

        @pl.when(k == 0)
        def _():
            acc[...] = jnp.zeros_like(acc)

        acc[...] += _dot(a_ref[...], b_ref[...], TN)

        @pl.when(k == nk - 1)
        def _():
            o_ref[...] = acc[...].astype(o_ref.dtype)

    return pl.pallas_call(
        body, name=name, grid=grid, in_specs=[a_spec, b_spec], out_specs=out_spec, out_shape=out_sds,
        scratch_shapes=[pltpu.VMEM(acc_shape, F32)],
        compiler_params=_cp(*(["parallel"] * (len(grid) - 1) + ["arbitrary"])),
    )(a, b)


def ffn_dw1(xb, dgu, f):
    t = xb.shape[0]
    tk = _row_tile(t, 512)
    return _mm_tn(f"ffn_dw1_{f}", xb, pl.BlockSpec((tk, D), lambda j, k: (k, 0)),
                  dgu, pl.BlockSpec((None, tk, FS), lambda j, k: (j, k, 0)),
                  SDS((8, D, FS), BF16), pl.BlockSpec((None, D, FS), lambda j, k: (j, 0, 0)),
                  (8, t // tk), (D, FS))


def ffn_dw2(a, df, f):
    t = df.shape[0]
    tk = _row_tile(t, 512)
    return _mm_tn(f"ffn_dw2_{f}", a, pl.BlockSpec((None, tk, FS), lambda j, k: (j, k, 0)),
                  df, pl.BlockSpec((tk, D), lambda j, k: (k, 0)),
                  SDS((D_FF, D), BF16), pl.BlockSpec((FS, D), lambda j, k: (j, 0)),
                  (4, t // tk), (FS, D))


def mm_tn_square(name, a, b):
    t = a.shape[0]
    tk = _row_tile(t, 512)
    return _mm_tn(name, a, pl.BlockSpec((tk, 512), lambda i, k: (k, i)),
                  b, pl.BlockSpec((tk, D), lambda i, k: (k, 0)),
                  SDS((D, D), BF16), pl.BlockSpec((512, D), lambda i, k: (i, 0)),
                  (2, t // tk), (512, D))


def mix_proj(xb, win):
    t = xb.shape[0]
    tm = _row_tile(t, 512)

    def body(x_ref, w_ref, o_ref):
        o_ref[...] = _dot(x_ref[...], w_ref[...], NN)

    return pl.pallas_call(
        body, name="mix_proj", grid=(7, t // tm),
        in_specs=[pl.BlockSpec((tm, D), lambda n, i: (i, 0)), pl.BlockSpec((D, D), lambda n, i: (0, n))],
        out_specs=pl.BlockSpec((tm, D), lambda n, i: (i, n)), out_shape=SDS((t, D_IN), F32),
        compiler_params=_cp("parallel", "parallel"),
    )(xb, win)


def _pcol(tm, k):
    return pl.BlockSpec((tm, D), lambda i: (i, k))


def _prev_halo(tm, k):
    return pl.BlockSpec((8, D), lambda i: (jnp.maximum(i * (tm // 8) - 1, 0), k))


def _next_halo(tm, t, k):
    return pl.BlockSpec((8, D), lambda i: (jnp.minimum((i + 1) * (tm // 8), t // 8 - 1), k))


def _full(shape):
    nd = len(shape)
    return pl.BlockSpec(shape, lambda i: (0,) * nd)


def mix_pre(p, small, rcb, ap, rgw):
    t = p.shape[0]
    tm = _row_tile(t, 256)

    def body(bg_ref, cg_ref, v_ref, xr_ref, cgh_ref, vh_ref, xrh_ref, sm_ref, rcb_ref, ap_ref, rgw_ref,
             ca_ref, pa_ref, xc_ref, xcb_ref, gi_ref, gr_ref, a_ref, bb_ref, ext1, ext2):
        i = pl.program_id(0)
        first = i == 0
        cv = cg_ref[...] * v_ref[...]
        ext1[0:8, :] = jnp.where(first, 0.0, cgh_ref[...] * vh_ref[...])
        ext1[8:, :] = cv
        xr = xr_ref[...]
        ext2[0:8, :] = jnp.where(first, 0.0, xrh_ref[...])
        ext2[8:, :] = xr
        ca = (sm_ref[R_SC:R_SC + 1, :] * ext1[pl.ds(6, tm), :] + sm_ref[R_SC + 1:R_SC + 2, :] * ext1[pl.ds(7, tm), :]
              + sm_ref[R_SC + 2:R_SC + 3, :] * cv)
        ca_ref[...] = ca
        pa_ref[...] = (bg_ref[...] * ca).astype(BF16)
        xc = (sm_ref[R_RC:R_RC + 1, :] * ext2[pl.ds(5, tm), :] + sm_ref[R_RC + 1:R_RC + 2, :] * ext2[pl.ds(6, tm), :]
              + sm_ref[R_RC + 2:R_RC + 3, :] * ext2[pl.ds(7, tm), :] + sm_ref[R_RC + 3:R_RC + 4, :] * xr
              + rcb_ref[...])
        xc_ref[...] = xc
        xcb = xc.astype(BF16)
        xcb_ref[...] = xcb
        g0, g1 = [], []
        for h in range(HEADS):
            xh = xcb[:, h * HD:(h + 1) * HD]
            g0.append(_dot(xh, rgw_ref[0, h], NN))
            g1.append(_dot(xh, rgw_ref[1, h], NN))
        gi = _sigmoid(jnp.concatenate(g0, axis=1) + sm_ref[R_RGB:R_RGB + 1, :])
        gr = _sigmoid(jnp.concatenate(g1, axis=1) + sm_ref[R_RGB + 1:R_RGB + 2, :])
        gi_ref[...] = gi
        gr_ref[...] = gr
        la = (-LRU_C) * gr * _softplus(-ap_ref[...])
        a_ref[...] = jnp.exp(la)
        row = lax.broadcasted_iota(jnp.int32, (tm, D), 0) + i * tm
        mult = jnp.where(row == 0, 1.0, jnp.sqrt(_neg_expm1(2.0 * la)))
        bb_ref[...] = xc * gi * mult

    row = pl.BlockSpec((tm, D), lambda i: (i, 0))
    f32o, b16o = SDS((t, D), F32), SDS((t, D), BF16)
    return pl.pallas_call(
        body, name="mix_pre", grid=(t // tm,),
        in_specs=[_pcol(tm, 0), _pcol(tm, 1), _pcol(tm, 2), _pcol(tm, 3),
                  _prev_halo(tm, 1), _prev_halo(tm, 2), _prev_halo(tm, 3),
                  _full((SMALL_ROWS, D)), _full((1, D)), _full((1, D)), _full((2, HEADS, HD, HD))],
        out_specs=(row,) * 8, out_shape=(f32o, b16o, f32o, b16o, f32o, f32o, f32o, f32o),
        scratch_shapes=[pltpu.VMEM((tm + 8, D), F32), pltpu.VMEM((tm + 8, D), F32)],
        compiler_params=_cp("parallel"),
    )(p, p, p, p, p, p, p, small, rcb, ap, rgw)


SCAN_LANES = 512


def scan_fwd(a, b):
    t = a.shape[0]
    cb = SCAN_LANES

    def body(a_ref, b_ref, h_ref):
        row8 = lax.broadcasted_iota(jnp.int32, (8, cb), 0)

        def step(g, carry):
            r = pl.multiple_of(g * 8, 8)
            aa = a_ref[pl.ds(r, 8), :]
            bb = b_ref[pl.ds(r, 8), :]
            for s in (1, 2, 4):
                a_sh = jnp.where(row8 >= s, pltpu.roll(aa, s, 0), 1.0)
                b_sh = jnp.where(row8 >= s, pltpu.roll(bb, s, 0), 0.0)
                bb = aa * b_sh + bb
                aa = aa * a_sh
            h = aa * carry + bb
            h_ref[pl.ds(r, 8), :] = h
            return jnp.broadcast_to(h[7:8, :], (8, cb))

        lax.fori_loop(0, t // 8, step, jnp.zeros((8, cb), F32))

    blk = pl.BlockSpec((t, cb), lambda c: (0, c))
    return pl.pallas_call(
        body, name="scan_fwd", grid=(D // cb,), in_specs=[blk, blk], out_specs=blk, out_shape=SDS((t, D), F32),
        compiler_params=_cp("parallel"),
    )(a, b)


def scan_bwd(a, dh):
    t = a.shape[0]
    cb = SCAN_LANES
    ng = t // 8

    def body(a_ref, d_ref, l_ref):
        row8 = lax.broadcasted_iota(jnp.int32, (8, cb), 0)

        def step(g, carry):
            lam_next, a_next = carry
            r = pl.multiple_of((ng - 1 - g) * 8, 8)
            a_grp = a_ref[pl.ds(r, 8), :]
            aa = jnp.where(row8 < 7, pltpu.roll(a_grp, 7, 0), a_next)
            bb = d_ref[pl.ds(r, 8), :]
            for s in (1, 2, 4):
                a_sh = jnp.where(row8 < 8 - s, pltpu.roll(aa, 8 - s, 0), 1.0)
                b_sh = jnp.where(row8 < 8 - s, pltpu.roll(bb, 8 - s, 0), 0.0)
                bb = aa * b_sh + bb
                aa = aa * a_sh
            lam = aa * lam_next + bb
            l_ref[pl.ds(r, 8), :] = lam
            return (jnp.broadcast_to(lam[0:1, :], (8, cb)), jnp.broadcast_to(a_grp[0:1, :], (8, cb)))

        z = jnp.zeros((8, cb), F32)
        lax.fori_loop(0, ng, step, (z, z))

    blk = pl.BlockSpec((t, cb), lambda c: (0, c))
    return pl.pallas_call(
        body, name="scan_bwd", grid=(D // cb,), in_specs=[blk, blk], out_specs=blk, out_shape=SDS((t, D), F32),
        compiler_params=_cp("parallel"),
    )(a, dh)


def mix_out(pa, h, p, bm, w3, xf, small):
    t = xf.shape[0]
    tm = _row_tile(t, 256)

    def body(pa_ref, h_ref, yr_ref, gla_ref, glb_ref, bma_ref, bmb_ref, w_ref, x_ref, sm_ref,
             pb_ref, ya_ref, yb_ref, m_ref, xo_ref, xb_ref, xh_ref, rs_ref):
        ge, _ = _gelu(yr_ref[...])
        pb = (h_ref[...] * ge).astype(BF16)
        pb_ref[...] = pb
        ya = _dot(pa_ref[...], w_ref[0], NN)
        yb = _dot(pb, w_ref[1], NN)
        ya_ref[...] = ya
        yb_ref[...] = yb
        ga = _sigmoid(gla_ref[...] + bma_ref[...])
        gb = _sigmoid(glb_ref[...] + bmb_ref[...])
        m = (ga * ya + gb * yb).astype(BF16)
        m_ref[...] = m
        z = ALPHA * x_ref[...] + _dot(m, w_ref[2], NN)
        y, xh, rstd = _ln_fwd(z, sm_ref[R_LNG + 1:R_LNG + 2, :], sm_ref[R_LNB + 1:R_LNB + 2, :])
        xo_ref[...] = y
        xb_ref[...] = y.astype(BF16)
        xh_ref[...] = xh
        rs_ref[...] = rstd

    row = pl.BlockSpec((tm, D), lambda i: (i, 0))
    f32o, b16o = SDS((t, D), F32), SDS((t, D), BF16)
    return pl.pallas_call(
        body, name="mix_out", grid=(t // tm,),
        in_specs=[row, row, _pcol(tm, 4), _pcol(tm, 5), _pcol(tm, 6),
                  pl.BlockSpec((1, D), lambda i: (0, 0)), pl.BlockSpec((1, D), lambda i: (0, 1)),
                  _full((3, D, D)), row, _full((SMALL_ROWS, D))],
        out_specs=(row,) * 7 + (pl.BlockSpec((tm, 1), lambda i: (i, 0)),),
        out_shape=(b16o, f32o, f32o, b16o, f32o, b16o, f32o, SDS((t, 1), F32)),
        compiler_params=_cp("parallel"),
    )(pa, h, p, p, p, bm, bm, w3, xf, small)


def mixb_head(dxn, xh, rstd, small, w3, p, bm, ya, yb, ca, h):
    t = dxn.shape[0]
    tm = _row_tile(t, 256)

    def body(dy_ref, xh_ref, rs_ref, sm_ref, w_ref, bg_ref, yr_ref, gla_ref, glb_ref, bma_ref, bmb_ref,
             ya_ref, yb_ref, ca_ref, h_ref,
             dz_ref, dzb_ref, dya_ref, dyb_ref, dbg_ref, dca_ref, dh_ref, dphi_ref, dln_ref, dbm_ref):
        i = pl.program_id(0)
        dy = dy_ref[...]
        xhat = xh_ref[...]
        dz = _ln_bwd(dy, xhat, rs_ref[...], sm_ref[R_LNG + 1:R_LNG + 2, :])

        @pl.when(i == 0)
        def _():
            dln_ref[...] = jnp.zeros_like(dln_ref)
            dbm_ref[...] = jnp.zeros_like(dbm_ref)

        dln_ref[0:1, :] += jnp.sum(dy * xhat, axis=0, keepdims=True)
        dln_ref[1:2, :] += jnp.sum(dy, axis=0, keepdims=True)
        dz_ref[...] = dz
        dzb = dz.astype(BF16)
        dzb_ref[...] = dzb
        dm = _dot(dzb, w_ref[2], NT)
        ga = _sigmoid(gla_ref[...] + bma_ref[...])
        gb = _sigmoid(glb_ref[...] + bmb_ref[...])
        dya = (dm * ga).astype(BF16)
        dyb = (dm * gb).astype(BF16)
        dya_ref[...] = dya
        dyb_ref[...] = dyb
        dgla = dm * ya_ref[...] * ga * (1.0 - ga)
        dglb = dm * yb_ref[...] * gb * (1.0 - gb)
        dbm_ref[0:1, :] += jnp.sum(dgla, axis=0, keepdims=True)
        dbm_ref[1:2, :] += jnp.sum(dglb, axis=0, keepdims=True)
        dphi_ref[:, D:2 * D] = dgla.astype(BF16)
        dphi_ref[:, 2 * D:3 * D] = dglb.astype(BF16)
        dpa = _dot(dya, w_ref[0], NT)
        dpb = _dot(dyb, w_ref[1], NT)
        dbg_ref[...] = (dpa * ca_ref[...]).astype(BF16)
        dca_ref[...] = dpa * bg_ref[...]
        yr = yr_ref[...]
        ge, th = _gelu(yr)
        dh_ref[...] = dpb * ge
        dphi_ref[:, 0:D] = (dpb * h_ref[...] * _gelu_grad(yr, th)).astype(BF16)

    row = pl.BlockSpec((tm, D), lambda i: (i, 0))
    f32o, b16o = SDS((t, D), F32), SDS((t, D), BF16)
    acc2 = pl.BlockSpec((2, D), lambda i: (0, 0))
    return pl.pallas_call(
        body, name="mixb_head", grid=(t // tm,),
        in_specs=[row, row, pl.BlockSpec((tm, 1), lambda i: (i, 0)), _full((SMALL_ROWS, D)), _full((3, D, D)),
                  _pcol(tm, 0), _pcol(tm, 4), _pcol(tm, 5), _pcol(tm, 6),
                  pl.BlockSpec((1, D), lambda i: (0, 0)), pl.BlockSpec((1, D), lambda i: (0, 1)),
                  row, row, row, row],
        out_specs=(row,) * 7 + (pl.BlockSpec((tm, 3 * D), lambda i: (i, 0)), acc2, acc2),
        out_shape=(f32o, b16o, b16o, b16o, b16o, f32o, f32o, SDS((t, 3 * D), BF16), SDS((2, D), F32), SDS((2, D), F32)),
        compiler_params=_cp("arbitrary"),
    )(dxn, xh, rstd, small, w3, p, p, p, p, bm, bm, ya, yb, ca, h)


def mixb_rec(lam, gr, gi, h, xc, ap, rgw):
    t = lam.shape[0]
    tm = _row_tile(t, 256)

    def body(l_ref, gr_ref, gi_ref, h_ref, hh_ref, xc_ref, ap_ref, rgw_ref, dg_ref, dxc_ref, red_ref, ext):
        i = pl.program_id(0)
        first = i == 0

        @pl.when(first)
        def _():
            red_ref[...] = jnp.zeros_like(red_ref)

        ext[0:8, :] = jnp.where(first, 0.0, hh_ref[...])
        ext[8:, :] = h_ref[...]
        hprev = ext[pl.ds(7, tm), :]
        lam = l_ref[...]
        gr = gr_ref[...]
        gi = gi_ref[...]
        xc = xc_ref[...]
        ap = ap_ref[...]
        sp = _softplus(-ap)
        la = (-LRU_C) * gr * sp
        a = jnp.exp(la)
        row = lax.broadcasted_iota(jnp.int32, (tm, D), 0) + i * tm
        start = row == 0
        mult = jnp.where(start, 1.0, jnp.sqrt(_neg_expm1(2.0 * la)))
        dmult = jnp.where(start, 0.0, lam * xc * gi)
        dla = lam * hprev * a - dmult * a * a / mult
        dg1 = (-LRU_C) * sp * dla * gr * (1.0 - gr)
        dg0 = lam * xc * mult * gi * (1.0 - gi)
        dsp = jnp.sum((-LRU_C) * gr * dla, axis=0, keepdims=True)
        red_ref[0:1, :] += jnp.sum(dg0, axis=0, keepdims=True)
        red_ref[1:2, :] += jnp.sum(dg1, axis=0, keepdims=True)
        red_ref[2:3, :] += -dsp * _sigmoid(-ap)
        dg0b = dg0.astype(BF16)
        dg1b = dg1.astype(BF16)
        dg_ref[0] = dg0b
        dg_ref[1] = dg1b
        parts = []
        for hd in range(HEADS):
            sl = slice(hd * HD, (hd + 1) * HD)
            parts.append(_dot(dg0b[:, sl], rgw_ref[0, hd], NT) + _dot(dg1b[:, sl], rgw_ref[1, hd], NT))
        dxc = lam * gi * mult + jnp.concatenate(parts, axis=1)
        dxc_ref[...] = dxc
        red_ref[3:4, :] += jnp.sum(dxc, axis=0, keepdims=True)

    row = pl.BlockSpec((tm, D), lambda i: (i, 0))
    return pl.pallas_call(
        body, name="mixb_rec", grid=(t // tm,),
        in_specs=[row, row, row, row, _prev_halo(tm, 0), row, _full((1, D)), _full((2, HEADS, HD, HD))],
        out_specs=(pl.BlockSpec((2, tm, D), lambda i: (0, i, 0)), row, pl.BlockSpec((8, D), lambda i: (0, 0))),
        out_shape=(SDS((2, t, D), BF16), SDS((t, D), F32), SDS((8, D), F32)),
        scratch_shapes=[pltpu.VMEM((tm + 8, D), F32)],
        compiler_params=_cp("arbitrary"),
    )(lam, gr, gi, h, h, xc, ap, rgw)


def mixb_conv(dca, dxc, dbg, p, small):
    t = dca.shape[0]
    tm = _row_tile(t, 256)
    nt = t // tm

    def body(dca_ref, dcan_ref, dxc_ref, dxcn_ref, dbg_ref, cg_ref, v_ref, xr_ref, cgh_ref, vh_ref, xrh_ref, sm_ref,
             dplo_ref, dxr_ref, red_ref, e_dca, e_dxc, e_cv, e_xr):
        i = pl.program_id(0)
        first = i == 0
        last = i == nt - 1

        @pl.when(first)
        def _():
            red_ref[...] = jnp.zeros_like(red_ref)

        dca = dca_ref[...]
        dxc = dxc_ref[...]
        e_dca[0:tm, :] = dca
        e_dca[tm:, :] = jnp.where(last, 0.0, dcan_ref[...])
        e_dxc[0:tm, :] = dxc
        e_dxc[tm:, :] = jnp.where(last, 0.0, dxcn_ref[...])
        cg = cg_ref[...]
        v = v_ref[...]
        xr = xr_ref[...]
        e_cv[0:8, :] = jnp.where(first, 0.0, cgh_ref[...] * vh_ref[...])
        e_cv[8:, :] = cg * v
        e_xr[0:8, :] = jnp.where(first, 0.0, xrh_ref[...])
        e_xr[8:, :] = xr
        dcv = (sm_ref[R_SC + 2:R_SC + 3, :] * dca + sm_ref[R_SC + 1:R_SC + 2, :] * e_dca[pl.ds(1, tm), :]
               + sm_ref[R_SC:R_SC + 1, :] * e_dca[pl.ds(2, tm), :])
        dplo_ref[:, 0:D] = dbg_ref[...]
        dplo_ref[:, D:2 * D] = (dcv * v).astype(BF16)
        dplo_ref[:, 2 * D:3 * D] = (dcv * cg).astype(BF16)
        dxr = (sm_ref[R_RC + 3:R_RC + 4, :] * dxc + sm_ref[R_RC + 2:R_RC + 3, :] * e_dxc[pl.ds(1, tm), :]
               + sm_ref[R_RC + 1:R_RC + 2, :] * e_dxc[pl.ds(2, tm), :] + sm_ref[R_RC:R_RC + 1, :] * e_dxc[pl.ds(3, tm), :])
        dxr_ref[...] = dxr.astype(BF16)
        for k in range(3):
            red_ref[R_SC + k:R_SC + k + 1, :] += jnp.sum(dca * e_cv[pl.ds(6 + k, tm), :], axis=0, keepdims=True)
        for k in range(4):
            red_ref[R_RC + k:R_RC + k + 1, :] += jnp.sum(dxc * e_xr[pl.ds(5 + k, tm), :], axis=0, keepdims=True)

    row = pl.BlockSpec((tm, D), lambda i: (i, 0))
    ext = pltpu.VMEM((tm + 8, D), F32)
    return pl.pallas_call(
        body, name="mixb_conv", grid=(nt,),
        in_specs=[row, _next_halo(tm, t, 0), row, _next_halo(tm, t, 0), row,
                  _pcol(tm, 1), _pcol(tm, 2), _pcol(tm, 3), _prev_halo(tm, 1), _prev_halo(tm, 2), _prev_halo(tm, 3),
                  _full((SMALL_ROWS, D))],
        out_specs=(pl.BlockSpec((tm, 3 * D), lambda i: (i, 0)), row, pl.BlockSpec((8, D), lambda i: (0, 0))),
        out_shape=(SDS((t, 3 * D), BF16), SDS((t, D), BF16), SDS((8, D), F32)),
        scratch_shapes=[ext, ext, ext, ext],
        compiler_params=_cp("arbitrary"),
    )(dca, dca, dxc, dxc, dbg, p, p, p, p, p, p, small)


def mixb_dx(dz, dplo, dxr, dphi, win):
    t = dz.shape[0]
    tm = _row_tile(t, 512)

    def body(dz_ref, lo_ref, xr_ref, hi_ref, w_ref, dx_ref, acc):
        k = pl.program_id(1)

        @pl.when(k == 0)
        def _():
            acc[...] = ALPHA * dz_ref[...]

        @pl.when(k < 3)
        def _():
            acc[...] += _dot(lo_ref[...], w_ref[...], NT)

        @pl.when(k == 3)
        def _():
            acc[...] += _dot(xr_ref[...], w_ref[...], NT)

        @pl.when(k > 3)
        def _():
            acc[...] += _dot(hi_ref[...], w_ref[...], NT)

        @pl.when(k == 6)
        def _():
            dx_ref[...] = acc[...]

    row = pl.BlockSpec((tm, D), lambda i, k: (i, 0))
    return pl.pallas_call(
        body, name="mixb_dx", grid=(t // tm, 7),
        in_specs=[row, pl.BlockSpec((tm, D), lambda i, k: (i, jnp.minimum(k, 2))), row,
                  pl.BlockSpec((tm, D), lambda i, k: (i, jnp.clip(k - 4, 0, 2))),
                  pl.BlockSpec((D, D), lambda i, k: (0, k))],
        out_specs=row, out_shape=SDS((t, D), F32),
        scratch_shapes=[pltpu.VMEM((tm, D), F32)],
        compiler_params=_cp("parallel", "arbitrary"),
    )(dz, dplo, dxr, dphi, win)


def mixb_dwin(xb, dplo, dxr, dphi):
    t = xb.shape[0]
    tk = _row_tile(t, 512)
    nk = t // tk

    def body(x_ref, lo_ref, xr_ref, hi_ref, o_ref, acc):
        n = pl.program_id(0)
        k = pl.program_id(1)

        @pl.when(k == 0)
        def _():
            acc[...] = jnp.zeros_like(acc)

        @pl.when(n < 3)
        def _():
            acc[...] += _dot(x_ref[...], lo_ref[...], TN)

        @pl.when(n == 3)
        def _():
            acc[...] += _dot(x_ref[...], xr_ref[...], TN)

        @pl.when(n > 3)
        def _():
            acc[...] += _dot(x_ref[...], hi_ref[...], TN)

        @pl.when(k == nk - 1)
        def _():
            o_ref[...] = acc[...].astype(BF16)

    return pl.pallas_call(
        body, name="mixb_dwin", grid=(7, nk),
        in_specs=[pl.BlockSpec((tk, D), lambda n, k: (k, 0)),
                  pl.BlockSpec((tk, D), lambda n, k: (jnp.where(n < 3, k, 0), jnp.minimum(n, 2))),
                  pl.BlockSpec((tk, D), lambda n, k: (jnp.where(n == 3, k, 0), 0)),
                  pl.BlockSpec((tk, D), lambda n, k: (jnp.where(n > 3, k, 0), jnp.clip(n - 4, 0, 2)))],
        out_specs=pl.BlockSpec((D, D), lambda n, k: (0, n)), out_shape=SDS((D, D_IN), BF16),
        scratch_shapes=[pltpu.VMEM((D, D), F32)],
        compiler_params=_cp("parallel", "arbitrary"),
    )(xb, dplo, dxr, dphi)


def mixb_drgw(xcb, dg):
    t = xcb.shape[0]
    tk = _row_tile(t, 512)
    return _mm_tn("mixb_drgw", xcb, pl.BlockSpec((tk, HD), lambda g, h, k: (k, h)),
                  dg, pl.BlockSpec((None, tk, HD), lambda g, h, k: (g, k, h)),
                  SDS((2, HEADS, HD, HD), BF16), pl.BlockSpec((None, None, HD, HD), lambda g, h, k: (g, h, 0, 0)),
                  (2, HEADS, t // tk), (HD, HD))


def loss_head(y, tgt):
    t = y.shape[0]
    tm = _row_tile(t, 512)

    def body(y_ref, t_ref, dy_ref, l_ref):
        i = pl.program_id(0)
        e = y_ref[...] - t_ref[...]
        dy_ref[...] = e * (1.0 / D)

        @pl.when(i == 0)
        def _():
            l_ref[...] = jnp.zeros_like(l_ref)

        l_ref[...] += 0.5 * jnp.sum(jnp.mean(e * e, axis=-1, keepdims=True), axis=0, keepdims=True)

    row = pl.BlockSpec((tm, D), lambda i: (i, 0))
    return pl.pallas_call(
        body, name="loss_head", grid=(t // tm,), in_specs=[row, row],
        out_specs=(row, pl.BlockSpec((8, 128), lambda i: (0, 0))),
        out_shape=(SDS((t, D), F32), SDS((8, 128), F32)),
        compiler_params=_cp("arbitrary"),
    )(y, tgt)


def _adamw(w, g, m, v):
    m = ADAM_B1 * m + (1.0 - ADAM_B1) * g
    v = ADAM_B2 * v + (1.0 - ADAM_B2) * (g * g)
    m_hat = m / (1.0 - ADAM_B1 ** ADAM_STEP)
    v_hat = v / (1.0 - ADAM_B2 ** ADAM_STEP)
    delta = -ADAM_LR * (m_hat / (jnp.sqrt(v_hat) + ADAM_EPS) + ADAM_WD * w)
    return delta, m, v


def adam_big(name, w, m, v, parts, rows, lanes, part_rows, blk_off, tr):
    nr = rows // tr

    def body(w_ref, m_ref, v_ref, p0, p1, p2, p3, g_ref, d_ref, mo_ref, vo_ref):
        l = pl.program_id(0)
        for ll, pr in enumerate((p0, p1, p2, p3)):
            @pl.when(l == ll)
            def _():
                g = pr[0].astype(F32)
                for s in range(1, N_DEV):
                    g = g + pr[s].astype(F32)
                g_ref[...] = g
                d, mn, vn = _adamw(w_ref[...], g, m_ref[...], v_ref[...])
                d_ref[...] = d
                mo_ref[...] = mn
                vo_ref[...] = vn

    blk = pl.BlockSpec((None, tr, lanes), lambda l, r: (l, r, 0))

    def part_spec(ll):
        return pl.BlockSpec((N_DEV, tr, lanes), lambda l, r: (0, jnp.where(l == ll, r, 0) + blk_off, 0))

    out = SDS((DEPTH, rows, lanes), F32)
    return pl.pallas_call(
        body, name=name, grid=(DEPTH, nr),
        in_specs=[blk, blk, blk] + [part_spec(ll) for ll in range(DEPTH)],
        out_specs=(blk,) * 4, out_shape=(out,) * 4,
        compiler_params=_cp("parallel", "parallel"),
    )(w, m, v, *parts)


def sum_small(land):
    def body(l_ref, o_ref):
        g = l_ref[0]
        for s in range(1, N_DEV):
            g = g + l_ref[s]
        o_ref[...] = g

    return pl.pallas_call(body, name="sum_small", out_shape=SDS(land.shape[1:], F32))(land)


def adam_small(name, w, g, m, v):
    def body(w_ref, g_ref, m_ref, v_ref, d_ref, mo_ref, vo_ref):
        d, mn, vn = _adamw(w_ref[...], g_ref[...], m_ref[...], v_ref[...])
        d_ref[...] = d
        mo_ref[...] = mn
        vo_ref[...] = vn

    out = SDS(w.shape, F32)
    return pl.pallas_call(body, name=name, out_shape=(out, out, out))(w, g, m, v)


def _ffn_forward(xf, xb, gw, f, s):
    g, u, a = ffn_up(xb, gw["w1"], f)
    xo, xob, xh, rs = ffn_down_ln(a, gw["w2"], f, xf, gw["small"], s)
    return (xo, xob), dict(xb=xb, g=g, u=u, a=a, xh=xh, rs=rs)


def _ffn_backward(dxn, sv, gw, f, s):
    dz, df, dgu, dln = ffn_bwd_gates(dxn, sv["xh"], sv["rs"], gw["small"], s, gw["w2"], f, sv["g"], sv["u"])
    dx = ffn_bwd_dx(dz, dgu, gw["w1"], f)
    dw1 = ffn_dw1(sv["xb"], dgu, f)
    dw2 = ffn_dw2(sv["a"], df, f)
    return dx, dw1, dw2, dln


def _mixer_forward(xf, xb, gw, rcb, ap, bm):
    p = mix_proj(xb, gw["win"])
    ca, pa, xc, xcb, gi, gr, a, bb = mix_pre(p, gw["small"], rcb, ap, gw["rgw"])
    h = scan_fwd(a, bb)
    pb, ya, yb, m, xo, xob, xh, rs = mix_out(pa, h, p, bm, gw["w3"], xf, gw["small"])
    sv = dict(xb=xb, p=p, ca=ca, pa=pa, xc=xc, xcb=xcb, gi=gi, gr=gr, a=a, h=h, pb=pb, ya=ya, yb=yb, m=m, xh=xh, rs=rs)
    return (xo, xob), sv


def _mixer_backward(dxn, sv, gw, rcb, ap, bm):
    dz, dzb, dya, dyb, dbg, dca, dh, dphi, dln, dbm = mixb_head(
        dxn, sv["xh"], sv["rs"], gw["small"], gw["w3"], sv["p"], bm, sv["ya"], sv["yb"], sv["ca"], sv["h"])
    lam = scan_bwd(sv["a"], dh)
    dg, dxc, red_rec = mixb_rec(lam, sv["gr"], sv["gi"], sv["h"], sv["xc"], ap, gw["rgw"])
    dplo, dxr, red_conv = mixb_conv(dca, dxc, dbg, sv["p"], gw["small"])
    dx = mixb_dx(dz, dplo, dxr, dphi, gw["win"])
    dwin = mixb_dwin(sv["xb"], dplo, dxr, dphi)
    dwo = mm_tn_square("mixb_dwo", sv["m"], dzb)
    dwoc = mm_tn_square("mixb_dwoc", sv["pa"], dya)
    dwor = mm_tn_square("mixb_dwor", sv["pb"], dyb)
    drgw = mixb_drgw(sv["xcb"], dg)
    return dx, dict(dwin=dwin, dwoc=dwoc, dwor=dwor, dwo=dwo, drgw=drgw), dln, dbm, red_rec, red_conv


def kernel(x, w_in, b_merge, sc_w, rc_w, rc_b, rg_w, rg_b, a_param, w_out_conv, w_out_rnn, w_o, ffn_w1, ffn_w2, ln_g, ln_b, loss_target, m_w_in, m_b_merge, m_sc_w, m_rc_w, m_rc_b, m_rg_w, m_rg_b, m_a_param, m_w_out_conv, m_w_out_rnn, m_w_o, m_ffn_w1, m_ffn_w2, m_ln_g, m_ln_b, v_w_in, v_b_merge, v_sc_w, v_rc_w, v_rc_b, v_rg_w, v_rg_b, v_a_param, v_w_out_conv, v_w_out_rnn, v_w_o, v_ffn_w1, v_ffn_w2, v_ln_g, v_ln_b):
    t = x.shape[1]
    me = _me()

    def pack_small(sc, rc, rgb, lng, lnb):
        return jnp.concatenate([sc, rc, rgb, lng, lnb], axis=1)

    def pack_rep(bmv, rcbv, apv):
        return jnp.concatenate([bmv.reshape(DEPTH, 2, D), rcbv[:, None], apv[:, None]], axis=1)

    win_b = w_in.astype(BF16)
    w3_b = jnp.stack([w_out_conv, w_out_rnn, w_o], axis=1).astype(BF16)
    w1_b = ffn_w1.astype(BF16)
    w2_b = ffn_w2.astype(BF16)
    rgw_b = rg_w.astype(BF16)
    small_sh = pack_small(sc_w, rc_w, rg_b, ln_g, ln_b)
    small = jnp.concatenate([small_sh, jnp.zeros((DEPTH, 1, 128), F32)], axis=1)

    gws = []
    for l in range(DEPTH):
        o_win, o_w3, o_w1, o_w2, o_rgw, o_sm = gather_layer(l, win_b, w3_b, w1_b, w2_b, rgw_b, small)
        gws.append(dict(win=o_win, w3=o_w3, w1=o_w1, w2=o_w2, rgw=o_rgw, small=o_sm))

    xf = x.reshape(t, D)
    cur = (xf, xf.astype(BF16))
    saved = []
    for l in range(DEPTH):
        gw = gws[l]
        rcb, ap, bm = rc_b[l][None], a_param[l][None], b_merge[l][None]
        cur, s0 = _ffn_forward(cur[0], cur[1], gw, 0, 0)
        cur, s1 = _mixer_forward(cur[0], cur[1], gw, rcb, ap, bm)
        cur, s2 = _ffn_forward(cur[0], cur[1], gw, 1, 2)
        saved.append((s0, s1, s2))

    dy, loss_tile = loss_head(cur[0], loss_target.reshape(t, D))
    loss = lax.psum(loss_tile[0, 0], ("x", "y", "c"))

    lands = [None] * DEPTH
    gsmall = [None] * DEPTH
    for l in reversed(range(DEPTH)):
        gw = gws[l]
        rcb, ap, bm = rc_b[l][None], a_param[l][None], b_merge[l][None]
        s0, s1, s2 = saved[l]
        dy, dw1b, dw2b, dln2 = _ffn_backward(dy, s2, gw, 1, 2)
        dy, dmix, dln1, dbm, red_rec, red_conv = _mixer_backward(dy, s1, gw, rcb, ap, bm)
        dy, dw1a, dw2a, dln0 = _ffn_backward(dy, s0, gw, 0, 0)
        lands[l] = scatter_layer(l, dmix["dwin"], dmix["dwoc"], dmix["dwor"], dmix["dwo"], dw1a, dw1b, dw2a, dw2b,
                                 dmix["drgw"])
        gsmall[l] = jnp.concatenate([
            red_conv[0:7], red_rec[0:2], dln0[0:1], dln1[0:1], dln2[0:1], dln0[1:2], dln1[1:2], dln2[1:2],
            dbm, red_rec[3:4], red_rec[2:3], jnp.zeros((GRAD_ROWS - 19, D), F32)], axis=0)
    grad_x = dy.reshape(1, t, D)

    gfull = sum_small(gather_small_grads(jnp.stack(gsmall, axis=0)))
    g_sh = lax.dynamic_slice(gfull, (0, 0, me * 128), (DEPTH, 15, 128))
    g_rep = gfull[:, G_BM:G_BM + 4, :]
    d_sh, m_sh, v_sh = adam_small("adam_small_sharded", small_sh, g_sh,
                                  pack_small(m_sc_w, m_rc_w, m_rg_b, m_ln_g, m_ln_b),
                                  pack_small(v_sc_w, v_rc_w, v_rg_b, v_ln_g, v_ln_b))
    d_rep, m_rep, v_rep = adam_small("adam_small_replicated", pack_rep(b_merge, rc_b, a_param), g_rep,
                                     pack_rep(m_b_merge, m_rc_b, m_a_param), pack_rep(v_b_merge, v_rc_b, v_a_param))

    def unpack_sh(a):
        return dict(sc_w=a[:, 0:3], rc_w=a[:, 3:7], rg_b=a[:, 7:9], ln_g=a[:, 9:12], ln_b=a[:, 12:15])

    def unpack_rep(a):
        return dict(b_merge=a[:, 0:2].reshape(DEPTH, 2 * D), rc_b=a[:, 2], a_param=a[:, 3])

    l_win = [lands[l][0] for l in range(DEPTH)]
    l_w3 = [lands[l][1].reshape(N_DEV, 3 * 128, D) for l in range(DEPTH)]
    l_w1 = [lands[l][2].reshape(N_DEV, 2 * D, FS) for l in range(DEPTH)]
    l_w2 = [lands[l][3].reshape(N_DEV, 2 * W2S, D) for l in range(DEPTH)]
    l_rgw = [lands[l][4].reshape(N_DEV, 2 * HEADS * 32, HD) for l in range(DEPTH)]

    def big(name, w, m, v, parts, rows, lanes, blk_off, tr):
        shp = w.shape
        r3 = lambda a: a.reshape(DEPTH, rows, lanes)
        outs = adam_big(name, r3(w), r3(m), r3(v), parts, rows, lanes, parts[0].shape[1], blk_off, tr)
        return [o.reshape(shp) for o in outs]

    res = {}
    res["w_in"] = big("adam_w_in", w_in, m_w_in, v_w_in, l_win, D, WIN_S, 0, 128)
    res["w_out_conv"] = big("adam_w_out_conv", w_out_conv, m_w_out_conv, v_w_out_conv, l_w3, 128, D, 0, 128)
    res["w_out_rnn"] = big("adam_w_out_rnn", w_out_rnn, m_w_out_rnn, v_w_out_rnn, l_w3, 128, D, 1, 128)
    res["w_o"] = big("adam_w_o", w_o, m_w_o, v_w_o, l_w3, 128, D, 2, 128)
    res["ffn_w1"] = big("adam_ffn_w1", ffn_w1, m_ffn_w1, v_ffn_w1, l_w1, 2 * D, FS, 0, 128)
    res["ffn_w2"] = big("adam_ffn_w2", ffn_w2, m_ffn_w2, v_ffn_w2, l_w2, 2 * W2S, D, 0, 2 * W2S // 8)
    res["rg_w"] = big("adam_rg_w", rg_w, m_rg_w, v_rg_w, l_rgw, 2 * HEADS * 32, HD, 0, 128)

    gsh, grep = unpack_sh(g_sh), unpack_rep(g_rep)
    dsh, drep = unpack_sh(d_sh), unpack_rep(d_rep)
    msh, mrep = unpack_sh(m_sh), unpack_rep(m_rep)
    vsh, vrep = unpack_sh(v_sh), unpack_rep(v_rep)
    for n in ("sc_w", "rc_w", "rg_b", "ln_g", "ln_b"):
        res[n] = [gsh[n], dsh[n], msh[n], vsh[n]]
    for n in ("b_merge", "rc_b", "a_param"):
        res[n] = [grep[n], drep[n], mrep[n], vrep[n]]

    names = ["w_in", "b_merge", "sc_w", "rc_w", "rc_b", "rg_w", "rg_b", "a_param", "w_out_conv", "w_out_rnn", "w_o",
             "ffn_w1", "ffn_w2", "ln_g", "ln_b"]
    out = [loss, grad_x]
    for k in range(4):
        out += [res[n][k] for n in names]
    return tuple(out)
```

```python
import functools

import jax
import jax.numpy as jnp
from jax import lax
from jax.experimental import pallas as pl
from jax.experimental.pallas import tpu as pltpu

F32 = jnp.float32
BF16 = jnp.bfloat16
SDS = jax.ShapeDtypeStruct

N_DEV = 8
DEPTH = 4
D = 1024
D_FF = 2816
FS = D_FF // 4
W2S = D_FF // 8
D_IN = 7 * D
WIN_S = D_IN // 8
HEADS = 4
HD = D // HEADS
LRU_C = 8.0
ALPHA = (2.0 * DEPTH) ** 0.25
LN_EPS = 1e-5
ADAM_LR, ADAM_B1, ADAM_B2, ADAM_EPS, ADAM_WD, ADAM_STEP = 0.001, 0.9, 0.999, 1e-08, 0.01, 10

R_SC, R_RC, R_RGB, R_LNG, R_LNB = 0, 3, 7, 9, 12
SMALL_ROWS = 16
G_BM, G_RCB, G_AP = 15, 17, 18
GRAD_ROWS = 24

NN = ((1,), (0,))
NT = ((1,), (1,))
TN = ((0,), (0,))
MESH = pl.DeviceIdType.MESH
ANY = pl.BlockSpec(memory_space=pl.ANY)
VMEM_LIMIT = 52 * 1024 * 1024


def _dot(a, b, dims):
    return lax.dot_general(a, b, (dims, ((), ())), preferred_element_type=F32)


def _cp(*sem):
    return pltpu.CompilerParams(dimension_semantics=sem, vmem_limit_bytes=VMEM_LIMIT)


def _sigmoid(x):
    return 1.0 / (1.0 + jnp.exp(-x))


def _log1p(e):
    u = 1.0 + e
    return jnp.where(u == 1.0, e, jnp.log(u) * e / jnp.where(u == 1.0, 1.0, u - 1.0))


def _softplus(x):
    return jnp.maximum(x, 0.0) + _log1p(jnp.exp(-jnp.abs(x)))


def _neg_expm1(x):
    u = jnp.exp(x)
    um1 = u - 1.0
    safe = jnp.logical_and(u != 1.0, um1 != -1.0)
    r = um1 * x / jnp.where(safe, jnp.log(jnp.where(safe, u, 0.5)), 1.0)
    return -jnp.where(u == 1.0, x, jnp.where(um1 == -1.0, -1.0, r))


def _gelu(y):
    c = 0.7978845608028654
    t = jnp.tanh(c * (y + 0.044715 * y * y * y))
    return 0.5 * y * (1.0 + t), t


def _gelu_grad(y, t):
    c = 0.7978845608028654
    return 0.5 * (1.0 + t) + 0.5 * y * (1.0 - t * t) * c * (1.0 + 3.0 * 0.044715 * y * y)


def _ln_fwd(z, g, b):
    mu = jnp.mean(z, axis=-1, keepdims=True)
    zc = z - mu
    var = jnp.mean(zc * zc, axis=-1, keepdims=True)
    rstd = lax.rsqrt(var + LN_EPS)
    xh = zc * rstd
    return xh * g + b, xh, rstd


def _ln_bwd(dy, xh, rstd, g):
    dxh = dy * g
    m1 = jnp.mean(dxh, axis=-1, keepdims=True)
    m2 = jnp.mean(dxh * xh, axis=-1, keepdims=True)
    return rstd * (dxh - m1 - xh * m2)


def _row_tile(t, want):
    return min(want, t)


def _me():
    return 4 * lax.axis_index("x") + 2 * lax.axis_index("y") + lax.axis_index("c")


def _coords(p):
    return (p // 4, (p // 2) % 2, p % 2)


def _all_to_all_copies(srcs_of, dsts_of, waits, send_sems, recv_sems, loc_sems):
    me = _me()
    n = len(waits)
    own_src, own_dst = srcs_of(me), dsts_of(me)
    local = [pltpu.make_async_copy(own_src[k], own_dst[k], loc_sems.at[k]) for k in range(n)]
    for cp in local:
        cp.start()
    for d in range(1, N_DEV):
        p = (me + d) % N_DEV
        src, dst = srcs_of(p), dsts_of(me)
        for k in range(n):
            pltpu.make_async_remote_copy(
                src_ref=src[k], dst_ref=dst[k], send_sem=send_sems.at[waits[k][1]],
                recv_sem=recv_sems.at[waits[k][1]], device_id=_coords(p), device_id_type=MESH).start()
    done = set()
    for k in range(n):
        ref, s = waits[k]
        if s in done:
            continue
        done.add(s)
        pltpu.make_async_remote_copy(
            src_ref=ref, dst_ref=ref, send_sem=send_sems.at[s], recv_sem=recv_sems.at[s],
            device_id=_coords(me), device_id_type=MESH).wait()
    for cp in local:
        cp.wait()


HBM = pl.BlockSpec(memory_space=pltpu.HBM)
SEM = pl.BlockSpec(memory_space=pltpu.SEMAPHORE)
EFFECT = pltpu.SideEffectType.DATAFLOW_SIDE_EFFECTING


def _in_hbm(a):
    return pltpu.with_memory_space_constraint(a, pltpu.HBM)


class Exchange:
    def __init__(self, name, src_of, dst_of, sem_of, span_of):
        self.name, self.src_of, self.dst_of, self.sem_of, self.span_of = name, src_of, dst_of, sem_of, span_of
        self.nsem = max(sem_of) + 1

    def place_own(self, srcs, land_shapes):
        n, m = len(srcs), len(land_shapes)
        ncopy = len(self.sem_of)

        def body(*refs):
            src_refs, land_refs, sems = refs[:n], refs[n:n + m], refs[n + m]
            me = _me()
            s, d = self.src_of(src_refs, me), self.dst_of(land_refs, me)
            cps = [pltpu.make_async_copy(s[k], d[k], sems.at[k]) for k in range(ncopy)]
            for cp in cps:
                cp.start()
            for cp in cps:
                cp.wait()

        return pl.pallas_call(
            body, name=self.name + "_own", out_shape=tuple(land_shapes), in_specs=[ANY] * n, out_specs=(ANY,) * m,
            scratch_shapes=[pltpu.SemaphoreType.DMA((ncopy,))],
            compiler_params=pltpu.CompilerParams(has_side_effects=True),
        )(*srcs)

    def start(self, srcs, lands, thru):
        n, m = len(srcs), len(lands)
        ops = list(srcs) + list(lands) + [thru]

        def body(*refs):
            src_refs, land_refs = refs[:n], refs[n:n + m]
            send_sems, recv_sems = refs[n + m + 1], refs[n + m + 2]
            me = _me()
            for dd in range(1, N_DEV):
                p = (me + dd) % N_DEV
                s, d = self.src_of(src_refs, p), self.dst_of(land_refs, me)
                for k in range(len(self.sem_of)):
                    pltpu.make_async_remote_copy(
                        src_ref=s[k], dst_ref=d[k], send_sem=send_sems.at[self.sem_of[k]],
                        recv_sem=recv_sems.at[self.sem_of[k]], device_id=_coords(p), device_id_type=MESH).start()

        sem = pltpu.SemaphoreType.DMA((self.nsem,))
        res = pl.pallas_call(
            body, name=self.name + "_start",
            out_shape=(sem, sem) + tuple(pltpu.HBM(a.shape, a.dtype) for a in ops),
            in_specs=[HBM] * len(ops), out_specs=(SEM, SEM) + (HBM,) * len(ops),
            input_output_aliases={i: 2 + i for i in range(len(ops))},
            compiler_params=pltpu.CompilerParams(has_side_effects=EFFECT),
        )(*[_in_hbm(a) for a in ops])
        return res[0], res[1], res[2:2 + n], res[2 + n:2 + n + m], res[2 + n + m]

    def wait(self, send_sems, recv_sems, srcs, lands, thru):
        n, m = len(srcs), len(lands)
        ops = list(srcs) + list(lands) + [thru]

        def body(*refs):
            land_refs = refs[n:n + m]
            ssem, rsem = refs[n + m + 1], refs[n + m + 2]
            me = _me()
            spans = self.span_of(land_refs)
            for s in range(self.nsem):
                cp = pltpu.make_async_remote_copy(
                    src_ref=spans[s], dst_ref=spans[s], send_sem=ssem.at[s], recv_sem=rsem.at[s],
                    device_id=_coords(me), device_id_type=MESH)
                cp.wait_send()
                cp.wait_recv()

        res = pl.pallas_call(
            body, name=self.name + "_wait",
            out_shape=tuple(pltpu.HBM(a.shape, a.dtype) for a in ops),
            in_specs=[HBM] * len(ops) + [SEM, SEM], out_specs=(HBM,) * len(ops),
            input_output_aliases={i: i for i in range(len(ops))},
            compiler_params=pltpu.CompilerParams(has_side_effects=EFFECT),
        )(*ops, send_sems, recv_sems)
        return res[n:n + m], res[n + m]


def weight_gather(l):
    def src_of(refs, p):
        return list(refs)

    def dst_of(lands, p):
        o_win, o_w3, o_w1, o_w2, o_rgw, o_sm = lands
        return [o_win.at[:, pl.ds(pl.multiple_of(p * WIN_S, 128), WIN_S)],
                o_w3.at[:, pl.ds(pl.multiple_of(p * 128, 128), 128), :],
                o_w1.at[:, p],
                o_w2.at[:, pl.ds(pl.multiple_of(p * W2S, 16), W2S), :],
                o_rgw.at[:, :, pl.ds(pl.multiple_of(p * 32, 32), 32), :],
                o_sm.at[:, pl.ds(pl.multiple_of(p * 128, 128), 128)]]

    def span_of(lands):
        o_win, o_w3, o_w1, o_w2, o_rgw, o_sm = lands
        return [o_win.at[:, pl.ds(0, 7 * WIN_S)], o_w3.at[:, pl.ds(0, 7 * 128), :], o_w1.at[:, pl.ds(0, 7)],
                o_w2.at[:, pl.ds(0, 7 * W2S), :], o_rgw.at[:, :, pl.ds(0, 7 * 32), :], o_sm.at[:, pl.ds(0, 7 * 128)]]

    return Exchange(f"gather{l}", src_of, dst_of, [0, 1, 2, 3, 4, 5], span_of)


GATHER_SHAPES = (SDS((D, D_IN), BF16), SDS((3, D, D), BF16), SDS((2, N_DEV, D, FS), BF16),
                 SDS((2, D_FF, D), BF16), SDS((2, HEADS, HD, HD), BF16), SDS((SMALL_ROWS, D), F32))


def grad_scatter(l):
    def src_of(refs, p):
        dwin, dwoc, dwor, dwo, dw1a, dw1b, dw2a, dw2b, drgw = refs
        rows3 = pl.ds(pl.multiple_of(p * 128, 128), 128)
        rows2 = pl.ds(pl.multiple_of(p * W2S, 16), W2S)
        return [dwin.at[:, pl.ds(pl.multiple_of(p * WIN_S, 128), WIN_S)],
                dwoc.at[rows3, :], dwor.at[rows3, :], dwo.at[rows3, :],
                dw1a.at[p], dw1b.at[p], dw2a.at[rows2, :], dw2b.at[rows2, :],
                drgw.at[:, :, pl.ds(pl.multiple_of(p * 32, 32), 32), :]]

    def dst_of(lands, p):
        l_win, l_w3, l_w1, l_w2, l_rgw = lands
        return [l_win.at[p], l_w3.at[p, 0], l_w3.at[p, 1], l_w3.at[p, 2], l_w1.at[p, 0], l_w1.at[p, 1],
                l_w2.at[p, 0], l_w2.at[p, 1], l_rgw.at[p]]

    def span_of(lands):
        return [a.at[pl.ds(0, 7)] for a in lands]

    return Exchange(f"scatter{l}", src_of, dst_of, [0, 1, 1, 1, 2, 2, 3, 3, 4], span_of)


SCATTER_SHAPES = (SDS((N_DEV, D, WIN_S), BF16), SDS((N_DEV, 3, 128, D), BF16), SDS((N_DEV, 2, D, FS), BF16),
                  SDS((N_DEV, 2, W2S, D), BF16), SDS((N_DEV, 2, HEADS, 32, HD), BF16))


def gather_small_grads(gsmall):
    shp = gsmall.shape

    def body(g, land, send_sems, recv_sems, loc_sems):
        _all_to_all_copies(lambda p: [g], lambda p: [land.at[p]], [(land.at[pl.ds(0, 7)], 0)],
                           send_sems, recv_sems, loc_sems)

    return pl.pallas_call(
        body, name="gather_small_grads", out_shape=SDS((N_DEV,) + shp, F32),
        in_specs=[ANY], out_specs=ANY,
        scratch_shapes=[pltpu.SemaphoreType.DMA((1,)), pltpu.SemaphoreType.DMA((1,)), pltpu.SemaphoreType.DMA((1,))],
        compiler_params=pltpu.CompilerParams(has_side_effects=True),
    )(gsmall)


def ffn_up(xb, w1, f):
    t = xb.shape[0]
    tm = _row_tile(t, 512)

    def body(x_ref, wg_ref, wu_ref, g_ref, u_ref, a_ref):
        x = x_ref[...]
        g = _dot(x, wg_ref[...], NN)
        u = _dot(x, wu_ref[...], NN)
        g_ref[...] = g.astype(BF16)
        u_ref[...] = u.astype(BF16)
        a_ref[...] = (g * _sigmoid(g) * u).astype(BF16)

    out = pl.BlockSpec((None, tm, FS), lambda j, i: (j, i, 0))
    return pl.pallas_call(
        body, name=f"ffn_up{f}", grid=(4, t // tm),
        in_specs=[pl.BlockSpec((tm, D), lambda j, i: (i, 0)),
                  pl.BlockSpec((None, None, D, FS), lambda j, i: (f, j, 0, 0)),
                  pl.BlockSpec((None, None, D, FS), lambda j, i: (f, j + 4, 0, 0))],
        out_specs=(out, out, out), out_shape=(SDS((4, t, FS), BF16),) * 3,
        compiler_params=_cp("parallel", "parallel"),
    )(xb, w1, w1)


def ffn_down_ln(a, w2, f, xf, small, s):
    t = xf.shape[0]
    tm = _row_tile(t, 256)

    def body(a_ref, w_ref, x_ref, sm_ref, xo_ref, xb_ref, xh_ref, rs_ref):
        acc = _dot(a_ref[0], w_ref[0:FS, :], NN)
        for j in range(1, 4):
            acc = acc + _dot(a_ref[j], w_ref[j * FS:(j + 1) * FS, :], NN)
        z = ALPHA * x_ref[...] + 0.5 * acc
        y, xh, rstd = _ln_fwd(z, sm_ref[R_LNG + s:R_LNG + s + 1, :], sm_ref[R_LNB + s:R_LNB + s + 1, :])
        xo_ref[...] = y
        xb_ref[...] = y.astype(BF16)
        xh_ref[...] = xh
        rs_ref[...] = rstd

    row = pl.BlockSpec((tm, D), lambda i: (i, 0))
    return pl.pallas_call(
        body, name=f"ffn_down_ln{f}", grid=(t // tm,),
        in_specs=[pl.BlockSpec((4, tm, FS), lambda i: (0, i, 0)),
                  pl.BlockSpec((None, D_FF, D), lambda i: (f, 0, 0)),
                  row, pl.BlockSpec((SMALL_ROWS, D), lambda i: (0, 0))],
        out_specs=(row, row, row, pl.BlockSpec((tm, 1), lambda i: (i, 0))),
        out_shape=(SDS((t, D), F32), SDS((t, D), BF16), SDS((t, D), F32), SDS((t, 1), F32)),
        compiler_params=_cp("parallel"),
    )(a, w2, xf, small)


def ffn_bwd_gates(dxn, xh, rstd, small, s, w2, f, g, u):
    t = dxn.shape[0]
    tm = _row_tile(t, 256)

    def body(dy_ref, xh_ref, rs_ref, sm_ref, w_ref, g_ref, u_ref, dz_ref, df_ref, dgu_ref, dln_ref):
        i = pl.program_id(0)
        dy = dy_ref[...]
        xhat = xh_ref[...]
        dz = _ln_bwd(dy, xhat, rs_ref[...], sm_ref[R_LNG + s:R_LNG + s + 1, :])

        @pl.when(i == 0)
        def _():
            dln_ref[...] = jnp.zeros_like(dln_ref)

        dln_ref[0:1, :] += jnp.sum(dy * xhat, axis=0, keepdims=True)
        dln_ref[1:2, :] += jnp.sum(dy, axis=0, keepdims=True)
        dz_ref[...] = dz
        df = (0.5 * dz).astype(BF16)
        df_ref[...] = df
        for j in range(4):
            da = _dot(df, w_ref[j * FS:(j + 1) * FS, :], NT)
            gg = g_ref[j].astype(F32)
            uu = u_ref[j].astype(F32)
            sg = _sigmoid(gg)
            dgu_ref[j] = (da * uu * (sg * (1.0 + gg * (1.0 - sg)))).astype(BF16)
            dgu_ref[j + 4] = (da * (gg * sg)).astype(BF16)

    row = pl.BlockSpec((tm, D), lambda i: (i, 0))
    gu = pl.BlockSpec((4, tm, FS), lambda i: (0, i, 0))
    return pl.pallas_call(
        body, name=f"ffn_bwd_gates{f}", grid=(t // tm,),
        in_specs=[row, row, pl.BlockSpec((tm, 1), lambda i: (i, 0)), pl.BlockSpec((SMALL_ROWS, D), lambda i: (0, 0)),
                  pl.BlockSpec((None, D_FF, D), lambda i: (f, 0, 0)), gu, gu],
        out_specs=(row, row, pl.BlockSpec((8, tm, FS), lambda i: (0, i, 0)), pl.BlockSpec((2, D), lambda i: (0, 0))),
        out_shape=(SDS((t, D), F32), SDS((t, D), BF16), SDS((8, t, FS), BF16), SDS((2, D), F32)),
        compiler_params=_cp("arbitrary"),
    )(dxn, xh, rstd, small, w2, g, u)


def ffn_bwd_dx(dz, dgu, w1, f):
    t = dz.shape[0]
    tm = _row_tile(t, 1024)

    def body(dz_ref, d_ref, w_ref, dx_ref, acc):
        k = pl.program_id(1)

        @pl.when(k == 0)
        def _():
            acc[...] = ALPHA * dz_ref[...]

        acc[...] += _dot(d_ref[...], w_ref[...], NT)

        @pl.when(k == 7)
        def _():
            dx_ref[...] = acc[...]

    row = pl.BlockSpec((tm, D), lambda i, k: (i, 0))
    return pl.pallas_call(
        body, name=f"ffn_bwd_dx{f}", grid=(t // tm, 8),
        in_specs=[row, pl.BlockSpec((None, tm, FS), lambda i, k: (k, i, 0)),
                  pl.BlockSpec((None, None, D, FS), lambda i, k: (f, k, 0, 0))],
        out_specs=row, out_shape=SDS((t, D), F32),
        scratch_shapes=[pltpu.VMEM((tm, D), F32)],
        compiler_params=_cp("parallel", "arbitrary"),
    )(dz, dgu, w1)


def _mm_tn(name, a, a_spec, b, b_spec, out_sds, out_spec, grid, acc_shape):
    nk = grid[-1]

    def body(a_ref, b_ref, o_ref, acc):
        k = pl.program_id(len(grid) - 1)

        @pl.when(k == 0)
        def _():
            acc[...] = jnp.zeros_like(acc)

        acc[...] += _dot(a_ref[...], b_ref[...], TN)

        @pl.when(k == nk - 1)
        def _():
            o_ref[...] = acc[...].astype(o_ref.dtype)

    return pl.pallas_call(
        body, name=name, grid=grid, in_specs=[a_spec, b_spec], out_specs=out_spec, out_shape=out_sds,
        scratch_shapes=[pltpu.VMEM(acc_shape, F32)],
        compiler_params=_cp(*(["parallel"] * (len(grid) - 1) + ["arbitrary"])),
    )(a, b)


def ffn_dw1(xb, dgu, f):
    t = xb.shape[0]
    tk = _row_tile(t, 512)
    return _mm_tn(f"ffn_dw1_{f}", xb, pl.BlockSpec((tk, D), lambda j, k: (k, 0)),
                  dgu, pl.BlockSpec((None, tk, FS), lambda j, k: (j, k, 0)),
                  SDS((8, D, FS), BF16), pl.BlockSpec((None, D, FS), lambda j, k: (j, 0, 0)),
                  (8, t // tk), (D, FS))


def ffn_dw2(a, df, f):
    t = df.shape[0]
    tk = _row_tile(t, 512)
    return _mm_tn(f"ffn_dw2_{f}", a, pl.BlockSpec((None, tk, FS), lambda j, k: (j, k, 0)),
                  df, pl.BlockSpec((tk, D), lambda j, k: (k, 0)),
                  SDS((D_FF, D), BF16), pl.BlockSpec((FS, D), lambda j, k: (j, 0)),
                  (4, t // tk), (FS, D))


def mm_tn_square(name, a, b):
    t = a.shape[0]
    tk = _row_tile(t, 512)
    return _mm_tn(name, a, pl.BlockSpec((tk, 512), lambda i, k: (k, i)),
                  b, pl.BlockSpec((tk, D), lambda i, k: (k, 0)),
                  SDS((D, D), BF16), pl.BlockSpec((512, D), lambda i, k: (i, 0)),
                  (2, t // tk), (512, D))


def mix_proj(xb, win):
    t = xb.shape[0]
    tm = _row_tile(t, 512)

    def body(x_ref, w_ref, o_ref):
        o_ref[...] = _dot(x_ref[...], w_ref[...], NN)

    return pl.pallas_call(
        body, name="mix_proj", grid=(7, t // tm),
        in_specs=[pl.BlockSpec((tm, D), lambda n, i: (i, 0)), pl.BlockSpec((D, D), lambda n, i: (0, n))],
        out_specs=pl.BlockSpec((tm, D), lambda n, i: (i, n)), out_shape=SDS((t, D_IN), F32),
        compiler_params=_cp("parallel", "parallel"),
    )(xb, win)


def _pcol(tm, k):
    return pl.BlockSpec((tm, D), lambda i: (i, k))


def _prev_halo(tm, k):
    return pl.BlockSpec((8, D), lambda i: (jnp.maximum(i * (tm // 8) - 1, 0), k))


def _next_halo(tm, t, k):
    return pl.BlockSpec((8, D), lambda i: (jnp.minimum((i + 1) * (tm // 8), t // 8 - 1), k))


def _full(shape):
    nd = len(shape)
    return pl.BlockSpec(shape, lambda i: (0,) * nd)


def mix_pre(p, small, rcb, ap, rgw):
    t = p.shape[0]
    tm = _row_tile(t, 256)

    def body(bg_ref, cg_ref, v_ref, xr_ref, cgh_ref, vh_ref, xrh_ref, sm_ref, rcb_ref, ap_ref, rgw_ref,
             ca_ref, pa_ref, xc_ref, xcb_ref, gi_ref, gr_ref, a_ref, bb_ref, ext1, ext2):
        i = pl.program_id(0)
        first = i == 0
        cv = cg_ref[...] * v_ref[...]
        ext1[0:8, :] = jnp.where(first, 0.0, cgh_ref[...] * vh_ref[...])
        ext1[8:, :] = cv
        xr = xr_ref[...]
        ext2[0:8, :] = jnp.where(first, 0.0, xrh_ref[...])
        ext2[8:, :] = xr
        ca = (sm_ref[R_SC:R_SC + 1, :] * ext1[pl.ds(6, tm), :] + sm_ref[R_SC + 1:R_SC + 2, :] * ext1[pl.ds(7, tm), :]
              + sm_ref[R_SC + 2:R_SC + 3, :] * cv)
        ca_ref[...] = ca
        pa_ref[...] = (bg_ref[...] * ca).astype(BF16)
        xc = (sm_ref[R_RC:R_RC + 1, :] * ext2[pl.ds(5, tm), :] + sm_ref[R_RC + 1:R_RC + 2, :] * ext2[pl.ds(6, tm), :]
              + sm_ref[R_RC + 2:R_RC + 3, :] * ext2[pl.ds(7, tm), :] + sm_ref[R_RC + 3:R_RC + 4, :] * xr
              + rcb_ref[...])
        xc_ref[...] = xc
        xcb = xc.astype(BF16)
        xcb_ref[...] = xcb
        g0, g1 = [], []
        for h in range(HEADS):
            xh = xcb[:, h * HD:(h + 1) * HD]
            g0.append(_dot(xh, rgw_ref[0, h], NN))
            g1.append(_dot(xh, rgw_ref[1, h], NN))
        gi = _sigmoid(jnp.concatenate(g0, axis=1) + sm_ref[R_RGB:R_RGB + 1, :])
        gr = _sigmoid(jnp.concatenate(g1, axis=1) + sm_ref[R_RGB + 1:R_RGB + 2, :])
        gi_ref[...] = gi
        gr_ref[...] = gr
        la = (-LRU_C) * gr * _softplus(-ap_ref[...])
        a_ref[...] = jnp.exp(la)
        row = lax.broadcasted_iota(jnp.int32, (tm, D), 0) + i * tm
        mult = jnp.where(row == 0, 1.0, jnp.sqrt(_neg_expm1(2.0 * la)))
        bb_ref[...] = xc * gi * mult

    row = pl.BlockSpec((tm, D), lambda i: (i, 0))
    f32o, b16o = SDS((t, D), F32), SDS((t, D), BF16)
    return pl.pallas_call(
        body, name="mix_pre", grid=(t // tm,),
        in_specs=[_pcol(tm, 0), _pcol(tm, 1), _pcol(tm, 2), _pcol(tm, 3),
                  _prev_halo(tm, 1), _prev_halo(tm, 2), _prev_halo(tm, 3),
                  _full((SMALL_ROWS, D)), _full((1, D)), _full((1, D)), _full((2, HEADS, HD, HD))],
        out_specs=(row,) * 8, out_shape=(f32o, b16o, f32o, b16o, f32o, f32o, f32o, f32o),
        scratch_shapes=[pltpu.VMEM((tm + 8, D), F32), pltpu.VMEM((tm + 8, D), F32)],
        compiler_params=_cp("parallel"),
    )(p, p, p, p, p, p, p, small, rcb, ap, rgw)


SCAN_LANES = 512


def scan_fwd(a, b):
    t = a.shape[0]
    cb = SCAN_LANES

    def body(a_ref, b_ref, h_ref):
        row8 = lax.broadcasted_iota(jnp.int32, (8, cb), 0)

        def step(g, carry):
            r = pl.multiple_of(g * 8, 8)
            aa = a_ref[pl.ds(r, 8), :]
            bb = b_ref[pl.ds(r, 8), :]
            for s in (1, 2, 4):
                a_sh = jnp.where(row8 >= s, pltpu.roll(aa, s, 0), 1.0)
                b_sh = jnp.where(row8 >= s, pltpu.roll(bb, s, 0), 0.0)
                bb = aa * b_sh + bb
                aa = aa * a_sh
            h = aa * carry + bb
            h_ref[pl.ds(r, 8), :] = h
            return jnp.broadcast_to(h[7:8, :], (8, cb))

        lax.fori_loop(0, t // 8, step, jnp.zeros((8, cb), F32))

    blk = pl.BlockSpec((t, cb), lambda c: (0, c))
    return pl.pallas_call(
        body, name="scan_fwd", grid=(D // cb,), in_specs=[blk, blk], out_specs=blk, out_shape=SDS((t, D), F32),
        compiler_params=_cp("parallel"),
    )(a, b)


def scan_bwd(a, dh):
    t = a.shape[0]
    cb = SCAN_LANES
    ng = t // 8

    def body(a_ref, d_ref, l_ref):
        row8 = lax.broadcasted_iota(jnp.int32, (8, cb), 0)

        def step(g, carry):
            lam_next, a_next = carry
            r = pl.multiple_of((ng - 1 - g) * 8, 8)
            a_grp = a_ref[pl.ds(r, 8), :]
            aa = jnp.where(row8 < 7, pltpu.roll(a_grp, 7, 0), a_next)
            bb = d_ref[pl.ds(r, 8), :]
            for s in (1, 2, 4):
                a_sh = jnp.where(row8 < 8 - s, pltpu.roll(aa, 8 - s, 0), 1.0)
                b_sh = jnp.where(row8 < 8 - s, pltpu.roll(bb, 8 - s, 0), 0.0)
                bb = aa * b_sh + bb
                aa = aa * a_sh
            lam = aa * lam_next + bb
            l_ref[pl.ds(r, 8), :] = lam
            return (jnp.broadcast_to(lam[0:1, :], (8, cb)), jnp.broadcast_to(a_grp[0:1, :], (8, cb)))

        z = jnp.zeros((8, cb), F32)
        lax.fori_loop(0, ng, step, (z, z))

    blk = pl.BlockSpec((t, cb), lambda c: (0, c))
    return pl.pallas_call(
        body, name="scan_bwd", grid=(D // cb,), in_specs=[blk, blk], out_specs=blk, out_shape=SDS((t, D), F32),
        compiler_params=_cp("parallel"),
    )(a, dh)


def mix_out(pa, h, p, bm, w3, xf, small):
    t = xf.shape[0]
    tm = _row_tile(t, 256)

    def body(pa_ref, h_ref, yr_ref, gla_ref, glb_ref, bma_ref, bmb_ref, w_ref, x_ref, sm_ref,
             pb_ref, ya_ref, yb_ref, m_ref, xo_ref, xb_ref, xh_ref, rs_ref):
        ge, _ = _gelu(yr_ref[...])
        pb = (h_ref[...] * ge).astype(BF16)
        pb_ref[...] = pb
        ya = _dot(pa_ref[...], w_ref[0], NN)
        yb = _dot(pb, w_ref[1], NN)
        ya_ref[...] = ya
        yb_ref[...] = yb
        ga = _sigmoid(gla_ref[...] + bma_ref[...])
        gb = _sigmoid(glb_ref[...] + bmb_ref[...])
        m = (ga * ya + gb * yb).astype(BF16)
        m_ref[...] = m
        z = ALPHA * x_ref[...] + _dot(m, w_ref[2], NN)
        y, xh, rstd = _ln_fwd(z, sm_ref[R_LNG + 1:R_LNG + 2, :], sm_ref[R_LNB + 1:R_LNB + 2, :])
        xo_ref[...] = y
        xb_ref[...] = y.astype(BF16)
        xh_ref[...] = xh
        rs_ref[...] = rstd

    row = pl.BlockSpec((tm, D), lambda i: (i, 0))
    f32o, b16o = SDS((t, D), F32), SDS((t, D), BF16)
    return pl.pallas_call(
        body, name="mix_out", grid=(t // tm,),
        in_specs=[row, row, _pcol(tm, 4), _pcol(tm, 5), _pcol(tm, 6),
                  pl.BlockSpec((1, D), lambda i: (0, 0)), pl.BlockSpec((1, D), lambda i: (0, 1)),
                  _full((3, D, D)), row, _full((SMALL_ROWS, D))],
        out_specs=(row,) * 7 + (pl.BlockSpec((tm, 1), lambda i: (i, 0)),),
        out_shape=(b16o, f32o, f32o, b16o, f32o, b16o, f32o, SDS((t, 1), F32)),
        compiler_params=_cp("parallel"),
    )(pa, h, p, p, p, bm, bm, w3, xf, small)


def mixb_head(dxn, xh, rstd, small, w3, p, bm, ya, yb, ca, h):
    t = dxn.shape[0]
    tm = _row_tile(t, 256)

    def body(dy_ref, xh_ref, rs_ref, sm_ref, w_ref, bg_ref, yr_ref, gla_ref, glb_ref, bma_ref, bmb_ref,
             ya_ref, yb_ref, ca_ref, h_ref,
             dz_ref, dzb_ref, dya_ref, dyb_ref, dbg_ref, dca_ref, dh_ref, dphi_ref, dln_ref, dbm_ref):
        i = pl.program_id(0)
        dy = dy_ref[...]
        xhat = xh_ref[...]
        dz = _ln_bwd(dy, xhat, rs_ref[...], sm_ref[R_LNG + 1:R_LNG + 2, :])

        @pl.when(i == 0)
        def _():
            dln_ref[...] = jnp.zeros_like(dln_ref)
            dbm_ref[...] = jnp.zeros_like(dbm_ref)

        dln_ref[0:1, :] += jnp.sum(dy * xhat, axis=0, keepdims=True)
        dln_ref[1:2, :] += jnp.sum(dy, axis=0, keepdims=True)
        dz_ref[...] = dz
        dzb = dz.astype(BF16)
        dzb_ref[...] = dzb
        dm = _dot(dzb, w_ref[2], NT)
        ga = _sigmoid(gla_ref[...] + bma_ref[...])
        gb = _sigmoid(glb_ref[...] + bmb_ref[...])
        dya = (dm * ga).astype(BF16)
        dyb = (dm * gb).astype(BF16)
        dya_ref[...] = dya
        dyb_ref[...] = dyb
        dgla = dm * ya_ref[...] * ga * (1.0 - ga)
        dglb = dm * yb_ref[...] * gb * (1.0 - gb)
        dbm_ref[0:1, :] += jnp.sum(dgla, axis=0, keepdims=True)
        dbm_ref[1:2, :] += jnp.sum(dglb, axis=0, keepdims=True)
        dphi_ref[:, D:2 * D] = dgla.astype(BF16)
        dphi_ref[:, 2 * D:3 * D] = dglb.astype(BF16)
        dpa = _dot(dya, w_ref[0], NT)
        dpb = _dot(dyb, w_ref[1], NT)
        dbg_ref[...] = (dpa * ca_ref[...]).astype(BF16)
        dca_ref[...] = dpa * bg_ref[...]
        yr = yr_ref[...]
        ge, th = _gelu(yr)
        dh_ref[...] = dpb * ge
        dphi_ref[:, 0:D] = (dpb * h_ref[...] * _gelu_grad(yr, th)).astype(BF16)

    row = pl.BlockSpec((tm, D), lambda i: (i, 0))
    f32o, b16o = SDS((t, D), F32), SDS((t, D), BF16)
    acc2 = pl.BlockSpec((2, D), lambda i: (0, 0))
    return pl.pallas_call(
        body, name="mixb_head", grid=(t // tm,),
        in_specs=[row, row, pl.BlockSpec((tm, 1), lambda i: (i, 0)), _full((SMALL_ROWS, D)), _full((3, D, D)),
                  _pcol(tm, 0), _pcol(tm, 4), _pcol(tm, 5), _pcol(tm, 6),
                  pl.BlockSpec((1, D), lambda i: (0, 0)), pl.BlockSpec((1, D), lambda i: (0, 1)),
                  row, row, row, row],
        out_specs=(row,) * 7 + (pl.BlockSpec((tm, 3 * D), lambda i: (i, 0)), acc2, acc2),
        out_shape=(f32o, b16o, b16o, b16o, b16o, f32o, f32o, SDS((t, 3 * D), BF16), SDS((2, D), F32), SDS((2, D), F32)),
        compiler_params=_cp("arbitrary"),
    )(dxn, xh, rstd, small, w3, p, p, p, p, bm, bm, ya, yb, ca, h)


def mixb_rec(lam, gr, gi, h, xc, ap, rgw):
    t = lam.shape[0]
    tm = _row_tile(t, 256)

    def body(l_ref, gr_ref, gi_ref, h_ref, hh_ref, xc_ref, ap_ref, rgw_ref, dg_ref, dxc_ref, red_ref, ext):
        i = pl.program_id(0)
        first = i == 0

        @pl.when(first)
        def _():
            red_ref[...] = jnp.zeros_like(red_ref)

        ext[0:8, :] = jnp.where(first, 0.0, hh_ref[...])
        ext[8:, :] = h_ref[...]
        hprev = ext[pl.ds(7, tm), :]
        lam = l_ref[...]
        gr = gr_ref[...]
        gi = gi_ref[...]
        xc = xc_ref[...]
        ap = ap_ref[...]
        sp = _softplus(-ap)
        la = (-LRU_C) * gr * sp
        a = jnp.exp(la)
        row = lax.broadcasted_iota(jnp.int32, (tm, D), 0) + i * tm
        start = row == 0
        mult = jnp.where(start, 1.0, jnp.sqrt(_neg_expm1(2.0 * la)))
        dmult = jnp.where(start, 0.0, lam * xc * gi)
        dla = lam * hprev * a - dmult * a * a / mult
        dg1 = (-LRU_C) * sp * dla * gr * (1.0 - gr)
        dg0 = lam * xc * mult * gi * (1.0 - gi)
        dsp = jnp.sum((-LRU_C) * gr * dla, axis=0, keepdims=True)
        red_ref[0:1, :] += jnp.sum(dg0, axis=0, keepdims=True)
        red_ref[1:2, :] += jnp.sum(dg1, axis=0, keepdims=True)
        red_ref[2:3, :] += -dsp * _sigmoid(-ap)
        dg0b = dg0.astype(BF16)
        dg1b = dg1.astype(BF16)
        dg_ref[0] = dg0b
        dg_ref[1] = dg1b
        parts = []
        for hd in range(HEADS):
            sl = slice(hd * HD, (hd + 1) * HD)
            parts.append(_dot(dg0b[:, sl], rgw_ref[0, hd], NT) + _dot(dg1b[:, sl], rgw_ref[1, hd], NT))
        dxc = lam * gi * mult + jnp.concatenate(parts, axis=1)
        dxc_ref[...] = dxc
        red_ref[3:4, :] += jnp.sum(dxc, axis=0, keepdims=True)

    row = pl.BlockSpec((tm, D), lambda i: (i, 0))
    return pl.pallas_call(
        body, name="mixb_rec", grid=(t // tm,),
        in_specs=[row, row, row, row, _prev_halo(tm, 0), row, _full((1, D)), _full((2, HEADS, HD, HD))],
        out_specs=(pl.BlockSpec((2, tm, D), lambda i: (0, i, 0)), row, pl.BlockSpec((8, D), lambda i: (0, 0))),
        out_shape=(SDS((2, t, D), BF16), SDS((t, D), F32), SDS((8, D), F32)),
        scratch_shapes=[pltpu.VMEM((tm + 8, D), F32)],
        compiler_params=_cp("arbitrary"),
    )(lam, gr, gi, h, h, xc, ap, rgw)


def mixb_conv(dca, dxc, dbg, p, small):
    t = dca.shape[0]
    tm = _row_tile(t, 256)
    nt = t // tm

    def body(dca_ref, dcan_ref, dxc_ref, dxcn_ref, dbg_ref, cg_ref, v_ref, xr_ref, cgh_ref, vh_ref, xrh_ref, sm_ref,
             dplo_ref, dxr_ref, red_ref, e_dca, e_dxc, e_cv, e_xr):
        i = pl.program_id(0)
        first = i == 0
        last = i == nt - 1

        @pl.when(first)
        def _():
            red_ref[...] = jnp.zeros_like(red_ref)

        dca = dca_ref[...]
        dxc = dxc_ref[...]
        e_dca[0:tm, :] = dca
        e_dca[tm:, :] = jnp.where(last, 0.0, dcan_ref[...])
        e_dxc[0:tm, :] = dxc
        e_dxc[tm:, :] = jnp.where(last, 0.0, dxcn_ref[...])
        cg = cg_ref[...]
        v = v_ref[...]
        xr = xr_ref[...]
        e_cv[0:8, :] = jnp.where(first, 0.0, cgh_ref[...] * vh_ref[...])
        e_cv[8:, :] = cg * v
        e_xr[0:8, :] = jnp.where(first, 0.0, xrh_ref[...])
        e_xr[8:, :] = xr
        dcv = (sm_ref[R_SC + 2:R_SC + 3, :] * dca + sm_ref[R_SC + 1:R_SC + 2, :] * e_dca[pl.ds(1, tm), :]
               + sm_ref[R_SC:R_SC + 1, :] * e_dca[pl.ds(2, tm), :])
        dplo_ref[:, 0:D] = dbg_ref[...]
        dplo_ref[:, D:2 * D] = (dcv * v).astype(BF16)
        dplo_ref[:, 2 * D:3 * D] = (dcv * cg).astype(BF16)
        dxr = (sm_ref[R_RC + 3:R_RC + 4, :] * dxc + sm_ref[R_RC + 2:R_RC + 3, :] * e_dxc[pl.ds(1, tm), :]
               + sm_ref[R_RC + 1:R_RC + 2, :] * e_dxc[pl.ds(2, tm), :] + sm_ref[R_RC:R_RC + 1, :] * e_dxc[pl.ds(3, tm), :])
        dxr_ref[...] = dxr.astype(BF16)
        for k in range(3):
            red_ref[R_SC + k:R_SC + k + 1, :] += jnp.sum(dca * e_cv[pl.ds(6 + k, tm), :], axis=0, keepdims=True)
        for k in range(4):
            red_ref[R_RC + k:R_RC + k + 1, :] += jnp.sum(dxc * e_xr[pl.ds(5 + k, tm), :], axis=0, keepdims=True)

    row = pl.BlockSpec((tm, D), lambda i: (i, 0))
    ext = pltpu.VMEM((tm + 8, D), F32)
    return pl.pallas_call(
        body, name="mixb_conv", grid=(nt,),
        in_specs=[row, _next_halo(tm, t, 0), row, _next_halo(tm, t, 0), row,
                  _pcol(tm, 1), _pcol(tm, 2), _pcol(tm, 3), _prev_halo(tm, 1), _prev_halo(tm, 2), _prev_halo(tm, 3),
                  _full((SMALL_ROWS, D))],
        out_specs=(pl.BlockSpec((tm, 3 * D), lambda i: (i, 0)), row, pl.BlockSpec((8, D), lambda i: (0, 0))),
        out_shape=(SDS((t, 3 * D), BF16), SDS((t, D), BF16), SDS((8, D), F32)),
        scratch_shapes=[ext, ext, ext, ext],
        compiler_params=_cp("arbitrary"),
    )(dca, dca, dxc, dxc, dbg, p, p, p, p, p, p, small)


def mixb_dx(dz, dplo, dxr, dphi, win):
    t = dz.shape[0]
    tm = _row_tile(t, 512)

    def body(dz_ref, lo_ref, xr_ref, hi_ref, w_ref, dx_ref, acc):
        k = pl.program_id(1)

        @pl.when(k == 0)
        def _():
            acc[...] = ALPHA * dz_ref[...]

        @pl.when(k < 3)
        def _():
            acc[...] += _dot(lo_ref[...], w_ref[...], NT)

        @pl.when(k == 3)
        def _():
            acc[...] += _dot(xr_ref[...], w_ref[...], NT)

        @pl.when(k > 3)
        def _():
            acc[...] += _dot(hi_ref[...], w_ref[...], NT)

        @pl.when(k == 6)
        def _():
            dx_ref[...] = acc[...]

    row = pl.BlockSpec((tm, D), lambda i, k: (i, 0))
    return pl.pallas_call(
        body, name="mixb_dx", grid=(t // tm, 7),
        in_specs=[row, pl.BlockSpec((tm, D), lambda i, k: (i, jnp.minimum(k, 2))), row,
                  pl.BlockSpec((tm, D), lambda i, k: (i, jnp.clip(k - 4, 0, 2))),
                  pl.BlockSpec((D, D), lambda i, k: (0, k))],
        out_specs=row, out_shape=SDS((t, D), F32),
        scratch_shapes=[pltpu.VMEM((tm, D), F32)],
        compiler_params=_cp("parallel", "arbitrary"),
    )(dz, dplo, dxr, dphi, win)


def mixb_dwin(xb, dplo, dxr, dphi):
    t = xb.shape[0]
    tk = _row_tile(t, 512)
    nk = t // tk

    def body(x_ref, lo_ref, xr_ref, hi_ref, o_ref, acc):
        n = pl.program_id(0)
        k = pl.program_id(1)

        @pl.when(k == 0)
        def _():
            acc[...] = jnp.zeros_like(acc)

        @pl.when(n < 3)
        def _():
            acc[...] += _dot(x_ref[...], lo_ref[...], TN)

        @pl.when(n == 3)
        def _():
            acc[...] += _dot(x_ref[...], xr_ref[...], TN)

        @pl.when(n > 3)
        def _():
            acc[...] += _dot(x_ref[...], hi_ref[...], TN)

        @pl.when(k == nk - 1)
        def _():
            o_ref[...] = acc[...].astype(BF16)

    return pl.pallas_call(
        body, name="mixb_dwin", grid=(7, nk),
        in_specs=[pl.BlockSpec((tk, D), lambda n, k: (k, 0)),
                  pl.BlockSpec((tk, D), lambda n, k: (jnp.where(n < 3, k, 0), jnp.minimum(n, 2))),
                  pl.BlockSpec((tk, D), lambda n, k: (jnp.where(n == 3, k, 0), 0)),
                  pl.BlockSpec((tk, D), lambda n, k: (jnp.where(n > 3, k, 0), jnp.clip(n - 4, 0, 2)))],
        out_specs=pl.BlockSpec((D, D), lambda n, k: (0, n)), out_shape=SDS((D, D_IN), BF16),
        scratch_shapes=[pltpu.VMEM((D, D), F32)],
        compiler_params=_cp("parallel", "arbitrary"),
    )(xb, dplo, dxr, dphi)


def mixb_drgw(xcb, dg):
    t = xcb.shape[0]
    tk = _row_tile(t, 512)
    return _mm_tn("mixb_drgw", xcb, pl.BlockSpec((tk, HD), lambda g, h, k: (k, h)),
                  dg, pl.BlockSpec((None, tk, HD), lambda g, h, k: (g, k, h)),
                  SDS((2, HEADS, HD, HD), BF16), pl.BlockSpec((None, None, HD, HD), lambda g, h, k: (g, h, 0, 0)),
                  (2, HEADS, t // tk), (HD, HD))


def loss_head(y, tgt):
    t = y.shape[0]
    tm = _row_tile(t, 512)

    def body(y_ref, t_ref, dy_ref, l_ref):
        i = pl.program_id(0)
        e = y_ref[...] - t_ref[...]
        dy_ref[...] = e * (1.0 / D)

        @pl.when(i == 0)
        def _():
            l_ref[...] = jnp.zeros_like(l_ref)

        l_ref[...] += 0.5 * jnp.sum(jnp.mean(e * e, axis=-1, keepdims=True), axis=0, keepdims=True)

    row = pl.BlockSpec((tm, D), lambda i: (i, 0))
    return pl.pallas_call(
        body, name="loss_head", grid=(t // tm,), in_specs=[row, row],
        out_specs=(row, pl.BlockSpec((8, 128), lambda i: (0, 0))),
        out_shape=(SDS((t, D), F32), SDS((8, 128), F32)),
        compiler_params=_cp("arbitrary"),
    )(y, tgt)


def _adamw(w, g, m, v):
    m = ADAM_B1 * m + (1.0 - ADAM_B1) * g
    v = ADAM_B2 * v + (1.0 - ADAM_B2) * (g * g)
    m_hat = m / (1.0 - ADAM_B1 ** ADAM_STEP)
    v_hat = v / (1.0 - ADAM_B2 ** ADAM_STEP)
    delta = -ADAM_LR * (m_hat / (jnp.sqrt(v_hat) + ADAM_EPS) + ADAM_WD * w)
    return delta, m, v


def adam_big(name, w, m, v, parts, rows, lanes, part_rows, blk_off, tr):
    nr = rows // tr

    def body(w_ref, m_ref, v_ref, p0, p1, p2, p3, g_ref, d_ref, mo_ref, vo_ref):
        l = pl.program_id(0)
        for ll, pr in enumerate((p0, p1, p2, p3)):
            @pl.when(l == ll)
            def _():
                g = pr[0].astype(F32)
                for s in range(1, N_DEV):
                    g = g + pr[s].astype(F32)
                g_ref[...] = g
                d, mn, vn = _adamw(w_ref[...], g, m_ref[...], v_ref[...])
                d_ref[...] = d
                mo_ref[...] = mn
                vo_ref[...] = vn

    blk = pl.BlockSpec((None, tr, lanes), lambda l, r: (l, r, 0))

    def part_spec(ll):
        return pl.BlockSpec((N_DEV, tr, lanes), lambda l, r: (0, jnp.where(l == ll, r, 0) + blk_off, 0))

    out = SDS((DEPTH, rows, lanes), F32)
    return pl.pallas_call(
        body, name=name, grid=(DEPTH, nr),
        in_specs=[blk, blk, blk] + [part_spec(ll) for ll in range(DEPTH)],
        out_specs=(blk,) * 4, out_shape=(out,) * 4,
        compiler_params=_cp("parallel", "parallel"),
    )(w, m, v, *parts)


def sum_small(land):
    def body(l_ref, o_ref):
        g = l_ref[0]
        for s in range(1, N_DEV):
            g = g + l_ref[s]
        o_ref[...] = g

    return pl.pallas_call(body, name="sum_small", out_shape=SDS(land.shape[1:], F32))(land)


def adam_small(name, w, g, m, v):
    def body(w_ref, g_ref, m_ref, v_ref, d_ref, mo_ref, vo_ref):
        d, mn, vn = _adamw(w_ref[...], g_ref[...], m_ref[...], v_ref[...])
        d_ref[...] = d
        mo_ref[...] = mn
        vo_ref[...] = vn

    out = SDS(w.shape, F32)
    return pl.pallas_call(body, name=name, out_shape=(out, out, out))(w, g, m, v)


def _ffn_forward(xf, xb, gw, f, s):
    g, u, a = ffn_up(xb, gw["w1"], f)
    xo, xob, xh, rs = ffn_down_ln(a, gw["w2"], f, xf, gw["small"], s)
    return (xo, xob), dict(xb=xb, g=g, u=u, a=a, xh=xh, rs=rs)


def _ffn_backward(dxn, sv, gw, f, s):
    dz, df, dgu, dln = ffn_bwd_gates(dxn, sv["xh"], sv["rs"], gw["small"], s, gw["w2"], f, sv["g"], sv["u"])
    dx = ffn_bwd_dx(dz, dgu, gw["w1"], f)
    dw1 = ffn_dw1(sv["xb"], dgu, f)
    dw2 = ffn_dw2(sv["a"], df, f)
    return dx, dw1, dw2, dln


def _mixer_forward(xf, xb, gw, rcb, ap, bm):
    p = mix_proj(xb, gw["win"])
    ca, pa, xc, xcb, gi, gr, a, bb = mix_pre(p, gw["small"], rcb, ap, gw["rgw"])
    h = scan_fwd(a, bb)
    pb, ya, yb, m, xo, xob, xh, rs = mix_out(pa, h, p, bm, gw["w3"], xf, gw["small"])
    sv = dict(xb=xb, p=p, ca=ca, pa=pa, xc=xc, xcb=xcb, gi=gi, gr=gr, a=a, h=h, pb=pb, ya=ya, yb=yb, m=m, xh=xh, rs=rs)
    return (xo, xob), sv


def _mixer_backward(dxn, sv, gw, rcb, ap, bm):
    dz, dzb, dya, dyb, dbg, dca, dh, dphi, dln, dbm = mixb_head(
        dxn, sv["xh"], sv["rs"], gw["small"], gw["w3"], sv["p"], bm, sv["ya"], sv["yb"], sv["ca"], sv["h"])
    lam = scan_bwd(sv["a"], dh)
    dg, dxc, red_rec = mixb_rec(lam, sv["gr"], sv["gi"], sv["h"], sv["xc"], ap, gw["rgw"])
    dplo, dxr, red_conv = mixb_conv(dca, dxc, dbg, sv["p"], gw["small"])
    dx = mixb_dx(dz, dplo, dxr, dphi, gw["win"])
    dwin = mixb_dwin(sv["xb"], dplo, dxr, dphi)
    dwo = mm_tn_square("mixb_dwo", sv["m"], dzb)
    dwoc = mm_tn_square("mixb_dwoc", sv["pa"], dya)
    dwor = mm_tn_square("mixb_dwor", sv["pb"], dyb)
    drgw = mixb_drgw(sv["xcb"], dg)
    return dx, dict(dwin=dwin, dwoc=dwoc, dwor=dwor, dwo=dwo, drgw=drgw), dln, dbm, red_rec, red_conv


def kernel(x, w_in, b_merge, sc_w, rc_w, rc_b, rg_w, rg_b, a_param, w_out_conv, w_out_rnn, w_o, ffn_w1, ffn_w2, ln_g, ln_b, loss_target, m_w_in, m_b_merge, m_sc_w, m_rc_w, m_rc_b, m_rg_w, m_rg_b, m_a_param, m_w_out_conv, m_w_out_rnn, m_w_o, m_ffn_w1, m_ffn_w2, m_ln_g, m_ln_b, v_w_in, v_b_merge, v_sc_w, v_rc_w, v_rc_b, v_rg_w, v_rg_b, v_a_param, v_w_out_conv, v_w_out_rnn, v_w_o, v_ffn_w1, v_ffn_w2, v_ln_g, v_ln_b):
    t = x.shape[1]
    me = _me()

    def rows(parts, total):
        out, off = None, 0
        for part in parts:
            r = part.shape[-2]
            pad = [(0, 0)] * (part.ndim - 2) + [(off, total - off - r), (0, 0)]
            padded = jnp.pad(part, pad)
            out = padded if out is None else out + padded
            off += r
        return out

    def pack_small(sc, rc, rgb, lng, lnb):
        return rows([sc, rc, rgb, lng, lnb], SMALL_ROWS)

    def pack_rep(bmv, rcbv, apv):
        return rows([bmv.reshape(DEPTH, 2, D), rcbv[:, None], apv[:, None]], 8)

    small = pack_small(sc_w, rc_w, rg_b, ln_g, ln_b)

    def layer_shards(l):
        return [w_in[l].astype(BF16), jnp.stack([w_out_conv[l], w_out_rnn[l], w_o[l]]).astype(BF16),
                ffn_w1[l].astype(BF16), ffn_w2[l].astype(BF16), rg_w[l].astype(BF16), small[l]]

    def as_weights(lands):
        return dict(zip(("win", "w3", "w1", "w2", "rgw", "small"), lands))

    xf = x.reshape(t, D)
    cur = (xf, xf.astype(BF16))
    gathers = [weight_gather(l) for l in range(DEPTH)]
    shards = layer_shards(0)
    flight = gathers[0].start(shards, gathers[0].place_own(shards, GATHER_SHAPES), cur[1])
    lands, xf0 = gathers[0].wait(flight[0], flight[1], flight[2], flight[3], cur[0])
    cur = (xf0, flight[4])
    gws, saved = [as_weights(lands)], []
    for l in range(DEPTH):
        gw = gws[l]
        if l + 1 < DEPTH:
            shards = layer_shards(l + 1)
            flight = gathers[l + 1].start(shards, gathers[l + 1].place_own(shards, GATHER_SHAPES), cur[1])
            cur = (cur[0], flight[4])
        rcb, ap, bm = rc_b[l][None], a_param[l][None], b_merge[l][None]
        cur, s0 = _ffn_forward(cur[0], cur[1], gw, 0, 0)
        cur, s1 = _mixer_forward(cur[0], cur[1], gw, rcb, ap, bm)
        cur, s2 = _ffn_forward(cur[0], cur[1], gw, 1, 2)
        saved.append((s0, s1, s2))
        if l + 1 < DEPTH:
            lands, xo = gathers[l + 1].wait(flight[0], flight[1], flight[2], flight[3], cur[0])
            cur = (xo, cur[1])
            gws.append(as_weights(lands))

    dy, loss_tile = loss_head(cur[0], loss_target.reshape(t, D))
    loss = lax.psum(loss_tile[0, 0], ("x", "y", "c"))

    lands = [None] * DEPTH
    gsmall = [None] * DEPTH
    scatters = [grad_scatter(l) for l in range(DEPTH)]
    flight = None
    for l in reversed(range(DEPTH)):
        gw = gws[l]
        rcb, ap, bm = rc_b[l][None], a_param[l][None], b_merge[l][None]
        s0, s1, s2 = saved[l]
        dy, dw1b, dw2b, dln2 = _ffn_backward(dy, s2, gw, 1, 2)
        dy, dmix, dln1, dbm, red_rec, red_conv = _mixer_backward(dy, s1, gw, rcb, ap, bm)
        dy, dw1a, dw2a, dln0 = _ffn_backward(dy, s0, gw, 0, 0)
        if flight is not None:
            lands[l + 1], dy = scatters[l + 1].wait(flight[0], flight[1], flight[2], flight[3], dy)
        grads = [dmix["dwin"], dmix["dwoc"], dmix["dwor"], dmix["dwo"], dw1a, dw1b, dw2a, dw2b, dmix["drgw"]]
        flight = scatters[l].start(grads, scatters[l].place_own(grads, SCATTER_SHAPES), dy)
        dy = flight[4]
        gsmall[l] = rows([red_conv[0:7], red_rec[0:2], dln0[0:1], dln1[0:1], dln2[0:1], dln0[1:2], dln1[1:2],
                          dln2[1:2], dbm, red_rec[3:4], red_rec[2:3]], GRAD_ROWS)
    lands[0], dy = scatters[0].wait(flight[0], flight[1], flight[2], flight[3], dy)
    grad_x = dy.reshape(1, t, D)

    gfull = sum_small(gather_small_grads(jnp.stack(gsmall, axis=0)))
    g_sh = lax.dynamic_slice(gfull, (0, 0, me * 128), (DEPTH, SMALL_ROWS, 128))
    g_rep = gfull[:, G_BM:G_BM + 8, :]
    d_sh, m_sh, v_sh = adam_small("adam_small_sharded", small, g_sh,
                                  pack_small(m_sc_w, m_rc_w, m_rg_b, m_ln_g, m_ln_b),
                                  pack_small(v_sc_w, v_rc_w, v_rg_b, v_ln_g, v_ln_b))
    d_rep, m_rep, v_rep = adam_small("adam_small_replicated", pack_rep(b_merge, rc_b, a_param), g_rep,
                                     pack_rep(m_b_merge, m_rc_b, m_a_param), pack_rep(v_b_merge, v_rc_b, v_a_param))

    def unpack_sh(a):
        return dict(sc_w=a[:, 0:3], rc_w=a[:, 3:7], rg_b=a[:, 7:9], ln_g=a[:, 9:12], ln_b=a[:, 12:15])

    def unpack_rep(a):
        return dict(b_merge=a[:, 0:2].reshape(DEPTH, 2 * D), rc_b=a[:, 2], a_param=a[:, 3])

    l_win = [lands[l][0] for l in range(DEPTH)]
    l_w3 = [lands[l][1].reshape(N_DEV, 3 * 128, D) for l in range(DEPTH)]
    l_w1 = [lands[l][2].reshape(N_DEV, 2 * D, FS) for l in range(DEPTH)]
    l_w2 = [lands[l][3].reshape(N_DEV, 2 * W2S, D) for l in range(DEPTH)]
    l_rgw = [lands[l][4].reshape(N_DEV, 2 * HEADS * 32, HD) for l in range(DEPTH)]

    def big(name, w, m, v, parts, rows, lanes, blk_off, tr):
        shp = w.shape
        r3 = lambda a: a.reshape(DEPTH, rows, lanes)
        outs = adam_big(name, r3(w), r3(m), r3(v), parts, rows, lanes, parts[0].shape[1], blk_off, tr)
        return [o.reshape(shp) for o in outs]

    res = {}
    res["w_in"] = big("adam_w_in", w_in, m_w_in, v_w_in, l_win, D, WIN_S, 0, 128)
    res["w_out_conv"] = big("adam_w_out_conv", w_out_conv, m_w_out_conv, v_w_out_conv, l_w3, 128, D, 0, 128)
    res["w_out_rnn"] = big("adam_w_out_rnn", w_out_rnn, m_w_out_rnn, v_w_out_rnn, l_w3, 128, D, 1, 128)
    res["w_o"] = big("adam_w_o", w_o, m_w_o, v_w_o, l_w3, 128, D, 2, 128)
    res["ffn_w1"] = big("adam_ffn_w1", ffn_w1, m_ffn_w1, v_ffn_w1, l_w1, 2 * D, FS, 0, 128)
    res["ffn_w2"] = big("adam_ffn_w2", ffn_w2, m_ffn_w2, v_ffn_w2, l_w2, 2 * W2S, D, 0, 2 * W2S // 8)
    res["rg_w"] = big("adam_rg_w", rg_w, m_rg_w, v_rg_w, l_rgw, 2 * HEADS * 32, HD, 0, 128)

    gsh, grep = unpack_sh(g_sh), unpack_rep(g_rep)
    dsh, drep = unpack_sh(d_sh), unpack_rep(d_rep)
    msh, mrep = unpack_sh(m_sh), unpack_rep(m_rep)
    vsh, vrep = unpack_sh(v_sh), unpack_rep(v_rep)
    for n in ("sc_w", "rc_w", "rg_b", "ln_g", "ln_b"):
        res[n] = [gsh[n], dsh[n], msh[n], vsh[n]]
    for n in ("b_merge", "rc_b", "a_param"):
        res[n] = [grep[n], drep[n], mrep[n], vrep[n]]

    names = ["w_in", "b_merge", "sc_w", "rc_w", "rc_b", "rg_w", "rg_b", "a_param", "w_out_conv", "w_out_rnn", "w_o",
             "ffn_w1", "ffn_w2", "ln_g", "ln_b"]
    out = [loss, grad_x]
    for k in range(4):
        out += [res[n][k] for n in names]
    return tuple(out)
```

```python
import functools

import jax
import jax.numpy as jnp
from jax import lax
from jax.experimental import pallas as pl
from jax.experimental.pallas import tpu as pltpu

F32 = jnp.float32
BF16 = jnp.bfloat16
SDS = jax.ShapeDtypeStruct

N_DEV = 8
DEPTH = 4
D = 1024
D_FF = 2816
FS = D_FF // 4
W2S = D_FF // 8
D_IN = 7 * D
WIN_S = D_IN // 8
HEADS = 4
HD = D // HEADS
LRU_C = 8.0
ALPHA = (2.0 * DEPTH) ** 0.25
LN_EPS = 1e-5
ADAM_LR, ADAM_B1, ADAM_B2, ADAM_EPS, ADAM_WD, ADAM_STEP = 0.001, 0.9, 0.999, 1e-08, 0.01, 10

R_SC, R_RC, R_RGB, R_LNG, R_LNB = 0, 3, 7, 9, 12
SMALL_ROWS = 16
G_BM, G_RCB, G_AP = 15, 17, 18
GRAD_ROWS = 24

NN = ((1,), (0,))
NT = ((1,), (1,))
TN = ((0,), (0,))
MESH = pl.DeviceIdType.MESH
ANY = pl.BlockSpec(memory_space=pl.ANY)
VMEM_LIMIT = 52 * 1024 * 1024


def _dot(a, b, dims):
    return lax.dot_general(a, b, (dims, ((), ())), preferred_element_type=F32)


def _cp(*sem):
    return pltpu.CompilerParams(dimension_semantics=sem, vmem_limit_bytes=VMEM_LIMIT)


def _sigmoid(x):
    return 1.0 / (1.0 + jnp.exp(-x))


def _log1p(e):
    u = 1.0 + e
    return jnp.where(u == 1.0, e, jnp.log(u) * e / jnp.where(u == 1.0, 1.0, u - 1.0))


def _softplus(x):
    return jnp.maximum(x, 0.0) + _log1p(jnp.exp(-jnp.abs(x)))


def _neg_expm1(x):
    u = jnp.exp(x)
    um1 = u - 1.0
    safe = jnp.logical_and(u != 1.0, um1 != -1.0)
    r = um1 * x / jnp.where(safe, jnp.log(jnp.where(safe, u, 0.5)), 1.0)
    return -jnp.where(u == 1.0, x, jnp.where(um1 == -1.0, -1.0, r))


def _gelu(y):
    c = 0.7978845608028654
    t = jnp.tanh(c * (y + 0.044715 * y * y * y))
    return 0.5 * y * (1.0 + t), t


def _gelu_grad(y, t):
    c = 0.7978845608028654
    return 0.5 * (1.0 + t) + 0.5 * y * (1.0 - t * t) * c * (1.0 + 3.0 * 0.044715 * y * y)


def _ln_fwd(z, g, b):
    mu = jnp.mean(z, axis=-1, keepdims=True)
    zc = z - mu
    var = jnp.mean(zc * zc, axis=-1, keepdims=True)
    rstd = lax.rsqrt(var + LN_EPS)
    xh = zc * rstd
    return xh * g + b, xh, rstd


def _ln_bwd(dy, xh, rstd, g):
    dxh = dy * g
    m1 = jnp.mean(dxh, axis=-1, keepdims=True)
    m2 = jnp.mean(dxh * xh, axis=-1, keepdims=True)
    return rstd * (dxh - m1 - xh * m2)


def _row_tile(t, want):
    return min(want, t)


def _me():
    return 4 * lax.axis_index("x") + 2 * lax.axis_index("y") + lax.axis_index("c")


def _coords(p):
    return (p // 4, (p // 2) % 2, p % 2)


def _all_to_all_copies(srcs_of, dsts_of, waits, send_sems, recv_sems, loc_sems):
    me = _me()
    n = len(waits)
    own_src, own_dst = srcs_of(me), dsts_of(me)
    local = [pltpu.make_async_copy(own_src[k], own_dst[k], loc_sems.at[k]) for k in range(n)]
    for cp in local:
        cp.start()
    for d in range(1, N_DEV):
        p = (me + d) % N_DEV
        src, dst = srcs_of(p), dsts_of(me)
        for k in range(n):
            pltpu.make_async_remote_copy(
                src_ref=src[k], dst_ref=dst[k], send_sem=send_sems.at[waits[k][1]],
                recv_sem=recv_sems.at[waits[k][1]], device_id=_coords(p), device_id_type=MESH).start()
    done = set()
    for k in range(n):
        ref, s = waits[k]
        if s in done:
            continue
        done.add(s)
        pltpu.make_async_remote_copy(
            src_ref=ref, dst_ref=ref, send_sem=send_sems.at[s], recv_sem=recv_sems.at[s],
            device_id=_coords(me), device_id_type=MESH).wait()
    for cp in local:
        cp.wait()


HBM = pl.BlockSpec(memory_space=pltpu.HBM)
SEM = pl.BlockSpec(memory_space=pltpu.SEMAPHORE)
EFFECT = pltpu.SideEffectType.DATAFLOW_SIDE_EFFECTING


def _in_hbm(a):
    return pltpu.with_memory_space_constraint(a, pltpu.HBM)


class Exchange:
    def __init__(self, name, src_of, dst_of, sem_of, span_of):
        self.name, self.src_of, self.dst_of, self.sem_of, self.span_of = name, src_of, dst_of, sem_of, span_of
        self.nsem = max(sem_of) + 1

    def start(self, srcs, lands, thru):
        n, m = len(srcs), len(lands)
        ops = list(srcs) + list(lands) + list(thru)

        def body(*refs):
            src_refs, land_refs = refs[:n], refs[n:n + m]
            send_sems, recv_sems = refs[len(ops)], refs[len(ops) + 1]
            me = _me()
            for dd in range(1, N_DEV):
                p = (me + dd) % N_DEV
                s, d = self.src_of(src_refs, p), self.dst_of(land_refs, me)
                for k in range(len(self.sem_of)):
                    pltpu.make_async_remote_copy(
                        src_ref=s[k], dst_ref=d[k], send_sem=send_sems.at[self.sem_of[k]],
                        recv_sem=recv_sems.at[self.sem_of[k]], device_id=_coords(p), device_id_type=MESH).start()

        sem = pltpu.SemaphoreType.DMA((self.nsem,))
        res = pl.pallas_call(
            body, name=self.name + "_start",
            out_shape=(sem, sem) + tuple(pltpu.HBM(a.shape, a.dtype) for a in ops),
            in_specs=[HBM] * len(ops), out_specs=(SEM, SEM) + (HBM,) * len(ops),
            input_output_aliases={i: 2 + i for i in range(len(ops))},
            compiler_params=pltpu.CompilerParams(has_side_effects=EFFECT),
        )(*[_in_hbm(a) for a in ops])
        return res[0], res[1], res[2:2 + n], res[2 + n:2 + n + m], list(res[2 + n + m:])

    def wait(self, send_sems, recv_sems, srcs, lands, thru):
        n, m = len(srcs), len(lands)
        ops = list(srcs) + list(lands) + list(thru)

        def body(*refs):
            land_refs = refs[n:n + m]
            ssem, rsem = refs[len(ops)], refs[len(ops) + 1]
            me = _me()
            spans = self.span_of(land_refs)
            for s in range(self.nsem):
                cp = pltpu.make_async_remote_copy(
                    src_ref=spans[s], dst_ref=spans[s], send_sem=ssem.at[s], recv_sem=rsem.at[s],
                    device_id=_coords(me), device_id_type=MESH)
                cp.wait_send()
                cp.wait_recv()

        res = pl.pallas_call(
            body, name=self.name + "_wait",
            out_shape=tuple(pltpu.HBM(a.shape, a.dtype) for a in ops),
            in_specs=[HBM] * len(ops) + [SEM, SEM], out_specs=(HBM,) * len(ops),
            input_output_aliases={i: i for i in range(len(ops))},
            compiler_params=pltpu.CompilerParams(has_side_effects=EFFECT),
        )(*ops, send_sems, recv_sems)
        return res[n:n + m], list(res[n + m:])


def weight_gather(l):
    def src_of(refs, p):
        return list(refs)

    def dst_of(lands, p):
        o_win, o_w3, o_w1, o_w2, o_rgw, o_sm = lands
        return [o_win.at[:, pl.ds(pl.multiple_of(p * WIN_S, 128), WIN_S)],
                o_w3.at[:, pl.ds(pl.multiple_of(p * 128, 128), 128), :],
                o_w1.at[:, p],
                o_w2.at[:, pl.ds(pl.multiple_of(p * W2S, 16), W2S), :],
                o_rgw.at[:, :, pl.ds(pl.multiple_of(p * 32, 32), 32), :],
                o_sm.at[:, pl.ds(pl.multiple_of(p * 128, 128), 128)]]

    def span_of(lands):
        o_win, o_w3, o_w1, o_w2, o_rgw, o_sm = lands
        return [o_win.at[:, pl.ds(0, 7 * WIN_S)], o_w3.at[:, pl.ds(0, 7 * 128), :], o_w1.at[:, pl.ds(0, 7)],
                o_w2.at[:, pl.ds(0, 7 * W2S), :], o_rgw.at[:, :, pl.ds(0, 7 * 32), :], o_sm.at[:, pl.ds(0, 7 * 128)]]

    return Exchange(f"gather{l}", src_of, dst_of, [0, 1, 2, 3, 4, 5], span_of)


GATHER_SHAPES = (SDS((D, D_IN), BF16), SDS((3, D, D), BF16), SDS((2, N_DEV, D, FS), BF16),
                 SDS((2, D_FF, D), BF16), SDS((2, HEADS, HD, HD), BF16), SDS((SMALL_ROWS, D), F32))


def grad_scatter(l):
    def src_of(refs, p):
        dwin, dwoc, dwor, dwo, dw1a, dw1b, dw2a, dw2b, drgw = refs
        rows3 = pl.ds(pl.multiple_of(p * 128, 128), 128)
        rows2 = pl.ds(pl.multiple_of(p * W2S, 16), W2S)
        return [dwin.at[:, pl.ds(pl.multiple_of(p * WIN_S, 128), WIN_S)],
                dwoc.at[rows3, :], dwor.at[rows3, :], dwo.at[rows3, :],
                dw1a.at[p], dw1b.at[p], dw2a.at[rows2, :], dw2b.at[rows2, :],
                drgw.at[:, :, pl.ds(pl.multiple_of(p * 32, 32), 32), :]]

    def dst_of(lands, p):
        l_win, l_w3, l_w1, l_w2, l_rgw = lands
        return [l_win.at[p], l_w3.at[p, 0], l_w3.at[p, 1], l_w3.at[p, 2], l_w1.at[p, 0], l_w1.at[p, 1],
                l_w2.at[p, 0], l_w2.at[p, 1], l_rgw.at[p]]

    def span_of(lands):
        return [a.at[pl.ds(0, 7)] for a in lands]

    return Exchange(f"scatter{l}", src_of, dst_of, [0, 1, 1, 1, 2, 2, 3, 3, 4], span_of)


SCATTER_SHAPES = (SDS((N_DEV, D, WIN_S), BF16), SDS((N_DEV, 3, 128, D), BF16), SDS((N_DEV, 2, D, FS), BF16),
                  SDS((N_DEV, 2, W2S, D), BF16), SDS((N_DEV, 2, HEADS, 32, HD), BF16))


def gather_prep(l, me1, w_in, w_out_conv, w_out_rnn, w_o, ffn_w1, ffn_w2, rg_w, small):
    hd, hw = D // 2, W2S // 2

    def body(me_ref, win, woc, wor, wo, w1, w2, rgw, sm, c_win, c_w3, c_w1, c_w2, c_rgw, c_sm,
             o_win, o_w3, o_w1, o_w2, o_rgw, o_sm):
        a = win[...].astype(BF16)
        c_win[...] = a
        o_win[...] = a
        for k, r in enumerate((woc, wor, wo)):
            b = r[...].astype(BF16)
            c_w3[k] = b
            o_w3[k] = b
        for src, comp, own in ((w1, c_w1, o_w1), (w2, c_w2, o_w2), (rgw, c_rgw, o_rgw)):
            b = src[...].astype(BF16)
            comp[...] = b
            own[...] = b
        c_sm[...] = sm[...]
        o_sm[...] = sm[...]

    three = pl.BlockSpec((None, 64, D), lambda i, me: (l, i, 0))
    in_specs = [pl.BlockSpec((None, hd, WIN_S), lambda i, me: (l, i, 0)), three, three, three,
                pl.BlockSpec((None, 2, hd, FS), lambda i, me: (l, 0, i, 0)),
                pl.BlockSpec((None, 2, hw, D), lambda i, me: (l, 0, i, 0)),
                pl.BlockSpec((None, 2, HEADS, 16, HD), lambda i, me: (l, 0, 0, i, 0)),
                pl.BlockSpec((None, 8, 128), lambda i, me: (l, i, 0))]
    out_specs = (pl.BlockSpec((hd, WIN_S), lambda i, me: (i, 0)), pl.BlockSpec((3, 64, D), lambda i, me: (0, i, 0)),
                 pl.BlockSpec((2, hd, FS), lambda i, me: (0, i, 0)), pl.BlockSpec((2, hw, D), lambda i, me: (0, i, 0)),
                 pl.BlockSpec((2, HEADS, 16, HD), lambda i, me: (0, 0, i, 0)), pl.BlockSpec((8, 128), lambda i, me: (i, 0)),
                 pl.BlockSpec((hd, WIN_S), lambda i, me: (i, me[0])),
                 pl.BlockSpec((3, 64, D), lambda i, me: (0, 2 * me[0] + i, 0)),
                 pl.BlockSpec((2, None, hd, FS), lambda i, me: (0, me[0], i, 0)),
                 pl.BlockSpec((2, hw, D), lambda i, me: (0, 2 * me[0] + i, 0)),
                 pl.BlockSpec((2, HEADS, 16, HD), lambda i, me: (0, 0, 2 * me[0] + i, 0)),
                 pl.BlockSpec((8, 128), lambda i, me: (i, me[0])))
    compact = (SDS((D, WIN_S), BF16), SDS((3, 128, D), BF16), SDS((2, D, FS), BF16), SDS((2, W2S, D), BF16),
               SDS((2, HEADS, 32, HD), BF16), SDS((SMALL_ROWS, 128), F32))
    res = pl.pallas_call(
        body, name=f"gather_prep{l}", out_shape=compact + GATHER_SHAPES,
        grid_spec=pltpu.PrefetchScalarGridSpec(num_scalar_prefetch=1, grid=(2,), in_specs=in_specs, out_specs=out_specs),
        compiler_params=_cp("arbitrary"),
    )(me1, w_in, w_out_conv, w_out_rnn, w_o, ffn_w1, ffn_w2, rg_w, small)
    return list(res[:6]), list(res[6:])


def scatter_prep(l, me1, grads):
    hd, hw = D // 2, W2S // 2

    def body(me_ref, dwin, dwoc, dwor, dwo, dw1a, dw1b, dw2a, dw2b, drgw, l_win, l_w3, l_w1, l_w2, l_rgw):
        l_win[...] = dwin[...]
        for k, r in enumerate((dwoc, dwor, dwo)):
            l_w3[k] = r[...]
        l_w1[0] = dw1a[...]
        l_w1[1] = dw1b[...]
        l_w2[0] = dw2a[...]
        l_w2[1] = dw2b[...]
        l_rgw[...] = drgw[...]

    three = pl.BlockSpec((64, D), lambda i, me: (2 * me[0] + i, 0))
    one = pl.BlockSpec((None, hd, FS), lambda i, me: (me[0], i, 0))
    two = pl.BlockSpec((hw, D), lambda i, me: (2 * me[0] + i, 0))
    in_specs = [pl.BlockSpec((hd, WIN_S), lambda i, me: (i, me[0])), three, three, three, one, one, two, two,
                pl.BlockSpec((2, HEADS, 16, HD), lambda i, me: (0, 0, 2 * me[0] + i, 0))]
    out_specs = (pl.BlockSpec((None, hd, WIN_S), lambda i, me: (me[0], i, 0)),
                 pl.BlockSpec((None, 3, 64, D), lambda i, me: (me[0], 0, i, 0)),
                 pl.BlockSpec((None, 2, hd, FS), lambda i, me: (me[0], 0, i, 0)),
                 pl.BlockSpec((None, 2, hw, D), lambda i, me: (me[0], 0, i, 0)),
                 pl.BlockSpec((None, 2, HEADS, 16, HD), lambda i, me: (me[0], 0, 0, i, 0)))
    return list(pl.pallas_call(
        body, name=f"scatter_prep{l}", out_shape=SCATTER_SHAPES,
        grid_spec=pltpu.PrefetchScalarGridSpec(num_scalar_prefetch=1, grid=(2,), in_specs=in_specs, out_specs=out_specs),
        compiler_params=_cp("arbitrary"),
    )(me1, *grads))


def gather_small_grads(gsmall):
    shp = gsmall.shape

    def body(g, land, send_sems, recv_sems, loc_sems):
        _all_to_all_copies(lambda p: [g], lambda p: [land.at[p]], [(land.at[pl.ds(0, 7)], 0)],
                           send_sems, recv_sems, loc_sems)

    return pl.pallas_call(
        body, name="gather_small_grads", out_shape=SDS((N_DEV,) + shp, F32),
        in_specs=[ANY], out_specs=ANY,
        scratch_shapes=[pltpu.SemaphoreType.DMA((1,)), pltpu.SemaphoreType.DMA((1,)), pltpu.SemaphoreType.DMA((1,))],
        compiler_params=pltpu.CompilerParams(has_side_effects=True),
    )(gsmall)


def ffn_up(xb, w1, f):
    t = xb.shape[0]
    tm = _row_tile(t, 512)

    def body(x_ref, wg_ref, wu_ref, g_ref, u_ref, a_ref):
        x = x_ref[...]
        g = _dot(x, wg_ref[...], NN)
        u = _dot(x, wu_ref[...], NN)
        g_ref[...] = g.astype(BF16)
        u_ref[...] = u.astype(BF16)
        a_ref[...] = (g * _sigmoid(g) * u).astype(BF16)

    out = pl.BlockSpec((None, tm, FS), lambda j, i: (j, i, 0))
    return pl.pallas_call(
        body, name=f"ffn_up{f}", grid=(4, t // tm),
        in_specs=[pl.BlockSpec((tm, D), lambda j, i: (i, 0)),
                  pl.BlockSpec((None, None, D, FS), lambda j, i: (f, j, 0, 0)),
                  pl.BlockSpec((None, None, D, FS), lambda j, i: (f, j + 4, 0, 0))],
        out_specs=(out, out, out), out_shape=(SDS((4, t, FS), BF16),) * 3,
        compiler_params=_cp("parallel", "parallel"),
    )(xb, w1, w1)


def ffn_down_ln(a, w2, f, xf, small, s):
    t = xf.shape[0]
    tm = _row_tile(t, 256)

    def body(a_ref, w_ref, x_ref, sm_ref, xo_ref, xb_ref, xh_ref, rs_ref):
        acc = _dot(a_ref[0], w_ref[0:FS, :], NN)
        for j in range(1, 4):
            acc = acc + _dot(a_ref[j], w_ref[j * FS:(j + 1) * FS, :], NN)
        z = ALPHA * x_ref[...] + 0.5 * acc
        y, xh, rstd = _ln_fwd(z, sm_ref[R_LNG + s:R_LNG + s + 1, :], sm_ref[R_LNB + s:R_LNB + s + 1, :])
        xo_ref[...] = y
        xb_ref[...] = y.astype(BF16)
        xh_ref[...] = xh
        rs_ref[...] = rstd

    row = pl.BlockSpec((tm, D), lambda i: (i, 0))
    return pl.pallas_call(
        body, name=f"ffn_down_ln{f}", grid=(t // tm,),
        in_specs=[pl.BlockSpec((4, tm, FS), lambda i: (0, i, 0)),
                  pl.BlockSpec((None, D_FF, D), lambda i: (f, 0, 0)),
                  row, pl.BlockSpec((SMALL_ROWS, D), lambda i: (0, 0))],
        out_specs=(row, row, row, pl.BlockSpec((tm, 1), lambda i: (i, 0))),
        out_shape=(SDS((t, D), F32), SDS((t, D), BF16), SDS((t, D), F32), SDS((t, 1), F32)),
        compiler_params=_cp("parallel"),
    )(a, w2, xf, small)


def ffn_bwd_gates(dxn, xh, rstd, small, s, w2, f, g, u):
    t = dxn.shape[0]
    tm = _row_tile(t, 256)

    def body(dy_ref, xh_ref, rs_ref, sm_ref, w_ref, g_ref, u_ref, dz_ref, df_ref, dgu_ref, dln_ref):
        i = pl.program_id(0)
        dy = dy_ref[...]
        xhat = xh_ref[...]
        dz = _ln_bwd(dy, xhat, rs_ref[...], sm_ref[R_LNG + s:R_LNG + s + 1, :])

        @pl.when(i == 0)
        def _():
            dln_ref[...] = jnp.zeros_like(dln_ref)

        dln_ref[0:1, :] += jnp.sum(dy * xhat, axis=0, keepdims=True)
        dln_ref[1:2, :] += jnp.sum(dy, axis=0, keepdims=True)
        dz_ref[...] = dz
        df = (0.5 * dz).astype(BF16)
        df_ref[...] = df
        for j in range(4):
            da = _dot(df, w_ref[j * FS:(j + 1) * FS, :], NT)
            gg = g_ref[j].astype(F32)
            uu = u_ref[j].astype(F32)
            sg = _sigmoid(gg)
            dgu_ref[j] = (da * uu * (sg * (1.0 + gg * (1.0 - sg)))).astype(BF16)
            dgu_ref[j + 4] = (da * (gg * sg)).astype(BF16)

    row = pl.BlockSpec((tm, D), lambda i: (i, 0))
    gu = pl.BlockSpec((4, tm, FS), lambda i: (0, i, 0))
    return pl.pallas_call(
        body, name=f"ffn_bwd_gates{f}", grid=(t // tm,),
        in_specs=[row, row, pl.BlockSpec((tm, 1), lambda i: (i, 0)), pl.BlockSpec((SMALL_ROWS, D), lambda i: (0, 0)),
                  pl.BlockSpec((None, D_FF, D), lambda i: (f, 0, 0)), gu, gu],
        out_specs=(row, row, pl.BlockSpec((8, tm, FS), lambda i: (0, i, 0)), pl.BlockSpec((2, D), lambda i: (0, 0))),
        out_shape=(SDS((t, D), F32), SDS((t, D), BF16), SDS((8, t, FS), BF16), SDS((2, D), F32)),
        compiler_params=_cp("arbitrary"),
    )(dxn, xh, rstd, small, w2, g, u)


def ffn_bwd_dx(dz, dgu, w1, f):
    t = dz.shape[0]
    tm = _row_tile(t, 1024)

    def body(dz_ref, d_ref, w_ref, dx_ref, acc):
        k = pl.program_id(1)

        @pl.when(k == 0)
        def _():
            acc[...] = ALPHA * dz_ref[...]

        acc[...] += _dot(d_ref[...], w_ref[...], NT)

        @pl.when(k == 7)
        def _():
            dx_ref[...] = acc[...]

    row = pl.BlockSpec((tm, D), lambda i, k: (i, 0))
    return pl.pallas_call(
        body, name=f"ffn_bwd_dx{f}", grid=(t // tm, 8),
        in_specs=[row, pl.BlockSpec((None, tm, FS), lambda i, k: (k, i, 0)),
                  pl.BlockSpec((None, None, D, FS), lambda i, k: (f, k, 0, 0))],
        out_specs=row, out_shape=SDS((t, D), F32),
        scratch_shapes=[pltpu.VMEM((tm, D), F32)],
        compiler_params=_cp("parallel", "arbitrary"),
    )(dz, dgu, w1)


def _mm_tn(name, a, a_spec, b, b_spec, out_sds, out_spec, grid, acc_shape):
    nk = grid[-1]

    def body(a_ref, b_ref, o_ref, acc):
        k = pl.program_id(len(grid) - 1)

        @pl.when(k == 0)
        def _():
            acc[...] = jnp.zeros_like(acc)

        acc[...] += _dot(a_ref[...], b_ref[...], TN)

        @pl.when(k == nk - 1)
        def _():
            o_ref[...] = acc[...].astype(o_ref.dtype)

    return pl.pallas_call(
        body, name=name, grid=grid, in_specs=[a_spec, b_spec], out_specs=out_spec, out_shape=out_sds,
        scratch_shapes=[pltpu.VMEM(acc_shape, F32)],
        compiler_params=_cp(*(["parallel"] * (len(grid) - 1) + ["arbitrary"])),
    )(a, b)


def ffn_dw1(xb, dgu, f):
    t = xb.shape[0]
    tk = _row_tile(t, 512)
    return _mm_tn(f"ffn_dw1_{f}", xb, pl.BlockSpec((tk, D), lambda j, k: (k, 0)),
                  dgu, pl.BlockSpec((None, tk, FS), lambda j, k: (j, k, 0)),
                  SDS((8, D, FS), BF16), pl.BlockSpec((None, D, FS), lambda j, k: (j, 0, 0)),
                  (8, t // tk), (D, FS))


def ffn_dw2(a, df, f):
    t = df.shape[0]
    tk = _row_tile(t, 512)
    return _mm_tn(f"ffn_dw2_{f}", a, pl.BlockSpec((None, tk, FS), lambda j, k: (j, k, 0)),
                  df, pl.BlockSpec((tk, D), lambda j, k: (k, 0)),
                  SDS((D_FF, D), BF16), pl.BlockSpec((FS, D), lambda j, k: (j, 0)),
                  (4, t // tk), (FS, D))


def mm_tn_square(name, a, b):
    t = a.shape[0]
    tk = _row_tile(t, 512)
    return _mm_tn(name, a, pl.BlockSpec((tk, 512), lambda i, k: (k, i)),
                  b, pl.BlockSpec((tk, D), lambda i, k: (k, 0)),
                  SDS((D, D), BF16), pl.BlockSpec((512, D), lambda i, k: (i, 0)),
                  (2, t // tk), (512, D))


def mix_proj(xb, win):
    t = xb.shape[0]
    tm = _row_tile(t, 512)

    def body(x_ref, w_ref, o_ref):
        o_ref[...] = _dot(x_ref[...], w_ref[...], NN)

    return pl.pallas_call(
        body, name="mix_proj", grid=(7, t // tm),
        in_specs=[pl.BlockSpec((tm, D), lambda n, i: (i, 0)), pl.BlockSpec((D, D), lambda n, i: (0, n))],
        out_specs=pl.BlockSpec((tm, D), lambda n, i: (i, n)), out_shape=SDS((t, D_IN), F32),
        compiler_params=_cp("parallel", "parallel"),
    )(xb, win)


def _pcol(tm, k):
    return pl.BlockSpec((tm, D), lambda i: (i, k))


def _prev_halo(tm, k):
    return pl.BlockSpec((8, D), lambda i: (jnp.maximum(i * (tm // 8) - 1, 0), k))


def _next_halo(tm, t, k):
    return pl.BlockSpec((8, D), lambda i: (jnp.minimum((i + 1) * (tm // 8), t // 8 - 1), k))


def _full(shape):
    nd = len(shape)
    return pl.BlockSpec(shape, lambda i: (0,) * nd)


def mix_pre(p, small, rcb, ap, rgw):
    t = p.shape[0]
    tm = _row_tile(t, 256)

    def body(bg_ref, cg_ref, v_ref, xr_ref, cgh_ref, vh_ref, xrh_ref, sm_ref, rcb_ref, ap_ref, rgw_ref,
             ca_ref, pa_ref, xc_ref, xcb_ref, gi_ref, gr_ref, a_ref, bb_ref, ext1, ext2):
        i = pl.program_id(0)
        first = i == 0
        cv = cg_ref[...] * v_ref[...]
        ext1[0:8, :] = jnp.where(first, 0.0, cgh_ref[...] * vh_ref[...])
        ext1[8:, :] = cv
        xr = xr_ref[...]
        ext2[0:8, :] = jnp.where(first, 0.0, xrh_ref[...])
        ext2[8:, :] = xr
        ca = (sm_ref[R_SC:R_SC + 1, :] * ext1[pl.ds(6, tm), :] + sm_ref[R_SC + 1:R_SC + 2, :] * ext1[pl.ds(7, tm), :]
              + sm_ref[R_SC + 2:R_SC + 3, :] * cv)
        ca_ref[...] = ca
        pa_ref[...] = (bg_ref[...] * ca).astype(BF16)
        xc = (sm_ref[R_RC:R_RC + 1, :] * ext2[pl.ds(5, tm), :] + sm_ref[R_RC + 1:R_RC + 2, :] * ext2[pl.ds(6, tm), :]
              + sm_ref[R_RC + 2:R_RC + 3, :] * ext2[pl.ds(7, tm), :] + sm_ref[R_RC + 3:R_RC + 4, :] * xr
              + rcb_ref[...])
        xc_ref[...] = xc
        xcb = xc.astype(BF16)
        xcb_ref[...] = xcb
        g0, g1 = [], []
        for h in range(HEADS):
            xh = xcb[:, h * HD:(h + 1) * HD]
            g0.append(_dot(xh, rgw_ref[0, h], NN))
            g1.append(_dot(xh, rgw_ref[1, h], NN))
        gi = _sigmoid(jnp.concatenate(g0, axis=1) + sm_ref[R_RGB:R_RGB + 1, :])
        gr = _sigmoid(jnp.concatenate(g1, axis=1) + sm_ref[R_RGB + 1:R_RGB + 2, :])
        gi_ref[...] = gi
        gr_ref[...] = gr
        la = (-LRU_C) * gr * _softplus(-ap_ref[...])
        a_ref[...] = jnp.exp(la)
        row = lax.broadcasted_iota(jnp.int32, (tm, D), 0) + i * tm
        mult = jnp.where(row == 0, 1.0, jnp.sqrt(_neg_expm1(2.0 * la)))
        bb_ref[...] = xc * gi * mult

    row = pl.BlockSpec((tm, D), lambda i: (i, 0))
    f32o, b16o = SDS((t, D), F32), SDS((t, D), BF16)
    return pl.pallas_call(
        body, name="mix_pre", grid=(t // tm,),
        in_specs=[_pcol(tm, 0), _pcol(tm, 1), _pcol(tm, 2), _pcol(tm, 3),
                  _prev_halo(tm, 1), _prev_halo(tm, 2), _prev_halo(tm, 3),
                  _full((SMALL_ROWS, D)), _full((1, D)), _full((1, D)), _full((2, HEADS, HD, HD))],
        out_specs=(row,) * 8, out_shape=(f32o, b16o, f32o, b16o, f32o, f32o, f32o, f32o),
        scratch_shapes=[pltpu.VMEM((tm + 8, D), F32), pltpu.VMEM((tm + 8, D), F32)],
        compiler_params=_cp("parallel"),
    )(p, p, p, p, p, p, p, small, rcb, ap, rgw)


SCAN_LANES = 512


def scan_fwd(a, b):
    t = a.shape[0]
    cb = SCAN_LANES

    def body(a_ref, b_ref, h_ref):
        row8 = lax.broadcasted_iota(jnp.int32, (8, cb), 0)

        def step(g, carry):
            r = pl.multiple_of(g * 8, 8)
            aa = a_ref[pl.ds(r, 8), :]
            bb = b_ref[pl.ds(r, 8), :]
            for s in (1, 2, 4):
                a_sh = jnp.where(row8 >= s, pltpu.roll(aa, s, 0), 1.0)
                b_sh = jnp.where(row8 >= s, pltpu.roll(bb, s, 0), 0.0)
                bb = aa * b_sh + bb
                aa = aa * a_sh
            h = aa * carry + bb
            h_ref[pl.ds(r, 8), :] = h
            return jnp.broadcast_to(h[7:8, :], (8, cb))

        lax.fori_loop(0, t // 8, step, jnp.zeros((8, cb), F32))

    blk = pl.BlockSpec((t, cb), lambda c: (0, c))
    return pl.pallas_call(
        body, name="scan_fwd", grid=(D // cb,), in_specs=[blk, blk], out_specs=blk, out_shape=SDS((t, D), F32),
        compiler_params=_cp("parallel"),
    )(a, b)


def scan_bwd(a, dh):
    t = a.shape[0]
    cb = SCAN_LANES
    ng = t // 8

    def body(a_ref, d_ref, l_ref):
        row8 = lax.broadcasted_iota(jnp.int32, (8, cb), 0)

        def step(g, carry):
            lam_next, a_next = carry
            r = pl.multiple_of((ng - 1 - g) * 8, 8)
            a_grp = a_ref[pl.ds(r, 8), :]
            aa = jnp.where(row8 < 7, pltpu.roll(a_grp, 7, 0), a_next)
            bb = d_ref[pl.ds(r, 8), :]
            for s in (1, 2, 4):
                a_sh = jnp.where(row8 < 8 - s, pltpu.roll(aa, 8 - s, 0), 1.0)
                b_sh = jnp.where(row8 < 8 - s, pltpu.roll(bb, 8 - s, 0), 0.0)
                bb = aa * b_sh + bb
                aa = aa * a_sh
            lam = aa * lam_next + bb
            l_ref[pl.ds(r, 8), :] = lam
            return (jnp.broadcast_to(lam[0:1, :], (8, cb)), jnp.broadcast_to(a_grp[0:1, :], (8, cb)))

        z = jnp.zeros((8, cb), F32)
        lax.fori_loop(0, ng, step, (z, z))

    blk = pl.BlockSpec((t, cb), lambda c: (0, c))
    return pl.pallas_call(
        body, name="scan_bwd", grid=(D // cb,), in_specs=[blk, blk], out_specs=blk, out_shape=SDS((t, D), F32),
        compiler_params=_cp("parallel"),
    )(a, dh)


def mix_out(pa, h, p, bm, w3, xf, small):
    t = xf.shape[0]
    tm = _row_tile(t, 256)

    def body(pa_ref, h_ref, yr_ref, gla_ref, glb_ref, bma_ref, bmb_ref, w_ref, x_ref, sm_ref,
             pb_ref, ya_ref, yb_ref, m_ref, xo_ref, xb_ref, xh_ref, rs_ref):
        ge, _ = _gelu(yr_ref[...])
        pb = (h_ref[...] * ge).astype(BF16)
        pb_ref[...] = pb
        ya = _dot(pa_ref[...], w_ref[0], NN)
        yb = _dot(pb, w_ref[1], NN)
        ya_ref[...] = ya
        yb_ref[...] = yb
        ga = _sigmoid(gla_ref[...] + bma_ref[...])
        gb = _sigmoid(glb_ref[...] + bmb_ref[...])
        m = (ga * ya + gb * yb).astype(BF16)
        m_ref[...] = m
        z = ALPHA * x_ref[...] + _dot(m, w_ref[2], NN)
        y, xh, rstd = _ln_fwd(z, sm_ref[R_LNG + 1:R_LNG + 2, :], sm_ref[R_LNB + 1:R_LNB + 2, :])
        xo_ref[...] = y
        xb_ref[...] = y.astype(BF16)
        xh_ref[...] = xh
        rs_ref[...] = rstd

    row = pl.BlockSpec((tm, D), lambda i: (i, 0))
    f32o, b16o = SDS((t, D), F32), SDS((t, D), BF16)
    return pl.pallas_call(
        body, name="mix_out", grid=(t // tm,),
        in_specs=[row, row, _pcol(tm, 4), _pcol(tm, 5), _pcol(tm, 6),
                  pl.BlockSpec((1, D), lambda i: (0, 0)), pl.BlockSpec((1, D), lambda i: (0, 1)),
                  _full((3, D, D)), row, _full((SMALL_ROWS, D))],
        out_specs=(row,) * 7 + (pl.BlockSpec((tm, 1), lambda i: (i, 0)),),
        out_shape=(b16o, f32o, f32o, b16o, f32o, b16o, f32o, SDS((t, 1), F32)),
        compiler_params=_cp("parallel"),
    )(pa, h, p, p, p, bm, bm, w3, xf, small)


def mixb_head(dxn, xh, rstd, small, w3, p, bm, ya, yb, ca, h):
    t = dxn.shape[0]
    tm = _row_tile(t, 256)

    def body(dy_ref, xh_ref, rs_ref, sm_ref, w_ref, bg_ref, yr_ref, gla_ref, glb_ref, bma_ref, bmb_ref,
             ya_ref, yb_ref, ca_ref, h_ref,
             dz_ref, dzb_ref, dya_ref, dyb_ref, dbg_ref, dca_ref, dh_ref, dphi_ref, dln_ref, dbm_ref):
        i = pl.program_id(0)
        dy = dy_ref[...]
        xhat = xh_ref[...]
        dz = _ln_bwd(dy, xhat, rs_ref[...], sm_ref[R_LNG + 1:R_LNG + 2, :])

        @pl.when(i == 0)
        def _():
            dln_ref[...] = jnp.zeros_like(dln_ref)
            dbm_ref[...] = jnp.zeros_like(dbm_ref)

        dln_ref[0:1, :] += jnp.sum(dy * xhat, axis=0, keepdims=True)
        dln_ref[1:2, :] += jnp.sum(dy, axis=0, keepdims=True)
        dz_ref[...] = dz
        dzb = dz.astype(BF16)
        dzb_ref[...] = dzb
        dm = _dot(dzb, w_ref[2], NT)
        ga = _sigmoid(gla_ref[...] + bma_ref[...])
        gb = _sigmoid(glb_ref[...] + bmb_ref[...])
        dya = (dm * ga).astype(BF16)
        dyb = (dm * gb).astype(BF16)
        dya_ref[...] = dya
        dyb_ref[...] = dyb
        dgla = dm * ya_ref[...] * ga * (1.0 - ga)
        dglb = dm * yb_ref[...] * gb * (1.0 - gb)
        dbm_ref[0:1, :] += jnp.sum(dgla, axis=0, keepdims=True)
        dbm_ref[1:2, :] += jnp.sum(dglb, axis=0, keepdims=True)
        dphi_ref[:, D:2 * D] = dgla.astype(BF16)
        dphi_ref[:, 2 * D:3 * D] = dglb.astype(BF16)
        dpa = _dot(dya, w_ref[0], NT)
        dpb = _dot(dyb, w_ref[1], NT)
        dbg_ref[...] = (dpa * ca_ref[...]).astype(BF16)
        dca_ref[...] = dpa * bg_ref[...]
        yr = yr_ref[...]
        ge, th = _gelu(yr)
        dh_ref[...] = dpb * ge
        dphi_ref[:, 0:D] = (dpb * h_ref[...] * _gelu_grad(yr, th)).astype(BF16)

    row = pl.BlockSpec((tm, D), lambda i: (i, 0))
    f32o, b16o = SDS((t, D), F32), SDS((t, D), BF16)
    acc2 = pl.BlockSpec((2, D), lambda i: (0, 0))
    return pl.pallas_call(
        body, name="mixb_head", grid=(t // tm,),
        in_specs=[row, row, pl.BlockSpec((tm, 1), lambda i: (i, 0)), _full((SMALL_ROWS, D)), _full((3, D, D)),
                  _pcol(tm, 0), _pcol(tm, 4), _pcol(tm, 5), _pcol(tm, 6),
                  pl.BlockSpec((1, D), lambda i: (0, 0)), pl.BlockSpec((1, D), lambda i: (0, 1)),
                  row, row, row, row],
        out_specs=(row,) * 7 + (pl.BlockSpec((tm, 3 * D), lambda i: (i, 0)), acc2, acc2),
        out_shape=(f32o, b16o, b16o, b16o, b16o, f32o, f32o, SDS((t, 3 * D), BF16), SDS((2, D), F32), SDS((2, D), F32)),
        compiler_params=_cp("arbitrary"),
    )(dxn, xh, rstd, small, w3, p, p, p, p, bm, bm, ya, yb, ca, h)


def mixb_rec(lam, gr, gi, h, xc, ap, rgw):
    t = lam.shape[0]
    tm = _row_tile(t, 256)

    def body(l_ref, gr_ref, gi_ref, h_ref, hh_ref, xc_ref, ap_ref, rgw_ref, dg_ref, dxc_ref, red_ref, ext):
        i = pl.program_id(0)
        first = i == 0

        @pl.when(first)
        def _():
            red_ref[...] = jnp.zeros_like(red_ref)

        ext[0:8, :] = jnp.where(first, 0.0, hh_ref[...])
        ext[8:, :] = h_ref[...]
        hprev = ext[pl.ds(7, tm), :]
        lam = l_ref[...]
        gr = gr_ref[...]
        gi = gi_ref[...]
        xc = xc_ref[...]
        ap = ap_ref[...]
        sp = _softplus(-ap)
        la = (-LRU_C) * gr * sp
        a = jnp.exp(la)
        row = lax.broadcasted_iota(jnp.int32, (tm, D), 0) + i * tm
        start = row == 0
        mult = jnp.where(start, 1.0, jnp.sqrt(_neg_expm1(2.0 * la)))
        dmult = jnp.where(start, 0.0, lam * xc * gi)
        dla = lam * hprev * a - dmult * a * a / mult
        dg1 = (-LRU_C) * sp * dla * gr * (1.0 - gr)
        dg0 = lam * xc * mult * gi * (1.0 - gi)
        dsp = jnp.sum((-LRU_C) * gr * dla, axis=0, keepdims=True)
        red_ref[0:1, :] += jnp.sum(dg0, axis=0, keepdims=True)
        red_ref[1:2, :] += jnp.sum(dg1, axis=0, keepdims=True)
        red_ref[2:3, :] += -dsp * _sigmoid(-ap)
        dg0b = dg0.astype(BF16)
        dg1b = dg1.astype(BF16)
        dg_ref[0] = dg0b
        dg_ref[1] = dg1b
        parts = []
        for hd in range(HEADS):
            sl = slice(hd * HD, (hd + 1) * HD)
            parts.append(_dot(dg0b[:, sl], rgw_ref[0, hd], NT) + _dot(dg1b[:, sl], rgw_ref[1, hd], NT))
        dxc = lam * gi * mult + jnp.concatenate(parts, axis=1)
        dxc_ref[...] = dxc
        red_ref[3:4, :] += jnp.sum(dxc, axis=0, keepdims=True)

    row = pl.BlockSpec((tm, D), lambda i: (i, 0))
    return pl.pallas_call(
        body, name="mixb_rec", grid=(t // tm,),
        in_specs=[row, row, row, row, _prev_halo(tm, 0), row, _full((1, D)), _full((2, HEADS, HD, HD))],
        out_specs=(pl.BlockSpec((2, tm, D), lambda i: (0, i, 0)), row, pl.BlockSpec((8, D), lambda i: (0, 0))),
        out_shape=(SDS((2, t, D), BF16), SDS((t, D), F32), SDS((8, D), F32)),
        scratch_shapes=[pltpu.VMEM((tm + 8, D), F32)],
        compiler_params=_cp("arbitrary"),
    )(lam, gr, gi, h, h, xc, ap, rgw)


def mixb_conv(dca, dxc, dbg, p, small):
    t = dca.shape[0]
    tm = _row_tile(t, 256)
    nt = t // tm

    def body(dca_ref, dcan_ref, dxc_ref, dxcn_ref, dbg_ref, cg_ref, v_ref, xr_ref, cgh_ref, vh_ref, xrh_ref, sm_ref,
             dplo_ref, dxr_ref, red_ref, e_dca, e_dxc, e_cv, e_xr):
        i = pl.program_id(0)
        first = i == 0
        last = i == nt - 1

        @pl.when(first)
        def _():
            red_ref[...] = jnp.zeros_like(red_ref)

        dca = dca_ref[...]
        dxc = dxc_ref[...]
        e_dca[0:tm, :] = dca
        e_dca[tm:, :] = jnp.where(last, 0.0, dcan_ref[...])
        e_dxc[0:tm, :] = dxc
        e_dxc[tm:, :] = jnp.where(last, 0.0, dxcn_ref[...])
        cg = cg_ref[...]
        v = v_ref[...]
        xr = xr_ref[...]
        e_cv[0:8, :] = jnp.where(first, 0.0, cgh_ref[...] * vh_ref[...])
        e_cv[8:, :] = cg * v
        e_xr[0:8, :] = jnp.where(first, 0.0, xrh_ref[...])
        e_xr[8:, :] = xr
        dcv = (sm_ref[R_SC + 2:R_SC + 3, :] * dca + sm_ref[R_SC + 1:R_SC + 2, :] * e_dca[pl.ds(1, tm), :]
               + sm_ref[R_SC:R_SC + 1, :] * e_dca[pl.ds(2, tm), :])
        dplo_ref[:, 0:D] = dbg_ref[...]
        dplo_ref[:, D:2 * D] = (dcv * v).astype(BF16)
        dplo_ref[:, 2 * D:3 * D] = (dcv * cg).astype(BF16)
        dxr = (sm_ref[R_RC + 3:R_RC + 4, :] * dxc + sm_ref[R_RC + 2:R_RC + 3, :] * e_dxc[pl.ds(1, tm), :]
               + sm_ref[R_RC + 1:R_RC + 2, :] * e_dxc[pl.ds(2, tm), :] + sm_ref[R_RC:R_RC + 1, :] * e_dxc[pl.ds(3, tm), :])
        dxr_ref[...] = dxr.astype(BF16)
        for k in range(3):
            red_ref[R_SC + k:R_SC + k + 1, :] += jnp.sum(dca * e_cv[pl.ds(6 + k, tm), :], axis=0, keepdims=True)
        for k in range(4):
            red_ref[R_RC + k:R_RC + k + 1, :] += jnp.sum(dxc * e_xr[pl.ds(5 + k, tm), :], axis=0, keepdims=True)

    row = pl.BlockSpec((tm, D), lambda i: (i, 0))
    ext = pltpu.VMEM((tm + 8, D), F32)
    return pl.pallas_call(
        body, name="mixb_conv", grid=(nt,),
        in_specs=[row, _next_halo(tm, t, 0), row, _next_halo(tm, t, 0), row,
                  _pcol(tm, 1), _pcol(tm, 2), _pcol(tm, 3), _prev_halo(tm, 1), _prev_halo(tm, 2), _prev_halo(tm, 3),
                  _full((SMALL_ROWS, D))],
        out_specs=(pl.BlockSpec((tm, 3 * D), lambda i: (i, 0)), row, pl.BlockSpec((8, D), lambda i: (0, 0))),
        out_shape=(SDS((t, 3 * D), BF16), SDS((t, D), BF16), SDS((8, D), F32)),
        scratch_shapes=[ext, ext, ext, ext],
        compiler_params=_cp("arbitrary"),
    )(dca, dca, dxc, dxc, dbg, p, p, p, p, p, p, small)


def mixb_dx(dz, dplo, dxr, dphi, win):
    t = dz.shape[0]
    tm = _row_tile(t, 512)

    def body(dz_ref, lo_ref, xr_ref, hi_ref, w_ref, dx_ref, acc):
        k = pl.program_id(1)

        @pl.when(k == 0)
        def _():
            acc[...] = ALPHA * dz_ref[...]

        @pl.when(k < 3)
        def _():
            acc[...] += _dot(lo_ref[...], w_ref[...], NT)

        @pl.when(k == 3)
        def _():
            acc[...] += _dot(xr_ref[...], w_ref[...], NT)

        @pl.when(k > 3)
        def _():
            acc[...] += _dot(hi_ref[...], w_ref[...], NT)

        @pl.when(k == 6)
        def _():
            dx_ref[...] = acc[...]

    row = pl.BlockSpec((tm, D), lambda i, k: (i, 0))
    return pl.pallas_call(
        body, name="mixb_dx", grid=(t // tm, 7),
        in_specs=[row, pl.BlockSpec((tm, D), lambda i, k: (i, jnp.minimum(k, 2))), row,
                  pl.BlockSpec((tm, D), lambda i, k: (i, jnp.clip(k - 4, 0, 2))),
                  pl.BlockSpec((D, D), lambda i, k: (0, k))],
        out_specs=row, out_shape=SDS((t, D), F32),
        scratch_shapes=[pltpu.VMEM((tm, D), F32)],
        compiler_params=_cp("parallel", "arbitrary"),
    )(dz, dplo, dxr, dphi, win)


def mixb_dwin(xb, dplo, dxr, dphi):
    t = xb.shape[0]
    tk = _row_tile(t, 512)
    nk = t // tk

    def body(x_ref, lo_ref, xr_ref, hi_ref, o_ref, acc):
        n = pl.program_id(0)
        k = pl.program_id(1)

        @pl.when(k == 0)
        def _():
            acc[...] = jnp.zeros_like(acc)

        @pl.when(n < 3)
        def _():
            acc[...] += _dot(x_ref[...], lo_ref[...], TN)

        @pl.when(n == 3)
        def _():
            acc[...] += _dot(x_ref[...], xr_ref[...], TN)

        @pl.when(n > 3)
        def _():
            acc[...] += _dot(x_ref[...], hi_ref[...], TN)

        @pl.when(k == nk - 1)
        def _():
            o_ref[...] = acc[...].astype(BF16)

    return pl.pallas_call(
        body, name="mixb_dwin", grid=(7, nk),
        in_specs=[pl.BlockSpec((tk, D), lambda n, k: (k, 0)),
                  pl.BlockSpec((tk, D), lambda n, k: (jnp.where(n < 3, k, 0), jnp.minimum(n, 2))),
                  pl.BlockSpec((tk, D), lambda n, k: (jnp.where(n == 3, k, 0), 0)),
                  pl.BlockSpec((tk, D), lambda n, k: (jnp.where(n > 3, k, 0), jnp.clip(n - 4, 0, 2)))],
        out_specs=pl.BlockSpec((D, D), lambda n, k: (0, n)), out_shape=SDS((D, D_IN), BF16),
        scratch_shapes=[pltpu.VMEM((D, D), F32)],
        compiler_params=_cp("parallel", "arbitrary"),
    )(xb, dplo, dxr, dphi)


def mixb_drgw(xcb, dg):
    t = xcb.shape[0]
    tk = _row_tile(t, 512)
    return _mm_tn("mixb_drgw", xcb, pl.BlockSpec((tk, HD), lambda g, h, k: (k, h)),
                  dg, pl.BlockSpec((None, tk, HD), lambda g, h, k: (g, k, h)),
                  SDS((2, HEADS, HD, HD), BF16), pl.BlockSpec((None, None, HD, HD), lambda g, h, k: (g, h, 0, 0)),
                  (2, HEADS, t // tk), (HD, HD))


def loss_head(y, tgt):
    t = y.shape[0]
    tm = _row_tile(t, 512)

    def body(y_ref, t_ref, dy_ref, l_ref):
        i = pl.program_id(0)
        e = y_ref[...] - t_ref[...]
        dy_ref[...] = e * (1.0 / D)

        @pl.when(i == 0)
        def _():
            l_ref[...] = jnp.zeros_like(l_ref)

        l_ref[...] += 0.5 * jnp.sum(jnp.mean(e * e, axis=-1, keepdims=True), axis=0, keepdims=True)

    row = pl.BlockSpec((tm, D), lambda i: (i, 0))
    return pl.pallas_call(
        body, name="loss_head", grid=(t // tm,), in_specs=[row, row],
        out_specs=(row, pl.BlockSpec((8, 128), lambda i: (0, 0))),
        out_shape=(SDS((t, D), F32), SDS((8, 128), F32)),
        compiler_params=_cp("arbitrary"),
    )(y, tgt)


def _adamw(w, g, m, v):
    m = ADAM_B1 * m + (1.0 - ADAM_B1) * g
    v = ADAM_B2 * v + (1.0 - ADAM_B2) * (g * g)
    m_hat = m / (1.0 - ADAM_B1 ** ADAM_STEP)
    v_hat = v / (1.0 - ADAM_B2 ** ADAM_STEP)
    delta = -ADAM_LR * (m_hat / (jnp.sqrt(v_hat) + ADAM_EPS) + ADAM_WD * w)
    return delta, m, v


def adam_big(name, w, m, v, parts, rows, lanes, blk_off, tr, l0, prev=None):
    nr = rows // tr
    nl = len(parts)

    def body(w_ref, m_ref, v_ref, *rest):
        g_ref, d_ref, mo_ref, vo_ref = rest[-4:]
        l = pl.program_id(0)
        for ll in range(nl):
            pr = rest[ll]

            @pl.when(l == ll)
            def _():
                g = pr[0].astype(F32)
                for s in range(1, N_DEV):
                    g = g + pr[s].astype(F32)
                g_ref[...] = g
                d, mn, vn = _adamw(w_ref[...], g, m_ref[...], v_ref[...])
                d_ref[...] = d
                mo_ref[...] = mn
                vo_ref[...] = vn

    blk = pl.BlockSpec((None, tr, lanes), lambda l, r: (l + l0, r, 0))

    def part_spec(ll):
        return pl.BlockSpec((N_DEV, tr, lanes), lambda l, r: (0, jnp.where(l == ll, r, 0) + blk_off, 0))

    out = SDS((DEPTH, rows, lanes), F32)
    extra = [] if prev is None else list(prev)
    return pl.pallas_call(
        body, name=name, grid=(nl, nr),
        in_specs=[blk, blk, blk] + [part_spec(ll) for ll in range(nl)] + [ANY] * len(extra),
        out_specs=(blk,) * 4, out_shape=(out,) * 4,
        input_output_aliases={3 + nl + i: i for i in range(len(extra))},
        compiler_params=_cp("parallel", "parallel"),
    )(w, m, v, *parts, *extra)


def sum_small(land):
    def body(l_ref, o_ref):
        g = l_ref[0]
        for s in range(1, N_DEV):
            g = g + l_ref[s]
        o_ref[...] = g

    return pl.pallas_call(body, name="sum_small", out_shape=SDS(land.shape[1:], F32))(land)


def adam_small(name, w, g, m, v):
    def body(w_ref, g_ref, m_ref, v_ref, d_ref, mo_ref, vo_ref):
        d, mn, vn = _adamw(w_ref[...], g_ref[...], m_ref[...], v_ref[...])
        d_ref[...] = d
        mo_ref[...] = mn
        vo_ref[...] = vn

    out = SDS(w.shape, F32)
    return pl.pallas_call(body, name=name, out_shape=(out, out, out))(w, g, m, v)


def _ffn_forward(xf, xb, gw, f, s):
    g, u, a = ffn_up(xb, gw["w1"], f)
    xo, xob, xh, rs = ffn_down_ln(a, gw["w2"], f, xf, gw["small"], s)
    return (xo, xob), dict(xb=xb, g=g, u=u, a=a, xh=xh, rs=rs)


def _ffn_backward(dxn, sv, gw, f, s):
    dz, df, dgu, dln = ffn_bwd_gates(dxn, sv["xh"], sv["rs"], gw["small"], s, gw["w2"], f, sv["g"], sv["u"])
    dx = ffn_bwd_dx(dz, dgu, gw["w1"], f)
    dw1 = ffn_dw1(sv["xb"], dgu, f)
    dw2 = ffn_dw2(sv["a"], df, f)
    return dx, dw1, dw2, dln


def _mixer_forward(xf, xb, gw, rcb, ap, bm):
    p = mix_proj(xb, gw["win"])
    ca, pa, xc, xcb, gi, gr, a, bb = mix_pre(p, gw["small"], rcb, ap, gw["rgw"])
    h = scan_fwd(a, bb)
    pb, ya, yb, m, xo, xob, xh, rs = mix_out(pa, h, p, bm, gw["w3"], xf, gw["small"])
    sv = dict(xb=xb, p=p, ca=ca, pa=pa, xc=xc, xcb=xcb, gi=gi, gr=gr, a=a, h=h, pb=pb, ya=ya, yb=yb, m=m, xh=xh, rs=rs)
    return (xo, xob), sv


def _mixer_backward(dxn, sv, gw, rcb, ap, bm):
    dz, dzb, dya, dyb, dbg, dca, dh, dphi, dln, dbm = mixb_head(
        dxn, sv["xh"], sv["rs"], gw["small"], gw["w3"], sv["p"], bm, sv["ya"], sv["yb"], sv["ca"], sv["h"])
    lam = scan_bwd(sv["a"], dh)
    dg, dxc, red_rec = mixb_rec(lam, sv["gr"], sv["gi"], sv["h"], sv["xc"], ap, gw["rgw"])
    dplo, dxr, red_conv = mixb_conv(dca, dxc, dbg, sv["p"], gw["small"])
    dx = mixb_dx(dz, dplo, dxr, dphi, gw["win"])
    dwin = mixb_dwin(sv["xb"], dplo, dxr, dphi)
    dwo = mm_tn_square("mixb_dwo", sv["m"], dzb)
    dwoc = mm_tn_square("mixb_dwoc", sv["pa"], dya)
    dwor = mm_tn_square("mixb_dwor", sv["pb"], dyb)
    drgw = mixb_drgw(sv["xcb"], dg)
    return dx, dict(dwin=dwin, dwoc=dwoc, dwor=dwor, dwo=dwo, drgw=drgw), dln, dbm, red_rec, red_conv


def kernel(x, w_in, b_merge, sc_w, rc_w, rc_b, rg_w, rg_b, a_param, w_out_conv, w_out_rnn, w_o, ffn_w1, ffn_w2, ln_g, ln_b, loss_target, m_w_in, m_b_merge, m_sc_w, m_rc_w, m_rc_b, m_rg_w, m_rg_b, m_a_param, m_w_out_conv, m_w_out_rnn, m_w_o, m_ffn_w1, m_ffn_w2, m_ln_g, m_ln_b, v_w_in, v_b_merge, v_sc_w, v_rc_w, v_rc_b, v_rg_w, v_rg_b, v_a_param, v_w_out_conv, v_w_out_rnn, v_w_o, v_ffn_w1, v_ffn_w2, v_ln_g, v_ln_b):
    t = x.shape[1]
    me = _me()

    def rows(parts, total):
        out, off = None, 0
        for part in parts:
            r = part.shape[-2]
            pad = [(0, 0)] * (part.ndim - 2) + [(off, total - off - r), (0, 0)]
            padded = jnp.pad(part, pad)
            out = padded if out is None else out + padded
            off += r
        return out

    def pack_small(sc, rc, rgb, lng, lnb):
        return rows([sc, rc, rgb, lng, lnb], SMALL_ROWS)

    def pack_rep(bmv, rcbv, apv):
        return rows([bmv.reshape(DEPTH, 2, D), rcbv[:, None], apv[:, None]], 8)

    small = pack_small(sc_w, rc_w, rg_b, ln_g, ln_b)

    me1 = jnp.reshape(me, (1,)).astype(jnp.int32)

    def layer_shards(l):
        return gather_prep(l, me1, w_in, w_out_conv, w_out_rnn, w_o, ffn_w1, ffn_w2, rg_w, small)

    def as_weights(lands):
        return dict(zip(("win", "w3", "w1", "w2", "rgw", "small"), lands))

    xf = x.reshape(t, D)
    cur = (xf, xf.astype(BF16))
    gathers = [weight_gather(l) for l in range(DEPTH)]
    shards, own = layer_shards(0)
    flight = gathers[0].start(shards, own, [cur[1]])
    lands, (xf0,) = gathers[0].wait(flight[0], flight[1], flight[2], flight[3], [cur[0]])
    cur = (xf0, flight[4][0])
    gws, saved = [as_weights(lands)], []
    for l in range(DEPTH):
        gw = gws[l]
        if l + 1 < DEPTH:
            shards, own = layer_shards(l + 1)
            flight = gathers[l + 1].start(shards, own, [cur[1]])
            cur = (cur[0], flight[4][0])
        rcb, ap, bm = rc_b[l][None], a_param[l][None], b_merge[l][None]
        cur, s0 = _ffn_forward(cur[0], cur[1], gw, 0, 0)
        cur, s1 = _mixer_forward(cur[0], cur[1], gw, rcb, ap, bm)
        cur, s2 = _ffn_forward(cur[0], cur[1], gw, 1, 2)
        saved.append((s0, s1, s2))
        if l + 1 < DEPTH:
            lands, (xo,) = gathers[l + 1].wait(flight[0], flight[1], flight[2], flight[3], [cur[0]])
            cur = (xo, cur[1])
            gws.append(as_weights(lands))

    dy, loss_tile = loss_head(cur[0], loss_target.reshape(t, D))
    loss = lax.psum(loss_tile[0, 0], ("x", "y", "c"))

    lands = [None] * DEPTH
    gsmall = [None] * DEPTH
    scatters = [grad_scatter(l) for l in range(DEPTH)]
    flight = None
    for l in reversed(range(DEPTH)):
        gw = gws[l]
        rcb, ap, bm = rc_b[l][None], a_param[l][None], b_merge[l][None]
        s0, s1, s2 = saved[l]
        dy, dw1b, dw2b, dln2 = _ffn_backward(dy, s2, gw, 1, 2)
        dy, dmix, dln1, dbm, red_rec, red_conv = _mixer_backward(dy, s1, gw, rcb, ap, bm)
        dy, dw1a, dw2a, dln0 = _ffn_backward(dy, s0, gw, 0, 0)
        if flight is not None:
            lands[l + 1], (dy,) = scatters[l + 1].wait(flight[0], flight[1], flight[2], flight[3], [dy])
        grads = [dmix["dwin"], dmix["dwoc"], dmix["dwor"], dmix["dwo"], dw1a, dw1b, dw2a, dw2b, dmix["drgw"]]
        flight = scatters[l].start(grads, scatter_prep(l, me1, grads), [dy])
        dy = flight[4][0]
        gsmall[l] = rows([red_conv[0:7], red_rec[0:2], dln0[0:1], dln1[0:1], dln2[0:1], dln0[1:2], dln1[1:2],
                          dln2[1:2], dbm, red_rec[3:4], red_rec[2:3]], GRAD_ROWS)
    grad_x = dy.reshape(1, t, D)

    gfull = sum_small(gather_small_grads(jnp.stack(gsmall, axis=0)))

    def parts_of(ls):
        return dict(w_in=[ls[l][0] for l in range(len(ls))],
                    w3=[ls[l][1].reshape(N_DEV, 3 * 128, D) for l in range(len(ls))],
                    w1=[ls[l][2].reshape(N_DEV, 2 * D, FS) for l in range(len(ls))],
                    w2=[ls[l][3].reshape(N_DEV, 2 * W2S, D) for l in range(len(ls))],
                    rgw=[ls[l][4].reshape(N_DEV, 2 * HEADS * 32, HD) for l in range(len(ls))])

    families = [("w_in", w_in, m_w_in, v_w_in, "w_in", D, WIN_S, 0, 128),
                ("w_out_conv", w_out_conv, m_w_out_conv, v_w_out_conv, "w3", 128, D, 0, 128),
                ("w_out_rnn", w_out_rnn, m_w_out_rnn, v_w_out_rnn, "w3", 128, D, 1, 128),
                ("w_o", w_o, m_w_o, v_w_o, "w3", 128, D, 2, 128),
                ("ffn_w1", ffn_w1, m_ffn_w1, v_ffn_w1, "w1", 2 * D, FS, 0, 256),
                ("ffn_w2", ffn_w2, m_ffn_w2, v_ffn_w2, "w2", 2 * W2S, D, 0, 2 * W2S // 4),
                ("rg_w", rg_w, m_rg_w, v_rg_w, "rgw", 2 * HEADS * 32, HD, 0, 256)]

    def adam_pass(tag, parts, l0, prev):
        outs = {}
        for name, w, m, v, fam, nrow, lanes, blk_off, tr in families:
            r3 = lambda a: a.reshape(DEPTH, nrow, lanes)
            outs[name] = adam_big(f"adam_{name}_{tag}", r3(w), r3(m), r3(v), parts[fam], nrow, lanes, blk_off, tr, l0,
                                  None if prev is None else prev[name])
        return outs

    first = adam_pass("rest", parts_of(lands[1:]), 1, None)
    thru = [first[f[0]][0] for f in families] + [gfull]
    lands[0], thru = scatters[0].wait(flight[0], flight[1], flight[2], flight[3], thru)
    gfull = thru[-1]
    prev = {f[0]: [thru[i]] + list(first[f[0]][1:]) for i, f in enumerate(families)}
    final = adam_pass("first", parts_of(lands[0:1]), 0, prev)
    res = {f[0]: [o.reshape(f[1].shape) for o in final[f[0]]] for f in families}

    g_sh = lax.dynamic_slice(gfull, (0, 0, me * 128), (DEPTH, SMALL_ROWS, 128))
    g_rep = gfull[:, G_BM:G_BM + 8, :]
    d_sh, m_sh, v_sh = adam_small("adam_small_sharded", small, g_sh,
                                  pack_small(m_sc_w, m_rc_w, m_rg_b, m_ln_g, m_ln_b),
                                  pack_small(v_sc_w, v_rc_w, v_rg_b, v_ln_g, v_ln_b))
    d_rep, m_rep, v_rep = adam_small("adam_small_replicated", pack_rep(b_merge, rc_b, a_param), g_rep,
                                     pack_rep(m_b_merge, m_rc_b, m_a_param), pack_rep(v_b_merge, v_rc_b, v_a_param))

    def unpack_sh(a):
        return dict(sc_w=a[:, 0:3], rc_w=a[:, 3:7], rg_b=a[:, 7:9], ln_g=a[:, 9:12], ln_b=a[:, 12:15])

    def unpack_rep(a):
        return dict(b_merge=a[:, 0:2].reshape(DEPTH, 2 * D), rc_b=a[:, 2], a_param=a[:, 3])

    gsh, grep = unpack_sh(g_sh), unpack_rep(g_rep)
    dsh, drep = unpack_sh(d_sh), unpack_rep(d_rep)
    msh, mrep = unpack_sh(m_sh), unpack_rep(m_rep)
    vsh, vrep = unpack_sh(v_sh), unpack_rep(v_rep)
    for n in ("sc_w", "rc_w", "rg_b", "ln_g", "ln_b"):
        res[n] = [gsh[n], dsh[n], msh[n], vsh[n]]
    for n in ("b_merge", "rc_b", "a_param"):
        res[n] = [grep[n], drep[n], mrep[n], vrep[n]]

    names = ["w_in", "b_merge", "sc_w", "rc_w", "rc_b", "rg_w", "rg_b", "a_param", "w_out_conv", "w_out_rnn", "w_o",
             "ffn_w1", "ffn_w2", "ln_g", "ln_b"]
    out = [loss, grad_x]
    for k in range(4):
        out += [res[n][k] for n in names]
    return tuple(out)
```

```python
import functools

import jax
import jax.numpy as jnp
from jax import lax
from jax.experimental import pallas as pl
from jax.experimental.pallas import tpu as pltpu

F32 = jnp.float32
BF16 = jnp.bfloat16
SDS = jax.ShapeDtypeStruct

N_DEV = 8
DEPTH = 4
D = 1024
D_FF = 2816
FS = D_FF // 4
W2S = D_FF // 8
D_IN = 7 * D
WIN_S = D_IN // 8
HEADS = 4
HD = D // HEADS
LRU_C = 8.0
ALPHA = (2.0 * DEPTH) ** 0.25
LN_EPS = 1e-5
ADAM_LR, ADAM_B1, ADAM_B2, ADAM_EPS, ADAM_WD, ADAM_STEP = 0.001, 0.9, 0.999, 1e-08, 0.01, 10

R_SC, R_RC, R_RGB, R_LNG, R_LNB = 0, 3, 7, 9, 12
SMALL_ROWS = 16
G_BM, G_RCB, G_AP = 15, 17, 18
GRAD_ROWS = 24

NN = ((1,), (0,))
NT = ((1,), (1,))
TN = ((0,), (0,))
MESH = pl.DeviceIdType.MESH
ANY = pl.BlockSpec(memory_space=pl.ANY)
VMEM_LIMIT = 52 * 1024 * 1024


def _dot(a, b, dims):
    return lax.dot_general(a, b, (dims, ((), ())), preferred_element_type=F32)


def _cp(*sem):
    return pltpu.CompilerParams(dimension_semantics=sem, vmem_limit_bytes=VMEM_LIMIT)


def _sigmoid(x):
    return 1.0 / (1.0 + jnp.exp(-x))


def _log1p(e):
    u = 1.0 + e
    return jnp.where(u == 1.0, e, jnp.log(u) * e / jnp.where(u == 1.0, 1.0, u - 1.0))


def _softplus(x):
    return jnp.maximum(x, 0.0) + _log1p(jnp.exp(-jnp.abs(x)))


def _neg_expm1(x):
    u = jnp.exp(x)
    um1 = u - 1.0
    safe = jnp.logical_and(u != 1.0, um1 != -1.0)
    r = um1 * x / jnp.where(safe, jnp.log(jnp.where(safe, u, 0.5)), 1.0)
    return -jnp.where(u == 1.0, x, jnp.where(um1 == -1.0, -1.0, r))


def _gelu(y):
    c = 0.7978845608028654
    t = jnp.tanh(c * (y + 0.044715 * y * y * y))
    return 0.5 * y * (1.0 + t), t


def _gelu_grad(y, t):
    c = 0.7978845608028654
    return 0.5 * (1.0 + t) + 0.5 * y * (1.0 - t * t) * c * (1.0 + 3.0 * 0.044715 * y * y)


def _ln_fwd(z, g, b):
    mu = jnp.mean(z, axis=-1, keepdims=True)
    zc = z - mu
    var = jnp.mean(zc * zc, axis=-1, keepdims=True)
    rstd = lax.rsqrt(var + LN_EPS)
    xh = zc * rstd
    return xh * g + b, xh, rstd


def _ln_bwd(dy, xh, rstd, g):
    dxh = dy * g
    m1 = jnp.mean(dxh, axis=-1, keepdims=True)
    m2 = jnp.mean(dxh * xh, axis=-1, keepdims=True)
    return rstd * (dxh - m1 - xh * m2)


def _row_tile(t, want):
    return min(want, t)


def _me():
    return 4 * lax.axis_index("x") + 2 * lax.axis_index("y") + lax.axis_index("c")


def _coords(p):
    return (p // 4, (p // 2) % 2, p % 2)


def _all_to_all_copies(srcs_of, dsts_of, waits, send_sems, recv_sems, loc_sems):
    me = _me()
    n = len(waits)
    own_src, own_dst = srcs_of(me), dsts_of(me)
    local = [pltpu.make_async_copy(own_src[k], own_dst[k], loc_sems.at[k]) for k in range(n)]
    for cp in local:
        cp.start()
    for d in range(1, N_DEV):
        p = (me + d) % N_DEV
        src, dst = srcs_of(p), dsts_of(me)
        for k in range(n):
            pltpu.make_async_remote_copy(
                src_ref=src[k], dst_ref=dst[k], send_sem=send_sems.at[waits[k][1]],
                recv_sem=recv_sems.at[waits[k][1]], device_id=_coords(p), device_id_type=MESH).start()
    done = set()
    for k in range(n):
        ref, s = waits[k]
        if s in done:
            continue
        done.add(s)
        pltpu.make_async_remote_copy(
            src_ref=ref, dst_ref=ref, send_sem=send_sems.at[s], recv_sem=recv_sems.at[s],
            device_id=_coords(me), device_id_type=MESH).wait()
    for cp in local:
        cp.wait()


HBM = pl.BlockSpec(memory_space=pltpu.HBM)
SEM = pl.BlockSpec(memory_space=pltpu.SEMAPHORE)
EFFECT = pltpu.SideEffectType.DATAFLOW_SIDE_EFFECTING


def _in_hbm(a):
    return pltpu.with_memory_space_constraint(a, pltpu.HBM)


class Exchange:
    def __init__(self, name, src_of, dst_of, sem_of, span_of):
        self.name, self.src_of, self.dst_of, self.sem_of, self.span_of = name, src_of, dst_of, sem_of, span_of
        self.nsem = max(sem_of) + 1

    def start(self, srcs, lands, thru):
        n, m = len(srcs), len(lands)
        ops = list(srcs) + list(lands) + list(thru)

        def body(*refs):
            src_refs, land_refs = refs[:n], refs[n:n + m]
            send_sems, recv_sems = refs[len(ops)], refs[len(ops) + 1]
            me = _me()
            for dd in range(1, N_DEV):
                p = (me + dd) % N_DEV
                s, d = self.src_of(src_refs, p), self.dst_of(land_refs, me)
                for k in range(len(self.sem_of)):
                    pltpu.make_async_remote_copy(
                        src_ref=s[k], dst_ref=d[k], send_sem=send_sems.at[self.sem_of[k]],
                        recv_sem=recv_sems.at[self.sem_of[k]], device_id=_coords(p), device_id_type=MESH).start()

        sem = pltpu.SemaphoreType.DMA((self.nsem,))
        res = pl.pallas_call(
            body, name=self.name + "_start",
            out_shape=(sem, sem) + tuple(pltpu.HBM(a.shape, a.dtype) for a in ops),
            in_specs=[HBM] * len(ops), out_specs=(SEM, SEM) + (HBM,) * len(ops),
            input_output_aliases={i: 2 + i for i in range(len(ops))},
            compiler_params=pltpu.CompilerParams(has_side_effects=EFFECT),
        )(*[_in_hbm(a) for a in ops])
        return res[0], res[1], res[2:2 + n], res[2 + n:2 + n + m], list(res[2 + n + m:])

    def wait(self, send_sems, recv_sems, srcs, lands, thru):
        n, m = len(srcs), len(lands)
        ops = list(srcs) + list(lands) + list(thru)

        def body(*refs):
            land_refs = refs[n:n + m]
            ssem, rsem = refs[len(ops)], refs[len(ops) + 1]
            me = _me()
            spans = self.span_of(land_refs)
            for s in range(self.nsem):
                cp = pltpu.make_async_remote_copy(
                    src_ref=spans[s], dst_ref=spans[s], send_sem=ssem.at[s], recv_sem=rsem.at[s],
                    device_id=_coords(me), device_id_type=MESH)
                cp.wait_send()
                cp.wait_recv()

        res = pl.pallas_call(
            body, name=self.name + "_wait",
            out_shape=tuple(pltpu.HBM(a.shape, a.dtype) for a in ops),
            in_specs=[HBM] * len(ops) + [SEM, SEM], out_specs=(HBM,) * len(ops),
            input_output_aliases={i: i for i in range(len(ops))},
            compiler_params=pltpu.CompilerParams(has_side_effects=EFFECT),
        )(*ops, send_sems, recv_sems)
        return res[n:n + m], list(res[n + m:])


def weight_gather(l):
    def src_of(refs, p):
        return list(refs)

    def dst_of(lands, p):
        o_win, o_w3, o_w1, o_w2, o_rgw, o_sm = lands
        return [o_win.at[:, pl.ds(pl.multiple_of(p * WIN_S, 128), WIN_S)],
                o_w3.at[:, pl.ds(pl.multiple_of(p * 128, 128), 128), :],
                o_w1.at[:, p],
                o_w2.at[:, pl.ds(pl.multiple_of(p * W2S, 16), W2S), :],
                o_rgw.at[:, :, pl.ds(pl.multiple_of(p * 32, 32), 32), :],
                o_sm.at[:, pl.ds(pl.multiple_of(p * 128, 128), 128)]]

    def span_of(lands):
        o_win, o_w3, o_w1, o_w2, o_rgw, o_sm = lands
        return [o_win.at[:, pl.ds(0, 7 * WIN_S)], o_w3.at[:, pl.ds(0, 7 * 128), :], o_w1.at[:, pl.ds(0, 7)],
                o_w2.at[:, pl.ds(0, 7 * W2S), :], o_rgw.at[:, :, pl.ds(0, 7 * 32), :], o_sm.at[:, pl.ds(0, 7 * 128)]]

    return Exchange(f"gather{l}", src_of, dst_of, [0, 1, 2, 3, 4, 5], span_of)


GATHER_SHAPES = (SDS((D, D_IN), BF16), SDS((3, D, D), BF16), SDS((2, N_DEV, D, FS), BF16),
                 SDS((2, D_FF, D), BF16), SDS((2, HEADS, HD, HD), BF16), SDS((SMALL_ROWS, D), F32))


def grad_scatter(l):
    def src_of(refs, p):
        dwin, dwoc, dwor, dwo, dw1a, dw1b, dw2a, dw2b, drgw = refs
        rows3 = pl.ds(pl.multiple_of(p * 128, 128), 128)
        rows2 = pl.ds(pl.multiple_of(p * W2S, 16), W2S)
        return [dwin.at[:, pl.ds(pl.multiple_of(p * WIN_S, 128), WIN_S)],
                dwoc.at[rows3, :], dwor.at[rows3, :], dwo.at[rows3, :],
                dw1a.at[p], dw1b.at[p], dw2a.at[rows2, :], dw2b.at[rows2, :],
                drgw.at[:, :, pl.ds(pl.multiple_of(p * 32, 32), 32), :]]

    def dst_of(lands, p):
        l_win, l_w3, l_w1, l_w2, l_rgw = lands
        return [l_win.at[p], l_w3.at[p, 0], l_w3.at[p, 1], l_w3.at[p, 2], l_w1.at[p, 0], l_w1.at[p, 1],
                l_w2.at[p, 0], l_w2.at[p, 1], l_rgw.at[p]]

    def span_of(lands):
        return [a.at[pl.ds(0, 7)] for a in lands]

    return Exchange(f"scatter{l}", src_of, dst_of, [0, 1, 1, 1, 2, 2, 3, 3, 4], span_of)


SCATTER_SHAPES = (SDS((N_DEV, D, WIN_S), BF16), SDS((N_DEV, 3, 128, D), BF16), SDS((N_DEV, 2, D, FS), BF16),
                  SDS((N_DEV, 2, W2S, D), BF16), SDS((N_DEV, 2, HEADS, 32, HD), BF16))


def gather_prep(l, me1, w_in, w_out_conv, w_out_rnn, w_o, ffn_w1, ffn_w2, rg_w, small):
    hd, hw = D // 2, W2S // 2

    def body(me_ref, win, woc, wor, wo, w1, w2, rgw, sm, c_win, c_w3, c_w1, c_w2, c_rgw, c_sm,
             o_win, o_w3, o_w1, o_w2, o_rgw, o_sm):
        a = win[...].astype(BF16)
        c_win[...] = a
        o_win[...] = a
        for k, r in enumerate((woc, wor, wo)):
            b = r[...].astype(BF16)
            c_w3[k] = b
            o_w3[k] = b
        for src, comp, own in ((w1, c_w1, o_w1), (w2, c_w2, o_w2), (rgw, c_rgw, o_rgw)):
            b = src[...].astype(BF16)
            comp[...] = b
            own[...] = b
        c_sm[...] = sm[...]
        o_sm[...] = sm[...]

    three = pl.BlockSpec((None, 64, D), lambda i, me: (l, i, 0))
    in_specs = [pl.BlockSpec((None, hd, WIN_S), lambda i, me: (l, i, 0)), three, three, three,
                pl.BlockSpec((None, 2, hd, FS), lambda i, me: (l, 0, i, 0)),
                pl.BlockSpec((None, 2, hw, D), lambda i, me: (l, 0, i, 0)),
                pl.BlockSpec((None, 2, HEADS, 16, HD), lambda i, me: (l, 0, 0, i, 0)),
                pl.BlockSpec((None, 8, 128), lambda i, me: (l, i, 0))]
    out_specs = (pl.BlockSpec((hd, WIN_S), lambda i, me: (i, 0)), pl.BlockSpec((3, 64, D), lambda i, me: (0, i, 0)),
                 pl.BlockSpec((2, hd, FS), lambda i, me: (0, i, 0)), pl.BlockSpec((2, hw, D), lambda i, me: (0, i, 0)),
                 pl.BlockSpec((2, HEADS, 16, HD), lambda i, me: (0, 0, i, 0)), pl.BlockSpec((8, 128), lambda i, me: (i, 0)),
                 pl.BlockSpec((hd, WIN_S), lambda i, me: (i, me[0])),
                 pl.BlockSpec((3, 64, D), lambda i, me: (0, 2 * me[0] + i, 0)),
                 pl.BlockSpec((2, None, hd, FS), lambda i, me: (0, me[0], i, 0)),
                 pl.BlockSpec((2, hw, D), lambda i, me: (0, 2 * me[0] + i, 0)),
                 pl.BlockSpec((2, HEADS, 16, HD), lambda i, me: (0, 0, 2 * me[0] + i, 0)),
                 pl.BlockSpec((8, 128), lambda i, me: (i, me[0])))
    compact = (SDS((D, WIN_S), BF16), SDS((3, 128, D), BF16), SDS((2, D, FS), BF16), SDS((2, W2S, D), BF16),
               SDS((2, HEADS, 32, HD), BF16), SDS((SMALL_ROWS, 128), F32))
    res = pl.pallas_call(
        body, name=f"gather_prep{l}", out_shape=compact + GATHER_SHAPES,
        grid_spec=pltpu.PrefetchScalarGridSpec(num_scalar_prefetch=1, grid=(2,), in_specs=in_specs, out_specs=out_specs),
        compiler_params=_cp("arbitrary"),
    )(me1, w_in, w_out_conv, w_out_rnn, w_o, ffn_w1, ffn_w2, rg_w, small)
    return list(res[:6]), list(res[6:])


def scatter_prep(l, me1, grads):
    hd, hw = D // 2, W2S // 2

    def body(me_ref, dwin, dwoc, dwor, dwo, dw1a, dw1b, dw2a, dw2b, drgw, l_win, l_w3, l_w1, l_w2, l_rgw):
        l_win[...] = dwin[...]
        for k, r in enumerate((dwoc, dwor, dwo)):
            l_w3[k] = r[...]
        l_w1[0] = dw1a[...]
        l_w1[1] = dw1b[...]
        l_w2[0] = dw2a[...]
        l_w2[1] = dw2b[...]
        l_rgw[...] = drgw[...]

    three = pl.BlockSpec((64, D), lambda i, me: (2 * me[0] + i, 0))
    one = pl.BlockSpec((None, hd, FS), lambda i, me: (me[0], i, 0))
    two = pl.BlockSpec((hw, D), lambda i, me: (2 * me[0] + i, 0))
    in_specs = [pl.BlockSpec((hd, WIN_S), lambda i, me: (i, me[0])), three, three, three, one, one, two, two,
                pl.BlockSpec((2, HEADS, 16, HD), lambda i, me: (0, 0, 2 * me[0] + i, 0))]
    out_specs = (pl.BlockSpec((None, hd, WIN_S), lambda i, me: (me[0], i, 0)),
                 pl.BlockSpec((None, 3, 64, D), lambda i, me: (me[0], 0, i, 0)),
                 pl.BlockSpec((None, 2, hd, FS), lambda i, me: (me[0], 0, i, 0)),
                 pl.BlockSpec((None, 2, hw, D), lambda i, me: (me[0], 0, i, 0)),
                 pl.BlockSpec((None, 2, HEADS, 16, HD), lambda i, me: (me[0], 0, 0, i, 0)))
    return list(pl.pallas_call(
        body, name=f"scatter_prep{l}", out_shape=SCATTER_SHAPES,
        grid_spec=pltpu.PrefetchScalarGridSpec(num_scalar_prefetch=1, grid=(2,), in_specs=in_specs, out_specs=out_specs),
        compiler_params=_cp("arbitrary"),
    )(me1, *grads))


def gather_small_grads(gsmall):
    shp = gsmall.shape

    def body(g, land, send_sems, recv_sems, loc_sems):
        _all_to_all_copies(lambda p: [g], lambda p: [land.at[p]], [(land.at[pl.ds(0, 7)], 0)],
                           send_sems, recv_sems, loc_sems)

    return pl.pallas_call(
        body, name="gather_small_grads", out_shape=SDS((N_DEV,) + shp, F32),
        in_specs=[ANY], out_specs=ANY,
        scratch_shapes=[pltpu.SemaphoreType.DMA((1,)), pltpu.SemaphoreType.DMA((1,)), pltpu.SemaphoreType.DMA((1,))],
        compiler_params=pltpu.CompilerParams(has_side_effects=True),
    )(gsmall)


def ffn_up(xb, w1, f):
    t = xb.shape[0]
    tm = _row_tile(t, 512)

    def body(x_ref, wg_ref, wu_ref, g_ref, u_ref, a_ref):
        x = x_ref[...]
        g = _dot(x, wg_ref[...], NN)
        u = _dot(x, wu_ref[...], NN)
        g_ref[...] = g.astype(BF16)
        u_ref[...] = u.astype(BF16)
        a_ref[...] = (g * _sigmoid(g) * u).astype(BF16)

    out = pl.BlockSpec((None, tm, FS), lambda j, i: (j, i, 0))
    return pl.pallas_call(
        body, name=f"ffn_up{f}", grid=(4, t // tm),
        in_specs=[pl.BlockSpec((tm, D), lambda j, i: (i, 0)),
                  pl.BlockSpec((None, None, D, FS), lambda j, i: (f, j, 0, 0)),
                  pl.BlockSpec((None, None, D, FS), lambda j, i: (f, j + 4, 0, 0))],
        out_specs=(out, out, out), out_shape=(SDS((4, t, FS), BF16),) * 3,
        compiler_params=_cp("parallel", "parallel"),
    )(xb, w1, w1)


def ffn_down_ln(a, w2, f, xf, small, s):
    t = xf.shape[0]
    tm = _row_tile(t, 256)

    def body(a_ref, w_ref, x_ref, sm_ref, xo_ref, xb_ref, xh_ref, rs_ref):
        acc = _dot(a_ref[0], w_ref[0:FS, :], NN)
        for j in range(1, 4):
            acc = acc + _dot(a_ref[j], w_ref[j * FS:(j + 1) * FS, :], NN)
        z = ALPHA * x_ref[...] + 0.5 * acc
        y, xh, rstd = _ln_fwd(z, sm_ref[R_LNG + s:R_LNG + s + 1, :], sm_ref[R_LNB + s:R_LNB + s + 1, :])
        xo_ref[...] = y
        xb_ref[...] = y.astype(BF16)
        xh_ref[...] = xh
        rs_ref[...] = rstd

    row = pl.BlockSpec((tm, D), lambda i: (i, 0))
    return pl.pallas_call(
        body, name=f"ffn_down_ln{f}", grid=(t // tm,),
        in_specs=[pl.BlockSpec((4, tm, FS), lambda i: (0, i, 0)),
                  pl.BlockSpec((None, D_FF, D), lambda i: (f, 0, 0)),
                  row, pl.BlockSpec((SMALL_ROWS, D), lambda i: (0, 0))],
        out_specs=(row, row, row, pl.BlockSpec((tm, 1), lambda i: (i, 0))),
        out_shape=(SDS((t, D), F32), SDS((t, D), BF16), SDS((t, D), F32), SDS((t, 1), F32)),
        compiler_params=_cp("parallel"),
    )(a, w2, xf, small)


def ffn_bwd_gates(dxn, xh, rstd, small, s, w2, f, g, u):
    t = dxn.shape[0]
    tm = _row_tile(t, 256)

    def body(dy_ref, xh_ref, rs_ref, sm_ref, w_ref, g_ref, u_ref, dz_ref, df_ref, dgu_ref, dln_ref):
        i = pl.program_id(0)
        dy = dy_ref[...]
        xhat = xh_ref[...]
        dz = _ln_bwd(dy, xhat, rs_ref[...], sm_ref[R_LNG + s:R_LNG + s + 1, :])

        @pl.when(i == 0)
        def _():
            dln_ref[...] = jnp.zeros_like(dln_ref)

        dln_ref[0:1, :] += jnp.sum(dy * xhat, axis=0, keepdims=True)
        dln_ref[1:2, :] += jnp.sum(dy, axis=0, keepdims=True)
        dz_ref[...] = dz
        df = (0.5 * dz).astype(BF16)
        df_ref[...] = df
        for j in range(4):
            da = _dot(df, w_ref[j * FS:(j + 1) * FS, :], NT)
            gg = g_ref[j].astype(F32)
            uu = u_ref[j].astype(F32)
            sg = _sigmoid(gg)
            dgu_ref[j] = (da * uu * (sg * (1.0 + gg * (1.0 - sg)))).astype(BF16)
            dgu_ref[j + 4] = (da * (gg * sg)).astype(BF16)

    row = pl.BlockSpec((tm, D), lambda i: (i, 0))
    gu = pl.BlockSpec((4, tm, FS), lambda i: (0, i, 0))
    return pl.pallas_call(
        body, name=f"ffn_bwd_gates{f}", grid=(t // tm,),
        in_specs=[row, row, pl.BlockSpec((tm, 1), lambda i: (i, 0)), pl.BlockSpec((SMALL_ROWS, D), lambda i: (0, 0)),
                  pl.BlockSpec((None, D_FF, D), lambda i: (f, 0, 0)), gu, gu],
        out_specs=(row, row, pl.BlockSpec((8, tm, FS), lambda i: (0, i, 0)), pl.BlockSpec((2, D), lambda i: (0, 0))),
        out_shape=(SDS((t, D), F32), SDS((t, D), BF16), SDS((8, t, FS), BF16), SDS((2, D), F32)),
        compiler_params=_cp("arbitrary"),
    )(dxn, xh, rstd, small, w2, g, u)


def ffn_bwd_dx(dz, dgu, w1, f):
    t = dz.shape[0]
    tm = _row_tile(t, 1024)

    def body(dz_ref, d_ref, w_ref, dx_ref, acc):
        k = pl.program_id(1)

        @pl.when(k == 0)
        def _():
            acc[...] = ALPHA * dz_ref[...]

        acc[...] += _dot(d_ref[...], w_ref[...], NT)

        @pl.when(k == 7)
        def _():
            dx_ref[...] = acc[...]

    row = pl.BlockSpec((tm, D), lambda i, k: (i, 0))
    return pl.pallas_call(
        body, name=f"ffn_bwd_dx{f}", grid=(t // tm, 8),
        in_specs=[row, pl.BlockSpec((None, tm, FS), lambda i, k: (k, i, 0)),
                  pl.BlockSpec((None, None, D, FS), lambda i, k: (f, k, 0, 0))],
        out_specs=row, out_shape=SDS((t, D), F32),
        scratch_shapes=[pltpu.VMEM((tm, D), F32)],
        compiler_params=_cp("parallel", "arbitrary"),
    )(dz, dgu, w1)


def _mm_tn(name, a, a_spec, b, b_spec, out_sds, out_spec, grid):
    def body(a_ref, b_ref, o_ref):
        o_ref[...] = _dot(a_ref[...], b_ref[...], TN).astype(o_ref.dtype)

    return pl.pallas_call(
        body, name=name, grid=grid, in_specs=[a_spec, b_spec], out_specs=out_spec, out_shape=out_sds,
        compiler_params=_cp(*(["parallel"] * len(grid))),
    )(a, b)


def ffn_dw1(xb, dgu, f):
    t = xb.shape[0]
    return _mm_tn(f"ffn_dw1_{f}", xb, pl.BlockSpec((t, D), lambda j: (0, 0)),
                  dgu, pl.BlockSpec((None, t, FS), lambda j: (j, 0, 0)),
                  SDS((8, D, FS), BF16), pl.BlockSpec((None, D, FS), lambda j: (j, 0, 0)), (8,))


def ffn_dw2(a, df, f):
    t = df.shape[0]
    return _mm_tn(f"ffn_dw2_{f}", a, pl.BlockSpec((None, t, FS), lambda j: (j, 0, 0)),
                  df, pl.BlockSpec((t, D), lambda j: (0, 0)),
                  SDS((D_FF, D), BF16), pl.BlockSpec((FS, D), lambda j: (j, 0)), (4,))


def mm_tn_square(name, a, b):
    t = a.shape[0]
    return _mm_tn(name, a, pl.BlockSpec((t, 512), lambda i: (0, i)),
                  b, pl.BlockSpec((t, D), lambda i: (0, 0)),
                  SDS((D, D), BF16), pl.BlockSpec((512, D), lambda i: (i, 0)), (2,))


def mix_proj(xb, win):
    t = xb.shape[0]
    tm = _row_tile(t, 512)

    def body(x_ref, w_ref, o_ref):
        o_ref[...] = _dot(x_ref[...], w_ref[...], NN).astype(BF16)

    return pl.pallas_call(
        body, name="mix_proj", grid=(7, t // tm),
        in_specs=[pl.BlockSpec((tm, D), lambda n, i: (i, 0)), pl.BlockSpec((D, D), lambda n, i: (0, n))],
        out_specs=pl.BlockSpec((tm, D), lambda n, i: (i, n)), out_shape=SDS((t, D_IN), BF16),
        compiler_params=_cp("parallel", "parallel"),
    )(xb, win)


def _pcol(tm, k):
    return pl.BlockSpec((tm, D), lambda i: (i, k))


def _prev_halo(tm, k):
    return pl.BlockSpec((8, D), lambda i: (jnp.maximum(i * (tm // 8) - 1, 0), k))


def _prev_halo16(tm, k):
    return pl.BlockSpec((16, D), lambda i: (jnp.maximum(i * (tm // 16) - 1, 0), k))


def _next_halo(tm, t, k):
    return pl.BlockSpec((8, D), lambda i: (jnp.minimum((i + 1) * (tm // 8), t // 8 - 1), k))


def _full(shape):
    nd = len(shape)
    return pl.BlockSpec(shape, lambda i: (0,) * nd)


def mix_pre(p, small, rcb, ap, rgw):
    t = p.shape[0]
    tm = _row_tile(t, 256)

    def body(bg_ref, cg_ref, v_ref, xr_ref, cgh_ref, vh_ref, xrh_ref, sm_ref, rcb_ref, ap_ref, rgw_ref,
             ca_ref, pa_ref, xc_ref, xcb_ref, gi_ref, gr_ref, h_ref, ext1, ext2, a_s, b_s, carry):
        i = pl.program_id(0)
        first = i == 0
        cv = cg_ref[...].astype(F32) * v_ref[...].astype(F32)
        ext1[0:16, :] = jnp.where(first, 0.0, cgh_ref[...].astype(F32) * vh_ref[...].astype(F32))
        ext1[16:, :] = cv
        xr = xr_ref[...].astype(F32)
        ext2[0:16, :] = jnp.where(first, 0.0, xrh_ref[...].astype(F32))
        ext2[16:, :] = xr
        ca = (sm_ref[R_SC:R_SC + 1, :] * ext1[pl.ds(14, tm), :] + sm_ref[R_SC + 1:R_SC + 2, :] * ext1[pl.ds(15, tm), :]
              + sm_ref[R_SC + 2:R_SC + 3, :] * cv)
        ca_ref[...] = ca
        pa_ref[...] = (bg_ref[...].astype(F32) * ca).astype(BF16)
        xc = (sm_ref[R_RC:R_RC + 1, :] * ext2[pl.ds(13, tm), :] + sm_ref[R_RC + 1:R_RC + 2, :] * ext2[pl.ds(14, tm), :]
              + sm_ref[R_RC + 2:R_RC + 3, :] * ext2[pl.ds(15, tm), :] + sm_ref[R_RC + 3:R_RC + 4, :] * xr
              + rcb_ref[...])
        xc_ref[...] = xc
        xcb = xc.astype(BF16)
        xcb_ref[...] = xcb
        g0, g1 = [], []
        for h in range(HEADS):
            xh = xcb[:, h * HD:(h + 1) * HD]
            g0.append(_dot(xh, rgw_ref[0, h], NN))
            g1.append(_dot(xh, rgw_ref[1, h], NN))
        gi = _sigmoid(jnp.concatenate(g0, axis=1) + sm_ref[R_RGB:R_RGB + 1, :])
        gr = _sigmoid(jnp.concatenate(g1, axis=1) + sm_ref[R_RGB + 1:R_RGB + 2, :])
        gi_ref[...] = gi
        gr_ref[...] = gr
        la = (-LRU_C) * gr * _softplus(-ap_ref[...])
        a_s[...] = jnp.exp(la)
        row = lax.broadcasted_iota(jnp.int32, (tm, D), 0) + i * tm
        mult = jnp.where(row == 0, 1.0, jnp.sqrt(_neg_expm1(2.0 * la)))
        b_s[...] = xc * gi * mult

        @pl.when(first)
        def _():
            carry[...] = jnp.zeros_like(carry)

        carry[...] = _scan_tile(a_s, b_s, h_ref, carry[...], tm, reverse=False)

    row = pl.BlockSpec((tm, D), lambda i: (i, 0))
    f32o, b16o = SDS((t, D), F32), SDS((t, D), BF16)
    ext, tile = pltpu.VMEM((tm + 16, D), F32), pltpu.VMEM((tm, D), F32)
    return pl.pallas_call(
        body, name="mix_pre", grid=(t // tm,),
        in_specs=[_pcol(tm, 0), _pcol(tm, 1), _pcol(tm, 2), _pcol(tm, 3),
                  _prev_halo16(tm, 1), _prev_halo16(tm, 2), _prev_halo16(tm, 3),
                  _full((SMALL_ROWS, D)), _full((1, D)), _full((1, D)), _full((2, HEADS, HD, HD))],
        out_specs=(row,) * 7, out_shape=(f32o, b16o, f32o, b16o, f32o, f32o, f32o),
        scratch_shapes=[ext, ext, tile, tile, pltpu.VMEM((8, D), F32)],
        compiler_params=_cp("arbitrary"),
    )(p, p, p, p, p, p, p, small, rcb, ap, rgw)


def _scan_tile(a_ref, b_ref, o_ref, carry, tm, reverse):
    width = a_ref.shape[1]
    ng = tm // 8
    row8 = lax.broadcasted_iota(jnp.int32, (8, width), 0)

    def step(g, c):
        r = pl.multiple_of((ng - 1 - g if reverse else g) * 8, 8)
        aa = a_ref[pl.ds(r, 8), :]
        bb = b_ref[pl.ds(r, 8), :]
        for s in (1, 2, 4):
            if reverse:
                keep, shift = row8 < 8 - s, 8 - s
            else:
                keep, shift = row8 >= s, s
            a_sh = jnp.where(keep, pltpu.roll(aa, shift, 0), 1.0)
            b_sh = jnp.where(keep, pltpu.roll(bb, shift, 0), 0.0)
            bb = aa * b_sh + bb
            aa = aa * a_sh
        o = aa * c + bb
        o_ref[pl.ds(r, 8), :] = o
        edge = o[0:1, :] if reverse else o[7:8, :]
        return jnp.broadcast_to(edge, (8, width))

    return lax.fori_loop(0, ng, step, carry)


def mix_out(pa, h, p, bm, w3, xf, small):
    t = xf.shape[0]
    tm = _row_tile(t, 256)

    def body(pa_ref, h_ref, yr_ref, gla_ref, glb_ref, bma_ref, bmb_ref, w_ref, x_ref, sm_ref,
             pb_ref, ya_ref, yb_ref, m_ref, xo_ref, xb_ref, xh_ref, rs_ref):
        ge, _ = _gelu(yr_ref[...].astype(F32))
        pb = (h_ref[...] * ge).astype(BF16)
        pb_ref[...] = pb
        ya = _dot(pa_ref[...], w_ref[0], NN)
        yb = _dot(pb, w_ref[1], NN)
        ya_ref[...] = ya
        yb_ref[...] = yb
        ga = _sigmoid(gla_ref[...].astype(F32) + bma_ref[...])
        gb = _sigmoid(glb_ref[...].astype(F32) + bmb_ref[...])
        m = (ga * ya + gb * yb).astype(BF16)
        m_ref[...] = m
        z = ALPHA * x_ref[...] + _dot(m, w_ref[2], NN)
        y, xh, rstd = _ln_fwd(z, sm_ref[R_LNG + 1:R_LNG + 2, :], sm_ref[R_LNB + 1:R_LNB + 2, :])
        xo_ref[...] = y
        xb_ref[...] = y.astype(BF16)
        xh_ref[...] = xh
        rs_ref[...] = rstd

    row = pl.BlockSpec((tm, D), lambda i: (i, 0))
    f32o, b16o = SDS((t, D), F32), SDS((t, D), BF16)
    return pl.pallas_call(
        body, name="mix_out", grid=(t // tm,),
        in_specs=[row, row, _pcol(tm, 4), _pcol(tm, 5), _pcol(tm, 6),
                  pl.BlockSpec((1, D), lambda i: (0, 0)), pl.BlockSpec((1, D), lambda i: (0, 1)),
                  _full((3, D, D)), row, _full((SMALL_ROWS, D))],
        out_specs=(row,) * 7 + (pl.BlockSpec((tm, 1), lambda i: (i, 0)),),
        out_shape=(b16o, f32o, f32o, b16o, f32o, b16o, f32o, SDS((t, 1), F32)),
        compiler_params=_cp("parallel"),
    )(pa, h, p, p, p, bm, bm, w3, xf, small)


def mixb_head(dxn, xh, rstd, small, w3, p, bm, ya, yb, ca, h):
    t = dxn.shape[0]
    tm = _row_tile(t, 256)

    def body(dy_ref, xh_ref, rs_ref, sm_ref, w_ref, bg_ref, yr_ref, gla_ref, glb_ref, bma_ref, bmb_ref,
             ya_ref, yb_ref, ca_ref, h_ref,
             dz_ref, dzb_ref, dya_ref, dyb_ref, dbg_ref, dca_ref, dh_ref, dphi_ref, dln_ref, dbm_ref):
        i = pl.program_id(0)
        dy = dy_ref[...]
        xhat = xh_ref[...]
        dz = _ln_bwd(dy, xhat, rs_ref[...], sm_ref[R_LNG + 1:R_LNG + 2, :])

        @pl.when(i == 0)
        def _():
            dln_ref[...] = jnp.zeros_like(dln_ref)
            dbm_ref[...] = jnp.zeros_like(dbm_ref)

        dln_ref[0:1, :] += jnp.sum(dy * xhat, axis=0, keepdims=True)
        dln_ref[1:2, :] += jnp.sum(dy, axis=0, keepdims=True)
        dz_ref[...] = dz
        dzb = dz.astype(BF16)
        dzb_ref[...] = dzb
        dm = _dot(dzb, w_ref[2], NT)
        ga = _sigmoid(gla_ref[...].astype(F32) + bma_ref[...])
        gb = _sigmoid(glb_ref[...].astype(F32) + bmb_ref[...])
        dya = (dm * ga).astype(BF16)
        dyb = (dm * gb).astype(BF16)
        dya_ref[...] = dya
        dyb_ref[...] = dyb
        dgla = dm * ya_ref[...] * ga * (1.0 - ga)
        dglb = dm * yb_ref[...] * gb * (1.0 - gb)
        dbm_ref[0:1, :] += jnp.sum(dgla, axis=0, keepdims=True)
        dbm_ref[1:2, :] += jnp.sum(dglb, axis=0, keepdims=True)
        dphi_ref[:, D:2 * D] = dgla.astype(BF16)
        dphi_ref[:, 2 * D:3 * D] = dglb.astype(BF16)
        dpa = _dot(dya, w_ref[0], NT)
        dpb = _dot(dyb, w_ref[1], NT)
        dbg_ref[...] = (dpa * ca_ref[...]).astype(BF16)
        dca_ref[...] = dpa * bg_ref[...].astype(F32)
        yr = yr_ref[...].astype(F32)
        ge, th = _gelu(yr)
        dh_ref[...] = dpb * ge
        dphi_ref[:, 0:D] = (dpb * h_ref[...] * _gelu_grad(yr, th)).astype(BF16)

    row = pl.BlockSpec((tm, D), lambda i: (i, 0))
    f32o, b16o = SDS((t, D), F32), SDS((t, D), BF16)
    acc2 = pl.BlockSpec((2, D), lambda i: (0, 0))
    return pl.pallas_call(
        body, name="mixb_head", grid=(t // tm,),
        in_specs=[row, row, pl.BlockSpec((tm, 1), lambda i: (i, 0)), _full((SMALL_ROWS, D)), _full((3, D, D)),
                  _pcol(tm, 0), _pcol(tm, 4), _pcol(tm, 5), _pcol(tm, 6),
                  pl.BlockSpec((1, D), lambda i: (0, 0)), pl.BlockSpec((1, D), lambda i: (0, 1)),
                  row, row, row, row],
        out_specs=(row,) * 7 + (pl.BlockSpec((tm, 3 * D), lambda i: (i, 0)), acc2, acc2),
        out_shape=(f32o, b16o, b16o, b16o, b16o, f32o, f32o, SDS((t, 3 * D), BF16), SDS((2, D), F32), SDS((2, D), F32)),
        compiler_params=_cp("arbitrary"),
    )(dxn, xh, rstd, small, w3, p, p, p, p, bm, bm, ya, yb, ca, h)


def mixb_rec(dh, gr, gi, h, xc, ap, rgw):
    t = dh.shape[0]
    tm = _row_tile(t, 256)
    nt = t // tm

    def body(dh_ref, gr_ref, gi_ref, h_ref, hh_ref, xc_ref, ap_ref, rgw_ref, dg_ref, dxc_ref, red_ref,
             ext, ext_a, c_s, lam_s, lam_c, a_c):
        i = nt - 1 - pl.program_id(0)
        first = i == 0

        @pl.when(pl.program_id(0) == 0)
        def _():
            red_ref[...] = jnp.zeros_like(red_ref)
            lam_c[...] = jnp.zeros_like(lam_c)
            a_c[...] = jnp.zeros_like(a_c)

        ext[0:8, :] = jnp.where(first, 0.0, hh_ref[...])
        ext[8:, :] = h_ref[...]
        hprev = ext[pl.ds(7, tm), :]
        gr = gr_ref[...]
        gi = gi_ref[...]
        xc = xc_ref[...]
        ap = ap_ref[...]
        sp = _softplus(-ap)
        la = (-LRU_C) * gr * sp
        a = jnp.exp(la)
        ext_a[0:tm, :] = a
        ext_a[tm:, :] = a_c[...]
        c_s[...] = ext_a[pl.ds(1, tm), :]
        lam_c[...] = _scan_tile(c_s, dh_ref, lam_s, lam_c[...], tm, reverse=True)
        a_c[...] = jnp.broadcast_to(a[0:1, :], (8, D))
        lam = lam_s[...]
        row = lax.broadcasted_iota(jnp.int32, (tm, D), 0) + i * tm
        start = row == 0
        mult = jnp.where(start, 1.0, jnp.sqrt(_neg_expm1(2.0 * la)))
        dmult = jnp.where(start, 0.0, lam * xc * gi)
        dla = lam * hprev * a - dmult * a * a / mult
        dg1 = (-LRU_C) * sp * dla * gr * (1.0 - gr)
        dg0 = lam * xc * mult * gi * (1.0 - gi)
        dsp = jnp.sum((-LRU_C) * gr * dla, axis=0, keepdims=True)
        red_ref[0:1, :] += jnp.sum(dg0, axis=0, keepdims=True)
        red_ref[1:2, :] += jnp.sum(dg1, axis=0, keepdims=True)
        red_ref[2:3, :] += -dsp * _sigmoid(-ap)
        dg0b = dg0.astype(BF16)
        dg1b = dg1.astype(BF16)
        dg_ref[0] = dg0b
        dg_ref[1] = dg1b
        parts = []
        for hd in range(HEADS):
            sl = slice(hd * HD, (hd + 1) * HD)
            parts.append(_dot(dg0b[:, sl], rgw_ref[0, hd], NT) + _dot(dg1b[:, sl], rgw_ref[1, hd], NT))
        dxc = lam * gi * mult + jnp.concatenate(parts, axis=1)
        dxc_ref[...] = dxc
        red_ref[3:4, :] += jnp.sum(dxc, axis=0, keepdims=True)

    row = pl.BlockSpec((tm, D), lambda i: (nt - 1 - i, 0))
    halo = pl.BlockSpec((8, D), lambda i: (jnp.maximum((nt - 1 - i) * (tm // 8) - 1, 0), 0))
    ext, tile, edge = pltpu.VMEM((tm + 8, D), F32), pltpu.VMEM((tm, D), F32), pltpu.VMEM((8, D), F32)
    return pl.pallas_call(
        body, name="mixb_rec", grid=(nt,),
        in_specs=[row, row, row, row, halo, row, _full((1, D)), _full((2, HEADS, HD, HD))],
        out_specs=(pl.BlockSpec((2, tm, D), lambda i: (0, nt - 1 - i, 0)), row, pl.BlockSpec((8, D), lambda i: (0, 0))),
        out_shape=(SDS((2, t, D), BF16), SDS((t, D), F32), SDS((8, D), F32)),
        scratch_shapes=[ext, ext, tile, tile, edge, edge],
        compiler_params=_cp("arbitrary"),
    )(dh, gr, gi, h, h, xc, ap, rgw)


def mixb_conv(dca, dxc, dbg, p, small):
    t = dca.shape[0]
    tm = _row_tile(t, 256)
    nt = t // tm

    def body(dca_ref, dcan_ref, dxc_ref, dxcn_ref, dbg_ref, cg_ref, v_ref, xr_ref, cgh_ref, vh_ref, xrh_ref, sm_ref,
             dplo_ref, dxr_ref, red_ref, e_dca, e_dxc, e_cv, e_xr):
        i = pl.program_id(0)
        first = i == 0
        last = i == nt - 1

        @pl.when(first)
        def _():
            red_ref[...] = jnp.zeros_like(red_ref)

        dca = dca_ref[...]
        dxc = dxc_ref[...]
        e_dca[0:tm, :] = dca
        e_dca[tm:, :] = jnp.where(last, 0.0, dcan_ref[...])
        e_dxc[0:tm, :] = dxc
        e_dxc[tm:, :] = jnp.where(last, 0.0, dxcn_ref[...])
        cg = cg_ref[...].astype(F32)
        v = v_ref[...].astype(F32)
        xr = xr_ref[...].astype(F32)
        e_cv[0:8, :] = jnp.where(first, 0.0, (cgh_ref[...].astype(F32) * vh_ref[...].astype(F32))[8:16, :])
        e_cv[8:, :] = cg * v
        e_xr[0:8, :] = jnp.where(first, 0.0, xrh_ref[...].astype(F32)[8:16, :])
        e_xr[8:, :] = xr
        dcv = (sm_ref[R_SC + 2:R_SC + 3, :] * dca + sm_ref[R_SC + 1:R_SC + 2, :] * e_dca[pl.ds(1, tm), :]
               + sm_ref[R_SC:R_SC + 1, :] * e_dca[pl.ds(2, tm), :])
        dplo_ref[:, 0:D] = dbg_ref[...]
        dplo_ref[:, D:2 * D] = (dcv * v).astype(BF16)
        dplo_ref[:, 2 * D:3 * D] = (dcv * cg).astype(BF16)
        dxr = (sm_ref[R_RC + 3:R_RC + 4, :] * dxc + sm_ref[R_RC + 2:R_RC + 3, :] * e_dxc[pl.ds(1, tm), :]
               + sm_ref[R_RC + 1:R_RC + 2, :] * e_dxc[pl.ds(2, tm), :] + sm_ref[R_RC:R_RC + 1, :] * e_dxc[pl.ds(3, tm), :])
        dxr_ref[...] = dxr.astype(BF16)
        for k in range(3):
            red_ref[R_SC + k:R_SC + k + 1, :] += jnp.sum(dca * e_cv[pl.ds(6 + k, tm), :], axis=0, keepdims=True)
        for k in range(4):
            red_ref[R_RC + k:R_RC + k + 1, :] += jnp.sum(dxc * e_xr[pl.ds(5 + k, tm), :], axis=0, keepdims=True)

    row = pl.BlockSpec((tm, D), lambda i: (i, 0))
    ext = pltpu.VMEM((tm + 8, D), F32)
    return pl.pallas_call(
        body, name="mixb_conv", grid=(nt,),
        in_specs=[row, _next_halo(tm, t, 0), row, _next_halo(tm, t, 0), row,
                  _pcol(tm, 1), _pcol(tm, 2), _pcol(tm, 3), _prev_halo16(tm, 1), _prev_halo16(tm, 2), _prev_halo16(tm, 3),
                  _full((SMALL_ROWS, D))],
        out_specs=(pl.BlockSpec((tm, 3 * D), lambda i: (i, 0)), row, pl.BlockSpec((8, D), lambda i: (0, 0))),
        out_shape=(SDS((t, 3 * D), BF16), SDS((t, D), BF16), SDS((8, D), F32)),
        scratch_shapes=[ext, ext, ext, ext],
        compiler_params=_cp("arbitrary"),
    )(dca, dca, dxc, dxc, dbg, p, p, p, p, p, p, small)


def mixb_dx(dz, dplo, dxr, dphi, win):
    t = dz.shape[0]
    tm = _row_tile(t, 512)

    def body(dz_ref, lo_ref, xr_ref, hi_ref, w_ref, dx_ref, acc):
        k = pl.program_id(1)

        @pl.when(k == 0)
        def _():
            acc[...] = ALPHA * dz_ref[...]

        @pl.when(k < 3)
        def _():
            acc[...] += _dot(lo_ref[...], w_ref[...], NT)

        @pl.when(k == 3)
        def _():
            acc[...] += _dot(xr_ref[...], w_ref[...], NT)

        @pl.when(k > 3)
        def _():
            acc[...] += _dot(hi_ref[...], w_ref[...], NT)

        @pl.when(k == 6)
        def _():
            dx_ref[...] = acc[...]

    row = pl.BlockSpec((tm, D), lambda i, k: (i, 0))
    return pl.pallas_call(
        body, name="mixb_dx", grid=(t // tm, 7),
        in_specs=[row, pl.BlockSpec((tm, D), lambda i, k: (i, jnp.minimum(k, 2))), row,
                  pl.BlockSpec((tm, D), lambda i, k: (i, jnp.clip(k - 4, 0, 2))),
                  pl.BlockSpec((D, D), lambda i, k: (0, k))],
        out_specs=row, out_shape=SDS((t, D), F32),
        scratch_shapes=[pltpu.VMEM((tm, D), F32)],
        compiler_params=_cp("parallel", "arbitrary"),
    )(dz, dplo, dxr, dphi, win)


def mixb_dwin(xb, dplo, dxr, dphi):
    t = xb.shape[0]
    tk = _row_tile(t, 2048)
    nk = t // tk

    def body(x_ref, lo_ref, xr_ref, hi_ref, o_ref, acc):
        n = pl.program_id(0)
        k = pl.program_id(1)

        @pl.when(k == 0)
        def _():
            acc[...] = jnp.zeros_like(acc)

        @pl.when(n < 3)
        def _():
            acc[...] += _dot(x_ref[...], lo_ref[...], TN)

        @pl.when(n == 3)
        def _():
            acc[...] += _dot(x_ref[...], xr_ref[...], TN)

        @pl.when(n > 3)
        def _():
            acc[...] += _dot(x_ref[...], hi_ref[...], TN)

        @pl.when(k == nk - 1)
        def _():
            o_ref[...] = acc[...].astype(BF16)

    return pl.pallas_call(
        body, name="mixb_dwin", grid=(7, nk),
        in_specs=[pl.BlockSpec((tk, D), lambda n, k: (k, 0)),
                  pl.BlockSpec((tk, D), lambda n, k: (jnp.where(n < 3, k, 0), jnp.minimum(n, 2))),
                  pl.BlockSpec((tk, D), lambda n, k: (jnp.where(n == 3, k, 0), 0)),
                  pl.BlockSpec((tk, D), lambda n, k: (jnp.where(n > 3, k, 0), jnp.clip(n - 4, 0, 2)))],
        out_specs=pl.BlockSpec((D, D), lambda n, k: (0, n)), out_shape=SDS((D, D_IN), BF16),
        scratch_shapes=[pltpu.VMEM((D, D), F32)],
        compiler_params=_cp("parallel", "arbitrary"),
    )(xb, dplo, dxr, dphi)


def mixb_drgw(xcb, dg):
    t = xcb.shape[0]
    return _mm_tn("mixb_drgw", xcb, pl.BlockSpec((t, HD), lambda g, h: (0, h)),
                  dg, pl.BlockSpec((None, t, HD), lambda g, h: (g, 0, h)),
                  SDS((2, HEADS, HD, HD), BF16), pl.BlockSpec((None, None, HD, HD), lambda g, h: (g, h, 0, 0)),
                  (2, HEADS))


def loss_head(y, tgt):
    t = y.shape[0]
    tm = _row_tile(t, 512)

    def body(y_ref, t_ref, dy_ref, l_ref):
        i = pl.program_id(0)
        e = y_ref[...] - t_ref[...]
        dy_ref[...] = e * (1.0 / D)

        @pl.when(i == 0)
        def _():
            l_ref[...] = jnp.zeros_like(l_ref)

        l_ref[...] += 0.5 * jnp.sum(jnp.mean(e * e, axis=-1, keepdims=True), axis=0, keepdims=True)

    row = pl.BlockSpec((tm, D), lambda i: (i, 0))
    return pl.pallas_call(
        body, name="loss_head", grid=(t // tm,), in_specs=[row, row],
        out_specs=(row, pl.BlockSpec((8, 128), lambda i: (0, 0))),
        out_shape=(SDS((t, D), F32), SDS((8, 128), F32)),
        compiler_params=_cp("arbitrary"),
    )(y, tgt)


def _adamw(w, g, m, v):
    m = ADAM_B1 * m + (1.0 - ADAM_B1) * g
    v = ADAM_B2 * v + (1.0 - ADAM_B2) * (g * g)
    m_hat = m / (1.0 - ADAM_B1 ** ADAM_STEP)
    v_hat = v / (1.0 - ADAM_B2 ** ADAM_STEP)
    delta = -ADAM_LR * (m_hat / (jnp.sqrt(v_hat) + ADAM_EPS) + ADAM_WD * w)
    return delta, m, v


def adam_big(name, w, m, v, parts, rows, lanes, blk_off, tr, l0, prev=None):
    nr = rows // tr
    nl = len(parts)

    def body(w_ref, m_ref, v_ref, *rest):
        g_ref, d_ref, mo_ref, vo_ref = rest[-4:]
        l = pl.program_id(0)
        for ll in range(nl):
            pr = rest[ll]

            @pl.when(l == ll)
            def _():
                g = pr[0].astype(F32)
                for s in range(1, N_DEV):
                    g = g + pr[s].astype(F32)
                g_ref[...] = g
                d, mn, vn = _adamw(w_ref[...], g, m_ref[...], v_ref[...])
                d_ref[...] = d
                mo_ref[...] = mn
                vo_ref[...] = vn

    blk = pl.BlockSpec((None, tr, lanes), lambda l, r: (l + l0, r, 0))

    def part_spec(ll):
        return pl.BlockSpec((N_DEV, tr, lanes), lambda l, r: (0, jnp.where(l == ll, r, 0) + blk_off, 0))

    out = SDS((DEPTH, rows, lanes), F32)
    extra = [] if prev is None else list(prev)
    return pl.pallas_call(
        body, name=name, grid=(nl, nr),
        in_specs=[blk, blk, blk] + [part_spec(ll) for ll in range(nl)] + [ANY] * len(extra),
        out_specs=(blk,) * 4, out_shape=(out,) * 4,
        input_output_aliases={3 + nl + i: i for i in range(len(extra))},
        compiler_params=_cp("parallel", "parallel"),
    )(w, m, v, *parts, *extra)


def sum_small(land):
    def body(l_ref, o_ref):
        g = l_ref[0]
        for s in range(1, N_DEV):
            g = g + l_ref[s]
        o_ref[...] = g

    return pl.pallas_call(body, name="sum_small", out_shape=SDS(land.shape[1:], F32))(land)


def adam_small(name, w, g, m, v):
    def body(w_ref, g_ref, m_ref, v_ref, d_ref, mo_ref, vo_ref):
        d, mn, vn = _adamw(w_ref[...], g_ref[...], m_ref[...], v_ref[...])
        d_ref[...] = d
        mo_ref[...] = mn
        vo_ref[...] = vn

    out = SDS(w.shape, F32)
    return pl.pallas_call(body, name=name, out_shape=(out, out, out))(w, g, m, v)


def _ffn_forward(xf, xb, gw, f, s):
    g, u, a = ffn_up(xb, gw["w1"], f)
    xo, xob, xh, rs = ffn_down_ln(a, gw["w2"], f, xf, gw["small"], s)
    return (xo, xob), dict(xb=xb, g=g, u=u, a=a, xh=xh, rs=rs)


def _ffn_backward(dxn, sv, gw, f, s):
    dz, df, dgu, dln = ffn_bwd_gates(dxn, sv["xh"], sv["rs"], gw["small"], s, gw["w2"], f, sv["g"], sv["u"])
    dx = ffn_bwd_dx(dz, dgu, gw["w1"], f)
    dw1 = ffn_dw1(sv["xb"], dgu, f)
    dw2 = ffn_dw2(sv["a"], df, f)
    return dx, dw1, dw2, dln


def _mixer_forward(xf, xb, gw, rcb, ap, bm):
    p = mix_proj(xb, gw["win"])
    ca, pa, xc, xcb, gi, gr, h = mix_pre(p, gw["small"], rcb, ap, gw["rgw"])
    pb, ya, yb, m, xo, xob, xh, rs = mix_out(pa, h, p, bm, gw["w3"], xf, gw["small"])
    sv = dict(xb=xb, p=p, ca=ca, pa=pa, xc=xc, xcb=xcb, gi=gi, gr=gr, h=h, pb=pb, ya=ya, yb=yb, m=m, xh=xh, rs=rs)
    return (xo, xob), sv


def _mixer_backward(dxn, sv, gw, rcb, ap, bm):
    dz, dzb, dya, dyb, dbg, dca, dh, dphi, dln, dbm = mixb_head(
        dxn, sv["xh"], sv["rs"], gw["small"], gw["w3"], sv["p"], bm, sv["ya"], sv["yb"], sv["ca"], sv["h"])
    dg, dxc, red_rec = mixb_rec(dh, sv["gr"], sv["gi"], sv["h"], sv["xc"], ap, gw["rgw"])
    dplo, dxr, red_conv = mixb_conv(dca, dxc, dbg, sv["p"], gw["small"])
    dx = mixb_dx(dz, dplo, dxr, dphi, gw["win"])
    dwin = mixb_dwin(sv["xb"], dplo, dxr, dphi)
    dwo = mm_tn_square("mixb_dwo", sv["m"], dzb)
    dwoc = mm_tn_square("mixb_dwoc", sv["pa"], dya)
    dwor = mm_tn_square("mixb_dwor", sv["pb"], dyb)
    drgw = mixb_drgw(sv["xcb"], dg)
    return dx, dict(dwin=dwin, dwoc=dwoc, dwor=dwor, dwo=dwo, drgw=drgw), dln, dbm, red_rec, red_conv


def kernel(x, w_in, b_merge, sc_w, rc_w, rc_b, rg_w, rg_b, a_param, w_out_conv, w_out_rnn, w_o, ffn_w1, ffn_w2, ln_g, ln_b, loss_target, m_w_in, m_b_merge, m_sc_w, m_rc_w, m_rc_b, m_rg_w, m_rg_b, m_a_param, m_w_out_conv, m_w_out_rnn, m_w_o, m_ffn_w1, m_ffn_w2, m_ln_g, m_ln_b, v_w_in, v_b_merge, v_sc_w, v_rc_w, v_rc_b, v_rg_w, v_rg_b, v_a_param, v_w_out_conv, v_w_out_rnn, v_w_o, v_ffn_w1, v_ffn_w2, v_ln_g, v_ln_b):
    t = x.shape[1]
    me = _me()

    def rows(parts, total):
        out, off = None, 0
        for part in parts:
            r = part.shape[-2]
            pad = [(0, 0)] * (part.ndim - 2) + [(off, total - off - r), (0, 0)]
            padded = jnp.pad(part, pad)
            out = padded if out is None else out + padded
            off += r
        return out

    def pack_small(sc, rc, rgb, lng, lnb):
        return rows([sc, rc, rgb, lng, lnb], SMALL_ROWS)

    def pack_rep(bmv, rcbv, apv):
        return rows([bmv.reshape(DEPTH, 2, D), rcbv[:, None], apv[:, None]], 8)

    small = pack_small(sc_w, rc_w, rg_b, ln_g, ln_b)

    me1 = jnp.reshape(me, (1,)).astype(jnp.int32)

    def layer_shards(l):
        return gather_prep(l, me1, w_in, w_out_conv, w_out_rnn, w_o, ffn_w1, ffn_w2, rg_w, small)

    def as_weights(lands):
        return dict(zip(("win", "w3", "w1", "w2", "rgw", "small"), lands))

    xf = x.reshape(t, D)
    cur = (xf, xf.astype(BF16))
    gathers = [weight_gather(l) for l in range(DEPTH)]
    shards, own = layer_shards(0)
    flight = gathers[0].start(shards, own, [cur[1]])
    lands, (xf0,) = gathers[0].wait(flight[0], flight[1], flight[2], flight[3], [cur[0]])
    cur = (xf0, flight[4][0])
    gws, saved = [as_weights(lands)], []
    for l in range(DEPTH):
        gw = gws[l]
        if l + 1 < DEPTH:
            shards, own = layer_shards(l + 1)
            flight = gathers[l + 1].start(shards, own, [cur[1]])
            cur = (cur[0], flight[4][0])
        rcb, ap, bm = rc_b[l][None], a_param[l][None], b_merge[l][None]
        cur, s0 = _ffn_forward(cur[0], cur[1], gw, 0, 0)
        cur, s1 = _mixer_forward(cur[0], cur[1], gw, rcb, ap, bm)
        cur, s2 = _ffn_forward(cur[0], cur[1], gw, 1, 2)
        saved.append((s0, s1, s2))
        if l + 1 < DEPTH:
            lands, (xo,) = gathers[l + 1].wait(flight[0], flight[1], flight[2], flight[3], [cur[0]])
            cur = (xo, cur[1])
            gws.append(as_weights(lands))

    dy, loss_tile = loss_head(cur[0], loss_target.reshape(t, D))
    loss = lax.psum(loss_tile[0, 0], ("x", "y", "c"))

    lands = [None] * DEPTH
    gsmall = [None] * DEPTH
    scatters = [grad_scatter(l) for l in range(DEPTH)]
    flight = None
    for l in reversed(range(DEPTH)):
        gw = gws[l]
        rcb, ap, bm = rc_b[l][None], a_param[l][None], b_merge[l][None]
        s0, s1, s2 = saved[l]
        dy, dw1b, dw2b, dln2 = _ffn_backward(dy, s2, gw, 1, 2)
        dy, dmix, dln1, dbm, red_rec, red_conv = _mixer_backward(dy, s1, gw, rcb, ap, bm)
        dy, dw1a, dw2a, dln0 = _ffn_backward(dy, s0, gw, 0, 0)
        if flight is not None:
            lands[l + 1], (dy,) = scatters[l + 1].wait(flight[0], flight[1], flight[2], flight[3], [dy])
        grads = [dmix["dwin"], dmix["dwoc"], dmix["dwor"], dmix["dwo"], dw1a, dw1b, dw2a, dw2b, dmix["drgw"]]
        flight = scatters[l].start(grads, scatter_prep(l, me1, grads), [dy])
        dy = flight[4][0]
        gsmall[l] = rows([red_conv[0:7], red_rec[0:2], dln0[0:1], dln1[0:1], dln2[0:1], dln0[1:2], dln1[1:2],
                          dln2[1:2], dbm, red_rec[3:4], red_rec[2:3]], GRAD_ROWS)
    grad_x = dy.reshape(1, t, D)

    gfull = sum_small(gather_small_grads(jnp.stack(gsmall, axis=0)))

    def parts_of(ls):
        return dict(w_in=[ls[l][0] for l in range(len(ls))],
                    w3=[ls[l][1].reshape(N_DEV, 3 * 128, D) for l in range(len(ls))],
                    w1=[ls[l][2].reshape(N_DEV, 2 * D, FS) for l in range(len(ls))],
                    w2=[ls[l][3].reshape(N_DEV, 2 * W2S, D) for l in range(len(ls))],
                    rgw=[ls[l][4].reshape(N_DEV, 2 * HEADS * 32, HD) for l in range(len(ls))])

    families = [("w_in", w_in, m_w_in, v_w_in, "w_in", D, WIN_S, 0, 128),
                ("w_out_conv", w_out_conv, m_w_out_conv, v_w_out_conv, "w3", 128, D, 0, 128),
                ("w_out_rnn", w_out_rnn, m_w_out_rnn, v_w_out_rnn, "w3", 128, D, 1, 128),
                ("w_o", w_o, m_w_o, v_w_o, "w3", 128, D, 2, 128),
                ("ffn_w1", ffn_w1, m_ffn_w1, v_ffn_w1, "w1", 2 * D, FS, 0, 256),
                ("ffn_w2", ffn_w2, m_ffn_w2, v_ffn_w2, "w2", 2 * W2S, D, 0, 2 * W2S // 4),
                ("rg_w", rg_w, m_rg_w, v_rg_w, "rgw", 2 * HEADS * 32, HD, 0, 256)]

    def adam_pass(tag, parts, l0, prev):
        outs = {}
        for name, w, m, v, fam, nrow, lanes, blk_off, tr in families:
            r3 = lambda a: a.reshape(DEPTH, nrow, lanes)
            outs[name] = adam_big(f"adam_{name}_{tag}", r3(w), r3(m), r3(v), parts[fam], nrow, lanes, blk_off, tr, l0,
                                  None if prev is None else prev[name])
        return outs

    first = adam_pass("rest", parts_of(lands[1:]), 1, None)
    thru = [first[f[0]][0] for f in families] + [gfull]
    lands[0], thru = scatters[0].wait(flight[0], flight[1], flight[2], flight[3], thru)
    gfull = thru[-1]
    prev = {f[0]: [thru[i]] + list(first[f[0]][1:]) for i, f in enumerate(families)}
    final = adam_pass("first", parts_of(lands[0:1]), 0, prev)
    res = {f[0]: [o.reshape(f[1].shape) for o in final[f[0]]] for f in families}

    g_sh = lax.dynamic_slice(gfull, (0, 0, me * 128), (DEPTH, SMALL_ROWS, 128))
    g_rep = gfull[:, G_BM:G_BM + 8, :]
    d_sh, m_sh, v_sh = adam_small("adam_small_sharded", small, g_sh,
                                  pack_small(m_sc_w, m_rc_w, m_rg_b, m_ln_g, m_ln_b),
                                  pack_small(v_sc_w, v_rc_w, v_rg_b, v_ln_g, v_ln_b))
    d_rep, m_rep, v_rep = adam_small("adam_small_replicated", pack_rep(b_merge, rc_b, a_param), g_rep,
                                     pack_rep(m_b_merge, m_rc_b, m_a_param), pack_rep(v_b_merge, v_rc_b, v_a_param))

    def unpack_sh(a):
        return dict(sc_w=a[:, 0:3], rc_w=a[:, 3:7], rg_b=a[:, 7:9], ln_g=a[:, 9:12], ln_b=a[:, 12:15])

    def unpack_rep(a):
        return dict(b_merge=a[:, 0:2].reshape(DEPTH, 2 * D), rc_b=a[:, 2], a_param=a[:, 3])

    gsh, grep = unpack_sh(g_sh), unpack_rep(g_rep)
    dsh, drep = unpack_sh(d_sh), unpack_rep(d_rep)
    msh, mrep = unpack_sh(m_sh), unpack_rep(m_rep)
    vsh, vrep = unpack_sh(v_sh), unpack_rep(v_rep)
    for n in ("sc_w", "rc_w", "rg_b", "ln_g", "ln_b"):
        res[n] = [gsh[n], dsh[n], msh[n], vsh[n]]
    for n in ("b_merge", "rc_b", "a_param"):
        res[n] = [grep[n], drep[n], mrep[n], vrep[n]]

    names = ["w_in", "b_merge", "sc_w", "rc_w", "rc_b", "rg_w", "rg_b", "a_param", "w_out_conv", "w_out_rnn", "w_o",
             "ffn_w1", "ffn_w2", "ln_g", "ln_b"]
    out = [loss, grad_x]
    for k in range(4):
        out += [res[n][k] for n in names]
    return tuple(out)
```

```python
import functools

import jax
import jax.numpy as jnp
from jax import lax
from jax.experimental import pallas as pl
from jax.experimental.pallas import tpu as pltpu

F32 = jnp.float32
BF16 = jnp.bfloat16
SDS = jax.ShapeDtypeStruct

N_DEV = 8
DEPTH = 4
D = 1024
D_FF = 2816
FS = D_FF // 4
W2S = D_FF // 8
D_IN = 7 * D
WIN_S = D_IN // 8
HEADS = 4
HD = D // HEADS
LRU_C = 8.0
ALPHA = (2.0 * DEPTH) ** 0.25
LN_EPS = 1e-5
ADAM_LR, ADAM_B1, ADAM_B2, ADAM_EPS, ADAM_WD, ADAM_STEP = 0.001, 0.9, 0.999, 1e-08, 0.01, 10

R_SC, R_RC, R_RGB, R_LNG, R_LNB = 0, 3, 7, 9, 12
SMALL_ROWS = 16
G_BM, G_RCB, G_AP = 15, 17, 18
GRAD_ROWS = 24

NN = ((1,), (0,))
NT = ((1,), (1,))
TN = ((0,), (0,))
MESH = pl.DeviceIdType.MESH
ANY = pl.BlockSpec(memory_space=pl.ANY)
VMEM_LIMIT = 52 * 1024 * 1024


def _dot(a, b, dims):
    return lax.dot_general(a, b, (dims, ((), ())), preferred_element_type=F32)


def _cp(*sem):
    return pltpu.CompilerParams(dimension_semantics=sem, vmem_limit_bytes=VMEM_LIMIT)


def _sigmoid(x):
    return 1.0 / (1.0 + jnp.exp(-x))


def _log1p(e):
    u = 1.0 + e
    return jnp.where(u == 1.0, e, jnp.log(u) * e / jnp.where(u == 1.0, 1.0, u - 1.0))


def _softplus(x):
    return jnp.maximum(x, 0.0) + _log1p(jnp.exp(-jnp.abs(x)))


def _neg_expm1(x):
    u = jnp.exp(x)
    um1 = u - 1.0
    safe = jnp.logical_and(u != 1.0, um1 != -1.0)
    r = um1 * x / jnp.where(safe, jnp.log(jnp.where(safe, u, 0.5)), 1.0)
    return -jnp.where(u == 1.0, x, jnp.where(um1 == -1.0, -1.0, r))


def _gelu(y):
    c = 0.7978845608028654
    t = jnp.tanh(c * (y + 0.044715 * y * y * y))
    return 0.5 * y * (1.0 + t), t


def _gelu_grad(y, t):
    c = 0.7978845608028654
    return 0.5 * (1.0 + t) + 0.5 * y * (1.0 - t * t) * c * (1.0 + 3.0 * 0.044715 * y * y)


def _ln_fwd(z, g, b):
    mu = jnp.mean(z, axis=-1, keepdims=True)
    zc = z - mu
    var = jnp.mean(zc * zc, axis=-1, keepdims=True)
    rstd = lax.rsqrt(var + LN_EPS)
    xh = zc * rstd
    return xh * g + b, xh, rstd


def _ln_bwd(dy, xh, rstd, g):
    dxh = dy * g
    m1 = jnp.mean(dxh, axis=-1, keepdims=True)
    m2 = jnp.mean(dxh * xh, axis=-1, keepdims=True)
    return rstd * (dxh - m1 - xh * m2)


def _row_tile(t, want):
    return min(want, t)


def _me():
    return 4 * lax.axis_index("x") + 2 * lax.axis_index("y") + lax.axis_index("c")


def _coords(p):
    return (p // 4, (p // 2) % 2, p % 2)


def _all_to_all_copies(srcs_of, dsts_of, waits, send_sems, recv_sems, loc_sems):
    me = _me()
    n = len(waits)
    own_src, own_dst = srcs_of(me), dsts_of(me)
    local = [pltpu.make_async_copy(own_src[k], own_dst[k], loc_sems.at[k]) for k in range(n)]
    for cp in local:
        cp.start()
    for d in range(1, N_DEV):
        p = (me + d) % N_DEV
        src, dst = srcs_of(p), dsts_of(me)
        for k in range(n):
            pltpu.make_async_remote_copy(
                src_ref=src[k], dst_ref=dst[k], send_sem=send_sems.at[waits[k][1]],
                recv_sem=recv_sems.at[waits[k][1]], device_id=_coords(p), device_id_type=MESH).start()
    done = set()
    for k in range(n):
        ref, s = waits[k]
        if s in done:
            continue
        done.add(s)
        pltpu.make_async_remote_copy(
            src_ref=ref, dst_ref=ref, send_sem=send_sems.at[s], recv_sem=recv_sems.at[s],
            device_id=_coords(me), device_id_type=MESH).wait()
    for cp in local:
        cp.wait()


HBM = pl.BlockSpec(memory_space=pltpu.HBM)
SEM = pl.BlockSpec(memory_space=pltpu.SEMAPHORE)
EFFECT = pltpu.SideEffectType.DATAFLOW_SIDE_EFFECTING


def _in_hbm(a):
    return pltpu.with_memory_space_constraint(a, pltpu.HBM)


class Exchange:
    def __init__(self, name, src_of, dst_of, sem_of, span_of):
        self.name, self.src_of, self.dst_of, self.sem_of, self.span_of = name, src_of, dst_of, sem_of, span_of
        self.nsem = max(sem_of) + 1

    def start(self, srcs, lands, thru):
        n, m = len(srcs), len(lands)
        ops = list(srcs) + list(lands) + list(thru)

        def body(*refs):
            src_refs, land_refs = refs[:n], refs[n:n + m]
            send_sems, recv_sems = refs[len(ops)], refs[len(ops) + 1]
            me = _me()
            for dd in range(1, N_DEV):
                p = (me + dd) % N_DEV
                s, d = self.src_of(src_refs, p), self.dst_of(land_refs, me)
                for k in range(len(self.sem_of)):
                    pltpu.make_async_remote_copy(
                        src_ref=s[k], dst_ref=d[k], send_sem=send_sems.at[self.sem_of[k]],
                        recv_sem=recv_sems.at[self.sem_of[k]], device_id=_coords(p), device_id_type=MESH).start()

        sem = pltpu.SemaphoreType.DMA((self.nsem,))
        res = pl.pallas_call(
            body, name=self.name + "_start",
            out_shape=(sem, sem) + tuple(pltpu.HBM(a.shape, a.dtype) for a in ops),
            in_specs=[HBM] * len(ops), out_specs=(SEM, SEM) + (HBM,) * len(ops),
            input_output_aliases={i: 2 + i for i in range(len(ops))},
            compiler_params=pltpu.CompilerParams(has_side_effects=EFFECT),
        )(*[_in_hbm(a) for a in ops])
        return res[0], res[1], res[2:2 + n], res[2 + n:2 + n + m], list(res[2 + n + m:])

    def wait(self, send_sems, recv_sems, srcs, lands, thru):
        n, m = len(srcs), len(lands)
        ops = list(srcs) + list(lands) + list(thru)

        def body(*refs):
            land_refs = refs[n:n + m]
            ssem, rsem = refs[len(ops)], refs[len(ops) + 1]
            me = _me()
            spans = self.span_of(land_refs)
            for s in range(self.nsem):
                cp = pltpu.make_async_remote_copy(
                    src_ref=spans[s], dst_ref=spans[s], send_sem=ssem.at[s], recv_sem=rsem.at[s],
                    device_id=_coords(me), device_id_type=MESH)
                cp.wait_send()
                cp.wait_recv()

        res = pl.pallas_call(
            body, name=self.name + "_wait",
            out_shape=tuple(pltpu.HBM(a.shape, a.dtype) for a in ops),
            in_specs=[HBM] * len(ops) + [SEM, SEM], out_specs=(HBM,) * len(ops),
            input_output_aliases={i: i for i in range(len(ops))},
            compiler_params=pltpu.CompilerParams(has_side_effects=EFFECT),
        )(*ops, send_sems, recv_sems)
        return res[n:n + m], list(res[n + m:])


class ChipGather:
    NARR = 6

    def __init__(self, name):
        self.name = name

    @staticmethod
    def _slots(lands, p):
        o_win, o_w3, o_w1, o_w2, o_rgw, o_sm = lands
        return [o_win.at[:, pl.ds(pl.multiple_of(p * WIN_S, 128), WIN_S)],
                o_w3.at[:, pl.ds(pl.multiple_of(p * 128, 128), 128), :],
                o_w1.at[:, p],
                o_w2.at[:, pl.ds(pl.multiple_of(p * W2S, 16), W2S), :],
                o_rgw.at[:, :, pl.ds(pl.multiple_of(p * 32, 32), 32), :],
                o_sm.at[:, pl.ds(pl.multiple_of(p * 128, 128), 128)]]

    @staticmethod
    def _places():
        x, y, c = lax.axis_index("x"), lax.axis_index("y"), lax.axis_index("c")
        chips = [(1 - x, y), (x, 1 - y), (1 - x, 1 - y)]
        return (x, y, c), (x, y, 1 - c), chips

    @staticmethod
    def _index(place):
        return 4 * place[0] + 2 * place[1] + place[2]

    def _call(self, tag, body, ops, sems_in, sems_out):
        nops = len(ops)
        sem = pltpu.SemaphoreType.DMA((max(sems_out, 1),))
        res = pl.pallas_call(
            body, name=f"{self.name}_{tag}",
            out_shape=((sem, sem) if sems_out else ()) + tuple(pltpu.HBM(a.shape, a.dtype) for a in ops),
            in_specs=[HBM] * nops + [SEM] * len(sems_in),
            out_specs=((SEM, SEM) if sems_out else ()) + (HBM,) * nops,
            input_output_aliases={i: (2 if sems_out else 0) + i for i in range(nops)},
            compiler_params=pltpu.CompilerParams(has_side_effects=EFFECT),
        )(*[_in_hbm(a) for a in ops], *sems_in)
        return res

    def first(self, srcs, lands, thru):
        n = self.NARR
        ops = list(srcs) + list(lands) + list(thru)

        def body(*refs):
            src_refs, land_refs = refs[:n], refs[n:2 * n]
            send_sems, recv_sems = refs[len(ops)], refs[len(ops) + 1]
            me, sibling, chips = self._places()
            dst = self._slots(land_refs, self._index(me))
            targets = [sibling] + [(cx, cy, me[2]) for cx, cy in chips]
            for r, to in enumerate(targets):
                for k in range(n):
                    pltpu.make_async_remote_copy(
                        src_ref=src_refs[k], dst_ref=dst[k], send_sem=send_sems.at[6 * r + k],
                        recv_sem=recv_sems.at[6 * r + k], device_id=to, device_id_type=MESH).start()

        res = self._call("first", body, ops, [], 4 * n)
        return (res[0], res[1]), list(res[2:2 + n]), list(res[2 + n:2 + 2 * n]), list(res[2 + 2 * n:])

    def relay(self, sems, srcs, lands, thru):
        n = self.NARR
        ops = list(srcs) + list(lands) + list(thru)

        def wait_body(*refs):
            land_refs = refs[n:2 * n]
            ssem, rsem = refs[len(ops)], refs[len(ops) + 1]
            me, sibling, chips = self._places()
            for j, (cx, cy) in enumerate(chips):
                got = self._slots(land_refs, self._index((cx, cy, me[2])))
                for k in range(n):
                    pltpu.make_async_remote_copy(
                        src_ref=got[k], dst_ref=got[k], send_sem=ssem.at[6 * (1 + j) + k], recv_sem=rsem.at[6 * (1 + j) + k],
                        device_id=me, device_id_type=MESH).wait_recv()

        ops = list(self._call("landed", wait_body, ops, list(sems), 0))

        def start_body(*refs):
            land_refs = refs[n:2 * n]
            ssem, rsem = refs[len(ops)], refs[len(ops) + 1]
            me, sibling, chips = self._places()
            for j, (cx, cy) in enumerate(chips):
                got = self._slots(land_refs, self._index((cx, cy, me[2])))
                for k in range(n):
                    pltpu.make_async_remote_copy(
                        src_ref=got[k], dst_ref=got[k], send_sem=ssem.at[6 * j + k], recv_sem=rsem.at[6 * j + k],
                        device_id=sibling, device_id_type=MESH).start()

        res = self._call("relay", start_body, ops, [], 3 * n)
        return (res[0], res[1]), list(res[2:2 + n]), list(res[2 + n:2 + 2 * n]), list(res[2 + 2 * n:])

    def last(self, sems, relay_sems, srcs, lands, thru):
        n = self.NARR
        ops = list(srcs) + list(lands) + list(thru)

        def body(*refs):
            src_refs, land_refs = refs[:n], refs[n:2 * n]
            ssem, rsem, ssem2, rsem2 = refs[len(ops):len(ops) + 4]
            me, sibling, chips = self._places()
            got = self._slots(land_refs, self._index(sibling))
            for k in range(n):
                pltpu.make_async_remote_copy(
                    src_ref=got[k], dst_ref=got[k], send_sem=ssem.at[k], recv_sem=rsem.at[k],
                    device_id=me, device_id_type=MESH).wait_recv()
            for j, (cx, cy) in enumerate(chips):
                got = self._slots(land_refs, self._index((cx, cy, 1 - me[2])))
                for k in range(n):
                    pltpu.make_async_remote_copy(
                        src_ref=got[k], dst_ref=got[k], send_sem=ssem2.at[6 * j + k], recv_sem=rsem2.at[6 * j + k],
                        device_id=me, device_id_type=MESH).wait_recv()
            for sem_s, sem_r, count in ((ssem, rsem, 4), (ssem2, rsem2, 3)):
                for r in range(count):
                    for k in range(n):
                        pltpu.make_async_remote_copy(
                            src_ref=src_refs[k], dst_ref=src_refs[k], send_sem=sem_s.at[6 * r + k],
                            recv_sem=sem_r.at[6 * r + k], device_id=me, device_id_type=MESH).wait_send()

        res = self._call("last", body, ops, list(sems) + list(relay_sems), 0)
        return list(res[n:2 * n]), list(res[2 * n:])


GATHER_SHAPES = (SDS((D, D_IN), BF16), SDS((3, D, D), BF16), SDS((2, N_DEV, D, FS), BF16),
                 SDS((2, D_FF, D), BF16), SDS((2, HEADS, HD, HD), BF16), SDS((SMALL_ROWS, D), F32))


def grad_scatter(l):
    def src_of(refs, p):
        dwin, dwoc, dwor, dwo, dw1a, dw1b, dw2a, dw2b, drgw = refs
        rows3 = pl.ds(pl.multiple_of(p * 128, 128), 128)
        rows2 = pl.ds(pl.multiple_of(p * W2S, 16), W2S)
        return [dwin.at[:, pl.ds(pl.multiple_of(p * WIN_S, 128), WIN_S)],
                dwoc.at[rows3, :], dwor.at[rows3, :], dwo.at[rows3, :],
                dw1a.at[p], dw1b.at[p], dw2a.at[rows2, :], dw2b.at[rows2, :],
                drgw.at[:, :, pl.ds(pl.multiple_of(p * 32, 32), 32), :]]

    def dst_of(lands, p):
        l_win, l_w3, l_w1, l_w2, l_rgw = lands
        return [l_win.at[p], l_w3.at[p, 0], l_w3.at[p, 1], l_w3.at[p, 2], l_w1.at[p, 0], l_w1.at[p, 1],
                l_w2.at[p, 0], l_w2.at[p, 1], l_rgw.at[p]]

    def span_of(lands):
        return [a.at[pl.ds(0, 7)] for a in lands]

    return Exchange(f"scatter{l}", src_of, dst_of, [0, 1, 1, 1, 2, 2, 3, 3, 4], span_of)


SCATTER_SHAPES = (SDS((N_DEV, D, WIN_S), BF16), SDS((N_DEV, 3, 128, D), BF16), SDS((N_DEV, 2, D, FS), BF16),
                  SDS((N_DEV, 2, W2S, D), BF16), SDS((N_DEV, 2, HEADS, 32, HD), BF16))


def gather_prep(l, me1, w_in, w_out_conv, w_out_rnn, w_o, ffn_w1, ffn_w2, rg_w, small):
    hd, hw = D // 2, W2S // 2

    def body(me_ref, win, woc, wor, wo, w1, w2, rgw, sm, c_win, c_w3, c_w1, c_w2, c_rgw, c_sm,
             o_win, o_w3, o_w1, o_w2, o_rgw, o_sm):
        a = win[...].astype(BF16)
        c_win[...] = a
        o_win[...] = a
        for k, r in enumerate((woc, wor, wo)):
            b = r[...].astype(BF16)
            c_w3[k] = b
            o_w3[k] = b
        for src, comp, own in ((w1, c_w1, o_w1), (w2, c_w2, o_w2), (rgw, c_rgw, o_rgw)):
            b = src[...].astype(BF16)
            comp[...] = b
            own[...] = b
        c_sm[...] = sm[...]
        o_sm[...] = sm[...]

    three = pl.BlockSpec((None, 64, D), lambda i, me: (l, i, 0))
    in_specs = [pl.BlockSpec((None, hd, WIN_S), lambda i, me: (l, i, 0)), three, three, three,
                pl.BlockSpec((None, 2, hd, FS), lambda i, me: (l, 0, i, 0)),
                pl.BlockSpec((None, 2, hw, D), lambda i, me: (l, 0, i, 0)),
                pl.BlockSpec((None, 2, HEADS, 16, HD), lambda i, me: (l, 0, 0, i, 0)),
                pl.BlockSpec((None, 8, 128), lambda i, me: (l, i, 0))]
    out_specs = (pl.BlockSpec((hd, WIN_S), lambda i, me: (i, 0)), pl.BlockSpec((3, 64, D), lambda i, me: (0, i, 0)),
                 pl.BlockSpec((2, hd, FS), lambda i, me: (0, i, 0)), pl.BlockSpec((2, hw, D), lambda i, me: (0, i, 0)),
                 pl.BlockSpec((2, HEADS, 16, HD), lambda i, me: (0, 0, i, 0)), pl.BlockSpec((8, 128), lambda i, me: (i, 0)),
                 pl.BlockSpec((hd, WIN_S), lambda i, me: (i, me[0])),
                 pl.BlockSpec((3, 64, D), lambda i, me: (0, 2 * me[0] + i, 0)),
                 pl.BlockSpec((2, None, hd, FS), lambda i, me: (0, me[0], i, 0)),
                 pl.BlockSpec((2, hw, D), lambda i, me: (0, 2 * me[0] + i, 0)),
                 pl.BlockSpec((2, HEADS, 16, HD), lambda i, me: (0, 0, 2 * me[0] + i, 0)),
                 pl.BlockSpec((8, 128), lambda i, me: (i, me[0])))
    compact = (SDS((D, WIN_S), BF16), SDS((3, 128, D), BF16), SDS((2, D, FS), BF16), SDS((2, W2S, D), BF16),
               SDS((2, HEADS, 32, HD), BF16), SDS((SMALL_ROWS, 128), F32))
    res = pl.pallas_call(
        body, name=f"gather_prep{l}", out_shape=compact + GATHER_SHAPES,
        grid_spec=pltpu.PrefetchScalarGridSpec(num_scalar_prefetch=1, grid=(2,), in_specs=in_specs, out_specs=out_specs),
        compiler_params=_cp("arbitrary"),
    )(me1, w_in, w_out_conv, w_out_rnn, w_o, ffn_w1, ffn_w2, rg_w, small)
    return list(res[:6]), list(res[6:])


def scatter_prep(l, me1, grads):
    hd, hw = D // 2, W2S // 2

    def body(me_ref, dwin, dwoc, dwor, dwo, dw1a, dw1b, dw2a, dw2b, drgw, l_win, l_w3, l_w1, l_w2, l_rgw):
        l_win[...] = dwin[...]
        for k, r in enumerate((dwoc, dwor, dwo)):
            l_w3[k] = r[...]
        l_w1[0] = dw1a[...]
        l_w1[1] = dw1b[...]
        l_w2[0] = dw2a[...]
        l_w2[1] = dw2b[...]
        l_rgw[...] = drgw[...]

    three = pl.BlockSpec((64, D), lambda i, me: (2 * me[0] + i, 0))
    one = pl.BlockSpec((None, hd, FS), lambda i, me: (me[0], i, 0))
    two = pl.BlockSpec((hw, D), lambda i, me: (2 * me[0] + i, 0))
    in_specs = [pl.BlockSpec((hd, WIN_S), lambda i, me: (i, me[0])), three, three, three, one, one, two, two,
                pl.BlockSpec((2, HEADS, 16, HD), lambda i, me: (0, 0, 2 * me[0] + i, 0))]
    out_specs = (pl.BlockSpec((None, hd, WIN_S), lambda i, me: (me[0], i, 0)),
                 pl.BlockSpec((None, 3, 64, D), lambda i, me: (me[0], 0, i, 0)),
                 pl.BlockSpec((None, 2, hd, FS), lambda i, me: (me[0], 0, i, 0)),
                 pl.BlockSpec((None, 2, hw, D), lambda i, me: (me[0], 0, i, 0)),
                 pl.BlockSpec((None, 2, HEADS, 16, HD), lambda i, me: (me[0], 0, 0, i, 0)))
    return list(pl.pallas_call(
        body, name=f"scatter_prep{l}", out_shape=SCATTER_SHAPES,
        grid_spec=pltpu.PrefetchScalarGridSpec(num_scalar_prefetch=1, grid=(2,), in_specs=in_specs, out_specs=out_specs),
        compiler_params=_cp("arbitrary"),
    )(me1, *grads))


def gather_small_grads(gsmall):
    shp = gsmall.shape

    def body(g, land, send_sems, recv_sems, loc_sems):
        _all_to_all_copies(lambda p: [g], lambda p: [land.at[p]], [(land.at[pl.ds(0, 7)], 0)],
                           send_sems, recv_sems, loc_sems)

    return pl.pallas_call(
        body, name="gather_small_grads", out_shape=SDS((N_DEV,) + shp, F32),
        in_specs=[ANY], out_specs=ANY,
        scratch_shapes=[pltpu.SemaphoreType.DMA((1,)), pltpu.SemaphoreType.DMA((1,)), pltpu.SemaphoreType.DMA((1,))],
        compiler_params=pltpu.CompilerParams(has_side_effects=True),
    )(gsmall)


def ffn_up(xb, w1, f):
    t = xb.shape[0]
    tm = _row_tile(t, 512)

    def body(x_ref, wg_ref, wu_ref, g_ref, u_ref, a_ref):
        x = x_ref[...]
        g = _dot(x, wg_ref[...], NN)
        u = _dot(x, wu_ref[...], NN)
        g_ref[...] = g.astype(BF16)
        u_ref[...] = u.astype(BF16)
        a_ref[...] = (g * _sigmoid(g) * u).astype(BF16)

    out = pl.BlockSpec((None, tm, FS), lambda j, i: (j, i, 0))
    return pl.pallas_call(
        body, name=f"ffn_up{f}", grid=(4, t // tm),
        in_specs=[pl.BlockSpec((tm, D), lambda j, i: (i, 0)),
                  pl.BlockSpec((None, None, D, FS), lambda j, i: (f, j, 0, 0)),
                  pl.BlockSpec((None, None, D, FS), lambda j, i: (f, j + 4, 0, 0))],
        out_specs=(out, out, out), out_shape=(SDS((4, t, FS), BF16),) * 3,
        compiler_params=_cp("parallel", "parallel"),
    )(xb, w1, w1)


def ffn_down_ln(a, w2, f, xf, small, s):
    t = xf.shape[0]
    tm = _row_tile(t, 256)

    def body(a_ref, w_ref, x_ref, sm_ref, xo_ref, xb_ref, xh_ref, rs_ref):
        acc = _dot(a_ref[0], w_ref[0:FS, :], NN)
        for j in range(1, 4):
            acc = acc + _dot(a_ref[j], w_ref[j * FS:(j + 1) * FS, :], NN)
        z = ALPHA * x_ref[...] + 0.5 * acc
        y, xh, rstd = _ln_fwd(z, sm_ref[R_LNG + s:R_LNG + s + 1, :], sm_ref[R_LNB + s:R_LNB + s + 1, :])
        xo_ref[...] = y
        xb_ref[...] = y.astype(BF16)
        xh_ref[...] = xh
        rs_ref[...] = rstd

    row = pl.BlockSpec((tm, D), lambda i: (i, 0))
    return pl.pallas_call(
        body, name=f"ffn_down_ln{f}", grid=(t // tm,),
        in_specs=[pl.BlockSpec((4, tm, FS), lambda i: (0, i, 0)),
                  pl.BlockSpec((None, D_FF, D), lambda i: (f, 0, 0)),
                  row, pl.BlockSpec((SMALL_ROWS, D), lambda i: (0, 0))],
        out_specs=(row, row, row, pl.BlockSpec((tm, 1), lambda i: (i, 0))),
        out_shape=(SDS((t, D), F32), SDS((t, D), BF16), SDS((t, D), F32), SDS((t, 1), F32)),
        compiler_params=_cp("parallel"),
    )(a, w2, xf, small)


def ffn_bwd_gates(dxn, xh, rstd, small, s, w2, f, g, u):
    t = dxn.shape[0]
    tm = _row_tile(t, 256)

    def body(dy_ref, xh_ref, rs_ref, sm_ref, w_ref, g_ref, u_ref, dz_ref, df_ref, dgu_ref, dln_ref):
        i = pl.program_id(0)
        dy = dy_ref[...]
        xhat = xh_ref[...]
        dz = _ln_bwd(dy, xhat, rs_ref[...], sm_ref[R_LNG + s:R_LNG + s + 1, :])

        @pl.when(i == 0)
        def _():
            dln_ref[...] = jnp.zeros_like(dln_ref)

        dln_ref[0:1, :] += jnp.sum(dy * xhat, axis=0, keepdims=True)
        dln_ref[1:2, :] += jnp.sum(dy, axis=0, keepdims=True)
        dz_ref[...] = dz
        df = (0.5 * dz).astype(BF16)
        df_ref[...] = df
        for j in range(4):
            da = _dot(df, w_ref[j * FS:(j + 1) * FS, :], NT)
            gg = g_ref[j].astype(F32)
            uu = u_ref[j].astype(F32)
            sg = _sigmoid(gg)
            dgu_ref[j] = (da * uu * (sg * (1.0 + gg * (1.0 - sg)))).astype(BF16)
            dgu_ref[j + 4] = (da * (gg * sg)).astype(BF16)

    row = pl.BlockSpec((tm, D), lambda i: (i, 0))
    gu = pl.BlockSpec((4, tm, FS), lambda i: (0, i, 0))
    return pl.pallas_call(
        body, name=f"ffn_bwd_gates{f}", grid=(t // tm,),
        in_specs=[row, row, pl.BlockSpec((tm, 1), lambda i: (i, 0)), pl.BlockSpec((SMALL_ROWS, D), lambda i: (0, 0)),
                  pl.BlockSpec((None, D_FF, D), lambda i: (f, 0, 0)), gu, gu],
        out_specs=(row, row, pl.BlockSpec((8, tm, FS), lambda i: (0, i, 0)), pl.BlockSpec((2, D), lambda i: (0, 0))),
        out_shape=(SDS((t, D), F32), SDS((t, D), BF16), SDS((8, t, FS), BF16), SDS((2, D), F32)),
        compiler_params=_cp("arbitrary"),
    )(dxn, xh, rstd, small, w2, g, u)


def ffn_bwd_dx(dz, dgu, w1, f):
    t = dz.shape[0]
    tm = _row_tile(t, 1024)

    def body(dz_ref, d_ref, w_ref, dx_ref, acc):
        k = pl.program_id(1)

        @pl.when(k == 0)
        def _():
            acc[...] = ALPHA * dz_ref[...]

        acc[...] += _dot(d_ref[...], w_ref[...], NT)

        @pl.when(k == 7)
        def _():
            dx_ref[...] = acc[...]

    row = pl.BlockSpec((tm, D), lambda i, k: (i, 0))
    return pl.pallas_call(
        body, name=f"ffn_bwd_dx{f}", grid=(t // tm, 8),
        in_specs=[row, pl.BlockSpec((None, tm, FS), lambda i, k: (k, i, 0)),
                  pl.BlockSpec((None, None, D, FS), lambda i, k: (f, k, 0, 0))],
        out_specs=row, out_shape=SDS((t, D), F32),
        scratch_shapes=[pltpu.VMEM((tm, D), F32)],
        compiler_params=_cp("parallel", "arbitrary"),
    )(dz, dgu, w1)


def _mm_tn(name, a, a_spec, b, b_spec, out_sds, out_spec, grid):
    def body(a_ref, b_ref, o_ref):
        o_ref[...] = _dot(a_ref[...], b_ref[...], TN).astype(o_ref.dtype)

    return pl.pallas_call(
        body, name=name, grid=grid, in_specs=[a_spec, b_spec], out_specs=out_spec, out_shape=out_sds,
        compiler_params=_cp(*(["parallel"] * len(grid))),
    )(a, b)


def ffn_dw1(xb, dgu, f):
    t = xb.shape[0]
    return _mm_tn(f"ffn_dw1_{f}", xb, pl.BlockSpec((t, D), lambda j: (0, 0)),
                  dgu, pl.BlockSpec((None, t, FS), lambda j: (j, 0, 0)),
                  SDS((8, D, FS), BF16), pl.BlockSpec((None, D, FS), lambda j: (j, 0, 0)), (8,))


def ffn_dw2(a, df, f):
    t = df.shape[0]
    return _mm_tn(f"ffn_dw2_{f}", a, pl.BlockSpec((None, t, FS), lambda j: (j, 0, 0)),
                  df, pl.BlockSpec((t, D), lambda j: (0, 0)),
                  SDS((D_FF, D), BF16), pl.BlockSpec((FS, D), lambda j: (j, 0)), (4,))


def mm_tn_square(name, a, b):
    t = a.shape[0]
    return _mm_tn(name, a, pl.BlockSpec((t, 512), lambda i: (0, i)),
                  b, pl.BlockSpec((t, D), lambda i: (0, 0)),
                  SDS((D, D), BF16), pl.BlockSpec((512, D), lambda i: (i, 0)), (2,))


def mix_proj(xb, win):
    t = xb.shape[0]
    tm = _row_tile(t, 512)

    def body(x_ref, w_ref, o_ref):
        o_ref[...] = _dot(x_ref[...], w_ref[...], NN).astype(BF16)

    return pl.pallas_call(
        body, name="mix_proj", grid=(7, t // tm),
        in_specs=[pl.BlockSpec((tm, D), lambda n, i: (i, 0)), pl.BlockSpec((D, D), lambda n, i: (0, n))],
        out_specs=pl.BlockSpec((tm, D), lambda n, i: (i, n)), out_shape=SDS((t, D_IN), BF16),
        compiler_params=_cp("parallel", "parallel"),
    )(xb, win)


def _pcol(tm, k):
    return pl.BlockSpec((tm, D), lambda i: (i, k))


def _prev_halo(tm, k):
    return pl.BlockSpec((8, D), lambda i: (jnp.maximum(i * (tm // 8) - 1, 0), k))


def _prev_halo16(tm, k):
    return pl.BlockSpec((16, D), lambda i: (jnp.maximum(i * (tm // 16) - 1, 0), k))


def _next_halo(tm, t, k):
    return pl.BlockSpec((8, D), lambda i: (jnp.minimum((i + 1) * (tm // 8), t // 8 - 1), k))


def _full(shape):
    nd = len(shape)
    return pl.BlockSpec(shape, lambda i: (0,) * nd)


def mix_pre(p, small, rcb, ap, rgw):
    t = p.shape[0]
    tm = _row_tile(t, 256)

    def body(bg_ref, cg_ref, v_ref, xr_ref, cgh_ref, vh_ref, xrh_ref, sm_ref, rcb_ref, ap_ref, rgw_ref,
             ca_ref, pa_ref, xc_ref, xcb_ref, gi_ref, gr_ref, h_ref, ext1, ext2, a_s, b_s, carry):
        i = pl.program_id(0)
        first = i == 0
        cv = cg_ref[...].astype(F32) * v_ref[...].astype(F32)
        ext1[0:16, :] = jnp.where(first, 0.0, cgh_ref[...].astype(F32) * vh_ref[...].astype(F32))
        ext1[16:, :] = cv
        xr = xr_ref[...].astype(F32)
        ext2[0:16, :] = jnp.where(first, 0.0, xrh_ref[...].astype(F32))
        ext2[16:, :] = xr
        ca = (sm_ref[R_SC:R_SC + 1, :] * ext1[pl.ds(14, tm), :] + sm_ref[R_SC + 1:R_SC + 2, :] * ext1[pl.ds(15, tm), :]
              + sm_ref[R_SC + 2:R_SC + 3, :] * cv)
        ca_ref[...] = ca
        pa_ref[...] = (bg_ref[...].astype(F32) * ca).astype(BF16)
        xc = (sm_ref[R_RC:R_RC + 1, :] * ext2[pl.ds(13, tm), :] + sm_ref[R_RC + 1:R_RC + 2, :] * ext2[pl.ds(14, tm), :]
              + sm_ref[R_RC + 2:R_RC + 3, :] * ext2[pl.ds(15, tm), :] + sm_ref[R_RC + 3:R_RC + 4, :] * xr
              + rcb_ref[...])
        xc_ref[...] = xc
        xcb = xc.astype(BF16)
        xcb_ref[...] = xcb
        g0, g1 = [], []
        for h in range(HEADS):
            xh = xcb[:, h * HD:(h + 1) * HD]
            g0.append(_dot(xh, rgw_ref[0, h], NN))
            g1.append(_dot(xh, rgw_ref[1, h], NN))
        gi = _sigmoid(jnp.concatenate(g0, axis=1) + sm_ref[R_RGB:R_RGB + 1, :])
        gr = _sigmoid(jnp.concatenate(g1, axis=1) + sm_ref[R_RGB + 1:R_RGB + 2, :])
        gi_ref[...] = gi
        gr_ref[...] = gr
        la = (-LRU_C) * gr * _softplus(-ap_ref[...])
        a_s[...] = jnp.exp(la)
        row = lax.broadcasted_iota(jnp.int32, (tm, D), 0) + i * tm
        mult = jnp.where(row == 0, 1.0, jnp.sqrt(_neg_expm1(2.0 * la)))
        b_s[...] = xc * gi * mult

        @pl.when(first)
        def _():
            carry[...] = jnp.zeros_like(carry)

        carry[...] = _scan_tile(a_s, b_s, h_ref, carry[...], tm, reverse=False)

    row = pl.BlockSpec((tm, D), lambda i: (i, 0))
    f32o, b16o = SDS((t, D), F32), SDS((t, D), BF16)
    ext, tile = pltpu.VMEM((tm + 16, D), F32), pltpu.VMEM((tm, D), F32)
    return pl.pallas_call(
        body, name="mix_pre", grid=(t // tm,),
        in_specs=[_pcol(tm, 0), _pcol(tm, 1), _pcol(tm, 2), _pcol(tm, 3),
                  _prev_halo16(tm, 1), _prev_halo16(tm, 2), _prev_halo16(tm, 3),
                  _full((SMALL_ROWS, D)), _full((1, D)), _full((1, D)), _full((2, HEADS, HD, HD))],
        out_specs=(row,) * 7, out_shape=(f32o, b16o, f32o, b16o, f32o, f32o, f32o),
        scratch_shapes=[ext, ext, tile, tile, pltpu.VMEM((8, D), F32)],
        compiler_params=_cp("arbitrary"),
    )(p, p, p, p, p, p, p, small, rcb, ap, rgw)


def _scan_tile(a_ref, b_ref, o_ref, carry, tm, reverse):
    width = a_ref.shape[1]
    ng = tm // 8
    row8 = lax.broadcasted_iota(jnp.int32, (8, width), 0)

    def step(g, c):
        r = pl.multiple_of((ng - 1 - g if reverse else g) * 8, 8)
        aa = a_ref[pl.ds(r, 8), :]
        bb = b_ref[pl.ds(r, 8), :]
        for s in (1, 2, 4):
            if reverse:
                keep, shift = row8 < 8 - s, 8 - s
            else:
                keep, shift = row8 >= s, s
            a_sh = jnp.where(keep, pltpu.roll(aa, shift, 0), 1.0)
            b_sh = jnp.where(keep, pltpu.roll(bb, shift, 0), 0.0)
            bb = aa * b_sh + bb
            aa = aa * a_sh
        o = aa * c + bb
        o_ref[pl.ds(r, 8), :] = o
        edge = o[0:1, :] if reverse else o[7:8, :]
        return jnp.broadcast_to(edge, (8, width))

    return lax.fori_loop(0, ng, step, carry)


def mix_out(pa, h, p, bm, w3, xf, small):
    t = xf.shape[0]
    tm = _row_tile(t, 256)

    def body(pa_ref, h_ref, yr_ref, gla_ref, glb_ref, bma_ref, bmb_ref, w_ref, x_ref, sm_ref,
             pb_ref, ya_ref, yb_ref, m_ref, xo_ref, xb_ref, xh_ref, rs_ref):
        ge, _ = _gelu(yr_ref[...].astype(F32))
        pb = (h_ref[...] * ge).astype(BF16)
        pb_ref[...] = pb
        ya = _dot(pa_ref[...], w_ref[0], NN)
        yb = _dot(pb, w_ref[1], NN)
        ya_ref[...] = ya
        yb_ref[...] = yb
        ga = _sigmoid(gla_ref[...].astype(F32) + bma_ref[...])
        gb = _sigmoid(glb_ref[...].astype(F32) + bmb_ref[...])
        m = (ga * ya + gb * yb).astype(BF16)
        m_ref[...] = m
        z = ALPHA * x_ref[...] + _dot(m, w_ref[2], NN)
        y, xh, rstd = _ln_fwd(z, sm_ref[R_LNG + 1:R_LNG + 2, :], sm_ref[R_LNB + 1:R_LNB + 2, :])
        xo_ref[...] = y
        xb_ref[...] = y.astype(BF16)
        xh_ref[...] = xh
        rs_ref[...] = rstd

    row = pl.BlockSpec((tm, D), lambda i: (i, 0))
    f32o, b16o = SDS((t, D), F32), SDS((t, D), BF16)
    return pl.pallas_call(
        body, name="mix_out", grid=(t // tm,),
        in_specs=[row, row, _pcol(tm, 4), _pcol(tm, 5), _pcol(tm, 6),
                  pl.BlockSpec((1, D), lambda i: (0, 0)), pl.BlockSpec((1, D), lambda i: (0, 1)),
                  _full((3, D, D)), row, _full((SMALL_ROWS, D))],
        out_specs=(row,) * 7 + (pl.BlockSpec((tm, 1), lambda i: (i, 0)),),
        out_shape=(b16o, f32o, f32o, b16o, f32o, b16o, f32o, SDS((t, 1), F32)),
        compiler_params=_cp("parallel"),
    )(pa, h, p, p, p, bm, bm, w3, xf, small)


def mixb_head(dxn, xh, rstd, small, w3, p, bm, ya, yb, ca, h):
    t = dxn.shape[0]
    tm = _row_tile(t, 256)

    def body(dy_ref, xh_ref, rs_ref, sm_ref, w_ref, bg_ref, yr_ref, gla_ref, glb_ref, bma_ref, bmb_ref,
             ya_ref, yb_ref, ca_ref, h_ref,
             dz_ref, dzb_ref, dya_ref, dyb_ref, dbg_ref, dca_ref, dh_ref, dphi_ref, dln_ref, dbm_ref):
        i = pl.program_id(0)
        dy = dy_ref[...]
        xhat = xh_ref[...]
        dz = _ln_bwd(dy, xhat, rs_ref[...], sm_ref[R_LNG + 1:R_LNG + 2, :])

        @pl.when(i == 0)
        def _():
            dln_ref[...] = jnp.zeros_like(dln_ref)
            dbm_ref[...] = jnp.zeros_like(dbm_ref)

        dln_ref[0:1, :] += jnp.sum(dy * xhat, axis=0, keepdims=True)
        dln_ref[1:2, :] += jnp.sum(dy, axis=0, keepdims=True)
        dz_ref[...] = dz
        dzb = dz.astype(BF16)
        dzb_ref[...] = dzb
        dm = _dot(dzb, w_ref[2], NT)
        ga = _sigmoid(gla_ref[...].astype(F32) + bma_ref[...])
        gb = _sigmoid(glb_ref[...].astype(F32) + bmb_ref[...])
        dya = (dm * ga).astype(BF16)
        dyb = (dm * gb).astype(BF16)
        dya_ref[...] = dya
        dyb_ref[...] = dyb
        dgla = dm * ya_ref[...] * ga * (1.0 - ga)
        dglb = dm * yb_ref[...] * gb * (1.0 - gb)
        dbm_ref[0:1, :] += jnp.sum(dgla, axis=0, keepdims=True)
        dbm_ref[1:2, :] += jnp.sum(dglb, axis=0, keepdims=True)
        dphi_ref[:, D:2 * D] = dgla.astype(BF16)
        dphi_ref[:, 2 * D:3 * D] = dglb.astype(BF16)
        dpa = _dot(dya, w_ref[0], NT)
        dpb = _dot(dyb, w_ref[1], NT)
        dbg_ref[...] = (dpa * ca_ref[...]).astype(BF16)
        dca_ref[...] = dpa * bg_ref[...].astype(F32)
        yr = yr_ref[...].astype(F32)
        ge, th = _gelu(yr)
        dh_ref[...] = dpb * ge
        dphi_ref[:, 0:D] = (dpb * h_ref[...] * _gelu_grad(yr, th)).astype(BF16)

    row = pl.BlockSpec((tm, D), lambda i: (i, 0))
    f32o, b16o = SDS((t, D), F32), SDS((t, D), BF16)
    acc2 = pl.BlockSpec((2, D), lambda i: (0, 0))
    return pl.pallas_call(
        body, name="mixb_head", grid=(t // tm,),
        in_specs=[row, row, pl.BlockSpec((tm, 1), lambda i: (i, 0)), _full((SMALL_ROWS, D)), _full((3, D, D)),
                  _pcol(tm, 0), _pcol(tm, 4), _pcol(tm, 5), _pcol(tm, 6),
                  pl.BlockSpec((1, D), lambda i: (0, 0)), pl.BlockSpec((1, D), lambda i: (0, 1)),
                  row, row, row, row],
        out_specs=(row,) * 7 + (pl.BlockSpec((tm, 3 * D), lambda i: (i, 0)), acc2, acc2),
        out_shape=(f32o, b16o, b16o, b16o, b16o, f32o, f32o, SDS((t, 3 * D), BF16), SDS((2, D), F32), SDS((2, D), F32)),
        compiler_params=_cp("arbitrary"),
    )(dxn, xh, rstd, small, w3, p, p, p, p, bm, bm, ya, yb, ca, h)


def mixb_rec(dh, gr, gi, h, xc, ap, rgw):
    t = dh.shape[0]
    tm = _row_tile(t, 256)
    nt = t // tm

    def body(dh_ref, gr_ref, gi_ref, h_ref, hh_ref, xc_ref, ap_ref, rgw_ref, dg_ref, dxc_ref, red_ref,
             ext, ext_a, c_s, lam_s, lam_c, a_c):
        i = nt - 1 - pl.program_id(0)
        first = i == 0

        @pl.when(pl.program_id(0) == 0)
        def _():
            red_ref[...] = jnp.zeros_like(red_ref)
            lam_c[...] = jnp.zeros_like(lam_c)
            a_c[...] = jnp.zeros_like(a_c)

        ext[0:8, :] = jnp.where(first, 0.0, hh_ref[...])
        ext[8:, :] = h_ref[...]
        hprev = ext[pl.ds(7, tm), :]
        gr = gr_ref[...]
        gi = gi_ref[...]
        xc = xc_ref[...]
        ap = ap_ref[...]
        sp = _softplus(-ap)
        la = (-LRU_C) * gr * sp
        a = jnp.exp(la)
        ext_a[0:tm, :] = a
        ext_a[tm:, :] = a_c[...]
        c_s[...] = ext_a[pl.ds(1, tm), :]
        lam_c[...] = _scan_tile(c_s, dh_ref, lam_s, lam_c[...], tm, reverse=True)
        a_c[...] = jnp.broadcast_to(a[0:1, :], (8, D))
        lam = lam_s[...]
        row = lax.broadcasted_iota(jnp.int32, (tm, D), 0) + i * tm
        start = row == 0
        mult = jnp.where(start, 1.0, jnp.sqrt(_neg_expm1(2.0 * la)))
        dmult = jnp.where(start, 0.0, lam * xc * gi)
        dla = lam * hprev * a - dmult * a * a / mult
        dg1 = (-LRU_C) * sp * dla * gr * (1.0 - gr)
        dg0 = lam * xc * mult * gi * (1.0 - gi)
        dsp = jnp.sum((-LRU_C) * gr * dla, axis=0, keepdims=True)
        red_ref[0:1, :] += jnp.sum(dg0, axis=0, keepdims=True)
        red_ref[1:2, :] += jnp.sum(dg1, axis=0, keepdims=True)
        red_ref[2:3, :] += -dsp * _sigmoid(-ap)
        dg0b = dg0.astype(BF16)
        dg1b = dg1.astype(BF16)
        dg_ref[0] = dg0b
        dg_ref[1] = dg1b
        parts = []
        for hd in range(HEADS):
            sl = slice(hd * HD, (hd + 1) * HD)
            parts.append(_dot(dg0b[:, sl], rgw_ref[0, hd], NT) + _dot(dg1b[:, sl], rgw_ref[1, hd], NT))
        dxc = lam * gi * mult + jnp.concatenate(parts, axis=1)
        dxc_ref[...] = dxc
        red_ref[3:4, :] += jnp.sum(dxc, axis=0, keepdims=True)

    row = pl.BlockSpec((tm, D), lambda i: (nt - 1 - i, 0))
    halo = pl.BlockSpec((8, D), lambda i: (jnp.maximum((nt - 1 - i) * (tm // 8) - 1, 0), 0))
    ext, tile, edge = pltpu.VMEM((tm + 8, D), F32), pltpu.VMEM((tm, D), F32), pltpu.VMEM((8, D), F32)
    return pl.pallas_call(
        body, name="mixb_rec", grid=(nt,),
        in_specs=[row, row, row, row, halo, row, _full((1, D)), _full((2, HEADS, HD, HD))],
        out_specs=(pl.BlockSpec((2, tm, D), lambda i: (0, nt - 1 - i, 0)), row, pl.BlockSpec((8, D), lambda i: (0, 0))),
        out_shape=(SDS((2, t, D), BF16), SDS((t, D), F32), SDS((8, D), F32)),
        scratch_shapes=[ext, ext, tile, tile, edge, edge],
        compiler_params=_cp("arbitrary"),
    )(dh, gr, gi, h, h, xc, ap, rgw)


def mixb_conv(dca, dxc, dbg, p, small):
    t = dca.shape[0]
    tm = _row_tile(t, 256)
    nt = t // tm

    def body(dca_ref, dcan_ref, dxc_ref, dxcn_ref, dbg_ref, cg_ref, v_ref, xr_ref, cgh_ref, vh_ref, xrh_ref, sm_ref,
             dplo_ref, dxr_ref, red_ref, e_dca, e_dxc, e_cv, e_xr):
        i = pl.program_id(0)
        first = i == 0
        last = i == nt - 1

        @pl.when(first)
        def _():
            red_ref[...] = jnp.zeros_like(red_ref)

        dca = dca_ref[...]
        dxc = dxc_ref[...]
        e_dca[0:tm, :] = dca
        e_dca[tm:, :] = jnp.where(last, 0.0, dcan_ref[...])
        e_dxc[0:tm, :] = dxc
        e_dxc[tm:, :] = jnp.where(last, 0.0, dxcn_ref[...])
        cg = cg_ref[...].astype(F32)
        v = v_ref[...].astype(F32)
        xr = xr_ref[...].astype(F32)
        e_cv[0:8, :] = jnp.where(first, 0.0, (cgh_ref[...].astype(F32) * vh_ref[...].astype(F32))[8:16, :])
        e_cv[8:, :] = cg * v
        e_xr[0:8, :] = jnp.where(first, 0.0, xrh_ref[...].astype(F32)[8:16, :])
        e_xr[8:, :] = xr
        dcv = (sm_ref[R_SC + 2:R_SC + 3, :] * dca + sm_ref[R_SC + 1:R_SC + 2, :] * e_dca[pl.ds(1, tm), :]
               + sm_ref[R_SC:R_SC + 1, :] * e_dca[pl.ds(2, tm), :])
        dplo_ref[:, 0:D] = dbg_ref[...]
        dplo_ref[:, D:2 * D] = (dcv * v).astype(BF16)
        dplo_ref[:, 2 * D:3 * D] = (dcv * cg).astype(BF16)
        dxr = (sm_ref[R_RC + 3:R_RC + 4, :] * dxc + sm_ref[R_RC + 2:R_RC + 3, :] * e_dxc[pl.ds(1, tm), :]
               + sm_ref[R_RC + 1:R_RC + 2, :] * e_dxc[pl.ds(2, tm), :] + sm_ref[R_RC:R_RC + 1, :] * e_dxc[pl.ds(3, tm), :])
        dxr_ref[...] = dxr.astype(BF16)
        for k in range(3):
            red_ref[R_SC + k:R_SC + k + 1, :] += jnp.sum(dca * e_cv[pl.ds(6 + k, tm), :], axis=0, keepdims=True)
        for k in range(4):
            red_ref[R_RC + k:R_RC + k + 1, :] += jnp.sum(dxc * e_xr[pl.ds(5 + k, tm), :], axis=0, keepdims=True)

    row = pl.BlockSpec((tm, D), lambda i: (i, 0))
    ext = pltpu.VMEM((tm + 8, D), F32)
    return pl.pallas_call(
        body, name="mixb_conv", grid=(nt,),
        in_specs=[row, _next_halo(tm, t, 0), row, _next_halo(tm, t, 0), row,
                  _pcol(tm, 1), _pcol(tm, 2), _pcol(tm, 3), _prev_halo16(tm, 1), _prev_halo16(tm, 2), _prev_halo16(tm, 3),
                  _full((SMALL_ROWS, D))],
        out_specs=(pl.BlockSpec((tm, 3 * D), lambda i: (i, 0)), row, pl.BlockSpec((8, D), lambda i: (0, 0))),
        out_shape=(SDS((t, 3 * D), BF16), SDS((t, D), BF16), SDS((8, D), F32)),
        scratch_shapes=[ext, ext, ext, ext],
        compiler_params=_cp("arbitrary"),
    )(dca, dca, dxc, dxc, dbg, p, p, p, p, p, p, small)


def mixb_dx(dz, dplo, dxr, dphi, win):
    t = dz.shape[0]
    tm = _row_tile(t, 512)

    def body(dz_ref, lo_ref, xr_ref, hi_ref, w_ref, dx_ref, acc):
        k = pl.program_id(1)

        @pl.when(k == 0)
        def _():
            acc[...] = ALPHA * dz_ref[...]

        @pl.when(k < 3)
        def _():
            acc[...] += _dot(lo_ref[...], w_ref[...], NT)

        @pl.when(k == 3)
        def _():
            acc[...] += _dot(xr_ref[...], w_ref[...], NT)

        @pl.when(k > 3)
        def _():
            acc[...] += _dot(hi_ref[...], w_ref[...], NT)

        @pl.when(k == 6)
        def _():
            dx_ref[...] = acc[...]

    row = pl.BlockSpec((tm, D), lambda i, k: (i, 0))
    return pl.pallas_call(
        body, name="mixb_dx", grid=(t // tm, 7),
        in_specs=[row, pl.BlockSpec((tm, D), lambda i, k: (i, jnp.minimum(k, 2))), row,
                  pl.BlockSpec((tm, D), lambda i, k: (i, jnp.clip(k - 4, 0, 2))),
                  pl.BlockSpec((D, D), lambda i, k: (0, k))],
        out_specs=row, out_shape=SDS((t, D), F32),
        scratch_shapes=[pltpu.VMEM((tm, D), F32)],
        compiler_params=_cp("parallel", "arbitrary"),
    )(dz, dplo, dxr, dphi, win)


def mixb_dwin(xb, dplo, dxr, dphi):
    t = xb.shape[0]
    tk = _row_tile(t, 2048)
    nk = t // tk

    def body(x_ref, lo_ref, xr_ref, hi_ref, o_ref, acc):
        n = pl.program_id(0)
        k = pl.program_id(1)

        @pl.when(k == 0)
        def _():
            acc[...] = jnp.zeros_like(acc)

        @pl.when(n < 3)
        def _():
            acc[...] += _dot(x_ref[...], lo_ref[...], TN)

        @pl.when(n == 3)
        def _():
            acc[...] += _dot(x_ref[...], xr_ref[...], TN)

        @pl.when(n > 3)
        def _():
            acc[...] += _dot(x_ref[...], hi_ref[...], TN)

        @pl.when(k == nk - 1)
        def _():
            o_ref[...] = acc[...].astype(BF16)

    return pl.pallas_call(
        body, name="mixb_dwin", grid=(7, nk),
        in_specs=[pl.BlockSpec((tk, D), lambda n, k: (k, 0)),
                  pl.BlockSpec((tk, D), lambda n, k: (jnp.where(n < 3, k, 0), jnp.minimum(n, 2))),
                  pl.BlockSpec((tk, D), lambda n, k: (jnp.where(n == 3, k, 0), 0)),
                  pl.BlockSpec((tk, D), lambda n, k: (jnp.where(n > 3, k, 0), jnp.clip(n - 4, 0, 2)))],
        out_specs=pl.BlockSpec((D, D), lambda n, k: (0, n)), out_shape=SDS((D, D_IN), BF16),
        scratch_shapes=[pltpu.VMEM((D, D), F32)],
        compiler_params=_cp("parallel", "arbitrary"),
    )(xb, dplo, dxr, dphi)


def mixb_drgw(xcb, dg):
    t = xcb.shape[0]
    return _mm_tn("mixb_drgw", xcb, pl.BlockSpec((t, HD), lambda g, h: (0, h)),
                  dg, pl.BlockSpec((None, t, HD), lambda g, h: (g, 0, h)),
                  SDS((2, HEADS, HD, HD), BF16), pl.BlockSpec((None, None, HD, HD), lambda g, h: (g, h, 0, 0)),
                  (2, HEADS))


def loss_head(y, tgt):
    t = y.shape[0]
    tm = _row_tile(t, 512)

    def body(y_ref, t_ref, dy_ref, l_ref):
        i = pl.program_id(0)
        e = y_ref[...] - t_ref[...]
        dy_ref[...] = e * (1.0 / D)

        @pl.when(i == 0)
        def _():
            l_ref[...] = jnp.zeros_like(l_ref)

        l_ref[...] += 0.5 * jnp.sum(jnp.mean(e * e, axis=-1, keepdims=True), axis=0, keepdims=True)

    row = pl.BlockSpec((tm, D), lambda i: (i, 0))
    return pl.pallas_call(
        body, name="loss_head", grid=(t // tm,), in_specs=[row, row],
        out_specs=(row, pl.BlockSpec((8, 128), lambda i: (0, 0))),
        out_shape=(SDS((t, D), F32), SDS((8, 128), F32)),
        compiler_params=_cp("arbitrary"),
    )(y, tgt)


def _adamw(w, g, m, v):
    m = ADAM_B1 * m + (1.0 - ADAM_B1) * g
    v = ADAM_B2 * v + (1.0 - ADAM_B2) * (g * g)
    m_hat = m / (1.0 - ADAM_B1 ** ADAM_STEP)
    v_hat = v / (1.0 - ADAM_B2 ** ADAM_STEP)
    delta = -ADAM_LR * (m_hat / (jnp.sqrt(v_hat) + ADAM_EPS) + ADAM_WD * w)
    return delta, m, v


def adam_big(name, w, m, v, parts, rows, lanes, blk_off, tr, l0, prev=None):
    nr = rows // tr
    nl = len(parts)

    def body(w_ref, m_ref, v_ref, *rest):
        g_ref, d_ref, mo_ref, vo_ref = rest[-4:]
        l = pl.program_id(0)
        for ll in range(nl):
            pr = rest[ll]

            @pl.when(l == ll)
            def _():
                g = pr[0].astype(F32)
                for s in range(1, N_DEV):
                    g = g + pr[s].astype(F32)
                g_ref[...] = g
                d, mn, vn = _adamw(w_ref[...], g, m_ref[...], v_ref[...])
                d_ref[...] = d
                mo_ref[...] = mn
                vo_ref[...] = vn

    blk = pl.BlockSpec((None, tr, lanes), lambda l, r: (l + l0, r, 0))

    def part_spec(ll):
        return pl.BlockSpec((N_DEV, tr, lanes), lambda l, r: (0, jnp.where(l == ll, r, 0) + blk_off, 0))

    out = SDS((DEPTH, rows, lanes), F32)
    extra = [] if prev is None else list(prev)
    return pl.pallas_call(
        body, name=name, grid=(nl, nr),
        in_specs=[blk, blk, blk] + [part_spec(ll) for ll in range(nl)] + [ANY] * len(extra),
        out_specs=(blk,) * 4, out_shape=(out,) * 4,
        input_output_aliases={3 + nl + i: i for i in range(len(extra))},
        compiler_params=_cp("parallel", "parallel"),
    )(w, m, v, *parts, *extra)


def sum_small(land):
    def body(l_ref, o_ref):
        g = l_ref[0]
        for s in range(1, N_DEV):
            g = g + l_ref[s]
        o_ref[...] = g

    return pl.pallas_call(body, name="sum_small", out_shape=SDS(land.shape[1:], F32))(land)


def adam_small(name, w, g, m, v):
    def body(w_ref, g_ref, m_ref, v_ref, d_ref, mo_ref, vo_ref):
        d, mn, vn = _adamw(w_ref[...], g_ref[...], m_ref[...], v_ref[...])
        d_ref[...] = d
        mo_ref[...] = mn
        vo_ref[...] = vn

    out = SDS(w.shape, F32)
    return pl.pallas_call(body, name=name, out_shape=(out, out, out))(w, g, m, v)


def _ffn_forward(xf, xb, gw, f, s):
    g, u, a = ffn_up(xb, gw["w1"], f)
    xo, xob, xh, rs = ffn_down_ln(a, gw["w2"], f, xf, gw["small"], s)
    return (xo, xob), dict(xb=xb, g=g, u=u, a=a, xh=xh, rs=rs)


def _ffn_backward(dxn, sv, gw, f, s):
    dz, df, dgu, dln = ffn_bwd_gates(dxn, sv["xh"], sv["rs"], gw["small"], s, gw["w2"], f, sv["g"], sv["u"])
    dx = ffn_bwd_dx(dz, dgu, gw["w1"], f)
    dw1 = ffn_dw1(sv["xb"], dgu, f)
    dw2 = ffn_dw2(sv["a"], df, f)
    return dx, dw1, dw2, dln


def _mixer_forward(xf, xb, gw, rcb, ap, bm):
    p = mix_proj(xb, gw["win"])
    ca, pa, xc, xcb, gi, gr, h = mix_pre(p, gw["small"], rcb, ap, gw["rgw"])
    pb, ya, yb, m, xo, xob, xh, rs = mix_out(pa, h, p, bm, gw["w3"], xf, gw["small"])
    sv = dict(xb=xb, p=p, ca=ca, pa=pa, xc=xc, xcb=xcb, gi=gi, gr=gr, h=h, pb=pb, ya=ya, yb=yb, m=m, xh=xh, rs=rs)
    return (xo, xob), sv


def _mixer_backward(dxn, sv, gw, rcb, ap, bm):
    dz, dzb, dya, dyb, dbg, dca, dh, dphi, dln, dbm = mixb_head(
        dxn, sv["xh"], sv["rs"], gw["small"], gw["w3"], sv["p"], bm, sv["ya"], sv["yb"], sv["ca"], sv["h"])
    dg, dxc, red_rec = mixb_rec(dh, sv["gr"], sv["gi"], sv["h"], sv["xc"], ap, gw["rgw"])
    dplo, dxr, red_conv = mixb_conv(dca, dxc, dbg, sv["p"], gw["small"])
    dx = mixb_dx(dz, dplo, dxr, dphi, gw["win"])
    dwin = mixb_dwin(sv["xb"], dplo, dxr, dphi)
    dwo = mm_tn_square("mixb_dwo", sv["m"], dzb)
    dwoc = mm_tn_square("mixb_dwoc", sv["pa"], dya)
    dwor = mm_tn_square("mixb_dwor", sv["pb"], dyb)
    drgw = mixb_drgw(sv["xcb"], dg)
    return dx, dict(dwin=dwin, dwoc=dwoc, dwor=dwor, dwo=dwo, drgw=drgw), dln, dbm, red_rec, red_conv


def kernel(x, w_in, b_merge, sc_w, rc_w, rc_b, rg_w, rg_b, a_param, w_out_conv, w_out_rnn, w_o, ffn_w1, ffn_w2, ln_g, ln_b, loss_target, m_w_in, m_b_merge, m_sc_w, m_rc_w, m_rc_b, m_rg_w, m_rg_b, m_a_param, m_w_out_conv, m_w_out_rnn, m_w_o, m_ffn_w1, m_ffn_w2, m_ln_g, m_ln_b, v_w_in, v_b_merge, v_sc_w, v_rc_w, v_rc_b, v_rg_w, v_rg_b, v_a_param, v_w_out_conv, v_w_out_rnn, v_w_o, v_ffn_w1, v_ffn_w2, v_ln_g, v_ln_b):
    t = x.shape[1]
    me = _me()

    def rows(parts, total):
        out, off = None, 0
        for part in parts:
            r = part.shape[-2]
            pad = [(0, 0)] * (part.ndim - 2) + [(off, total - off - r), (0, 0)]
            padded = jnp.pad(part, pad)
            out = padded if out is None else out + padded
            off += r
        return out

    def pack_small(sc, rc, rgb, lng, lnb):
        return rows([sc, rc, rgb, lng, lnb], SMALL_ROWS)

    def pack_rep(bmv, rcbv, apv):
        return rows([bmv.reshape(DEPTH, 2, D), rcbv[:, None], apv[:, None]], 8)

    small = pack_small(sc_w, rc_w, rg_b, ln_g, ln_b)

    me1 = jnp.reshape(me, (1,)).astype(jnp.int32)

    def layer_shards(l):
        return gather_prep(l, me1, w_in, w_out_conv, w_out_rnn, w_o, ffn_w1, ffn_w2, rg_w, small)

    def as_weights(lands):
        return dict(zip(("win", "w3", "w1", "w2", "rgw", "small"), lands))

    xf = x.reshape(t, D)
    cur = (xf, xf.astype(BF16))
    gathers = [ChipGather(f"gather{l}") for l in range(DEPTH)]
    shards, own = layer_shards(0)
    sems, shards, own, cur = gathers[0].first(shards, own, cur)
    relay_sems, shards, own, cur = gathers[0].relay(sems, shards, own, cur)
    lands, cur = gathers[0].last(sems, relay_sems, shards, own, cur)
    gws, saved = [as_weights(lands)], []
    for l in range(DEPTH):
        gw = gws[l]
        nxt = l + 1 < DEPTH
        if nxt:
            shards, own = layer_shards(l + 1)
            sems, shards, own, cur = gathers[l + 1].first(shards, own, cur)
        rcb, ap, bm = rc_b[l][None], a_param[l][None], b_merge[l][None]
        cur, s0 = _ffn_forward(cur[0], cur[1], gw, 0, 0)
        cur, s1 = _mixer_forward(cur[0], cur[1], gw, rcb, ap, bm)
        if nxt:
            relay_sems, shards, own, cur = gathers[l + 1].relay(sems, shards, own, cur)
        cur, s2 = _ffn_forward(cur[0], cur[1], gw, 1, 2)
        saved.append((s0, s1, s2))
        if nxt:
            lands, cur = gathers[l + 1].last(sems, relay_sems, shards, own, cur)
            gws.append(as_weights(lands))

    dy, loss_tile = loss_head(cur[0], loss_target.reshape(t, D))
    loss = lax.psum(loss_tile[0, 0], ("x", "y", "c"))

    lands = [None] * DEPTH
    gsmall = [None] * DEPTH
    scatters = [grad_scatter(l) for l in range(DEPTH)]
    flight = None
    for l in reversed(range(DEPTH)):
        gw = gws[l]
        rcb, ap, bm = rc_b[l][None], a_param[l][None], b_merge[l][None]
        s0, s1, s2 = saved[l]
        dy, dw1b, dw2b, dln2 = _ffn_backward(dy, s2, gw, 1, 2)
        dy, dmix, dln1, dbm, red_rec, red_conv = _mixer_backward(dy, s1, gw, rcb, ap, bm)
        dy, dw1a, dw2a, dln0 = _ffn_backward(dy, s0, gw, 0, 0)
        if flight is not None:
            lands[l + 1], (dy,) = scatters[l + 1].wait(flight[0], flight[1], flight[2], flight[3], [dy])
        grads = [dmix["dwin"], dmix["dwoc"], dmix["dwor"], dmix["dwo"], dw1a, dw1b, dw2a, dw2b, dmix["drgw"]]
        flight = scatters[l].start(grads, scatter_prep(l, me1, grads), [dy])
        dy = flight[4][0]
        gsmall[l] = rows([red_conv[0:7], red_rec[0:2], dln0[0:1], dln1[0:1], dln2[0:1], dln0[1:2], dln1[1:2],
                          dln2[1:2], dbm, red_rec[3:4], red_rec[2:3]], GRAD_ROWS)
    grad_x = dy.reshape(1, t, D)

    gfull = sum_small(gather_small_grads(jnp.stack(gsmall, axis=0)))

    def parts_of(ls):
        return dict(w_in=[ls[l][0] for l in range(len(ls))],
                    w3=[ls[l][1].reshape(N_DEV, 3 * 128, D) for l in range(len(ls))],
                    w1=[ls[l][2].reshape(N_DEV, 2 * D, FS) for l in range(len(ls))],
                    w2=[ls[l][3].reshape(N_DEV, 2 * W2S, D) for l in range(len(ls))],
                    rgw=[ls[l][4].reshape(N_DEV, 2 * HEADS * 32, HD) for l in range(len(ls))])

    families = [("w_in", w_in, m_w_in, v_w_in, "w_in", D, WIN_S, 0, 128),
                ("w_out_conv", w_out_conv, m_w_out_conv, v_w_out_conv, "w3", 128, D, 0, 128),
                ("w_out_rnn", w_out_rnn, m_w_out_rnn, v_w_out_rnn, "w3", 128, D, 1, 128),
                ("w_o", w_o, m_w_o, v_w_o, "w3", 128, D, 2, 128),
                ("ffn_w1", ffn_w1, m_ffn_w1, v_ffn_w1, "w1", 2 * D, FS, 0, 256),
                ("ffn_w2", ffn_w2, m_ffn_w2, v_ffn_w2, "w2", 2 * W2S, D, 0, 2 * W2S // 4),
                ("rg_w", rg_w, m_rg_w, v_rg_w, "rgw", 2 * HEADS * 32, HD, 0, 256)]

    def adam_pass(tag, parts, l0, prev):
        outs = {}
        for name, w, m, v, fam, nrow, lanes, blk_off, tr in families:
            r3 = lambda a: a.reshape(DEPTH, nrow, lanes)
            outs[name] = adam_big(f"adam_{name}_{tag}", r3(w), r3(m), r3(v), parts[fam], nrow, lanes, blk_off, tr, l0,
                                  None if prev is None else prev[name])
        return outs

    first = adam_pass("rest", parts_of(lands[1:]), 1, None)
    thru = [first[f[0]][0] for f in families] + [gfull]
    lands[0], thru = scatters[0].wait(flight[0], flight[1], flight[2], flight[3], thru)
    gfull = thru[-1]
    prev = {f[0]: [thru[i]] + list(first[f[0]][1:]) for i, f in enumerate(families)}
    final = adam_pass("first", parts_of(lands[0:1]), 0, prev)
    res = {f[0]: [o.reshape(f[1].shape) for o in final[f[0]]] for f in families}

    g_sh = lax.dynamic_slice(gfull, (0, 0, me * 128), (DEPTH, SMALL_ROWS, 128))
    g_rep = gfull[:, G_BM:G_BM + 8, :]
    d_sh, m_sh, v_sh = adam_small("adam_small_sharded", small, g_sh,
                                  pack_small(m_sc_w, m_rc_w, m_rg_b, m_ln_g, m_ln_b),
                                  pack_small(v_sc_w, v_rc_w, v_rg_b, v_ln_g, v_ln_b))
    d_rep, m_rep, v_rep = adam_small("adam_small_replicated", pack_rep(b_merge, rc_b, a_param), g_rep,
                                     pack_rep(m_b_merge, m_rc_b, m_a_param), pack_rep(v_b_merge, v_rc_b, v_a_param))

    def unpack_sh(a):
        return dict(sc_w=a[:, 0:3], rc_w=a[:, 3:7], rg_b=a[:, 7:9], ln_g=a[:, 9:12], ln_b=a[:, 12:15])

    def unpack_rep(a):
        return dict(b_merge=a[:, 0:2].reshape(DEPTH, 2 * D), rc_b=a[:, 2], a_param=a[:, 3])

    gsh, grep = unpack_sh(g_sh), unpack_rep(g_rep)
    dsh, drep = unpack_sh(d_sh), unpack_rep(d_rep)
    msh, mrep = unpack_sh(m_sh), unpack_rep(m_rep)
    vsh, vrep = unpack_sh(v_sh), unpack_rep(v_rep)
    for n in ("sc_w", "rc_w", "rg_b", "ln_g", "ln_b"):
        res[n] = [gsh[n], dsh[n], msh[n], vsh[n]]
    for n in ("b_merge", "rc_b", "a_param"):
        res[n] = [grep[n], drep[n], mrep[n], vrep[n]]

    names = ["w_in", "b_merge", "sc_w", "rc_w", "rc_b", "rg_w", "rg_b", "a_param", "w_out_conv", "w_out_rnn", "w_o",
             "ffn_w1", "ffn_w2", "ln_g", "ln_b"]
    out = [loss, grad_x]
    for k in range(4):
        out += [res[n][k] for n in names]
    return tuple(out)
```

```python
import functools

import jax
import jax.numpy as jnp
from jax import lax
from jax.experimental import pallas as pl
from jax.experimental.pallas import tpu as pltpu

F32 = jnp.float32
BF16 = jnp.bfloat16
SDS = jax.ShapeDtypeStruct

N_DEV = 8
DEPTH = 4
D = 1024
D_FF = 2816
FS = D_FF // 4
W2S = D_FF // 8
D_IN = 7 * D
WIN_S = D_IN // 8
HEADS = 4
HD = D // HEADS
LRU_C = 8.0
ALPHA = (2.0 * DEPTH) ** 0.25
LN_EPS = 1e-5
ADAM_LR, ADAM_B1, ADAM_B2, ADAM_EPS, ADAM_WD, ADAM_STEP = 0.001, 0.9, 0.999, 1e-08, 0.01, 10

R_SC, R_RC, R_RGB, R_LNG, R_LNB = 0, 3, 7, 9, 12
SMALL_ROWS = 16
G_BM, G_RCB, G_AP = 15, 17, 18
GRAD_ROWS = 24

NN = ((1,), (0,))
NT = ((1,), (1,))
TN = ((0,), (0,))
MESH = pl.DeviceIdType.MESH
ANY = pl.BlockSpec(memory_space=pl.ANY)
VMEM_LIMIT = 52 * 1024 * 1024


def _dot(a, b, dims):
    return lax.dot_general(a, b, (dims, ((), ())), preferred_element_type=F32)


def _cp(*sem):
    return pltpu.CompilerParams(dimension_semantics=sem, vmem_limit_bytes=VMEM_LIMIT)


def _sigmoid(x):
    return 1.0 / (1.0 + jnp.exp(-x))


def _log1p(e):
    u = 1.0 + e
    return jnp.where(u == 1.0, e, jnp.log(u) * e / jnp.where(u == 1.0, 1.0, u - 1.0))


def _softplus(x):
    return jnp.maximum(x, 0.0) + _log1p(jnp.exp(-jnp.abs(x)))


def _neg_expm1(x):
    u = jnp.exp(x)
    um1 = u - 1.0
    safe = jnp.logical_and(u != 1.0, um1 != -1.0)
    r = um1 * x / jnp.where(safe, jnp.log(jnp.where(safe, u, 0.5)), 1.0)
    return -jnp.where(u == 1.0, x, jnp.where(um1 == -1.0, -1.0, r))


def _gelu(y):
    c = 0.7978845608028654
    t = jnp.tanh(c * (y + 0.044715 * y * y * y))
    return 0.5 * y * (1.0 + t), t


def _gelu_grad(y, t):
    c = 0.7978845608028654
    return 0.5 * (1.0 + t) + 0.5 * y * (1.0 - t * t) * c * (1.0 + 3.0 * 0.044715 * y * y)


def _ln_fwd(z, g, b):
    mu = jnp.mean(z, axis=-1, keepdims=True)
    zc = z - mu
    var = jnp.mean(zc * zc, axis=-1, keepdims=True)
    rstd = lax.rsqrt(var + LN_EPS)
    xh = zc * rstd
    return xh * g + b, xh, rstd


def _ln_bwd(dy, xh, rstd, g):
    dxh = dy * g
    m1 = jnp.mean(dxh, axis=-1, keepdims=True)
    m2 = jnp.mean(dxh * xh, axis=-1, keepdims=True)
    return rstd * (dxh - m1 - xh * m2)


def _row_tile(t, want):
    return min(want, t)


def _me():
    return 4 * lax.axis_index("x") + 2 * lax.axis_index("y") + lax.axis_index("c")


def _coords(p):
    return (p // 4, (p // 2) % 2, p % 2)


def _all_to_all_copies(srcs_of, dsts_of, waits, send_sems, recv_sems, loc_sems):
    me = _me()
    n = len(waits)
    own_src, own_dst = srcs_of(me), dsts_of(me)
    local = [pltpu.make_async_copy(own_src[k], own_dst[k], loc_sems.at[k]) for k in range(n)]
    for cp in local:
        cp.start()
    for d in range(1, N_DEV):
        p = (me + d) % N_DEV
        src, dst = srcs_of(p), dsts_of(me)
        for k in range(n):
            pltpu.make_async_remote_copy(
                src_ref=src[k], dst_ref=dst[k], send_sem=send_sems.at[waits[k][1]],
                recv_sem=recv_sems.at[waits[k][1]], device_id=_coords(p), device_id_type=MESH).start()
    done = set()
    for k in range(n):
        ref, s = waits[k]
        if s in done:
            continue
        done.add(s)
        pltpu.make_async_remote_copy(
            src_ref=ref, dst_ref=ref, send_sem=send_sems.at[s], recv_sem=recv_sems.at[s],
            device_id=_coords(me), device_id_type=MESH).wait()
    for cp in local:
        cp.wait()


HBM = pl.BlockSpec(memory_space=pltpu.HBM)
SEM = pl.BlockSpec(memory_space=pltpu.SEMAPHORE)
EFFECT = pltpu.SideEffectType.DATAFLOW_SIDE_EFFECTING


def _in_hbm(a):
    return pltpu.with_memory_space_constraint(a, pltpu.HBM)


class Exchange:
    def __init__(self, name, src_of, dst_of, sem_of, span_of):
        self.name, self.src_of, self.dst_of, self.sem_of, self.span_of = name, src_of, dst_of, sem_of, span_of
        self.nsem = max(sem_of) + 1

    def start(self, srcs, lands, thru):
        n, m = len(srcs), len(lands)
        ops = list(srcs) + list(lands) + list(thru)

        def body(*refs):
            src_refs, land_refs = refs[:n], refs[n:n + m]
            send_sems, recv_sems = refs[len(ops)], refs[len(ops) + 1]
            me = _me()
            for dd in range(1, N_DEV):
                p = (me + dd) % N_DEV
                s, d = self.src_of(src_refs, p), self.dst_of(land_refs, me)
                for k in range(len(self.sem_of)):
                    pltpu.make_async_remote_copy(
                        src_ref=s[k], dst_ref=d[k], send_sem=send_sems.at[self.sem_of[k]],
                        recv_sem=recv_sems.at[self.sem_of[k]], device_id=_coords(p), device_id_type=MESH).start()

        sem = pltpu.SemaphoreType.DMA((self.nsem,))
        res = pl.pallas_call(
            body, name=self.name + "_start",
            out_shape=(sem, sem) + tuple(pltpu.HBM(a.shape, a.dtype) for a in ops),
            in_specs=[HBM] * len(ops), out_specs=(SEM, SEM) + (HBM,) * len(ops),
            input_output_aliases={i: 2 + i for i in range(len(ops))},
            compiler_params=pltpu.CompilerParams(has_side_effects=EFFECT),
        )(*[_in_hbm(a) for a in ops])
        return res[0], res[1], res[2:2 + n], res[2 + n:2 + n + m], list(res[2 + n + m:])

    def wait(self, send_sems, recv_sems, srcs, lands, thru):
        n, m = len(srcs), len(lands)
        ops = list(srcs) + list(lands) + list(thru)

        def body(*refs):
            land_refs = refs[n:n + m]
            ssem, rsem = refs[len(ops)], refs[len(ops) + 1]
            me = _me()
            spans = self.span_of(land_refs)
            for s in range(self.nsem):
                cp = pltpu.make_async_remote_copy(
                    src_ref=spans[s], dst_ref=spans[s], send_sem=ssem.at[s], recv_sem=rsem.at[s],
                    device_id=_coords(me), device_id_type=MESH)
                cp.wait_send()
                cp.wait_recv()

        res = pl.pallas_call(
            body, name=self.name + "_wait",
            out_shape=tuple(pltpu.HBM(a.shape, a.dtype) for a in ops),
            in_specs=[HBM] * len(ops) + [SEM, SEM], out_specs=(HBM,) * len(ops),
            input_output_aliases={i: i for i in range(len(ops))},
            compiler_params=pltpu.CompilerParams(has_side_effects=EFFECT),
        )(*ops, send_sems, recv_sems)
        return res[n:n + m], list(res[n + m:])


class ChipGather:
    NARR = 6

    def __init__(self, name):
        self.name = name

    @staticmethod
    def _slots(lands, p):
        o_win, o_w3, o_w1, o_w2, o_rgw, o_sm = lands
        return [o_win.at[:, pl.ds(pl.multiple_of(p * WIN_S, 128), WIN_S)],
                o_w3.at[:, pl.ds(pl.multiple_of(p * 128, 128), 128), :],
                o_w1.at[:, p],
                o_w2.at[:, pl.ds(pl.multiple_of(p * W2S, 16), W2S), :],
                o_rgw.at[:, :, pl.ds(pl.multiple_of(p * 32, 32), 32), :],
                o_sm.at[:, pl.ds(pl.multiple_of(p * 128, 128), 128)]]

    @staticmethod
    def _places():
        x, y, c = lax.axis_index("x"), lax.axis_index("y"), lax.axis_index("c")
        chips = [(1 - x, y), (x, 1 - y), (1 - x, 1 - y)]
        return (x, y, c), (x, y, 1 - c), chips

    @staticmethod
    def _index(place):
        return 4 * place[0] + 2 * place[1] + place[2]

    def _call(self, tag, body, ops, sems_in, sems_out):
        nops = len(ops)
        sem = pltpu.SemaphoreType.DMA((max(sems_out, 1),))
        res = pl.pallas_call(
            body, name=f"{self.name}_{tag}",
            out_shape=((sem, sem) if sems_out else ()) + tuple(pltpu.HBM(a.shape, a.dtype) for a in ops),
            in_specs=[HBM] * nops + [SEM] * len(sems_in),
            out_specs=((SEM, SEM) if sems_out else ()) + (HBM,) * nops,
            input_output_aliases={i: (2 if sems_out else 0) + i for i in range(nops)},
            compiler_params=pltpu.CompilerParams(has_side_effects=EFFECT),
        )(*[_in_hbm(a) for a in ops], *sems_in)
        return res

    def first(self, srcs, lands, thru):
        n = self.NARR
        ops = list(srcs) + list(lands) + list(thru)

        def body(*refs):
            src_refs, land_refs = refs[:n], refs[n:2 * n]
            send_sems, recv_sems = refs[len(ops)], refs[len(ops) + 1]
            me, sibling, chips = self._places()
            dst = self._slots(land_refs, self._index(me))
            targets = [sibling] + [(cx, cy, me[2]) for cx, cy in chips]
            for r, to in enumerate(targets):
                for k in range(n):
                    pltpu.make_async_remote_copy(
                        src_ref=src_refs[k], dst_ref=dst[k], send_sem=send_sems.at[6 * r + k],
                        recv_sem=recv_sems.at[6 * r + k], device_id=to, device_id_type=MESH).start()

        res = self._call("first", body, ops, [], 4 * n)
        return (res[0], res[1]), list(res[2:2 + n]), list(res[2 + n:2 + 2 * n]), list(res[2 + 2 * n:])

    def relay(self, sems, srcs, lands, thru):
        n = self.NARR
        ops = list(srcs) + list(lands) + list(thru)

        def wait_body(*refs):
            land_refs = refs[n:2 * n]
            ssem, rsem = refs[len(ops)], refs[len(ops) + 1]
            me, sibling, chips = self._places()
            for j, (cx, cy) in enumerate(chips):
                got = self._slots(land_refs, self._index((cx, cy, me[2])))
                for k in range(n):
                    pltpu.make_async_remote_copy(
                        src_ref=got[k], dst_ref=got[k], send_sem=ssem.at[6 * (1 + j) + k], recv_sem=rsem.at[6 * (1 + j) + k],
                        device_id=me, device_id_type=MESH).wait_recv()

        ops = list(self._call("landed", wait_body, ops, list(sems), 0))

        def start_body(*refs):
            land_refs = refs[n:2 * n]
            ssem, rsem = refs[len(ops)], refs[len(ops) + 1]
            me, sibling, chips = self._places()
            for j, (cx, cy) in enumerate(chips):
                got = self._slots(land_refs, self._index((cx, cy, me[2])))
                for k in range(n):
                    pltpu.make_async_remote_copy(
                        src_ref=got[k], dst_ref=got[k], send_sem=ssem.at[6 * j + k], recv_sem=rsem.at[6 * j + k],
                        device_id=sibling, device_id_type=MESH).start()

        res = self._call("relay", start_body, ops, [], 3 * n)
        return (res[0], res[1]), list(res[2:2 + n]), list(res[2 + n:2 + 2 * n]), list(res[2 + 2 * n:])

    def last(self, sems, relay_sems, srcs, lands, thru):
        n = self.NARR
        ops = list(srcs) + list(lands) + list(thru)

        def body(*refs):
            src_refs, land_refs = refs[:n], refs[n:2 * n]
            ssem, rsem, ssem2, rsem2 = refs[len(ops):len(ops) + 4]
            me, sibling, chips = self._places()
            got = self._slots(land_refs, self._index(sibling))
            for k in range(n):
                pltpu.make_async_remote_copy(
                    src_ref=got[k], dst_ref=got[k], send_sem=ssem.at[k], recv_sem=rsem.at[k],
                    device_id=me, device_id_type=MESH).wait_recv()
            for j, (cx, cy) in enumerate(chips):
                got = self._slots(land_refs, self._index((cx, cy, 1 - me[2])))
                for k in range(n):
                    pltpu.make_async_remote_copy(
                        src_ref=got[k], dst_ref=got[k], send_sem=ssem2.at[6 * j + k], recv_sem=rsem2.at[6 * j + k],
                        device_id=me, device_id_type=MESH).wait_recv()
            for sem_s, sem_r, count in ((ssem, rsem, 4), (ssem2, rsem2, 3)):
                for r in range(count):
                    for k in range(n):
                        pltpu.make_async_remote_copy(
                            src_ref=src_refs[k], dst_ref=src_refs[k], send_sem=sem_s.at[6 * r + k],
                            recv_sem=sem_r.at[6 * r + k], device_id=me, device_id_type=MESH).wait_send()

        res = self._call("last", body, ops, list(sems) + list(relay_sems), 0)
        return list(res[n:2 * n]), list(res[2 * n:])


GATHER_SHAPES = (SDS((D, D_IN), BF16), SDS((3, D, D), BF16), SDS((2, N_DEV, D, FS), BF16),
                 SDS((2, D_FF, D), BF16), SDS((2, HEADS, HD, HD), BF16), SDS((SMALL_ROWS, D), F32))


def _slab(p, n, align):
    return pl.ds(pl.multiple_of(p * n, align), n)


class GradGroup:
    def __init__(self, nsrc, lands, src_of, dst_of, sem_of, in_specs, out_specs, copy):
        self.nsrc, self.lands, self.src_of, self.dst_of, self.sem_of = nsrc, lands, src_of, dst_of, sem_of
        self.in_specs, self.out_specs, self.copy = in_specs, out_specs, copy


def _mixer_group():
    hd = D // 2

    def src_of(refs, p):
        dwin, dwoc, dwor, dwo, drgw = refs
        r3 = _slab(p, 128, 128)
        return [dwin.at[:, _slab(p, WIN_S, 128)], dwoc.at[r3, :], dwor.at[r3, :], dwo.at[r3, :],
                drgw.at[:, :, _slab(p, 32, 32), :]]

    def dst_of(lands, p):
        l_win, l_w3, l_rgw = lands
        return [l_win.at[p], l_w3.at[p, 0], l_w3.at[p, 1], l_w3.at[p, 2], l_rgw.at[p]]

    def copy(srcs, lands):
        lands[0][...] = srcs[0][...]
        for k in range(3):
            lands[1][k] = srcs[1 + k][...]
        lands[2][...] = srcs[4][...]

    three = pl.BlockSpec((64, D), lambda i, me: (2 * me[0] + i, 0))
    return GradGroup(
        5, (SDS((N_DEV, D, WIN_S), BF16), SDS((N_DEV, 3, 128, D), BF16), SDS((N_DEV, 2, HEADS, 32, HD), BF16)),
        src_of, dst_of, [0, 1, 1, 1, 2],
        [pl.BlockSpec((hd, WIN_S), lambda i, me: (i, me[0])), three, three, three,
         pl.BlockSpec((2, HEADS, 16, HD), lambda i, me: (0, 0, 2 * me[0] + i, 0))],
        [pl.BlockSpec((None, hd, WIN_S), lambda i, me: (me[0], i, 0)),
         pl.BlockSpec((None, 3, 64, D), lambda i, me: (me[0], 0, i, 0)),
         pl.BlockSpec((None, 2, HEADS, 16, HD), lambda i, me: (me[0], 0, 0, i, 0))],
        copy)


def _ffn_group():
    hd, hw = D // 2, W2S // 2

    def src_of(refs, p):
        return [refs[0].at[p], refs[1].at[_slab(p, W2S, 16), :]]

    def dst_of(lands, p):
        return [lands[0].at[p], lands[1].at[p]]

    def copy(srcs, lands):
        lands[0][...] = srcs[0][...]
        lands[1][...] = srcs[1][...]

    return GradGroup(
        2, (SDS((N_DEV, D, FS), BF16), SDS((N_DEV, W2S, D), BF16)), src_of, dst_of, [0, 1],
        [pl.BlockSpec((None, hd, FS), lambda i, me: (me[0], i, 0)), pl.BlockSpec((hw, D), lambda i, me: (2 * me[0] + i, 0))],
        [pl.BlockSpec((None, hd, FS), lambda i, me: (me[0], i, 0)), pl.BlockSpec((None, hw, D), lambda i, me: (me[0], i, 0))],
        copy)


def grad_scatter(name, kinds):
    groups = [_mixer_group() if k == "mixer" else _ffn_group() for k in kinds]

    def per_group(refs, counts, fn):
        out, i = [], 0
        for g, c in zip(groups, counts):
            out += fn(g, refs[i:i + c])
            i += c
        return out

    nsrcs = [g.nsrc for g in groups]
    nlands = [len(g.lands) for g in groups]
    sem_of, off = [], 0
    for g in groups:
        sem_of += [off + s for s in g.sem_of]
        off += len(g.lands)

    ex = Exchange(name,
                  lambda refs, p: per_group(refs, nsrcs, lambda g, r: g.src_of(r, p)),
                  lambda lands, p: per_group(lands, nlands, lambda g, r: g.dst_of(r, p)),
                  sem_of, lambda lands: [a.at[pl.ds(0, 7)] for a in lands])

    def prep(me1, grads):
        nsrc = sum(nsrcs)

        def body(me_ref, *refs):
            srcs, lands = refs[:nsrc], refs[nsrc:]
            i = j = 0
            for g in groups:
                g.copy(srcs[i:i + g.nsrc], lands[j:j + len(g.lands)])
                i += g.nsrc
                j += len(g.lands)

        return list(pl.pallas_call(
            body, name=name + "_prep", out_shape=tuple(s for g in groups for s in g.lands),
            grid_spec=pltpu.PrefetchScalarGridSpec(
                num_scalar_prefetch=1, grid=(2,), in_specs=[s for g in groups for s in g.in_specs],
                out_specs=tuple(s for g in groups for s in g.out_specs)),
            compiler_params=_cp("arbitrary"),
        )(me1, *grads))

    return ex, prep


def gather_prep(l, me1, w_in, w_out_conv, w_out_rnn, w_o, ffn_w1, ffn_w2, rg_w, small):
    hd, hw = D // 2, W2S // 2

    def body(me_ref, win, woc, wor, wo, w1, w2, rgw, sm, c_win, c_w3, c_w1, c_w2, c_rgw, c_sm,
             o_win, o_w3, o_w1, o_w2, o_rgw, o_sm):
        a = win[...].astype(BF16)
        c_win[...] = a
        o_win[...] = a
        for k, r in enumerate((woc, wor, wo)):
            b = r[...].astype(BF16)
            c_w3[k] = b
            o_w3[k] = b
        for src, comp, own in ((w1, c_w1, o_w1), (w2, c_w2, o_w2), (rgw, c_rgw, o_rgw)):
            b = src[...].astype(BF16)
            comp[...] = b
            own[...] = b
        c_sm[...] = sm[...]
        o_sm[...] = sm[...]

    three = pl.BlockSpec((None, 64, D), lambda i, me: (l, i, 0))
    in_specs = [pl.BlockSpec((None, hd, WIN_S), lambda i, me: (l, i, 0)), three, three, three,
                pl.BlockSpec((None, 2, hd, FS), lambda i, me: (l, 0, i, 0)),
                pl.BlockSpec((None, 2, hw, D), lambda i, me: (l, 0, i, 0)),
                pl.BlockSpec((None, 2, HEADS, 16, HD), lambda i, me: (l, 0, 0, i, 0)),
                pl.BlockSpec((None, 8, 128), lambda i, me: (l, i, 0))]
    out_specs = (pl.BlockSpec((hd, WIN_S), lambda i, me: (i, 0)), pl.BlockSpec((3, 64, D), lambda i, me: (0, i, 0)),
                 pl.BlockSpec((2, hd, FS), lambda i, me: (0, i, 0)), pl.BlockSpec((2, hw, D), lambda i, me: (0, i, 0)),
                 pl.BlockSpec((2, HEADS, 16, HD), lambda i, me: (0, 0, i, 0)), pl.BlockSpec((8, 128), lambda i, me: (i, 0)),
                 pl.BlockSpec((hd, WIN_S), lambda i, me: (i, me[0])),
                 pl.BlockSpec((3, 64, D), lambda i, me: (0, 2 * me[0] + i, 0)),
                 pl.BlockSpec((2, None, hd, FS), lambda i, me: (0, me[0], i, 0)),
                 pl.BlockSpec((2, hw, D), lambda i, me: (0, 2 * me[0] + i, 0)),
                 pl.BlockSpec((2, HEADS, 16, HD), lambda i, me: (0, 0, 2 * me[0] + i, 0)),
                 pl.BlockSpec((8, 128), lambda i, me: (i, me[0])))
    compact = (SDS((D, WIN_S), BF16), SDS((3, 128, D), BF16), SDS((2, D, FS), BF16), SDS((2, W2S, D), BF16),
               SDS((2, HEADS, 32, HD), BF16), SDS((SMALL_ROWS, 128), F32))
    res = pl.pallas_call(
        body, name=f"gather_prep{l}", out_shape=compact + GATHER_SHAPES,
        grid_spec=pltpu.PrefetchScalarGridSpec(num_scalar_prefetch=1, grid=(2,), in_specs=in_specs, out_specs=out_specs),
        compiler_params=_cp("arbitrary"),
    )(me1, w_in, w_out_conv, w_out_rnn, w_o, ffn_w1, ffn_w2, rg_w, small)
    return list(res[:6]), list(res[6:])


def gather_small_grads(gsmall):
    shp = gsmall.shape

    def body(g, land, send_sems, recv_sems, loc_sems):
        _all_to_all_copies(lambda p: [g], lambda p: [land.at[p]], [(land.at[pl.ds(0, 7)], 0)],
                           send_sems, recv_sems, loc_sems)

    return pl.pallas_call(
        body, name="gather_small_grads", out_shape=SDS((N_DEV,) + shp, F32),
        in_specs=[ANY], out_specs=ANY,
        scratch_shapes=[pltpu.SemaphoreType.DMA((1,)), pltpu.SemaphoreType.DMA((1,)), pltpu.SemaphoreType.DMA((1,))],
        compiler_params=pltpu.CompilerParams(has_side_effects=True),
    )(gsmall)


def ffn_up(xb, w1, f):
    t = xb.shape[0]
    tm = _row_tile(t, 512)

    def body(x_ref, wg_ref, wu_ref, g_ref, u_ref, a_ref):
        x = x_ref[...]
        g = _dot(x, wg_ref[...], NN)
        u = _dot(x, wu_ref[...], NN)
        g_ref[...] = g.astype(BF16)
        u_ref[...] = u.astype(BF16)
        a_ref[...] = (g * _sigmoid(g) * u).astype(BF16)

    out = pl.BlockSpec((None, tm, FS), lambda j, i: (j, i, 0))
    return pl.pallas_call(
        body, name=f"ffn_up{f}", grid=(4, t // tm),
        in_specs=[pl.BlockSpec((tm, D), lambda j, i: (i, 0)),
                  pl.BlockSpec((None, None, D, FS), lambda j, i: (f, j, 0, 0)),
                  pl.BlockSpec((None, None, D, FS), lambda j, i: (f, j + 4, 0, 0))],
        out_specs=(out, out, out), out_shape=(SDS((4, t, FS), BF16),) * 3,
        compiler_params=_cp("parallel", "parallel"),
    )(xb, w1, w1)


def ffn_down_ln(a, w2, f, xf, small, s):
    t = xf.shape[0]
    tm = _row_tile(t, 256)

    def body(a_ref, w_ref, x_ref, sm_ref, xo_ref, xb_ref, xh_ref, rs_ref):
        acc = _dot(a_ref[0], w_ref[0:FS, :], NN)
        for j in range(1, 4):
            acc = acc + _dot(a_ref[j], w_ref[j * FS:(j + 1) * FS, :], NN)
        z = ALPHA * x_ref[...] + 0.5 * acc
        y, xh, rstd = _ln_fwd(z, sm_ref[R_LNG + s:R_LNG + s + 1, :], sm_ref[R_LNB + s:R_LNB + s + 1, :])
        xo_ref[...] = y
        xb_ref[...] = y.astype(BF16)
        xh_ref[...] = xh
        rs_ref[...] = rstd

    row = pl.BlockSpec((tm, D), lambda i: (i, 0))
    return pl.pallas_call(
        body, name=f"ffn_down_ln{f}", grid=(t // tm,),
        in_specs=[pl.BlockSpec((4, tm, FS), lambda i: (0, i, 0)),
                  pl.BlockSpec((None, D_FF, D), lambda i: (f, 0, 0)),
                  row, pl.BlockSpec((SMALL_ROWS, D), lambda i: (0, 0))],
        out_specs=(row, row, row, pl.BlockSpec((tm, 1), lambda i: (i, 0))),
        out_shape=(SDS((t, D), F32), SDS((t, D), BF16), SDS((t, D), F32), SDS((t, 1), F32)),
        compiler_params=_cp("parallel"),
    )(a, w2, xf, small)


def ffn_bwd_gates(dxn, xh, rstd, small, s, w2, f, g, u):
    t = dxn.shape[0]
    tm = _row_tile(t, 256)

    def body(dy_ref, xh_ref, rs_ref, sm_ref, w_ref, g_ref, u_ref, dz_ref, df_ref, dgu_ref, dln_ref):
        i = pl.program_id(0)
        dy = dy_ref[...]
        xhat = xh_ref[...]
        dz = _ln_bwd(dy, xhat, rs_ref[...], sm_ref[R_LNG + s:R_LNG + s + 1, :])

        @pl.when(i == 0)
        def _():
            dln_ref[...] = jnp.zeros_like(dln_ref)

        dln_ref[0:1, :] += jnp.sum(dy * xhat, axis=0, keepdims=True)
        dln_ref[1:2, :] += jnp.sum(dy, axis=0, keepdims=True)
        dz_ref[...] = dz
        df = (0.5 * dz).astype(BF16)
        df_ref[...] = df
        for j in range(4):
            da = _dot(df, w_ref[j * FS:(j + 1) * FS, :], NT)
            gg = g_ref[j].astype(F32)
            uu = u_ref[j].astype(F32)
            sg = _sigmoid(gg)
            dgu_ref[j] = (da * uu * (sg * (1.0 + gg * (1.0 - sg)))).astype(BF16)
            dgu_ref[j + 4] = (da * (gg * sg)).astype(BF16)

    row = pl.BlockSpec((tm, D), lambda i: (i, 0))
    gu = pl.BlockSpec((4, tm, FS), lambda i: (0, i, 0))
    return pl.pallas_call(
        body, name=f"ffn_bwd_gates{f}", grid=(t // tm,),
        in_specs=[row, row, pl.BlockSpec((tm, 1), lambda i: (i, 0)), pl.BlockSpec((SMALL_ROWS, D), lambda i: (0, 0)),
                  pl.BlockSpec((None, D_FF, D), lambda i: (f, 0, 0)), gu, gu],
        out_specs=(row, row, pl.BlockSpec((8, tm, FS), lambda i: (0, i, 0)), pl.BlockSpec((2, D), lambda i: (0, 0))),
        out_shape=(SDS((t, D), F32), SDS((t, D), BF16), SDS((8, t, FS), BF16), SDS((2, D), F32)),
        compiler_params=_cp("arbitrary"),
    )(dxn, xh, rstd, small, w2, g, u)


def ffn_bwd_dx(dz, dgu, w1, f):
    t = dz.shape[0]
    tm = _row_tile(t, 1024)

    def body(dz_ref, d_ref, w_ref, dx_ref, acc):
        k = pl.program_id(1)

        @pl.when(k == 0)
        def _():
            acc[...] = ALPHA * dz_ref[...]

        acc[...] += _dot(d_ref[...], w_ref[...], NT)

        @pl.when(k == 7)
        def _():
            dx_ref[...] = acc[...]

    row = pl.BlockSpec((tm, D), lambda i, k: (i, 0))
    return pl.pallas_call(
        body, name=f"ffn_bwd_dx{f}", grid=(t // tm, 8),
        in_specs=[row, pl.BlockSpec((None, tm, FS), lambda i, k: (k, i, 0)),
                  pl.BlockSpec((None, None, D, FS), lambda i, k: (f, k, 0, 0))],
        out_specs=row, out_shape=SDS((t, D), F32),
        scratch_shapes=[pltpu.VMEM((tm, D), F32)],
        compiler_params=_cp("parallel", "arbitrary"),
    )(dz, dgu, w1)


def _mm_tn(name, a, a_spec, b, b_spec, out_sds, out_spec, grid):
    def body(a_ref, b_ref, o_ref):
        o_ref[...] = _dot(a_ref[...], b_ref[...], TN).astype(o_ref.dtype)

    return pl.pallas_call(
        body, name=name, grid=grid, in_specs=[a_spec, b_spec], out_specs=out_spec, out_shape=out_sds,
        compiler_params=_cp(*(["parallel"] * len(grid))),
    )(a, b)


def ffn_dw1(xb, dgu, f):
    t = xb.shape[0]
    return _mm_tn(f"ffn_dw1_{f}", xb, pl.BlockSpec((t, D), lambda j: (0, 0)),
                  dgu, pl.BlockSpec((None, t, FS), lambda j: (j, 0, 0)),
                  SDS((8, D, FS), BF16), pl.BlockSpec((None, D, FS), lambda j: (j, 0, 0)), (8,))


def ffn_dw2(a, df, f):
    t = df.shape[0]
    return _mm_tn(f"ffn_dw2_{f}", a, pl.BlockSpec((None, t, FS), lambda j: (j, 0, 0)),
                  df, pl.BlockSpec((t, D), lambda j: (0, 0)),
                  SDS((D_FF, D), BF16), pl.BlockSpec((FS, D), lambda j: (j, 0)), (4,))


def mm_tn_square(name, a, b):
    t = a.shape[0]
    return _mm_tn(name, a, pl.BlockSpec((t, 512), lambda i: (0, i)),
                  b, pl.BlockSpec((t, D), lambda i: (0, 0)),
                  SDS((D, D), BF16), pl.BlockSpec((512, D), lambda i: (i, 0)), (2,))


def mix_proj(xb, win):
    t = xb.shape[0]
    tm = _row_tile(t, 512)

    def body(x_ref, w_ref, o_ref):
        o_ref[...] = _dot(x_ref[...], w_ref[...], NN).astype(BF16)

    return pl.pallas_call(
        body, name="mix_proj", grid=(7, t // tm),
        in_specs=[pl.BlockSpec((tm, D), lambda n, i: (i, 0)), pl.BlockSpec((D, D), lambda n, i: (0, n))],
        out_specs=pl.BlockSpec((tm, D), lambda n, i: (i, n)), out_shape=SDS((t, D_IN), BF16),
        compiler_params=_cp("parallel", "parallel"),
    )(xb, win)


def _pcol(tm, k):
    return pl.BlockSpec((tm, D), lambda i: (i, k))


def _prev_halo(tm, k):
    return pl.BlockSpec((8, D), lambda i: (jnp.maximum(i * (tm // 8) - 1, 0), k))


def _prev_halo16(tm, k):
    return pl.BlockSpec((16, D), lambda i: (jnp.maximum(i * (tm // 16) - 1, 0), k))


def _next_halo(tm, t, k):
    return pl.BlockSpec((8, D), lambda i: (jnp.minimum((i + 1) * (tm // 8), t // 8 - 1), k))


def _full(shape):
    nd = len(shape)
    return pl.BlockSpec(shape, lambda i: (0,) * nd)


def mix_pre(p, small, rcb, ap, rgw):
    t = p.shape[0]
    tm = _row_tile(t, 256)

    def body(bg_ref, cg_ref, v_ref, xr_ref, cgh_ref, vh_ref, xrh_ref, sm_ref, rcb_ref, ap_ref, rgw_ref,
             ca_ref, pa_ref, xc_ref, xcb_ref, gi_ref, gr_ref, h_ref, ext1, ext2, a_s, b_s, carry):
        i = pl.program_id(0)
        first = i == 0
        cv = cg_ref[...].astype(F32) * v_ref[...].astype(F32)
        ext1[0:16, :] = jnp.where(first, 0.0, cgh_ref[...].astype(F32) * vh_ref[...].astype(F32))
        ext1[16:, :] = cv
        xr = xr_ref[...].astype(F32)
        ext2[0:16, :] = jnp.where(first, 0.0, xrh_ref[...].astype(F32))
        ext2[16:, :] = xr
        ca = (sm_ref[R_SC:R_SC + 1, :] * ext1[pl.ds(14, tm), :] + sm_ref[R_SC + 1:R_SC + 2, :] * ext1[pl.ds(15, tm), :]
              + sm_ref[R_SC + 2:R_SC + 3, :] * cv)
        ca_ref[...] = ca
        pa_ref[...] = (bg_ref[...].astype(F32) * ca).astype(BF16)
        xc = (sm_ref[R_RC:R_RC + 1, :] * ext2[pl.ds(13, tm), :] + sm_ref[R_RC + 1:R_RC + 2, :] * ext2[pl.ds(14, tm), :]
              + sm_ref[R_RC + 2:R_RC + 3, :] * ext2[pl.ds(15, tm), :] + sm_ref[R_RC + 3:R_RC + 4, :] * xr
              + rcb_ref[...])
        xc_ref[...] = xc
        xcb = xc.astype(BF16)
        xcb_ref[...] = xcb
        g0, g1 = [], []
        for h in range(HEADS):
            xh = xcb[:, h * HD:(h + 1) * HD]
            g0.append(_dot(xh, rgw_ref[0, h], NN))
            g1.append(_dot(xh, rgw_ref[1, h], NN))
        gi = _sigmoid(jnp.concatenate(g0, axis=1) + sm_ref[R_RGB:R_RGB + 1, :])
        gr = _sigmoid(jnp.concatenate(g1, axis=1) + sm_ref[R_RGB + 1:R_RGB + 2, :])
        gi_ref[...] = gi
        gr_ref[...] = gr
        la = (-LRU_C) * gr * _softplus(-ap_ref[...])
        a_s[...] = jnp.exp(la)
        row = lax.broadcasted_iota(jnp.int32, (tm, D), 0) + i * tm
        mult = jnp.where(row == 0, 1.0, jnp.sqrt(_neg_expm1(2.0 * la)))
        b_s[...] = xc * gi * mult

        @pl.when(first)
        def _():
            carry[...] = jnp.zeros_like(carry)

        carry[...] = _scan_tile(a_s, b_s, h_ref, carry[...], tm, reverse=False)

    row = pl.BlockSpec((tm, D), lambda i: (i, 0))
    f32o, b16o = SDS((t, D), F32), SDS((t, D), BF16)
    ext, tile = pltpu.VMEM((tm + 16, D), F32), pltpu.VMEM((tm, D), F32)
    return pl.pallas_call(
        body, name="mix_pre", grid=(t // tm,),
        in_specs=[_pcol(tm, 0), _pcol(tm, 1), _pcol(tm, 2), _pcol(tm, 3),
                  _prev_halo16(tm, 1), _prev_halo16(tm, 2), _prev_halo16(tm, 3),
                  _full((SMALL_ROWS, D)), _full((1, D)), _full((1, D)), _full((2, HEADS, HD, HD))],
        out_specs=(row,) * 7, out_shape=(f32o, b16o, f32o, b16o, f32o, f32o, f32o),
        scratch_shapes=[ext, ext, tile, tile, pltpu.VMEM((8, D), F32)],
        compiler_params=_cp("arbitrary"),
    )(p, p, p, p, p, p, p, small, rcb, ap, rgw)


def _scan_tile(a_ref, b_ref, o_ref, carry, tm, reverse):
    width = a_ref.shape[1]
    ng = tm // 8
    row8 = lax.broadcasted_iota(jnp.int32, (8, width), 0)

    def step(g, c):
        r = pl.multiple_of((ng - 1 - g if reverse else g) * 8, 8)
        aa = a_ref[pl.ds(r, 8), :]
        bb = b_ref[pl.ds(r, 8), :]
        for s in (1, 2, 4):
            if reverse:
                keep, shift = row8 < 8 - s, 8 - s
            else:
                keep, shift = row8 >= s, s
            a_sh = jnp.where(keep, pltpu.roll(aa, shift, 0), 1.0)
            b_sh = jnp.where(keep, pltpu.roll(bb, shift, 0), 0.0)
            bb = aa * b_sh + bb
            aa = aa * a_sh
        o = aa * c + bb
        o_ref[pl.ds(r, 8), :] = o
        edge = o[0:1, :] if reverse else o[7:8, :]
        return jnp.broadcast_to(edge, (8, width))

    return lax.fori_loop(0, ng, step, carry)


def mix_out(pa, h, p, bm, w3, xf, small):
    t = xf.shape[0]
    tm = _row_tile(t, 256)

    def body(pa_ref, h_ref, yr_ref, gla_ref, glb_ref, bma_ref, bmb_ref, w_ref, x_ref, sm_ref,
             pb_ref, ya_ref, yb_ref, m_ref, xo_ref, xb_ref, xh_ref, rs_ref):
        ge, _ = _gelu(yr_ref[...].astype(F32))
        pb = (h_ref[...] * ge).astype(BF16)
        pb_ref[...] = pb
        ya = _dot(pa_ref[...], w_ref[0], NN)
        yb = _dot(pb, w_ref[1], NN)
        ya_ref[...] = ya
        yb_ref[...] = yb
        ga = _sigmoid(gla_ref[...].astype(F32) + bma_ref[...])
        gb = _sigmoid(glb_ref[...].astype(F32) + bmb_ref[...])
        m = (ga * ya + gb * yb).astype(BF16)
        m_ref[...] = m
        z = ALPHA * x_ref[...] + _dot(m, w_ref[2], NN)
        y, xh, rstd = _ln_fwd(z, sm_ref[R_LNG + 1:R_LNG + 2, :], sm_ref[R_LNB + 1:R_LNB + 2, :])
        xo_ref[...] = y
        xb_ref[...] = y.astype(BF16)
        xh_ref[...] = xh
        rs_ref[...] = rstd

    row = pl.BlockSpec((tm, D), lambda i: (i, 0))
    f32o, b16o = SDS((t, D), F32), SDS((t, D), BF16)
    return pl.pallas_call(
        body, name="mix_out", grid=(t // tm,),
        in_specs=[row, row, _pcol(tm, 4), _pcol(tm, 5), _pcol(tm, 6),
                  pl.BlockSpec((1, D), lambda i: (0, 0)), pl.BlockSpec((1, D), lambda i: (0, 1)),
                  _full((3, D, D)), row, _full((SMALL_ROWS, D))],
        out_specs=(row,) * 7 + (pl.BlockSpec((tm, 1), lambda i: (i, 0)),),
        out_shape=(b16o, f32o, f32o, b16o, f32o, b16o, f32o, SDS((t, 1), F32)),
        compiler_params=_cp("parallel"),
    )(pa, h, p, p, p, bm, bm, w3, xf, small)


def mixb_head(dxn, xh, rstd, small, w3, p, bm, ya, yb, ca, h):
    t = dxn.shape[0]
    tm = _row_tile(t, 256)

    def body(dy_ref, xh_ref, rs_ref, sm_ref, w_ref, bg_ref, yr_ref, gla_ref, glb_ref, bma_ref, bmb_ref,
             ya_ref, yb_ref, ca_ref, h_ref,
             dz_ref, dzb_ref, dya_ref, dyb_ref, dbg_ref, dca_ref, dh_ref, dphi_ref, dln_ref, dbm_ref):
        i = pl.program_id(0)
        dy = dy_ref[...]
        xhat = xh_ref[...]
        dz = _ln_bwd(dy, xhat, rs_ref[...], sm_ref[R_LNG + 1:R_LNG + 2, :])

        @pl.when(i == 0)
        def _():
            dln_ref[...] = jnp.zeros_like(dln_ref)
            dbm_ref[...] = jnp.zeros_like(dbm_ref)

        dln_ref[0:1, :] += jnp.sum(dy * xhat, axis=0, keepdims=True)
        dln_ref[1:2, :] += jnp.sum(dy, axis=0, keepdims=True)
        dz_ref[...] = dz
        dzb = dz.astype(BF16)
        dzb_ref[...] = dzb
        dm = _dot(dzb, w_ref[2], NT)
        ga = _sigmoid(gla_ref[...].astype(F32) + bma_ref[...])
        gb = _sigmoid(glb_ref[...].astype(F32) + bmb_ref[...])
        dya = (dm * ga).astype(BF16)
        dyb = (dm * gb).astype(BF16)
        dya_ref[...] = dya
        dyb_ref[...] = dyb
        dgla = dm * ya_ref[...] * ga * (1.0 - ga)
        dglb = dm * yb_ref[...] * gb * (1.0 - gb)
        dbm_ref[0:1, :] += jnp.sum(dgla, axis=0, keepdims=True)
        dbm_ref[1:2, :] += jnp.sum(dglb, axis=0, keepdims=True)
        dphi_ref[:, D:2 * D] = dgla.astype(BF16)
        dphi_ref[:, 2 * D:3 * D] = dglb.astype(BF16)
        dpa = _dot(dya, w_ref[0], NT)
        dpb = _dot(dyb, w_ref[1], NT)
        dbg_ref[...] = (dpa * ca_ref[...]).astype(BF16)
        dca_ref[...] = dpa * bg_ref[...].astype(F32)
        yr = yr_ref[...].astype(F32)
        ge, th = _gelu(yr)
        dh_ref[...] = dpb * ge
        dphi_ref[:, 0:D] = (dpb * h_ref[...] * _gelu_grad(yr, th)).astype(BF16)

    row = pl.BlockSpec((tm, D), lambda i: (i, 0))
    f32o, b16o = SDS((t, D), F32), SDS((t, D), BF16)
    acc2 = pl.BlockSpec((2, D), lambda i: (0, 0))
    return pl.pallas_call(
        body, name="mixb_head", grid=(t // tm,),
        in_specs=[row, row, pl.BlockSpec((tm, 1), lambda i: (i, 0)), _full((SMALL_ROWS, D)), _full((3, D, D)),
                  _pcol(tm, 0), _pcol(tm, 4), _pcol(tm, 5), _pcol(tm, 6),
                  pl.BlockSpec((1, D), lambda i: (0, 0)), pl.BlockSpec((1, D), lambda i: (0, 1)),
                  row, row, row, row],
        out_specs=(row,) * 7 + (pl.BlockSpec((tm, 3 * D), lambda i: (i, 0)), acc2, acc2),
        out_shape=(f32o, b16o, b16o, b16o, b16o, f32o, f32o, SDS((t, 3 * D), BF16), SDS((2, D), F32), SDS((2, D), F32)),
        compiler_params=_cp("arbitrary"),
    )(dxn, xh, rstd, small, w3, p, p, p, p, bm, bm, ya, yb, ca, h)


def mixb_rec(dh, gr, gi, h, xc, ap, rgw):
    t = dh.shape[0]
    tm = _row_tile(t, 256)
    nt = t // tm

    def body(dh_ref, gr_ref, gi_ref, h_ref, hh_ref, xc_ref, ap_ref, rgw_ref, dg_ref, dxc_ref, red_ref,
             ext, ext_a, c_s, lam_s, lam_c, a_c):
        i = nt - 1 - pl.program_id(0)
        first = i == 0

        @pl.when(pl.program_id(0) == 0)
        def _():
            red_ref[...] = jnp.zeros_like(red_ref)
            lam_c[...] = jnp.zeros_like(lam_c)
            a_c[...] = jnp.zeros_like(a_c)

        ext[0:8, :] = jnp.where(first, 0.0, hh_ref[...])
        ext[8:, :] = h_ref[...]
        hprev = ext[pl.ds(7, tm), :]
        gr = gr_ref[...]
        gi = gi_ref[...]
        xc = xc_ref[...]
        ap = ap_ref[...]
        sp = _softplus(-ap)
        la = (-LRU_C) * gr * sp
        a = jnp.exp(la)
        ext_a[0:tm, :] = a
        ext_a[tm:, :] = a_c[...]
        c_s[...] = ext_a[pl.ds(1, tm), :]
        lam_c[...] = _scan_tile(c_s, dh_ref, lam_s, lam_c[...], tm, reverse=True)
        a_c[...] = jnp.broadcast_to(a[0:1, :], (8, D))
        lam = lam_s[...]
        row = lax.broadcasted_iota(jnp.int32, (tm, D), 0) + i * tm
        start = row == 0
        mult = jnp.where(start, 1.0, jnp.sqrt(_neg_expm1(2.0 * la)))
        dmult = jnp.where(start, 0.0, lam * xc * gi)
        dla = lam * hprev * a - dmult * a * a / mult
        dg1 = (-LRU_C) * sp * dla * gr * (1.0 - gr)
        dg0 = lam * xc * mult * gi * (1.0 - gi)
        dsp = jnp.sum((-LRU_C) * gr * dla, axis=0, keepdims=True)
        red_ref[0:1, :] += jnp.sum(dg0, axis=0, keepdims=True)
        red_ref[1:2, :] += jnp.sum(dg1, axis=0, keepdims=True)
        red_ref[2:3, :] += -dsp * _sigmoid(-ap)
        dg0b = dg0.astype(BF16)
        dg1b = dg1.astype(BF16)
        dg_ref[0] = dg0b
        dg_ref[1] = dg1b
        parts = []
        for hd in range(HEADS):
            sl = slice(hd * HD, (hd + 1) * HD)
            parts.append(_dot(dg0b[:, sl], rgw_ref[0, hd], NT) + _dot(dg1b[:, sl], rgw_ref[1, hd], NT))
        dxc = lam * gi * mult + jnp.concatenate(parts, axis=1)
        dxc_ref[...] = dxc
        red_ref[3:4, :] += jnp.sum(dxc, axis=0, keepdims=True)

    row = pl.BlockSpec((tm, D), lambda i: (nt - 1 - i, 0))
    halo = pl.BlockSpec((8, D), lambda i: (jnp.maximum((nt - 1 - i) * (tm // 8) - 1, 0), 0))
    ext, tile, edge = pltpu.VMEM((tm + 8, D), F32), pltpu.VMEM((tm, D), F32), pltpu.VMEM((8, D), F32)
    return pl.pallas_call(
        body, name="mixb_rec", grid=(nt,),
        in_specs=[row, row, row, row, halo, row, _full((1, D)), _full((2, HEADS, HD, HD))],
        out_specs=(pl.BlockSpec((2, tm, D), lambda i: (0, nt - 1 - i, 0)), row, pl.BlockSpec((8, D), lambda i: (0, 0))),
        out_shape=(SDS((2, t, D), BF16), SDS((t, D), F32), SDS((8, D), F32)),
        scratch_shapes=[ext, ext, tile, tile, edge, edge],
        compiler_params=_cp("arbitrary"),
    )(dh, gr, gi, h, h, xc, ap, rgw)


def mixb_conv(dca, dxc, dbg, p, small):
    t = dca.shape[0]
    tm = _row_tile(t, 256)
    nt = t // tm

    def body(dca_ref, dcan_ref, dxc_ref, dxcn_ref, dbg_ref, cg_ref, v_ref, xr_ref, cgh_ref, vh_ref, xrh_ref, sm_ref,
             dplo_ref, dxr_ref, red_ref, e_dca, e_dxc, e_cv, e_xr):
        i = pl.program_id(0)
        first = i == 0
        last = i == nt - 1

        @pl.when(first)
        def _():
            red_ref[...] = jnp.zeros_like(red_ref)

        dca = dca_ref[...]
        dxc = dxc_ref[...]
        e_dca[0:tm, :] = dca
        e_dca[tm:, :] = jnp.where(last, 0.0, dcan_ref[...])
        e_dxc[0:tm, :] = dxc
        e_dxc[tm:, :] = jnp.where(last, 0.0, dxcn_ref[...])
        cg = cg_ref[...].astype(F32)
        v = v_ref[...].astype(F32)
        xr = xr_ref[...].astype(F32)
        e_cv[0:8, :] = jnp.where(first, 0.0, (cgh_ref[...].astype(F32) * vh_ref[...].astype(F32))[8:16, :])
        e_cv[8:, :] = cg * v
        e_xr[0:8, :] = jnp.where(first, 0.0, xrh_ref[...].astype(F32)[8:16, :])
        e_xr[8:, :] = xr
        dcv = (sm_ref[R_SC + 2:R_SC + 3, :] * dca + sm_ref[R_SC + 1:R_SC + 2, :] * e_dca[pl.ds(1, tm), :]
               + sm_ref[R_SC:R_SC + 1, :] * e_dca[pl.ds(2, tm), :])
        dplo_ref[:, 0:D] = dbg_ref[...]
        dplo_ref[:, D:2 * D] = (dcv * v).astype(BF16)
        dplo_ref[:, 2 * D:3 * D] = (dcv * cg).astype(BF16)
        dxr = (sm_ref[R_RC + 3:R_RC + 4, :] * dxc + sm_ref[R_RC + 2:R_RC + 3, :] * e_dxc[pl.ds(1, tm), :]
               + sm_ref[R_RC + 1:R_RC + 2, :] * e_dxc[pl.ds(2, tm), :] + sm_ref[R_RC:R_RC + 1, :] * e_dxc[pl.ds(3, tm), :])
        dxr_ref[...] = dxr.astype(BF16)
        for k in range(3):
            red_ref[R_SC + k:R_SC + k + 1, :] += jnp.sum(dca * e_cv[pl.ds(6 + k, tm), :], axis=0, keepdims=True)
        for k in range(4):
            red_ref[R_RC + k:R_RC + k + 1, :] += jnp.sum(dxc * e_xr[pl.ds(5 + k, tm), :], axis=0, keepdims=True)

    row = pl.BlockSpec((tm, D), lambda i: (i, 0))
    ext = pltpu.VMEM((tm + 8, D), F32)
    return pl.pallas_call(
        body, name="mixb_conv", grid=(nt,),
        in_specs=[row, _next_halo(tm, t, 0), row, _next_halo(tm, t, 0), row,
                  _pcol(tm, 1), _pcol(tm, 2), _pcol(tm, 3), _prev_halo16(tm, 1), _prev_halo16(tm, 2), _prev_halo16(tm, 3),
                  _full((SMALL_ROWS, D))],
        out_specs=(pl.BlockSpec((tm, 3 * D), lambda i: (i, 0)), row, pl.BlockSpec((8, D), lambda i: (0, 0))),
        out_shape=(SDS((t, 3 * D), BF16), SDS((t, D), BF16), SDS((8, D), F32)),
        scratch_shapes=[ext, ext, ext, ext],
        compiler_params=_cp("arbitrary"),
    )(dca, dca, dxc, dxc, dbg, p, p, p, p, p, p, small)


def mixb_dx(dz, dplo, dxr, dphi, win):
    t = dz.shape[0]
    tm = _row_tile(t, 512)

    def body(dz_ref, lo_ref, xr_ref, hi_ref, w_ref, dx_ref, acc):
        k = pl.program_id(1)

        @pl.when(k == 0)
        def _():
            acc[...] = ALPHA * dz_ref[...]

        @pl.when(k < 3)
        def _():
            acc[...] += _dot(lo_ref[...], w_ref[...], NT)

        @pl.when(k == 3)
        def _():
            acc[...] += _dot(xr_ref[...], w_ref[...], NT)

        @pl.when(k > 3)
        def _():
            acc[...] += _dot(hi_ref[...], w_ref[...], NT)

        @pl.when(k == 6)
        def _():
            dx_ref[...] = acc[...]

    row = pl.BlockSpec((tm, D), lambda i, k: (i, 0))
    return pl.pallas_call(
        body, name="mixb_dx", grid=(t // tm, 7),
        in_specs=[row, pl.BlockSpec((tm, D), lambda i, k: (i, jnp.minimum(k, 2))), row,
                  pl.BlockSpec((tm, D), lambda i, k: (i, jnp.clip(k - 4, 0, 2))),
                  pl.BlockSpec((D, D), lambda i, k: (0, k))],
        out_specs=row, out_shape=SDS((t, D), F32),
        scratch_shapes=[pltpu.VMEM((tm, D), F32)],
        compiler_params=_cp("parallel", "arbitrary"),
    )(dz, dplo, dxr, dphi, win)


def mixb_dwin(xb, dplo, dxr, dphi):
    t = xb.shape[0]
    tk = _row_tile(t, 2048)
    nk = t // tk

    def body(x_ref, lo_ref, xr_ref, hi_ref, o_ref, acc):
        n = pl.program_id(0)
        k = pl.program_id(1)

        @pl.when(k == 0)
        def _():
            acc[...] = jnp.zeros_like(acc)

        @pl.when(n < 3)
        def _():
            acc[...] += _dot(x_ref[...], lo_ref[...], TN)

        @pl.when(n == 3)
        def _():
            acc[...] += _dot(x_ref[...], xr_ref[...], TN)

        @pl.when(n > 3)
        def _():
            acc[...] += _dot(x_ref[...], hi_ref[...], TN)

        @pl.when(k == nk - 1)
        def _():
            o_ref[...] = acc[...].astype(BF16)

    return pl.pallas_call(
        body, name="mixb_dwin", grid=(7, nk),
        in_specs=[pl.BlockSpec((tk, D), lambda n, k: (k, 0)),
                  pl.BlockSpec((tk, D), lambda n, k: (jnp.where(n < 3, k, 0), jnp.minimum(n, 2))),
                  pl.BlockSpec((tk, D), lambda n, k: (jnp.where(n == 3, k, 0), 0)),
                  pl.BlockSpec((tk, D), lambda n, k: (jnp.where(n > 3, k, 0), jnp.clip(n - 4, 0, 2)))],
        out_specs=pl.BlockSpec((D, D), lambda n, k: (0, n)), out_shape=SDS((D, D_IN), BF16),
        scratch_shapes=[pltpu.VMEM((D, D), F32)],
        compiler_params=_cp("parallel", "arbitrary"),
    )(xb, dplo, dxr, dphi)


def mixb_drgw(xcb, dg):
    t = xcb.shape[0]
    return _mm_tn("mixb_drgw", xcb, pl.BlockSpec((t, HD), lambda g, h: (0, h)),
                  dg, pl.BlockSpec((None, t, HD), lambda g, h: (g, 0, h)),
                  SDS((2, HEADS, HD, HD), BF16), pl.BlockSpec((None, None, HD, HD), lambda g, h: (g, h, 0, 0)),
                  (2, HEADS))


def loss_head(y, tgt):
    t = y.shape[0]
    tm = _row_tile(t, 512)

    def body(y_ref, t_ref, dy_ref, l_ref):
        i = pl.program_id(0)
        e = y_ref[...] - t_ref[...]
        dy_ref[...] = e * (1.0 / D)

        @pl.when(i == 0)
        def _():
            l_ref[...] = jnp.zeros_like(l_ref)

        l_ref[...] += 0.5 * jnp.sum(jnp.mean(e * e, axis=-1, keepdims=True), axis=0, keepdims=True)

    row = pl.BlockSpec((tm, D), lambda i: (i, 0))
    return pl.pallas_call(
        body, name="loss_head", grid=(t // tm,), in_specs=[row, row],
        out_specs=(row, pl.BlockSpec((8, 128), lambda i: (0, 0))),
        out_shape=(SDS((t, D), F32), SDS((8, 128), F32)),
        compiler_params=_cp("arbitrary"),
    )(y, tgt)


def _adamw(w, g, m, v):
    m = ADAM_B1 * m + (1.0 - ADAM_B1) * g
    v = ADAM_B2 * v + (1.0 - ADAM_B2) * (g * g)
    m_hat = m / (1.0 - ADAM_B1 ** ADAM_STEP)
    v_hat = v / (1.0 - ADAM_B2 ** ADAM_STEP)
    delta = -ADAM_LR * (m_hat / (jnp.sqrt(v_hat) + ADAM_EPS) + ADAM_WD * w)
    return delta, m, v


def adam_big(name, w, m, v, parts, rows, lanes, blk_off, tr, l0, prev=None):
    nr = rows // tr
    nl = len(parts)

    def body(w_ref, m_ref, v_ref, *rest):
        g_ref, d_ref, mo_ref, vo_ref = rest[-4:]
        l = pl.program_id(0)
        for ll in range(nl):
            pr = rest[ll]

            @pl.when(l == ll)
            def _():
                g = pr[0].astype(F32)
                for s in range(1, N_DEV):
                    g = g + pr[s].astype(F32)
                g_ref[...] = g
                d, mn, vn = _adamw(w_ref[...], g, m_ref[...], v_ref[...])
                d_ref[...] = d
                mo_ref[...] = mn
                vo_ref[...] = vn

    blk = pl.BlockSpec((None, tr, lanes), lambda l, r: (l + l0, r, 0))

    def part_spec(ll):
        return pl.BlockSpec((N_DEV, tr, lanes), lambda l, r: (0, jnp.where(l == ll, r, 0) + blk_off, 0))

    out = SDS(w.shape, F32)
    extra = [] if prev is None else list(prev)
    return pl.pallas_call(
        body, name=name, grid=(nl, nr),
        in_specs=[blk, blk, blk] + [part_spec(ll) for ll in range(nl)] + [ANY] * len(extra),
        out_specs=(blk,) * 4, out_shape=(out,) * 4,
        input_output_aliases={3 + nl + i: i for i in range(len(extra))},
        compiler_params=_cp("parallel", "parallel"),
    )(w, m, v, *parts, *extra)


def sum_small(land):
    def body(l_ref, o_ref):
        g = l_ref[0]
        for s in range(1, N_DEV):
            g = g + l_ref[s]
        o_ref[...] = g

    return pl.pallas_call(body, name="sum_small", out_shape=SDS(land.shape[1:], F32))(land)


def adam_small(name, w, g, m, v):
    def body(w_ref, g_ref, m_ref, v_ref, d_ref, mo_ref, vo_ref):
        d, mn, vn = _adamw(w_ref[...], g_ref[...], m_ref[...], v_ref[...])
        d_ref[...] = d
        mo_ref[...] = mn
        vo_ref[...] = vn

    out = SDS(w.shape, F32)
    return pl.pallas_call(body, name=name, out_shape=(out, out, out))(w, g, m, v)


def _ffn_forward(xf, xb, gw, f, s):
    g, u, a = ffn_up(xb, gw["w1"], f)
    xo, xob, xh, rs = ffn_down_ln(a, gw["w2"], f, xf, gw["small"], s)
    return (xo, xob), dict(xb=xb, g=g, u=u, a=a, xh=xh, rs=rs)


def _ffn_backward(dxn, sv, gw, f, s):
    dz, df, dgu, dln = ffn_bwd_gates(dxn, sv["xh"], sv["rs"], gw["small"], s, gw["w2"], f, sv["g"], sv["u"])
    dx = ffn_bwd_dx(dz, dgu, gw["w1"], f)
    dw1 = ffn_dw1(sv["xb"], dgu, f)
    dw2 = ffn_dw2(sv["a"], df, f)
    return dx, dw1, dw2, dln


def _mixer_forward(xf, xb, gw, rcb, ap, bm):
    p = mix_proj(xb, gw["win"])
    ca, pa, xc, xcb, gi, gr, h = mix_pre(p, gw["small"], rcb, ap, gw["rgw"])
    pb, ya, yb, m, xo, xob, xh, rs = mix_out(pa, h, p, bm, gw["w3"], xf, gw["small"])
    sv = dict(xb=xb, p=p, ca=ca, pa=pa, xc=xc, xcb=xcb, gi=gi, gr=gr, h=h, pb=pb, ya=ya, yb=yb, m=m, xh=xh, rs=rs)
    return (xo, xob), sv


def _mixer_backward(dxn, sv, gw, rcb, ap, bm):
    dz, dzb, dya, dyb, dbg, dca, dh, dphi, dln, dbm = mixb_head(
        dxn, sv["xh"], sv["rs"], gw["small"], gw["w3"], sv["p"], bm, sv["ya"], sv["yb"], sv["ca"], sv["h"])
    dg, dxc, red_rec = mixb_rec(dh, sv["gr"], sv["gi"], sv["h"], sv["xc"], ap, gw["rgw"])
    dplo, dxr, red_conv = mixb_conv(dca, dxc, dbg, sv["p"], gw["small"])
    dx = mixb_dx(dz, dplo, dxr, dphi, gw["win"])
    dwin = mixb_dwin(sv["xb"], dplo, dxr, dphi)
    dwo = mm_tn_square("mixb_dwo", sv["m"], dzb)
    dwoc = mm_tn_square("mixb_dwoc", sv["pa"], dya)
    dwor = mm_tn_square("mixb_dwor", sv["pb"], dyb)
    drgw = mixb_drgw(sv["xcb"], dg)
    return dx, dict(dwin=dwin, dwoc=dwoc, dwor=dwor, dwo=dwo, drgw=drgw), dln, dbm, red_rec, red_conv


def kernel(x, w_in, b_merge, sc_w, rc_w, rc_b, rg_w, rg_b, a_param, w_out_conv, w_out_rnn, w_o, ffn_w1, ffn_w2, ln_g, ln_b, loss_target, m_w_in, m_b_merge, m_sc_w, m_rc_w, m_rc_b, m_rg_w, m_rg_b, m_a_param, m_w_out_conv, m_w_out_rnn, m_w_o, m_ffn_w1, m_ffn_w2, m_ln_g, m_ln_b, v_w_in, v_b_merge, v_sc_w, v_rc_w, v_rc_b, v_rg_w, v_rg_b, v_a_param, v_w_out_conv, v_w_out_rnn, v_w_o, v_ffn_w1, v_ffn_w2, v_ln_g, v_ln_b):
    t = x.shape[1]
    me = _me()

    def rows(parts, total):
        out, off = None, 0
        for part in parts:
            r = part.shape[-2]
            pad = [(0, 0)] * (part.ndim - 2) + [(off, total - off - r), (0, 0)]
            padded = jnp.pad(part, pad)
            out = padded if out is None else out + padded
            off += r
        return out

    def pack_small(sc, rc, rgb, lng, lnb):
        return rows([sc, rc, rgb, lng, lnb], SMALL_ROWS)

    def pack_rep(bmv, rcbv, apv):
        return rows([bmv.reshape(DEPTH, 2, D), rcbv[:, None], apv[:, None]], 8)

    small = pack_small(sc_w, rc_w, rg_b, ln_g, ln_b)

    me1 = jnp.reshape(me, (1,)).astype(jnp.int32)

    def layer_shards(l):
        return gather_prep(l, me1, w_in, w_out_conv, w_out_rnn, w_o, ffn_w1, ffn_w2, rg_w, small)

    def as_weights(lands):
        return dict(zip(("win", "w3", "w1", "w2", "rgw", "small"), lands))

    xf = x.reshape(t, D)
    cur = (xf, xf.astype(BF16))
    gathers = [ChipGather(f"gather{l}") for l in range(DEPTH)]
    shards, own = layer_shards(0)
    sems, shards, own, cur = gathers[0].first(shards, own, cur)
    relay_sems, shards, own, cur = gathers[0].relay(sems, shards, own, cur)
    lands, cur = gathers[0].last(sems, relay_sems, shards, own, cur)
    gws, saved = [as_weights(lands)], []
    for l in range(DEPTH):
        gw = gws[l]
        nxt = l + 1 < DEPTH
        if nxt:
            shards, own = layer_shards(l + 1)
            sems, shards, own, cur = gathers[l + 1].first(shards, own, cur)
        rcb, ap, bm = rc_b[l][None], a_param[l][None], b_merge[l][None]
        cur, s0 = _ffn_forward(cur[0], cur[1], gw, 0, 0)
        cur, s1 = _mixer_forward(cur[0], cur[1], gw, rcb, ap, bm)
        if nxt:
            relay_sems, shards, own, cur = gathers[l + 1].relay(sems, shards, own, cur)
        cur, s2 = _ffn_forward(cur[0], cur[1], gw, 1, 2)
        saved.append((s0, s1, s2))
        if nxt:
            lands, cur = gathers[l + 1].last(sems, relay_sems, shards, own, cur)
            gws.append(as_weights(lands))

    dy, loss_tile = loss_head(cur[0], loss_target.reshape(t, D))
    loss = lax.psum(loss_tile[0, 0], ("x", "y", "c"))

    lands = [dict() for _ in range(DEPTH)]
    gsmall = [None] * DEPTH
    flights = {}

    def launch(key, kinds, grads, dy):
        ex, prep = grad_scatter(f"scatter{key}", kinds)
        fl = ex.start(grads, prep(me1, grads), [dy])
        flights[key] = (ex, fl)
        return fl[4][0]

    def landed(key, thru):
        ex, fl = flights.pop(key)
        return ex.wait(fl[0], fl[1], fl[2], fl[3], thru)

    ffn_b, mixer, ffn_a = ("w1b", "w2b"), ("w_in", "w3", "rgw"), ("w1a", "w2a")
    for l in reversed(range(DEPTH)):
        gw = gws[l]
        rcb, ap, bm = rc_b[l][None], a_param[l][None], b_merge[l][None]
        s0, s1, s2 = saved[l]
        dy, dw1b, dw2b, dln2 = _ffn_backward(dy, s2, gw, 1, 2)
        if l == 0:
            dy = launch("0b", ["ffn"], [dw1b, dw2b], dy)
        dy, dmix, dln1, dbm, red_rec, red_conv = _mixer_backward(dy, s1, gw, rcb, ap, bm)
        mixer_grads = [dmix["dwin"], dmix["dwoc"], dmix["dwor"], dmix["dwo"], dmix["drgw"]]
        if l == 0:
            dy = launch("0m", ["mixer"], mixer_grads, dy)
        dy, dw1a, dw2a, dln0 = _ffn_backward(dy, s0, gw, 0, 0)
        if l == 0:
            dy = launch("0a", ["ffn"], [dw1a, dw2a], dy)
        else:
            if l + 1 < DEPTH:
                got, (dy,) = landed(str(l + 1), [dy])
                lands[l + 1] = dict(zip(ffn_b + mixer + ffn_a, got))
            dy = launch(str(l), ["ffn", "mixer", "ffn"], [dw1b, dw2b] + mixer_grads + [dw1a, dw2a], dy)
        gsmall[l] = rows([red_conv[0:7], red_rec[0:2], dln0[0:1], dln1[0:1], dln2[0:1], dln0[1:2], dln1[1:2],
                          dln2[1:2], dbm, red_rec[3:4], red_rec[2:3]], GRAD_ROWS)
    grad_x = dy.reshape(1, t, D)

    gfull = sum_small(gather_small_grads(jnp.stack(gsmall, axis=0)))

    def parts_of(layers):
        ls = [lands[l] for l in layers]
        return dict(w_in=[a["w_in"] for a in ls],
                    w3=[a["w3"].reshape(N_DEV, 3 * 128, D) for a in ls],
                    rgw=[a["rgw"].reshape(N_DEV, 2 * HEADS * 32, HD) for a in ls],
                    w1=[a[k] for a in ls for k in ("w1a", "w1b")],
                    w2=[a[k] for a in ls for k in ("w2a", "w2b")])

    families = [("w_in", w_in, m_w_in, v_w_in, "w_in", 1, D, WIN_S, 0, 128),
                ("w_out_conv", w_out_conv, m_w_out_conv, v_w_out_conv, "w3", 1, 128, D, 0, 128),
                ("w_out_rnn", w_out_rnn, m_w_out_rnn, v_w_out_rnn, "w3", 1, 128, D, 1, 128),
                ("w_o", w_o, m_w_o, v_w_o, "w3", 1, 128, D, 2, 128),
                ("ffn_w1", ffn_w1, m_ffn_w1, v_ffn_w1, "w1", 2, D, FS, 0, 256),
                ("ffn_w2", ffn_w2, m_ffn_w2, v_ffn_w2, "w2", 2, W2S, D, 0, W2S // 2),
                ("rg_w", rg_w, m_rg_w, v_rg_w, "rgw", 1, 2 * HEADS * 32, HD, 0, 256)]

    def adam_pass(tag, layers, prev):
        parts = parts_of(layers)
        outs = {}
        for name, w, m, v, fam, per, nrow, lanes, blk_off, tr in families:
            r3 = lambda a: a.reshape(DEPTH * per, nrow, lanes)
            outs[name] = adam_big(f"adam_{name}_{tag}", r3(w), r3(m), r3(v), parts[fam], nrow, lanes, blk_off, tr,
                                  per * layers[0], None if prev is None else prev[name])
        return outs

    def through(outs, key, extra):
        thru = [outs[f[0]][0] for f in families] + list(extra)
        got, thru = landed(key, thru)
        outs = {f[0]: [thru[i]] + list(outs[f[0]][1:]) for i, f in enumerate(families)}
        return got, outs, thru[len(families):]

    done = adam_pass("top", list(range(2, DEPTH)), None)
    got, done, (gfull,) = through(done, "1", [gfull])
    lands[1] = dict(zip(ffn_b + mixer + ffn_a, got))
    done = adam_pass("second", [1], done)
    for key, names in (("0b", ffn_b), ("0m", mixer), ("0a", ffn_a)):
        got, done, _ = through(done, key, [])
        lands[0].update(zip(names, got))
    done = adam_pass("first", [0], done)
    res = {f[0]: [o.reshape(f[1].shape) for o in done[f[0]]] for f in families}

    g_sh = lax.dynamic_slice(gfull, (0, 0, me * 128), (DEPTH, SMALL_ROWS, 128))
    g_rep = gfull[:, G_BM:G_BM + 8, :]
    d_sh, m_sh, v_sh = adam_small("adam_small_sharded", small, g_sh,
                                  pack_small(m_sc_w, m_rc_w, m_rg_b, m_ln_g, m_ln_b),
                                  pack_small(v_sc_w, v_rc_w, v_rg_b, v_ln_g, v_ln_b))
    d_rep, m_rep, v_rep = adam_small("adam_small_replicated", pack_rep(b_merge, rc_b, a_param), g_rep,
                                     pack_rep(m_b_merge, m_rc_b, m_a_param), pack_rep(v_b_merge, v_rc_b, v_a_param))

    def unpack_sh(a):
        return dict(sc_w=a[:, 0:3], rc_w=a[:, 3:7], rg_b=a[:, 7:9], ln_g=a[:, 9:12], ln_b=a[:, 12:15])

    def unpack_rep(a):
        return dict(b_merge=a[:, 0:2].reshape(DEPTH, 2 * D), rc_b=a[:, 2], a_param=a[:, 3])

    gsh, grep = unpack_sh(g_sh), unpack_rep(g_rep)
    dsh, drep = unpack_sh(d_sh), unpack_rep(d_rep)
    msh, mrep = unpack_sh(m_sh), unpack_rep(m_rep)
    vsh, vrep = unpack_sh(v_sh), unpack_rep(v_rep)
    for n in ("sc_w", "rc_w", "rg_b", "ln_g", "ln_b"):
        res[n] = [gsh[n], dsh[n], msh[n], vsh[n]]
    for n in ("b_merge", "rc_b", "a_param"):
        res[n] = [grep[n], drep[n], mrep[n], vrep[n]]

    names = ["w_in", "b_merge", "sc_w", "rc_w", "rc_b", "rg_w", "rg_b", "a_param", "w_out_conv", "w_out_rnn", "w_o",
             "ffn_w1", "ffn_w2", "ln_g", "ln_b"]
    out = [loss, grad_x]
    for k in range(4):
        out += [res[n][k] for n in names]
    return tuple(out)
```

```python
import functools

import jax
import jax.numpy as jnp
from jax import lax
from jax.experimental import pallas as pl
from jax.experimental.pallas import tpu as pltpu

F32 = jnp.float32
BF16 = jnp.bfloat16
SDS = jax.ShapeDtypeStruct

N_DEV = 8
DEPTH = 4
D = 1024
D_FF = 2816
FS = D_FF // 4
W2S = D_FF // 8
D_IN = 7 * D
WIN_S = D_IN // 8
HEADS = 4
HD = D // HEADS
LRU_C = 8.0
ALPHA = (2.0 * DEPTH) ** 0.25
LN_EPS = 1e-5
ADAM_LR, ADAM_B1, ADAM_B2, ADAM_EPS, ADAM_WD, ADAM_STEP = 0.001, 0.9, 0.999, 1e-08, 0.01, 10

R_SC, R_RC, R_RGB, R_LNG, R_LNB = 0, 3, 7, 9, 12
SMALL_ROWS = 16
GS_ROWS = ((0, 3), (3, 4), (8, 2), (10, 3), (13, 3))

NN = ((1,), (0,))
NT = ((1,), (1,))
TN = ((0,), (0,))
MESH = pl.DeviceIdType.MESH
ANY = pl.BlockSpec(memory_space=pl.ANY)
VMEM_LIMIT = 52 * 1024 * 1024


def _dot(a, b, dims):
    return lax.dot_general(a, b, (dims, ((), ())), preferred_element_type=F32)


def _cp(*sem):
    return pltpu.CompilerParams(dimension_semantics=sem, vmem_limit_bytes=VMEM_LIMIT)


def _sigmoid(x):
    return 1.0 / (1.0 + jnp.exp(-x))


def _sigmoid_t(x):
    return 0.5 * jnp.tanh(0.5 * x) + 0.5


def _log1p(e):
    u = 1.0 + e
    return jnp.where(u == 1.0, e, jnp.log(u) * e / jnp.where(u == 1.0, 1.0, u - 1.0))


def _softplus(x):
    return jnp.maximum(x, 0.0) + _log1p(jnp.exp(-jnp.abs(x)))


def _neg_expm1(x):
    u = jnp.exp(x)
    um1 = u - 1.0
    safe = jnp.logical_and(u != 1.0, um1 != -1.0)
    r = um1 * x / jnp.where(safe, jnp.log(jnp.where(safe, u, 0.5)), 1.0)
    return -jnp.where(u == 1.0, x, jnp.where(um1 == -1.0, -1.0, r))


def _gelu(y):
    c = 0.7978845608028654
    t = jnp.tanh(c * (y + 0.044715 * y * y * y))
    return 0.5 * y * (1.0 + t), t


def _gelu_grad(y, t):
    c = 0.7978845608028654
    return 0.5 * (1.0 + t) + 0.5 * y * (1.0 - t * t) * c * (1.0 + 3.0 * 0.044715 * y * y)


def _ln_fwd(z, g, b):
    mu = jnp.mean(z, axis=-1, keepdims=True)
    zc = z - mu
    var = jnp.mean(zc * zc, axis=-1, keepdims=True)
    rstd = lax.rsqrt(var + LN_EPS)
    xh = zc * rstd
    return xh * g + b, xh, rstd


def _ln_bwd(dy, xh, rstd, g):
    dxh = dy * g
    m1 = jnp.mean(dxh, axis=-1, keepdims=True)
    m2 = jnp.mean(dxh * xh, axis=-1, keepdims=True)
    return rstd * (dxh - m1 - xh * m2)


def _row_tile(t, want):
    return min(want, t)


def _me():
    return 4 * lax.axis_index("x") + 2 * lax.axis_index("y") + lax.axis_index("c")


def _coords(p):
    return (p // 4, (p // 2) % 2, p % 2)


def _all_to_all_copies(srcs_of, dsts_of, waits, send_sems, recv_sems, loc_sems):
    me = _me()
    n = len(waits)
    own_src, own_dst = srcs_of(me), dsts_of(me)
    local = [pltpu.make_async_copy(own_src[k], own_dst[k], loc_sems.at[k]) for k in range(n)]
    for cp in local:
        cp.start()
    for d in range(1, N_DEV):
        p = (me + d) % N_DEV
        src, dst = srcs_of(p), dsts_of(me)
        for k in range(n):
            pltpu.make_async_remote_copy(
                src_ref=src[k], dst_ref=dst[k], send_sem=send_sems.at[waits[k][1]],
                recv_sem=recv_sems.at[waits[k][1]], device_id=_coords(p), device_id_type=MESH).start()
    done = set()
    for k in range(n):
        ref, s = waits[k]
        if s in done:
            continue
        done.add(s)
        pltpu.make_async_remote_copy(
            src_ref=ref, dst_ref=ref, send_sem=send_sems.at[s], recv_sem=recv_sems.at[s],
            device_id=_coords(me), device_id_type=MESH).wait()
    for cp in local:
        cp.wait()


HBM = pl.BlockSpec(memory_space=pltpu.HBM)
SEM = pl.BlockSpec(memory_space=pltpu.SEMAPHORE)
EFFECT = pltpu.SideEffectType.DATAFLOW_SIDE_EFFECTING


def _in_hbm(a):
    return pltpu.with_memory_space_constraint(a, pltpu.HBM)


class Exchange:
    def __init__(self, name, src_of, dst_of, sem_of, span_of):
        self.name, self.src_of, self.dst_of, self.sem_of, self.span_of = name, src_of, dst_of, sem_of, span_of
        self.nsem = max(sem_of) + 1

    def start(self, srcs, lands, thru):
        n, m = len(srcs), len(lands)
        ops = list(srcs) + list(lands) + list(thru)

        def body(*refs):
            src_refs, land_refs = refs[:n], refs[n:n + m]
            send_sems, recv_sems = refs[len(ops)], refs[len(ops) + 1]
            me = _me()
            for dd in range(1, N_DEV):
                p = (me + dd) % N_DEV
                s, d = self.src_of(src_refs, p), self.dst_of(land_refs, me)
                for k in range(len(self.sem_of)):
                    pltpu.make_async_remote_copy(
                        src_ref=s[k], dst_ref=d[k], send_sem=send_sems.at[self.sem_of[k]],
                        recv_sem=recv_sems.at[self.sem_of[k]], device_id=_coords(p), device_id_type=MESH).start()

        sem = pltpu.SemaphoreType.DMA((self.nsem,))
        res = pl.pallas_call(
            body, name=self.name + "_start",
            out_shape=(sem, sem) + tuple(pltpu.HBM(a.shape, a.dtype) for a in ops),
            in_specs=[HBM] * len(ops), out_specs=(SEM, SEM) + (HBM,) * len(ops),
            input_output_aliases={i: 2 + i for i in range(len(ops))},
            compiler_params=pltpu.CompilerParams(has_side_effects=EFFECT),
        )(*[_in_hbm(a) for a in ops])
        return res[0], res[1], res[2:2 + n], res[2 + n:2 + n + m], list(res[2 + n + m:])

    def wait(self, send_sems, recv_sems, srcs, lands, thru):
        n, m = len(srcs), len(lands)
        ops = list(srcs) + list(lands) + list(thru)

        def body(*refs):
            land_refs = refs[n:n + m]
            ssem, rsem = refs[len(ops)], refs[len(ops) + 1]
            me = _me()
            spans = self.span_of(land_refs)
            for s in range(self.nsem):
                cp = pltpu.make_async_remote_copy(
                    src_ref=spans[s], dst_ref=spans[s], send_sem=ssem.at[s], recv_sem=rsem.at[s],
                    device_id=_coords(me), device_id_type=MESH)
                cp.wait_send()
                cp.wait_recv()

        res = pl.pallas_call(
            body, name=self.name + "_wait",
            out_shape=tuple(pltpu.HBM(a.shape, a.dtype) for a in ops),
            in_specs=[HBM] * len(ops) + [SEM, SEM], out_specs=(HBM,) * len(ops),
            input_output_aliases={i: i for i in range(len(ops))},
            compiler_params=pltpu.CompilerParams(has_side_effects=EFFECT),
        )(*ops, send_sems, recv_sems)
        return res[n:n + m], list(res[n + m:])


class ChipGather:
    NARR = 6

    def __init__(self, name):
        self.name = name

    @staticmethod
    def _slots(lands, p):
        o_win, o_w3, o_w1, o_w2, o_rgw, o_sm = lands
        return [o_win.at[:, pl.ds(pl.multiple_of(p * WIN_S, 128), WIN_S)],
                o_w3.at[:, pl.ds(pl.multiple_of(p * 128, 128), 128), :],
                o_w1.at[:, p],
                o_w2.at[:, pl.ds(pl.multiple_of(p * W2S, 16), W2S), :],
                o_rgw.at[:, :, pl.ds(pl.multiple_of(p * 32, 32), 32), :],
                o_sm.at[:, pl.ds(pl.multiple_of(p * 128, 128), 128)]]

    @staticmethod
    def _places():
        x, y, c = lax.axis_index("x"), lax.axis_index("y"), lax.axis_index("c")
        chips = [(1 - x, y), (x, 1 - y), (1 - x, 1 - y)]
        return (x, y, c), (x, y, 1 - c), chips

    @staticmethod
    def _index(place):
        return 4 * place[0] + 2 * place[1] + place[2]

    def _call(self, tag, body, ops, sems_in, sems_out):
        nops = len(ops)
        sem = pltpu.SemaphoreType.DMA((max(sems_out, 1),))
        res = pl.pallas_call(
            body, name=f"{self.name}_{tag}",
            out_shape=((sem, sem) if sems_out else ()) + tuple(pltpu.HBM(a.shape, a.dtype) for a in ops),
            in_specs=[HBM] * nops + [SEM] * len(sems_in),
            out_specs=((SEM, SEM) if sems_out else ()) + (HBM,) * nops,
            input_output_aliases={i: (2 if sems_out else 0) + i for i in range(nops)},
            compiler_params=pltpu.CompilerParams(has_side_effects=EFFECT),
        )(*[_in_hbm(a) for a in ops], *sems_in)
        return res

    def first(self, srcs, lands, thru):
        n = self.NARR
        ops = list(srcs) + list(lands) + list(thru)

        def body(*refs):
            src_refs, land_refs = refs[:n], refs[n:2 * n]
            send_sems, recv_sems = refs[len(ops)], refs[len(ops) + 1]
            me, sibling, chips = self._places()
            dst = self._slots(land_refs, self._index(me))
            targets = [sibling] + [(cx, cy, me[2]) for cx, cy in chips]
            for r, to in enumerate(targets):
                for k in range(n):
                    pltpu.make_async_remote_copy(
                        src_ref=src_refs[k], dst_ref=dst[k], send_sem=send_sems.at[6 * r + k],
                        recv_sem=recv_sems.at[6 * r + k], device_id=to, device_id_type=MESH).start()

        res = self._call("first", body, ops, [], 4 * n)
        return (res[0], res[1]), list(res[2:2 + n]), list(res[2 + n:2 + 2 * n]), list(res[2 + 2 * n:])

    def relay(self, sems, srcs, lands, thru):
        n = self.NARR
        ops = list(srcs) + list(lands) + list(thru)

        def wait_body(*refs):
            land_refs = refs[n:2 * n]
            ssem, rsem = refs[len(ops)], refs[len(ops) + 1]
            me, sibling, chips = self._places()
            for j, (cx, cy) in enumerate(chips):
                got = self._slots(land_refs, self._index((cx, cy, me[2])))
                for k in range(n):
                    pltpu.make_async_remote_copy(
                        src_ref=got[k], dst_ref=got[k], send_sem=ssem.at[6 * (1 + j) + k], recv_sem=rsem.at[6 * (1 + j) + k],
                        device_id=me, device_id_type=MESH).wait_recv()

        ops = list(self._call("landed", wait_body, ops, list(sems), 0))

        def start_body(*refs):
            land_refs = refs[n:2 * n]
            ssem, rsem = refs[len(ops)], refs[len(ops) + 1]
            me, sibling, chips = self._places()
            for j, (cx, cy) in enumerate(chips):
                got = self._slots(land_refs, self._index((cx, cy, me[2])))
                for k in range(n):
                    pltpu.make_async_remote_copy(
                        src_ref=got[k], dst_ref=got[k], send_sem=ssem.at[6 * j + k], recv_sem=rsem.at[6 * j + k],
                        device_id=sibling, device_id_type=MESH).start()

        res = self._call("relay", start_body, ops, [], 3 * n)
        return (res[0], res[1]), list(res[2:2 + n]), list(res[2 + n:2 + 2 * n]), list(res[2 + 2 * n:])

    def last(self, sems, relay_sems, srcs, lands, thru):
        n = self.NARR
        ops = list(srcs) + list(lands) + list(thru)

        def body(*refs):
            src_refs, land_refs = refs[:n], refs[n:2 * n]
            ssem, rsem, ssem2, rsem2 = refs[len(ops):len(ops) + 4]
            me, sibling, chips = self._places()
            got = self._slots(land_refs, self._index(sibling))
            for k in range(n):
                pltpu.make_async_remote_copy(
                    src_ref=got[k], dst_ref=got[k], send_sem=ssem.at[k], recv_sem=rsem.at[k],
                    device_id=me, device_id_type=MESH).wait_recv()
            for j, (cx, cy) in enumerate(chips):
                got = self._slots(land_refs, self._index((cx, cy, 1 - me[2])))
                for k in range(n):
                    pltpu.make_async_remote_copy(
                        src_ref=got[k], dst_ref=got[k], send_sem=ssem2.at[6 * j + k], recv_sem=rsem2.at[6 * j + k],
                        device_id=me, device_id_type=MESH).wait_recv()
            for sem_s, sem_r, count in ((ssem, rsem, 4), (ssem2, rsem2, 3)):
                for r in range(count):
                    for k in range(n):
                        pltpu.make_async_remote_copy(
                            src_ref=src_refs[k], dst_ref=src_refs[k], send_sem=sem_s.at[6 * r + k],
                            recv_sem=sem_r.at[6 * r + k], device_id=me, device_id_type=MESH).wait_send()

        res = self._call("last", body, ops, list(sems) + list(relay_sems), 0)
        return list(res[n:2 * n]), list(res[2 * n:])


GATHER_SHAPES = (SDS((D, D_IN), BF16), SDS((3, D, D), BF16), SDS((2, N_DEV, D, FS), BF16),
                 SDS((2, D_FF, D), BF16), SDS((2, HEADS, HD, HD), BF16), SDS((SMALL_ROWS, D), F32))


def _slab(p, n, align):
    return pl.ds(pl.multiple_of(p * n, align), n)


class GradGroup:
    def __init__(self, nsrc, lands, src_of, dst_of, sem_of, in_specs, out_specs, copy):
        self.nsrc, self.lands, self.src_of, self.dst_of, self.sem_of = nsrc, lands, src_of, dst_of, sem_of
        self.in_specs, self.out_specs, self.copy = in_specs, out_specs, copy


def _mixer_group():
    hd = D // 2

    def src_of(refs, p):
        dwin, dwoc, dwor, dwo, drgw = refs
        r3 = _slab(p, 128, 128)
        return [dwin.at[:, _slab(p, WIN_S, 128)], dwoc.at[r3, :], dwor.at[r3, :], dwo.at[r3, :],
                drgw.at[:, :, _slab(p, 32, 32), :]]

    def dst_of(lands, p):
        l_win, l_w3, l_rgw = lands
        return [l_win.at[p], l_w3.at[p, 0], l_w3.at[p, 1], l_w3.at[p, 2], l_rgw.at[p]]

    def copy(srcs, lands):
        lands[0][...] = srcs[0][...]
        for k in range(3):
            lands[1][k] = srcs[1 + k][...]
        lands[2][...] = srcs[4][...]

    three = pl.BlockSpec((64, D), lambda i, me: (2 * me[0] + i, 0))
    return GradGroup(
        5, (SDS((N_DEV, D, WIN_S), BF16), SDS((N_DEV, 3, 128, D), BF16), SDS((N_DEV, 2, HEADS, 32, HD), BF16)),
        src_of, dst_of, [0, 1, 1, 1, 2],
        [pl.BlockSpec((hd, WIN_S), lambda i, me: (i, me[0])), three, three, three,
         pl.BlockSpec((2, HEADS, 16, HD), lambda i, me: (0, 0, 2 * me[0] + i, 0))],
        [pl.BlockSpec((None, hd, WIN_S), lambda i, me: (me[0], i, 0)),
         pl.BlockSpec((None, 3, 64, D), lambda i, me: (me[0], 0, i, 0)),
         pl.BlockSpec((None, 2, HEADS, 16, HD), lambda i, me: (me[0], 0, 0, i, 0))],
        copy)


def _ffn_group():
    hd, hw = D // 2, W2S // 2

    def src_of(refs, p):
        return [refs[0].at[p], refs[1].at[_slab(p, W2S, 16), :]]

    def dst_of(lands, p):
        return [lands[0].at[p], lands[1].at[p]]

    def copy(srcs, lands):
        lands[0][...] = srcs[0][...]
        lands[1][...] = srcs[1][...]

    return GradGroup(
        2, (SDS((N_DEV, D, FS), BF16), SDS((N_DEV, W2S, D), BF16)), src_of, dst_of, [0, 1],
        [pl.BlockSpec((None, hd, FS), lambda i, me: (me[0], i, 0)), pl.BlockSpec((hw, D), lambda i, me: (2 * me[0] + i, 0))],
        [pl.BlockSpec((None, hd, FS), lambda i, me: (me[0], i, 0)), pl.BlockSpec((None, hw, D), lambda i, me: (me[0], i, 0))],
        copy)


def grad_scatter(name, kinds):
    groups = [_mixer_group() if k == "mixer" else _ffn_group() for k in kinds]

    def per_group(refs, counts, fn):
        out, i = [], 0
        for g, c in zip(groups, counts):
            out += fn(g, refs[i:i + c])
            i += c
        return out

    nsrcs = [g.nsrc for g in groups]
    nlands = [len(g.lands) for g in groups]
    sem_of, off = [], 0
    for g in groups:
        sem_of += [off + s for s in g.sem_of]
        off += len(g.lands)

    ex = Exchange(name,
                  lambda refs, p: per_group(refs, nsrcs, lambda g, r: g.src_of(r, p)),
                  lambda lands, p: per_group(lands, nlands, lambda g, r: g.dst_of(r, p)),
                  sem_of, lambda lands: [a.at[pl.ds(0, 7)] for a in lands])

    def prep(me1, grads):
        nsrc = sum(nsrcs)

        def body(me_ref, *refs):
            srcs, lands = refs[:nsrc], refs[nsrc:]
            i = j = 0
            for g in groups:
                g.copy(srcs[i:i + g.nsrc], lands[j:j + len(g.lands)])
                i += g.nsrc
                j += len(g.lands)

        return list(pl.pallas_call(
            body, name=name + "_prep", out_shape=tuple(s for g in groups for s in g.lands),
            grid_spec=pltpu.PrefetchScalarGridSpec(
                num_scalar_prefetch=1, grid=(2,), in_specs=[s for g in groups for s in g.in_specs],
                out_specs=tuple(s for g in groups for s in g.out_specs)),
            compiler_params=_cp("arbitrary"),
        )(me1, *grads))

    return ex, prep


def gather_prep(l, me1, w_in, w_out_conv, w_out_rnn, w_o, ffn_w1, ffn_w2, rg_w, small):
    hd, hw = D // 2, W2S // 2

    def body(me_ref, win, woc, wor, wo, w1, w2, rgw, sm, c_win, c_w3, c_w1, c_w2, c_rgw, c_sm,
             o_win, o_w3, o_w1, o_w2, o_rgw, o_sm):
        a = win[...].astype(BF16)
        c_win[...] = a
        o_win[...] = a
        for k, r in enumerate((woc, wor, wo)):
            b = r[...].astype(BF16)
            c_w3[k] = b
            o_w3[k] = b
        for src, comp, own in ((w1, c_w1, o_w1), (w2, c_w2, o_w2), (rgw, c_rgw, o_rgw)):
            b = src[...].astype(BF16)
            comp[...] = b
            own[...] = b
        c_sm[...] = sm[...]
        o_sm[...] = sm[...]

    three = pl.BlockSpec((None, 64, D), lambda i, me: (l, i, 0))
    in_specs = [pl.BlockSpec((None, hd, WIN_S), lambda i, me: (l, i, 0)), three, three, three,
                pl.BlockSpec((None, 2, hd, FS), lambda i, me: (l, 0, i, 0)),
                pl.BlockSpec((None, 2, hw, D), lambda i, me: (l, 0, i, 0)),
                pl.BlockSpec((None, 2, HEADS, 16, HD), lambda i, me: (l, 0, 0, i, 0)),
                pl.BlockSpec((None, 8, 128), lambda i, me: (l, i, 0))]
    out_specs = (pl.BlockSpec((hd, WIN_S), lambda i, me: (i, 0)), pl.BlockSpec((3, 64, D), lambda i, me: (0, i, 0)),
                 pl.BlockSpec((2, hd, FS), lambda i, me: (0, i, 0)), pl.BlockSpec((2, hw, D), lambda i, me: (0, i, 0)),
                 pl.BlockSpec((2, HEADS, 16, HD), lambda i, me: (0, 0, i, 0)), pl.BlockSpec((8, 128), lambda i, me: (i, 0)),
                 pl.BlockSpec((hd, WIN_S), lambda i, me: (i, me[0])),
                 pl.BlockSpec((3, 64, D), lambda i, me: (0, 2 * me[0] + i, 0)),
                 pl.BlockSpec((2, None, hd, FS), lambda i, me: (0, me[0], i, 0)),
                 pl.BlockSpec((2, hw, D), lambda i, me: (0, 2 * me[0] + i, 0)),
                 pl.BlockSpec((2, HEADS, 16, HD), lambda i, me: (0, 0, 2 * me[0] + i, 0)),
                 pl.BlockSpec((8, 128), lambda i, me: (i, me[0])))
    compact = (SDS((D, WIN_S), BF16), SDS((3, 128, D), BF16), SDS((2, D, FS), BF16), SDS((2, W2S, D), BF16),
               SDS((2, HEADS, 32, HD), BF16), SDS((SMALL_ROWS, 128), F32))
    res = pl.pallas_call(
        body, name=f"gather_prep{l}", out_shape=compact + GATHER_SHAPES,
        grid_spec=pltpu.PrefetchScalarGridSpec(num_scalar_prefetch=1, grid=(2,), in_specs=in_specs, out_specs=out_specs),
        compiler_params=_cp("arbitrary"),
    )(me1, w_in, w_out_conv, w_out_rnn, w_o, ffn_w1, ffn_w2, rg_w, small)
    return list(res[:6]), list(res[6:])


def gather_small_grads(g_sharded, g_replicated):
    def body(ga, gb, la, lb, send_sems, recv_sems, loc_sems):
        _all_to_all_copies(lambda p: [ga, gb], lambda p: [la.at[p], lb.at[p]],
                           [(la.at[pl.ds(0, 7)], 0), (lb.at[pl.ds(0, 7)], 1)], send_sems, recv_sems, loc_sems)

    return pl.pallas_call(
        body, name="gather_small_grads",
        out_shape=(SDS((N_DEV,) + g_sharded.shape, F32), SDS((N_DEV,) + g_replicated.shape, F32)),
        in_specs=[ANY, ANY], out_specs=(ANY, ANY),
        scratch_shapes=[pltpu.SemaphoreType.DMA((2,)), pltpu.SemaphoreType.DMA((2,)), pltpu.SemaphoreType.DMA((2,))],
        compiler_params=pltpu.CompilerParams(has_side_effects=True),
    )(g_sharded, g_replicated)


def ffn_up(xb, w1, f):
    t = xb.shape[0]
    tm = _row_tile(t, 1024)

    def body(x_ref, wg_ref, wu_ref, g_ref, u_ref, a_ref):
        x = x_ref[...]
        g = _dot(x, wg_ref[...], NN)
        u = _dot(x, wu_ref[...], NN)
        g_ref[...] = g.astype(BF16)
        u_ref[...] = u.astype(BF16)
        a_ref[...] = (g * _sigmoid_t(g) * u).astype(BF16)

    out = pl.BlockSpec((None, tm, FS), lambda j, i: (j, i, 0))
    return pl.pallas_call(
        body, name=f"ffn_up{f}", grid=(4, t // tm),
        in_specs=[pl.BlockSpec((tm, D), lambda j, i: (i, 0)),
                  pl.BlockSpec((None, None, D, FS), lambda j, i: (f, j, 0, 0)),
                  pl.BlockSpec((None, None, D, FS), lambda j, i: (f, j + 4, 0, 0))],
        out_specs=(out, out, out), out_shape=(SDS((4, t, FS), BF16),) * 3,
        compiler_params=_cp("parallel", "parallel"),
    )(xb, w1, w1)


def ffn_down_ln(a, w2, f, xf, small, s):
    t = xf.shape[0]
    tm = _row_tile(t, 512)

    def body(a_ref, w_ref, x_ref, sm_ref, xo_ref, xb_ref, xh_ref, rs_ref):
        acc = _dot(a_ref[0], w_ref[0:FS, :], NN)
        for j in range(1, 4):
            acc = acc + _dot(a_ref[j], w_ref[j * FS:(j + 1) * FS, :], NN)
        z = ALPHA * x_ref[...] + 0.5 * acc
        y, xh, rstd = _ln_fwd(z, sm_ref[R_LNG + s:R_LNG + s + 1, :], sm_ref[R_LNB + s:R_LNB + s + 1, :])
        xo_ref[...] = y
        xb_ref[...] = y.astype(BF16)
        xh_ref[...] = xh
        rs_ref[...] = rstd

    row = pl.BlockSpec((tm, D), lambda i: (i, 0))
    return pl.pallas_call(
        body, name=f"ffn_down_ln{f}", grid=(t // tm,),
        in_specs=[pl.BlockSpec((4, tm, FS), lambda i: (0, i, 0)),
                  pl.BlockSpec((None, D_FF, D), lambda i: (f, 0, 0)),
                  row, pl.BlockSpec((SMALL_ROWS, D), lambda i: (0, 0))],
        out_specs=(row, row, row, pl.BlockSpec((tm, 1), lambda i: (i, 0))),
        out_shape=(SDS((t, D), F32), SDS((t, D), BF16), SDS((t, D), F32), SDS((t, 1), F32)),
        compiler_params=_cp("parallel"),
    )(a, w2, xf, small)


def ffn_bwd_gates(dxn, xh, rstd, small, s, w2, f, g, u):
    t = dxn.shape[0]
    tm = _row_tile(t, 256)

    def body(dy_ref, xh_ref, rs_ref, sm_ref, w_ref, g_ref, u_ref, dz_ref, df_ref, dgu_ref, dln_ref):
        i = pl.program_id(0)
        dy = dy_ref[...]
        xhat = xh_ref[...]
        dz = _ln_bwd(dy, xhat, rs_ref[...], sm_ref[R_LNG + s:R_LNG + s + 1, :])

        @pl.when(i == 0)
        def _():
            dln_ref[...] = jnp.zeros_like(dln_ref)

        dln_ref[0:1, :] += jnp.sum(dy * xhat, axis=0, keepdims=True)
        dln_ref[1:2, :] += jnp.sum(dy, axis=0, keepdims=True)
        dz_ref[...] = dz
        df = (0.5 * dz).astype(BF16)
        df_ref[...] = df
        for j in range(4):
            da = _dot(df, w_ref[j * FS:(j + 1) * FS, :], NT)
            gg = g_ref[j].astype(F32)
            uu = u_ref[j].astype(F32)
            sg = _sigmoid_t(gg)
            dgu_ref[j] = (da * uu * (sg * (1.0 + gg * (1.0 - sg)))).astype(BF16)
            dgu_ref[j + 4] = (da * (gg * sg)).astype(BF16)

    row = pl.BlockSpec((tm, D), lambda i: (i, 0))
    gu = pl.BlockSpec((4, tm, FS), lambda i: (0, i, 0))
    return pl.pallas_call(
        body, name=f"ffn_bwd_gates{f}", grid=(t // tm,),
        in_specs=[row, row, pl.BlockSpec((tm, 1), lambda i: (i, 0)), pl.BlockSpec((SMALL_ROWS, D), lambda i: (0, 0)),
                  pl.BlockSpec((None, D_FF, D), lambda i: (f, 0, 0)), gu, gu],
        out_specs=(row, row, pl.BlockSpec((8, tm, FS), lambda i: (0, i, 0)), pl.BlockSpec((2, D), lambda i: (0, 0))),
        out_shape=(SDS((t, D), F32), SDS((t, D), BF16), SDS((8, t, FS), BF16), SDS((2, D), F32)),
        compiler_params=_cp("arbitrary"),
    )(dxn, xh, rstd, small, w2, g, u)


def ffn_bwd_dx(dz, dgu, w1, f):
    t = dz.shape[0]
    tm = _row_tile(t, 1024)

    def body(dz_ref, d_ref, w_ref, dx_ref, acc):
        k = pl.program_id(1)

        @pl.when(k == 0)
        def _():
            acc[...] = ALPHA * dz_ref[...]

        acc[...] += _dot(d_ref[...], w_ref[...], NT)

        @pl.when(k == 7)
        def _():
            dx_ref[...] = acc[...]

    row = pl.BlockSpec((tm, D), lambda i, k: (i, 0))
    return pl.pallas_call(
        body, name=f"ffn_bwd_dx{f}", grid=(t // tm, 8),
        in_specs=[row, pl.BlockSpec((None, tm, FS), lambda i, k: (k, i, 0)),
                  pl.BlockSpec((None, None, D, FS), lambda i, k: (f, k, 0, 0))],
        out_specs=row, out_shape=SDS((t, D), F32),
        scratch_shapes=[pltpu.VMEM((tm, D), F32)],
        compiler_params=_cp("parallel", "arbitrary"),
    )(dz, dgu, w1)


def _mm_tn(name, a, a_spec, b, b_spec, out_sds, out_spec, grid):
    def body(a_ref, b_ref, o_ref):
        o_ref[...] = _dot(a_ref[...], b_ref[...], TN).astype(o_ref.dtype)

    return pl.pallas_call(
        body, name=name, grid=grid, in_specs=[a_spec, b_spec], out_specs=out_spec, out_shape=out_sds,
        compiler_params=_cp(*(["parallel"] * len(grid))),
    )(a, b)


def ffn_dw1(xb, dgu, f):
    t = xb.shape[0]
    return _mm_tn(f"ffn_dw1_{f}", xb, pl.BlockSpec((t, D), lambda j: (0, 0)),
                  dgu, pl.BlockSpec((None, t, FS), lambda j: (j, 0, 0)),
                  SDS((8, D, FS), BF16), pl.BlockSpec((None, D, FS), lambda j: (j, 0, 0)), (8,))


def ffn_dw2(a, df, f):
    t = df.shape[0]
    return _mm_tn(f"ffn_dw2_{f}", a, pl.BlockSpec((None, t, FS), lambda j: (j, 0, 0)),
                  df, pl.BlockSpec((t, D), lambda j: (0, 0)),
                  SDS((D_FF, D), BF16), pl.BlockSpec((FS, D), lambda j: (j, 0)), (4,))


def mm_tn_square(name, a, b):
    t = a.shape[0]
    return _mm_tn(name, a, pl.BlockSpec((t, 512), lambda i: (0, i)),
                  b, pl.BlockSpec((t, D), lambda i: (0, 0)),
                  SDS((D, D), BF16), pl.BlockSpec((512, D), lambda i: (i, 0)), (2,))


def mix_proj(xb, win):
    t = xb.shape[0]
    tm = _row_tile(t, 1024)

    def body(x_ref, w_ref, o_ref):
        o_ref[...] = _dot(x_ref[...], w_ref[...], NN).astype(BF16)

    return pl.pallas_call(
        body, name="mix_proj", grid=(7, t // tm),
        in_specs=[pl.BlockSpec((tm, D), lambda n, i: (i, 0)), pl.BlockSpec((D, D), lambda n, i: (0, n))],
        out_specs=pl.BlockSpec((tm, D), lambda n, i: (i, n)), out_shape=SDS((t, D_IN), BF16),
        compiler_params=_cp("parallel", "parallel"),
    )(xb, win)


def _pcol(tm, k):
    return pl.BlockSpec((tm, D), lambda i: (i, k))


def _prev_halo(tm, k):
    return pl.BlockSpec((8, D), lambda i: (jnp.maximum(i * (tm // 8) - 1, 0), k))


def _prev_halo16(tm, k):
    return pl.BlockSpec((16, D), lambda i: (jnp.maximum(i * (tm // 16) - 1, 0), k))


def _next_halo(tm, t, k):
    return pl.BlockSpec((8, D), lambda i: (jnp.minimum((i + 1) * (tm // 8), t // 8 - 1), k))


def _full(shape):
    nd = len(shape)
    return pl.BlockSpec(shape, lambda i: (0,) * nd)


def mix_pre(p, small, rcb, ap, rgw):
    t = p.shape[0]
    tm = _row_tile(t, 256)

    def body(bg_ref, cg_ref, v_ref, xr_ref, cgh_ref, vh_ref, xrh_ref, sm_ref, rcb_ref, ap_ref, rgw_ref,
             ca_ref, pa_ref, xc_ref, xcb_ref, gi_ref, gr_ref, h_ref, ext1, ext2, a_s, b_s, carry):
        i = pl.program_id(0)
        first = i == 0
        cv = cg_ref[...].astype(F32) * v_ref[...].astype(F32)
        ext1[0:16, :] = jnp.where(first, 0.0, cgh_ref[...].astype(F32) * vh_ref[...].astype(F32))
        ext1[16:, :] = cv
        xr = xr_ref[...].astype(F32)
        ext2[0:16, :] = jnp.where(first, 0.0, xrh_ref[...].astype(F32))
        ext2[16:, :] = xr
        ca = (sm_ref[R_SC:R_SC + 1, :] * ext1[pl.ds(14, tm), :] + sm_ref[R_SC + 1:R_SC + 2, :] * ext1[pl.ds(15, tm), :]
              + sm_ref[R_SC + 2:R_SC + 3, :] * cv)
        ca_ref[...] = ca.astype(BF16)
        pa_ref[...] = (bg_ref[...].astype(F32) * ca).astype(BF16)
        xc = (sm_ref[R_RC:R_RC + 1, :] * ext2[pl.ds(13, tm), :] + sm_ref[R_RC + 1:R_RC + 2, :] * ext2[pl.ds(14, tm), :]
              + sm_ref[R_RC + 2:R_RC + 3, :] * ext2[pl.ds(15, tm), :] + sm_ref[R_RC + 3:R_RC + 4, :] * xr
              + rcb_ref[...])
        xc_ref[...] = xc
        xcb = xc.astype(BF16)
        xcb_ref[...] = xcb
        g0, g1 = [], []
        for h in range(HEADS):
            xh = xcb[:, h * HD:(h + 1) * HD]
            g0.append(_dot(xh, rgw_ref[0, h], NN))
            g1.append(_dot(xh, rgw_ref[1, h], NN))
        gi = _sigmoid(jnp.concatenate(g0, axis=1) + sm_ref[R_RGB:R_RGB + 1, :])
        gr = _sigmoid(jnp.concatenate(g1, axis=1) + sm_ref[R_RGB + 1:R_RGB + 2, :])
        gi_ref[...] = gi
        gr_ref[...] = gr
        la = (-LRU_C) * gr * _softplus(-ap_ref[...])
        a_s[...] = jnp.exp(la)
        row = lax.broadcasted_iota(jnp.int32, (tm, D), 0) + i * tm
        mult = jnp.where(row == 0, 1.0, jnp.sqrt(_neg_expm1(2.0 * la)))
        b_s[...] = xc * gi * mult

        @pl.when(first)
        def _():
            carry[...] = jnp.zeros_like(carry)

        carry[...] = _scan_tile(a_s, b_s, a_s, carry[...], tm, reverse=False)
        h_ref[...] = a_s[...].astype(BF16)

    row = pl.BlockSpec((tm, D), lambda i: (i, 0))
    f32o, b16o = SDS((t, D), F32), SDS((t, D), BF16)
    ext, tile = pltpu.VMEM((tm + 16, D), F32), pltpu.VMEM((tm, D), F32)
    return pl.pallas_call(
        body, name="mix_pre", grid=(t // tm,),
        in_specs=[_pcol(tm, 0), _pcol(tm, 1), _pcol(tm, 2), _pcol(tm, 3),
                  _prev_halo16(tm, 1), _prev_halo16(tm, 2), _prev_halo16(tm, 3),
                  _full((SMALL_ROWS, D)), _full((1, D)), _full((1, D)), _full((2, HEADS, HD, HD))],
        out_specs=(row,) * 7, out_shape=(b16o, b16o, f32o, b16o, f32o, f32o, b16o),
        scratch_shapes=[ext, ext, tile, tile, pltpu.VMEM((8, D), F32)],
        compiler_params=_cp("arbitrary"),
    )(p, p, p, p, p, p, p, small, rcb, ap, rgw)


def _scan_tile(a_ref, b_ref, o_ref, carry, tm, reverse):
    width = a_ref.shape[1]
    ng = tm // 8
    row8 = lax.broadcasted_iota(jnp.int32, (8, width), 0)

    def step(g, c):
        r = pl.multiple_of((ng - 1 - g if reverse else g) * 8, 8)
        aa = a_ref[pl.ds(r, 8), :]
        bb = b_ref[pl.ds(r, 8), :]
        for s in (1, 2, 4):
            if reverse:
                keep, shift = row8 < 8 - s, 8 - s
            else:
                keep, shift = row8 >= s, s
            a_sh = jnp.where(keep, pltpu.roll(aa, shift, 0), 1.0)
            b_sh = jnp.where(keep, pltpu.roll(bb, shift, 0), 0.0)
            bb = aa * b_sh + bb
            aa = aa * a_sh
        o = aa * c + bb
        o_ref[pl.ds(r, 8), :] = o
        edge = o[0:1, :] if reverse else o[7:8, :]
        return jnp.broadcast_to(edge, (8, width))

    return lax.fori_loop(0, ng, step, carry)


def mix_out(pa, h, p, bm, w3, xf, small):
    t = xf.shape[0]
    tm = _row_tile(t, 256)

    def body(pa_ref, h_ref, yr_ref, gla_ref, glb_ref, bma_ref, bmb_ref, w_ref, x_ref, sm_ref,
             pb_ref, ya_ref, yb_ref, m_ref, xo_ref, xb_ref, xh_ref, rs_ref):
        ge, _ = _gelu(yr_ref[...].astype(F32))
        pb = (h_ref[...].astype(F32) * ge).astype(BF16)
        pb_ref[...] = pb
        ya = _dot(pa_ref[...], w_ref[0], NN)
        yb = _dot(pb, w_ref[1], NN)
        ya_ref[...] = ya.astype(BF16)
        yb_ref[...] = yb.astype(BF16)
        ga = _sigmoid_t(gla_ref[...].astype(F32) + bma_ref[...])
        gb = _sigmoid_t(glb_ref[...].astype(F32) + bmb_ref[...])
        m = (ga * ya + gb * yb).astype(BF16)
        m_ref[...] = m
        z = ALPHA * x_ref[...] + _dot(m, w_ref[2], NN)
        y, xh, rstd = _ln_fwd(z, sm_ref[R_LNG + 1:R_LNG + 2, :], sm_ref[R_LNB + 1:R_LNB + 2, :])
        xo_ref[...] = y
        xb_ref[...] = y.astype(BF16)
        xh_ref[...] = xh
        rs_ref[...] = rstd

    row = pl.BlockSpec((tm, D), lambda i: (i, 0))
    f32o, b16o = SDS((t, D), F32), SDS((t, D), BF16)
    return pl.pallas_call(
        body, name="mix_out", grid=(t // tm,),
        in_specs=[row, row, _pcol(tm, 4), _pcol(tm, 5), _pcol(tm, 6),
                  pl.BlockSpec((1, D), lambda i: (0, 0)), pl.BlockSpec((1, D), lambda i: (0, 1)),
                  _full((3, D, D)), row, _full((SMALL_ROWS, D))],
        out_specs=(row,) * 7 + (pl.BlockSpec((tm, 1), lambda i: (i, 0)),),
        out_shape=(b16o, b16o, b16o, b16o, f32o, b16o, f32o, SDS((t, 1), F32)),
        compiler_params=_cp("parallel"),
    )(pa, h, p, p, p, bm, bm, w3, xf, small)


def mixb_head(dxn, xh, rstd, small, w3, p, bm, ya, yb, ca, h):
    t = dxn.shape[0]
    tm = _row_tile(t, 256)

    def body(dy_ref, xh_ref, rs_ref, sm_ref, w_ref, bg_ref, yr_ref, gla_ref, glb_ref, bma_ref, bmb_ref,
             ya_ref, yb_ref, ca_ref, h_ref,
             dz_ref, dzb_ref, dya_ref, dyb_ref, dbg_ref, dca_ref, dh_ref, dphi_ref, dln_ref, dbm_ref):
        i = pl.program_id(0)
        dy = dy_ref[...]
        xhat = xh_ref[...]
        dz = _ln_bwd(dy, xhat, rs_ref[...], sm_ref[R_LNG + 1:R_LNG + 2, :])

        @pl.when(i == 0)
        def _():
            dln_ref[...] = jnp.zeros_like(dln_ref)
            dbm_ref[...] = jnp.zeros_like(dbm_ref)

        dln_ref[0:1, :] += jnp.sum(dy * xhat, axis=0, keepdims=True)
        dln_ref[1:2, :] += jnp.sum(dy, axis=0, keepdims=True)
        dz_ref[...] = dz
        dzb = dz.astype(BF16)
        dzb_ref[...] = dzb
        dm = _dot(dzb, w_ref[2], NT)
        ga = _sigmoid_t(gla_ref[...].astype(F32) + bma_ref[...])
        gb = _sigmoid_t(glb_ref[...].astype(F32) + bmb_ref[...])
        dya = (dm * ga).astype(BF16)
        dyb = (dm * gb).astype(BF16)
        dya_ref[...] = dya
        dyb_ref[...] = dyb
        dgla = dm * ya_ref[...].astype(F32) * ga * (1.0 - ga)
        dglb = dm * yb_ref[...].astype(F32) * gb * (1.0 - gb)
        dbm_ref[0:1, :] += jnp.sum(dgla, axis=0, keepdims=True)
        dbm_ref[1:2, :] += jnp.sum(dglb, axis=0, keepdims=True)
        dphi_ref[:, D:2 * D] = dgla.astype(BF16)
        dphi_ref[:, 2 * D:3 * D] = dglb.astype(BF16)
        dpa = _dot(dya, w_ref[0], NT)
        dpb = _dot(dyb, w_ref[1], NT)
        dbg_ref[...] = (dpa * ca_ref[...].astype(F32)).astype(BF16)
        dca_ref[...] = dpa * bg_ref[...].astype(F32)
        yr = yr_ref[...].astype(F32)
        ge, th = _gelu(yr)
        dh_ref[...] = (dpb * ge).astype(BF16)
        dphi_ref[:, 0:D] = (dpb * h_ref[...].astype(F32) * _gelu_grad(yr, th)).astype(BF16)

    row = pl.BlockSpec((tm, D), lambda i: (i, 0))
    f32o, b16o = SDS((t, D), F32), SDS((t, D), BF16)
    acc2 = pl.BlockSpec((2, D), lambda i: (0, 0))
    return pl.pallas_call(
        body, name="mixb_head", grid=(t // tm,),
        in_specs=[row, row, pl.BlockSpec((tm, 1), lambda i: (i, 0)), _full((SMALL_ROWS, D)), _full((3, D, D)),
                  _pcol(tm, 0), _pcol(tm, 4), _pcol(tm, 5), _pcol(tm, 6),
                  pl.BlockSpec((1, D), lambda i: (0, 0)), pl.BlockSpec((1, D), lambda i: (0, 1)),
                  row, row, row, row],
        out_specs=(row,) * 7 + (pl.BlockSpec((tm, 3 * D), lambda i: (i, 0)), acc2, acc2),
        out_shape=(f32o, b16o, b16o, b16o, b16o, f32o, b16o, SDS((t, 3 * D), BF16), SDS((2, D), F32), SDS((2, D), F32)),
        compiler_params=_cp("arbitrary"),
    )(dxn, xh, rstd, small, w3, p, p, p, p, bm, bm, ya, yb, ca, h)


def mixb_rec(dh, gr, gi, h, xc, ap, rgw):
    t = dh.shape[0]
    tm = _row_tile(t, 256)
    nt = t // tm

    def body(dh_ref, gr_ref, gi_ref, h_ref, hh_ref, xc_ref, ap_ref, rgw_ref, dg_ref, dxc_ref, red_ref,
             ext, ext_a, c_s, lam_s, lam_c, a_c):
        i = nt - 1 - pl.program_id(0)
        first = i == 0

        @pl.when(pl.program_id(0) == 0)
        def _():
            red_ref[...] = jnp.zeros_like(red_ref)
            lam_c[...] = jnp.zeros_like(lam_c)
            a_c[...] = jnp.zeros_like(a_c)

        ext[0:8, :] = jnp.where(first, 0.0, hh_ref[...].astype(F32)[8:16, :])
        ext[8:, :] = h_ref[...].astype(F32)
        hprev = ext[pl.ds(7, tm), :]
        lam_s[...] = dh_ref[...].astype(F32)
        gr = gr_ref[...]
        gi = gi_ref[...]
        xc = xc_ref[...]
        ap = ap_ref[...]
        sp = _softplus(-ap)
        la = (-LRU_C) * gr * sp
        a = jnp.exp(la)
        ext_a[0:tm, :] = a
        ext_a[tm:, :] = a_c[...]
        c_s[...] = ext_a[pl.ds(1, tm), :]
        lam_c[...] = _scan_tile(c_s, lam_s, lam_s, lam_c[...], tm, reverse=True)
        a_c[...] = jnp.broadcast_to(a[0:1, :], (8, D))
        lam = lam_s[...]
        row = lax.broadcasted_iota(jnp.int32, (tm, D), 0) + i * tm
        start = row == 0
        mult = jnp.where(start, 1.0, jnp.sqrt(_neg_expm1(2.0 * la)))
        dmult = jnp.where(start, 0.0, lam * xc * gi)
        dla = lam * hprev * a - dmult * a * a / mult
        dg1 = (-LRU_C) * sp * dla * gr * (1.0 - gr)
        dg0 = lam * xc * mult * gi * (1.0 - gi)
        dsp = jnp.sum((-LRU_C) * gr * dla, axis=0, keepdims=True)
        red_ref[0:1, :] += jnp.sum(dg0, axis=0, keepdims=True)
        red_ref[1:2, :] += jnp.sum(dg1, axis=0, keepdims=True)
        red_ref[2:3, :] += -dsp * _sigmoid(-ap)
        dg0b = dg0.astype(BF16)
        dg1b = dg1.astype(BF16)
        dg_ref[0] = dg0b
        dg_ref[1] = dg1b
        parts = []
        for hd in range(HEADS):
            sl = slice(hd * HD, (hd + 1) * HD)
            parts.append(_dot(dg0b[:, sl], rgw_ref[0, hd], NT) + _dot(dg1b[:, sl], rgw_ref[1, hd], NT))
        dxc = lam * gi * mult + jnp.concatenate(parts, axis=1)
        dxc_ref[...] = dxc
        red_ref[3:4, :] += jnp.sum(dxc, axis=0, keepdims=True)

    row = pl.BlockSpec((tm, D), lambda i: (nt - 1 - i, 0))
    halo = pl.BlockSpec((16, D), lambda i: (jnp.maximum((nt - 1 - i) * (tm // 16) - 1, 0), 0))
    ext, tile, edge = pltpu.VMEM((tm + 8, D), F32), pltpu.VMEM((tm, D), F32), pltpu.VMEM((8, D), F32)
    return pl.pallas_call(
        body, name="mixb_rec", grid=(nt,),
        in_specs=[row, row, row, row, halo, row, _full((1, D)), _full((2, HEADS, HD, HD))],
        out_specs=(pl.BlockSpec((2, tm, D), lambda i: (0, nt - 1 - i, 0)), row, pl.BlockSpec((8, D), lambda i: (0, 0))),
        out_shape=(SDS((2, t, D), BF16), SDS((t, D), F32), SDS((8, D), F32)),
        scratch_shapes=[ext, ext, tile, tile, edge, edge],
        compiler_params=_cp("arbitrary"),
    )(dh, gr, gi, h, h, xc, ap, rgw)


def mixb_conv(dca, dxc, dbg, p, small):
    t = dca.shape[0]
    tm = _row_tile(t, 256)
    nt = t // tm

    def body(dca_ref, dcan_ref, dxc_ref, dxcn_ref, dbg_ref, cg_ref, v_ref, xr_ref, cgh_ref, vh_ref, xrh_ref, sm_ref,
             dplo_ref, dxr_ref, red_ref, e_dca, e_dxc, e_cv, e_xr):
        i = pl.program_id(0)
        first = i == 0
        last = i == nt - 1

        @pl.when(first)
        def _():
            red_ref[...] = jnp.zeros_like(red_ref)

        dca = dca_ref[...]
        dxc = dxc_ref[...]
        e_dca[0:tm, :] = dca
        e_dca[tm:, :] = jnp.where(last, 0.0, dcan_ref[...])
        e_dxc[0:tm, :] = dxc
        e_dxc[tm:, :] = jnp.where(last, 0.0, dxcn_ref[...])
        cg = cg_ref[...].astype(F32)
        v = v_ref[...].astype(F32)
        xr = xr_ref[...].astype(F32)
        e_cv[0:8, :] = jnp.where(first, 0.0, (cgh_ref[...].astype(F32) * vh_ref[...].astype(F32))[8:16, :])
        e_cv[8:, :] = cg * v
        e_xr[0:8, :] = jnp.where(first, 0.0, xrh_ref[...].astype(F32)[8:16, :])
        e_xr[8:, :] = xr
        dcv = (sm_ref[R_SC + 2:R_SC + 3, :] * dca + sm_ref[R_SC + 1:R_SC + 2, :] * e_dca[pl.ds(1, tm), :]
               + sm_ref[R_SC:R_SC + 1, :] * e_dca[pl.ds(2, tm), :])
        dplo_ref[:, 0:D] = dbg_ref[...]
        dplo_ref[:, D:2 * D] = (dcv * v).astype(BF16)
        dplo_ref[:, 2 * D:3 * D] = (dcv * cg).astype(BF16)
        dxr = (sm_ref[R_RC + 3:R_RC + 4, :] * dxc + sm_ref[R_RC + 2:R_RC + 3, :] * e_dxc[pl.ds(1, tm), :]
               + sm_ref[R_RC + 1:R_RC + 2, :] * e_dxc[pl.ds(2, tm), :] + sm_ref[R_RC:R_RC + 1, :] * e_dxc[pl.ds(3, tm), :])
        dxr_ref[...] = dxr.astype(BF16)
        for k in range(3):
            red_ref[R_SC + k:R_SC + k + 1, :] += jnp.sum(dca * e_cv[pl.ds(6 + k, tm), :], axis=0, keepdims=True)
        for k in range(4):
            red_ref[R_RC + k:R_RC + k + 1, :] += jnp.sum(dxc * e_xr[pl.ds(5 + k, tm), :], axis=0, keepdims=True)

    row = pl.BlockSpec((tm, D), lambda i: (i, 0))
    ext = pltpu.VMEM((tm + 8, D), F32)
    return pl.pallas_call(
        body, name="mixb_conv", grid=(nt,),
        in_specs=[row, _next_halo(tm, t, 0), row, _next_halo(tm, t, 0), row,
                  _pcol(tm, 1), _pcol(tm, 2), _pcol(tm, 3), _prev_halo16(tm, 1), _prev_halo16(tm, 2), _prev_halo16(tm, 3),
                  _full((SMALL_ROWS, D))],
        out_specs=(pl.BlockSpec((tm, 3 * D), lambda i: (i, 0)), row, pl.BlockSpec((8, D), lambda i: (0, 0))),
        out_shape=(SDS((t, 3 * D), BF16), SDS((t, D), BF16), SDS((8, D), F32)),
        scratch_shapes=[ext, ext, ext, ext],
        compiler_params=_cp("arbitrary"),
    )(dca, dca, dxc, dxc, dbg, p, p, p, p, p, p, small)


def mixb_dx(dz, dplo, dxr, dphi, win):
    t = dz.shape[0]
    tm = _row_tile(t, 1024)

    def body(dz_ref, lo_ref, xr_ref, hi_ref, w_ref, dx_ref, acc):
        k = pl.program_id(1)

        @pl.when(k == 0)
        def _():
            acc[...] = ALPHA * dz_ref[...]

        @pl.when(k < 3)
        def _():
            acc[...] += _dot(lo_ref[...], w_ref[...], NT)

        @pl.when(k == 3)
        def _():
            acc[...] += _dot(xr_ref[...], w_ref[...], NT)

        @pl.when(k > 3)
        def _():
            acc[...] += _dot(hi_ref[...], w_ref[...], NT)

        @pl.when(k == 6)
        def _():
            dx_ref[...] = acc[...]

    row = pl.BlockSpec((tm, D), lambda i, k: (i, 0))
    return pl.pallas_call(
        body, name="mixb_dx", grid=(t // tm, 7),
        in_specs=[row, pl.BlockSpec((tm, D), lambda i, k: (i, jnp.minimum(k, 2))), row,
                  pl.BlockSpec((tm, D), lambda i, k: (i, jnp.clip(k - 4, 0, 2))),
                  pl.BlockSpec((D, D), lambda i, k: (0, k))],
        out_specs=row, out_shape=SDS((t, D), F32),
        scratch_shapes=[pltpu.VMEM((tm, D), F32)],
        compiler_params=_cp("parallel", "arbitrary"),
    )(dz, dplo, dxr, dphi, win)


def mixb_dwin(xb, dplo, dxr, dphi):
    t = xb.shape[0]
    tk = _row_tile(t, 2048)
    nk = t // tk

    def body(x_ref, lo_ref, xr_ref, hi_ref, o_ref, acc):
        n = pl.program_id(0)
        k = pl.program_id(1)

        @pl.when(k == 0)
        def _():
            acc[...] = jnp.zeros_like(acc)

        @pl.when(n < 3)
        def _():
            acc[...] += _dot(x_ref[...], lo_ref[...], TN)

        @pl.when(n == 3)
        def _():
            acc[...] += _dot(x_ref[...], xr_ref[...], TN)

        @pl.when(n > 3)
        def _():
            acc[...] += _dot(x_ref[...], hi_ref[...], TN)

        @pl.when(k == nk - 1)
        def _():
            o_ref[...] = acc[...].astype(BF16)

    return pl.pallas_call(
        body, name="mixb_dwin", grid=(7, nk),
        in_specs=[pl.BlockSpec((tk, D), lambda n, k: (k, 0)),
                  pl.BlockSpec((tk, D), lambda n, k: (jnp.where(n < 3, k, 0), jnp.minimum(n, 2))),
                  pl.BlockSpec((tk, D), lambda n, k: (jnp.where(n == 3, k, 0), 0)),
                  pl.BlockSpec((tk, D), lambda n, k: (jnp.where(n > 3, k, 0), jnp.clip(n - 4, 0, 2)))],
        out_specs=pl.BlockSpec((D, D), lambda n, k: (0, n)), out_shape=SDS((D, D_IN), BF16),
        scratch_shapes=[pltpu.VMEM((D, D), F32)],
        compiler_params=_cp("parallel", "arbitrary"),
    )(xb, dplo, dxr, dphi)


def mixb_drgw(xcb, dg):
    t = xcb.shape[0]
    return _mm_tn("mixb_drgw", xcb, pl.BlockSpec((t, HD), lambda g, h: (0, h)),
                  dg, pl.BlockSpec((None, t, HD), lambda g, h: (g, 0, h)),
                  SDS((2, HEADS, HD, HD), BF16), pl.BlockSpec((None, None, HD, HD), lambda g, h: (g, h, 0, 0)),
                  (2, HEADS))


def loss_head(y, tgt):
    t = y.shape[0]
    tm = _row_tile(t, 512)

    def body(y_ref, t_ref, dy_ref, l_ref):
        i = pl.program_id(0)
        e = y_ref[...] - t_ref[...]
        dy_ref[...] = e * (1.0 / D)

        @pl.when(i == 0)
        def _():
            l_ref[...] = jnp.zeros_like(l_ref)

        l_ref[...] += 0.5 * jnp.sum(jnp.mean(e * e, axis=-1, keepdims=True), axis=0, keepdims=True)

    row = pl.BlockSpec((tm, D), lambda i: (i, 0))
    return pl.pallas_call(
        body, name="loss_head", grid=(t // tm,), in_specs=[row, row],
        out_specs=(row, pl.BlockSpec((8, 128), lambda i: (0, 0))),
        out_shape=(SDS((t, D), F32), SDS((8, 128), F32)),
        compiler_params=_cp("arbitrary"),
    )(y, tgt)


def _adamw(w, g, m, v):
    m = ADAM_B1 * m + (1.0 - ADAM_B1) * g
    v = ADAM_B2 * v + (1.0 - ADAM_B2) * (g * g)
    m_hat = m / (1.0 - ADAM_B1 ** ADAM_STEP)
    v_hat = v / (1.0 - ADAM_B2 ** ADAM_STEP)
    delta = -ADAM_LR * (m_hat / (jnp.sqrt(v_hat) + ADAM_EPS) + ADAM_WD * w)
    return delta, m, v


def adam_big(name, w, m, v, parts, rows, lanes, blk_off, tr, l0, prev=None):
    nr = rows // tr
    nl = len(parts)

    def body(w_ref, m_ref, v_ref, *rest):
        g_ref, d_ref, mo_ref, vo_ref = rest[-4:]
        l = pl.program_id(0)
        for ll in range(nl):
            pr = rest[ll]

            @pl.when(l == ll)
            def _():
                g = pr[0].astype(F32)
                for s in range(1, N_DEV):
                    g = g + pr[s].astype(F32)
                g_ref[...] = g
                d, mn, vn = _adamw(w_ref[...], g, m_ref[...], v_ref[...])
                d_ref[...] = d
                mo_ref[...] = mn
                vo_ref[...] = vn

    blk = pl.BlockSpec((None, tr, lanes), lambda l, r: (l + l0, r, 0))

    def part_spec(ll):
        return pl.BlockSpec((N_DEV, tr, lanes), lambda l, r: (0, jnp.where(l == ll, r, 0) + blk_off, 0))

    out = SDS(w.shape, F32)
    extra = [] if prev is None else list(prev)
    return pl.pallas_call(
        body, name=name, grid=(nl, nr),
        in_specs=[blk, blk, blk] + [part_spec(ll) for ll in range(nl)] + [ANY] * len(extra),
        out_specs=(blk,) * 4, out_shape=(out,) * 4,
        input_output_aliases={3 + nl + i: i for i in range(len(extra))},
        compiler_params=_cp("parallel", "parallel"),
    )(w, m, v, *parts, *extra)


def small_update(me1, land_sh, land_rep, sharded, replicated):
    ns, nr = len(sharded), len(replicated)

    def body(me_ref, lsh, lrep, *refs):
        ins, outs = refs[:3 * (ns + nr)], refs[3 * (ns + nr):]

        def total(read):
            g = read(0)
            for s in range(1, N_DEV):
                g = g + read(s)
            return g

        def update(k, g, sl):
            w_ref, m_ref, v_ref = ins[3 * k:3 * k + 3]
            d, mn, vn = _adamw(w_ref[sl], g, m_ref[sl], v_ref[sl])
            for o, val in zip(outs[4 * k:4 * k + 4], (g, d, mn, vn)):
                o[sl] = val

        for k, (r0, n) in enumerate(GS_ROWS):
            update(k, total(lambda s: lsh[s, :, r0:r0 + n, :]), (slice(None),) * 3)
        lo, hi = (slice(None), slice(0, D)), (slice(None), slice(D, 2 * D))
        update(ns, total(lambda s: lrep[s, 0:4, :]), lo)
        update(ns, total(lambda s: lrep[s, 4:8, :]), hi)
        update(ns + 1, total(lambda s: lrep[s, 8:12, :]), (slice(None),) * 2)
        update(ns + 2, total(lambda s: lrep[s, 12:16, :]), (slice(None),) * 2)

    def whole(a):
        nd = a.ndim
        return pl.BlockSpec(a.shape, lambda i, me: (0,) * nd)

    params = [a for wmv in list(sharded) + list(replicated) for a in wmv]
    out_shape = tuple(SDS(wmv[0].shape, F32) for wmv in list(sharded) + list(replicated) for _ in range(4))
    res = pl.pallas_call(
        body, name="small_update", out_shape=out_shape,
        grid_spec=pltpu.PrefetchScalarGridSpec(
            num_scalar_prefetch=1, grid=(1,),
            in_specs=[pl.BlockSpec((N_DEV, DEPTH, 16, 128), lambda i, me: (0, 0, 0, me[0])), whole(land_rep)]
            + [whole(a) for a in params],
            out_specs=tuple(pl.BlockSpec(s.shape, lambda i, me, nd=len(s.shape): (0,) * nd) for s in out_shape)),
        compiler_params=_cp("arbitrary"),
    )(me1, land_sh, land_rep, *params)
    return [list(res[4 * k:4 * k + 4]) for k in range(ns + nr)]


def _ffn_forward(xf, xb, gw, f, s):
    g, u, a = ffn_up(xb, gw["w1"], f)
    xo, xob, xh, rs = ffn_down_ln(a, gw["w2"], f, xf, gw["small"], s)
    return (xo, xob), dict(xb=xb, g=g, u=u, a=a, xh=xh, rs=rs)


def _ffn_backward(dxn, sv, gw, f, s):
    dz, df, dgu, dln = ffn_bwd_gates(dxn, sv["xh"], sv["rs"], gw["small"], s, gw["w2"], f, sv["g"], sv["u"])
    dx = ffn_bwd_dx(dz, dgu, gw["w1"], f)
    dw1 = ffn_dw1(sv["xb"], dgu, f)
    dw2 = ffn_dw2(sv["a"], df, f)
    return dx, dw1, dw2, dln


def _mixer_forward(xf, xb, gw, rcb, ap, bm):
    p = mix_proj(xb, gw["win"])
    ca, pa, xc, xcb, gi, gr, h = mix_pre(p, gw["small"], rcb, ap, gw["rgw"])
    pb, ya, yb, m, xo, xob, xh, rs = mix_out(pa, h, p, bm, gw["w3"], xf, gw["small"])
    sv = dict(xb=xb, p=p, ca=ca, pa=pa, xc=xc, xcb=xcb, gi=gi, gr=gr, h=h, pb=pb, ya=ya, yb=yb, m=m, xh=xh, rs=rs)
    return (xo, xob), sv


def _mixer_backward(dxn, sv, gw, rcb, ap, bm):
    dz, dzb, dya, dyb, dbg, dca, dh, dphi, dln, dbm = mixb_head(
        dxn, sv["xh"], sv["rs"], gw["small"], gw["w3"], sv["p"], bm, sv["ya"], sv["yb"], sv["ca"], sv["h"])
    dg, dxc, red_rec = mixb_rec(dh, sv["gr"], sv["gi"], sv["h"], sv["xc"], ap, gw["rgw"])
    dplo, dxr, red_conv = mixb_conv(dca, dxc, dbg, sv["p"], gw["small"])
    dx = mixb_dx(dz, dplo, dxr, dphi, gw["win"])
    dwin = mixb_dwin(sv["xb"], dplo, dxr, dphi)
    dwo = mm_tn_square("mixb_dwo", sv["m"], dzb)
    dwoc = mm_tn_square("mixb_dwoc", sv["pa"], dya)
    dwor = mm_tn_square("mixb_dwor", sv["pb"], dyb)
    drgw = mixb_drgw(sv["xcb"], dg)
    return dx, dict(dwin=dwin, dwoc=dwoc, dwor=dwor, dwo=dwo, drgw=drgw), dln, dbm, red_rec, red_conv


def kernel(x, w_in, b_merge, sc_w, rc_w, rc_b, rg_w, rg_b, a_param, w_out_conv, w_out_rnn, w_o, ffn_w1, ffn_w2, ln_g, ln_b, loss_target, m_w_in, m_b_merge, m_sc_w, m_rc_w, m_rc_b, m_rg_w, m_rg_b, m_a_param, m_w_out_conv, m_w_out_rnn, m_w_o, m_ffn_w1, m_ffn_w2, m_ln_g, m_ln_b, v_w_in, v_b_merge, v_sc_w, v_rc_w, v_rc_b, v_rg_w, v_rg_b, v_a_param, v_w_out_conv, v_w_out_rnn, v_w_o, v_ffn_w1, v_ffn_w2, v_ln_g, v_ln_b):
    t = x.shape[1]
    me = _me()

    def rows(parts, total):
        out, off = None, 0
        for part in parts:
            r = part.shape[-2]
            pad = [(0, 0)] * (part.ndim - 2) + [(off, total - off - r), (0, 0)]
            padded = jnp.pad(part, pad)
            out = padded if out is None else out + padded
            off += r
        return out

    small = rows([sc_w, rc_w, rg_b, ln_g, ln_b], SMALL_ROWS)

    me1 = jnp.reshape(me, (1,)).astype(jnp.int32)

    def layer_shards(l):
        return gather_prep(l, me1, w_in, w_out_conv, w_out_rnn, w_o, ffn_w1, ffn_w2, rg_w, small)

    def as_weights(lands):
        return dict(zip(("win", "w3", "w1", "w2", "rgw", "small"), lands))

    xf = x.reshape(t, D)
    cur = (xf, xf.astype(BF16))
    gathers = [ChipGather(f"gather{l}") for l in range(DEPTH)]
    shards, own = layer_shards(0)
    sems, shards, own, cur = gathers[0].first(shards, own, cur)
    relay_sems, shards, own, cur = gathers[0].relay(sems, shards, own, cur)
    lands, cur = gathers[0].last(sems, relay_sems, shards, own, cur)
    gws, saved = [as_weights(lands)], []
    for l in range(DEPTH):
        gw = gws[l]
        nxt = l + 1 < DEPTH
        if nxt:
            shards, own = layer_shards(l + 1)
            sems, shards, own, cur = gathers[l + 1].first(shards, own, cur)
        rcb, ap, bm = rc_b[l][None], a_param[l][None], b_merge[l][None]
        cur, s0 = _ffn_forward(cur[0], cur[1], gw, 0, 0)
        cur, s1 = _mixer_forward(cur[0], cur[1], gw, rcb, ap, bm)
        if nxt:
            relay_sems, shards, own, cur = gathers[l + 1].relay(sems, shards, own, cur)
        cur, s2 = _ffn_forward(cur[0], cur[1], gw, 1, 2)
        saved.append((s0, s1, s2))
        if nxt:
            lands, cur = gathers[l + 1].last(sems, relay_sems, shards, own, cur)
            gws.append(as_weights(lands))

    dy, loss_tile = loss_head(cur[0], loss_target.reshape(t, D))
    loss = lax.psum(loss_tile[0, 0], ("x", "y", "c"))

    lands = [dict() for _ in range(DEPTH)]
    gsmall, grep = [None] * DEPTH, [None] * DEPTH
    flights = {}

    def launch(key, kinds, grads, dy):
        ex, prep = grad_scatter(f"scatter{key}", kinds)
        fl = ex.start(grads, prep(me1, grads), [dy])
        flights[key] = (ex, fl)
        return fl[4][0]

    def landed(key, thru):
        ex, fl = flights.pop(key)
        return ex.wait(fl[0], fl[1], fl[2], fl[3], thru)

    ffn_b, mixer, ffn_a = ("w1b", "w2b"), ("w_in", "w3", "rgw"), ("w1a", "w2a")
    for l in reversed(range(DEPTH)):
        gw = gws[l]
        rcb, ap, bm = rc_b[l][None], a_param[l][None], b_merge[l][None]
        s0, s1, s2 = saved[l]
        dy, dw1b, dw2b, dln2 = _ffn_backward(dy, s2, gw, 1, 2)
        if l == 0:
            dy = launch("0b", ["ffn"], [dw1b, dw2b], dy)
        dy, dmix, dln1, dbm, red_rec, red_conv = _mixer_backward(dy, s1, gw, rcb, ap, bm)
        mixer_grads = [dmix["dwin"], dmix["dwoc"], dmix["dwor"], dmix["dwo"], dmix["drgw"]]
        if l == 0:
            dy = launch("0m", ["mixer"], mixer_grads, dy)
        dy, dw1a, dw2a, dln0 = _ffn_backward(dy, s0, gw, 0, 0)
        if l == 0:
            dy = launch("0a", ["ffn"], [dw1a, dw2a], dy)
        else:
            if l + 1 < DEPTH:
                got, (dy,) = landed(str(l + 1), [dy])
                lands[l + 1] = dict(zip(ffn_b + mixer + ffn_a, got))
            dy = launch(str(l), ["ffn", "mixer", "ffn"], [dw1b, dw2b] + mixer_grads + [dw1a, dw2a], dy)
        gsmall[l] = rows([red_conv[0:8], red_rec[0:2], dln0[0:1], dln1[0:1], dln2[0:1], dln0[1:2], dln1[1:2],
                          dln2[1:2]], 16)
        grep[l] = [dbm[0:1], dbm[1:2], red_rec[3:4], red_rec[2:3]]
    grad_x = dy.reshape(1, t, D)

    g_rep = rows([grep[l][kind] for kind in range(4) for l in range(DEPTH)], 16)
    land_sh, land_rep = gather_small_grads(jnp.stack(gsmall, axis=0), g_rep)
    small_names = ("sc_w", "rc_w", "rg_b", "ln_g", "ln_b", "b_merge", "rc_b", "a_param")
    small_res = small_update(
        me1, land_sh, land_rep,
        [(sc_w, m_sc_w, v_sc_w), (rc_w, m_rc_w, v_rc_w), (rg_b, m_rg_b, v_rg_b), (ln_g, m_ln_g, v_ln_g),
         (ln_b, m_ln_b, v_ln_b)],
        [(b_merge, m_b_merge, v_b_merge), (rc_b, m_rc_b, v_rc_b), (a_param, m_a_param, v_a_param)])

    def parts_of(layers):
        ls = [lands[l] for l in layers]
        return dict(w_in=[a["w_in"] for a in ls],
                    w3=[a["w3"].reshape(N_DEV, 3 * 128, D) for a in ls],
                    rgw=[a["rgw"].reshape(N_DEV, 2 * HEADS * 32, HD) for a in ls],
                    w1=[a[k] for a in ls for k in ("w1a", "w1b")],
                    w2=[a[k] for a in ls for k in ("w2a", "w2b")])

    families = [("w_in", w_in, m_w_in, v_w_in, "w_in", 1, D, WIN_S, 0, 128),
                ("w_out_conv", w_out_conv, m_w_out_conv, v_w_out_conv, "w3", 1, 128, D, 0, 128),
                ("w_out_rnn", w_out_rnn, m_w_out_rnn, v_w_out_rnn, "w3", 1, 128, D, 1, 128),
                ("w_o", w_o, m_w_o, v_w_o, "w3", 1, 128, D, 2, 128),
                ("ffn_w1", ffn_w1, m_ffn_w1, v_ffn_w1, "w1", 2, D, FS, 0, 256),
                ("ffn_w2", ffn_w2, m_ffn_w2, v_ffn_w2, "w2", 2, W2S, D, 0, W2S // 2),
                ("rg_w", rg_w, m_rg_w, v_rg_w, "rgw", 1, 2 * HEADS * 32, HD, 0, 256)]

    def adam_pass(tag, layers, prev):
        parts = parts_of(layers)
        outs = {}
        for name, w, m, v, fam, per, nrow, lanes, blk_off, tr in families:
            r3 = lambda a: a.reshape(DEPTH * per, nrow, lanes)
            outs[name] = adam_big(f"adam_{name}_{tag}", r3(w), r3(m), r3(v), parts[fam], nrow, lanes, blk_off, tr,
                                  per * layers[0], None if prev is None else prev[name])
        return outs

    def through(outs, key, extra):
        thru = [outs[f[0]][0] for f in families] + list(extra)
        got, thru = landed(key, thru)
        outs = {f[0]: [thru[i]] + list(outs[f[0]][1:]) for i, f in enumerate(families)}
        return got, outs, thru[len(families):]

    done = adam_pass("top", list(range(2, DEPTH)), None)
    got, done, _ = through(done, "1", [])
    lands[1] = dict(zip(ffn_b + mixer + ffn_a, got))
    done = adam_pass("second", [1], done)
    for key, names in (("0b", ffn_b), ("0m", mixer), ("0a", ffn_a)):
        got, done, _ = through(done, key, [])
        lands[0].update(zip(names, got))
    done = adam_pass("first", [0], done)
    res = {f[0]: [o.reshape(f[1].shape) for o in done[f[0]]] for f in families}

    res.update(zip(small_names, small_res))

    names = ["w_in", "b_merge", "sc_w", "rc_w", "rc_b", "rg_w", "rg_b", "a_param", "w_out_conv", "w_out_rnn", "w_o",
             "ffn_w1", "ffn_w2", "ln_g", "ln_b"]
    out = [loss, grad_x]
    for k in range(4):
        out += [res[n][k] for n in names]
    return tuple(out)
```

```python
import functools

import jax
import jax.numpy as jnp
from jax import lax
from jax.experimental import pallas as pl
from jax.experimental.pallas import tpu as pltpu

F32 = jnp.float32
BF16 = jnp.bfloat16
SDS = jax.ShapeDtypeStruct

N_DEV = 8
DEPTH = 4
D = 1024
D_FF = 2816
FS = D_FF // 4
W2S = D_FF // 8
D_IN = 7 * D
WIN_S = D_IN // 8
HEADS = 4
HD = D // HEADS
LRU_C = 8.0
ALPHA = (2.0 * DEPTH) ** 0.25
LN_EPS = 1e-5
ADAM_LR, ADAM_B1, ADAM_B2, ADAM_EPS, ADAM_WD, ADAM_STEP = 0.001, 0.9, 0.999, 1e-08, 0.01, 10

R_SC, R_RC, R_RGB, R_LNG, R_LNB = 0, 3, 7, 9, 12
SMALL_ROWS = 16
GS_ROWS = ((0, 3), (3, 4), (8, 2), (10, 3), (13, 3))

NN = ((1,), (0,))
NT = ((1,), (1,))
TN = ((0,), (0,))
MESH = pl.DeviceIdType.MESH
ANY = pl.BlockSpec(memory_space=pl.ANY)
VMEM_LIMIT = 52 * 1024 * 1024


def _dot(a, b, dims):
    return lax.dot_general(a, b, (dims, ((), ())), preferred_element_type=F32)


def _cp(*sem):
    return pltpu.CompilerParams(dimension_semantics=sem, vmem_limit_bytes=VMEM_LIMIT)


def _sigmoid(x):
    return 1.0 / (1.0 + jnp.exp(-x))


def _sigmoid_t(x):
    return 0.5 * jnp.tanh(0.5 * x) + 0.5


def _log1p(e):
    u = 1.0 + e
    return jnp.where(u == 1.0, e, jnp.log(u) * e / jnp.where(u == 1.0, 1.0, u - 1.0))


def _softplus(x):
    return jnp.maximum(x, 0.0) + _log1p(jnp.exp(-jnp.abs(x)))


def _neg_expm1(x):
    u = jnp.exp(x)
    um1 = u - 1.0
    safe = jnp.logical_and(u != 1.0, um1 != -1.0)
    r = um1 * x / jnp.where(safe, jnp.log(jnp.where(safe, u, 0.5)), 1.0)
    return -jnp.where(u == 1.0, x, jnp.where(um1 == -1.0, -1.0, r))


def _gelu(y):
    c = 0.7978845608028654
    t = jnp.tanh(c * (y + 0.044715 * y * y * y))
    return 0.5 * y * (1.0 + t), t


def _gelu_grad(y, t):
    c = 0.7978845608028654
    return 0.5 * (1.0 + t) + 0.5 * y * (1.0 - t * t) * c * (1.0 + 3.0 * 0.044715 * y * y)


def _ln_fwd(z, g, b):
    mu = jnp.mean(z, axis=-1, keepdims=True)
    zc = z - mu
    var = jnp.mean(zc * zc, axis=-1, keepdims=True)
    rstd = lax.rsqrt(var + LN_EPS)
    xh = zc * rstd
    return xh * g + b, xh, rstd


def _ln_bwd(dy, xh, rstd, g):
    dxh = dy * g
    m1 = jnp.mean(dxh, axis=-1, keepdims=True)
    m2 = jnp.mean(dxh * xh, axis=-1, keepdims=True)
    return rstd * (dxh - m1 - xh * m2)


def _row_tile(t, want):
    return min(want, t)


def _me():
    return 4 * lax.axis_index("x") + 2 * lax.axis_index("y") + lax.axis_index("c")


def _coords(p):
    return (p // 4, (p // 2) % 2, p % 2)


def _all_to_all_copies(srcs_of, dsts_of, waits, send_sems, recv_sems, loc_sems):
    me = _me()
    n = len(waits)
    own_src, own_dst = srcs_of(me), dsts_of(me)
    local = [pltpu.make_async_copy(own_src[k], own_dst[k], loc_sems.at[k]) for k in range(n)]
    for cp in local:
        cp.start()
    for d in range(1, N_DEV):
        p = (me + d) % N_DEV
        src, dst = srcs_of(p), dsts_of(me)
        for k in range(n):
            pltpu.make_async_remote_copy(
                src_ref=src[k], dst_ref=dst[k], send_sem=send_sems.at[waits[k][1]],
                recv_sem=recv_sems.at[waits[k][1]], device_id=_coords(p), device_id_type=MESH).start()
    done = set()
    for k in range(n):
        ref, s = waits[k]
        if s in done:
            continue
        done.add(s)
        pltpu.make_async_remote_copy(
            src_ref=ref, dst_ref=ref, send_sem=send_sems.at[s], recv_sem=recv_sems.at[s],
            device_id=_coords(me), device_id_type=MESH).wait()
    for cp in local:
        cp.wait()


HBM = pl.BlockSpec(memory_space=pltpu.HBM)
SEM = pl.BlockSpec(memory_space=pltpu.SEMAPHORE)
EFFECT = pltpu.SideEffectType.DATAFLOW_SIDE_EFFECTING


def _in_hbm(a):
    return pltpu.with_memory_space_constraint(a, pltpu.HBM)


class Exchange:
    def __init__(self, name, src_of, dst_of, sem_of, span_of):
        self.name, self.src_of, self.dst_of, self.sem_of, self.span_of = name, src_of, dst_of, sem_of, span_of
        self.nsem = max(sem_of) + 1

    def start(self, srcs, lands, thru):
        n, m = len(srcs), len(lands)
        ops = list(srcs) + list(lands) + list(thru)

        def body(*refs):
            src_refs, land_refs = refs[:n], refs[n:n + m]
            send_sems, recv_sems = refs[len(ops)], refs[len(ops) + 1]
            me = _me()
            for dd in range(1, N_DEV):
                p = (me + dd) % N_DEV
                s, d = self.src_of(src_refs, p), self.dst_of(land_refs, me)
                for k in range(len(self.sem_of)):
                    pltpu.make_async_remote_copy(
                        src_ref=s[k], dst_ref=d[k], send_sem=send_sems.at[self.sem_of[k]],
                        recv_sem=recv_sems.at[self.sem_of[k]], device_id=_coords(p), device_id_type=MESH).start()

        sem = pltpu.SemaphoreType.DMA((self.nsem,))
        res = pl.pallas_call(
            body, name=self.name + "_start",
            out_shape=(sem, sem) + tuple(pltpu.HBM(a.shape, a.dtype) for a in ops),
            in_specs=[HBM] * len(ops), out_specs=(SEM, SEM) + (HBM,) * len(ops),
            input_output_aliases={i: 2 + i for i in range(len(ops))},
            compiler_params=pltpu.CompilerParams(has_side_effects=EFFECT),
        )(*[_in_hbm(a) for a in ops])
        return res[0], res[1], res[2:2 + n], res[2 + n:2 + n + m], list(res[2 + n + m:])

    def wait(self, send_sems, recv_sems, srcs, lands, thru):
        n, m = len(srcs), len(lands)
        ops = list(srcs) + list(lands) + list(thru)

        def body(*refs):
            land_refs = refs[n:n + m]
            ssem, rsem = refs[len(ops)], refs[len(ops) + 1]
            me = _me()
            spans = self.span_of(land_refs)
            for s in range(self.nsem):
                cp = pltpu.make_async_remote_copy(
                    src_ref=spans[s], dst_ref=spans[s], send_sem=ssem.at[s], recv_sem=rsem.at[s],
                    device_id=_coords(me), device_id_type=MESH)
                cp.wait_send()
                cp.wait_recv()

        res = pl.pallas_call(
            body, name=self.name + "_wait",
            out_shape=tuple(pltpu.HBM(a.shape, a.dtype) for a in ops),
            in_specs=[HBM] * len(ops) + [SEM, SEM], out_specs=(HBM,) * len(ops),
            input_output_aliases={i: i for i in range(len(ops))},
            compiler_params=pltpu.CompilerParams(has_side_effects=EFFECT),
        )(*ops, send_sems, recv_sems)
        return res[n:n + m], list(res[n + m:])


class ChipGather:
    def __init__(self, name, nsrc=6, nland=6, ncopy=6, views=None, slots=None):
        self.name, self.nsrc, self.nland, self.ncopy = name, nsrc, nland, ncopy
        self.views = views if views is not None else list
        self.slots = slots if slots is not None else self.layer_slots

    @staticmethod
    def layer_slots(lands, p):
        o_win, o_w3, o_w1, o_w2, o_rgw, o_sm = lands
        return [o_win.at[:, pl.ds(pl.multiple_of(p * WIN_S, 128), WIN_S)],
                o_w3.at[:, pl.ds(pl.multiple_of(p * 128, 128), 128), :],
                o_w1.at[:, p],
                o_w2.at[:, pl.ds(pl.multiple_of(p * W2S, 16), W2S), :],
                o_rgw.at[:, :, pl.ds(pl.multiple_of(p * 32, 32), 32), :],
                o_sm.at[:, pl.ds(pl.multiple_of(p * 128, 128), 128)]]

    @staticmethod
    def _places():
        x, y, c = lax.axis_index("x"), lax.axis_index("y"), lax.axis_index("c")
        chips = [(1 - x, y), (x, 1 - y), (1 - x, 1 - y)]
        return (x, y, c), (x, y, 1 - c), chips

    @staticmethod
    def _index(place):
        return 4 * place[0] + 2 * place[1] + place[2]

    def _call(self, tag, body, ops, sems_in, sems_out):
        nops = len(ops)
        sem = pltpu.SemaphoreType.DMA((max(sems_out, 1),))
        res = pl.pallas_call(
            body, name=f"{self.name}_{tag}",
            out_shape=((sem, sem) if sems_out else ()) + tuple(pltpu.HBM(a.shape, a.dtype) for a in ops),
            in_specs=[HBM] * nops + [SEM] * len(sems_in),
            out_specs=((SEM, SEM) if sems_out else ()) + (HBM,) * nops,
            input_output_aliases={i: (2 if sems_out else 0) + i for i in range(nops)},
            compiler_params=pltpu.CompilerParams(has_side_effects=EFFECT),
        )(*[_in_hbm(a) for a in ops], *sems_in)
        return res

    def first(self, srcs, lands, thru):
        ns, nl, n = self.nsrc, self.nland, self.ncopy
        ops = list(srcs) + list(lands) + list(thru)

        def body(*refs):
            src_refs, land_refs = self.views(refs[:ns]), refs[ns:ns + nl]
            send_sems, recv_sems = refs[len(ops)], refs[len(ops) + 1]
            me, sibling, chips = self._places()
            dst = self.slots(land_refs, self._index(me))
            targets = [sibling] + [(cx, cy, me[2]) for cx, cy in chips]
            for r, to in enumerate(targets):
                for k in range(n):
                    pltpu.make_async_remote_copy(
                        src_ref=src_refs[k], dst_ref=dst[k], send_sem=send_sems.at[n *r + k],
                        recv_sem=recv_sems.at[n *r + k], device_id=to, device_id_type=MESH).start()

        res = self._call("first", body, ops, [], 4 * n)
        return (res[0], res[1]), list(res[2:2 + ns]), list(res[2 + ns:2 + ns + nl]), list(res[2 + ns + nl:])

    def relay(self, sems, srcs, lands, thru):
        ns, nl, n = self.nsrc, self.nland, self.ncopy
        ops = list(srcs) + list(lands) + list(thru)

        def wait_body(*refs):
            land_refs = refs[ns:ns + nl]
            ssem, rsem = refs[len(ops)], refs[len(ops) + 1]
            me, sibling, chips = self._places()
            for j, (cx, cy) in enumerate(chips):
                got = self.slots(land_refs, self._index((cx, cy, me[2])))
                for k in range(n):
                    pltpu.make_async_remote_copy(
                        src_ref=got[k], dst_ref=got[k], send_sem=ssem.at[n *(1 + j) + k], recv_sem=rsem.at[n *(1 + j) + k],
                        device_id=me, device_id_type=MESH).wait_recv()

        ops = list(self._call("landed", wait_body, ops, list(sems), 0))

        def start_body(*refs):
            land_refs = refs[ns:ns + nl]
            ssem, rsem = refs[len(ops)], refs[len(ops) + 1]
            me, sibling, chips = self._places()
            for j, (cx, cy) in enumerate(chips):
                got = self.slots(land_refs, self._index((cx, cy, me[2])))
                for k in range(n):
                    pltpu.make_async_remote_copy(
                        src_ref=got[k], dst_ref=got[k], send_sem=ssem.at[n *j + k], recv_sem=rsem.at[n *j + k],
                        device_id=sibling, device_id_type=MESH).start()

        res = self._call("relay", start_body, ops, [], 3 * n)
        return (res[0], res[1]), list(res[2:2 + ns]), list(res[2 + ns:2 + ns + nl]), list(res[2 + ns + nl:])

    def last(self, sems, relay_sems, srcs, lands, thru):
        ns, nl, n = self.nsrc, self.nland, self.ncopy
        ops = list(srcs) + list(lands) + list(thru)

        def body(*refs):
            src_refs, land_refs = self.views(refs[:ns]), refs[ns:ns + nl]
            ssem, rsem, ssem2, rsem2 = refs[len(ops):len(ops) + 4]
            me, sibling, chips = self._places()
            got = self.slots(land_refs, self._index(sibling))
            for k in range(n):
                pltpu.make_async_remote_copy(
                    src_ref=got[k], dst_ref=got[k], send_sem=ssem.at[k], recv_sem=rsem.at[k],
                    device_id=me, device_id_type=MESH).wait_recv()
            for j, (cx, cy) in enumerate(chips):
                got = self.slots(land_refs, self._index((cx, cy, 1 - me[2])))
                for k in range(n):
                    pltpu.make_async_remote_copy(
                        src_ref=got[k], dst_ref=got[k], send_sem=ssem2.at[n *j + k], recv_sem=rsem2.at[n *j + k],
                        device_id=me, device_id_type=MESH).wait_recv()
            for sem_s, sem_r, count in ((ssem, rsem, 4), (ssem2, rsem2, 3)):
                for r in range(count):
                    for k in range(n):
                        pltpu.make_async_remote_copy(
                            src_ref=src_refs[k], dst_ref=src_refs[k], send_sem=sem_s.at[n *r + k],
                            recv_sem=sem_r.at[n *r + k], device_id=me, device_id_type=MESH).wait_send()

        res = self._call("last", body, ops, list(sems) + list(relay_sems), 0)
        return list(res[:ns]), list(res[ns:ns + nl]), list(res[ns + nl:])


GATHER_SHAPES = (SDS((D, D_IN), BF16), SDS((3, D, D), BF16), SDS((2, N_DEV, D, FS), BF16),
                 SDS((2, D_FF, D), BF16), SDS((2, HEADS, HD, HD), BF16), SDS((SMALL_ROWS, D), F32))


def _slab(p, n, align):
    return pl.ds(pl.multiple_of(p * n, align), n)


class GradGroup:
    def __init__(self, nsrc, lands, src_of, dst_of, sem_of, in_specs, out_specs, copy):
        self.nsrc, self.lands, self.src_of, self.dst_of, self.sem_of = nsrc, lands, src_of, dst_of, sem_of
        self.in_specs, self.out_specs, self.copy = in_specs, out_specs, copy


def _mixer_group():
    hd = D // 2

    def src_of(refs, p):
        dwin, dwoc, dwor, dwo, drgw = refs
        r3 = _slab(p, 128, 128)
        return [dwin.at[:, _slab(p, WIN_S, 128)], dwoc.at[r3, :], dwor.at[r3, :], dwo.at[r3, :],
                drgw.at[:, :, _slab(p, 32, 32), :]]

    def dst_of(lands, p):
        l_win, l_w3, l_rgw = lands
        return [l_win.at[p], l_w3.at[p, 0], l_w3.at[p, 1], l_w3.at[p, 2], l_rgw.at[p]]

    def copy(srcs, lands):
        lands[0][...] = srcs[0][...]
        for k in range(3):
            lands[1][k] = srcs[1 + k][...]
        lands[2][...] = srcs[4][...]

    three = pl.BlockSpec((64, D), lambda i, me: (2 * me[0] + i, 0))
    return GradGroup(
        5, (SDS((N_DEV, D, WIN_S), BF16), SDS((N_DEV, 3, 128, D), BF16), SDS((N_DEV, 2, HEADS, 32, HD), BF16)),
        src_of, dst_of, [0, 1, 1, 1, 2],
        [pl.BlockSpec((hd, WIN_S), lambda i, me: (i, me[0])), three, three, three,
         pl.BlockSpec((2, HEADS, 16, HD), lambda i, me: (0, 0, 2 * me[0] + i, 0))],
        [pl.BlockSpec((None, hd, WIN_S), lambda i, me: (me[0], i, 0)),
         pl.BlockSpec((None, 3, 64, D), lambda i, me: (me[0], 0, i, 0)),
         pl.BlockSpec((None, 2, HEADS, 16, HD), lambda i, me: (me[0], 0, 0, i, 0))],
        copy)


def _ffn_group():
    hd, hw = D // 2, W2S // 2

    def src_of(refs, p):
        return [refs[0].at[p], refs[1].at[_slab(p, W2S, 16), :]]

    def dst_of(lands, p):
        return [lands[0].at[p], lands[1].at[p]]

    def copy(srcs, lands):
        lands[0][...] = srcs[0][...]
        lands[1][...] = srcs[1][...]

    return GradGroup(
        2, (SDS((N_DEV, D, FS), BF16), SDS((N_DEV, W2S, D), BF16)), src_of, dst_of, [0, 1],
        [pl.BlockSpec((None, hd, FS), lambda i, me: (me[0], i, 0)), pl.BlockSpec((hw, D), lambda i, me: (2 * me[0] + i, 0))],
        [pl.BlockSpec((None, hd, FS), lambda i, me: (me[0], i, 0)), pl.BlockSpec((None, hw, D), lambda i, me: (me[0], i, 0))],
        copy)


def grad_scatter(name, kinds):
    groups = [_mixer_group() if k == "mixer" else _ffn_group() for k in kinds]

    def per_group(refs, counts, fn):
        out, i = [], 0
        for g, c in zip(groups, counts):
            out += fn(g, refs[i:i + c])
            i += c
        return out

    nsrcs = [g.nsrc for g in groups]
    nlands = [len(g.lands) for g in groups]
    sem_of, off = [], 0
    for g in groups:
        sem_of += [off + s for s in g.sem_of]
        off += len(g.lands)

    ex = Exchange(name,
                  lambda refs, p: per_group(refs, nsrcs, lambda g, r: g.src_of(r, p)),
                  lambda lands, p: per_group(lands, nlands, lambda g, r: g.dst_of(r, p)),
                  sem_of, lambda lands: [a.at[pl.ds(0, 7)] for a in lands])

    def prep(me1, grads):
        nsrc = sum(nsrcs)

        def body(me_ref, *refs):
            srcs, lands = refs[:nsrc], refs[nsrc:]
            i = j = 0
            for g in groups:
                g.copy(srcs[i:i + g.nsrc], lands[j:j + len(g.lands)])
                i += g.nsrc
                j += len(g.lands)

        return list(pl.pallas_call(
            body, name=name + "_prep", out_shape=tuple(s for g in groups for s in g.lands),
            grid_spec=pltpu.PrefetchScalarGridSpec(
                num_scalar_prefetch=1, grid=(2,), in_specs=[s for g in groups for s in g.in_specs],
                out_specs=tuple(s for g in groups for s in g.out_specs)),
            compiler_params=_cp("arbitrary"),
        )(me1, *grads))

    return ex, prep


def gather_prep(l, me1, w_in, w_out_conv, w_out_rnn, w_o, ffn_w1, ffn_w2, rg_w, small):
    hd, hw = D // 2, W2S // 2

    def body(me_ref, win, woc, wor, wo, w1, w2, rgw, sm, c_win, c_w3, c_w1, c_w2, c_rgw, c_sm,
             o_win, o_w3, o_w1, o_w2, o_rgw, o_sm):
        a = win[...].astype(BF16)
        c_win[...] = a
        o_win[...] = a
        for k, r in enumerate((woc, wor, wo)):
            b = r[...].astype(BF16)
            c_w3[k] = b
            o_w3[k] = b
        for src, comp, own in ((w1, c_w1, o_w1), (w2, c_w2, o_w2), (rgw, c_rgw, o_rgw)):
            b = src[...].astype(BF16)
            comp[...] = b
            own[...] = b
        c_sm[...] = sm[...]
        o_sm[...] = sm[...]

    three = pl.BlockSpec((None, 64, D), lambda i, me: (l, i, 0))
    in_specs = [pl.BlockSpec((None, hd, WIN_S), lambda i, me: (l, i, 0)), three, three, three,
                pl.BlockSpec((None, 2, hd, FS), lambda i, me: (l, 0, i, 0)),
                pl.BlockSpec((None, 2, hw, D), lambda i, me: (l, 0, i, 0)),
                pl.BlockSpec((None, 2, HEADS, 16, HD), lambda i, me: (l, 0, 0, i, 0)),
                pl.BlockSpec((None, 8, 128), lambda i, me: (l, i, 0))]
    out_specs = (pl.BlockSpec((hd, WIN_S), lambda i, me: (i, 0)), pl.BlockSpec((3, 64, D), lambda i, me: (0, i, 0)),
                 pl.BlockSpec((2, hd, FS), lambda i, me: (0, i, 0)), pl.BlockSpec((2, hw, D), lambda i, me: (0, i, 0)),
                 pl.BlockSpec((2, HEADS, 16, HD), lambda i, me: (0, 0, i, 0)), pl.BlockSpec((8, 128), lambda i, me: (i, 0)),
                 pl.BlockSpec((hd, WIN_S), lambda i, me: (i, me[0])),
                 pl.BlockSpec((3, 64, D), lambda i, me: (0, 2 * me[0] + i, 0)),
                 pl.BlockSpec((2, None, hd, FS), lambda i, me: (0, me[0], i, 0)),
                 pl.BlockSpec((2, hw, D), lambda i, me: (0, 2 * me[0] + i, 0)),
                 pl.BlockSpec((2, HEADS, 16, HD), lambda i, me: (0, 0, 2 * me[0] + i, 0)),
                 pl.BlockSpec((8, 128), lambda i, me: (i, me[0])))
    compact = (SDS((D, WIN_S), BF16), SDS((3, 128, D), BF16), SDS((2, D, FS), BF16), SDS((2, W2S, D), BF16),
               SDS((2, HEADS, 32, HD), BF16), SDS((SMALL_ROWS, 128), F32))
    res = pl.pallas_call(
        body, name=f"gather_prep{l}", out_shape=compact + GATHER_SHAPES,
        grid_spec=pltpu.PrefetchScalarGridSpec(num_scalar_prefetch=1, grid=(2,), in_specs=in_specs, out_specs=out_specs),
        compiler_params=_cp("arbitrary"),
    )(me1, w_in, w_out_conv, w_out_rnn, w_o, ffn_w1, ffn_w2, rg_w, small)
    return list(res[:6]), list(res[6:])


def gather_small_grads(g_sharded, g_replicated):
    def body(ga, gb, la, lb, send_sems, recv_sems, loc_sems):
        _all_to_all_copies(lambda p: [ga, gb], lambda p: [la.at[p], lb.at[p]],
                           [(la.at[pl.ds(0, 7)], 0), (lb.at[pl.ds(0, 7)], 1)], send_sems, recv_sems, loc_sems)

    return pl.pallas_call(
        body, name="gather_small_grads",
        out_shape=(SDS((N_DEV,) + g_sharded.shape, F32), SDS((N_DEV,) + g_replicated.shape, F32)),
        in_specs=[ANY, ANY], out_specs=(ANY, ANY),
        scratch_shapes=[pltpu.SemaphoreType.DMA((2,)), pltpu.SemaphoreType.DMA((2,)), pltpu.SemaphoreType.DMA((2,))],
        compiler_params=pltpu.CompilerParams(has_side_effects=True),
    )(g_sharded, g_replicated)


def ffn_up(xb, w1, f):
    t = xb.shape[0]
    tm = _row_tile(t, 1024)

    def body(x_ref, wg_ref, wu_ref, g_ref, u_ref, a_ref):
        x = x_ref[...]
        g = _dot(x, wg_ref[...], NN)
        u = _dot(x, wu_ref[...], NN)
        g_ref[...] = g.astype(BF16)
        u_ref[...] = u.astype(BF16)
        a_ref[...] = (g * _sigmoid_t(g) * u).astype(BF16)

    out = pl.BlockSpec((None, tm, FS), lambda j, i: (j, i, 0))
    return pl.pallas_call(
        body, name=f"ffn_up{f}", grid=(4, t // tm),
        in_specs=[pl.BlockSpec((tm, D), lambda j, i: (i, 0)),
                  pl.BlockSpec((None, None, D, FS), lambda j, i: (f, j, 0, 0)),
                  pl.BlockSpec((None, None, D, FS), lambda j, i: (f, j + 4, 0, 0))],
        out_specs=(out, out, out), out_shape=(SDS((4, t, FS), BF16),) * 3,
        compiler_params=_cp("parallel", "parallel"),
    )(xb, w1, w1)


def ffn_down_ln(a, w2, f, xf, small, s):
    t = xf.shape[0]
    tm = _row_tile(t, 512)

    def body(a_ref, w_ref, x_ref, sm_ref, xo_ref, xb_ref, xh_ref, rs_ref):
        acc = _dot(a_ref[0], w_ref[0:FS, :], NN)
        for j in range(1, 4):
            acc = acc + _dot(a_ref[j], w_ref[j * FS:(j + 1) * FS, :], NN)
        z = ALPHA * x_ref[...] + 0.5 * acc
        y, xh, rstd = _ln_fwd(z, sm_ref[R_LNG + s:R_LNG + s + 1, :], sm_ref[R_LNB + s:R_LNB + s + 1, :])
        xo_ref[...] = y
        xb_ref[...] = y.astype(BF16)
        xh_ref[...] = xh
        rs_ref[...] = rstd

    row = pl.BlockSpec((tm, D), lambda i: (i, 0))
    return pl.pallas_call(
        body, name=f"ffn_down_ln{f}", grid=(t // tm,),
        in_specs=[pl.BlockSpec((4, tm, FS), lambda i: (0, i, 0)),
                  pl.BlockSpec((None, D_FF, D), lambda i: (f, 0, 0)),
                  row, pl.BlockSpec((SMALL_ROWS, D), lambda i: (0, 0))],
        out_specs=(row, row, row, pl.BlockSpec((tm, 1), lambda i: (i, 0))),
        out_shape=(SDS((t, D), F32), SDS((t, D), BF16), SDS((t, D), F32), SDS((t, 1), F32)),
        compiler_params=_cp("parallel"),
    )(a, w2, xf, small)


def ffn_bwd_gates(dxn, xh, rstd, small, s, w2, f, g, u):
    t = dxn.shape[0]
    tm = _row_tile(t, 256)

    def body(dy_ref, xh_ref, rs_ref, sm_ref, w_ref, g_ref, u_ref, dz_ref, df_ref, dgu_ref, dln_ref):
        i = pl.program_id(0)
        dy = dy_ref[...]
        xhat = xh_ref[...]
        dz = _ln_bwd(dy, xhat, rs_ref[...], sm_ref[R_LNG + s:R_LNG + s + 1, :])

        @pl.when(i == 0)
        def _():
            dln_ref[...] = jnp.zeros_like(dln_ref)

        dln_ref[0:1, :] += jnp.sum(dy * xhat, axis=0, keepdims=True)
        dln_ref[1:2, :] += jnp.sum(dy, axis=0, keepdims=True)
        dz_ref[...] = dz
        df = (0.5 * dz).astype(BF16)
        df_ref[...] = df
        for j in range(4):
            da = _dot(df, w_ref[j * FS:(j + 1) * FS, :], NT)
            gg = g_ref[j].astype(F32)
            uu = u_ref[j].astype(F32)
            sg = _sigmoid_t(gg)
            dgu_ref[j] = (da * uu * (sg * (1.0 + gg * (1.0 - sg)))).astype(BF16)
            dgu_ref[j + 4] = (da * (gg * sg)).astype(BF16)

    row = pl.BlockSpec((tm, D), lambda i: (i, 0))
    gu = pl.BlockSpec((4, tm, FS), lambda i: (0, i, 0))
    return pl.pallas_call(
        body, name=f"ffn_bwd_gates{f}", grid=(t // tm,),
        in_specs=[row, row, pl.BlockSpec((tm, 1), lambda i: (i, 0)), pl.BlockSpec((SMALL_ROWS, D), lambda i: (0, 0)),
                  pl.BlockSpec((None, D_FF, D), lambda i: (f, 0, 0)), gu, gu],
        out_specs=(row, row, pl.BlockSpec((8, tm, FS), lambda i: (0, i, 0)), pl.BlockSpec((2, D), lambda i: (0, 0))),
        out_shape=(SDS((t, D), F32), SDS((t, D), BF16), SDS((8, t, FS), BF16), SDS((2, D), F32)),
        compiler_params=_cp("arbitrary"),
    )(dxn, xh, rstd, small, w2, g, u)


def ffn_bwd_dx(dz, dgu, w1, f):
    t = dz.shape[0]
    tm = _row_tile(t, 1024)

    def body(dz_ref, d_ref, w_ref, dx_ref, acc):
        k = pl.program_id(1)

        @pl.when(k == 0)
        def _():
            acc[...] = ALPHA * dz_ref[...]

        acc[...] += _dot(d_ref[...], w_ref[...], NT)

        @pl.when(k == 7)
        def _():
            dx_ref[...] = acc[...]

    row = pl.BlockSpec((tm, D), lambda i, k: (i, 0))
    return pl.pallas_call(
        body, name=f"ffn_bwd_dx{f}", grid=(t // tm, 8),
        in_specs=[row, pl.BlockSpec((None, tm, FS), lambda i, k: (k, i, 0)),
                  pl.BlockSpec((None, None, D, FS), lambda i, k: (f, k, 0, 0))],
        out_specs=row, out_shape=SDS((t, D), F32),
        scratch_shapes=[pltpu.VMEM((tm, D), F32)],
        compiler_params=_cp("parallel", "arbitrary"),
    )(dz, dgu, w1)


def _mm_tn(name, a, a_spec, b, b_spec, out_sds, out_spec, grid):
    def body(a_ref, b_ref, o_ref):
        o_ref[...] = _dot(a_ref[...], b_ref[...], TN).astype(o_ref.dtype)

    return pl.pallas_call(
        body, name=name, grid=grid, in_specs=[a_spec, b_spec], out_specs=out_spec, out_shape=out_sds,
        compiler_params=_cp(*(["parallel"] * len(grid))),
    )(a, b)


def ffn_dw1(xb, dgu, f):
    t = xb.shape[0]
    return _mm_tn(f"ffn_dw1_{f}", xb, pl.BlockSpec((t, D), lambda j: (0, 0)),
                  dgu, pl.BlockSpec((None, t, FS), lambda j: (j, 0, 0)),
                  SDS((8, D, FS), BF16), pl.BlockSpec((None, D, FS), lambda j: (j, 0, 0)), (8,))


def ffn_dw2(a, df, f):
    t = df.shape[0]
    return _mm_tn(f"ffn_dw2_{f}", a, pl.BlockSpec((None, t, FS), lambda j: (j, 0, 0)),
                  df, pl.BlockSpec((t, D), lambda j: (0, 0)),
                  SDS((D_FF, D), BF16), pl.BlockSpec((FS, D), lambda j: (j, 0)), (4,))


def mm_tn_square(name, a, b):
    t = a.shape[0]
    return _mm_tn(name, a, pl.BlockSpec((t, 512), lambda i: (0, i)),
                  b, pl.BlockSpec((t, D), lambda i: (0, 0)),
                  SDS((D, D), BF16), pl.BlockSpec((512, D), lambda i: (i, 0)), (2,))


def mix_proj(xb, win):
    t = xb.shape[0]
    tm = _row_tile(t, 1024)

    def body(x_ref, w_ref, o_ref):
        o_ref[...] = _dot(x_ref[...], w_ref[...], NN).astype(BF16)

    return pl.pallas_call(
        body, name="mix_proj", grid=(7, t // tm),
        in_specs=[pl.BlockSpec((tm, D), lambda n, i: (i, 0)), pl.BlockSpec((D, D), lambda n, i: (0, n))],
        out_specs=pl.BlockSpec((tm, D), lambda n, i: (i, n)), out_shape=SDS((t, D_IN), BF16),
        compiler_params=_cp("parallel", "parallel"),
    )(xb, win)


def _pcol(tm, k):
    return pl.BlockSpec((tm, D), lambda i: (i, k))


def _prev_halo(tm, k):
    return pl.BlockSpec((8, D), lambda i: (jnp.maximum(i * (tm // 8) - 1, 0), k))


def _prev_halo16(tm, k):
    return pl.BlockSpec((16, D), lambda i: (jnp.maximum(i * (tm // 16) - 1, 0), k))


def _next_halo(tm, t, k):
    return pl.BlockSpec((8, D), lambda i: (jnp.minimum((i + 1) * (tm // 8), t // 8 - 1), k))


def _full(shape):
    nd = len(shape)
    return pl.BlockSpec(shape, lambda i: (0,) * nd)


def mix_pre(p, small, rcb, ap, rgw):
    t = p.shape[0]
    tm = _row_tile(t, 256)

    def body(bg_ref, cg_ref, v_ref, xr_ref, cgh_ref, vh_ref, xrh_ref, sm_ref, rcb_ref, ap_ref, rgw_ref,
             ca_ref, pa_ref, xc_ref, xcb_ref, gi_ref, gr_ref, h_ref, ext1, ext2, a_s, b_s, carry):
        i = pl.program_id(0)
        first = i == 0
        cv = cg_ref[...].astype(F32) * v_ref[...].astype(F32)
        ext1[0:16, :] = jnp.where(first, 0.0, cgh_ref[...].astype(F32) * vh_ref[...].astype(F32))
        ext1[16:, :] = cv
        xr = xr_ref[...].astype(F32)
        ext2[0:16, :] = jnp.where(first, 0.0, xrh_ref[...].astype(F32))
        ext2[16:, :] = xr
        ca = (sm_ref[R_SC:R_SC + 1, :] * ext1[pl.ds(14, tm), :] + sm_ref[R_SC + 1:R_SC + 2, :] * ext1[pl.ds(15, tm), :]
              + sm_ref[R_SC + 2:R_SC + 3, :] * cv)
        ca_ref[...] = ca.astype(BF16)
        pa_ref[...] = (bg_ref[...].astype(F32) * ca).astype(BF16)
        xc = (sm_ref[R_RC:R_RC + 1, :] * ext2[pl.ds(13, tm), :] + sm_ref[R_RC + 1:R_RC + 2, :] * ext2[pl.ds(14, tm), :]
              + sm_ref[R_RC + 2:R_RC + 3, :] * ext2[pl.ds(15, tm), :] + sm_ref[R_RC + 3:R_RC + 4, :] * xr
              + rcb_ref[...])
        xc_ref[...] = xc
        xcb = xc.astype(BF16)
        xcb_ref[...] = xcb
        g0, g1 = [], []
        for h in range(HEADS):
            xh = xcb[:, h * HD:(h + 1) * HD]
            g0.append(_dot(xh, rgw_ref[0, h], NN))
            g1.append(_dot(xh, rgw_ref[1, h], NN))
        gi = _sigmoid(jnp.concatenate(g0, axis=1) + sm_ref[R_RGB:R_RGB + 1, :])
        gr = _sigmoid(jnp.concatenate(g1, axis=1) + sm_ref[R_RGB + 1:R_RGB + 2, :])
        gi_ref[...] = gi
        gr_ref[...] = gr
        la = (-LRU_C) * gr * _softplus(-ap_ref[...])
        a_s[...] = jnp.exp(la)
        row = lax.broadcasted_iota(jnp.int32, (tm, D), 0) + i * tm
        mult = jnp.where(row == 0, 1.0, jnp.sqrt(_neg_expm1(2.0 * la)))
        b_s[...] = xc * gi * mult

        @pl.when(first)
        def _():
            carry[...] = jnp.zeros_like(carry)

        carry[...] = _scan_tile(a_s, b_s, a_s, carry[...], tm, reverse=False)
        h_ref[...] = a_s[...].astype(BF16)

    row = pl.BlockSpec((tm, D), lambda i: (i, 0))
    f32o, b16o = SDS((t, D), F32), SDS((t, D), BF16)
    ext, tile = pltpu.VMEM((tm + 16, D), F32), pltpu.VMEM((tm, D), F32)
    return pl.pallas_call(
        body, name="mix_pre", grid=(t // tm,),
        in_specs=[_pcol(tm, 0), _pcol(tm, 1), _pcol(tm, 2), _pcol(tm, 3),
                  _prev_halo16(tm, 1), _prev_halo16(tm, 2), _prev_halo16(tm, 3),
                  _full((SMALL_ROWS, D)), _full((1, D)), _full((1, D)), _full((2, HEADS, HD, HD))],
        out_specs=(row,) * 7, out_shape=(b16o, b16o, f32o, b16o, f32o, f32o, b16o),
        scratch_shapes=[ext, ext, tile, tile, pltpu.VMEM((8, D), F32)],
        compiler_params=_cp("arbitrary"),
    )(p, p, p, p, p, p, p, small, rcb, ap, rgw)


def _scan_tile(a_ref, b_ref, o_ref, carry, tm, reverse):
    width = a_ref.shape[1]
    ng = tm // 8
    row8 = lax.broadcasted_iota(jnp.int32, (8, width), 0)

    def step(g, c):
        r = pl.multiple_of((ng - 1 - g if reverse else g) * 8, 8)
        aa = a_ref[pl.ds(r, 8), :]
        bb = b_ref[pl.ds(r, 8), :]
        for s in (1, 2, 4):
            if reverse:
                keep, shift = row8 < 8 - s, 8 - s
            else:
                keep, shift = row8 >= s, s
            a_sh = jnp.where(keep, pltpu.roll(aa, shift, 0), 1.0)
            b_sh = jnp.where(keep, pltpu.roll(bb, shift, 0), 0.0)
            bb = aa * b_sh + bb
            aa = aa * a_sh
        o = aa * c + bb
        o_ref[pl.ds(r, 8), :] = o
        edge = o[0:1, :] if reverse else o[7:8, :]
        return jnp.broadcast_to(edge, (8, width))

    return lax.fori_loop(0, ng, step, carry)


def mix_out(pa, h, p, bm, w3, xf, small):
    t = xf.shape[0]
    tm = _row_tile(t, 256)

    def body(pa_ref, h_ref, yr_ref, gla_ref, glb_ref, bma_ref, bmb_ref, w_ref, x_ref, sm_ref,
             pb_ref, ya_ref, yb_ref, m_ref, xo_ref, xb_ref, xh_ref, rs_ref):
        ge, _ = _gelu(yr_ref[...].astype(F32))
        pb = (h_ref[...].astype(F32) * ge).astype(BF16)
        pb_ref[...] = pb
        ya = _dot(pa_ref[...], w_ref[0], NN)
        yb = _dot(pb, w_ref[1], NN)
        ya_ref[...] = ya.astype(BF16)
        yb_ref[...] = yb.astype(BF16)
        ga = _sigmoid_t(gla_ref[...].astype(F32) + bma_ref[...])
        gb = _sigmoid_t(glb_ref[...].astype(F32) + bmb_ref[...])
        m = (ga * ya + gb * yb).astype(BF16)
        m_ref[...] = m
        z = ALPHA * x_ref[...] + _dot(m, w_ref[2], NN)
        y, xh, rstd = _ln_fwd(z, sm_ref[R_LNG + 1:R_LNG + 2, :], sm_ref[R_LNB + 1:R_LNB + 2, :])
        xo_ref[...] = y
        xb_ref[...] = y.astype(BF16)
        xh_ref[...] = xh
        rs_ref[...] = rstd

    row = pl.BlockSpec((tm, D), lambda i: (i, 0))
    f32o, b16o = SDS((t, D), F32), SDS((t, D), BF16)
    return pl.pallas_call(
        body, name="mix_out", grid=(t // tm,),
        in_specs=[row, row, _pcol(tm, 4), _pcol(tm, 5), _pcol(tm, 6),
                  pl.BlockSpec((1, D), lambda i: (0, 0)), pl.BlockSpec((1, D), lambda i: (0, 1)),
                  _full((3, D, D)), row, _full((SMALL_ROWS, D))],
        out_specs=(row,) * 7 + (pl.BlockSpec((tm, 1), lambda i: (i, 0)),),
        out_shape=(b16o, b16o, b16o, b16o, f32o, b16o, f32o, SDS((t, 1), F32)),
        compiler_params=_cp("parallel"),
    )(pa, h, p, p, p, bm, bm, w3, xf, small)


def mixb_head(dxn, xh, rstd, small, w3, p, bm, ya, yb, ca, h):
    t = dxn.shape[0]
    tm = _row_tile(t, 256)

    def body(dy_ref, xh_ref, rs_ref, sm_ref, w_ref, bg_ref, yr_ref, gla_ref, glb_ref, bma_ref, bmb_ref,
             ya_ref, yb_ref, ca_ref, h_ref,
             dz_ref, dzb_ref, dya_ref, dyb_ref, dbg_ref, dca_ref, dh_ref, dphi_ref, dln_ref, dbm_ref):
        i = pl.program_id(0)
        dy = dy_ref[...]
        xhat = xh_ref[...]
        dz = _ln_bwd(dy, xhat, rs_ref[...], sm_ref[R_LNG + 1:R_LNG + 2, :])

        @pl.when(i == 0)
        def _():
            dln_ref[...] = jnp.zeros_like(dln_ref)
            dbm_ref[...] = jnp.zeros_like(dbm_ref)

        dln_ref[0:1, :] += jnp.sum(dy * xhat, axis=0, keepdims=True)
        dln_ref[1:2, :] += jnp.sum(dy, axis=0, keepdims=True)
        dz_ref[...] = dz
        dzb = dz.astype(BF16)
        dzb_ref[...] = dzb
        dm = _dot(dzb, w_ref[2], NT)
        ga = _sigmoid_t(gla_ref[...].astype(F32) + bma_ref[...])
        gb = _sigmoid_t(glb_ref[...].astype(F32) + bmb_ref[...])
        dya = (dm * ga).astype(BF16)
        dyb = (dm * gb).astype(BF16)
        dya_ref[...] = dya
        dyb_ref[...] = dyb
        dgla = dm * ya_ref[...].astype(F32) * ga * (1.0 - ga)
        dglb = dm * yb_ref[...].astype(F32) * gb * (1.0 - gb)
        dbm_ref[0:1, :] += jnp.sum(dgla, axis=0, keepdims=True)
        dbm_ref[1:2, :] += jnp.sum(dglb, axis=0, keepdims=True)
        dphi_ref[:, D:2 * D] = dgla.astype(BF16)
        dphi_ref[:, 2 * D:3 * D] = dglb.astype(BF16)
        dpa = _dot(dya, w_ref[0], NT)
        dpb = _dot(dyb, w_ref[1], NT)
        dbg_ref[...] = (dpa * ca_ref[...].astype(F32)).astype(BF16)
        dca_ref[...] = dpa * bg_ref[...].astype(F32)
        yr = yr_ref[...].astype(F32)
        ge, th = _gelu(yr)
        dh_ref[...] = (dpb * ge).astype(BF16)
        dphi_ref[:, 0:D] = (dpb * h_ref[...].astype(F32) * _gelu_grad(yr, th)).astype(BF16)

    row = pl.BlockSpec((tm, D), lambda i: (i, 0))
    f32o, b16o = SDS((t, D), F32), SDS((t, D), BF16)
    acc2 = pl.BlockSpec((2, D), lambda i: (0, 0))
    return pl.pallas_call(
        body, name="mixb_head", grid=(t // tm,),
        in_specs=[row, row, pl.BlockSpec((tm, 1), lambda i: (i, 0)), _full((SMALL_ROWS, D)), _full((3, D, D)),
                  _pcol(tm, 0), _pcol(tm, 4), _pcol(tm, 5), _pcol(tm, 6),
                  pl.BlockSpec((1, D), lambda i: (0, 0)), pl.BlockSpec((1, D), lambda i: (0, 1)),
                  row, row, row, row],
        out_specs=(row,) * 7 + (pl.BlockSpec((tm, 3 * D), lambda i: (i, 0)), acc2, acc2),
        out_shape=(f32o, b16o, b16o, b16o, b16o, f32o, b16o, SDS((t, 3 * D), BF16), SDS((2, D), F32), SDS((2, D), F32)),
        compiler_params=_cp("arbitrary"),
    )(dxn, xh, rstd, small, w3, p, p, p, p, bm, bm, ya, yb, ca, h)


def mixb_rec(dh, gr, gi, h, xc, ap, rgw):
    t = dh.shape[0]
    tm = _row_tile(t, 256)
    nt = t // tm

    def body(dh_ref, gr_ref, gi_ref, h_ref, hh_ref, xc_ref, ap_ref, rgw_ref, dg_ref, dxc_ref, red_ref,
             ext, ext_a, c_s, lam_s, lam_c, a_c):
        i = nt - 1 - pl.program_id(0)
        first = i == 0

        @pl.when(pl.program_id(0) == 0)
        def _():
            red_ref[...] = jnp.zeros_like(red_ref)
            lam_c[...] = jnp.zeros_like(lam_c)
            a_c[...] = jnp.zeros_like(a_c)

        ext[0:8, :] = jnp.where(first, 0.0, hh_ref[...].astype(F32)[8:16, :])
        ext[8:, :] = h_ref[...].astype(F32)
        hprev = ext[pl.ds(7, tm), :]
        lam_s[...] = dh_ref[...].astype(F32)
        gr = gr_ref[...]
        gi = gi_ref[...]
        xc = xc_ref[...]
        ap = ap_ref[...]
        sp = _softplus(-ap)
        la = (-LRU_C) * gr * sp
        a = jnp.exp(la)
        ext_a[0:tm, :] = a
        ext_a[tm:, :] = a_c[...]
        c_s[...] = ext_a[pl.ds(1, tm), :]
        lam_c[...] = _scan_tile(c_s, lam_s, lam_s, lam_c[...], tm, reverse=True)
        a_c[...] = jnp.broadcast_to(a[0:1, :], (8, D))
        lam = lam_s[...]
        row = lax.broadcasted_iota(jnp.int32, (tm, D), 0) + i * tm
        start = row == 0
        mult = jnp.where(start, 1.0, jnp.sqrt(_neg_expm1(2.0 * la)))
        dmult = jnp.where(start, 0.0, lam * xc * gi)
        dla = lam * hprev * a - dmult * a * a / mult
        dg1 = (-LRU_C) * sp * dla * gr * (1.0 - gr)
        dg0 = lam * xc * mult * gi * (1.0 - gi)
        dsp = jnp.sum((-LRU_C) * gr * dla, axis=0, keepdims=True)
        red_ref[0:1, :] += jnp.sum(dg0, axis=0, keepdims=True)
        red_ref[1:2, :] += jnp.sum(dg1, axis=0, keepdims=True)
        red_ref[2:3, :] += -dsp * _sigmoid(-ap)
        dg0b = dg0.astype(BF16)
        dg1b = dg1.astype(BF16)
        dg_ref[0] = dg0b
        dg_ref[1] = dg1b
        parts = []
        for hd in range(HEADS):
            sl = slice(hd * HD, (hd + 1) * HD)
            parts.append(_dot(dg0b[:, sl], rgw_ref[0, hd], NT) + _dot(dg1b[:, sl], rgw_ref[1, hd], NT))
        dxc = lam * gi * mult + jnp.concatenate(parts, axis=1)
        dxc_ref[...] = dxc
        red_ref[3:4, :] += jnp.sum(dxc, axis=0, keepdims=True)

    row = pl.BlockSpec((tm, D), lambda i: (nt - 1 - i, 0))
    halo = pl.BlockSpec((16, D), lambda i: (jnp.maximum((nt - 1 - i) * (tm // 16) - 1, 0), 0))
    ext, tile, edge = pltpu.VMEM((tm + 8, D), F32), pltpu.VMEM((tm, D), F32), pltpu.VMEM((8, D), F32)
    return pl.pallas_call(
        body, name="mixb_rec", grid=(nt,),
        in_specs=[row, row, row, row, halo, row, _full((1, D)), _full((2, HEADS, HD, HD))],
        out_specs=(pl.BlockSpec((2, tm, D), lambda i: (0, nt - 1 - i, 0)), row, pl.BlockSpec((8, D), lambda i: (0, 0))),
        out_shape=(SDS((2, t, D), BF16), SDS((t, D), F32), SDS((8, D), F32)),
        scratch_shapes=[ext, ext, tile, tile, edge, edge],
        compiler_params=_cp("arbitrary"),
    )(dh, gr, gi, h, h, xc, ap, rgw)


def mixb_conv(dca, dxc, dbg, p, small):
    t = dca.shape[0]
    tm = _row_tile(t, 256)
    nt = t // tm

    def body(dca_ref, dcan_ref, dxc_ref, dxcn_ref, dbg_ref, cg_ref, v_ref, xr_ref, cgh_ref, vh_ref, xrh_ref, sm_ref,
             dplo_ref, dxr_ref, red_ref, e_dca, e_dxc, e_cv, e_xr):
        i = pl.program_id(0)
        first = i == 0
        last = i == nt - 1

        @pl.when(first)
        def _():
            red_ref[...] = jnp.zeros_like(red_ref)

        dca = dca_ref[...]
        dxc = dxc_ref[...]
        e_dca[0:tm, :] = dca
        e_dca[tm:, :] = jnp.where(last, 0.0, dcan_ref[...])
        e_dxc[0:tm, :] = dxc
        e_dxc[tm:, :] = jnp.where(last, 0.0, dxcn_ref[...])
        cg = cg_ref[...].astype(F32)
        v = v_ref[...].astype(F32)
        xr = xr_ref[...].astype(F32)
        e_cv[0:8, :] = jnp.where(first, 0.0, (cgh_ref[...].astype(F32) * vh_ref[...].astype(F32))[8:16, :])
        e_cv[8:, :] = cg * v
        e_xr[0:8, :] = jnp.where(first, 0.0, xrh_ref[...].astype(F32)[8:16, :])
        e_xr[8:, :] = xr
        dcv = (sm_ref[R_SC + 2:R_SC + 3, :] * dca + sm_ref[R_SC + 1:R_SC + 2, :] * e_dca[pl.ds(1, tm), :]
               + sm_ref[R_SC:R_SC + 1, :] * e_dca[pl.ds(2, tm), :])
        dplo_ref[:, 0:D] = dbg_ref[...]
        dplo_ref[:, D:2 * D] = (dcv * v).astype(BF16)
        dplo_ref[:, 2 * D:3 * D] = (dcv * cg).astype(BF16)
        dxr = (sm_ref[R_RC + 3:R_RC + 4, :] * dxc + sm_ref[R_RC + 2:R_RC + 3, :] * e_dxc[pl.ds(1, tm), :]
               + sm_ref[R_RC + 1:R_RC + 2, :] * e_dxc[pl.ds(2, tm), :] + sm_ref[R_RC:R_RC + 1, :] * e_dxc[pl.ds(3, tm), :])
        dxr_ref[...] = dxr.astype(BF16)
        for k in range(3):
            red_ref[R_SC + k:R_SC + k + 1, :] += jnp.sum(dca * e_cv[pl.ds(6 + k, tm), :], axis=0, keepdims=True)
        for k in range(4):
            red_ref[R_RC + k:R_RC + k + 1, :] += jnp.sum(dxc * e_xr[pl.ds(5 + k, tm), :], axis=0, keepdims=True)

    row = pl.BlockSpec((tm, D), lambda i: (i, 0))
    ext = pltpu.VMEM((tm + 8, D), F32)
    return pl.pallas_call(
        body, name="mixb_conv", grid=(nt,),
        in_specs=[row, _next_halo(tm, t, 0), row, _next_halo(tm, t, 0), row,
                  _pcol(tm, 1), _pcol(tm, 2), _pcol(tm, 3), _prev_halo16(tm, 1), _prev_halo16(tm, 2), _prev_halo16(tm, 3),
                  _full((SMALL_ROWS, D))],
        out_specs=(pl.BlockSpec((tm, 3 * D), lambda i: (i, 0)), row, pl.BlockSpec((8, D), lambda i: (0, 0))),
        out_shape=(SDS((t, 3 * D), BF16), SDS((t, D), BF16), SDS((8, D), F32)),
        scratch_shapes=[ext, ext, ext, ext],
        compiler_params=_cp("arbitrary"),
    )(dca, dca, dxc, dxc, dbg, p, p, p, p, p, p, small)


def mixb_dx(dz, dplo, dxr, dphi, win):
    t = dz.shape[0]
    tm = _row_tile(t, 1024)

    def body(dz_ref, lo_ref, xr_ref, hi_ref, w_ref, dx_ref, acc):
        k = pl.program_id(1)

        @pl.when(k == 0)
        def _():
            acc[...] = ALPHA * dz_ref[...]

        @pl.when(k < 3)
        def _():
            acc[...] += _dot(lo_ref[...], w_ref[...], NT)

        @pl.when(k == 3)
        def _():
            acc[...] += _dot(xr_ref[...], w_ref[...], NT)

        @pl.when(k > 3)
        def _():
            acc[...] += _dot(hi_ref[...], w_ref[...], NT)

        @pl.when(k == 6)
        def _():
            dx_ref[...] = acc[...]

    row = pl.BlockSpec((tm, D), lambda i, k: (i, 0))
    return pl.pallas_call(
        body, name="mixb_dx", grid=(t // tm, 7),
        in_specs=[row, pl.BlockSpec((tm, D), lambda i, k: (i, jnp.minimum(k, 2))), row,
                  pl.BlockSpec((tm, D), lambda i, k: (i, jnp.clip(k - 4, 0, 2))),
                  pl.BlockSpec((D, D), lambda i, k: (0, k))],
        out_specs=row, out_shape=SDS((t, D), F32),
        scratch_shapes=[pltpu.VMEM((tm, D), F32)],
        compiler_params=_cp("parallel", "arbitrary"),
    )(dz, dplo, dxr, dphi, win)


def mixb_dwin(xb, dplo, dxr, dphi):
    t = xb.shape[0]
    tk = _row_tile(t, 2048)
    nk = t // tk

    def body(x_ref, lo_ref, xr_ref, hi_ref, o_ref, acc):
        n = pl.program_id(0)
        k = pl.program_id(1)

        @pl.when(k == 0)
        def _():
            acc[...] = jnp.zeros_like(acc)

        @pl.when(n < 3)
        def _():
            acc[...] += _dot(x_ref[...], lo_ref[...], TN)

        @pl.when(n == 3)
        def _():
            acc[...] += _dot(x_ref[...], xr_ref[...], TN)

        @pl.when(n > 3)
        def _():
            acc[...] += _dot(x_ref[...], hi_ref[...], TN)

        @pl.when(k == nk - 1)
        def _():
            o_ref[...] = acc[...].astype(BF16)

    return pl.pallas_call(
        body, name="mixb_dwin", grid=(7, nk),
        in_specs=[pl.BlockSpec((tk, D), lambda n, k: (k, 0)),
                  pl.BlockSpec((tk, D), lambda n, k: (jnp.where(n < 3, k, 0), jnp.minimum(n, 2))),
                  pl.BlockSpec((tk, D), lambda n, k: (jnp.where(n == 3, k, 0), 0)),
                  pl.BlockSpec((tk, D), lambda n, k: (jnp.where(n > 3, k, 0), jnp.clip(n - 4, 0, 2)))],
        out_specs=pl.BlockSpec((D, D), lambda n, k: (0, n)), out_shape=SDS((D, D_IN), BF16),
        scratch_shapes=[pltpu.VMEM((D, D), F32)],
        compiler_params=_cp("parallel", "arbitrary"),
    )(xb, dplo, dxr, dphi)


def mixb_drgw(xcb, dg):
    t = xcb.shape[0]
    return _mm_tn("mixb_drgw", xcb, pl.BlockSpec((t, HD), lambda g, h: (0, h)),
                  dg, pl.BlockSpec((None, t, HD), lambda g, h: (g, 0, h)),
                  SDS((2, HEADS, HD, HD), BF16), pl.BlockSpec((None, None, HD, HD), lambda g, h: (g, h, 0, 0)),
                  (2, HEADS))


def loss_head(y, tgt):
    t = y.shape[0]
    tm = _row_tile(t, 512)

    def body(y_ref, t_ref, dy_ref, l_ref):
        i = pl.program_id(0)
        e = y_ref[...] - t_ref[...]
        dy_ref[...] = e * (1.0 / D)

        @pl.when(i == 0)
        def _():
            l_ref[...] = jnp.zeros_like(l_ref)

        l_ref[...] += 0.5 * jnp.sum(jnp.mean(e * e, axis=-1, keepdims=True), axis=0, keepdims=True)

    row = pl.BlockSpec((tm, D), lambda i: (i, 0))
    return pl.pallas_call(
        body, name="loss_head", grid=(t // tm,), in_specs=[row, row],
        out_specs=(row, pl.BlockSpec((8, 128), lambda i: (0, 0))),
        out_shape=(SDS((t, D), F32), SDS((8, 128), F32)),
        compiler_params=_cp("arbitrary"),
    )(y, tgt)


def _adamw(w, g, m, v):
    m = ADAM_B1 * m + (1.0 - ADAM_B1) * g
    v = ADAM_B2 * v + (1.0 - ADAM_B2) * (g * g)
    m_hat = m / (1.0 - ADAM_B1 ** ADAM_STEP)
    v_hat = v / (1.0 - ADAM_B2 ** ADAM_STEP)
    delta = -ADAM_LR * (m_hat / (jnp.sqrt(v_hat) + ADAM_EPS) + ADAM_WD * w)
    return delta, m, v


def adam_big(name, w, m, v, parts, rows, lanes, blk_off, tr, l0, prev=None):
    nr = rows // tr
    nl = len(parts)

    def body(w_ref, m_ref, v_ref, *rest):
        g_ref, d_ref, mo_ref, vo_ref = rest[-4:]
        l = pl.program_id(0)
        for ll in range(nl):
            pr = rest[ll]

            @pl.when(l == ll)
            def _():
                g = pr[0].astype(F32)
                for s in range(1, N_DEV):
                    g = g + pr[s].astype(F32)
                g_ref[...] = g
                d, mn, vn = _adamw(w_ref[...], g, m_ref[...], v_ref[...])
                d_ref[...] = d
                mo_ref[...] = mn
                vo_ref[...] = vn

    blk = pl.BlockSpec((None, tr, lanes), lambda l, r: (l + l0, r, 0))

    def part_spec(ll):
        return pl.BlockSpec((N_DEV, tr, lanes), lambda l, r: (0, jnp.where(l == ll, r, 0) + blk_off, 0))

    out = SDS(w.shape, F32)
    extra = [] if prev is None else list(prev)
    return pl.pallas_call(
        body, name=name, grid=(nl, nr),
        in_specs=[blk, blk, blk] + [part_spec(ll) for ll in range(nl)] + [ANY] * len(extra),
        out_specs=(blk,) * 4, out_shape=(out,) * 4,
        input_output_aliases={3 + nl + i: i for i in range(len(extra))},
        compiler_params=_cp("parallel", "parallel"),
    )(w, m, v, *parts, *extra)


def small_update(me1, land_sh, land_rep, sharded, replicated):
    ns, nr = len(sharded), len(replicated)

    def body(me_ref, lsh, lrep, *refs):
        ins, outs = refs[:3 * (ns + nr)], refs[3 * (ns + nr):]

        def total(read):
            g = read(0)
            for s in range(1, N_DEV):
                g = g + read(s)
            return g

        def update(k, g, sl):
            w_ref, m_ref, v_ref = ins[3 * k:3 * k + 3]
            d, mn, vn = _adamw(w_ref[sl], g, m_ref[sl], v_ref[sl])
            for o, val in zip(outs[4 * k:4 * k + 4], (g, d, mn, vn)):
                o[sl] = val

        for k, (r0, n) in enumerate(GS_ROWS):
            update(k, total(lambda s: lsh[s, :, r0:r0 + n, :]), (slice(None),) * 3)
        lo, hi = (slice(None), slice(0, D)), (slice(None), slice(D, 2 * D))
        update(ns, total(lambda s: lrep[s, 0:4, :]), lo)
        update(ns, total(lambda s: lrep[s, 4:8, :]), hi)
        update(ns + 1, total(lambda s: lrep[s, 8:12, :]), (slice(None),) * 2)
        update(ns + 2, total(lambda s: lrep[s, 12:16, :]), (slice(None),) * 2)

    def whole(a):
        nd = a.ndim
        return pl.BlockSpec(a.shape, lambda i, me: (0,) * nd)

    params = [a for wmv in list(sharded) + list(replicated) for a in wmv]
    out_shape = tuple(SDS(wmv[0].shape, F32) for wmv in list(sharded) + list(replicated) for _ in range(4))
    res = pl.pallas_call(
        body, name="small_update", out_shape=out_shape,
        grid_spec=pltpu.PrefetchScalarGridSpec(
            num_scalar_prefetch=1, grid=(1,),
            in_specs=[pl.BlockSpec((N_DEV, DEPTH, 16, 128), lambda i, me: (0, 0, 0, me[0])), whole(land_rep)]
            + [whole(a) for a in params],
            out_specs=tuple(pl.BlockSpec(s.shape, lambda i, me, nd=len(s.shape): (0,) * nd) for s in out_shape)),
        compiler_params=_cp("arbitrary"),
    )(me1, land_sh, land_rep, *params)
    return [list(res[4 * k:4 * k + 4]) for k in range(ns + nr)]


def _ffn_forward(xf, xb, gw, f, s):
    g, u, a = ffn_up(xb, gw["w1"], f)
    xo, xob, xh, rs = ffn_down_ln(a, gw["w2"], f, xf, gw["small"], s)
    return (xo, xob), dict(xb=xb, g=g, u=u, a=a, xh=xh, rs=rs)


def _ffn_backward(dxn, sv, gw, f, s):
    dz, df, dgu, dln = ffn_bwd_gates(dxn, sv["xh"], sv["rs"], gw["small"], s, gw["w2"], f, sv["g"], sv["u"])
    dx = ffn_bwd_dx(dz, dgu, gw["w1"], f)
    dw1 = ffn_dw1(sv["xb"], dgu, f)
    dw2 = ffn_dw2(sv["a"], df, f)
    return dx, dw1, dw2, dln


def _mixer_forward(xf, xb, gw, rcb, ap, bm):
    p = mix_proj(xb, gw["win"])
    ca, pa, xc, xcb, gi, gr, h = mix_pre(p, gw["small"], rcb, ap, gw["rgw"])
    pb, ya, yb, m, xo, xob, xh, rs = mix_out(pa, h, p, bm, gw["w3"], xf, gw["small"])
    sv = dict(xb=xb, p=p, ca=ca, pa=pa, xc=xc, xcb=xcb, gi=gi, gr=gr, h=h, pb=pb, ya=ya, yb=yb, m=m, xh=xh, rs=rs)
    return (xo, xob), sv


def _mixer_backward(dxn, sv, gw, rcb, ap, bm):
    dz, dzb, dya, dyb, dbg, dca, dh, dphi, dln, dbm = mixb_head(
        dxn, sv["xh"], sv["rs"], gw["small"], gw["w3"], sv["p"], bm, sv["ya"], sv["yb"], sv["ca"], sv["h"])
    dg, dxc, red_rec = mixb_rec(dh, sv["gr"], sv["gi"], sv["h"], sv["xc"], ap, gw["rgw"])
    dplo, dxr, red_conv = mixb_conv(dca, dxc, dbg, sv["p"], gw["small"])
    dx = mixb_dx(dz, dplo, dxr, dphi, gw["win"])
    dwin = mixb_dwin(sv["xb"], dplo, dxr, dphi)
    dwo = mm_tn_square("mixb_dwo", sv["m"], dzb)
    dwoc = mm_tn_square("mixb_dwoc", sv["pa"], dya)
    dwor = mm_tn_square("mixb_dwor", sv["pb"], dyb)
    drgw = mixb_drgw(sv["xcb"], dg)
    return dx, dict(dwin=dwin, dwoc=dwoc, dwor=dwor, dwo=dwo, drgw=drgw), dln, dbm, red_rec, red_conv


def kernel(x, w_in, b_merge, sc_w, rc_w, rc_b, rg_w, rg_b, a_param, w_out_conv, w_out_rnn, w_o, ffn_w1, ffn_w2, ln_g, ln_b, loss_target, m_w_in, m_b_merge, m_sc_w, m_rc_w, m_rc_b, m_rg_w, m_rg_b, m_a_param, m_w_out_conv, m_w_out_rnn, m_w_o, m_ffn_w1, m_ffn_w2, m_ln_g, m_ln_b, v_w_in, v_b_merge, v_sc_w, v_rc_w, v_rc_b, v_rg_w, v_rg_b, v_a_param, v_w_out_conv, v_w_out_rnn, v_w_o, v_ffn_w1, v_ffn_w2, v_ln_g, v_ln_b):
    t = x.shape[1]
    me = _me()

    def rows(parts, total):
        out, off = None, 0
        for part in parts:
            r = part.shape[-2]
            pad = [(0, 0)] * (part.ndim - 2) + [(off, total - off - r), (0, 0)]
            padded = jnp.pad(part, pad)
            out = padded if out is None else out + padded
            off += r
        return out

    small = rows([sc_w, rc_w, rg_b, ln_g, ln_b], SMALL_ROWS)

    me1 = jnp.reshape(me, (1,)).astype(jnp.int32)

    def layer_shards(l, small_now):
        return gather_prep(l, me1, w_in, w_out_conv, w_out_rnn, w_o, ffn_w1, ffn_w2, rg_w, small_now)

    def as_weights(lands):
        return dict(zip(("win", "w3", "w1", "w2", "rgw", "small"), lands))

    xf = x.reshape(t, D)
    xb = xf.astype(BF16)

    early_at, rest_at = [2, 3, 5], [0, 1, 2, 3, 4]

    def ffn_slots(f, o_w1, o_w2, p):
        return [o_w1.at[f, p], o_w2.at[f, _slab(p, W2S, 16), :]]

    def early_slots(lands, p):
        return ffn_slots(0, lands[0], lands[1], p) + [lands[2].at[:, _slab(p, 128, 128)]]

    def rest_slots(lands, p):
        o_win, o_w3, o_w1, o_w2, o_rgw = lands
        return ([o_win.at[:, _slab(p, WIN_S, 128)], o_w3.at[:, _slab(p, 128, 128), :]] + ffn_slots(1, o_w1, o_w2, p)
                + [o_rgw.at[:, :, _slab(p, 32, 32), :]])

    early = ChipGather("gather0a", 3, 3, 3, lambda refs: [refs[0].at[0], refs[1].at[0], refs[2]], early_slots)
    rest = ChipGather("gather0b", 5, 5, 5, lambda refs: [refs[0], refs[1], refs[2].at[1], refs[3].at[1], refs[4]],
                      rest_slots)
    shards, own = layer_shards(0, small)

    def step(method, at, *args):
        res = method(*args[:-1], [shards[i] for i in at], [own[i] for i in at], args[-1])
        for i, s_new, o_new in zip(at, res[-3], res[-2]):
            shards[i], own[i] = s_new, o_new
        return tuple(res[:-3]) + (res[-1],)

    sems_a, thru = step(early.first, early_at, [xb, small])
    sems_b, thru = step(rest.first, rest_at, thru)
    relay_a, thru = step(early.relay, early_at, sems_a, thru)
    ((xb, small),) = step(early.last, early_at, sems_a, relay_a, thru)
    cur, s0 = _ffn_forward(xf, xb, as_weights(own), 0, 0)
    relay_b, thru = step(rest.relay, rest_at, sems_b, list(cur) + [small])
    (thru,) = step(rest.last, rest_at, sems_b, relay_b, thru)
    cur, small = thru[:2], thru[2]

    gathers = [ChipGather(f"gather{l}") for l in range(DEPTH)]
    gws, saved = [as_weights(own)], []
    for l in range(DEPTH):
        gw = gws[l]
        nxt = l + 1 < DEPTH
        if nxt:
            shards, own = layer_shards(l + 1, small)
            sems, shards, own, thru = gathers[l + 1].first(shards, own, list(cur) + [small])
            cur, small = thru[:2], thru[2]
        rcb, ap, bm = rc_b[l][None], a_param[l][None], b_merge[l][None]
        if l > 0:
            cur, s0 = _ffn_forward(cur[0], cur[1], gw, 0, 0)
        cur, s1 = _mixer_forward(cur[0], cur[1], gw, rcb, ap, bm)
        if nxt and l > 0:
            relay_sems, shards, own, cur = gathers[l + 1].relay(sems, shards, own, cur)
        cur, s2 = _ffn_forward(cur[0], cur[1], gw, 1, 2)
        if nxt and l == 0:
            relay_sems, shards, own, cur = gathers[l + 1].relay(sems, shards, own, cur)
        saved.append((s0, s1, s2))
        if nxt:
            _, own, cur = gathers[l + 1].last(sems, relay_sems, shards, own, cur)
            gws.append(as_weights(own))

    dy, loss_tile = loss_head(cur[0], loss_target.reshape(t, D))
    loss = lax.psum(loss_tile[0, 0], ("x", "y", "c"))

    lands = [dict() for _ in range(DEPTH)]
    gsmall, grep = [None] * DEPTH, [None] * DEPTH
    flights = {}

    def launch(key, kinds, grads, dy):
        ex, prep = grad_scatter(f"scatter{key}", kinds)
        fl = ex.start(grads, prep(me1, grads), [] if dy is None else [dy])
        flights[key] = (ex, fl)
        return None if dy is None else fl[4][0]

    def landed(key, thru):
        ex, fl = flights.pop(key)
        return ex.wait(fl[0], fl[1], fl[2], fl[3], thru)

    ffn_b, mixer, ffn_a = ("w1b", "w2b"), ("w_in", "w3", "rgw"), ("w1a", "w2a")
    for l in reversed(range(DEPTH)):
        gw = gws[l]
        rcb, ap, bm = rc_b[l][None], a_param[l][None], b_merge[l][None]
        s0, s1, s2 = saved[l]
        dy, dw1b, dw2b, dln2 = _ffn_backward(dy, s2, gw, 1, 2)
        if l == 0:
            dy = launch("0b", ["ffn"], [dw1b, dw2b], dy)
        dy, dmix, dln1, dbm, red_rec, red_conv = _mixer_backward(dy, s1, gw, rcb, ap, bm)
        mixer_grads = [dmix["dwin"], dmix["dwoc"], dmix["dwor"], dmix["dwo"], dmix["drgw"]]
        if l == 0:
            dy = launch("0m", ["mixer"], mixer_grads, dy)
        dy, dw1a, dw2a, dln0 = _ffn_backward(dy, s0, gw, 0, 0)
        if l == 0:
            launch("0a", ["ffn"], [dw1a, dw2a], None)
        else:
            if l + 1 < DEPTH:
                got, (dy,) = landed(str(l + 1), [dy])
                lands[l + 1] = dict(zip(ffn_b + mixer + ffn_a, got))
            dy = launch(str(l), ["ffn", "mixer", "ffn"], [dw1b, dw2b] + mixer_grads + [dw1a, dw2a], dy)
        gsmall[l] = rows([red_conv[0:8], red_rec[0:2], dln0[0:1], dln1[0:1], dln2[0:1], dln0[1:2], dln1[1:2],
                          dln2[1:2]], 16)
        grep[l] = [dbm[0:1], dbm[1:2], red_rec[3:4], red_rec[2:3]]
    grad_x = dy.reshape(1, t, D)

    g_rep = rows([grep[l][kind] for kind in range(4) for l in range(DEPTH)], 16)
    land_sh, land_rep = gather_small_grads(jnp.stack(gsmall, axis=0), g_rep)
    small_names = ("sc_w", "rc_w", "rg_b", "ln_g", "ln_b", "b_merge", "rc_b", "a_param")
    small_res = small_update(
        me1, land_sh, land_rep,
        [(sc_w, m_sc_w, v_sc_w), (rc_w, m_rc_w, v_rc_w), (rg_b, m_rg_b, v_rg_b), (ln_g, m_ln_g, v_ln_g),
         (ln_b, m_ln_b, v_ln_b)],
        [(b_merge, m_b_merge, v_b_merge), (rc_b, m_rc_b, v_rc_b), (a_param, m_a_param, v_a_param)])

    def parts_of(layers):
        ls = [lands[l] for l in layers]
        return dict(w_in=[a["w_in"] for a in ls],
                    w3=[a["w3"].reshape(N_DEV, 3 * 128, D) for a in ls],
                    rgw=[a["rgw"].reshape(N_DEV, 2 * HEADS * 32, HD) for a in ls],
                    w1=[a[k] for a in ls for k in ("w1a", "w1b")],
                    w2=[a[k] for a in ls for k in ("w2a", "w2b")])

    families = [("w_in", w_in, m_w_in, v_w_in, "w_in", 1, D, WIN_S, 0, 128),
                ("w_out_conv", w_out_conv, m_w_out_conv, v_w_out_conv, "w3", 1, 128, D, 0, 128),
                ("w_out_rnn", w_out_rnn, m_w_out_rnn, v_w_out_rnn, "w3", 1, 128, D, 1, 128),
                ("w_o", w_o, m_w_o, v_w_o, "w3", 1, 128, D, 2, 128),
                ("ffn_w1", ffn_w1, m_ffn_w1, v_ffn_w1, "w1", 2, D, FS, 0, 256),
                ("ffn_w2", ffn_w2, m_ffn_w2, v_ffn_w2, "w2", 2, W2S, D, 0, W2S // 2),
                ("rg_w", rg_w, m_rg_w, v_rg_w, "rgw", 1, 2 * HEADS * 32, HD, 0, 256)]

    def adam_pass(tag, layers, prev):
        parts = parts_of(layers)
        outs = {}
        for name, w, m, v, fam, per, nrow, lanes, blk_off, tr in families:
            r3 = lambda a: a.reshape(DEPTH * per, nrow, lanes)
            outs[name] = adam_big(f"adam_{name}_{tag}", r3(w), r3(m), r3(v), parts[fam], nrow, lanes, blk_off, tr,
                                  per * layers[0], None if prev is None else prev[name])
        return outs

    def through(outs, key, extra):
        thru = [outs[f[0]][0] for f in families] + list(extra)
        got, thru = landed(key, thru)
        outs = {f[0]: [thru[i]] + list(outs[f[0]][1:]) for i, f in enumerate(families)}
        return got, outs, thru[len(families):]

    done = adam_pass("top", list(range(2, DEPTH)), None)
    got, done, _ = through(done, "1", [])
    lands[1] = dict(zip(ffn_b + mixer + ffn_a, got))
    done = adam_pass("second", [1], done)
    for key, names in (("0b", ffn_b), ("0m", mixer), ("0a", ffn_a)):
        got, done, _ = through(done, key, [])
        lands[0].update(zip(names, got))
    done = adam_pass("first", [0], done)
    res = {f[0]: [o.reshape(f[1].shape) for o in done[f[0]]] for f in families}

    res.update(zip(small_names, small_res))

    names = ["w_in", "b_merge", "sc_w", "rc_w", "rc_b", "rg_w", "rg_b", "a_param", "w_out_conv", "w_out_rnn", "w_o",
             "ffn_w1", "ffn_w2", "ln_g", "ln_b"]
    out = [loss, grad_x]
    for k in range(4):
        out += [res[n][k] for n in names]
    return tuple(out)
```

```python
import functools

import jax
import jax.numpy as jnp
from jax import lax
from jax.experimental import pallas as pl
from jax.experimental.pallas import tpu as pltpu

F32 = jnp.float32
BF16 = jnp.bfloat16
SDS = jax.ShapeDtypeStruct

N_DEV = 8
DEPTH = 4
D = 1024
D_FF = 2816
FS = D_FF // 4
W2S = D_FF // 8
D_IN = 7 * D
WIN_S = D_IN // 8
HEADS = 4
HD = D // HEADS
LRU_C = 8.0
ALPHA = (2.0 * DEPTH) ** 0.25
LN_EPS = 1e-5
ADAM_LR, ADAM_B1, ADAM_B2, ADAM_EPS, ADAM_WD, ADAM_STEP = 0.001, 0.9, 0.999, 1e-08, 0.01, 10

R_SC, R_RC, R_RGB, R_LNG, R_LNB = 0, 3, 7, 9, 12
SMALL_ROWS = 16
GS_ROWS = ((0, 3), (3, 4), (8, 2), (10, 3), (13, 3))

NN = ((1,), (0,))
NT = ((1,), (1,))
TN = ((0,), (0,))
MESH = pl.DeviceIdType.MESH
ANY = pl.BlockSpec(memory_space=pl.ANY)
VMEM_LIMIT = 52 * 1024 * 1024


def _dot(a, b, dims):
    return lax.dot_general(a, b, (dims, ((), ())), preferred_element_type=F32)


def _cp(*sem):
    return pltpu.CompilerParams(dimension_semantics=sem, vmem_limit_bytes=VMEM_LIMIT)


def _sigmoid(x):
    return 1.0 / (1.0 + jnp.exp(-x))


def _sigmoid_t(x):
    return 0.5 * jnp.tanh(0.5 * x) + 0.5


def _log1p(e):
    u = 1.0 + e
    return jnp.where(u == 1.0, e, jnp.log(u) * e / jnp.where(u == 1.0, 1.0, u - 1.0))


def _softplus(x):
    return jnp.maximum(x, 0.0) + _log1p(jnp.exp(-jnp.abs(x)))


def _neg_expm1(x):
    u = jnp.exp(x)
    um1 = u - 1.0
    safe = jnp.logical_and(u != 1.0, um1 != -1.0)
    r = um1 * x / jnp.where(safe, jnp.log(jnp.where(safe, u, 0.5)), 1.0)
    return -jnp.where(u == 1.0, x, jnp.where(um1 == -1.0, -1.0, r))


def _gelu(y):
    c = 0.7978845608028654
    t = jnp.tanh(c * (y + 0.044715 * y * y * y))
    return 0.5 * y * (1.0 + t), t


def _gelu_grad(y, t):
    c = 0.7978845608028654
    return 0.5 * (1.0 + t) + 0.5 * y * (1.0 - t * t) * c * (1.0 + 3.0 * 0.044715 * y * y)


def _ln_fwd(z, g, b):
    mu = jnp.mean(z, axis=-1, keepdims=True)
    zc = z - mu
    var = jnp.mean(zc * zc, axis=-1, keepdims=True)
    rstd = lax.rsqrt(var + LN_EPS)
    xh = zc * rstd
    return xh * g + b, xh, rstd


def _ln_bwd(dy, xh, rstd, g):
    dxh = dy * g
    m1 = jnp.mean(dxh, axis=-1, keepdims=True)
    m2 = jnp.mean(dxh * xh, axis=-1, keepdims=True)
    return rstd * (dxh - m1 - xh * m2)


def _row_tile(t, want):
    return min(want, t)


def _me():
    return 4 * lax.axis_index("x") + 2 * lax.axis_index("y") + lax.axis_index("c")


def _coords(p):
    return (p // 4, (p // 2) % 2, p % 2)


def _all_to_all_copies(srcs_of, dsts_of, waits, send_sems, recv_sems, loc_sems):
    me = _me()
    n = len(waits)
    own_src, own_dst = srcs_of(me), dsts_of(me)
    local = [pltpu.make_async_copy(own_src[k], own_dst[k], loc_sems.at[k]) for k in range(n)]
    for cp in local:
        cp.start()
    for d in range(1, N_DEV):
        p = (me + d) % N_DEV
        src, dst = srcs_of(p), dsts_of(me)
        for k in range(n):
            pltpu.make_async_remote_copy(
                src_ref=src[k], dst_ref=dst[k], send_sem=send_sems.at[waits[k][1]],
                recv_sem=recv_sems.at[waits[k][1]], device_id=_coords(p), device_id_type=MESH).start()
    done = set()
    for k in range(n):
        ref, s = waits[k]
        if s in done:
            continue
        done.add(s)
        pltpu.make_async_remote_copy(
            src_ref=ref, dst_ref=ref, send_sem=send_sems.at[s], recv_sem=recv_sems.at[s],
            device_id=_coords(me), device_id_type=MESH).wait()
    for cp in local:
        cp.wait()


HBM = pl.BlockSpec(memory_space=pltpu.HBM)
SEM = pl.BlockSpec(memory_space=pltpu.SEMAPHORE)
EFFECT = pltpu.SideEffectType.DATAFLOW_SIDE_EFFECTING


def _in_hbm(a):
    return pltpu.with_memory_space_constraint(a, pltpu.HBM)


class Exchange:
    def __init__(self, name, src_of, dst_of, sem_of, span_of):
        self.name, self.src_of, self.dst_of, self.sem_of, self.span_of = name, src_of, dst_of, sem_of, span_of
        self.nsem = max(sem_of) + 1

    def start(self, srcs, lands, thru):
        n, m = len(srcs), len(lands)
        ops = list(srcs) + list(lands) + list(thru)

        def body(*refs):
            src_refs, land_refs = refs[:n], refs[n:n + m]
            send_sems, recv_sems = refs[len(ops)], refs[len(ops) + 1]
            me = _me()
            for dd in range(1, N_DEV):
                p = (me + dd) % N_DEV
                s, d = self.src_of(src_refs, p), self.dst_of(land_refs, me)
                for k in range(len(self.sem_of)):
                    pltpu.make_async_remote_copy(
                        src_ref=s[k], dst_ref=d[k], send_sem=send_sems.at[self.sem_of[k]],
                        recv_sem=recv_sems.at[self.sem_of[k]], device_id=_coords(p), device_id_type=MESH).start()

        sem = pltpu.SemaphoreType.DMA((self.nsem,))
        res = pl.pallas_call(
            body, name=self.name + "_start",
            out_shape=(sem, sem) + tuple(pltpu.HBM(a.shape, a.dtype) for a in ops),
            in_specs=[HBM] * len(ops), out_specs=(SEM, SEM) + (HBM,) * len(ops),
            input_output_aliases={i: 2 + i for i in range(len(ops))},
            compiler_params=pltpu.CompilerParams(has_side_effects=EFFECT),
        )(*[_in_hbm(a) for a in ops])
        return res[0], res[1], res[2:2 + n], res[2 + n:2 + n + m], list(res[2 + n + m:])

    def wait(self, send_sems, recv_sems, srcs, lands, thru):
        n, m = len(srcs), len(lands)
        ops = list(srcs) + list(lands) + list(thru)

        def body(*refs):
            land_refs = refs[n:n + m]
            ssem, rsem = refs[len(ops)], refs[len(ops) + 1]
            me = _me()
            spans = self.span_of(land_refs)
            for s in range(self.nsem):
                cp = pltpu.make_async_remote_copy(
                    src_ref=spans[s], dst_ref=spans[s], send_sem=ssem.at[s], recv_sem=rsem.at[s],
                    device_id=_coords(me), device_id_type=MESH)
                cp.wait_send()
                cp.wait_recv()

        res = pl.pallas_call(
            body, name=self.name + "_wait",
            out_shape=tuple(pltpu.HBM(a.shape, a.dtype) for a in ops),
            in_specs=[HBM] * len(ops) + [SEM, SEM], out_specs=(HBM,) * len(ops),
            input_output_aliases={i: i for i in range(len(ops))},
            compiler_params=pltpu.CompilerParams(has_side_effects=EFFECT),
        )(*ops, send_sems, recv_sems)
        return res[n:n + m], list(res[n + m:])


class ChipGather:
    def __init__(self, name, nsrc=6, nland=6, ncopy=6, views=None, slots=None):
        self.name, self.nsrc, self.nland, self.ncopy = name, nsrc, nland, ncopy
        self.views = views if views is not None else list
        self.slots = slots if slots is not None else self.layer_slots

    @staticmethod
    def layer_slots(lands, p):
        o_win, o_w3, o_w1, o_w2, o_rgw, o_sm = lands
        return [o_win.at[:, pl.ds(pl.multiple_of(p * WIN_S, 128), WIN_S)],
                o_w3.at[:, pl.ds(pl.multiple_of(p * 128, 128), 128), :],
                o_w1.at[:, p],
                o_w2.at[:, pl.ds(pl.multiple_of(p * W2S, 16), W2S), :],
                o_rgw.at[:, :, pl.ds(pl.multiple_of(p * 32, 32), 32), :],
                o_sm.at[:, pl.ds(pl.multiple_of(p * 128, 128), 128)]]

    @staticmethod
    def _places():
        x, y, c = lax.axis_index("x"), lax.axis_index("y"), lax.axis_index("c")
        chips = [(1 - x, y), (x, 1 - y), (1 - x, 1 - y)]
        return (x, y, c), (x, y, 1 - c), chips

    @staticmethod
    def _index(place):
        return 4 * place[0] + 2 * place[1] + place[2]

    def _call(self, tag, body, ops, sems_in, sems_out):
        nops = len(ops)
        sem = pltpu.SemaphoreType.DMA((max(sems_out, 1),))
        res = pl.pallas_call(
            body, name=f"{self.name}_{tag}",
            out_shape=((sem, sem) if sems_out else ()) + tuple(pltpu.HBM(a.shape, a.dtype) for a in ops),
            in_specs=[HBM] * nops + [SEM] * len(sems_in),
            out_specs=((SEM, SEM) if sems_out else ()) + (HBM,) * nops,
            input_output_aliases={i: (2 if sems_out else 0) + i for i in range(nops)},
            compiler_params=pltpu.CompilerParams(has_side_effects=EFFECT),
        )(*[_in_hbm(a) for a in ops], *sems_in)
        return res

    def first(self, srcs, lands, thru):
        ns, nl, n = self.nsrc, self.nland, self.ncopy
        ops = list(srcs) + list(lands) + list(thru)

        def body(*refs):
            src_refs, land_refs = self.views(refs[:ns]), refs[ns:ns + nl]
            send_sems, recv_sems = refs[len(ops)], refs[len(ops) + 1]
            me, sibling, chips = self._places()
            dst = self.slots(land_refs, self._index(me))
            targets = [sibling] + [(cx, cy, me[2]) for cx, cy in chips]
            for r, to in enumerate(targets):
                for k in range(n):
                    pltpu.make_async_remote_copy(
                        src_ref=src_refs[k], dst_ref=dst[k], send_sem=send_sems.at[n *r + k],
                        recv_sem=recv_sems.at[n *r + k], device_id=to, device_id_type=MESH).start()

        res = self._call("first", body, ops, [], 4 * n)
        return (res[0], res[1]), list(res[2:2 + ns]), list(res[2 + ns:2 + ns + nl]), list(res[2 + ns + nl:])

    def relay(self, sems, srcs, lands, thru):
        ns, nl, n = self.nsrc, self.nland, self.ncopy
        ops = list(srcs) + list(lands) + list(thru)

        def wait_body(*refs):
            land_refs = refs[ns:ns + nl]
            ssem, rsem = refs[len(ops)], refs[len(ops) + 1]
            me, sibling, chips = self._places()
            for j, (cx, cy) in enumerate(chips):
                got = self.slots(land_refs, self._index((cx, cy, me[2])))
                for k in range(n):
                    pltpu.make_async_remote_copy(
                        src_ref=got[k], dst_ref=got[k], send_sem=ssem.at[n *(1 + j) + k], recv_sem=rsem.at[n *(1 + j) + k],
                        device_id=me, device_id_type=MESH).wait_recv()

        ops = list(self._call("landed", wait_body, ops, list(sems), 0))

        def start_body(*refs):
            land_refs = refs[ns:ns + nl]
            ssem, rsem = refs[len(ops)], refs[len(ops) + 1]
            me, sibling, chips = self._places()
            for j, (cx, cy) in enumerate(chips):
                got = self.slots(land_refs, self._index((cx, cy, me[2])))
                for k in range(n):
                    pltpu.make_async_remote_copy(
                        src_ref=got[k], dst_ref=got[k], send_sem=ssem.at[n *j + k], recv_sem=rsem.at[n *j + k],
                        device_id=sibling, device_id_type=MESH).start()

        res = self._call("relay", start_body, ops, [], 3 * n)
        return (res[0], res[1]), list(res[2:2 + ns]), list(res[2 + ns:2 + ns + nl]), list(res[2 + ns + nl:])

    def last(self, sems, relay_sems, srcs, lands, thru):
        ns, nl, n = self.nsrc, self.nland, self.ncopy
        ops = list(srcs) + list(lands) + list(thru)

        def body(*refs):
            src_refs, land_refs = self.views(refs[:ns]), refs[ns:ns + nl]
            ssem, rsem, ssem2, rsem2 = refs[len(ops):len(ops) + 4]
            me, sibling, chips = self._places()
            got = self.slots(land_refs, self._index(sibling))
            for k in range(n):
                pltpu.make_async_remote_copy(
                    src_ref=got[k], dst_ref=got[k], send_sem=ssem.at[k], recv_sem=rsem.at[k],
                    device_id=me, device_id_type=MESH).wait_recv()
            for j, (cx, cy) in enumerate(chips):
                got = self.slots(land_refs, self._index((cx, cy, 1 - me[2])))
                for k in range(n):
                    pltpu.make_async_remote_copy(
                        src_ref=got[k], dst_ref=got[k], send_sem=ssem2.at[n *j + k], recv_sem=rsem2.at[n *j + k],
                        device_id=me, device_id_type=MESH).wait_recv()
            for sem_s, sem_r, count in ((ssem, rsem, 4), (ssem2, rsem2, 3)):
                for r in range(count):
                    for k in range(n):
                        pltpu.make_async_remote_copy(
                            src_ref=src_refs[k], dst_ref=src_refs[k], send_sem=sem_s.at[n *r + k],
                            recv_sem=sem_r.at[n *r + k], device_id=me, device_id_type=MESH).wait_send()

        res = self._call("last", body, ops, list(sems) + list(relay_sems), 0)
        return list(res[:ns]), list(res[ns:ns + nl]), list(res[ns + nl:])


GATHER_SHAPES = (SDS((D, D_IN), BF16), SDS((3, D, D), BF16), SDS((2, N_DEV, D, FS), BF16),
                 SDS((2, D_FF, D), BF16), SDS((2, HEADS, HD, HD), BF16), SDS((SMALL_ROWS, D), F32))


def _slab(p, n, align):
    return pl.ds(pl.multiple_of(p * n, align), n)


class GradGroup:
    def __init__(self, nsrc, lands, src_of, dst_of, sem_of, in_specs, out_specs, copy):
        self.nsrc, self.lands, self.src_of, self.dst_of, self.sem_of = nsrc, lands, src_of, dst_of, sem_of
        self.in_specs, self.out_specs, self.copy = in_specs, out_specs, copy


def _mixer_group():
    hd = D // 2

    def src_of(refs, p):
        dwin, dwoc, dwor, dwo, drgw = refs
        r3 = _slab(p, 128, 128)
        return [dwin.at[:, _slab(p, WIN_S, 128)], dwoc.at[r3, :], dwor.at[r3, :], dwo.at[r3, :],
                drgw.at[:, :, _slab(p, 32, 32), :]]

    def dst_of(lands, p):
        l_win, l_w3, l_rgw = lands
        return [l_win.at[p], l_w3.at[p, 0], l_w3.at[p, 1], l_w3.at[p, 2], l_rgw.at[p]]

    def copy(srcs, lands):
        lands[0][...] = srcs[0][...]
        for k in range(3):
            lands[1][k] = srcs[1 + k][...]
        lands[2][...] = srcs[4][...]

    three = pl.BlockSpec((64, D), lambda i, me: (2 * me[0] + i, 0))
    return GradGroup(
        5, (SDS((N_DEV, D, WIN_S), BF16), SDS((N_DEV, 3, 128, D), BF16), SDS((N_DEV, 2, HEADS, 32, HD), BF16)),
        src_of, dst_of, [0, 1, 1, 1, 2],
        [pl.BlockSpec((hd, WIN_S), lambda i, me: (i, me[0])), three, three, three,
         pl.BlockSpec((2, HEADS, 16, HD), lambda i, me: (0, 0, 2 * me[0] + i, 0))],
        [pl.BlockSpec((None, hd, WIN_S), lambda i, me: (me[0], i, 0)),
         pl.BlockSpec((None, 3, 64, D), lambda i, me: (me[0], 0, i, 0)),
         pl.BlockSpec((None, 2, HEADS, 16, HD), lambda i, me: (me[0], 0, 0, i, 0))],
        copy)


def _ffn_group():
    hd, hw = D // 2, W2S // 2

    def src_of(refs, p):
        return [refs[0].at[p], refs[1].at[_slab(p, W2S, 16), :]]

    def dst_of(lands, p):
        return [lands[0].at[p], lands[1].at[p]]

    def copy(srcs, lands):
        lands[0][...] = srcs[0][...]
        lands[1][...] = srcs[1][...]

    return GradGroup(
        2, (SDS((N_DEV, D, FS), BF16), SDS((N_DEV, W2S, D), BF16)), src_of, dst_of, [0, 1],
        [pl.BlockSpec((None, hd, FS), lambda i, me: (me[0], i, 0)), pl.BlockSpec((hw, D), lambda i, me: (2 * me[0] + i, 0))],
        [pl.BlockSpec((None, hd, FS), lambda i, me: (me[0], i, 0)), pl.BlockSpec((None, hw, D), lambda i, me: (me[0], i, 0))],
        copy)


def grad_scatter(name, kinds):
    groups = [_mixer_group() if k == "mixer" else _ffn_group() for k in kinds]

    def per_group(refs, counts, fn):
        out, i = [], 0
        for g, c in zip(groups, counts):
            out += fn(g, refs[i:i + c])
            i += c
        return out

    nsrcs = [g.nsrc for g in groups]
    nlands = [len(g.lands) for g in groups]
    sem_of, off = [], 0
    for g in groups:
        sem_of += [off + s for s in g.sem_of]
        off += len(g.lands)

    ex = Exchange(name,
                  lambda refs, p: per_group(refs, nsrcs, lambda g, r: g.src_of(r, p)),
                  lambda lands, p: per_group(lands, nlands, lambda g, r: g.dst_of(r, p)),
                  sem_of, lambda lands: [a.at[pl.ds(0, 7)] for a in lands])

    def prep(me1, grads):
        nsrc = sum(nsrcs)

        def body(me_ref, *refs):
            srcs, lands = refs[:nsrc], refs[nsrc:]
            i = j = 0
            for g in groups:
                g.copy(srcs[i:i + g.nsrc], lands[j:j + len(g.lands)])
                i += g.nsrc
                j += len(g.lands)

        return list(pl.pallas_call(
            body, name=name + "_prep", out_shape=tuple(s for g in groups for s in g.lands),
            grid_spec=pltpu.PrefetchScalarGridSpec(
                num_scalar_prefetch=1, grid=(2,), in_specs=[s for g in groups for s in g.in_specs],
                out_specs=tuple(s for g in groups for s in g.out_specs)),
            compiler_params=_cp("arbitrary"),
        )(me1, *grads))

    return ex, prep


def gather_prep(l, me1, w_in, w_out_conv, w_out_rnn, w_o, ffn_w1, ffn_w2, rg_w, small):
    hd, hw = D // 2, W2S // 2

    def body(me_ref, win, woc, wor, wo, w1, w2, rgw, sm, c_win, c_w3, c_w1, c_w2, c_rgw, c_sm,
             o_win, o_w3, o_w1, o_w2, o_rgw, o_sm):
        a = win[...].astype(BF16)
        c_win[...] = a
        o_win[...] = a
        for k, r in enumerate((woc, wor, wo)):
            b = r[...].astype(BF16)
            c_w3[k] = b
            o_w3[k] = b
        for src, comp, own in ((w1, c_w1, o_w1), (w2, c_w2, o_w2), (rgw, c_rgw, o_rgw)):
            b = src[...].astype(BF16)
            comp[...] = b
            own[...] = b
        c_sm[...] = sm[...]
        o_sm[...] = sm[...]

    three = pl.BlockSpec((None, 64, D), lambda i, me: (l, i, 0))
    in_specs = [pl.BlockSpec((None, hd, WIN_S), lambda i, me: (l, i, 0)), three, three, three,
                pl.BlockSpec((None, 2, hd, FS), lambda i, me: (l, 0, i, 0)),
                pl.BlockSpec((None, 2, hw, D), lambda i, me: (l, 0, i, 0)),
                pl.BlockSpec((None, 2, HEADS, 16, HD), lambda i, me: (l, 0, 0, i, 0)),
                pl.BlockSpec((None, 8, 128), lambda i, me: (l, i, 0))]
    out_specs = (pl.BlockSpec((hd, WIN_S), lambda i, me: (i, 0)), pl.BlockSpec((3, 64, D), lambda i, me: (0, i, 0)),
                 pl.BlockSpec((2, hd, FS), lambda i, me: (0, i, 0)), pl.BlockSpec((2, hw, D), lambda i, me: (0, i, 0)),
                 pl.BlockSpec((2, HEADS, 16, HD), lambda i, me: (0, 0, i, 0)), pl.BlockSpec((8, 128), lambda i, me: (i, 0)),
                 pl.BlockSpec((hd, WIN_S), lambda i, me: (i, me[0])),
                 pl.BlockSpec((3, 64, D), lambda i, me: (0, 2 * me[0] + i, 0)),
                 pl.BlockSpec((2, None, hd, FS), lambda i, me: (0, me[0], i, 0)),
                 pl.BlockSpec((2, hw, D), lambda i, me: (0, 2 * me[0] + i, 0)),
                 pl.BlockSpec((2, HEADS, 16, HD), lambda i, me: (0, 0, 2 * me[0] + i, 0)),
                 pl.BlockSpec((8, 128), lambda i, me: (i, me[0])))
    compact = (SDS((D, WIN_S), BF16), SDS((3, 128, D), BF16), SDS((2, D, FS), BF16), SDS((2, W2S, D), BF16),
               SDS((2, HEADS, 32, HD), BF16), SDS((SMALL_ROWS, 128), F32))
    res = pl.pallas_call(
        body, name=f"gather_prep{l}", out_shape=compact + GATHER_SHAPES,
        grid_spec=pltpu.PrefetchScalarGridSpec(num_scalar_prefetch=1, grid=(2,), in_specs=in_specs, out_specs=out_specs),
        compiler_params=_cp("arbitrary"),
    )(me1, w_in, w_out_conv, w_out_rnn, w_o, ffn_w1, ffn_w2, rg_w, small)
    return list(res[:6]), list(res[6:])


def gather_small_grads(g_sharded, g_replicated):
    def body(ga, gb, la, lb, send_sems, recv_sems, loc_sems):
        _all_to_all_copies(lambda p: [ga, gb], lambda p: [la.at[p], lb.at[p]],
                           [(la.at[pl.ds(0, 7)], 0), (lb.at[pl.ds(0, 7)], 1)], send_sems, recv_sems, loc_sems)

    return pl.pallas_call(
        body, name="gather_small_grads",
        out_shape=(SDS((N_DEV,) + g_sharded.shape, F32), SDS((N_DEV,) + g_replicated.shape, F32)),
        in_specs=[ANY, ANY], out_specs=(ANY, ANY),
        scratch_shapes=[pltpu.SemaphoreType.DMA((2,)), pltpu.SemaphoreType.DMA((2,)), pltpu.SemaphoreType.DMA((2,))],
        compiler_params=pltpu.CompilerParams(has_side_effects=True),
    )(g_sharded, g_replicated)


def ffn_up(xb, w1, f):
    t = xb.shape[0]
    tm = _row_tile(t, 1024)

    def body(x_ref, wg_ref, wu_ref, g_ref, u_ref, a_ref):
        x = x_ref[...]
        g = _dot(x, wg_ref[...], NN)
        u = _dot(x, wu_ref[...], NN)
        g_ref[...] = g.astype(BF16)
        u_ref[...] = u.astype(BF16)
        a_ref[...] = (g * _sigmoid_t(g) * u).astype(BF16)

    out = pl.BlockSpec((None, tm, FS), lambda j, i: (j, i, 0))
    return pl.pallas_call(
        body, name=f"ffn_up{f}", grid=(4, t // tm),
        in_specs=[pl.BlockSpec((tm, D), lambda j, i: (i, 0)),
                  pl.BlockSpec((None, None, D, FS), lambda j, i: (f, j, 0, 0)),
                  pl.BlockSpec((None, None, D, FS), lambda j, i: (f, j + 4, 0, 0))],
        out_specs=(out, out, out), out_shape=(SDS((4, t, FS), BF16),) * 3,
        compiler_params=_cp("parallel", "parallel"),
    )(xb, w1, w1)


def ffn_down_ln(a, w2, f, xf, small, s):
    t = xf.shape[0]
    tm = _row_tile(t, 512)

    def body(a_ref, w_ref, x_ref, sm_ref, xo_ref, xb_ref, xh_ref, rs_ref):
        acc = _dot(a_ref[0], w_ref[0:FS, :], NN)
        for j in range(1, 4):
            acc = acc + _dot(a_ref[j], w_ref[j * FS:(j + 1) * FS, :], NN)
        z = ALPHA * x_ref[...] + 0.5 * acc
        y, xh, rstd = _ln_fwd(z, sm_ref[R_LNG + s:R_LNG + s + 1, :], sm_ref[R_LNB + s:R_LNB + s + 1, :])
        xo_ref[...] = y
        xb_ref[...] = y.astype(BF16)
        xh_ref[...] = xh
        rs_ref[...] = rstd

    row = pl.BlockSpec((tm, D), lambda i: (i, 0))
    return pl.pallas_call(
        body, name=f"ffn_down_ln{f}", grid=(t // tm,),
        in_specs=[pl.BlockSpec((4, tm, FS), lambda i: (0, i, 0)),
                  pl.BlockSpec((None, D_FF, D), lambda i: (f, 0, 0)),
                  row, pl.BlockSpec((SMALL_ROWS, D), lambda i: (0, 0))],
        out_specs=(row, row, row, pl.BlockSpec((tm, 1), lambda i: (i, 0))),
        out_shape=(SDS((t, D), F32), SDS((t, D), BF16), SDS((t, D), F32), SDS((t, 1), F32)),
        compiler_params=_cp("parallel"),
    )(a, w2, xf, small)


def ffn_bwd_gates(dxn, xh, rstd, small, s, w2, f, g, u):
    t = dxn.shape[0]
    tm = _row_tile(t, 256)

    def body(dy_ref, xh_ref, rs_ref, sm_ref, w_ref, g_ref, u_ref, dz_ref, df_ref, dgu_ref, dln_ref):
        i = pl.program_id(0)
        dy = dy_ref[...]
        xhat = xh_ref[...]
        dz = _ln_bwd(dy, xhat, rs_ref[...], sm_ref[R_LNG + s:R_LNG + s + 1, :])

        @pl.when(i == 0)
        def _():
            dln_ref[...] = jnp.zeros_like(dln_ref)

        dln_ref[0:1, :] += jnp.sum(dy * xhat, axis=0, keepdims=True)
        dln_ref[1:2, :] += jnp.sum(dy, axis=0, keepdims=True)
        dz_ref[...] = dz
        df = (0.5 * dz).astype(BF16)
        df_ref[...] = df
        for j in range(4):
            da = _dot(df, w_ref[j * FS:(j + 1) * FS, :], NT)
            gg = g_ref[j].astype(F32)
            uu = u_ref[j].astype(F32)
            sg = _sigmoid_t(gg)
            dgu_ref[j] = (da * uu * (sg * (1.0 + gg * (1.0 - sg)))).astype(BF16)
            dgu_ref[j + 4] = (da * (gg * sg)).astype(BF16)

    row = pl.BlockSpec((tm, D), lambda i: (i, 0))
    gu = pl.BlockSpec((4, tm, FS), lambda i: (0, i, 0))
    return pl.pallas_call(
        body, name=f"ffn_bwd_gates{f}", grid=(t // tm,),
        in_specs=[row, row, pl.BlockSpec((tm, 1), lambda i: (i, 0)), pl.BlockSpec((SMALL_ROWS, D), lambda i: (0, 0)),
                  pl.BlockSpec((None, D_FF, D), lambda i: (f, 0, 0)), gu, gu],
        out_specs=(row, row, pl.BlockSpec((8, tm, FS), lambda i: (0, i, 0)), pl.BlockSpec((2, D), lambda i: (0, 0))),
        out_shape=(SDS((t, D), F32), SDS((t, D), BF16), SDS((8, t, FS), BF16), SDS((2, D), F32)),
        compiler_params=_cp("arbitrary"),
    )(dxn, xh, rstd, small, w2, g, u)


def ffn_bwd_dx(dz, dgu, w1, f):
    t = dz.shape[0]
    tm = _row_tile(t, 1024)

    def body(dz_ref, d_ref, w_ref, dx_ref, acc):
        k = pl.program_id(1)

        @pl.when(k == 0)
        def _():
            acc[...] = ALPHA * dz_ref[...]

        acc[...] += _dot(d_ref[...], w_ref[...], NT)

        @pl.when(k == 7)
        def _():
            dx_ref[...] = acc[...]

    row = pl.BlockSpec((tm, D), lambda i, k: (i, 0))
    return pl.pallas_call(
        body, name=f"ffn_bwd_dx{f}", grid=(t // tm, 8),
        in_specs=[row, pl.BlockSpec((None, tm, FS), lambda i, k: (k, i, 0)),
                  pl.BlockSpec((None, None, D, FS), lambda i, k: (f, k, 0, 0))],
        out_specs=row, out_shape=SDS((t, D), F32),
        scratch_shapes=[pltpu.VMEM((tm, D), F32)],
        compiler_params=_cp("parallel", "arbitrary"),
    )(dz, dgu, w1)


def _mm_tn(name, a, a_spec, b, b_spec, out_sds, out_spec, grid):
    def body(a_ref, b_ref, o_ref):
        o_ref[...] = _dot(a_ref[...], b_ref[...], TN).astype(o_ref.dtype)

    return pl.pallas_call(
        body, name=name, grid=grid, in_specs=[a_spec, b_spec], out_specs=out_spec, out_shape=out_sds,
        compiler_params=_cp(*(["parallel"] * len(grid))),
    )(a, b)


def ffn_dw1(xb, dgu, f):
    t = xb.shape[0]
    return _mm_tn(f"ffn_dw1_{f}", xb, pl.BlockSpec((t, D), lambda j: (0, 0)),
                  dgu, pl.BlockSpec((None, t, FS), lambda j: (j, 0, 0)),
                  SDS((8, D, FS), BF16), pl.BlockSpec((None, D, FS), lambda j: (j, 0, 0)), (8,))


def ffn_dw2(a, df, f):
    t = df.shape[0]
    return _mm_tn(f"ffn_dw2_{f}", a, pl.BlockSpec((None, t, FS), lambda j: (j, 0, 0)),
                  df, pl.BlockSpec((t, D), lambda j: (0, 0)),
                  SDS((D_FF, D), BF16), pl.BlockSpec((FS, D), lambda j: (j, 0)), (4,))


def mm_tn_square(name, a, b):
    t = a.shape[0]
    return _mm_tn(name, a, pl.BlockSpec((t, 512), lambda i: (0, i)),
                  b, pl.BlockSpec((t, D), lambda i: (0, 0)),
                  SDS((D, D), BF16), pl.BlockSpec((512, D), lambda i: (i, 0)), (2,))


def mix_proj(xb, win):
    t = xb.shape[0]
    tm = _row_tile(t, 1024)

    def body(x_ref, w_ref, o_ref):
        o_ref[...] = _dot(x_ref[...], w_ref[...], NN).astype(BF16)

    return pl.pallas_call(
        body, name="mix_proj", grid=(7, t // tm),
        in_specs=[pl.BlockSpec((tm, D), lambda n, i: (i, 0)), pl.BlockSpec((D, D), lambda n, i: (0, n))],
        out_specs=pl.BlockSpec((tm, D), lambda n, i: (i, n)), out_shape=SDS((t, D_IN), BF16),
        compiler_params=_cp("parallel", "parallel"),
    )(xb, win)


def _pcol(tm, k):
    return pl.BlockSpec((tm, D), lambda i: (i, k))


def _prev_halo(tm, k):
    return pl.BlockSpec((8, D), lambda i: (jnp.maximum(i * (tm // 8) - 1, 0), k))


def _prev_halo16(tm, k):
    return pl.BlockSpec((16, D), lambda i: (jnp.maximum(i * (tm // 16) - 1, 0), k))


def _next_halo(tm, t, k):
    return pl.BlockSpec((8, D), lambda i: (jnp.minimum((i + 1) * (tm // 8), t // 8 - 1), k))


def _full(shape):
    nd = len(shape)
    return pl.BlockSpec(shape, lambda i: (0,) * nd)


def mix_pre(p, small, rcb, ap, rgw):
    t = p.shape[0]
    tm = _row_tile(t, 256)

    def body(bg_ref, cg_ref, v_ref, xr_ref, cgh_ref, vh_ref, xrh_ref, sm_ref, rcb_ref, ap_ref, rgw_ref,
             ca_ref, pa_ref, xc_ref, xcb_ref, gi_ref, gr_ref, h_ref, ext1, ext2, a_s, b_s, carry):
        i = pl.program_id(0)
        first = i == 0
        cv = cg_ref[...].astype(F32) * v_ref[...].astype(F32)
        ext1[0:16, :] = jnp.where(first, 0.0, cgh_ref[...].astype(F32) * vh_ref[...].astype(F32))
        ext1[16:, :] = cv
        xr = xr_ref[...].astype(F32)
        ext2[0:16, :] = jnp.where(first, 0.0, xrh_ref[...].astype(F32))
        ext2[16:, :] = xr
        ca = (sm_ref[R_SC:R_SC + 1, :] * ext1[pl.ds(14, tm), :] + sm_ref[R_SC + 1:R_SC + 2, :] * ext1[pl.ds(15, tm), :]
              + sm_ref[R_SC + 2:R_SC + 3, :] * cv)
        ca_ref[...] = ca.astype(BF16)
        pa_ref[...] = (bg_ref[...].astype(F32) * ca).astype(BF16)
        xc = (sm_ref[R_RC:R_RC + 1, :] * ext2[pl.ds(13, tm), :] + sm_ref[R_RC + 1:R_RC + 2, :] * ext2[pl.ds(14, tm), :]
              + sm_ref[R_RC + 2:R_RC + 3, :] * ext2[pl.ds(15, tm), :] + sm_ref[R_RC + 3:R_RC + 4, :] * xr
              + rcb_ref[...])
        xc_ref[...] = xc
        xcb = xc.astype(BF16)
        xcb_ref[...] = xcb
        g0, g1 = [], []
        for h in range(HEADS):
            xh = xcb[:, h * HD:(h + 1) * HD]
            g0.append(_dot(xh, rgw_ref[0, h], NN))
            g1.append(_dot(xh, rgw_ref[1, h], NN))
        gi = _sigmoid(jnp.concatenate(g0, axis=1) + sm_ref[R_RGB:R_RGB + 1, :])
        gr = _sigmoid(jnp.concatenate(g1, axis=1) + sm_ref[R_RGB + 1:R_RGB + 2, :])
        gi_ref[...] = gi
        gr_ref[...] = gr
        la = (-LRU_C) * gr * _softplus(-ap_ref[...])
        a_s[...] = jnp.exp(la)
        row = lax.broadcasted_iota(jnp.int32, (tm, D), 0) + i * tm
        mult = jnp.where(row == 0, 1.0, jnp.sqrt(_neg_expm1(2.0 * la)))
        b_s[...] = xc * gi * mult

        @pl.when(first)
        def _():
            carry[...] = jnp.zeros_like(carry)

        carry[...] = _scan_tile(a_s, b_s, a_s, carry[...], tm, reverse=False)
        h_ref[...] = a_s[...].astype(BF16)

    row = pl.BlockSpec((tm, D), lambda i: (i, 0))
    f32o, b16o = SDS((t, D), F32), SDS((t, D), BF16)
    ext, tile = pltpu.VMEM((tm + 16, D), F32), pltpu.VMEM((tm, D), F32)
    return pl.pallas_call(
        body, name="mix_pre", grid=(t // tm,),
        in_specs=[_pcol(tm, 0), _pcol(tm, 1), _pcol(tm, 2), _pcol(tm, 3),
                  _prev_halo16(tm, 1), _prev_halo16(tm, 2), _prev_halo16(tm, 3),
                  _full((SMALL_ROWS, D)), _full((1, D)), _full((1, D)), _full((2, HEADS, HD, HD))],
        out_specs=(row,) * 7, out_shape=(b16o, b16o, f32o, b16o, f32o, f32o, b16o),
        scratch_shapes=[ext, ext, tile, tile, pltpu.VMEM((8, D), F32)],
        compiler_params=_cp("arbitrary"),
    )(p, p, p, p, p, p, p, small, rcb, ap, rgw)


def _scan_tile(a_ref, b_ref, o_ref, carry, tm, reverse):
    width = a_ref.shape[1]
    ng = tm // 8
    row8 = lax.broadcasted_iota(jnp.int32, (8, width), 0)

    def step(g, c):
        r = pl.multiple_of((ng - 1 - g if reverse else g) * 8, 8)
        aa = a_ref[pl.ds(r, 8), :]
        bb = b_ref[pl.ds(r, 8), :]
        for s in (1, 2, 4):
            if reverse:
                keep, shift = row8 < 8 - s, 8 - s
            else:
                keep, shift = row8 >= s, s
            a_sh = jnp.where(keep, pltpu.roll(aa, shift, 0), 1.0)
            b_sh = jnp.where(keep, pltpu.roll(bb, shift, 0), 0.0)
            bb = aa * b_sh + bb
            aa = aa * a_sh
        o = aa * c + bb
        o_ref[pl.ds(r, 8), :] = o
        edge = o[0:1, :] if reverse else o[7:8, :]
        return jnp.broadcast_to(edge, (8, width))

    return lax.fori_loop(0, ng, step, carry)


def mix_out(pa, h, p, bm, w3, xf, small):
    t = xf.shape[0]
    tm = _row_tile(t, 256)

    def body(pa_ref, h_ref, yr_ref, gla_ref, glb_ref, bma_ref, bmb_ref, w_ref, x_ref, sm_ref,
             pb_ref, ya_ref, yb_ref, m_ref, xo_ref, xb_ref, xh_ref, rs_ref):
        ge, _ = _gelu(yr_ref[...].astype(F32))
        pb = (h_ref[...].astype(F32) * ge).astype(BF16)
        pb_ref[...] = pb
        ya = _dot(pa_ref[...], w_ref[0], NN)
        yb = _dot(pb, w_ref[1], NN)
        ya_ref[...] = ya.astype(BF16)
        yb_ref[...] = yb.astype(BF16)
        ga = _sigmoid_t(gla_ref[...].astype(F32) + bma_ref[...])
        gb = _sigmoid_t(glb_ref[...].astype(F32) + bmb_ref[...])
        m = (ga * ya + gb * yb).astype(BF16)
        m_ref[...] = m
        z = ALPHA * x_ref[...] + _dot(m, w_ref[2], NN)
        y, xh, rstd = _ln_fwd(z, sm_ref[R_LNG + 1:R_LNG + 2, :], sm_ref[R_LNB + 1:R_LNB + 2, :])
        xo_ref[...] = y
        xb_ref[...] = y.astype(BF16)
        xh_ref[...] = xh
        rs_ref[...] = rstd

    row = pl.BlockSpec((tm, D), lambda i: (i, 0))
    f32o, b16o = SDS((t, D), F32), SDS((t, D), BF16)
    return pl.pallas_call(
        body, name="mix_out", grid=(t // tm,),
        in_specs=[row, row, _pcol(tm, 4), _pcol(tm, 5), _pcol(tm, 6),
                  pl.BlockSpec((1, D), lambda i: (0, 0)), pl.BlockSpec((1, D), lambda i: (0, 1)),
                  _full((3, D, D)), row, _full((SMALL_ROWS, D))],
        out_specs=(row,) * 7 + (pl.BlockSpec((tm, 1), lambda i: (i, 0)),),
        out_shape=(b16o, b16o, b16o, b16o, f32o, b16o, f32o, SDS((t, 1), F32)),
        compiler_params=_cp("parallel"),
    )(pa, h, p, p, p, bm, bm, w3, xf, small)


def mixb_head(dxn, xh, rstd, small, w3, p, bm, ya, yb, ca, h):
    t = dxn.shape[0]
    tm = _row_tile(t, 256)

    def body(dy_ref, xh_ref, rs_ref, sm_ref, w_ref, bg_ref, yr_ref, gla_ref, glb_ref, bma_ref, bmb_ref,
             ya_ref, yb_ref, ca_ref, h_ref,
             dz_ref, dzb_ref, dya_ref, dyb_ref, dbg_ref, dca_ref, dh_ref, dphi_ref, dln_ref, dbm_ref):
        i = pl.program_id(0)
        dy = dy_ref[...]
        xhat = xh_ref[...]
        dz = _ln_bwd(dy, xhat, rs_ref[...], sm_ref[R_LNG + 1:R_LNG + 2, :])

        @pl.when(i == 0)
        def _():
            dln_ref[...] = jnp.zeros_like(dln_ref)
            dbm_ref[...] = jnp.zeros_like(dbm_ref)

        dln_ref[0:1, :] += jnp.sum(dy * xhat, axis=0, keepdims=True)
        dln_ref[1:2, :] += jnp.sum(dy, axis=0, keepdims=True)
        dz_ref[...] = dz
        dzb = dz.astype(BF16)
        dzb_ref[...] = dzb
        dm = _dot(dzb, w_ref[2], NT)
        ga = _sigmoid_t(gla_ref[...].astype(F32) + bma_ref[...])
        gb = _sigmoid_t(glb_ref[...].astype(F32) + bmb_ref[...])
        dya = (dm * ga).astype(BF16)
        dyb = (dm * gb).astype(BF16)
        dya_ref[...] = dya
        dyb_ref[...] = dyb
        dgla = dm * ya_ref[...].astype(F32) * ga * (1.0 - ga)
        dglb = dm * yb_ref[...].astype(F32) * gb * (1.0 - gb)
        dbm_ref[0:1, :] += jnp.sum(dgla, axis=0, keepdims=True)
        dbm_ref[1:2, :] += jnp.sum(dglb, axis=0, keepdims=True)
        dphi_ref[:, D:2 * D] = dgla.astype(BF16)
        dphi_ref[:, 2 * D:3 * D] = dglb.astype(BF16)
        dpa = _dot(dya, w_ref[0], NT)
        dpb = _dot(dyb, w_ref[1], NT)
        dbg_ref[...] = (dpa * ca_ref[...].astype(F32)).astype(BF16)
        dca_ref[...] = dpa * bg_ref[...].astype(F32)
        yr = yr_ref[...].astype(F32)
        ge, th = _gelu(yr)
        dh_ref[...] = (dpb * ge).astype(BF16)
        dphi_ref[:, 0:D] = (dpb * h_ref[...].astype(F32) * _gelu_grad(yr, th)).astype(BF16)

    row = pl.BlockSpec((tm, D), lambda i: (i, 0))
    f32o, b16o = SDS((t, D), F32), SDS((t, D), BF16)
    acc2 = pl.BlockSpec((2, D), lambda i: (0, 0))
    return pl.pallas_call(
        body, name="mixb_head", grid=(t // tm,),
        in_specs=[row, row, pl.BlockSpec((tm, 1), lambda i: (i, 0)), _full((SMALL_ROWS, D)), _full((3, D, D)),
                  _pcol(tm, 0), _pcol(tm, 4), _pcol(tm, 5), _pcol(tm, 6),
                  pl.BlockSpec((1, D), lambda i: (0, 0)), pl.BlockSpec((1, D), lambda i: (0, 1)),
                  row, row, row, row],
        out_specs=(row,) * 7 + (pl.BlockSpec((tm, 3 * D), lambda i: (i, 0)), acc2, acc2),
        out_shape=(f32o, b16o, b16o, b16o, b16o, f32o, b16o, SDS((t, 3 * D), BF16), SDS((2, D), F32), SDS((2, D), F32)),
        compiler_params=_cp("arbitrary"),
    )(dxn, xh, rstd, small, w3, p, p, p, p, bm, bm, ya, yb, ca, h)


def mixb_rec(dh, gr, gi, h, xc, ap, rgw):
    t = dh.shape[0]
    tm = _row_tile(t, 256)
    nt = t // tm

    def body(dh_ref, gr_ref, gi_ref, h_ref, hh_ref, xc_ref, ap_ref, rgw_ref, dg_ref, dxc_ref, red_ref,
             ext, ext_a, c_s, lam_s, lam_c, a_c):
        i = nt - 1 - pl.program_id(0)
        first = i == 0

        @pl.when(pl.program_id(0) == 0)
        def _():
            red_ref[...] = jnp.zeros_like(red_ref)
            lam_c[...] = jnp.zeros_like(lam_c)
            a_c[...] = jnp.zeros_like(a_c)

        ext[0:8, :] = jnp.where(first, 0.0, hh_ref[...].astype(F32)[8:16, :])
        ext[8:, :] = h_ref[...].astype(F32)
        hprev = ext[pl.ds(7, tm), :]
        lam_s[...] = dh_ref[...].astype(F32)
        gr = gr_ref[...]
        gi = gi_ref[...]
        xc = xc_ref[...]
        ap = ap_ref[...]
        sp = _softplus(-ap)
        la = (-LRU_C) * gr * sp
        a = jnp.exp(la)
        ext_a[0:tm, :] = a
        ext_a[tm:, :] = a_c[...]
        c_s[...] = ext_a[pl.ds(1, tm), :]
        lam_c[...] = _scan_tile(c_s, lam_s, lam_s, lam_c[...], tm, reverse=True)
        a_c[...] = jnp.broadcast_to(a[0:1, :], (8, D))
        lam = lam_s[...]
        row = lax.broadcasted_iota(jnp.int32, (tm, D), 0) + i * tm
        start = row == 0
        mult = jnp.where(start, 1.0, jnp.sqrt(_neg_expm1(2.0 * la)))
        dmult = jnp.where(start, 0.0, lam * xc * gi)
        dla = lam * hprev * a - dmult * a * a / mult
        dg1 = (-LRU_C) * sp * dla * gr * (1.0 - gr)
        dg0 = lam * xc * mult * gi * (1.0 - gi)
        dsp = jnp.sum((-LRU_C) * gr * dla, axis=0, keepdims=True)
        red_ref[0:1, :] += jnp.sum(dg0, axis=0, keepdims=True)
        red_ref[1:2, :] += jnp.sum(dg1, axis=0, keepdims=True)
        red_ref[2:3, :] += -dsp * _sigmoid(-ap)
        dg0b = dg0.astype(BF16)
        dg1b = dg1.astype(BF16)
        dg_ref[0] = dg0b
        dg_ref[1] = dg1b
        parts = []
        for hd in range(HEADS):
            sl = slice(hd * HD, (hd + 1) * HD)
            parts.append(_dot(dg0b[:, sl], rgw_ref[0, hd], NT) + _dot(dg1b[:, sl], rgw_ref[1, hd], NT))
        dxc = lam * gi * mult + jnp.concatenate(parts, axis=1)
        dxc_ref[...] = dxc
        red_ref[3:4, :] += jnp.sum(dxc, axis=0, keepdims=True)

    row = pl.BlockSpec((tm, D), lambda i: (nt - 1 - i, 0))
    halo = pl.BlockSpec((16, D), lambda i: (jnp.maximum((nt - 1 - i) * (tm // 16) - 1, 0), 0))
    ext, tile, edge = pltpu.VMEM((tm + 8, D), F32), pltpu.VMEM((tm, D), F32), pltpu.VMEM((8, D), F32)
    return pl.pallas_call(
        body, name="mixb_rec", grid=(nt,),
        in_specs=[row, row, row, row, halo, row, _full((1, D)), _full((2, HEADS, HD, HD))],
        out_specs=(pl.BlockSpec((2, tm, D), lambda i: (0, nt - 1 - i, 0)), row, pl.BlockSpec((8, D), lambda i: (0, 0))),
        out_shape=(SDS((2, t, D), BF16), SDS((t, D), F32), SDS((8, D), F32)),
        scratch_shapes=[ext, ext, tile, tile, edge, edge],
        compiler_params=_cp("arbitrary"),
    )(dh, gr, gi, h, h, xc, ap, rgw)


def mixb_conv(dca, dxc, dbg, p, small):
    t = dca.shape[0]
    tm = _row_tile(t, 256)
    nt = t // tm

    def body(dca_ref, dcan_ref, dxc_ref, dxcn_ref, dbg_ref, cg_ref, v_ref, xr_ref, cgh_ref, vh_ref, xrh_ref, sm_ref,
             dplo_ref, dxr_ref, red_ref, e_dca, e_dxc, e_cv, e_xr):
        i = pl.program_id(0)
        first = i == 0
        last = i == nt - 1

        @pl.when(first)
        def _():
            red_ref[...] = jnp.zeros_like(red_ref)

        dca = dca_ref[...]
        dxc = dxc_ref[...]
        e_dca[0:tm, :] = dca
        e_dca[tm:, :] = jnp.where(last, 0.0, dcan_ref[...])
        e_dxc[0:tm, :] = dxc
        e_dxc[tm:, :] = jnp.where(last, 0.0, dxcn_ref[...])
        cg = cg_ref[...].astype(F32)
        v = v_ref[...].astype(F32)
        xr = xr_ref[...].astype(F32)
        e_cv[0:8, :] = jnp.where(first, 0.0, (cgh_ref[...].astype(F32) * vh_ref[...].astype(F32))[8:16, :])
        e_cv[8:, :] = cg * v
        e_xr[0:8, :] = jnp.where(first, 0.0, xrh_ref[...].astype(F32)[8:16, :])
        e_xr[8:, :] = xr
        dcv = (sm_ref[R_SC + 2:R_SC + 3, :] * dca + sm_ref[R_SC + 1:R_SC + 2, :] * e_dca[pl.ds(1, tm), :]
               + sm_ref[R_SC:R_SC + 1, :] * e_dca[pl.ds(2, tm), :])
        dplo_ref[:, 0:D] = dbg_ref[...]
        dplo_ref[:, D:2 * D] = (dcv * v).astype(BF16)
        dplo_ref[:, 2 * D:3 * D] = (dcv * cg).astype(BF16)
        dxr = (sm_ref[R_RC + 3:R_RC + 4, :] * dxc + sm_ref[R_RC + 2:R_RC + 3, :] * e_dxc[pl.ds(1, tm), :]
               + sm_ref[R_RC + 1:R_RC + 2, :] * e_dxc[pl.ds(2, tm), :] + sm_ref[R_RC:R_RC + 1, :] * e_dxc[pl.ds(3, tm), :])
        dxr_ref[...] = dxr.astype(BF16)
        for k in range(3):
            red_ref[R_SC + k:R_SC + k + 1, :] += jnp.sum(dca * e_cv[pl.ds(6 + k, tm), :], axis=0, keepdims=True)
        for k in range(4):
            red_ref[R_RC + k:R_RC + k + 1, :] += jnp.sum(dxc * e_xr[pl.ds(5 + k, tm), :], axis=0, keepdims=True)

    row = pl.BlockSpec((tm, D), lambda i: (i, 0))
    ext = pltpu.VMEM((tm + 8, D), F32)
    return pl.pallas_call(
        body, name="mixb_conv", grid=(nt,),
        in_specs=[row, _next_halo(tm, t, 0), row, _next_halo(tm, t, 0), row,
                  _pcol(tm, 1), _pcol(tm, 2), _pcol(tm, 3), _prev_halo16(tm, 1), _prev_halo16(tm, 2), _prev_halo16(tm, 3),
                  _full((SMALL_ROWS, D))],
        out_specs=(pl.BlockSpec((tm, 3 * D), lambda i: (i, 0)), row, pl.BlockSpec((8, D), lambda i: (0, 0))),
        out_shape=(SDS((t, 3 * D), BF16), SDS((t, D), BF16), SDS((8, D), F32)),
        scratch_shapes=[ext, ext, ext, ext],
        compiler_params=_cp("arbitrary"),
    )(dca, dca, dxc, dxc, dbg, p, p, p, p, p, p, small)


def mixb_dx(dz, dplo, dxr, dphi, win):
    t = dz.shape[0]
    tm = _row_tile(t, 1024)

    def body(dz_ref, lo_ref, xr_ref, hi_ref, w_ref, dx_ref, acc):
        k = pl.program_id(1)

        @pl.when(k == 0)
        def _():
            acc[...] = ALPHA * dz_ref[...]

        @pl.when(k < 3)
        def _():
            acc[...] += _dot(lo_ref[...], w_ref[...], NT)

        @pl.when(k == 3)
        def _():
            acc[...] += _dot(xr_ref[...], w_ref[...], NT)

        @pl.when(k > 3)
        def _():
            acc[...] += _dot(hi_ref[...], w_ref[...], NT)

        @pl.when(k == 6)
        def _():
            dx_ref[...] = acc[...]

    row = pl.BlockSpec((tm, D), lambda i, k: (i, 0))
    return pl.pallas_call(
        body, name="mixb_dx", grid=(t // tm, 7),
        in_specs=[row, pl.BlockSpec((tm, D), lambda i, k: (i, jnp.minimum(k, 2))), row,
                  pl.BlockSpec((tm, D), lambda i, k: (i, jnp.clip(k - 4, 0, 2))),
                  pl.BlockSpec((D, D), lambda i, k: (0, k))],
        out_specs=row, out_shape=SDS((t, D), F32),
        scratch_shapes=[pltpu.VMEM((tm, D), F32)],
        compiler_params=_cp("parallel", "arbitrary"),
    )(dz, dplo, dxr, dphi, win)


def mixb_dwin(xb, dplo, dxr, dphi):
    t = xb.shape[0]
    tk = _row_tile(t, 2048)
    nk = t // tk

    def body(x_ref, lo_ref, xr_ref, hi_ref, o_ref, acc):
        n = pl.program_id(0)
        k = pl.program_id(1)

        @pl.when(k == 0)
        def _():
            acc[...] = jnp.zeros_like(acc)

        @pl.when(n < 3)
        def _():
            acc[...] += _dot(x_ref[...], lo_ref[...], TN)

        @pl.when(n == 3)
        def _():
            acc[...] += _dot(x_ref[...], xr_ref[...], TN)

        @pl.when(n > 3)
        def _():
            acc[...] += _dot(x_ref[...], hi_ref[...], TN)

        @pl.when(k == nk - 1)
        def _():
            o_ref[...] = acc[...].astype(BF16)

    return pl.pallas_call(
        body, name="mixb_dwin", grid=(7, nk),
        in_specs=[pl.BlockSpec((tk, D), lambda n, k: (k, 0)),
                  pl.BlockSpec((tk, D), lambda n, k: (jnp.where(n < 3, k, 0), jnp.minimum(n, 2))),
                  pl.BlockSpec((tk, D), lambda n, k: (jnp.where(n == 3, k, 0), 0)),
                  pl.BlockSpec((tk, D), lambda n, k: (jnp.where(n > 3, k, 0), jnp.clip(n - 4, 0, 2)))],
        out_specs=pl.BlockSpec((D, D), lambda n, k: (0, n)), out_shape=SDS((D, D_IN), BF16),
        scratch_shapes=[pltpu.VMEM((D, D), F32)],
        compiler_params=_cp("parallel", "arbitrary"),
    )(xb, dplo, dxr, dphi)


def mixb_drgw(xcb, dg):
    t = xcb.shape[0]
    return _mm_tn("mixb_drgw", xcb, pl.BlockSpec((t, HD), lambda g, h: (0, h)),
                  dg, pl.BlockSpec((None, t, HD), lambda g, h: (g, 0, h)),
                  SDS((2, HEADS, HD, HD), BF16), pl.BlockSpec((None, None, HD, HD), lambda g, h: (g, h, 0, 0)),
                  (2, HEADS))


def loss_head(y, tgt):
    t = y.shape[0]
    tm = _row_tile(t, 512)

    def body(y_ref, t_ref, dy_ref, l_ref):
        i = pl.program_id(0)
        e = y_ref[...] - t_ref[...]
        dy_ref[...] = e * (1.0 / D)

        @pl.when(i == 0)
        def _():
            l_ref[...] = jnp.zeros_like(l_ref)

        l_ref[...] += 0.5 * jnp.sum(jnp.mean(e * e, axis=-1, keepdims=True), axis=0, keepdims=True)

    row = pl.BlockSpec((tm, D), lambda i: (i, 0))
    return pl.pallas_call(
        body, name="loss_head", grid=(t // tm,), in_specs=[row, row],
        out_specs=(row, pl.BlockSpec((8, 128), lambda i: (0, 0))),
        out_shape=(SDS((t, D), F32), SDS((8, 128), F32)),
        compiler_params=_cp("arbitrary"),
    )(y, tgt)


def _adamw(w, g, m, v):
    m = ADAM_B1 * m + (1.0 - ADAM_B1) * g
    v = ADAM_B2 * v + (1.0 - ADAM_B2) * (g * g)
    m_hat = m / (1.0 - ADAM_B1 ** ADAM_STEP)
    v_hat = v / (1.0 - ADAM_B2 ** ADAM_STEP)
    delta = -ADAM_LR * (m_hat / (jnp.sqrt(v_hat) + ADAM_EPS) + ADAM_WD * w)
    return delta, m, v


def adam_big(name, w, m, v, parts, rows, lanes, blk_off, tr, l0, prev=None):
    nr = rows // tr
    nl = len(parts)

    def body(w_ref, m_ref, v_ref, *rest):
        g_ref, d_ref, mo_ref, vo_ref = rest[-4:]
        l = pl.program_id(0)
        for ll in range(nl):
            pr = rest[ll]

            @pl.when(l == ll)
            def _():
                g = pr[0].astype(F32)
                for s in range(1, N_DEV):
                    g = g + pr[s].astype(F32)
                g_ref[...] = g
                d, mn, vn = _adamw(w_ref[...], g, m_ref[...], v_ref[...])
                d_ref[...] = d
                mo_ref[...] = mn
                vo_ref[...] = vn

    blk = pl.BlockSpec((None, tr, lanes), lambda l, r: (l + l0, r, 0))

    def part_spec(ll):
        return pl.BlockSpec((N_DEV, tr, lanes), lambda l, r: (0, jnp.where(l == ll, r, 0) + blk_off, 0))

    out = SDS(w.shape, F32)
    extra = [] if prev is None else list(prev)
    return pl.pallas_call(
        body, name=name, grid=(nl, nr),
        in_specs=[blk, blk, blk] + [part_spec(ll) for ll in range(nl)] + [ANY] * len(extra),
        out_specs=(blk,) * 4, out_shape=(out,) * 4,
        input_output_aliases={3 + nl + i: i for i in range(len(extra))},
        compiler_params=_cp("parallel", "parallel"),
    )(w, m, v, *parts, *extra)


def small_update(me1, land_sh, land_rep, sharded, replicated):
    ns, nr = len(sharded), len(replicated)

    def body(me_ref, lsh, lrep, *refs):
        ins, outs = refs[:3 * (ns + nr)], refs[3 * (ns + nr):]

        def total(read):
            g = read(0)
            for s in range(1, N_DEV):
                g = g + read(s)
            return g

        def update(k, g, sl):
            w_ref, m_ref, v_ref = ins[3 * k:3 * k + 3]
            d, mn, vn = _adamw(w_ref[sl], g, m_ref[sl], v_ref[sl])
            for o, val in zip(outs[4 * k:4 * k + 4], (g, d, mn, vn)):
                o[sl] = val

        for k, (r0, n) in enumerate(GS_ROWS):
            update(k, total(lambda s: lsh[s, :, r0:r0 + n, :]), (slice(None),) * 3)
        lo, hi = (slice(None), slice(0, D)), (slice(None), slice(D, 2 * D))
        update(ns, total(lambda s: lrep[s, 0:4, :]), lo)
        update(ns, total(lambda s: lrep[s, 4:8, :]), hi)
        update(ns + 1, total(lambda s: lrep[s, 8:12, :]), (slice(None),) * 2)
        update(ns + 2, total(lambda s: lrep[s, 12:16, :]), (slice(None),) * 2)

    def whole(a):
        nd = a.ndim
        return pl.BlockSpec(a.shape, lambda i, me: (0,) * nd)

    params = [a for wmv in list(sharded) + list(replicated) for a in wmv]
    out_shape = tuple(SDS(wmv[0].shape, F32) for wmv in list(sharded) + list(replicated) for _ in range(4))
    res = pl.pallas_call(
        body, name="small_update", out_shape=out_shape,
        grid_spec=pltpu.PrefetchScalarGridSpec(
            num_scalar_prefetch=1, grid=(1,),
            in_specs=[pl.BlockSpec((N_DEV, DEPTH, 16, 128), lambda i, me: (0, 0, 0, me[0])), whole(land_rep)]
            + [whole(a) for a in params],
            out_specs=tuple(pl.BlockSpec(s.shape, lambda i, me, nd=len(s.shape): (0,) * nd) for s in out_shape)),
        compiler_params=_cp("arbitrary"),
    )(me1, land_sh, land_rep, *params)
    return [list(res[4 * k:4 * k + 4]) for k in range(ns + nr)]


def _ffn_forward(xf, xb, gw, f, s):
    g, u, a = ffn_up(xb, gw["w1"], f)
    xo, xob, xh, rs = ffn_down_ln(a, gw["w2"], f, xf, gw["small"], s)
    return (xo, xob), dict(xb=xb, g=g, u=u, a=a, xh=xh, rs=rs)


def _ffn_backward(dxn, sv, gw, f, s):
    dz, df, dgu, dln = ffn_bwd_gates(dxn, sv["xh"], sv["rs"], gw["small"], s, gw["w2"], f, sv["g"], sv["u"])
    dx = ffn_bwd_dx(dz, dgu, gw["w1"], f)
    dw1 = ffn_dw1(sv["xb"], dgu, f)
    dw2 = ffn_dw2(sv["a"], df, f)
    return dx, dw1, dw2, dln


def _mixer_forward(xf, xb, gw, rcb, ap, bm):
    p = mix_proj(xb, gw["win"])
    ca, pa, xc, xcb, gi, gr, h = mix_pre(p, gw["small"], rcb, ap, gw["rgw"])
    pb, ya, yb, m, xo, xob, xh, rs = mix_out(pa, h, p, bm, gw["w3"], xf, gw["small"])
    sv = dict(xb=xb, p=p, ca=ca, pa=pa, xc=xc, xcb=xcb, gi=gi, gr=gr, h=h, pb=pb, ya=ya, yb=yb, m=m, xh=xh, rs=rs)
    return (xo, xob), sv


def _mixer_backward(dxn, sv, gw, rcb, ap, bm):
    dz, dzb, dya, dyb, dbg, dca, dh, dphi, dln, dbm = mixb_head(
        dxn, sv["xh"], sv["rs"], gw["small"], gw["w3"], sv["p"], bm, sv["ya"], sv["yb"], sv["ca"], sv["h"])
    dg, dxc, red_rec = mixb_rec(dh, sv["gr"], sv["gi"], sv["h"], sv["xc"], ap, gw["rgw"])
    dplo, dxr, red_conv = mixb_conv(dca, dxc, dbg, sv["p"], gw["small"])
    dx = mixb_dx(dz, dplo, dxr, dphi, gw["win"])
    dwin = mixb_dwin(sv["xb"], dplo, dxr, dphi)
    dwo = mm_tn_square("mixb_dwo", sv["m"], dzb)
    dwoc = mm_tn_square("mixb_dwoc", sv["pa"], dya)
    dwor = mm_tn_square("mixb_dwor", sv["pb"], dyb)
    drgw = mixb_drgw(sv["xcb"], dg)
    return dx, dict(dwin=dwin, dwoc=dwoc, dwor=dwor, dwo=dwo, drgw=drgw), dln, dbm, red_rec, red_conv


def kernel(x, w_in, b_merge, sc_w, rc_w, rc_b, rg_w, rg_b, a_param, w_out_conv, w_out_rnn, w_o, ffn_w1, ffn_w2, ln_g, ln_b, loss_target, m_w_in, m_b_merge, m_sc_w, m_rc_w, m_rc_b, m_rg_w, m_rg_b, m_a_param, m_w_out_conv, m_w_out_rnn, m_w_o, m_ffn_w1, m_ffn_w2, m_ln_g, m_ln_b, v_w_in, v_b_merge, v_sc_w, v_rc_w, v_rc_b, v_rg_w, v_rg_b, v_a_param, v_w_out_conv, v_w_out_rnn, v_w_o, v_ffn_w1, v_ffn_w2, v_ln_g, v_ln_b):
    t = x.shape[1]
    me = _me()

    def rows(parts, total):
        out, off = None, 0
        for part in parts:
            r = part.shape[-2]
            pad = [(0, 0)] * (part.ndim - 2) + [(off, total - off - r), (0, 0)]
            padded = jnp.pad(part, pad)
            out = padded if out is None else out + padded
            off += r
        return out

    small = rows([sc_w, rc_w, rg_b, ln_g, ln_b], SMALL_ROWS)

    me1 = jnp.reshape(me, (1,)).astype(jnp.int32)

    def layer_shards(l, small_now):
        return gather_prep(l, me1, w_in, w_out_conv, w_out_rnn, w_o, ffn_w1, ffn_w2, rg_w, small_now)

    def as_weights(lands):
        return dict(zip(("win", "w3", "w1", "w2", "rgw", "small"), lands))

    xf = x.reshape(t, D)
    xb = xf.astype(BF16)

    early_at, rest_at = [2, 3, 5], [0, 1, 2, 3, 4]

    def ffn_slots(f, o_w1, o_w2, p):
        return [o_w1.at[f, p], o_w2.at[f, _slab(p, W2S, 16), :]]

    def early_slots(lands, p):
        return ffn_slots(0, lands[0], lands[1], p) + [lands[2].at[:, _slab(p, 128, 128)]]

    def rest_slots(lands, p):
        o_win, o_w3, o_w1, o_w2, o_rgw = lands
        return ([o_win.at[:, _slab(p, WIN_S, 128)], o_w3.at[:, _slab(p, 128, 128), :]] + ffn_slots(1, o_w1, o_w2, p)
                + [o_rgw.at[:, :, _slab(p, 32, 32), :]])

    early = ChipGather("gather0a", 3, 3, 3, lambda refs: [refs[0].at[0], refs[1].at[0], refs[2]], early_slots)
    rest = ChipGather("gather0b", 5, 5, 5, lambda refs: [refs[0], refs[1], refs[2].at[1], refs[3].at[1], refs[4]],
                      rest_slots)
    shards, own = layer_shards(0, small)

    def step(method, at, *args):
        res = method(*args[:-1], [shards[i] for i in at], [own[i] for i in at], args[-1])
        for i, s_new, o_new in zip(at, res[-3], res[-2]):
            shards[i], own[i] = s_new, o_new
        return tuple(res[:-3]) + (res[-1],)

    sems_a, thru = step(early.first, early_at, [xb, small])
    sems_b, thru = step(rest.first, rest_at, thru)
    relay_a, thru = step(early.relay, early_at, sems_a, thru)
    ((xb, small),) = step(early.last, early_at, sems_a, relay_a, thru)
    cur, s0 = _ffn_forward(xf, xb, as_weights(own), 0, 0)
    relay_b, thru = step(rest.relay, rest_at, sems_b, list(cur) + [small])
    (thru,) = step(rest.last, rest_at, sems_b, relay_b, thru)
    cur, small = thru[:2], thru[2]

    gathers = [ChipGather(f"gather{l}") for l in range(DEPTH)]
    gws, saved = [as_weights(own)], []
    for l in range(DEPTH):
        gw = gws[l]
        nxt = l + 1 < DEPTH
        if nxt:
            shards, own = layer_shards(l + 1, small)
            sems, shards, own, thru = gathers[l + 1].first(shards, own, list(cur) + [small])
            cur, small = thru[:2], thru[2]
        rcb, ap, bm = rc_b[l][None], a_param[l][None], b_merge[l][None]
        if l > 0:
            cur, s0 = _ffn_forward(cur[0], cur[1], gw, 0, 0)
        cur, s1 = _mixer_forward(cur[0], cur[1], gw, rcb, ap, bm)
        if nxt and l > 0:
            relay_sems, shards, own, cur = gathers[l + 1].relay(sems, shards, own, cur)
        cur, s2 = _ffn_forward(cur[0], cur[1], gw, 1, 2)
        if nxt and l == 0:
            relay_sems, shards, own, cur = gathers[l + 1].relay(sems, shards, own, cur)
        saved.append((s0, s1, s2))
        if nxt:
            _, own, cur = gathers[l + 1].last(sems, relay_sems, shards, own, cur)
            gws.append(as_weights(own))

    dy, loss_tile = loss_head(cur[0], loss_target.reshape(t, D))
    loss = lax.psum(loss_tile[0, 0], ("x", "y", "c"))

    lands = [dict() for _ in range(DEPTH)]
    gsmall, grep = [None] * DEPTH, [None] * DEPTH
    flights = {}

    def launch(key, kinds, grads, dy):
        ex, prep = grad_scatter(f"scatter{key}", kinds)
        fl = ex.start(grads, prep(me1, grads), [] if dy is None else [dy])
        flights[key] = (ex, fl)
        return None if dy is None else fl[4][0]

    def landed(key, thru):
        ex, fl = flights.pop(key)
        return ex.wait(fl[0], fl[1], fl[2], fl[3], thru)

    ffn_b, mixer, ffn_a = ("w1b", "w2b"), ("w_in", "w3", "rgw"), ("w1a", "w2a")
    for l in reversed(range(DEPTH)):
        gw = gws[l]
        rcb, ap, bm = rc_b[l][None], a_param[l][None], b_merge[l][None]
        s0, s1, s2 = saved[l]
        dy, dw1b, dw2b, dln2 = _ffn_backward(dy, s2, gw, 1, 2)
        if l == 0:
            dy = launch("0b", ["ffn"], [dw1b, dw2b], dy)
        dy, dmix, dln1, dbm, red_rec, red_conv = _mixer_backward(dy, s1, gw, rcb, ap, bm)
        mixer_grads = [dmix["dwin"], dmix["dwoc"], dmix["dwor"], dmix["dwo"], dmix["drgw"]]
        if l == 0:
            dy = launch("0m", ["mixer"], mixer_grads, dy)
        dy, dw1a, dw2a, dln0 = _ffn_backward(dy, s0, gw, 0, 0)
        if l == 0:
            lands[DEPTH - 1]["w_in"] = launch("0a", ["ffn"], [dw1a, dw2a], lands[DEPTH - 1]["w_in"])
        else:
            if l + 1 < DEPTH:
                got, (dy,) = landed(str(l + 1), [dy])
                lands[l + 1] = dict(zip(ffn_b + mixer + ffn_a, got))
            dy = launch(str(l), ["ffn", "mixer", "ffn"], [dw1b, dw2b] + mixer_grads + [dw1a, dw2a], dy)
        gsmall[l] = rows([red_conv[0:8], red_rec[0:2], dln0[0:1], dln1[0:1], dln2[0:1], dln0[1:2], dln1[1:2],
                          dln2[1:2]], 16)
        grep[l] = [dbm[0:1], dbm[1:2], red_rec[3:4], red_rec[2:3]]
    grad_x = dy.reshape(1, t, D)

    g_rep = rows([grep[l][kind] for kind in range(4) for l in range(DEPTH)], 16)
    land_sh, land_rep = gather_small_grads(jnp.stack(gsmall, axis=0), g_rep)
    small_names = ("sc_w", "rc_w", "rg_b", "ln_g", "ln_b", "b_merge", "rc_b", "a_param")
    small_res = small_update(
        me1, land_sh, land_rep,
        [(sc_w, m_sc_w, v_sc_w), (rc_w, m_rc_w, v_rc_w), (rg_b, m_rg_b, v_rg_b), (ln_g, m_ln_g, v_ln_g),
         (ln_b, m_ln_b, v_ln_b)],
        [(b_merge, m_b_merge, v_b_merge), (rc_b, m_rc_b, v_rc_b), (a_param, m_a_param, v_a_param)])

    def parts_of(layers):
        ls = [lands[l] for l in layers]
        return dict(w_in=[a["w_in"] for a in ls],
                    w3=[a["w3"].reshape(N_DEV, 3 * 128, D) for a in ls],
                    rgw=[a["rgw"].reshape(N_DEV, 2 * HEADS * 32, HD) for a in ls],
                    w1=[a[k] for a in ls for k in ("w1a", "w1b")],
                    w2=[a[k] for a in ls for k in ("w2a", "w2b")])

    families = [("w_in", w_in, m_w_in, v_w_in, "w_in", 1, D, WIN_S, 0, 128),
                ("w_out_conv", w_out_conv, m_w_out_conv, v_w_out_conv, "w3", 1, 128, D, 0, 128),
                ("w_out_rnn", w_out_rnn, m_w_out_rnn, v_w_out_rnn, "w3", 1, 128, D, 1, 128),
                ("w_o", w_o, m_w_o, v_w_o, "w3", 1, 128, D, 2, 128),
                ("ffn_w1", ffn_w1, m_ffn_w1, v_ffn_w1, "w1", 2, D, FS, 0, 256),
                ("ffn_w2", ffn_w2, m_ffn_w2, v_ffn_w2, "w2", 2, W2S, D, 0, W2S // 2),
                ("rg_w", rg_w, m_rg_w, v_rg_w, "rgw", 1, 2 * HEADS * 32, HD, 0, 256)]

    def adam_pass(tag, layers, prev):
        parts = parts_of(layers)
        outs = {}
        for name, w, m, v, fam, per, nrow, lanes, blk_off, tr in families:
            r3 = lambda a: a.reshape(DEPTH * per, nrow, lanes)
            outs[name] = adam_big(f"adam_{name}_{tag}", r3(w), r3(m), r3(v), parts[fam], nrow, lanes, blk_off, tr,
                                  per * layers[0], None if prev is None else prev[name])
        return outs

    def through(outs, key, extra):
        thru = [outs[f[0]][0] for f in families] + list(extra)
        got, thru = landed(key, thru)
        outs = {f[0]: [thru[i]] + list(outs[f[0]][1:]) for i, f in enumerate(families)}
        return got, outs, thru[len(families):]

    done = adam_pass("top", list(range(2, DEPTH)), None)
    got, done, _ = through(done, "1", [])
    lands[1] = dict(zip(ffn_b + mixer + ffn_a, got))
    done = adam_pass("second", [1], done)
    for key, names in (("0b", ffn_b), ("0m", mixer), ("0a", ffn_a)):
        got, done, _ = through(done, key, [])
        lands[0].update(zip(names, got))
    done = adam_pass("first", [0], done)
    res = {f[0]: [o.reshape(f[1].shape) for o in done[f[0]]] for f in families}

    res.update(zip(small_names, small_res))

    names = ["w_in", "b_merge", "sc_w", "rc_w", "rc_b", "rg_w", "rg_b", "a_param", "w_out_conv", "w_out_rnn", "w_o",
             "ffn_w1", "ffn_w2", "ln_g", "ln_b"]
    out = [loss, grad_x]
    for k in range(4):
        out += [res[n][k] for n in names]
    return tuple(out)
```

```python
import functools

import jax
import jax.numpy as jnp
from jax import lax
from jax.experimental import pallas as pl
from jax.experimental.pallas import tpu as pltpu

F32 = jnp.float32
BF16 = jnp.bfloat16
SDS = jax.ShapeDtypeStruct

N_DEV = 8
DEPTH = 4
D = 1024
D_FF = 2816
FS = D_FF // 4
W2S = D_FF // 8
D_IN = 7 * D
WIN_S = D_IN // 8
HEADS = 4
HD = D // HEADS
LRU_C = 8.0
ALPHA = (2.0 * DEPTH) ** 0.25
LN_EPS = 1e-5
ADAM_LR, ADAM_B1, ADAM_B2, ADAM_EPS, ADAM_WD, ADAM_STEP = 0.001, 0.9, 0.999, 1e-08, 0.01, 10

R_SC, R_RC, R_RGB, R_LNG, R_LNB = 0, 3, 7, 9, 12
SMALL_ROWS = 16
GS_ROWS = ((0, 3), (3, 4), (8, 2), (10, 3), (13, 3))

NN = ((1,), (0,))
NT = ((1,), (1,))
TN = ((0,), (0,))
MESH = pl.DeviceIdType.MESH
ANY = pl.BlockSpec(memory_space=pl.ANY)
VMEM_LIMIT = 52 * 1024 * 1024


def _dot(a, b, dims):
    return lax.dot_general(a, b, (dims, ((), ())), preferred_element_type=F32)


def _cp(*sem):
    return pltpu.CompilerParams(dimension_semantics=sem, vmem_limit_bytes=VMEM_LIMIT)


def _sigmoid(x):
    return 1.0 / (1.0 + jnp.exp(-x))


def _sigmoid_t(x):
    return 0.5 * jnp.tanh(0.5 * x) + 0.5


def _log1p(e):
    u = 1.0 + e
    return jnp.where(u == 1.0, e, jnp.log(u) * e / jnp.where(u == 1.0, 1.0, u - 1.0))


def _softplus(x):
    return jnp.maximum(x, 0.0) + _log1p(jnp.exp(-jnp.abs(x)))


def _neg_expm1(x):
    u = jnp.exp(x)
    um1 = u - 1.0
    safe = jnp.logical_and(u != 1.0, um1 != -1.0)
    r = um1 * x / jnp.where(safe, jnp.log(jnp.where(safe, u, 0.5)), 1.0)
    return -jnp.where(u == 1.0, x, jnp.where(um1 == -1.0, -1.0, r))


def _gelu(y):
    c = 0.7978845608028654
    t = jnp.tanh(c * (y + 0.044715 * y * y * y))
    return 0.5 * y * (1.0 + t), t


def _gelu_grad(y, t):
    c = 0.7978845608028654
    return 0.5 * (1.0 + t) + 0.5 * y * (1.0 - t * t) * c * (1.0 + 3.0 * 0.044715 * y * y)


def _ln_fwd(z, g, b):
    mu = jnp.mean(z, axis=-1, keepdims=True)
    zc = z - mu
    var = jnp.mean(zc * zc, axis=-1, keepdims=True)
    rstd = lax.rsqrt(var + LN_EPS)
    xh = zc * rstd
    return xh * g + b, xh, rstd


def _ln_bwd(dy, xh, rstd, g):
    dxh = dy * g
    m1 = jnp.mean(dxh, axis=-1, keepdims=True)
    m2 = jnp.mean(dxh * xh, axis=-1, keepdims=True)
    return rstd * (dxh - m1 - xh * m2)


def _row_tile(t, want):
    return min(want, t)


def _me():
    return 4 * lax.axis_index("x") + 2 * lax.axis_index("y") + lax.axis_index("c")


def _coords(p):
    return (p // 4, (p // 2) % 2, p % 2)


def _all_to_all_copies(srcs_of, dsts_of, waits, send_sems, recv_sems, loc_sems):
    me = _me()
    n = len(waits)
    own_src, own_dst = srcs_of(me), dsts_of(me)
    local = [pltpu.make_async_copy(own_src[k], own_dst[k], loc_sems.at[k]) for k in range(n)]
    for cp in local:
        cp.start()
    for d in range(1, N_DEV):
        p = (me + d) % N_DEV
        src, dst = srcs_of(p), dsts_of(me)
        for k in range(n):
            pltpu.make_async_remote_copy(
                src_ref=src[k], dst_ref=dst[k], send_sem=send_sems.at[waits[k][1]],
                recv_sem=recv_sems.at[waits[k][1]], device_id=_coords(p), device_id_type=MESH).start()
    done = set()
    for k in range(n):
        ref, s = waits[k]
        if s in done:
            continue
        done.add(s)
        pltpu.make_async_remote_copy(
            src_ref=ref, dst_ref=ref, send_sem=send_sems.at[s], recv_sem=recv_sems.at[s],
            device_id=_coords(me), device_id_type=MESH).wait()
    for cp in local:
        cp.wait()


HBM = pl.BlockSpec(memory_space=pltpu.HBM)
SEM = pl.BlockSpec(memory_space=pltpu.SEMAPHORE)
EFFECT = pltpu.SideEffectType.DATAFLOW_SIDE_EFFECTING


def _in_hbm(a):
    return pltpu.with_memory_space_constraint(a, pltpu.HBM)


class Exchange:
    def __init__(self, name, src_of, dst_of, sem_of, span_of):
        self.name, self.src_of, self.dst_of, self.sem_of, self.span_of = name, src_of, dst_of, sem_of, span_of
        self.nsem = max(sem_of) + 1

    def start(self, srcs, lands, thru):
        n, m = len(srcs), len(lands)
        ops = list(srcs) + list(lands) + list(thru)

        def body(*refs):
            src_refs, land_refs = refs[:n], refs[n:n + m]
            send_sems, recv_sems = refs[len(ops)], refs[len(ops) + 1]
            me = _me()
            for dd in range(1, N_DEV):
                p = (me + dd) % N_DEV
                s, d = self.src_of(src_refs, p), self.dst_of(land_refs, me)
                for k in range(len(self.sem_of)):
                    pltpu.make_async_remote_copy(
                        src_ref=s[k], dst_ref=d[k], send_sem=send_sems.at[self.sem_of[k]],
                        recv_sem=recv_sems.at[self.sem_of[k]], device_id=_coords(p), device_id_type=MESH).start()

        sem = pltpu.SemaphoreType.DMA((self.nsem,))
        res = pl.pallas_call(
            body, name=self.name + "_start",
            out_shape=(sem, sem) + tuple(pltpu.HBM(a.shape, a.dtype) for a in ops),
            in_specs=[HBM] * len(ops), out_specs=(SEM, SEM) + (HBM,) * len(ops),
            input_output_aliases={i: 2 + i for i in range(len(ops))},
            compiler_params=pltpu.CompilerParams(has_side_effects=EFFECT),
        )(*[_in_hbm(a) for a in ops])
        return res[0], res[1], res[2:2 + n], res[2 + n:2 + n + m], list(res[2 + n + m:])

    def wait(self, send_sems, recv_sems, srcs, lands, thru):
        n, m = len(srcs), len(lands)
        ops = list(srcs) + list(lands) + list(thru)

        def body(*refs):
            land_refs = refs[n:n + m]
            ssem, rsem = refs[len(ops)], refs[len(ops) + 1]
            me = _me()
            spans = self.span_of(land_refs)
            for s in range(self.nsem):
                cp = pltpu.make_async_remote_copy(
                    src_ref=spans[s], dst_ref=spans[s], send_sem=ssem.at[s], recv_sem=rsem.at[s],
                    device_id=_coords(me), device_id_type=MESH)
                cp.wait_send()
                cp.wait_recv()

        res = pl.pallas_call(
            body, name=self.name + "_wait",
            out_shape=tuple(pltpu.HBM(a.shape, a.dtype) for a in ops),
            in_specs=[HBM] * len(ops) + [SEM, SEM], out_specs=(HBM,) * len(ops),
            input_output_aliases={i: i for i in range(len(ops))},
            compiler_params=pltpu.CompilerParams(has_side_effects=EFFECT),
        )(*ops, send_sems, recv_sems)
        return res[n:n + m], list(res[n + m:])


class ChipGather:
    def __init__(self, name, nsrc=6, nland=6, ncopy=6, views=None, slots=None):
        self.name, self.nsrc, self.nland, self.ncopy = name, nsrc, nland, ncopy
        self.views = views if views is not None else list
        self.slots = slots if slots is not None else self.layer_slots

    @staticmethod
    def layer_slots(lands, p):
        o_win, o_w3, o_w1, o_w2, o_rgw, o_sm = lands
        return [o_win.at[:, pl.ds(pl.multiple_of(p * WIN_S, 128), WIN_S)],
                o_w3.at[:, pl.ds(pl.multiple_of(p * 128, 128), 128), :],
                o_w1.at[:, p],
                o_w2.at[:, pl.ds(pl.multiple_of(p * W2S, 16), W2S), :],
                o_rgw.at[:, :, pl.ds(pl.multiple_of(p * 32, 32), 32), :],
                o_sm.at[:, pl.ds(pl.multiple_of(p * 128, 128), 128)]]

    @staticmethod
    def _places():
        x, y, c = lax.axis_index("x"), lax.axis_index("y"), lax.axis_index("c")
        chips = [(1 - x, y), (x, 1 - y), (1 - x, 1 - y)]
        return (x, y, c), (x, y, 1 - c), chips

    @staticmethod
    def _index(place):
        return 4 * place[0] + 2 * place[1] + place[2]

    def _call(self, tag, body, ops, sems_in, sems_out):
        nops = len(ops)
        sem = pltpu.SemaphoreType.DMA((max(sems_out, 1),))
        res = pl.pallas_call(
            body, name=f"{self.name}_{tag}",
            out_shape=((sem, sem) if sems_out else ()) + tuple(pltpu.HBM(a.shape, a.dtype) for a in ops),
            in_specs=[HBM] * nops + [SEM] * len(sems_in),
            out_specs=((SEM, SEM) if sems_out else ()) + (HBM,) * nops,
            input_output_aliases={i: (2 if sems_out else 0) + i for i in range(nops)},
            compiler_params=pltpu.CompilerParams(has_side_effects=EFFECT),
        )(*[_in_hbm(a) for a in ops], *sems_in)
        return res

    def first(self, srcs, lands, thru):
        ns, nl, n = self.nsrc, self.nland, self.ncopy
        ops = list(srcs) + list(lands) + list(thru)

        def body(*refs):
            src_refs, land_refs = self.views(refs[:ns]), refs[ns:ns + nl]
            send_sems, recv_sems = refs[len(ops)], refs[len(ops) + 1]
            me, sibling, chips = self._places()
            dst = self.slots(land_refs, self._index(me))
            targets = [sibling] + [(cx, cy, me[2]) for cx, cy in chips]
            for r, to in enumerate(targets):
                for k in range(n):
                    pltpu.make_async_remote_copy(
                        src_ref=src_refs[k], dst_ref=dst[k], send_sem=send_sems.at[n *r + k],
                        recv_sem=recv_sems.at[n *r + k], device_id=to, device_id_type=MESH).start()

        res = self._call("first", body, ops, [], 4 * n)
        return (res[0], res[1]), list(res[2:2 + ns]), list(res[2 + ns:2 + ns + nl]), list(res[2 + ns + nl:])

    def relay(self, sems, srcs, lands, thru):
        ns, nl, n = self.nsrc, self.nland, self.ncopy
        ops = list(srcs) + list(lands) + list(thru)

        def wait_body(*refs):
            land_refs = refs[ns:ns + nl]
            ssem, rsem = refs[len(ops)], refs[len(ops) + 1]
            me, sibling, chips = self._places()
            for j, (cx, cy) in enumerate(chips):
                got = self.slots(land_refs, self._index((cx, cy, me[2])))
                for k in range(n):
                    pltpu.make_async_remote_copy(
                        src_ref=got[k], dst_ref=got[k], send_sem=ssem.at[n *(1 + j) + k], recv_sem=rsem.at[n *(1 + j) + k],
                        device_id=me, device_id_type=MESH).wait_recv()

        ops = list(self._call("landed", wait_body, ops, list(sems), 0))

        def start_body(*refs):
            land_refs = refs[ns:ns + nl]
            ssem, rsem = refs[len(ops)], refs[len(ops) + 1]
            me, sibling, chips = self._places()
            for j, (cx, cy) in enumerate(chips):
                got = self.slots(land_refs, self._index((cx, cy, me[2])))
                for k in range(n):
                    pltpu.make_async_remote_copy(
                        src_ref=got[k], dst_ref=got[k], send_sem=ssem.at[n *j + k], recv_sem=rsem.at[n *j + k],
                        device_id=sibling, device_id_type=MESH).start()

        res = self._call("relay", start_body, ops, [], 3 * n)
        return (res[0], res[1]), list(res[2:2 + ns]), list(res[2 + ns:2 + ns + nl]), list(res[2 + ns + nl:])

    def last(self, sems, relay_sems, srcs, lands, thru):
        ns, nl, n = self.nsrc, self.nland, self.ncopy
        ops = list(srcs) + list(lands) + list(thru)

        def body(*refs):
            src_refs, land_refs = self.views(refs[:ns]), refs[ns:ns + nl]
            ssem, rsem, ssem2, rsem2 = refs[len(ops):len(ops) + 4]
            me, sibling, chips = self._places()
            got = self.slots(land_refs, self._index(sibling))
            for k in range(n):
                pltpu.make_async_remote_copy(
                    src_ref=got[k], dst_ref=got[k], send_sem=ssem.at[k], recv_sem=rsem.at[k],
                    device_id=me, device_id_type=MESH).wait_recv()
            for j, (cx, cy) in enumerate(chips):
                got = self.slots(land_refs, self._index((cx, cy, 1 - me[2])))
                for k in range(n):
                    pltpu.make_async_remote_copy(
                        src_ref=got[k], dst_ref=got[k], send_sem=ssem2.at[n *j + k], recv_sem=rsem2.at[n *j + k],
                        device_id=me, device_id_type=MESH).wait_recv()
            for sem_s, sem_r, count in ((ssem, rsem, 4), (ssem2, rsem2, 3)):
                for r in range(count):
                    for k in range(n):
                        pltpu.make_async_remote_copy(
                            src_ref=src_refs[k], dst_ref=src_refs[k], send_sem=sem_s.at[n *r + k],
                            recv_sem=sem_r.at[n *r + k], device_id=me, device_id_type=MESH).wait_send()

        res = self._call("last", body, ops, list(sems) + list(relay_sems), 0)
        return list(res[:ns]), list(res[ns:ns + nl]), list(res[ns + nl:])


GATHER_SHAPES = (SDS((D, D_IN), BF16), SDS((3, D, D), BF16), SDS((2, N_DEV, D, FS), BF16),
                 SDS((2, D_FF, D), BF16), SDS((2, HEADS, HD, HD), BF16), SDS((SMALL_ROWS, D), F32))


def _slab(p, n, align):
    return pl.ds(pl.multiple_of(p * n, align), n)


class GradGroup:
    def __init__(self, nsrc, lands, src_of, dst_of, sem_of, in_specs, out_specs, copy):
        self.nsrc, self.lands, self.src_of, self.dst_of, self.sem_of = nsrc, lands, src_of, dst_of, sem_of
        self.in_specs, self.out_specs, self.copy = in_specs, out_specs, copy


def _mixer_group():
    hd = D // 2

    def src_of(refs, p):
        dwin, dwoc, dwor, dwo, drgw = refs
        r3 = _slab(p, 128, 128)
        return [dwin.at[:, _slab(p, WIN_S, 128)], dwoc.at[r3, :], dwor.at[r3, :], dwo.at[r3, :],
                drgw.at[:, :, _slab(p, 32, 32), :]]

    def dst_of(lands, p):
        l_win, l_w3, l_rgw = lands
        return [l_win.at[p], l_w3.at[p, 0], l_w3.at[p, 1], l_w3.at[p, 2], l_rgw.at[p]]

    def copy(srcs, lands):
        lands[0][...] = srcs[0][...]
        for k in range(3):
            lands[1][k] = srcs[1 + k][...]
        lands[2][...] = srcs[4][...]

    three = pl.BlockSpec((64, D), lambda i, me: (2 * me[0] + i, 0))
    return GradGroup(
        5, (SDS((N_DEV, D, WIN_S), BF16), SDS((N_DEV, 3, 128, D), BF16), SDS((N_DEV, 2, HEADS, 32, HD), BF16)),
        src_of, dst_of, [0, 1, 1, 1, 2],
        [pl.BlockSpec((hd, WIN_S), lambda i, me: (i, me[0])), three, three, three,
         pl.BlockSpec((2, HEADS, 16, HD), lambda i, me: (0, 0, 2 * me[0] + i, 0))],
        [pl.BlockSpec((None, hd, WIN_S), lambda i, me: (me[0], i, 0)),
         pl.BlockSpec((None, 3, 64, D), lambda i, me: (me[0], 0, i, 0)),
         pl.BlockSpec((None, 2, HEADS, 16, HD), lambda i, me: (me[0], 0, 0, i, 0))],
        copy)


def _ffn_group():
    hd, hw = D // 2, W2S // 2

    def src_of(refs, p):
        return [refs[0].at[p], refs[1].at[_slab(p, W2S, 16), :]]

    def dst_of(lands, p):
        return [lands[0].at[p], lands[1].at[p]]

    def copy(srcs, lands):
        lands[0][...] = srcs[0][...]
        lands[1][...] = srcs[1][...]

    return GradGroup(
        2, (SDS((N_DEV, D, FS), BF16), SDS((N_DEV, W2S, D), BF16)), src_of, dst_of, [0, 1],
        [pl.BlockSpec((None, hd, FS), lambda i, me: (me[0], i, 0)), pl.BlockSpec((hw, D), lambda i, me: (2 * me[0] + i, 0))],
        [pl.BlockSpec((None, hd, FS), lambda i, me: (me[0], i, 0)), pl.BlockSpec((None, hw, D), lambda i, me: (me[0], i, 0))],
        copy)


def grad_scatter(name, kinds):
    groups = [_mixer_group() if k == "mixer" else _ffn_group() for k in kinds]

    def per_group(refs, counts, fn):
        out, i = [], 0
        for g, c in zip(groups, counts):
            out += fn(g, refs[i:i + c])
            i += c
        return out

    nsrcs = [g.nsrc for g in groups]
    nlands = [len(g.lands) for g in groups]
    sem_of, off = [], 0
    for g in groups:
        sem_of += [off + s for s in g.sem_of]
        off += len(g.lands)

    ex = Exchange(name,
                  lambda refs, p: per_group(refs, nsrcs, lambda g, r: g.src_of(r, p)),
                  lambda lands, p: per_group(lands, nlands, lambda g, r: g.dst_of(r, p)),
                  sem_of, lambda lands: [a.at[pl.ds(0, 7)] for a in lands])

    def prep(me1, grads):
        nsrc = sum(nsrcs)

        def body(me_ref, *refs):
            srcs, lands = refs[:nsrc], refs[nsrc:]
            i = j = 0
            for g in groups:
                g.copy(srcs[i:i + g.nsrc], lands[j:j + len(g.lands)])
                i += g.nsrc
                j += len(g.lands)

        return list(pl.pallas_call(
            body, name=name + "_prep", out_shape=tuple(s for g in groups for s in g.lands),
            grid_spec=pltpu.PrefetchScalarGridSpec(
                num_scalar_prefetch=1, grid=(2,), in_specs=[s for g in groups for s in g.in_specs],
                out_specs=tuple(s for g in groups for s in g.out_specs)),
            compiler_params=_cp("arbitrary"),
        )(me1, *grads))

    return ex, prep


def gather_prep(l, me1, w_in, w_out_conv, w_out_rnn, w_o, ffn_w1, ffn_w2, rg_w, small):
    hd, hw = D // 2, W2S // 2

    def body(me_ref, win, woc, wor, wo, w1, w2, rgw, sm, c_win, c_w3, c_w1, c_w2, c_rgw, c_sm,
             o_win, o_w3, o_w1, o_w2, o_rgw, o_sm):
        a = win[...].astype(BF16)
        c_win[...] = a
        o_win[...] = a
        for k, r in enumerate((woc, wor, wo)):
            b = r[...].astype(BF16)
            c_w3[k] = b
            o_w3[k] = b
        for src, comp, own in ((w1, c_w1, o_w1), (w2, c_w2, o_w2), (rgw, c_rgw, o_rgw)):
            b = src[...].astype(BF16)
            comp[...] = b
            own[...] = b
        c_sm[...] = sm[...]
        o_sm[...] = sm[...]

    three = pl.BlockSpec((None, 64, D), lambda i, me: (l, i, 0))
    in_specs = [pl.BlockSpec((None, hd, WIN_S), lambda i, me: (l, i, 0)), three, three, three,
                pl.BlockSpec((None, 2, hd, FS), lambda i, me: (l, 0, i, 0)),
                pl.BlockSpec((None, 2, hw, D), lambda i, me: (l, 0, i, 0)),
                pl.BlockSpec((None, 2, HEADS, 16, HD), lambda i, me: (l, 0, 0, i, 0)),
                pl.BlockSpec((None, 8, 128), lambda i, me: (l, i, 0))]
    out_specs = (pl.BlockSpec((hd, WIN_S), lambda i, me: (i, 0)), pl.BlockSpec((3, 64, D), lambda i, me: (0, i, 0)),
                 pl.BlockSpec((2, hd, FS), lambda i, me: (0, i, 0)), pl.BlockSpec((2, hw, D), lambda i, me: (0, i, 0)),
                 pl.BlockSpec((2, HEADS, 16, HD), lambda i, me: (0, 0, i, 0)), pl.BlockSpec((8, 128), lambda i, me: (i, 0)),
                 pl.BlockSpec((hd, WIN_S), lambda i, me: (i, me[0])),
                 pl.BlockSpec((3, 64, D), lambda i, me: (0, 2 * me[0] + i, 0)),
                 pl.BlockSpec((2, None, hd, FS), lambda i, me: (0, me[0], i, 0)),
                 pl.BlockSpec((2, hw, D), lambda i, me: (0, 2 * me[0] + i, 0)),
                 pl.BlockSpec((2, HEADS, 16, HD), lambda i, me: (0, 0, 2 * me[0] + i, 0)),
                 pl.BlockSpec((8, 128), lambda i, me: (i, me[0])))
    compact = (SDS((D, WIN_S), BF16), SDS((3, 128, D), BF16), SDS((2, D, FS), BF16), SDS((2, W2S, D), BF16),
               SDS((2, HEADS, 32, HD), BF16), SDS((SMALL_ROWS, 128), F32))
    res = pl.pallas_call(
        body, name=f"gather_prep{l}", out_shape=compact + GATHER_SHAPES,
        grid_spec=pltpu.PrefetchScalarGridSpec(num_scalar_prefetch=1, grid=(2,), in_specs=in_specs, out_specs=out_specs),
        compiler_params=_cp("arbitrary"),
    )(me1, w_in, w_out_conv, w_out_rnn, w_o, ffn_w1, ffn_w2, rg_w, small)
    return list(res[:6]), list(res[6:])


def gather_small_grads(g_sharded, g_replicated):
    def body(ga, gb, la, lb, send_sems, recv_sems, loc_sems):
        _all_to_all_copies(lambda p: [ga, gb], lambda p: [la.at[p], lb.at[p]],
                           [(la.at[pl.ds(0, 7)], 0), (lb.at[pl.ds(0, 7)], 1)], send_sems, recv_sems, loc_sems)

    return pl.pallas_call(
        body, name="gather_small_grads",
        out_shape=(SDS((N_DEV,) + g_sharded.shape, F32), SDS((N_DEV,) + g_replicated.shape, F32)),
        in_specs=[ANY, ANY], out_specs=(ANY, ANY),
        scratch_shapes=[pltpu.SemaphoreType.DMA((2,)), pltpu.SemaphoreType.DMA((2,)), pltpu.SemaphoreType.DMA((2,))],
        compiler_params=pltpu.CompilerParams(has_side_effects=True),
    )(g_sharded, g_replicated)


def ffn_up(xb, w1, f):
    t = xb.shape[0]
    tm = _row_tile(t, 1024)

    def body(x_ref, wg_ref, wu_ref, g_ref, u_ref, a_ref):
        x = x_ref[...]
        g = _dot(x, wg_ref[...], NN)
        u = _dot(x, wu_ref[...], NN)
        g_ref[...] = g.astype(BF16)
        u_ref[...] = u.astype(BF16)
        a_ref[...] = (g * _sigmoid_t(g) * u).astype(BF16)

    out = pl.BlockSpec((None, tm, FS), lambda j, i: (j, i, 0))
    return pl.pallas_call(
        body, name=f"ffn_up{f}", grid=(4, t // tm),
        in_specs=[pl.BlockSpec((tm, D), lambda j, i: (i, 0)),
                  pl.BlockSpec((None, None, D, FS), lambda j, i: (f, j, 0, 0)),
                  pl.BlockSpec((None, None, D, FS), lambda j, i: (f, j + 4, 0, 0))],
        out_specs=(out, out, out), out_shape=(SDS((4, t, FS), BF16),) * 3,
        compiler_params=_cp("parallel", "parallel"),
    )(xb, w1, w1)


def ffn_down_ln(a, w2, f, xf, small, s):
    t = xf.shape[0]
    tm = _row_tile(t, 512)

    def body(a_ref, w_ref, x_ref, sm_ref, xo_ref, xb_ref, xh_ref, rs_ref):
        acc = _dot(a_ref[0], w_ref[0:FS, :], NN)
        for j in range(1, 4):
            acc = acc + _dot(a_ref[j], w_ref[j * FS:(j + 1) * FS, :], NN)
        z = ALPHA * x_ref[...] + 0.5 * acc
        y, xh, rstd = _ln_fwd(z, sm_ref[R_LNG + s:R_LNG + s + 1, :], sm_ref[R_LNB + s:R_LNB + s + 1, :])
        xo_ref[...] = y
        xb_ref[...] = y.astype(BF16)
        xh_ref[...] = xh
        rs_ref[...] = rstd

    row = pl.BlockSpec((tm, D), lambda i: (i, 0))
    return pl.pallas_call(
        body, name=f"ffn_down_ln{f}", grid=(t // tm,),
        in_specs=[pl.BlockSpec((4, tm, FS), lambda i: (0, i, 0)),
                  pl.BlockSpec((None, D_FF, D), lambda i: (f, 0, 0)),
                  row, pl.BlockSpec((SMALL_ROWS, D), lambda i: (0, 0))],
        out_specs=(row, row, row, pl.BlockSpec((tm, 1), lambda i: (i, 0))),
        out_shape=(SDS((t, D), F32), SDS((t, D), BF16), SDS((t, D), F32), SDS((t, 1), F32)),
        compiler_params=_cp("parallel"),
    )(a, w2, xf, small)


def ffn_bwd_gates(dxn, xh, rstd, small, s, w2, f, g, u):
    t = dxn.shape[0]
    tm = _row_tile(t, 256)

    def body(dy_ref, xh_ref, rs_ref, sm_ref, w_ref, g_ref, u_ref, dz_ref, df_ref, dgu_ref, dln_ref):
        i = pl.program_id(0)
        dy = dy_ref[...]
        xhat = xh_ref[...]
        dz = _ln_bwd(dy, xhat, rs_ref[...], sm_ref[R_LNG + s:R_LNG + s + 1, :])

        @pl.when(i == 0)
        def _():
            dln_ref[...] = jnp.zeros_like(dln_ref)

        dln_ref[0:1, :] += jnp.sum(dy * xhat, axis=0, keepdims=True)
        dln_ref[1:2, :] += jnp.sum(dy, axis=0, keepdims=True)
        dz_ref[...] = dz
        df = (0.5 * dz).astype(BF16)
        df_ref[...] = df
        for j in range(4):
            da = _dot(df, w_ref[j * FS:(j + 1) * FS, :], NT)
            gg = g_ref[j].astype(F32)
            uu = u_ref[j].astype(F32)
            sg = _sigmoid_t(gg)
            dgu_ref[j] = (da * uu * (sg * (1.0 + gg * (1.0 - sg)))).astype(BF16)
            dgu_ref[j + 4] = (da * (gg * sg)).astype(BF16)

    row = pl.BlockSpec((tm, D), lambda i: (i, 0))
    gu = pl.BlockSpec((4, tm, FS), lambda i: (0, i, 0))
    return pl.pallas_call(
        body, name=f"ffn_bwd_gates{f}", grid=(t // tm,),
        in_specs=[row, row, pl.BlockSpec((tm, 1), lambda i: (i, 0)), pl.BlockSpec((SMALL_ROWS, D), lambda i: (0, 0)),
                  pl.BlockSpec((None, D_FF, D), lambda i: (f, 0, 0)), gu, gu],
        out_specs=(row, row, pl.BlockSpec((8, tm, FS), lambda i: (0, i, 0)), pl.BlockSpec((2, D), lambda i: (0, 0))),
        out_shape=(SDS((t, D), F32), SDS((t, D), BF16), SDS((8, t, FS), BF16), SDS((2, D), F32)),
        compiler_params=_cp("arbitrary"),
    )(dxn, xh, rstd, small, w2, g, u)


def ffn_bwd_dx(dz, dgu, w1, f):
    t = dz.shape[0]
    tm = _row_tile(t, 1024)

    def body(dz_ref, d_ref, w_ref, dx_ref, acc):
        k = pl.program_id(1)

        @pl.when(k == 0)
        def _():
            acc[...] = ALPHA * dz_ref[...]

        acc[...] += _dot(d_ref[...], w_ref[...], NT)

        @pl.when(k == 7)
        def _():
            dx_ref[...] = acc[...]

    row = pl.BlockSpec((tm, D), lambda i, k: (i, 0))
    return pl.pallas_call(
        body, name=f"ffn_bwd_dx{f}", grid=(t // tm, 8),
        in_specs=[row, pl.BlockSpec((None, tm, FS), lambda i, k: (k, i, 0)),
                  pl.BlockSpec((None, None, D, FS), lambda i, k: (f, k, 0, 0))],
        out_specs=row, out_shape=SDS((t, D), F32),
        scratch_shapes=[pltpu.VMEM((tm, D), F32)],
        compiler_params=_cp("parallel", "arbitrary"),
    )(dz, dgu, w1)


def _mm_tn(name, a, a_spec, b, b_spec, out_sds, out_spec, grid):
    def body(a_ref, b_ref, o_ref):
        o_ref[...] = _dot(a_ref[...], b_ref[...], TN).astype(o_ref.dtype)

    return pl.pallas_call(
        body, name=name, grid=grid, in_specs=[a_spec, b_spec], out_specs=out_spec, out_shape=out_sds,
        compiler_params=_cp(*(["parallel"] * len(grid))),
    )(a, b)


def ffn_dw1(xb, dgu, f):
    t = xb.shape[0]
    return _mm_tn(f"ffn_dw1_{f}", xb, pl.BlockSpec((t, D), lambda j: (0, 0)),
                  dgu, pl.BlockSpec((None, t, FS), lambda j: (j, 0, 0)),
                  SDS((8, D, FS), BF16), pl.BlockSpec((None, D, FS), lambda j: (j, 0, 0)), (8,))


def ffn_dw2(a, df, f):
    t = df.shape[0]
    return _mm_tn(f"ffn_dw2_{f}", a, pl.BlockSpec((None, t, FS), lambda j: (j, 0, 0)),
                  df, pl.BlockSpec((t, D), lambda j: (0, 0)),
                  SDS((D_FF, D), BF16), pl.BlockSpec((FS, D), lambda j: (j, 0)), (4,))


def mm_tn_square(name, a, b):
    t = a.shape[0]
    return _mm_tn(name, a, pl.BlockSpec((t, 512), lambda i: (0, i)),
                  b, pl.BlockSpec((t, D), lambda i: (0, 0)),
                  SDS((D, D), BF16), pl.BlockSpec((512, D), lambda i: (i, 0)), (2,))


def mix_proj(xb, win):
    t = xb.shape[0]
    tm = _row_tile(t, 1024)

    def body(x_ref, w_ref, o_ref):
        o_ref[...] = _dot(x_ref[...], w_ref[...], NN).astype(BF16)

    return pl.pallas_call(
        body, name="mix_proj", grid=(7, t // tm),
        in_specs=[pl.BlockSpec((tm, D), lambda n, i: (i, 0)), pl.BlockSpec((D, D), lambda n, i: (0, n))],
        out_specs=pl.BlockSpec((tm, D), lambda n, i: (i, n)), out_shape=SDS((t, D_IN), BF16),
        compiler_params=_cp("parallel", "parallel"),
    )(xb, win)


def _pcol(tm, k):
    return pl.BlockSpec((tm, D), lambda i: (i, k))


def _prev_halo(tm, k):
    return pl.BlockSpec((8, D), lambda i: (jnp.maximum(i * (tm // 8) - 1, 0), k))


def _prev_halo16(tm, k):
    return pl.BlockSpec((16, D), lambda i: (jnp.maximum(i * (tm // 16) - 1, 0), k))


def _next_halo(tm, t, k):
    return pl.BlockSpec((8, D), lambda i: (jnp.minimum((i + 1) * (tm // 8), t // 8 - 1), k))


def _full(shape):
    nd = len(shape)
    return pl.BlockSpec(shape, lambda i: (0,) * nd)


def mix_pre(p, small, rcb, ap, rgw):
    t = p.shape[0]
    tm = _row_tile(t, 256)

    def body(bg_ref, cg_ref, v_ref, xr_ref, cgh_ref, vh_ref, xrh_ref, sm_ref, rcb_ref, ap_ref, rgw_ref,
             ca_ref, pa_ref, xc_ref, xcb_ref, gi_ref, gr_ref, h_ref, ext1, ext2, a_s, b_s, carry):
        i = pl.program_id(0)
        first = i == 0
        cv = cg_ref[...].astype(F32) * v_ref[...].astype(F32)
        ext1[0:16, :] = jnp.where(first, 0.0, cgh_ref[...].astype(F32) * vh_ref[...].astype(F32))
        ext1[16:, :] = cv
        xr = xr_ref[...].astype(F32)
        ext2[0:16, :] = jnp.where(first, 0.0, xrh_ref[...].astype(F32))
        ext2[16:, :] = xr
        ca = (sm_ref[R_SC:R_SC + 1, :] * ext1[pl.ds(14, tm), :] + sm_ref[R_SC + 1:R_SC + 2, :] * ext1[pl.ds(15, tm), :]
              + sm_ref[R_SC + 2:R_SC + 3, :] * cv)
        ca_ref[...] = ca.astype(BF16)
        pa_ref[...] = (bg_ref[...].astype(F32) * ca).astype(BF16)
        xc = (sm_ref[R_RC:R_RC + 1, :] * ext2[pl.ds(13, tm), :] + sm_ref[R_RC + 1:R_RC + 2, :] * ext2[pl.ds(14, tm), :]
              + sm_ref[R_RC + 2:R_RC + 3, :] * ext2[pl.ds(15, tm), :] + sm_ref[R_RC + 3:R_RC + 4, :] * xr
              + rcb_ref[...])
        xc_ref[...] = xc
        xcb = xc.astype(BF16)
        xcb_ref[...] = xcb
        g0, g1 = [], []
        for h in range(HEADS):
            xh = xcb[:, h * HD:(h + 1) * HD]
            g0.append(_dot(xh, rgw_ref[0, h], NN))
            g1.append(_dot(xh, rgw_ref[1, h], NN))
        gi = _sigmoid(jnp.concatenate(g0, axis=1) + sm_ref[R_RGB:R_RGB + 1, :])
        gr = _sigmoid(jnp.concatenate(g1, axis=1) + sm_ref[R_RGB + 1:R_RGB + 2, :])
        gi_ref[...] = gi
        gr_ref[...] = gr
        la = (-LRU_C) * gr * _softplus(-ap_ref[...])
        a_s[...] = jnp.exp(la)
        row = lax.broadcasted_iota(jnp.int32, (tm, D), 0) + i * tm
        mult = jnp.where(row == 0, 1.0, jnp.sqrt(_neg_expm1(2.0 * la)))
        b_s[...] = xc * gi * mult

        @pl.when(first)
        def _():
            carry[...] = jnp.zeros_like(carry)

        carry[...] = _scan_tile(a_s, b_s, a_s, carry[...], tm, reverse=False)
        h_ref[...] = a_s[...].astype(BF16)

    row = pl.BlockSpec((tm, D), lambda i: (i, 0))
    f32o, b16o = SDS((t, D), F32), SDS((t, D), BF16)
    ext, tile = pltpu.VMEM((tm + 16, D), F32), pltpu.VMEM((tm, D), F32)
    return pl.pallas_call(
        body, name="mix_pre", grid=(t // tm,),
        in_specs=[_pcol(tm, 0), _pcol(tm, 1), _pcol(tm, 2), _pcol(tm, 3),
                  _prev_halo16(tm, 1), _prev_halo16(tm, 2), _prev_halo16(tm, 3),
                  _full((SMALL_ROWS, D)), _full((1, D)), _full((1, D)), _full((2, HEADS, HD, HD))],
        out_specs=(row,) * 7, out_shape=(b16o, b16o, f32o, b16o, f32o, f32o, b16o),
        scratch_shapes=[ext, ext, tile, tile, pltpu.VMEM((8, D), F32)],
        compiler_params=_cp("arbitrary"),
    )(p, p, p, p, p, p, p, small, rcb, ap, rgw)


def _scan_tile(a_ref, b_ref, o_ref, carry, tm, reverse):
    width = a_ref.shape[1]
    ng = tm // 8
    row8 = lax.broadcasted_iota(jnp.int32, (8, width), 0)

    def step(g, c):
        r = pl.multiple_of((ng - 1 - g if reverse else g) * 8, 8)
        aa = a_ref[pl.ds(r, 8), :]
        bb = b_ref[pl.ds(r, 8), :]
        for s in (1, 2, 4):
            if reverse:
                keep, shift = row8 < 8 - s, 8 - s
            else:
                keep, shift = row8 >= s, s
            a_sh = jnp.where(keep, pltpu.roll(aa, shift, 0), 1.0)
            b_sh = jnp.where(keep, pltpu.roll(bb, shift, 0), 0.0)
            bb = aa * b_sh + bb
            aa = aa * a_sh
        o = aa * c + bb
        o_ref[pl.ds(r, 8), :] = o
        edge = o[0:1, :] if reverse else o[7:8, :]
        return jnp.broadcast_to(edge, (8, width))

    return lax.fori_loop(0, ng, step, carry)


def mix_out(pa, h, p, bm, w3, xf, small):
    t = xf.shape[0]
    tm = _row_tile(t, 256)

    def body(pa_ref, h_ref, yr_ref, gla_ref, glb_ref, bma_ref, bmb_ref, w_ref, x_ref, sm_ref,
             pb_ref, ya_ref, yb_ref, m_ref, xo_ref, xb_ref, xh_ref, rs_ref):
        ge, _ = _gelu(yr_ref[...].astype(F32))
        pb = (h_ref[...].astype(F32) * ge).astype(BF16)
        pb_ref[...] = pb
        ya = _dot(pa_ref[...], w_ref[0], NN)
        yb = _dot(pb, w_ref[1], NN)
        ya_ref[...] = ya.astype(BF16)
        yb_ref[...] = yb.astype(BF16)
        ga = _sigmoid_t(gla_ref[...].astype(F32) + bma_ref[...])
        gb = _sigmoid_t(glb_ref[...].astype(F32) + bmb_ref[...])
        m = (ga * ya + gb * yb).astype(BF16)
        m_ref[...] = m
        z = ALPHA * x_ref[...] + _dot(m, w_ref[2], NN)
        y, xh, rstd = _ln_fwd(z, sm_ref[R_LNG + 1:R_LNG + 2, :], sm_ref[R_LNB + 1:R_LNB + 2, :])
        xo_ref[...] = y
        xb_ref[...] = y.astype(BF16)
        xh_ref[...] = xh
        rs_ref[...] = rstd

    row = pl.BlockSpec((tm, D), lambda i: (i, 0))
    f32o, b16o = SDS((t, D), F32), SDS((t, D), BF16)
    return pl.pallas_call(
        body, name="mix_out", grid=(t // tm,),
        in_specs=[row, row, _pcol(tm, 4), _pcol(tm, 5), _pcol(tm, 6),
                  pl.BlockSpec((1, D), lambda i: (0, 0)), pl.BlockSpec((1, D), lambda i: (0, 1)),
                  _full((3, D, D)), row, _full((SMALL_ROWS, D))],
        out_specs=(row,) * 7 + (pl.BlockSpec((tm, 1), lambda i: (i, 0)),),
        out_shape=(b16o, b16o, b16o, b16o, f32o, b16o, f32o, SDS((t, 1), F32)),
        compiler_params=_cp("parallel"),
    )(pa, h, p, p, p, bm, bm, w3, xf, small)


def mixb_head(dxn, xh, rstd, small, w3, p, bm, ya, yb, ca, h):
    t = dxn.shape[0]
    tm = _row_tile(t, 256)

    def body(dy_ref, xh_ref, rs_ref, sm_ref, w_ref, bg_ref, yr_ref, gla_ref, glb_ref, bma_ref, bmb_ref,
             ya_ref, yb_ref, ca_ref, h_ref,
             dz_ref, dzb_ref, dya_ref, dyb_ref, dbg_ref, dca_ref, dh_ref, dphi_ref, dln_ref, dbm_ref):
        i = pl.program_id(0)
        dy = dy_ref[...]
        xhat = xh_ref[...]
        dz = _ln_bwd(dy, xhat, rs_ref[...], sm_ref[R_LNG + 1:R_LNG + 2, :])

        @pl.when(i == 0)
        def _():
            dln_ref[...] = jnp.zeros_like(dln_ref)
            dbm_ref[...] = jnp.zeros_like(dbm_ref)

        dln_ref[0:1, :] += jnp.sum(dy * xhat, axis=0, keepdims=True)
        dln_ref[1:2, :] += jnp.sum(dy, axis=0, keepdims=True)
        dz_ref[...] = dz
        dzb = dz.astype(BF16)
        dzb_ref[...] = dzb
        dm = _dot(dzb, w_ref[2], NT)
        ga = _sigmoid_t(gla_ref[...].astype(F32) + bma_ref[...])
        gb = _sigmoid_t(glb_ref[...].astype(F32) + bmb_ref[...])
        dya = (dm * ga).astype(BF16)
        dyb = (dm * gb).astype(BF16)
        dya_ref[...] = dya
        dyb_ref[...] = dyb
        dgla = dm * ya_ref[...].astype(F32) * ga * (1.0 - ga)
        dglb = dm * yb_ref[...].astype(F32) * gb * (1.0 - gb)
        dbm_ref[0:1, :] += jnp.sum(dgla, axis=0, keepdims=True)
        dbm_ref[1:2, :] += jnp.sum(dglb, axis=0, keepdims=True)
        dphi_ref[:, D:2 * D] = dgla.astype(BF16)
        dphi_ref[:, 2 * D:3 * D] = dglb.astype(BF16)
        dpa = _dot(dya, w_ref[0], NT)
        dpb = _dot(dyb, w_ref[1], NT)
        dbg_ref[...] = (dpa * ca_ref[...].astype(F32)).astype(BF16)
        dca_ref[...] = dpa * bg_ref[...].astype(F32)
        yr = yr_ref[...].astype(F32)
        ge, th = _gelu(yr)
        dh_ref[...] = (dpb * ge).astype(BF16)
        dphi_ref[:, 0:D] = (dpb * h_ref[...].astype(F32) * _gelu_grad(yr, th)).astype(BF16)

    row = pl.BlockSpec((tm, D), lambda i: (i, 0))
    f32o, b16o = SDS((t, D), F32), SDS((t, D), BF16)
    acc2 = pl.BlockSpec((2, D), lambda i: (0, 0))
    return pl.pallas_call(
        body, name="mixb_head", grid=(t // tm,),
        in_specs=[row, row, pl.BlockSpec((tm, 1), lambda i: (i, 0)), _full((SMALL_ROWS, D)), _full((3, D, D)),
                  _pcol(tm, 0), _pcol(tm, 4), _pcol(tm, 5), _pcol(tm, 6),
                  pl.BlockSpec((1, D), lambda i: (0, 0)), pl.BlockSpec((1, D), lambda i: (0, 1)),
                  row, row, row, row],
        out_specs=(row,) * 7 + (pl.BlockSpec((tm, 3 * D), lambda i: (i, 0)), acc2, acc2),
        out_shape=(f32o, b16o, b16o, b16o, b16o, f32o, b16o, SDS((t, 3 * D), BF16), SDS((2, D), F32), SDS((2, D), F32)),
        compiler_params=_cp("arbitrary"),
    )(dxn, xh, rstd, small, w3, p, p, p, p, bm, bm, ya, yb, ca, h)


def mixb_rec(dh, gr, gi, h, xc, ap, rgw):
    t = dh.shape[0]
    tm = _row_tile(t, 256)
    nt = t // tm

    def body(dh_ref, gr_ref, gi_ref, h_ref, hh_ref, xc_ref, ap_ref, rgw_ref, dg_ref, dxc_ref, red_ref,
             ext, ext_a, c_s, lam_s, lam_c, a_c):
        i = nt - 1 - pl.program_id(0)
        first = i == 0

        @pl.when(pl.program_id(0) == 0)
        def _():
            red_ref[...] = jnp.zeros_like(red_ref)
            lam_c[...] = jnp.zeros_like(lam_c)
            a_c[...] = jnp.zeros_like(a_c)

        ext[0:8, :] = jnp.where(first, 0.0, hh_ref[...].astype(F32)[8:16, :])
        ext[8:, :] = h_ref[...].astype(F32)
        hprev = ext[pl.ds(7, tm), :]
        lam_s[...] = dh_ref[...].astype(F32)
        gr = gr_ref[...]
        gi = gi_ref[...]
        xc = xc_ref[...]
        ap = ap_ref[...]
        sp = _softplus(-ap)
        la = (-LRU_C) * gr * sp
        a = jnp.exp(la)
        ext_a[0:tm, :] = a
        ext_a[tm:, :] = a_c[...]
        c_s[...] = ext_a[pl.ds(1, tm), :]
        lam_c[...] = _scan_tile(c_s, lam_s, lam_s, lam_c[...], tm, reverse=True)
        a_c[...] = jnp.broadcast_to(a[0:1, :], (8, D))
        lam = lam_s[...]
        row = lax.broadcasted_iota(jnp.int32, (tm, D), 0) + i * tm
        start = row == 0
        mult = jnp.where(start, 1.0, jnp.sqrt(_neg_expm1(2.0 * la)))
        dmult = jnp.where(start, 0.0, lam * xc * gi)
        dla = lam * hprev * a - dmult * a * a / mult
        dg1 = (-LRU_C) * sp * dla * gr * (1.0 - gr)
        dg0 = lam * xc * mult * gi * (1.0 - gi)
        dsp = jnp.sum((-LRU_C) * gr * dla, axis=0, keepdims=True)
        red_ref[0:1, :] += jnp.sum(dg0, axis=0, keepdims=True)
        red_ref[1:2, :] += jnp.sum(dg1, axis=0, keepdims=True)
        red_ref[2:3, :] += -dsp * _sigmoid(-ap)
        dg0b = dg0.astype(BF16)
        dg1b = dg1.astype(BF16)
        dg_ref[0] = dg0b
        dg_ref[1] = dg1b
        parts = []
        for hd in range(HEADS):
            sl = slice(hd * HD, (hd + 1) * HD)
            parts.append(_dot(dg0b[:, sl], rgw_ref[0, hd], NT) + _dot(dg1b[:, sl], rgw_ref[1, hd], NT))
        dxc = lam * gi * mult + jnp.concatenate(parts, axis=1)
        dxc_ref[...] = dxc
        red_ref[3:4, :] += jnp.sum(dxc, axis=0, keepdims=True)

    row = pl.BlockSpec((tm, D), lambda i: (nt - 1 - i, 0))
    halo = pl.BlockSpec((16, D), lambda i: (jnp.maximum((nt - 1 - i) * (tm // 16) - 1, 0), 0))
    ext, tile, edge = pltpu.VMEM((tm + 8, D), F32), pltpu.VMEM((tm, D), F32), pltpu.VMEM((8, D), F32)
    return pl.pallas_call(
        body, name="mixb_rec", grid=(nt,),
        in_specs=[row, row, row, row, halo, row, _full((1, D)), _full((2, HEADS, HD, HD))],
        out_specs=(pl.BlockSpec((2, tm, D), lambda i: (0, nt - 1 - i, 0)), row, pl.BlockSpec((8, D), lambda i: (0, 0))),
        out_shape=(SDS((2, t, D), BF16), SDS((t, D), F32), SDS((8, D), F32)),
        scratch_shapes=[ext, ext, tile, tile, edge, edge],
        compiler_params=_cp("arbitrary"),
    )(dh, gr, gi, h, h, xc, ap, rgw)


def mixb_conv(dca, dxc, dbg, p, small):
    t = dca.shape[0]
    tm = _row_tile(t, 256)
    nt = t // tm

    def body(dca_ref, dcan_ref, dxc_ref, dxcn_ref, dbg_ref, cg_ref, v_ref, xr_ref, cgh_ref, vh_ref, xrh_ref, sm_ref,
             dplo_ref, dxr_ref, red_ref, e_dca, e_dxc, e_cv, e_xr):
        i = pl.program_id(0)
        first = i == 0
        last = i == nt - 1

        @pl.when(first)
        def _():
            red_ref[...] = jnp.zeros_like(red_ref)

        dca = dca_ref[...]
        dxc = dxc_ref[...]
        e_dca[0:tm, :] = dca
        e_dca[tm:, :] = jnp.where(last, 0.0, dcan_ref[...])
        e_dxc[0:tm, :] = dxc
        e_dxc[tm:, :] = jnp.where(last, 0.0, dxcn_ref[...])
        cg = cg_ref[...].astype(F32)
        v = v_ref[...].astype(F32)
        xr = xr_ref[...].astype(F32)
        e_cv[0:8, :] = jnp.where(first, 0.0, (cgh_ref[...].astype(F32) * vh_ref[...].astype(F32))[8:16, :])
        e_cv[8:, :] = cg * v
        e_xr[0:8, :] = jnp.where(first, 0.0, xrh_ref[...].astype(F32)[8:16, :])
        e_xr[8:, :] = xr
        dcv = (sm_ref[R_SC + 2:R_SC + 3, :] * dca + sm_ref[R_SC + 1:R_SC + 2, :] * e_dca[pl.ds(1, tm), :]
               + sm_ref[R_SC:R_SC + 1, :] * e_dca[pl.ds(2, tm), :])
        dplo_ref[:, 0:D] = dbg_ref[...]
        dplo_ref[:, D:2 * D] = (dcv * v).astype(BF16)
        dplo_ref[:, 2 * D:3 * D] = (dcv * cg).astype(BF16)
        dxr = (sm_ref[R_RC + 3:R_RC + 4, :] * dxc + sm_ref[R_RC + 2:R_RC + 3, :] * e_dxc[pl.ds(1, tm), :]
               + sm_ref[R_RC + 1:R_RC + 2, :] * e_dxc[pl.ds(2, tm), :] + sm_ref[R_RC:R_RC + 1, :] * e_dxc[pl.ds(3, tm), :])
        dxr_ref[...] = dxr.astype(BF16)
        for k in range(3):
            red_ref[R_SC + k:R_SC + k + 1, :] += jnp.sum(dca * e_cv[pl.ds(6 + k, tm), :], axis=0, keepdims=True)
        for k in range(4):
            red_ref[R_RC + k:R_RC + k + 1, :] += jnp.sum(dxc * e_xr[pl.ds(5 + k, tm), :], axis=0, keepdims=True)

    row = pl.BlockSpec((tm, D), lambda i: (i, 0))
    ext = pltpu.VMEM((tm + 8, D), F32)
    return pl.pallas_call(
        body, name="mixb_conv", grid=(nt,),
        in_specs=[row, _next_halo(tm, t, 0), row, _next_halo(tm, t, 0), row,
                  _pcol(tm, 1), _pcol(tm, 2), _pcol(tm, 3), _prev_halo16(tm, 1), _prev_halo16(tm, 2), _prev_halo16(tm, 3),
                  _full((SMALL_ROWS, D))],
        out_specs=(pl.BlockSpec((tm, 3 * D), lambda i: (i, 0)), row, pl.BlockSpec((8, D), lambda i: (0, 0))),
        out_shape=(SDS((t, 3 * D), BF16), SDS((t, D), BF16), SDS((8, D), F32)),
        scratch_shapes=[ext, ext, ext, ext],
        compiler_params=_cp("arbitrary"),
    )(dca, dca, dxc, dxc, dbg, p, p, p, p, p, p, small)


def mixb_dx(dz, dplo, dxr, dphi, win):
    t = dz.shape[0]
    tm = _row_tile(t, 1024)

    def body(dz_ref, lo_ref, xr_ref, hi_ref, w_ref, dx_ref, acc):
        k = pl.program_id(1)

        @pl.when(k == 0)
        def _():
            acc[...] = ALPHA * dz_ref[...]

        @pl.when(k < 3)
        def _():
            acc[...] += _dot(lo_ref[...], w_ref[...], NT)

        @pl.when(k == 3)
        def _():
            acc[...] += _dot(xr_ref[...], w_ref[...], NT)

        @pl.when(k > 3)
        def _():
            acc[...] += _dot(hi_ref[...], w_ref[...], NT)

        @pl.when(k == 6)
        def _():
            dx_ref[...] = acc[...]

    row = pl.BlockSpec((tm, D), lambda i, k: (i, 0))
    return pl.pallas_call(
        body, name="mixb_dx", grid=(t // tm, 7),
        in_specs=[row, pl.BlockSpec((tm, D), lambda i, k: (i, jnp.minimum(k, 2))), row,
                  pl.BlockSpec((tm, D), lambda i, k: (i, jnp.clip(k - 4, 0, 2))),
                  pl.BlockSpec((D, D), lambda i, k: (0, k))],
        out_specs=row, out_shape=SDS((t, D), F32),
        scratch_shapes=[pltpu.VMEM((tm, D), F32)],
        compiler_params=_cp("parallel", "arbitrary"),
    )(dz, dplo, dxr, dphi, win)


def mixb_dwin(xb, dplo, dxr, dphi):
    t = xb.shape[0]
    tk = _row_tile(t, 2048)
    nk = t // tk

    def body(x_ref, lo_ref, xr_ref, hi_ref, o_ref, acc):
        n = pl.program_id(0)
        k = pl.program_id(1)

        @pl.when(k == 0)
        def _():
            acc[...] = jnp.zeros_like(acc)

        @pl.when(n < 3)
        def _():
            acc[...] += _dot(x_ref[...], lo_ref[...], TN)

        @pl.when(n == 3)
        def _():
            acc[...] += _dot(x_ref[...], xr_ref[...], TN)

        @pl.when(n > 3)
        def _():
            acc[...] += _dot(x_ref[...], hi_ref[...], TN)

        @pl.when(k == nk - 1)
        def _():
            o_ref[...] = acc[...].astype(BF16)

    return pl.pallas_call(
        body, name="mixb_dwin", grid=(7, nk),
        in_specs=[pl.BlockSpec((tk, D), lambda n, k: (k, 0)),
                  pl.BlockSpec((tk, D), lambda n, k: (jnp.where(n < 3, k, 0), jnp.minimum(n, 2))),
                  pl.BlockSpec((tk, D), lambda n, k: (jnp.where(n == 3, k, 0), 0)),
                  pl.BlockSpec((tk, D), lambda n, k: (jnp.where(n > 3, k, 0), jnp.clip(n - 4, 0, 2)))],
        out_specs=pl.BlockSpec((D, D), lambda n, k: (0, n)), out_shape=SDS((D, D_IN), BF16),
        scratch_shapes=[pltpu.VMEM((D, D), F32)],
        compiler_params=_cp("parallel", "arbitrary"),
    )(xb, dplo, dxr, dphi)


def mixb_drgw(xcb, dg):
    t = xcb.shape[0]
    return _mm_tn("mixb_drgw", xcb, pl.BlockSpec((t, HD), lambda g, h: (0, h)),
                  dg, pl.BlockSpec((None, t, HD), lambda g, h: (g, 0, h)),
                  SDS((2, HEADS, HD, HD), BF16), pl.BlockSpec((None, None, HD, HD), lambda g, h: (g, h, 0, 0)),
                  (2, HEADS))


def loss_head(y, tgt):
    t = y.shape[0]
    tm = _row_tile(t, 512)

    def body(y_ref, t_ref, dy_ref, l_ref):
        i = pl.program_id(0)
        e = y_ref[...] - t_ref[...]
        dy_ref[...] = e * (1.0 / D)

        @pl.when(i == 0)
        def _():
            l_ref[...] = jnp.zeros_like(l_ref)

        l_ref[...] += 0.5 * jnp.sum(jnp.mean(e * e, axis=-1, keepdims=True), axis=0, keepdims=True)

    row = pl.BlockSpec((tm, D), lambda i: (i, 0))
    return pl.pallas_call(
        body, name="loss_head", grid=(t // tm,), in_specs=[row, row],
        out_specs=(row, pl.BlockSpec((8, 128), lambda i: (0, 0))),
        out_shape=(SDS((t, D), F32), SDS((8, 128), F32)),
        compiler_params=_cp("arbitrary"),
    )(y, tgt)


def _adamw(w, g, m, v):
    m = ADAM_B1 * m + (1.0 - ADAM_B1) * g
    v = ADAM_B2 * v + (1.0 - ADAM_B2) * (g * g)
    m_hat = m / (1.0 - ADAM_B1 ** ADAM_STEP)
    v_hat = v / (1.0 - ADAM_B2 ** ADAM_STEP)
    delta = -ADAM_LR * (m_hat / (jnp.sqrt(v_hat) + ADAM_EPS) + ADAM_WD * w)
    return delta, m, v


def adam_big(name, w, m, v, parts, rows, lanes, blk_off, tr, l0, prev=None):
    nr = rows // tr
    nl = len(parts)

    def body(w_ref, m_ref, v_ref, *rest):
        g_ref, d_ref, mo_ref, vo_ref = rest[-4:]
        l = pl.program_id(0)
        for ll in range(nl):
            pr = rest[ll]

            @pl.when(l == ll)
            def _():
                g = pr[0].astype(F32)
                for s in range(1, N_DEV):
                    g = g + pr[s].astype(F32)
                g_ref[...] = g
                d, mn, vn = _adamw(w_ref[...], g, m_ref[...], v_ref[...])
                d_ref[...] = d
                mo_ref[...] = mn
                vo_ref[...] = vn

    blk = pl.BlockSpec((None, tr, lanes), lambda l, r: (l + l0, r, 0))

    def part_spec(ll):
        return pl.BlockSpec((N_DEV, tr, lanes), lambda l, r: (0, jnp.where(l == ll, r, 0) + blk_off, 0))

    out = SDS(w.shape, F32)
    extra = [] if prev is None else list(prev)
    return pl.pallas_call(
        body, name=name, grid=(nl, nr),
        in_specs=[blk, blk, blk] + [part_spec(ll) for ll in range(nl)] + [ANY] * len(extra),
        out_specs=(blk,) * 4, out_shape=(out,) * 4,
        input_output_aliases={3 + nl + i: i for i in range(len(extra))},
        compiler_params=_cp("parallel", "parallel"),
    )(w, m, v, *parts, *extra)


def small_update(me1, land_sh, land_rep, sharded, replicated):
    ns, nr = len(sharded), len(replicated)

    def body(me_ref, lsh, lrep, *refs):
        ins, outs = refs[:3 * (ns + nr)], refs[3 * (ns + nr):]

        def total(read):
            g = read(0)
            for s in range(1, N_DEV):
                g = g + read(s)
            return g

        def update(k, g, sl):
            w_ref, m_ref, v_ref = ins[3 * k:3 * k + 3]
            d, mn, vn = _adamw(w_ref[sl], g, m_ref[sl], v_ref[sl])
            for o, val in zip(outs[4 * k:4 * k + 4], (g, d, mn, vn)):
                o[sl] = val

        for k, (r0, n) in enumerate(GS_ROWS):
            update(k, total(lambda s: lsh[s, :, r0:r0 + n, :]), (slice(None),) * 3)
        lo, hi = (slice(None), slice(0, D)), (slice(None), slice(D, 2 * D))
        update(ns, total(lambda s: lrep[s, 0:4, :]), lo)
        update(ns, total(lambda s: lrep[s, 4:8, :]), hi)
        update(ns + 1, total(lambda s: lrep[s, 8:12, :]), (slice(None),) * 2)
        update(ns + 2, total(lambda s: lrep[s, 12:16, :]), (slice(None),) * 2)

    def whole(a):
        nd = a.ndim
        return pl.BlockSpec(a.shape, lambda i, me: (0,) * nd)

    params = [a for wmv in list(sharded) + list(replicated) for a in wmv]
    out_shape = tuple(SDS(wmv[0].shape, F32) for wmv in list(sharded) + list(replicated) for _ in range(4))
    res = pl.pallas_call(
        body, name="small_update", out_shape=out_shape,
        grid_spec=pltpu.PrefetchScalarGridSpec(
            num_scalar_prefetch=1, grid=(1,),
            in_specs=[pl.BlockSpec((N_DEV, DEPTH, 16, 128), lambda i, me: (0, 0, 0, me[0])), whole(land_rep)]
            + [whole(a) for a in params],
            out_specs=tuple(pl.BlockSpec(s.shape, lambda i, me, nd=len(s.shape): (0,) * nd) for s in out_shape)),
        compiler_params=_cp("arbitrary"),
    )(me1, land_sh, land_rep, *params)
    return [list(res[4 * k:4 * k + 4]) for k in range(ns + nr)]


def _ffn_forward(xf, xb, gw, f, s):
    g, u, a = ffn_up(xb, gw["w1"], f)
    xo, xob, xh, rs = ffn_down_ln(a, gw["w2"], f, xf, gw["small"], s)
    return (xo, xob), dict(xb=xb, g=g, u=u, a=a, xh=xh, rs=rs)


def _ffn_backward(dxn, sv, gw, f, s):
    dz, df, dgu, dln = ffn_bwd_gates(dxn, sv["xh"], sv["rs"], gw["small"], s, gw["w2"], f, sv["g"], sv["u"])
    dx = ffn_bwd_dx(dz, dgu, gw["w1"], f)
    dw1 = ffn_dw1(sv["xb"], dgu, f)
    dw2 = ffn_dw2(sv["a"], df, f)
    return dx, dw1, dw2, dln


def _mixer_forward(xf, xb, gw, rcb, ap, bm):
    p = mix_proj(xb, gw["win"])
    ca, pa, xc, xcb, gi, gr, h = mix_pre(p, gw["small"], rcb, ap, gw["rgw"])
    pb, ya, yb, m, xo, xob, xh, rs = mix_out(pa, h, p, bm, gw["w3"], xf, gw["small"])
    sv = dict(xb=xb, p=p, ca=ca, pa=pa, xc=xc, xcb=xcb, gi=gi, gr=gr, h=h, pb=pb, ya=ya, yb=yb, m=m, xh=xh, rs=rs)
    return (xo, xob), sv


def _mixer_backward(dxn, sv, gw, rcb, ap, bm):
    dz, dzb, dya, dyb, dbg, dca, dh, dphi, dln, dbm = mixb_head(
        dxn, sv["xh"], sv["rs"], gw["small"], gw["w3"], sv["p"], bm, sv["ya"], sv["yb"], sv["ca"], sv["h"])
    dg, dxc, red_rec = mixb_rec(dh, sv["gr"], sv["gi"], sv["h"], sv["xc"], ap, gw["rgw"])
    dplo, dxr, red_conv = mixb_conv(dca, dxc, dbg, sv["p"], gw["small"])
    dx = mixb_dx(dz, dplo, dxr, dphi, gw["win"])
    dwin = mixb_dwin(sv["xb"], dplo, dxr, dphi)
    dwo = mm_tn_square("mixb_dwo", sv["m"], dzb)
    dwoc = mm_tn_square("mixb_dwoc", sv["pa"], dya)
    dwor = mm_tn_square("mixb_dwor", sv["pb"], dyb)
    drgw = mixb_drgw(sv["xcb"], dg)
    return dx, dict(dwin=dwin, dwoc=dwoc, dwor=dwor, dwo=dwo, drgw=drgw), dln, dbm, red_rec, red_conv


def kernel(x, w_in, b_merge, sc_w, rc_w, rc_b, rg_w, rg_b, a_param, w_out_conv, w_out_rnn, w_o, ffn_w1, ffn_w2, ln_g, ln_b, loss_target, m_w_in, m_b_merge, m_sc_w, m_rc_w, m_rc_b, m_rg_w, m_rg_b, m_a_param, m_w_out_conv, m_w_out_rnn, m_w_o, m_ffn_w1, m_ffn_w2, m_ln_g, m_ln_b, v_w_in, v_b_merge, v_sc_w, v_rc_w, v_rc_b, v_rg_w, v_rg_b, v_a_param, v_w_out_conv, v_w_out_rnn, v_w_o, v_ffn_w1, v_ffn_w2, v_ln_g, v_ln_b):
    t = x.shape[1]
    me = _me()

    def rows(parts, total):
        out, off = None, 0
        for part in parts:
            r = part.shape[-2]
            pad = [(0, 0)] * (part.ndim - 2) + [(off, total - off - r), (0, 0)]
            padded = jnp.pad(part, pad)
            out = padded if out is None else out + padded
            off += r
        return out

    small = rows([sc_w, rc_w, rg_b, ln_g, ln_b], SMALL_ROWS)

    me1 = jnp.reshape(me, (1,)).astype(jnp.int32)

    def layer_shards(l, small_now):
        return gather_prep(l, me1, w_in, w_out_conv, w_out_rnn, w_o, ffn_w1, ffn_w2, rg_w, small_now)

    def as_weights(lands):
        return dict(zip(("win", "w3", "w1", "w2", "rgw", "small"), lands))

    xf = x.reshape(t, D)
    xb = xf.astype(BF16)

    early_at, rest_at = [2, 3, 5], [0, 1, 2, 3, 4]

    def ffn_slots(f, o_w1, o_w2, p):
        return [o_w1.at[f, p], o_w2.at[f, _slab(p, W2S, 16), :]]

    def early_slots(lands, p):
        return ffn_slots(0, lands[0], lands[1], p) + [lands[2].at[:, _slab(p, 128, 128)]]

    def rest_slots(lands, p):
        o_win, o_w3, o_w1, o_w2, o_rgw = lands
        return ([o_win.at[:, _slab(p, WIN_S, 128)], o_w3.at[:, _slab(p, 128, 128), :]] + ffn_slots(1, o_w1, o_w2, p)
                + [o_rgw.at[:, :, _slab(p, 32, 32), :]])

    early = ChipGather("gather0a", 3, 3, 3, lambda refs: [refs[0].at[0], refs[1].at[0], refs[2]], early_slots)
    rest = ChipGather("gather0b", 5, 5, 5, lambda refs: [refs[0], refs[1], refs[2].at[1], refs[3].at[1], refs[4]],
                      rest_slots)
    shards, own = layer_shards(0, small)

    def step(method, at, *args):
        res = method(*args[:-1], [shards[i] for i in at], [own[i] for i in at], args[-1])
        for i, s_new, o_new in zip(at, res[-3], res[-2]):
            shards[i], own[i] = s_new, o_new
        return tuple(res[:-3]) + (res[-1],)

    sems_a, thru = step(early.first, early_at, [xb, small])
    sems_b, thru = step(rest.first, rest_at, thru)
    relay_a, thru = step(early.relay, early_at, sems_a, thru)
    ((xb, small),) = step(early.last, early_at, sems_a, relay_a, thru)
    cur, s0 = _ffn_forward(xf, xb, as_weights(own), 0, 0)
    relay_b, thru = step(rest.relay, rest_at, sems_b, list(cur) + [small])
    (thru,) = step(rest.last, rest_at, sems_b, relay_b, thru)
    cur, small = thru[:2], thru[2]

    gathers = [ChipGather(f"gather{l}") for l in range(DEPTH)]
    gws, saved = [as_weights(own)], []
    for l in range(DEPTH):
        gw = gws[l]
        nxt = l + 1 < DEPTH
        if nxt:
            shards, own = layer_shards(l + 1, small)
            sems, shards, own, thru = gathers[l + 1].first(shards, own, list(cur) + [small])
            cur, small = thru[:2], thru[2]
        rcb, ap, bm = rc_b[l][None], a_param[l][None], b_merge[l][None]
        if l > 0:
            cur, s0 = _ffn_forward(cur[0], cur[1], gw, 0, 0)
        cur, s1 = _mixer_forward(cur[0], cur[1], gw, rcb, ap, bm)
        if nxt and l > 0:
            relay_sems, shards, own, cur = gathers[l + 1].relay(sems, shards, own, cur)
        cur, s2 = _ffn_forward(cur[0], cur[1], gw, 1, 2)
        if nxt and l == 0:
            relay_sems, shards, own, cur = gathers[l + 1].relay(sems, shards, own, cur)
        saved.append((s0, s1, s2))
        if nxt:
            _, own, cur = gathers[l + 1].last(sems, relay_sems, shards, own, cur)
            gws.append(as_weights(own))

    dy, loss_tile = loss_head(cur[0], loss_target.reshape(t, D))
    loss = lax.psum(loss_tile[0, 0], ("x", "y", "c"))

    lands = [dict() for _ in range(DEPTH)]
    gsmall, grep = [None] * DEPTH, [None] * DEPTH
    flights = {}

    def launch(key, kinds, grads, dy):
        ex, prep = grad_scatter(f"scatter{key}", kinds)
        fl = ex.start(grads, prep(me1, grads), [] if dy is None else [dy])
        flights[key] = (ex, fl)
        return None if dy is None else fl[4][0]

    def landed(key, thru):
        ex, fl = flights.pop(key)
        return ex.wait(fl[0], fl[1], fl[2], fl[3], thru)

    ffn_b, mixer, ffn_a = ("w1b", "w2b"), ("w_in", "w3", "rgw"), ("w1a", "w2a")
    SPLIT = (0, 1)
    for l in reversed(range(DEPTH)):
        gw = gws[l]
        rcb, ap, bm = rc_b[l][None], a_param[l][None], b_merge[l][None]
        s0, s1, s2 = saved[l]
        dy, dw1b, dw2b, dln2 = _ffn_backward(dy, s2, gw, 1, 2)
        if l in SPLIT:
            dy = launch(f"{l}b", ["ffn"], [dw1b, dw2b], dy)
        dy, dmix, dln1, dbm, red_rec, red_conv = _mixer_backward(dy, s1, gw, rcb, ap, bm)
        mixer_grads = [dmix["dwin"], dmix["dwoc"], dmix["dwor"], dmix["dwo"], dmix["drgw"]]
        if l in SPLIT:
            dy = launch(f"{l}m", ["mixer"], mixer_grads, dy)
        dy, dw1a, dw2a, dln0 = _ffn_backward(dy, s0, gw, 0, 0)
        if l + 1 < DEPTH and l + 1 not in SPLIT:
            got, (dy,) = landed(str(l + 1), [dy])
            lands[l + 1] = dict(zip(ffn_b + mixer + ffn_a, got))
        if l == 0:
            lands[DEPTH - 1]["w_in"] = launch("0a", ["ffn"], [dw1a, dw2a], lands[DEPTH - 1]["w_in"])
        elif l in SPLIT:
            dy = launch(f"{l}a", ["ffn"], [dw1a, dw2a], dy)
        else:
            dy = launch(str(l), ["ffn", "mixer", "ffn"], [dw1b, dw2b] + mixer_grads + [dw1a, dw2a], dy)
        gsmall[l] = rows([red_conv[0:8], red_rec[0:2], dln0[0:1], dln1[0:1], dln2[0:1], dln0[1:2], dln1[1:2],
                          dln2[1:2]], 16)
        grep[l] = [dbm[0:1], dbm[1:2], red_rec[3:4], red_rec[2:3]]
    grad_x = dy.reshape(1, t, D)

    g_rep = rows([grep[l][kind] for kind in range(4) for l in range(DEPTH)], 16)
    land_sh, land_rep = gather_small_grads(jnp.stack(gsmall, axis=0), g_rep)
    small_names = ("sc_w", "rc_w", "rg_b", "ln_g", "ln_b", "b_merge", "rc_b", "a_param")
    small_res = small_update(
        me1, land_sh, land_rep,
        [(sc_w, m_sc_w, v_sc_w), (rc_w, m_rc_w, v_rc_w), (rg_b, m_rg_b, v_rg_b), (ln_g, m_ln_g, v_ln_g),
         (ln_b, m_ln_b, v_ln_b)],
        [(b_merge, m_b_merge, v_b_merge), (rc_b, m_rc_b, v_rc_b), (a_param, m_a_param, v_a_param)])

    def parts_of(layers):
        ls = [lands[l] for l in layers]
        return dict(w_in=[a["w_in"] for a in ls],
                    w3=[a["w3"].reshape(N_DEV, 3 * 128, D) for a in ls],
                    rgw=[a["rgw"].reshape(N_DEV, 2 * HEADS * 32, HD) for a in ls],
                    w1=[a[k] for a in ls for k in ("w1a", "w1b")],
                    w2=[a[k] for a in ls for k in ("w2a", "w2b")])

    families = [("w_in", w_in, m_w_in, v_w_in, "w_in", 1, D, WIN_S, 0, 128),
                ("w_out_conv", w_out_conv, m_w_out_conv, v_w_out_conv, "w3", 1, 128, D, 0, 128),
                ("w_out_rnn", w_out_rnn, m_w_out_rnn, v_w_out_rnn, "w3", 1, 128, D, 1, 128),
                ("w_o", w_o, m_w_o, v_w_o, "w3", 1, 128, D, 2, 128),
                ("ffn_w1", ffn_w1, m_ffn_w1, v_ffn_w1, "w1", 2, D, FS, 0, 256),
                ("ffn_w2", ffn_w2, m_ffn_w2, v_ffn_w2, "w2", 2, W2S, D, 0, W2S // 2),
                ("rg_w", rg_w, m_rg_w, v_rg_w, "rgw", 1, 2 * HEADS * 32, HD, 0, 256)]

    def adam_pass(tag, layers, prev):
        parts = parts_of(layers)
        outs = {}
        for name, w, m, v, fam, per, nrow, lanes, blk_off, tr in families:
            r3 = lambda a: a.reshape(DEPTH * per, nrow, lanes)
            outs[name] = adam_big(f"adam_{name}_{tag}", r3(w), r3(m), r3(v), parts[fam], nrow, lanes, blk_off, tr,
                                  per * layers[0], None if prev is None else prev[name])
        return outs

    def through(outs, key, extra):
        thru = [outs[f[0]][0] for f in families] + list(extra)
        got, thru = landed(key, thru)
        outs = {f[0]: [thru[i]] + list(outs[f[0]][1:]) for i, f in enumerate(families)}
        return got, outs, thru[len(families):]

    done = adam_pass("top", list(range(2, DEPTH)), None)
    for l, tag in ((1, "second"), (0, "first")):
        for part, names in (("b", ffn_b), ("m", mixer), ("a", ffn_a)):
            got, done, _ = through(done, f"{l}{part}", [])
            lands[l].update(zip(names, got))
        done = adam_pass(tag, [l], done)
    res = {f[0]: [o.reshape(f[1].shape) for o in done[f[0]]] for f in families}

    res.update(zip(small_names, small_res))

    names = ["w_in", "b_merge", "sc_w", "rc_w", "rc_b", "rg_w", "rg_b", "a_param", "w_out_conv", "w_out_rnn", "w_o",
             "ffn_w1", "ffn_w2", "ln_g", "ln_b"]
    out = [loss, grad_x]
    for k in range(4):
        out += [res[n][k] for n in names]
    return tuple(out)
```

```python
import functools

import jax
import jax.numpy as jnp
from jax import lax
from jax.experimental import pallas as pl
from jax.experimental.pallas import tpu as pltpu

F32 = jnp.float32
BF16 = jnp.bfloat16
SDS = jax.ShapeDtypeStruct

N_DEV = 8
DEPTH = 4
D = 1024
D_FF = 2816
FS = D_FF // 4
W2S = D_FF // 8
D_IN = 7 * D
WIN_S = D_IN // 8
HEADS = 4
HD = D // HEADS
LRU_C = 8.0
ALPHA = (2.0 * DEPTH) ** 0.25
LN_EPS = 1e-5
ADAM_LR, ADAM_B1, ADAM_B2, ADAM_EPS, ADAM_WD, ADAM_STEP = 0.001, 0.9, 0.999, 1e-08, 0.01, 10

R_SC, R_RC, R_RGB, R_LNG, R_LNB = 0, 3, 7, 9, 12
SMALL_ROWS = 16
GS_ROWS = ((0, 3), (3, 4), (8, 2), (10, 3), (13, 3))

NN = ((1,), (0,))
NT = ((1,), (1,))
TN = ((0,), (0,))
MESH = pl.DeviceIdType.MESH
ANY = pl.BlockSpec(memory_space=pl.ANY)
VMEM_LIMIT = 52 * 1024 * 1024


def _dot(a, b, dims):
    return lax.dot_general(a, b, (dims, ((), ())), preferred_element_type=F32)


def _cp(*sem):
    return pltpu.CompilerParams(dimension_semantics=sem, vmem_limit_bytes=VMEM_LIMIT)


def _sigmoid(x):
    return 1.0 / (1.0 + jnp.exp(-x))


def _sigmoid_t(x):
    return 0.5 * jnp.tanh(0.5 * x) + 0.5


def _log1p(e):
    u = 1.0 + e
    return jnp.where(u == 1.0, e, jnp.log(u) * e / jnp.where(u == 1.0, 1.0, u - 1.0))


def _softplus(x):
    return jnp.maximum(x, 0.0) + _log1p(jnp.exp(-jnp.abs(x)))


def _neg_expm1(x):
    u = jnp.exp(x)
    um1 = u - 1.0
    safe = jnp.logical_and(u != 1.0, um1 != -1.0)
    r = um1 * x / jnp.where(safe, jnp.log(jnp.where(safe, u, 0.5)), 1.0)
    return -jnp.where(u == 1.0, x, jnp.where(um1 == -1.0, -1.0, r))


def _gelu(y):
    c = 0.7978845608028654
    t = jnp.tanh(c * (y + 0.044715 * y * y * y))
    return 0.5 * y * (1.0 + t), t


def _gelu_grad(y, t):
    c = 0.7978845608028654
    return 0.5 * (1.0 + t) + 0.5 * y * (1.0 - t * t) * c * (1.0 + 3.0 * 0.044715 * y * y)


def _ln_fwd(z, g, b):
    mu = jnp.mean(z, axis=-1, keepdims=True)
    zc = z - mu
    var = jnp.mean(zc * zc, axis=-1, keepdims=True)
    rstd = lax.rsqrt(var + LN_EPS)
    xh = zc * rstd
    return xh * g + b, xh, rstd


def _ln_bwd(dy, xh, rstd, g):
    dxh = dy * g
    m1 = jnp.mean(dxh, axis=-1, keepdims=True)
    m2 = jnp.mean(dxh * xh, axis=-1, keepdims=True)
    return rstd * (dxh - m1 - xh * m2)


def _row_tile(t, want):
    return min(want, t)


def _me():
    return 4 * lax.axis_index("x") + 2 * lax.axis_index("y") + lax.axis_index("c")


def _coords(p):
    return (p // 4, (p // 2) % 2, p % 2)


def _all_to_all_copies(srcs_of, dsts_of, waits, send_sems, recv_sems, loc_sems):
    me = _me()
    n = len(waits)
    own_src, own_dst = srcs_of(me), dsts_of(me)
    local = [pltpu.make_async_copy(own_src[k], own_dst[k], loc_sems.at[k]) for k in range(n)]
    for cp in local:
        cp.start()
    for d in range(1, N_DEV):
        p = (me + d) % N_DEV
        src, dst = srcs_of(p), dsts_of(me)
        for k in range(n):
            pltpu.make_async_remote_copy(
                src_ref=src[k], dst_ref=dst[k], send_sem=send_sems.at[waits[k][1]],
                recv_sem=recv_sems.at[waits[k][1]], device_id=_coords(p), device_id_type=MESH).start()
    done = set()
    for k in range(n):
        ref, s = waits[k]
        if s in done:
            continue
        done.add(s)
        pltpu.make_async_remote_copy(
            src_ref=ref, dst_ref=ref, send_sem=send_sems.at[s], recv_sem=recv_sems.at[s],
            device_id=_coords(me), device_id_type=MESH).wait()
    for cp in local:
        cp.wait()


HBM = pl.BlockSpec(memory_space=pltpu.HBM)
SEM = pl.BlockSpec(memory_space=pltpu.SEMAPHORE)
EFFECT = pltpu.SideEffectType.DATAFLOW_SIDE_EFFECTING


def _in_hbm(a):
    return pltpu.with_memory_space_constraint(a, pltpu.HBM)


class Exchange:
    def __init__(self, name, src_of, dst_of, sem_of, span_of):
        self.name, self.src_of, self.dst_of, self.sem_of, self.span_of = name, src_of, dst_of, sem_of, span_of
        self.nsem = max(sem_of) + 1

    def start(self, srcs, lands, thru):
        n, m = len(srcs), len(lands)
        ops = list(srcs) + list(lands) + list(thru)

        def body(*refs):
            src_refs, land_refs = refs[:n], refs[n:n + m]
            send_sems, recv_sems = refs[len(ops)], refs[len(ops) + 1]
            me = _me()
            for dd in range(1, N_DEV):
                p = (me + dd) % N_DEV
                s, d = self.src_of(src_refs, p), self.dst_of(land_refs, me)
                for k in range(len(self.sem_of)):
                    pltpu.make_async_remote_copy(
                        src_ref=s[k], dst_ref=d[k], send_sem=send_sems.at[self.sem_of[k]],
                        recv_sem=recv_sems.at[self.sem_of[k]], device_id=_coords(p), device_id_type=MESH).start()

        sem = pltpu.SemaphoreType.DMA((self.nsem,))
        res = pl.pallas_call(
            body, name=self.name + "_start",
            out_shape=(sem, sem) + tuple(pltpu.HBM(a.shape, a.dtype) for a in ops),
            in_specs=[HBM] * len(ops), out_specs=(SEM, SEM) + (HBM,) * len(ops),
            input_output_aliases={i: 2 + i for i in range(len(ops))},
            compiler_params=pltpu.CompilerParams(has_side_effects=EFFECT),
        )(*[_in_hbm(a) for a in ops])
        return res[0], res[1], res[2:2 + n], res[2 + n:2 + n + m], list(res[2 + n + m:])

    def wait(self, send_sems, recv_sems, srcs, lands, thru):
        n, m = len(srcs), len(lands)
        ops = list(srcs) + list(lands) + list(thru)

        def body(*refs):
            land_refs = refs[n:n + m]
            ssem, rsem = refs[len(ops)], refs[len(ops) + 1]
            me = _me()
            spans = self.span_of(land_refs)
            for s in range(self.nsem):
                cp = pltpu.make_async_remote_copy(
                    src_ref=spans[s], dst_ref=spans[s], send_sem=ssem.at[s], recv_sem=rsem.at[s],
                    device_id=_coords(me), device_id_type=MESH)
                cp.wait_send()
                cp.wait_recv()

        res = pl.pallas_call(
            body, name=self.name + "_wait",
            out_shape=tuple(pltpu.HBM(a.shape, a.dtype) for a in ops),
            in_specs=[HBM] * len(ops) + [SEM, SEM], out_specs=(HBM,) * len(ops),
            input_output_aliases={i: i for i in range(len(ops))},
            compiler_params=pltpu.CompilerParams(has_side_effects=EFFECT),
        )(*ops, send_sems, recv_sems)
        return res[n:n + m], list(res[n + m:])


class ChipGather:
    def __init__(self, name, nsrc=6, nland=6, ncopy=6, views=None, slots=None):
        self.name, self.nsrc, self.nland, self.ncopy = name, nsrc, nland, ncopy
        self.views = views if views is not None else list
        self.slots = slots if slots is not None else self.layer_slots

    @staticmethod
    def layer_slots(lands, p):
        o_win, o_w3, o_w1, o_w2, o_rgw, o_sm = lands
        return [o_win.at[:, pl.ds(pl.multiple_of(p * WIN_S, 128), WIN_S)],
                o_w3.at[:, pl.ds(pl.multiple_of(p * 128, 128), 128), :],
                o_w1.at[:, p],
                o_w2.at[:, pl.ds(pl.multiple_of(p * W2S, 16), W2S), :],
                o_rgw.at[:, :, pl.ds(pl.multiple_of(p * 32, 32), 32), :],
                o_sm.at[:, pl.ds(pl.multiple_of(p * 128, 128), 128)]]

    @staticmethod
    def _places():
        x, y, c = lax.axis_index("x"), lax.axis_index("y"), lax.axis_index("c")
        chips = [(1 - x, y), (x, 1 - y), (1 - x, 1 - y)]
        return (x, y, c), (x, y, 1 - c), chips

    @staticmethod
    def _index(place):
        return 4 * place[0] + 2 * place[1] + place[2]

    def _call(self, tag, body, ops, sems_in, sems_out):
        nops = len(ops)
        sem = pltpu.SemaphoreType.DMA((max(sems_out, 1),))
        res = pl.pallas_call(
            body, name=f"{self.name}_{tag}",
            out_shape=((sem, sem) if sems_out else ()) + tuple(pltpu.HBM(a.shape, a.dtype) for a in ops),
            in_specs=[HBM] * nops + [SEM] * len(sems_in),
            out_specs=((SEM, SEM) if sems_out else ()) + (HBM,) * nops,
            input_output_aliases={i: (2 if sems_out else 0) + i for i in range(nops)},
            compiler_params=pltpu.CompilerParams(has_side_effects=EFFECT),
        )(*[_in_hbm(a) for a in ops], *sems_in)
        return res

    def first(self, srcs, lands, thru):
        ns, nl, n = self.nsrc, self.nland, self.ncopy
        ops = list(srcs) + list(lands) + list(thru)

        def body(*refs):
            src_refs, land_refs = self.views(refs[:ns]), refs[ns:ns + nl]
            send_sems, recv_sems = refs[len(ops)], refs[len(ops) + 1]
            me, sibling, chips = self._places()
            dst = self.slots(land_refs, self._index(me))
            targets = [sibling] + [(cx, cy, me[2]) for cx, cy in chips]
            for r, to in enumerate(targets):
                for k in range(n):
                    pltpu.make_async_remote_copy(
                        src_ref=src_refs[k], dst_ref=dst[k], send_sem=send_sems.at[n *r + k],
                        recv_sem=recv_sems.at[n *r + k], device_id=to, device_id_type=MESH).start()

        res = self._call("first", body, ops, [], 4 * n)
        return (res[0], res[1]), list(res[2:2 + ns]), list(res[2 + ns:2 + ns + nl]), list(res[2 + ns + nl:])

    def relay(self, sems, srcs, lands, thru):
        ns, nl, n = self.nsrc, self.nland, self.ncopy
        ops = list(srcs) + list(lands) + list(thru)

        def wait_body(*refs):
            land_refs = refs[ns:ns + nl]
            ssem, rsem = refs[len(ops)], refs[len(ops) + 1]
            me, sibling, chips = self._places()
            for j, (cx, cy) in enumerate(chips):
                got = self.slots(land_refs, self._index((cx, cy, me[2])))
                for k in range(n):
                    pltpu.make_async_remote_copy(
                        src_ref=got[k], dst_ref=got[k], send_sem=ssem.at[n *(1 + j) + k], recv_sem=rsem.at[n *(1 + j) + k],
                        device_id=me, device_id_type=MESH).wait_recv()

        ops = list(self._call("landed", wait_body, ops, list(sems), 0))

        def start_body(*refs):
            land_refs = refs[ns:ns + nl]
            ssem, rsem = refs[len(ops)], refs[len(ops) + 1]
            me, sibling, chips = self._places()
            for j, (cx, cy) in enumerate(chips):
                got = self.slots(land_refs, self._index((cx, cy, me[2])))
                for k in range(n):
                    pltpu.make_async_remote_copy(
                        src_ref=got[k], dst_ref=got[k], send_sem=ssem.at[n *j + k], recv_sem=rsem.at[n *j + k],
                        device_id=sibling, device_id_type=MESH).start()

        res = self._call("relay", start_body, ops, [], 3 * n)
        return (res[0], res[1]), list(res[2:2 + ns]), list(res[2 + ns:2 + ns + nl]), list(res[2 + ns + nl:])

    def last(self, sems, relay_sems, srcs, lands, thru):
        ns, nl, n = self.nsrc, self.nland, self.ncopy
        ops = list(srcs) + list(lands) + list(thru)

        def body(*refs):
            src_refs, land_refs = self.views(refs[:ns]), refs[ns:ns + nl]
            ssem, rsem, ssem2, rsem2 = refs[len(ops):len(ops) + 4]
            me, sibling, chips = self._places()
            got = self.slots(land_refs, self._index(sibling))
            for k in range(n):
                pltpu.make_async_remote_copy(
                    src_ref=got[k], dst_ref=got[k], send_sem=ssem.at[k], recv_sem=rsem.at[k],
                    device_id=me, device_id_type=MESH).wait_recv()
            for j, (cx, cy) in enumerate(chips):
                got = self.slots(land_refs, self._index((cx, cy, 1 - me[2])))
                for k in range(n):
                    pltpu.make_async_remote_copy(
                        src_ref=got[k], dst_ref=got[k], send_sem=ssem2.at[n *j + k], recv_sem=rsem2.at[n *j + k],
                        device_id=me, device_id_type=MESH).wait_recv()
            for sem_s, sem_r, count in ((ssem, rsem, 4), (ssem2, rsem2, 3)):
                for r in range(count):
                    for k in range(n):
                        pltpu.make_async_remote_copy(
                            src_ref=src_refs[k], dst_ref=src_refs[k], send_sem=sem_s.at[n *r + k],
                            recv_sem=sem_r.at[n *r + k], device_id=me, device_id_type=MESH).wait_send()

        res = self._call("last", body, ops, list(sems) + list(relay_sems), 0)
        return list(res[:ns]), list(res[ns:ns + nl]), list(res[ns + nl:])


GATHER_SHAPES = (SDS((D, D_IN), BF16), SDS((3, D, D), BF16), SDS((2, N_DEV, D, FS), BF16),
                 SDS((2, D_FF, D), BF16), SDS((2, HEADS, HD, HD), BF16), SDS((SMALL_ROWS, D), F32))


def _slab(p, n, align):
    return pl.ds(pl.multiple_of(p * n, align), n)


class GradGroup:
    def __init__(self, nsrc, lands, src_of, dst_of, sem_of, in_specs, out_specs, copy):
        self.nsrc, self.lands, self.src_of, self.dst_of, self.sem_of = nsrc, lands, src_of, dst_of, sem_of
        self.in_specs, self.out_specs, self.copy = in_specs, out_specs, copy


def _mixer_group():
    hd = D // 2

    def src_of(refs, p):
        dwin, dwoc, dwor, dwo, drgw = refs
        r3 = _slab(p, 128, 128)
        return [dwin.at[:, _slab(p, WIN_S, 128)], dwoc.at[r3, :], dwor.at[r3, :], dwo.at[r3, :],
                drgw.at[:, :, _slab(p, 32, 32), :]]

    def dst_of(lands, p):
        l_win, l_w3, l_rgw = lands
        return [l_win.at[p], l_w3.at[p, 0], l_w3.at[p, 1], l_w3.at[p, 2], l_rgw.at[p]]

    def copy(srcs, lands):
        lands[0][...] = srcs[0][...]
        for k in range(3):
            lands[1][k] = srcs[1 + k][...]
        lands[2][...] = srcs[4][...]

    three = pl.BlockSpec((64, D), lambda i, me: (2 * me[0] + i, 0))
    return GradGroup(
        5, (SDS((N_DEV, D, WIN_S), BF16), SDS((N_DEV, 3, 128, D), BF16), SDS((N_DEV, 2, HEADS, 32, HD), BF16)),
        src_of, dst_of, [0, 1, 1, 1, 2],
        [pl.BlockSpec((hd, WIN_S), lambda i, me: (i, me[0])), three, three, three,
         pl.BlockSpec((2, HEADS, 16, HD), lambda i, me: (0, 0, 2 * me[0] + i, 0))],
        [pl.BlockSpec((None, hd, WIN_S), lambda i, me: (me[0], i, 0)),
         pl.BlockSpec((None, 3, 64, D), lambda i, me: (me[0], 0, i, 0)),
         pl.BlockSpec((None, 2, HEADS, 16, HD), lambda i, me: (me[0], 0, 0, i, 0))],
        copy)


def _ffn_group():
    hd, hw = D // 2, W2S // 2

    def src_of(refs, p):
        return [refs[0].at[p], refs[1].at[_slab(p, W2S, 16), :]]

    def dst_of(lands, p):
        return [lands[0].at[p], lands[1].at[p]]

    def copy(srcs, lands):
        lands[0][...] = srcs[0][...]
        lands[1][...] = srcs[1][...]

    return GradGroup(
        2, (SDS((N_DEV, D, FS), BF16), SDS((N_DEV, W2S, D), BF16)), src_of, dst_of, [0, 1],
        [pl.BlockSpec((None, hd, FS), lambda i, me: (me[0], i, 0)), pl.BlockSpec((hw, D), lambda i, me: (2 * me[0] + i, 0))],
        [pl.BlockSpec((None, hd, FS), lambda i, me: (me[0], i, 0)), pl.BlockSpec((None, hw, D), lambda i, me: (me[0], i, 0))],
        copy)


def grad_scatter(name, kinds):
    groups = [_mixer_group() if k == "mixer" else _ffn_group() for k in kinds]

    def per_group(refs, counts, fn):
        out, i = [], 0
        for g, c in zip(groups, counts):
            out += fn(g, refs[i:i + c])
            i += c
        return out

    nsrcs = [g.nsrc for g in groups]
    nlands = [len(g.lands) for g in groups]
    sem_of, off = [], 0
    for g in groups:
        sem_of += [off + s for s in g.sem_of]
        off += len(g.lands)

    ex = Exchange(name,
                  lambda refs, p: per_group(refs, nsrcs, lambda g, r: g.src_of(r, p)),
                  lambda lands, p: per_group(lands, nlands, lambda g, r: g.dst_of(r, p)),
                  sem_of, lambda lands: [a.at[pl.ds(0, 7)] for a in lands])

    def prep(me1, grads):
        nsrc = sum(nsrcs)

        def body(me_ref, *refs):
            srcs, lands = refs[:nsrc], refs[nsrc:]
            i = j = 0
            for g in groups:
                g.copy(srcs[i:i + g.nsrc], lands[j:j + len(g.lands)])
                i += g.nsrc
                j += len(g.lands)

        return list(pl.pallas_call(
            body, name=name + "_prep", out_shape=tuple(s for g in groups for s in g.lands),
            grid_spec=pltpu.PrefetchScalarGridSpec(
                num_scalar_prefetch=1, grid=(2,), in_specs=[s for g in groups for s in g.in_specs],
                out_specs=tuple(s for g in groups for s in g.out_specs)),
            compiler_params=_cp("arbitrary"),
        )(me1, *grads))

    return ex, prep


def gather_prep(l, me1, w_in, w_out_conv, w_out_rnn, w_o, ffn_w1, ffn_w2, rg_w, small):
    hd, hw = D // 2, W2S // 2

    def body(me_ref, win, woc, wor, wo, w1, w2, rgw, sm, c_win, c_w3, c_w1, c_w2, c_rgw, c_sm,
             o_win, o_w3, o_w1, o_w2, o_rgw, o_sm):
        a = win[...].astype(BF16)
        c_win[...] = a
        o_win[...] = a
        for k, r in enumerate((woc, wor, wo)):
            b = r[...].astype(BF16)
            c_w3[k] = b
            o_w3[k] = b
        for src, comp, own in ((w1, c_w1, o_w1), (w2, c_w2, o_w2), (rgw, c_rgw, o_rgw)):
            b = src[...].astype(BF16)
            comp[...] = b
            own[...] = b
        c_sm[...] = sm[...]
        o_sm[...] = sm[...]

    three = pl.BlockSpec((None, 64, D), lambda i, me: (l, i, 0))
    in_specs = [pl.BlockSpec((None, hd, WIN_S), lambda i, me: (l, i, 0)), three, three, three,
                pl.BlockSpec((None, 2, hd, FS), lambda i, me: (l, 0, i, 0)),
                pl.BlockSpec((None, 2, hw, D), lambda i, me: (l, 0, i, 0)),
                pl.BlockSpec((None, 2, HEADS, 16, HD), lambda i, me: (l, 0, 0, i, 0)),
                pl.BlockSpec((None, 8, 128), lambda i, me: (l, i, 0))]
    out_specs = (pl.BlockSpec((hd, WIN_S), lambda i, me: (i, 0)), pl.BlockSpec((3, 64, D), lambda i, me: (0, i, 0)),
                 pl.BlockSpec((2, hd, FS), lambda i, me: (0, i, 0)), pl.BlockSpec((2, hw, D), lambda i, me: (0, i, 0)),
                 pl.BlockSpec((2, HEADS, 16, HD), lambda i, me: (0, 0, i, 0)), pl.BlockSpec((8, 128), lambda i, me: (i, 0)),
                 pl.BlockSpec((hd, WIN_S), lambda i, me: (i, me[0])),
                 pl.BlockSpec((3, 64, D), lambda i, me: (0, 2 * me[0] + i, 0)),
                 pl.BlockSpec((2, None, hd, FS), lambda i, me: (0, me[0], i, 0)),
                 pl.BlockSpec((2, hw, D), lambda i, me: (0, 2 * me[0] + i, 0)),
                 pl.BlockSpec((2, HEADS, 16, HD), lambda i, me: (0, 0, 2 * me[0] + i, 0)),
                 pl.BlockSpec((8, 128), lambda i, me: (i, me[0])))
    compact = (SDS((D, WIN_S), BF16), SDS((3, 128, D), BF16), SDS((2, D, FS), BF16), SDS((2, W2S, D), BF16),
               SDS((2, HEADS, 32, HD), BF16), SDS((SMALL_ROWS, 128), F32))
    res = pl.pallas_call(
        body, name=f"gather_prep{l}", out_shape=compact + GATHER_SHAPES,
        grid_spec=pltpu.PrefetchScalarGridSpec(num_scalar_prefetch=1, grid=(2,), in_specs=in_specs, out_specs=out_specs),
        compiler_params=_cp("arbitrary"),
    )(me1, w_in, w_out_conv, w_out_rnn, w_o, ffn_w1, ffn_w2, rg_w, small)
    return list(res[:6]), list(res[6:])


def gather_small_grads(g_sharded, g_replicated):
    def body(ga, gb, la, lb, send_sems, recv_sems, loc_sems):
        _all_to_all_copies(lambda p: [ga, gb], lambda p: [la.at[p], lb.at[p]],
                           [(la.at[pl.ds(0, 7)], 0), (lb.at[pl.ds(0, 7)], 1)], send_sems, recv_sems, loc_sems)

    return pl.pallas_call(
        body, name="gather_small_grads",
        out_shape=(SDS((N_DEV,) + g_sharded.shape, F32), SDS((N_DEV,) + g_replicated.shape, F32)),
        in_specs=[ANY, ANY], out_specs=(ANY, ANY),
        scratch_shapes=[pltpu.SemaphoreType.DMA((2,)), pltpu.SemaphoreType.DMA((2,)), pltpu.SemaphoreType.DMA((2,))],
        compiler_params=pltpu.CompilerParams(has_side_effects=True),
    )(g_sharded, g_replicated)


def ffn_up(xb, w1, f):
    t = xb.shape[0]
    tm = _row_tile(t, 1024)

    def body(x_ref, wg_ref, wu_ref, g_ref, u_ref, a_ref):
        x = x_ref[...]
        g = _dot(x, wg_ref[...], NN)
        u = _dot(x, wu_ref[...], NN)
        g_ref[...] = g.astype(BF16)
        u_ref[...] = u.astype(BF16)
        a_ref[...] = (g * _sigmoid_t(g) * u).astype(BF16)

    out = pl.BlockSpec((None, tm, FS), lambda j, i: (j, i, 0))
    return pl.pallas_call(
        body, name=f"ffn_up{f}", grid=(4, t // tm),
        in_specs=[pl.BlockSpec((tm, D), lambda j, i: (i, 0)),
                  pl.BlockSpec((None, None, D, FS), lambda j, i: (f, j, 0, 0)),
                  pl.BlockSpec((None, None, D, FS), lambda j, i: (f, j + 4, 0, 0))],
        out_specs=(out, out, out), out_shape=(SDS((4, t, FS), BF16),) * 3,
        compiler_params=_cp("parallel", "parallel"),
    )(xb, w1, w1)


def ffn_down_ln(a, w2, f, xf, small, s):
    t = xf.shape[0]
    tm = _row_tile(t, 512)

    def body(a_ref, w_ref, x_ref, sm_ref, xo_ref, xb_ref, xh_ref, rs_ref):
        acc = _dot(a_ref[0], w_ref[0:FS, :], NN)
        for j in range(1, 4):
            acc = acc + _dot(a_ref[j], w_ref[j * FS:(j + 1) * FS, :], NN)
        z = ALPHA * x_ref[...] + 0.5 * acc
        y, xh, rstd = _ln_fwd(z, sm_ref[R_LNG + s:R_LNG + s + 1, :], sm_ref[R_LNB + s:R_LNB + s + 1, :])
        xo_ref[...] = y
        xb_ref[...] = y.astype(BF16)
        xh_ref[...] = xh
        rs_ref[...] = rstd

    row = pl.BlockSpec((tm, D), lambda i: (i, 0))
    return pl.pallas_call(
        body, name=f"ffn_down_ln{f}", grid=(t // tm,),
        in_specs=[pl.BlockSpec((4, tm, FS), lambda i: (0, i, 0)),
                  pl.BlockSpec((None, D_FF, D), lambda i: (f, 0, 0)),
                  row, pl.BlockSpec((SMALL_ROWS, D), lambda i: (0, 0))],
        out_specs=(row, row, row, pl.BlockSpec((tm, 1), lambda i: (i, 0))),
        out_shape=(SDS((t, D), F32), SDS((t, D), BF16), SDS((t, D), F32), SDS((t, 1), F32)),
        compiler_params=_cp("parallel"),
    )(a, w2, xf, small)


def ffn_bwd_gates(dxn, xh, rstd, small, s, w2, f, g, u):
    t = dxn.shape[0]
    tm = _row_tile(t, 256)

    def body(dy_ref, xh_ref, rs_ref, sm_ref, w_ref, g_ref, u_ref, dz_ref, df_ref, dgu_ref, dln_ref):
        i = pl.program_id(0)
        dy = dy_ref[...]
        xhat = xh_ref[...]
        dz = _ln_bwd(dy, xhat, rs_ref[...], sm_ref[R_LNG + s:R_LNG + s + 1, :])

        @pl.when(i == 0)
        def _():
            dln_ref[...] = jnp.zeros_like(dln_ref)

        dln_ref[0:1, :] += jnp.sum(dy * xhat, axis=0, keepdims=True)
        dln_ref[1:2, :] += jnp.sum(dy, axis=0, keepdims=True)
        dz_ref[...] = dz
        df = (0.5 * dz).astype(BF16)
        df_ref[...] = df
        for j in range(4):
            da = _dot(df, w_ref[j * FS:(j + 1) * FS, :], NT)
            gg = g_ref[j].astype(F32)
            uu = u_ref[j].astype(F32)
            sg = _sigmoid_t(gg)
            dgu_ref[j] = (da * uu * (sg * (1.0 + gg * (1.0 - sg)))).astype(BF16)
            dgu_ref[j + 4] = (da * (gg * sg)).astype(BF16)

    row = pl.BlockSpec((tm, D), lambda i: (i, 0))
    gu = pl.BlockSpec((4, tm, FS), lambda i: (0, i, 0))
    return pl.pallas_call(
        body, name=f"ffn_bwd_gates{f}", grid=(t // tm,),
        in_specs=[row, row, pl.BlockSpec((tm, 1), lambda i: (i, 0)), pl.BlockSpec((SMALL_ROWS, D), lambda i: (0, 0)),
                  pl.BlockSpec((None, D_FF, D), lambda i: (f, 0, 0)), gu, gu],
        out_specs=(row, row, pl.BlockSpec((8, tm, FS), lambda i: (0, i, 0)), pl.BlockSpec((2, D), lambda i: (0, 0))),
        out_shape=(SDS((t, D), F32), SDS((t, D), BF16), SDS((8, t, FS), BF16), SDS((2, D), F32)),
        compiler_params=_cp("arbitrary"),
    )(dxn, xh, rstd, small, w2, g, u)


def ffn_bwd_dx(dz, dgu, w1, f):
    t = dz.shape[0]
    tm = _row_tile(t, 1024)

    def body(dz_ref, d_ref, w_ref, dx_ref, acc):
        k = pl.program_id(1)

        @pl.when(k == 0)
        def _():
            acc[...] = ALPHA * dz_ref[...]

        acc[...] += _dot(d_ref[...], w_ref[...], NT)

        @pl.when(k == 7)
        def _():
            dx_ref[...] = acc[...]

    row = pl.BlockSpec((tm, D), lambda i, k: (i, 0))
    return pl.pallas_call(
        body, name=f"ffn_bwd_dx{f}", grid=(t // tm, 8),
        in_specs=[row, pl.BlockSpec((None, tm, FS), lambda i, k: (k, i, 0)),
                  pl.BlockSpec((None, None, D, FS), lambda i, k: (f, k, 0, 0))],
        out_specs=row, out_shape=SDS((t, D), F32),
        scratch_shapes=[pltpu.VMEM((tm, D), F32)],
        compiler_params=_cp("parallel", "arbitrary"),
    )(dz, dgu, w1)


def _mm_tn(name, a, a_spec, b, b_spec, out_sds, out_spec, grid):
    def body(a_ref, b_ref, o_ref):
        o_ref[...] = _dot(a_ref[...], b_ref[...], TN).astype(o_ref.dtype)

    return pl.pallas_call(
        body, name=name, grid=grid, in_specs=[a_spec, b_spec], out_specs=out_spec, out_shape=out_sds,
        compiler_params=_cp(*(["parallel"] * len(grid))),
    )(a, b)


def ffn_dw1(xb, dgu, f):
    t = xb.shape[0]
    return _mm_tn(f"ffn_dw1_{f}", xb, pl.BlockSpec((t, D), lambda j: (0, 0)),
                  dgu, pl.BlockSpec((None, t, FS), lambda j: (j, 0, 0)),
                  SDS((8, D, FS), BF16), pl.BlockSpec((None, D, FS), lambda j: (j, 0, 0)), (8,))


def ffn_dw2(a, df, f):
    t = df.shape[0]
    return _mm_tn(f"ffn_dw2_{f}", a, pl.BlockSpec((None, t, FS), lambda j: (j, 0, 0)),
                  df, pl.BlockSpec((t, D), lambda j: (0, 0)),
                  SDS((D_FF, D), BF16), pl.BlockSpec((FS, D), lambda j: (j, 0)), (4,))


def mm_tn_square(name, a, b):
    t = a.shape[0]
    return _mm_tn(name, a, pl.BlockSpec((t, 512), lambda i: (0, i)),
                  b, pl.BlockSpec((t, D), lambda i: (0, 0)),
                  SDS((D, D), BF16), pl.BlockSpec((512, D), lambda i: (i, 0)), (2,))


def mix_proj(xb, win):
    t = xb.shape[0]
    tm = _row_tile(t, 1024)

    def body(x_ref, w_ref, o_ref):
        o_ref[...] = _dot(x_ref[...], w_ref[...], NN).astype(BF16)

    return pl.pallas_call(
        body, name="mix_proj", grid=(7, t // tm),
        in_specs=[pl.BlockSpec((tm, D), lambda n, i: (i, 0)), pl.BlockSpec((D, D), lambda n, i: (0, n))],
        out_specs=pl.BlockSpec((tm, D), lambda n, i: (i, n)), out_shape=SDS((t, D_IN), BF16),
        compiler_params=_cp("parallel", "parallel"),
    )(xb, win)


def _pcol(tm, k):
    return pl.BlockSpec((tm, D), lambda i: (i, k))


def _prev_halo(tm, k):
    return pl.BlockSpec((8, D), lambda i: (jnp.maximum(i * (tm // 8) - 1, 0), k))


def _prev_halo16(tm, k):
    return pl.BlockSpec((16, D), lambda i: (jnp.maximum(i * (tm // 16) - 1, 0), k))


def _next_halo(tm, t, k):
    return pl.BlockSpec((8, D), lambda i: (jnp.minimum((i + 1) * (tm // 8), t // 8 - 1), k))


def _full(shape):
    nd = len(shape)
    return pl.BlockSpec(shape, lambda i: (0,) * nd)


def mix_pre(p, small, rcb, ap, rgw):
    t = p.shape[0]
    tm = _row_tile(t, 256)

    def body(bg_ref, cg_ref, v_ref, xr_ref, cgh_ref, vh_ref, xrh_ref, sm_ref, rcb_ref, ap_ref, rgw_ref,
             ca_ref, pa_ref, xc_ref, xcb_ref, gi_ref, gr_ref, h_ref, ext1, ext2, a_s, b_s, carry):
        i = pl.program_id(0)
        first = i == 0
        cv = cg_ref[...].astype(F32) * v_ref[...].astype(F32)
        ext1[0:16, :] = jnp.where(first, 0.0, cgh_ref[...].astype(F32) * vh_ref[...].astype(F32))
        ext1[16:, :] = cv
        xr = xr_ref[...].astype(F32)
        ext2[0:16, :] = jnp.where(first, 0.0, xrh_ref[...].astype(F32))
        ext2[16:, :] = xr
        ca = (sm_ref[R_SC:R_SC + 1, :] * ext1[pl.ds(14, tm), :] + sm_ref[R_SC + 1:R_SC + 2, :] * ext1[pl.ds(15, tm), :]
              + sm_ref[R_SC + 2:R_SC + 3, :] * cv)
        ca_ref[...] = ca.astype(BF16)
        pa_ref[...] = (bg_ref[...].astype(F32) * ca).astype(BF16)
        xc = (sm_ref[R_RC:R_RC + 1, :] * ext2[pl.ds(13, tm), :] + sm_ref[R_RC + 1:R_RC + 2, :] * ext2[pl.ds(14, tm), :]
              + sm_ref[R_RC + 2:R_RC + 3, :] * ext2[pl.ds(15, tm), :] + sm_ref[R_RC + 3:R_RC + 4, :] * xr
              + rcb_ref[...])
        xc_ref[...] = xc
        xcb = xc.astype(BF16)
        xcb_ref[...] = xcb
        g0, g1 = [], []
        for h in range(HEADS):
            xh = xcb[:, h * HD:(h + 1) * HD]
            g0.append(_dot(xh, rgw_ref[0, h], NN))
            g1.append(_dot(xh, rgw_ref[1, h], NN))
        gi = _sigmoid(jnp.concatenate(g0, axis=1) + sm_ref[R_RGB:R_RGB + 1, :])
        gr = _sigmoid(jnp.concatenate(g1, axis=1) + sm_ref[R_RGB + 1:R_RGB + 2, :])
        gi_ref[...] = gi
        gr_ref[...] = gr
        la = (-LRU_C) * gr * _softplus(-ap_ref[...])
        a_s[...] = jnp.exp(la)
        row = lax.broadcasted_iota(jnp.int32, (tm, D), 0) + i * tm
        mult = jnp.where(row == 0, 1.0, jnp.sqrt(_neg_expm1(2.0 * la)))
        b_s[...] = xc * gi * mult

        @pl.when(first)
        def _():
            carry[...] = jnp.zeros_like(carry)

        carry[...] = _scan_tile(a_s, b_s, a_s, carry[...], tm, reverse=False)
        h_ref[...] = a_s[...].astype(BF16)

    row = pl.BlockSpec((tm, D), lambda i: (i, 0))
    f32o, b16o = SDS((t, D), F32), SDS((t, D), BF16)
    ext, tile = pltpu.VMEM((tm + 16, D), F32), pltpu.VMEM((tm, D), F32)
    return pl.pallas_call(
        body, name="mix_pre", grid=(t // tm,),
        in_specs=[_pcol(tm, 0), _pcol(tm, 1), _pcol(tm, 2), _pcol(tm, 3),
                  _prev_halo16(tm, 1), _prev_halo16(tm, 2), _prev_halo16(tm, 3),
                  _full((SMALL_ROWS, D)), _full((1, D)), _full((1, D)), _full((2, HEADS, HD, HD))],
        out_specs=(row,) * 7, out_shape=(b16o, b16o, f32o, b16o, f32o, f32o, b16o),
        scratch_shapes=[ext, ext, tile, tile, pltpu.VMEM((8, D), F32)],
        compiler_params=_cp("arbitrary"),
    )(p, p, p, p, p, p, p, small, rcb, ap, rgw)


def _scan_tile(a_ref, b_ref, o_ref, carry, tm, reverse):
    width = a_ref.shape[1]
    ng = tm // 8
    row8 = lax.broadcasted_iota(jnp.int32, (8, width), 0)

    def step(g, c):
        r = pl.multiple_of((ng - 1 - g if reverse else g) * 8, 8)
        aa = a_ref[pl.ds(r, 8), :]
        bb = b_ref[pl.ds(r, 8), :]
        for s in (1, 2, 4):
            if reverse:
                keep, shift = row8 < 8 - s, 8 - s
            else:
                keep, shift = row8 >= s, s
            a_sh = jnp.where(keep, pltpu.roll(aa, shift, 0), 1.0)
            b_sh = jnp.where(keep, pltpu.roll(bb, shift, 0), 0.0)
            bb = aa * b_sh + bb
            aa = aa * a_sh
        o = aa * c + bb
        o_ref[pl.ds(r, 8), :] = o
        edge = o[0:1, :] if reverse else o[7:8, :]
        return jnp.broadcast_to(edge, (8, width))

    return lax.fori_loop(0, ng, step, carry)


def mix_out(pa, h, p, bm, w3, xf, small):
    t = xf.shape[0]
    tm = _row_tile(t, 256)

    def body(pa_ref, h_ref, yr_ref, gla_ref, glb_ref, bma_ref, bmb_ref, w_ref, x_ref, sm_ref,
             pb_ref, ya_ref, yb_ref, m_ref, xo_ref, xb_ref, xh_ref, rs_ref):
        ge, _ = _gelu(yr_ref[...].astype(F32))
        pb = (h_ref[...].astype(F32) * ge).astype(BF16)
        pb_ref[...] = pb
        ya = _dot(pa_ref[...], w_ref[0], NN)
        yb = _dot(pb, w_ref[1], NN)
        ya_ref[...] = ya.astype(BF16)
        yb_ref[...] = yb.astype(BF16)
        ga = _sigmoid_t(gla_ref[...].astype(F32) + bma_ref[...])
        gb = _sigmoid_t(glb_ref[...].astype(F32) + bmb_ref[...])
        m = (ga * ya + gb * yb).astype(BF16)
        m_ref[...] = m
        z = ALPHA * x_ref[...] + _dot(m, w_ref[2], NN)
        y, xh, rstd = _ln_fwd(z, sm_ref[R_LNG + 1:R_LNG + 2, :], sm_ref[R_LNB + 1:R_LNB + 2, :])
        xo_ref[...] = y
        xb_ref[...] = y.astype(BF16)
        xh_ref[...] = xh
        rs_ref[...] = rstd

    row = pl.BlockSpec((tm, D), lambda i: (i, 0))
    f32o, b16o = SDS((t, D), F32), SDS((t, D), BF16)
    return pl.pallas_call(
        body, name="mix_out", grid=(t // tm,),
        in_specs=[row, row, _pcol(tm, 4), _pcol(tm, 5), _pcol(tm, 6),
                  pl.BlockSpec((1, D), lambda i: (0, 0)), pl.BlockSpec((1, D), lambda i: (0, 1)),
                  _full((3, D, D)), row, _full((SMALL_ROWS, D))],
        out_specs=(row,) * 7 + (pl.BlockSpec((tm, 1), lambda i: (i, 0)),),
        out_shape=(b16o, b16o, b16o, b16o, f32o, b16o, f32o, SDS((t, 1), F32)),
        compiler_params=_cp("parallel"),
    )(pa, h, p, p, p, bm, bm, w3, xf, small)


def mixb_head(dxn, xh, rstd, small, w3, p, bm, ya, yb, ca, h):
    t = dxn.shape[0]
    tm = _row_tile(t, 256)

    def body(dy_ref, xh_ref, rs_ref, sm_ref, w_ref, bg_ref, yr_ref, gla_ref, glb_ref, bma_ref, bmb_ref,
             ya_ref, yb_ref, ca_ref, h_ref,
             dz_ref, dzb_ref, dya_ref, dyb_ref, dbg_ref, dca_ref, dh_ref, dphi_ref, dln_ref, dbm_ref):
        i = pl.program_id(0)
        dy = dy_ref[...]
        xhat = xh_ref[...]
        dz = _ln_bwd(dy, xhat, rs_ref[...], sm_ref[R_LNG + 1:R_LNG + 2, :])

        @pl.when(i == 0)
        def _():
            dln_ref[...] = jnp.zeros_like(dln_ref)
            dbm_ref[...] = jnp.zeros_like(dbm_ref)

        dln_ref[0:1, :] += jnp.sum(dy * xhat, axis=0, keepdims=True)
        dln_ref[1:2, :] += jnp.sum(dy, axis=0, keepdims=True)
        dz_ref[...] = dz
        dzb = dz.astype(BF16)
        dzb_ref[...] = dzb
        dm = _dot(dzb, w_ref[2], NT)
        ga = _sigmoid_t(gla_ref[...].astype(F32) + bma_ref[...])
        gb = _sigmoid_t(glb_ref[...].astype(F32) + bmb_ref[...])
        dya = (dm * ga).astype(BF16)
        dyb = (dm * gb).astype(BF16)
        dya_ref[...] = dya
        dyb_ref[...] = dyb
        dgla = dm * ya_ref[...].astype(F32) * ga * (1.0 - ga)
        dglb = dm * yb_ref[...].astype(F32) * gb * (1.0 - gb)
        dbm_ref[0:1, :] += jnp.sum(dgla, axis=0, keepdims=True)
        dbm_ref[1:2, :] += jnp.sum(dglb, axis=0, keepdims=True)
        dphi_ref[:, D:2 * D] = dgla.astype(BF16)
        dphi_ref[:, 2 * D:3 * D] = dglb.astype(BF16)
        dpa = _dot(dya, w_ref[0], NT)
        dpb = _dot(dyb, w_ref[1], NT)
        dbg_ref[...] = (dpa * ca_ref[...].astype(F32)).astype(BF16)
        dca_ref[...] = dpa * bg_ref[...].astype(F32)
        yr = yr_ref[...].astype(F32)
        ge, th = _gelu(yr)
        dh_ref[...] = (dpb * ge).astype(BF16)
        dphi_ref[:, 0:D] = (dpb * h_ref[...].astype(F32) * _gelu_grad(yr, th)).astype(BF16)

    row = pl.BlockSpec((tm, D), lambda i: (i, 0))
    f32o, b16o = SDS((t, D), F32), SDS((t, D), BF16)
    acc2 = pl.BlockSpec((2, D), lambda i: (0, 0))
    return pl.pallas_call(
        body, name="mixb_head", grid=(t // tm,),
        in_specs=[row, row, pl.BlockSpec((tm, 1), lambda i: (i, 0)), _full((SMALL_ROWS, D)), _full((3, D, D)),
                  _pcol(tm, 0), _pcol(tm, 4), _pcol(tm, 5), _pcol(tm, 6),
                  pl.BlockSpec((1, D), lambda i: (0, 0)), pl.BlockSpec((1, D), lambda i: (0, 1)),
                  row, row, row, row],
        out_specs=(row,) * 7 + (pl.BlockSpec((tm, 3 * D), lambda i: (i, 0)), acc2, acc2),
        out_shape=(f32o, b16o, b16o, b16o, b16o, f32o, b16o, SDS((t, 3 * D), BF16), SDS((2, D), F32), SDS((2, D), F32)),
        compiler_params=_cp("arbitrary"),
    )(dxn, xh, rstd, small, w3, p, p, p, p, bm, bm, ya, yb, ca, h)


def mixb_rec(dh, gr, gi, h, xc, ap, rgw):
    t = dh.shape[0]
    tm = _row_tile(t, 256)
    nt = t // tm

    def body(dh_ref, gr_ref, gi_ref, h_ref, hh_ref, xc_ref, ap_ref, rgw_ref, dg_ref, dxc_ref, red_ref,
             ext, ext_a, c_s, lam_s, lam_c, a_c):
        i = nt - 1 - pl.program_id(0)
        first = i == 0

        @pl.when(pl.program_id(0) == 0)
        def _():
            red_ref[...] = jnp.zeros_like(red_ref)
            lam_c[...] = jnp.zeros_like(lam_c)
            a_c[...] = jnp.zeros_like(a_c)

        ext[0:8, :] = jnp.where(first, 0.0, hh_ref[...].astype(F32)[8:16, :])
        ext[8:, :] = h_ref[...].astype(F32)
        hprev = ext[pl.ds(7, tm), :]
        lam_s[...] = dh_ref[...].astype(F32)
        gr = gr_ref[...]
        gi = gi_ref[...]
        xc = xc_ref[...]
        ap = ap_ref[...]
        sp = _softplus(-ap)
        la = (-LRU_C) * gr * sp
        a = jnp.exp(la)
        ext_a[0:tm, :] = a
        ext_a[tm:, :] = a_c[...]
        c_s[...] = ext_a[pl.ds(1, tm), :]
        lam_c[...] = _scan_tile(c_s, lam_s, lam_s, lam_c[...], tm, reverse=True)
        a_c[...] = jnp.broadcast_to(a[0:1, :], (8, D))
        lam = lam_s[...]
        row = lax.broadcasted_iota(jnp.int32, (tm, D), 0) + i * tm
        start = row == 0
        mult = jnp.where(start, 1.0, jnp.sqrt(_neg_expm1(2.0 * la)))
        dmult = jnp.where(start, 0.0, lam * xc * gi)
        dla = lam * hprev * a - dmult * a * a / mult
        dg1 = (-LRU_C) * sp * dla * gr * (1.0 - gr)
        dg0 = lam * xc * mult * gi * (1.0 - gi)
        dsp = jnp.sum((-LRU_C) * gr * dla, axis=0, keepdims=True)
        red_ref[0:1, :] += jnp.sum(dg0, axis=0, keepdims=True)
        red_ref[1:2, :] += jnp.sum(dg1, axis=0, keepdims=True)
        red_ref[2:3, :] += -dsp * _sigmoid(-ap)
        dg0b = dg0.astype(BF16)
        dg1b = dg1.astype(BF16)
        dg_ref[0] = dg0b
        dg_ref[1] = dg1b
        parts = []
        for hd in range(HEADS):
            sl = slice(hd * HD, (hd + 1) * HD)
            parts.append(_dot(dg0b[:, sl], rgw_ref[0, hd], NT) + _dot(dg1b[:, sl], rgw_ref[1, hd], NT))
        dxc = lam * gi * mult + jnp.concatenate(parts, axis=1)
        dxc_ref[...] = dxc
        red_ref[3:4, :] += jnp.sum(dxc, axis=0, keepdims=True)

    row = pl.BlockSpec((tm, D), lambda i: (nt - 1 - i, 0))
    halo = pl.BlockSpec((16, D), lambda i: (jnp.maximum((nt - 1 - i) * (tm // 16) - 1, 0), 0))
    ext, tile, edge = pltpu.VMEM((tm + 8, D), F32), pltpu.VMEM((tm, D), F32), pltpu.VMEM((8, D), F32)
    return pl.pallas_call(
        body, name="mixb_rec", grid=(nt,),
        in_specs=[row, row, row, row, halo, row, _full((1, D)), _full((2, HEADS, HD, HD))],
        out_specs=(pl.BlockSpec((2, tm, D), lambda i: (0, nt - 1 - i, 0)), row, pl.BlockSpec((8, D), lambda i: (0, 0))),
        out_shape=(SDS((2, t, D), BF16), SDS((t, D), F32), SDS((8, D), F32)),
        scratch_shapes=[ext, ext, tile, tile, edge, edge],
        compiler_params=_cp("arbitrary"),
    )(dh, gr, gi, h, h, xc, ap, rgw)


def mixb_conv(dca, dxc, dbg, p, small):
    t = dca.shape[0]
    tm = _row_tile(t, 256)
    nt = t // tm

    def body(dca_ref, dcan_ref, dxc_ref, dxcn_ref, dbg_ref, cg_ref, v_ref, xr_ref, cgh_ref, vh_ref, xrh_ref, sm_ref,
             dplo_ref, dxr_ref, red_ref, e_dca, e_dxc, e_cv, e_xr):
        i = pl.program_id(0)
        first = i == 0
        last = i == nt - 1

        @pl.when(first)
        def _():
            red_ref[...] = jnp.zeros_like(red_ref)

        dca = dca_ref[...]
        dxc = dxc_ref[...]
        e_dca[0:tm, :] = dca
        e_dca[tm:, :] = jnp.where(last, 0.0, dcan_ref[...])
        e_dxc[0:tm, :] = dxc
        e_dxc[tm:, :] = jnp.where(last, 0.0, dxcn_ref[...])
        cg = cg_ref[...].astype(F32)
        v = v_ref[...].astype(F32)
        xr = xr_ref[...].astype(F32)
        e_cv[0:8, :] = jnp.where(first, 0.0, (cgh_ref[...].astype(F32) * vh_ref[...].astype(F32))[8:16, :])
        e_cv[8:, :] = cg * v
        e_xr[0:8, :] = jnp.where(first, 0.0, xrh_ref[...].astype(F32)[8:16, :])
        e_xr[8:, :] = xr
        dcv = (sm_ref[R_SC + 2:R_SC + 3, :] * dca + sm_ref[R_SC + 1:R_SC + 2, :] * e_dca[pl.ds(1, tm), :]
               + sm_ref[R_SC:R_SC + 1, :] * e_dca[pl.ds(2, tm), :])
        dplo_ref[:, 0:D] = dbg_ref[...]
        dplo_ref[:, D:2 * D] = (dcv * v).astype(BF16)
        dplo_ref[:, 2 * D:3 * D] = (dcv * cg).astype(BF16)
        dxr = (sm_ref[R_RC + 3:R_RC + 4, :] * dxc + sm_ref[R_RC + 2:R_RC + 3, :] * e_dxc[pl.ds(1, tm), :]
               + sm_ref[R_RC + 1:R_RC + 2, :] * e_dxc[pl.ds(2, tm), :] + sm_ref[R_RC:R_RC + 1, :] * e_dxc[pl.ds(3, tm), :])
        dxr_ref[...] = dxr.astype(BF16)
        for k in range(3):
            red_ref[R_SC + k:R_SC + k + 1, :] += jnp.sum(dca * e_cv[pl.ds(6 + k, tm), :], axis=0, keepdims=True)
        for k in range(4):
            red_ref[R_RC + k:R_RC + k + 1, :] += jnp.sum(dxc * e_xr[pl.ds(5 + k, tm), :], axis=0, keepdims=True)

    row = pl.BlockSpec((tm, D), lambda i: (i, 0))
    ext = pltpu.VMEM((tm + 8, D), F32)
    return pl.pallas_call(
        body, name="mixb_conv", grid=(nt,),
        in_specs=[row, _next_halo(tm, t, 0), row, _next_halo(tm, t, 0), row,
                  _pcol(tm, 1), _pcol(tm, 2), _pcol(tm, 3), _prev_halo16(tm, 1), _prev_halo16(tm, 2), _prev_halo16(tm, 3),
                  _full((SMALL_ROWS, D))],
        out_specs=(pl.BlockSpec((tm, 3 * D), lambda i: (i, 0)), row, pl.BlockSpec((8, D), lambda i: (0, 0))),
        out_shape=(SDS((t, 3 * D), BF16), SDS((t, D), BF16), SDS((8, D), F32)),
        scratch_shapes=[ext, ext, ext, ext],
        compiler_params=_cp("arbitrary"),
    )(dca, dca, dxc, dxc, dbg, p, p, p, p, p, p, small)


def mixb_dx(dz, dplo, dxr, dphi, win):
    t = dz.shape[0]
    tm = _row_tile(t, 1024)

    def body(dz_ref, lo_ref, xr_ref, hi_ref, w_ref, dx_ref, acc):
        k = pl.program_id(1)

        @pl.when(k == 0)
        def _():
            acc[...] = ALPHA * dz_ref[...]

        @pl.when(k < 3)
        def _():
            acc[...] += _dot(lo_ref[...], w_ref[...], NT)

        @pl.when(k == 3)
        def _():
            acc[...] += _dot(xr_ref[...], w_ref[...], NT)

        @pl.when(k > 3)
        def _():
            acc[...] += _dot(hi_ref[...], w_ref[...], NT)

        @pl.when(k == 6)
        def _():
            dx_ref[...] = acc[...]

    row = pl.BlockSpec((tm, D), lambda i, k: (i, 0))
    return pl.pallas_call(
        body, name="mixb_dx", grid=(t // tm, 7),
        in_specs=[row, pl.BlockSpec((tm, D), lambda i, k: (i, jnp.minimum(k, 2))), row,
                  pl.BlockSpec((tm, D), lambda i, k: (i, jnp.clip(k - 4, 0, 2))),
                  pl.BlockSpec((D, D), lambda i, k: (0, k))],
        out_specs=row, out_shape=SDS((t, D), F32),
        scratch_shapes=[pltpu.VMEM((tm, D), F32)],
        compiler_params=_cp("parallel", "arbitrary"),
    )(dz, dplo, dxr, dphi, win)


def mixb_dwin(xb, dplo, dxr, dphi):
    t = xb.shape[0]
    tk = _row_tile(t, 2048)
    nk = t // tk

    def body(x_ref, lo_ref, xr_ref, hi_ref, o_ref, acc):
        n = pl.program_id(0)
        k = pl.program_id(1)

        @pl.when(k == 0)
        def _():
            acc[...] = jnp.zeros_like(acc)

        @pl.when(n < 3)
        def _():
            acc[...] += _dot(x_ref[...], lo_ref[...], TN)

        @pl.when(n == 3)
        def _():
            acc[...] += _dot(x_ref[...], xr_ref[...], TN)

        @pl.when(n > 3)
        def _():
            acc[...] += _dot(x_ref[...], hi_ref[...], TN)

        @pl.when(k == nk - 1)
        def _():
            o_ref[...] = acc[...].astype(BF16)

    return pl.pallas_call(
        body, name="mixb_dwin", grid=(7, nk),
        in_specs=[pl.BlockSpec((tk, D), lambda n, k: (k, 0)),
                  pl.BlockSpec((tk, D), lambda n, k: (jnp.where(n < 3, k, 0), jnp.minimum(n, 2))),
                  pl.BlockSpec((tk, D), lambda n, k: (jnp.where(n == 3, k, 0), 0)),
                  pl.BlockSpec((tk, D), lambda n, k: (jnp.where(n > 3, k, 0), jnp.clip(n - 4, 0, 2)))],
        out_specs=pl.BlockSpec((D, D), lambda n, k: (0, n)), out_shape=SDS((D, D_IN), BF16),
        scratch_shapes=[pltpu.VMEM((D, D), F32)],
        compiler_params=_cp("parallel", "arbitrary"),
    )(xb, dplo, dxr, dphi)


def mixb_drgw(xcb, dg):
    t = xcb.shape[0]
    return _mm_tn("mixb_drgw", xcb, pl.BlockSpec((t, HD), lambda g, h: (0, h)),
                  dg, pl.BlockSpec((None, t, HD), lambda g, h: (g, 0, h)),
                  SDS((2, HEADS, HD, HD), BF16), pl.BlockSpec((None, None, HD, HD), lambda g, h: (g, h, 0, 0)),
                  (2, HEADS))


def loss_head(y, tgt):
    t = y.shape[0]
    tm = _row_tile(t, 512)

    def body(y_ref, t_ref, dy_ref, l_ref):
        i = pl.program_id(0)
        e = y_ref[...] - t_ref[...]
        dy_ref[...] = e * (1.0 / D)

        @pl.when(i == 0)
        def _():
            l_ref[...] = jnp.zeros_like(l_ref)

        l_ref[...] += 0.5 * jnp.sum(jnp.mean(e * e, axis=-1, keepdims=True), axis=0, keepdims=True)

    row = pl.BlockSpec((tm, D), lambda i: (i, 0))
    return pl.pallas_call(
        body, name="loss_head", grid=(t // tm,), in_specs=[row, row],
        out_specs=(row, pl.BlockSpec((8, 128), lambda i: (0, 0))),
        out_shape=(SDS((t, D), F32), SDS((8, 128), F32)),
        compiler_params=_cp("arbitrary"),
    )(y, tgt)


def _adamw(w, g, m, v):
    m = ADAM_B1 * m + (1.0 - ADAM_B1) * g
    v = ADAM_B2 * v + (1.0 - ADAM_B2) * (g * g)
    m_hat = m / (1.0 - ADAM_B1 ** ADAM_STEP)
    v_hat = v / (1.0 - ADAM_B2 ** ADAM_STEP)
    delta = -ADAM_LR * (m_hat / (jnp.sqrt(v_hat) + ADAM_EPS) + ADAM_WD * w)
    return delta, m, v


def adam_big(name, w, m, v, parts, rows, lanes, blk_off, tr, l0, prev=None):
    nr = rows // tr
    nl = len(parts)

    def body(w_ref, m_ref, v_ref, *rest):
        g_ref, d_ref, mo_ref, vo_ref = rest[-4:]
        l = pl.program_id(0)
        for ll in range(nl):
            pr = rest[ll]

            @pl.when(l == ll)
            def _():
                g = pr[0].astype(F32)
                for s in range(1, N_DEV):
                    g = g + pr[s].astype(F32)
                g_ref[...] = g
                d, mn, vn = _adamw(w_ref[...], g, m_ref[...], v_ref[...])
                d_ref[...] = d
                mo_ref[...] = mn
                vo_ref[...] = vn

    blk = pl.BlockSpec((None, tr, lanes), lambda l, r: (l + l0, r, 0))

    def part_spec(ll):
        return pl.BlockSpec((N_DEV, tr, lanes), lambda l, r: (0, jnp.where(l == ll, r, 0) + blk_off, 0))

    out = SDS(w.shape, F32)
    extra = [] if prev is None else list(prev)
    return pl.pallas_call(
        body, name=name, grid=(nl, nr),
        in_specs=[blk, blk, blk] + [part_spec(ll) for ll in range(nl)] + [ANY] * len(extra),
        out_specs=(blk,) * 4, out_shape=(out,) * 4,
        input_output_aliases={3 + nl + i: i for i in range(len(extra))},
        compiler_params=_cp("parallel", "parallel"),
    )(w, m, v, *parts, *extra)


def small_update(me1, land_sh, land_rep, sharded, replicated):
    ns, nr = len(sharded), len(replicated)

    def body(me_ref, lsh, lrep, *refs):
        ins, outs = refs[:3 * (ns + nr)], refs[3 * (ns + nr):]

        def total(read):
            g = read(0)
            for s in range(1, N_DEV):
                g = g + read(s)
            return g

        def update(k, g, sl):
            w_ref, m_ref, v_ref = ins[3 * k:3 * k + 3]
            d, mn, vn = _adamw(w_ref[sl], g, m_ref[sl], v_ref[sl])
            for o, val in zip(outs[4 * k:4 * k + 4], (g, d, mn, vn)):
                o[sl] = val

        for k, (r0, n) in enumerate(GS_ROWS):
            update(k, total(lambda s: lsh[s, :, r0:r0 + n, :]), (slice(None),) * 3)
        lo, hi = (slice(None), slice(0, D)), (slice(None), slice(D, 2 * D))
        update(ns, total(lambda s: lrep[s, 0:4, :]), lo)
        update(ns, total(lambda s: lrep[s, 4:8, :]), hi)
        update(ns + 1, total(lambda s: lrep[s, 8:12, :]), (slice(None),) * 2)
        update(ns + 2, total(lambda s: lrep[s, 12:16, :]), (slice(None),) * 2)

    def whole(a):
        nd = a.ndim
        return pl.BlockSpec(a.shape, lambda i, me: (0,) * nd)

    params = [a for wmv in list(sharded) + list(replicated) for a in wmv]
    out_shape = tuple(SDS(wmv[0].shape, F32) for wmv in list(sharded) + list(replicated) for _ in range(4))
    res = pl.pallas_call(
        body, name="small_update", out_shape=out_shape,
        grid_spec=pltpu.PrefetchScalarGridSpec(
            num_scalar_prefetch=1, grid=(1,),
            in_specs=[pl.BlockSpec((N_DEV, DEPTH, 16, 128), lambda i, me: (0, 0, 0, me[0])), whole(land_rep)]
            + [whole(a) for a in params],
            out_specs=tuple(pl.BlockSpec(s.shape, lambda i, me, nd=len(s.shape): (0,) * nd) for s in out_shape)),
        compiler_params=_cp("arbitrary"),
    )(me1, land_sh, land_rep, *params)
    return [list(res[4 * k:4 * k + 4]) for k in range(ns + nr)]


def _ffn_forward(xf, xb, gw, f, s, between=None):
    g, u, a = ffn_up(xb, gw["w1"], f)
    if between is not None:
        a = between(a)
    xo, xob, xh, rs = ffn_down_ln(a, gw["w2"], f, xf, gw["small"], s)
    return (xo, xob), dict(xb=xb, g=g, u=u, a=a, xh=xh, rs=rs)


def _ffn_backward(dxn, sv, gw, f, s):
    dz, df, dgu, dln = ffn_bwd_gates(dxn, sv["xh"], sv["rs"], gw["small"], s, gw["w2"], f, sv["g"], sv["u"])
    dx = ffn_bwd_dx(dz, dgu, gw["w1"], f)
    dw1 = ffn_dw1(sv["xb"], dgu, f)
    dw2 = ffn_dw2(sv["a"], df, f)
    return dx, dw1, dw2, dln


def _mixer_forward(xf, xb, gw, rcb, ap, bm):
    p = mix_proj(xb, gw["win"])
    ca, pa, xc, xcb, gi, gr, h = mix_pre(p, gw["small"], rcb, ap, gw["rgw"])
    pb, ya, yb, m, xo, xob, xh, rs = mix_out(pa, h, p, bm, gw["w3"], xf, gw["small"])
    sv = dict(xb=xb, p=p, ca=ca, pa=pa, xc=xc, xcb=xcb, gi=gi, gr=gr, h=h, pb=pb, ya=ya, yb=yb, m=m, xh=xh, rs=rs)
    return (xo, xob), sv


def _mixer_backward(dxn, sv, gw, rcb, ap, bm):
    dz, dzb, dya, dyb, dbg, dca, dh, dphi, dln, dbm = mixb_head(
        dxn, sv["xh"], sv["rs"], gw["small"], gw["w3"], sv["p"], bm, sv["ya"], sv["yb"], sv["ca"], sv["h"])
    dg, dxc, red_rec = mixb_rec(dh, sv["gr"], sv["gi"], sv["h"], sv["xc"], ap, gw["rgw"])
    dplo, dxr, red_conv = mixb_conv(dca, dxc, dbg, sv["p"], gw["small"])
    dx = mixb_dx(dz, dplo, dxr, dphi, gw["win"])
    dwin = mixb_dwin(sv["xb"], dplo, dxr, dphi)
    dwo = mm_tn_square("mixb_dwo", sv["m"], dzb)
    dwoc = mm_tn_square("mixb_dwoc", sv["pa"], dya)
    dwor = mm_tn_square("mixb_dwor", sv["pb"], dyb)
    drgw = mixb_drgw(sv["xcb"], dg)
    return dx, dict(dwin=dwin, dwoc=dwoc, dwor=dwor, dwo=dwo, drgw=drgw), dln, dbm, red_rec, red_conv


def kernel(x, w_in, b_merge, sc_w, rc_w, rc_b, rg_w, rg_b, a_param, w_out_conv, w_out_rnn, w_o, ffn_w1, ffn_w2, ln_g, ln_b, loss_target, m_w_in, m_b_merge, m_sc_w, m_rc_w, m_rc_b, m_rg_w, m_rg_b, m_a_param, m_w_out_conv, m_w_out_rnn, m_w_o, m_ffn_w1, m_ffn_w2, m_ln_g, m_ln_b, v_w_in, v_b_merge, v_sc_w, v_rc_w, v_rc_b, v_rg_w, v_rg_b, v_a_param, v_w_out_conv, v_w_out_rnn, v_w_o, v_ffn_w1, v_ffn_w2, v_ln_g, v_ln_b):
    t = x.shape[1]
    me = _me()

    def rows(parts, total):
        out, off = None, 0
        for part in parts:
            r = part.shape[-2]
            pad = [(0, 0)] * (part.ndim - 2) + [(off, total - off - r), (0, 0)]
            padded = jnp.pad(part, pad)
            out = padded if out is None else out + padded
            off += r
        return out

    small = rows([sc_w, rc_w, rg_b, ln_g, ln_b], SMALL_ROWS)

    me1 = jnp.reshape(me, (1,)).astype(jnp.int32)

    def layer_shards(l, small_now):
        return gather_prep(l, me1, w_in, w_out_conv, w_out_rnn, w_o, ffn_w1, ffn_w2, rg_w, small_now)

    def as_weights(lands):
        return dict(zip(("win", "w3", "w1", "w2", "rgw", "small"), lands))

    xf = x.reshape(t, D)
    xb = xf.astype(BF16)

    early_at, rest_at = [2, 3, 5], [0, 1, 2, 3, 4]

    def ffn_slots(f, o_w1, o_w2, p):
        return [o_w1.at[f, p], o_w2.at[f, _slab(p, W2S, 16), :]]

    def early_slots(lands, p):
        return ffn_slots(0, lands[0], lands[1], p) + [lands[2].at[:, _slab(p, 128, 128)]]

    def rest_slots(lands, p):
        o_win, o_w3, o_w1, o_w2, o_rgw = lands
        return ([o_win.at[:, _slab(p, WIN_S, 128)], o_w3.at[:, _slab(p, 128, 128), :]] + ffn_slots(1, o_w1, o_w2, p)
                + [o_rgw.at[:, :, _slab(p, 32, 32), :]])

    early = ChipGather("gather0a", 3, 3, 3, lambda refs: [refs[0].at[0], refs[1].at[0], refs[2]], early_slots)
    rest = ChipGather("gather0b", 5, 5, 5, lambda refs: [refs[0], refs[1], refs[2].at[1], refs[3].at[1], refs[4]],
                      rest_slots)
    shards, own = layer_shards(0, small)

    def step(method, at, *args):
        res = method(*args[:-1], [shards[i] for i in at], [own[i] for i in at], args[-1])
        for i, s_new, o_new in zip(at, res[-3], res[-2]):
            shards[i], own[i] = s_new, o_new
        return tuple(res[:-3]) + (res[-1],)

    sems_a, thru = step(early.first, early_at, [xb, small])
    sems_b, thru = step(rest.first, rest_at, thru)
    relay_a, thru = step(early.relay, early_at, sems_a, thru)
    ((xb, small),) = step(early.last, early_at, sems_a, relay_a, thru)
    cur, s0 = _ffn_forward(xf, xb, as_weights(own), 0, 0)
    relay_b, thru = step(rest.relay, rest_at, sems_b, list(cur) + [small])
    (thru,) = step(rest.last, rest_at, sems_b, relay_b, thru)
    cur, small = thru[:2], thru[2]

    gathers = [ChipGather(f"gather{l}") for l in range(DEPTH)]
    gws, saved = [as_weights(own)], []
    for l in range(DEPTH):
        gw = gws[l]
        nxt = l + 1 < DEPTH
        if nxt:
            shards, own = layer_shards(l + 1, small)
            sems, shards, own, thru = gathers[l + 1].first(shards, own, list(cur) + [small])
            cur, small = thru[:2], thru[2]
        rcb, ap, bm = rc_b[l][None], a_param[l][None], b_merge[l][None]
        if l > 0:
            cur, s0 = _ffn_forward(cur[0], cur[1], gw, 0, 0)
        cur, s1 = _mixer_forward(cur[0], cur[1], gw, rcb, ap, bm)
        if nxt and l > 0:
            relay_sems, shards, own, cur = gathers[l + 1].relay(sems, shards, own, cur)
        if nxt and l == 0:
            def late_relay(a):
                nonlocal relay_sems, shards, own
                relay_sems, shards, own, (a,) = gathers[1].relay(sems, shards, own, [a])
                return a

            cur, s2 = _ffn_forward(cur[0], cur[1], gw, 1, 2, late_relay)
        else:
            cur, s2 = _ffn_forward(cur[0], cur[1], gw, 1, 2)
        saved.append((s0, s1, s2))
        if nxt:
            _, own, cur = gathers[l + 1].last(sems, relay_sems, shards, own, cur)
            gws.append(as_weights(own))

    dy, loss_tile = loss_head(cur[0], loss_target.reshape(t, D))
    loss = lax.psum(loss_tile[0, 0], ("x", "y", "c"))

    lands = [dict() for _ in range(DEPTH)]
    gsmall, grep = [None] * DEPTH, [None] * DEPTH
    flights = {}

    def launch(key, kinds, grads, dy):
        ex, prep = grad_scatter(f"scatter{key}", kinds)
        fl = ex.start(grads, prep(me1, grads), [] if dy is None else [dy])
        flights[key] = (ex, fl)
        return None if dy is None else fl[4][0]

    def landed(key, thru):
        ex, fl = flights.pop(key)
        return ex.wait(fl[0], fl[1], fl[2], fl[3], thru)

    ffn_b, mixer, ffn_a = ("w1b", "w2b"), ("w_in", "w3", "rgw"), ("w1a", "w2a")
    SPLIT = (0, 1)
    for l in reversed(range(DEPTH)):
        gw = gws[l]
        rcb, ap, bm = rc_b[l][None], a_param[l][None], b_merge[l][None]
        s0, s1, s2 = saved[l]
        dy, dw1b, dw2b, dln2 = _ffn_backward(dy, s2, gw, 1, 2)
        if l in SPLIT:
            dy = launch(f"{l}b", ["ffn"], [dw1b, dw2b], dy)
        dy, dmix, dln1, dbm, red_rec, red_conv = _mixer_backward(dy, s1, gw, rcb, ap, bm)
        mixer_grads = [dmix["dwin"], dmix["dwoc"], dmix["dwor"], dmix["dwo"], dmix["drgw"]]
        if l in SPLIT:
            dy = launch(f"{l}m", ["mixer"], mixer_grads, dy)
        dy, dw1a, dw2a, dln0 = _ffn_backward(dy, s0, gw, 0, 0)
        if l + 1 < DEPTH and l + 1 not in SPLIT:
            got, (dy,) = landed(str(l + 1), [dy])
            lands[l + 1] = dict(zip(ffn_b + mixer + ffn_a, got))
        if l == 0:
            lands[DEPTH - 1]["w_in"] = launch("0a", ["ffn"], [dw1a, dw2a], lands[DEPTH - 1]["w_in"])
        elif l in SPLIT:
            dy = launch(f"{l}a", ["ffn"], [dw1a, dw2a], dy)
        else:
            dy = launch(str(l), ["ffn", "mixer", "ffn"], [dw1b, dw2b] + mixer_grads + [dw1a, dw2a], dy)
        gsmall[l] = rows([red_conv[0:8], red_rec[0:2], dln0[0:1], dln1[0:1], dln2[0:1], dln0[1:2], dln1[1:2],
                          dln2[1:2]], 16)
        grep[l] = [dbm[0:1], dbm[1:2], red_rec[3:4], red_rec[2:3]]
    grad_x = dy.reshape(1, t, D)

    g_rep = rows([grep[l][kind] for kind in range(4) for l in range(DEPTH)], 16)
    land_sh, land_rep = gather_small_grads(jnp.stack(gsmall, axis=0), g_rep)
    small_names = ("sc_w", "rc_w", "rg_b", "ln_g", "ln_b", "b_merge", "rc_b", "a_param")
    small_res = small_update(
        me1, land_sh, land_rep,
        [(sc_w, m_sc_w, v_sc_w), (rc_w, m_rc_w, v_rc_w), (rg_b, m_rg_b, v_rg_b), (ln_g, m_ln_g, v_ln_g),
         (ln_b, m_ln_b, v_ln_b)],
        [(b_merge, m_b_merge, v_b_merge), (rc_b, m_rc_b, v_rc_b), (a_param, m_a_param, v_a_param)])

    def parts_of(layers):
        ls = [lands[l] for l in layers]
        return dict(w_in=[a["w_in"] for a in ls],
                    w3=[a["w3"].reshape(N_DEV, 3 * 128, D) for a in ls],
                    rgw=[a["rgw"].reshape(N_DEV, 2 * HEADS * 32, HD) for a in ls],
                    w1=[a[k] for a in ls for k in ("w1a", "w1b")],
                    w2=[a[k] for a in ls for k in ("w2a", "w2b")])

    families = [("w_in", w_in, m_w_in, v_w_in, "w_in", 1, D, WIN_S, 0, 128),
                ("w_out_conv", w_out_conv, m_w_out_conv, v_w_out_conv, "w3", 1, 128, D, 0, 128),
                ("w_out_rnn", w_out_rnn, m_w_out_rnn, v_w_out_rnn, "w3", 1, 128, D, 1, 128),
                ("w_o", w_o, m_w_o, v_w_o, "w3", 1, 128, D, 2, 128),
                ("ffn_w1", ffn_w1, m_ffn_w1, v_ffn_w1, "w1", 2, D, FS, 0, 256),
                ("ffn_w2", ffn_w2, m_ffn_w2, v_ffn_w2, "w2", 2, W2S, D, 0, W2S // 2),
                ("rg_w", rg_w, m_rg_w, v_rg_w, "rgw", 1, 2 * HEADS * 32, HD, 0, 256)]

    def adam_pass(tag, layers, prev):
        parts = parts_of(layers)
        outs = {}
        for name, w, m, v, fam, per, nrow, lanes, blk_off, tr in families:
            r3 = lambda a: a.reshape(DEPTH * per, nrow, lanes)
            outs[name] = adam_big(f"adam_{name}_{tag}", r3(w), r3(m), r3(v), parts[fam], nrow, lanes, blk_off, tr,
                                  per * layers[0], None if prev is None else prev[name])
        return outs

    def through(outs, key, extra):
        thru = [outs[f[0]][0] for f in families] + list(extra)
        got, thru = landed(key, thru)
        outs = {f[0]: [thru[i]] + list(outs[f[0]][1:]) for i, f in enumerate(families)}
        return got, outs, thru[len(families):]

    done = adam_pass("top", list(range(2, DEPTH)), None)
    for l, tag in ((1, "second"), (0, "first")):
        for part, names in (("b", ffn_b), ("m", mixer), ("a", ffn_a)):
            got, done, _ = through(done, f"{l}{part}", [])
            lands[l].update(zip(names, got))
        done = adam_pass(tag, [l], done)
    res = {f[0]: [o.reshape(f[1].shape) for o in done[f[0]]] for f in families}

    res.update(zip(small_names, small_res))

    names = ["w_in", "b_merge", "sc_w", "rc_w", "rc_b", "rg_w", "rg_b", "a_param", "w_out_conv", "w_out_rnn", "w_o",
             "ffn_w1", "ffn_w2", "ln_g", "ln_b"]
    out = [loss, grad_x]
    for k in range(4):
        out += [res[n][k] for n in names]
    return tuple(out)
```

```python
import functools

import jax
import jax.numpy as jnp
from jax import lax
from jax.experimental import pallas as pl
from jax.experimental.pallas import tpu as pltpu

F32 = jnp.float32
BF16 = jnp.bfloat16
SDS = jax.ShapeDtypeStruct

N_DEV = 8
DEPTH = 4
D = 1024
D_FF = 2816
FS = D_FF // 4
W2S = D_FF // 8
D_IN = 7 * D
WIN_S = D_IN // 8
HEADS = 4
HD = D // HEADS
LRU_C = 8.0
ALPHA = (2.0 * DEPTH) ** 0.25
LN_EPS = 1e-5
ADAM_LR, ADAM_B1, ADAM_B2, ADAM_EPS, ADAM_WD, ADAM_STEP = 0.001, 0.9, 0.999, 1e-08, 0.01, 10

R_SC, R_RC, R_RGB, R_LNG, R_LNB = 0, 3, 7, 9, 12
SMALL_ROWS = 16
GS_ROWS = ((0, 3), (3, 4), (8, 2), (10, 3), (13, 3))

NN = ((1,), (0,))
NT = ((1,), (1,))
TN = ((0,), (0,))
MESH = pl.DeviceIdType.MESH
ANY = pl.BlockSpec(memory_space=pl.ANY)
VMEM_LIMIT = 52 * 1024 * 1024


def _dot(a, b, dims):
    return lax.dot_general(a, b, (dims, ((), ())), preferred_element_type=F32)


def _cp(*sem):
    return pltpu.CompilerParams(dimension_semantics=sem, vmem_limit_bytes=VMEM_LIMIT)


def _sigmoid(x):
    return 1.0 / (1.0 + jnp.exp(-x))


def _sigmoid_t(x):
    return 0.5 * jnp.tanh(0.5 * x) + 0.5


def _log1p(e):
    u = 1.0 + e
    return jnp.where(u == 1.0, e, jnp.log(u) * e / jnp.where(u == 1.0, 1.0, u - 1.0))


def _softplus(x):
    return jnp.maximum(x, 0.0) + _log1p(jnp.exp(-jnp.abs(x)))


def _neg_expm1(x):
    u = jnp.exp(x)
    um1 = u - 1.0
    safe = jnp.logical_and(u != 1.0, um1 != -1.0)
    r = um1 * x / jnp.where(safe, jnp.log(jnp.where(safe, u, 0.5)), 1.0)
    return -jnp.where(u == 1.0, x, jnp.where(um1 == -1.0, -1.0, r))


def _gelu(y):
    c = 0.7978845608028654
    t = jnp.tanh(c * (y + 0.044715 * y * y * y))
    return 0.5 * y * (1.0 + t), t


def _gelu_grad(y, t):
    c = 0.7978845608028654
    return 0.5 * (1.0 + t) + 0.5 * y * (1.0 - t * t) * c * (1.0 + 3.0 * 0.044715 * y * y)


def _ln_fwd(z, g, b):
    mu = jnp.mean(z, axis=-1, keepdims=True)
    zc = z - mu
    var = jnp.mean(zc * zc, axis=-1, keepdims=True)
    rstd = lax.rsqrt(var + LN_EPS)
    xh = zc * rstd
    return xh * g + b, xh, rstd


def _ln_bwd(dy, xh, rstd, g):
    dxh = dy * g
    m1 = jnp.mean(dxh, axis=-1, keepdims=True)
    m2 = jnp.mean(dxh * xh, axis=-1, keepdims=True)
    return rstd * (dxh - m1 - xh * m2)


def _row_tile(t, want):
    return min(want, t)


def _me():
    return 4 * lax.axis_index("x") + 2 * lax.axis_index("y") + lax.axis_index("c")


def _coords(p):
    return (p // 4, (p // 2) % 2, p % 2)


def _all_to_all_copies(srcs_of, dsts_of, waits, send_sems, recv_sems, loc_sems):
    me = _me()
    n = len(waits)
    own_src, own_dst = srcs_of(me), dsts_of(me)
    local = [pltpu.make_async_copy(own_src[k], own_dst[k], loc_sems.at[k]) for k in range(n)]
    for cp in local:
        cp.start()
    for d in range(1, N_DEV):
        p = (me + d) % N_DEV
        src, dst = srcs_of(p), dsts_of(me)
        for k in range(n):
            pltpu.make_async_remote_copy(
                src_ref=src[k], dst_ref=dst[k], send_sem=send_sems.at[waits[k][1]],
                recv_sem=recv_sems.at[waits[k][1]], device_id=_coords(p), device_id_type=MESH).start()
    done = set()
    for k in range(n):
        ref, s = waits[k]
        if s in done:
            continue
        done.add(s)
        pltpu.make_async_remote_copy(
            src_ref=ref, dst_ref=ref, send_sem=send_sems.at[s], recv_sem=recv_sems.at[s],
            device_id=_coords(me), device_id_type=MESH).wait()
    for cp in local:
        cp.wait()


HBM = pl.BlockSpec(memory_space=pltpu.HBM)
SEM = pl.BlockSpec(memory_space=pltpu.SEMAPHORE)
EFFECT = pltpu.SideEffectType.DATAFLOW_SIDE_EFFECTING


def _in_hbm(a):
    return pltpu.with_memory_space_constraint(a, pltpu.HBM)


class Exchange:
    def __init__(self, name, src_of, dst_of, sem_of, span_of):
        self.name, self.src_of, self.dst_of, self.sem_of, self.span_of = name, src_of, dst_of, sem_of, span_of
        self.nsem = max(sem_of) + 1

    def start(self, srcs, lands, thru):
        n, m = len(srcs), len(lands)
        ops = list(srcs) + list(lands) + list(thru)

        def body(*refs):
            src_refs, land_refs = refs[:n], refs[n:n + m]
            send_sems, recv_sems = refs[len(ops)], refs[len(ops) + 1]
            me = _me()
            for dd in range(1, N_DEV):
                p = (me + dd) % N_DEV
                s, d = self.src_of(src_refs, p), self.dst_of(land_refs, me)
                for k in range(len(self.sem_of)):
                    pltpu.make_async_remote_copy(
                        src_ref=s[k], dst_ref=d[k], send_sem=send_sems.at[self.sem_of[k]],
                        recv_sem=recv_sems.at[self.sem_of[k]], device_id=_coords(p), device_id_type=MESH).start()

        sem = pltpu.SemaphoreType.DMA((self.nsem,))
        res = pl.pallas_call(
            body, name=self.name + "_start",
            out_shape=(sem, sem) + tuple(pltpu.HBM(a.shape, a.dtype) for a in ops),
            in_specs=[HBM] * len(ops), out_specs=(SEM, SEM) + (HBM,) * len(ops),
            input_output_aliases={i: 2 + i for i in range(len(ops))},
            compiler_params=pltpu.CompilerParams(has_side_effects=EFFECT),
        )(*[_in_hbm(a) for a in ops])
        return res[0], res[1], res[2:2 + n], res[2 + n:2 + n + m], list(res[2 + n + m:])

    def wait(self, send_sems, recv_sems, srcs, lands, thru):
        n, m = len(srcs), len(lands)
        ops = list(srcs) + list(lands) + list(thru)

        def body(*refs):
            land_refs = refs[n:n + m]
            ssem, rsem = refs[len(ops)], refs[len(ops) + 1]
            me = _me()
            spans = self.span_of(land_refs)
            for s in range(self.nsem):
                cp = pltpu.make_async_remote_copy(
                    src_ref=spans[s], dst_ref=spans[s], send_sem=ssem.at[s], recv_sem=rsem.at[s],
                    device_id=_coords(me), device_id_type=MESH)
                cp.wait_send()
                cp.wait_recv()

        res = pl.pallas_call(
            body, name=self.name + "_wait",
            out_shape=tuple(pltpu.HBM(a.shape, a.dtype) for a in ops),
            in_specs=[HBM] * len(ops) + [SEM, SEM], out_specs=(HBM,) * len(ops),
            input_output_aliases={i: i for i in range(len(ops))},
            compiler_params=pltpu.CompilerParams(has_side_effects=EFFECT),
        )(*ops, send_sems, recv_sems)
        return res[n:n + m], list(res[n + m:])


class ChipGather:
    def __init__(self, name, nsrc=6, nland=6, ncopy=6, views=None, slots=None):
        self.name, self.nsrc, self.nland, self.ncopy = name, nsrc, nland, ncopy
        self.views = views if views is not None else list
        self.slots = slots if slots is not None else self.layer_slots

    @staticmethod
    def layer_slots(lands, p):
        o_win, o_w3, o_w1, o_w2, o_rgw, o_sm = lands
        return [o_win.at[:, pl.ds(pl.multiple_of(p * WIN_S, 128), WIN_S)],
                o_w3.at[:, pl.ds(pl.multiple_of(p * 128, 128), 128), :],
                o_w1.at[:, p],
                o_w2.at[:, pl.ds(pl.multiple_of(p * W2S, 16), W2S), :],
                o_rgw.at[:, :, pl.ds(pl.multiple_of(p * 32, 32), 32), :],
                o_sm.at[:, pl.ds(pl.multiple_of(p * 128, 128), 128)]]

    @staticmethod
    def _places():
        x, y, c = lax.axis_index("x"), lax.axis_index("y"), lax.axis_index("c")
        chips = [(1 - x, y), (x, 1 - y), (1 - x, 1 - y)]
        return (x, y, c), (x, y, 1 - c), chips

    @staticmethod
    def _index(place):
        return 4 * place[0] + 2 * place[1] + place[2]

    def _call(self, tag, body, ops, sems_in, sems_out):
        nops = len(ops)
        sem = pltpu.SemaphoreType.DMA((max(sems_out, 1),))
        res = pl.pallas_call(
            body, name=f"{self.name}_{tag}",
            out_shape=((sem, sem) if sems_out else ()) + tuple(pltpu.HBM(a.shape, a.dtype) for a in ops),
            in_specs=[HBM] * nops + [SEM] * len(sems_in),
            out_specs=((SEM, SEM) if sems_out else ()) + (HBM,) * nops,
            input_output_aliases={i: (2 if sems_out else 0) + i for i in range(nops)},
            compiler_params=pltpu.CompilerParams(has_side_effects=EFFECT),
        )(*[_in_hbm(a) for a in ops], *sems_in)
        return res

    def first(self, srcs, lands, thru):
        ns, nl, n = self.nsrc, self.nland, self.ncopy
        ops = list(srcs) + list(lands) + list(thru)

        def body(*refs):
            src_refs, land_refs = self.views(refs[:ns]), refs[ns:ns + nl]
            send_sems, recv_sems = refs[len(ops)], refs[len(ops) + 1]
            me, sibling, chips = self._places()
            dst = self.slots(land_refs, self._index(me))
            targets = [sibling] + [(cx, cy, me[2]) for cx, cy in chips]
            for r, to in enumerate(targets):
                for k in range(n):
                    pltpu.make_async_remote_copy(
                        src_ref=src_refs[k], dst_ref=dst[k], send_sem=send_sems.at[n *r + k],
                        recv_sem=recv_sems.at[n *r + k], device_id=to, device_id_type=MESH).start()

        res = self._call("first", body, ops, [], 4 * n)
        return (res[0], res[1]), list(res[2:2 + ns]), list(res[2 + ns:2 + ns + nl]), list(res[2 + ns + nl:])

    def relay(self, sems, srcs, lands, thru):
        ns, nl, n = self.nsrc, self.nland, self.ncopy
        ops = list(srcs) + list(lands) + list(thru)

        def wait_body(*refs):
            land_refs = refs[ns:ns + nl]
            ssem, rsem = refs[len(ops)], refs[len(ops) + 1]
            me, sibling, chips = self._places()
            for j, (cx, cy) in enumerate(chips):
                got = self.slots(land_refs, self._index((cx, cy, me[2])))
                for k in range(n):
                    pltpu.make_async_remote_copy(
                        src_ref=got[k], dst_ref=got[k], send_sem=ssem.at[n *(1 + j) + k], recv_sem=rsem.at[n *(1 + j) + k],
                        device_id=me, device_id_type=MESH).wait_recv()

        ops = list(self._call("landed", wait_body, ops, list(sems), 0))

        def start_body(*refs):
            land_refs = refs[ns:ns + nl]
            ssem, rsem = refs[len(ops)], refs[len(ops) + 1]
            me, sibling, chips = self._places()
            for j, (cx, cy) in enumerate(chips):
                got = self.slots(land_refs, self._index((cx, cy, me[2])))
                for k in range(n):
                    pltpu.make_async_remote_copy(
                        src_ref=got[k], dst_ref=got[k], send_sem=ssem.at[n *j + k], recv_sem=rsem.at[n *j + k],
                        device_id=sibling, device_id_type=MESH).start()

        res = self._call("relay", start_body, ops, [], 3 * n)
        return (res[0], res[1]), list(res[2:2 + ns]), list(res[2 + ns:2 + ns + nl]), list(res[2 + ns + nl:])

    def last(self, sems, relay_sems, srcs, lands, thru):
        ns, nl, n = self.nsrc, self.nland, self.ncopy
        ops = list(srcs) + list(lands) + list(thru)

        def body(*refs):
            src_refs, land_refs = self.views(refs[:ns]), refs[ns:ns + nl]
            ssem, rsem, ssem2, rsem2 = refs[len(ops):len(ops) + 4]
            me, sibling, chips = self._places()
            got = self.slots(land_refs, self._index(sibling))
            for k in range(n):
                pltpu.make_async_remote_copy(
                    src_ref=got[k], dst_ref=got[k], send_sem=ssem.at[k], recv_sem=rsem.at[k],
                    device_id=me, device_id_type=MESH).wait_recv()
            for j, (cx, cy) in enumerate(chips):
                got = self.slots(land_refs, self._index((cx, cy, 1 - me[2])))
                for k in range(n):
                    pltpu.make_async_remote_copy(
                        src_ref=got[k], dst_ref=got[k], send_sem=ssem2.at[n *j + k], recv_sem=rsem2.at[n *j + k],
                        device_id=me, device_id_type=MESH).wait_recv()
            for sem_s, sem_r, count in ((ssem, rsem, 4), (ssem2, rsem2, 3)):
                for r in range(count):
                    for k in range(n):
                        pltpu.make_async_remote_copy(
                            src_ref=src_refs[k], dst_ref=src_refs[k], send_sem=sem_s.at[n *r + k],
                            recv_sem=sem_r.at[n *r + k], device_id=me, device_id_type=MESH).wait_send()

        res = self._call("last", body, ops, list(sems) + list(relay_sems), 0)
        return list(res[:ns]), list(res[ns:ns + nl]), list(res[ns + nl:])


GATHER_SHAPES = (SDS((D, D_IN), BF16), SDS((3, D, D), BF16), SDS((2, N_DEV, D, FS), BF16),
                 SDS((2, D_FF, D), BF16), SDS((2, HEADS, HD, HD), BF16), SDS((SMALL_ROWS, D), F32))


def _slab(p, n, align):
    return pl.ds(pl.multiple_of(p * n, align), n)


class GradGroup:
    def __init__(self, nsrc, lands, src_of, dst_of, sem_of, in_specs, out_specs, copy):
        self.nsrc, self.lands, self.src_of, self.dst_of, self.sem_of = nsrc, lands, src_of, dst_of, sem_of
        self.in_specs, self.out_specs, self.copy = in_specs, out_specs, copy


def _mixer_group():
    hd = D // 2

    def src_of(refs, p):
        dwin, dwoc, dwor, dwo, drgw = refs
        r3 = _slab(p, 128, 128)
        return [dwin.at[:, _slab(p, WIN_S, 128)], dwoc.at[r3, :], dwor.at[r3, :], dwo.at[r3, :],
                drgw.at[:, :, _slab(p, 32, 32), :]]

    def dst_of(lands, p):
        l_win, l_w3, l_rgw = lands
        return [l_win.at[p], l_w3.at[p, 0], l_w3.at[p, 1], l_w3.at[p, 2], l_rgw.at[p]]

    def copy(srcs, lands):
        lands[0][...] = srcs[0][...]
        for k in range(3):
            lands[1][k] = srcs[1 + k][...]
        lands[2][...] = srcs[4][...]

    three = pl.BlockSpec((64, D), lambda i, me: (2 * me[0] + i, 0))
    return GradGroup(
        5, (SDS((N_DEV, D, WIN_S), BF16), SDS((N_DEV, 3, 128, D), BF16), SDS((N_DEV, 2, HEADS, 32, HD), BF16)),
        src_of, dst_of, [0, 1, 1, 1, 2],
        [pl.BlockSpec((hd, WIN_S), lambda i, me: (i, me[0])), three, three, three,
         pl.BlockSpec((2, HEADS, 16, HD), lambda i, me: (0, 0, 2 * me[0] + i, 0))],
        [pl.BlockSpec((None, hd, WIN_S), lambda i, me: (me[0], i, 0)),
         pl.BlockSpec((None, 3, 64, D), lambda i, me: (me[0], 0, i, 0)),
         pl.BlockSpec((None, 2, HEADS, 16, HD), lambda i, me: (me[0], 0, 0, i, 0))],
        copy)


def _ffn_group():
    hd, hw = D // 2, W2S // 2

    def src_of(refs, p):
        return [refs[0].at[p], refs[1].at[_slab(p, W2S, 16), :]]

    def dst_of(lands, p):
        return [lands[0].at[p], lands[1].at[p]]

    def copy(srcs, lands):
        lands[0][...] = srcs[0][...]
        lands[1][...] = srcs[1][...]

    return GradGroup(
        2, (SDS((N_DEV, D, FS), BF16), SDS((N_DEV, W2S, D), BF16)), src_of, dst_of, [0, 1],
        [pl.BlockSpec((None, hd, FS), lambda i, me: (me[0], i, 0)), pl.BlockSpec((hw, D), lambda i, me: (2 * me[0] + i, 0))],
        [pl.BlockSpec((None, hd, FS), lambda i, me: (me[0], i, 0)), pl.BlockSpec((None, hw, D), lambda i, me: (me[0], i, 0))],
        copy)


def grad_scatter(name, kinds):
    groups = [_mixer_group() if k == "mixer" else _ffn_group() for k in kinds]

    def per_group(refs, counts, fn):
        out, i = [], 0
        for g, c in zip(groups, counts):
            out += fn(g, refs[i:i + c])
            i += c
        return out

    nsrcs = [g.nsrc for g in groups]
    nlands = [len(g.lands) for g in groups]
    sem_of, off = [], 0
    for g in groups:
        sem_of += [off + s for s in g.sem_of]
        off += len(g.lands)

    ex = Exchange(name,
                  lambda refs, p: per_group(refs, nsrcs, lambda g, r: g.src_of(r, p)),
                  lambda lands, p: per_group(lands, nlands, lambda g, r: g.dst_of(r, p)),
                  sem_of, lambda lands: [a.at[pl.ds(0, 7)] for a in lands])

    def prep(me1, grads):
        nsrc = sum(nsrcs)

        def body(me_ref, *refs):
            srcs, lands = refs[:nsrc], refs[nsrc:]
            i = j = 0
            for g in groups:
                g.copy(srcs[i:i + g.nsrc], lands[j:j + len(g.lands)])
                i += g.nsrc
                j += len(g.lands)

        return list(pl.pallas_call(
            body, name=name + "_prep", out_shape=tuple(s for g in groups for s in g.lands),
            grid_spec=pltpu.PrefetchScalarGridSpec(
                num_scalar_prefetch=1, grid=(2,), in_specs=[s for g in groups for s in g.in_specs],
                out_specs=tuple(s for g in groups for s in g.out_specs)),
            compiler_params=_cp("arbitrary"),
        )(me1, *grads))

    return ex, prep


def gather_prep(l, me1, w_in, w_out_conv, w_out_rnn, w_o, ffn_w1, ffn_w2, rg_w, small):
    hd, hw = D // 2, W2S // 2

    def body(me_ref, win, woc, wor, wo, w1, w2, rgw, sm, c_win, c_w3, c_w1, c_w2, c_rgw, c_sm,
             o_win, o_w3, o_w1, o_w2, o_rgw, o_sm):
        a = win[...].astype(BF16)
        c_win[...] = a
        o_win[...] = a
        for k, r in enumerate((woc, wor, wo)):
            b = r[...].astype(BF16)
            c_w3[k] = b
            o_w3[k] = b
        for src, comp, own in ((w1, c_w1, o_w1), (w2, c_w2, o_w2), (rgw, c_rgw, o_rgw)):
            b = src[...].astype(BF16)
            comp[...] = b
            own[...] = b
        c_sm[...] = sm[...]
        o_sm[...] = sm[...]

    three = pl.BlockSpec((None, 64, D), lambda i, me: (l, i, 0))
    in_specs = [pl.BlockSpec((None, hd, WIN_S), lambda i, me: (l, i, 0)), three, three, three,
                pl.BlockSpec((None, 2, hd, FS), lambda i, me: (l, 0, i, 0)),
                pl.BlockSpec((None, 2, hw, D), lambda i, me: (l, 0, i, 0)),
                pl.BlockSpec((None, 2, HEADS, 16, HD), lambda i, me: (l, 0, 0, i, 0)),
                pl.BlockSpec((None, 8, 128), lambda i, me: (l, i, 0))]
    out_specs = (pl.BlockSpec((hd, WIN_S), lambda i, me: (i, 0)), pl.BlockSpec((3, 64, D), lambda i, me: (0, i, 0)),
                 pl.BlockSpec((2, hd, FS), lambda i, me: (0, i, 0)), pl.BlockSpec((2, hw, D), lambda i, me: (0, i, 0)),
                 pl.BlockSpec((2, HEADS, 16, HD), lambda i, me: (0, 0, i, 0)), pl.BlockSpec((8, 128), lambda i, me: (i, 0)),
                 pl.BlockSpec((hd, WIN_S), lambda i, me: (i, me[0])),
                 pl.BlockSpec((3, 64, D), lambda i, me: (0, 2 * me[0] + i, 0)),
                 pl.BlockSpec((2, None, hd, FS), lambda i, me: (0, me[0], i, 0)),
                 pl.BlockSpec((2, hw, D), lambda i, me: (0, 2 * me[0] + i, 0)),
                 pl.BlockSpec((2, HEADS, 16, HD), lambda i, me: (0, 0, 2 * me[0] + i, 0)),
                 pl.BlockSpec((8, 128), lambda i, me: (i, me[0])))
    compact = (SDS((D, WIN_S), BF16), SDS((3, 128, D), BF16), SDS((2, D, FS), BF16), SDS((2, W2S, D), BF16),
               SDS((2, HEADS, 32, HD), BF16), SDS((SMALL_ROWS, 128), F32))
    res = pl.pallas_call(
        body, name=f"gather_prep{l}", out_shape=compact + GATHER_SHAPES,
        grid_spec=pltpu.PrefetchScalarGridSpec(num_scalar_prefetch=1, grid=(2,), in_specs=in_specs, out_specs=out_specs),
        compiler_params=_cp("arbitrary"),
    )(me1, w_in, w_out_conv, w_out_rnn, w_o, ffn_w1, ffn_w2, rg_w, small)
    return list(res[:6]), list(res[6:])


def gather_small_grads(g_sharded, g_replicated):
    def body(ga, gb, la, lb, send_sems, recv_sems, loc_sems):
        _all_to_all_copies(lambda p: [ga, gb], lambda p: [la.at[p], lb.at[p]],
                           [(la.at[pl.ds(0, 7)], 0), (lb.at[pl.ds(0, 7)], 1)], send_sems, recv_sems, loc_sems)

    return pl.pallas_call(
        body, name="gather_small_grads",
        out_shape=(SDS((N_DEV,) + g_sharded.shape, F32), SDS((N_DEV,) + g_replicated.shape, F32)),
        in_specs=[ANY, ANY], out_specs=(ANY, ANY),
        scratch_shapes=[pltpu.SemaphoreType.DMA((2,)), pltpu.SemaphoreType.DMA((2,)), pltpu.SemaphoreType.DMA((2,))],
        compiler_params=pltpu.CompilerParams(has_side_effects=True),
    )(g_sharded, g_replicated)


def ffn_up(xb, w1, f):
    t = xb.shape[0]
    tm = _row_tile(t, 1024)

    def body(x_ref, wg_ref, wu_ref, g_ref, u_ref, a_ref):
        x = x_ref[...]
        g = _dot(x, wg_ref[...], NN)
        u = _dot(x, wu_ref[...], NN)
        g_ref[...] = g.astype(BF16)
        u_ref[...] = u.astype(BF16)
        a_ref[...] = (g * _sigmoid_t(g) * u).astype(BF16)

    out = pl.BlockSpec((None, tm, FS), lambda j, i: (j, i, 0))
    return pl.pallas_call(
        body, name=f"ffn_up{f}", grid=(4, t // tm),
        in_specs=[pl.BlockSpec((tm, D), lambda j, i: (i, 0)),
                  pl.BlockSpec((None, None, D, FS), lambda j, i: (f, j, 0, 0)),
                  pl.BlockSpec((None, None, D, FS), lambda j, i: (f, j + 4, 0, 0))],
        out_specs=(out, out, out), out_shape=(SDS((4, t, FS), BF16),) * 3,
        compiler_params=_cp("parallel", "parallel"),
    )(xb, w1, w1)


def ffn_down_ln(a, w2, f, xf, small, s):
    t = xf.shape[0]
    tm = _row_tile(t, 512)

    def body(a_ref, w_ref, x_ref, sm_ref, xo_ref, xb_ref, xh_ref, rs_ref):
        acc = _dot(a_ref[0], w_ref[0:FS, :], NN)
        for j in range(1, 4):
            acc = acc + _dot(a_ref[j], w_ref[j * FS:(j + 1) * FS, :], NN)
        z = ALPHA * x_ref[...] + 0.5 * acc
        y, xh, rstd = _ln_fwd(z, sm_ref[R_LNG + s:R_LNG + s + 1, :], sm_ref[R_LNB + s:R_LNB + s + 1, :])
        xo_ref[...] = y
        xb_ref[...] = y.astype(BF16)
        xh_ref[...] = xh
        rs_ref[...] = rstd

    row = pl.BlockSpec((tm, D), lambda i: (i, 0))
    return pl.pallas_call(
        body, name=f"ffn_down_ln{f}", grid=(t // tm,),
        in_specs=[pl.BlockSpec((4, tm, FS), lambda i: (0, i, 0)),
                  pl.BlockSpec((None, D_FF, D), lambda i: (f, 0, 0)),
                  row, pl.BlockSpec((SMALL_ROWS, D), lambda i: (0, 0))],
        out_specs=(row, row, row, pl.BlockSpec((tm, 1), lambda i: (i, 0))),
        out_shape=(SDS((t, D), F32), SDS((t, D), BF16), SDS((t, D), F32), SDS((t, 1), F32)),
        compiler_params=_cp("parallel"),
    )(a, w2, xf, small)


def ffn_bwd_gates(dxn, xh, rstd, small, s, w2, f, g, u):
    t = dxn.shape[0]
    tm = _row_tile(t, 256)

    def body(dy_ref, xh_ref, rs_ref, sm_ref, w_ref, g_ref, u_ref, dz_ref, df_ref, dgu_ref, dln_ref):
        i = pl.program_id(0)
        dy = dy_ref[...]
        xhat = xh_ref[...]
        dz = _ln_bwd(dy, xhat, rs_ref[...], sm_ref[R_LNG + s:R_LNG + s + 1, :])

        @pl.when(i == 0)
        def _():
            dln_ref[...] = jnp.zeros_like(dln_ref)

        dln_ref[0:1, :] += jnp.sum(dy * xhat, axis=0, keepdims=True)
        dln_ref[1:2, :] += jnp.sum(dy, axis=0, keepdims=True)
        dz_ref[...] = dz
        df = (0.5 * dz).astype(BF16)
        df_ref[...] = df
        for j in range(4):
            da = _dot(df, w_ref[j * FS:(j + 1) * FS, :], NT)
            gg = g_ref[j].astype(F32)
            uu = u_ref[j].astype(F32)
            sg = _sigmoid_t(gg)
            dgu_ref[j] = (da * uu * (sg * (1.0 + gg * (1.0 - sg)))).astype(BF16)
            dgu_ref[j + 4] = (da * (gg * sg)).astype(BF16)

    row = pl.BlockSpec((tm, D), lambda i: (i, 0))
    gu = pl.BlockSpec((4, tm, FS), lambda i: (0, i, 0))
    return pl.pallas_call(
        body, name=f"ffn_bwd_gates{f}", grid=(t // tm,),
        in_specs=[row, row, pl.BlockSpec((tm, 1), lambda i: (i, 0)), pl.BlockSpec((SMALL_ROWS, D), lambda i: (0, 0)),
                  pl.BlockSpec((None, D_FF, D), lambda i: (f, 0, 0)), gu, gu],
        out_specs=(row, row, pl.BlockSpec((8, tm, FS), lambda i: (0, i, 0)), pl.BlockSpec((2, D), lambda i: (0, 0))),
        out_shape=(SDS((t, D), F32), SDS((t, D), BF16), SDS((8, t, FS), BF16), SDS((2, D), F32)),
        compiler_params=_cp("arbitrary"),
    )(dxn, xh, rstd, small, w2, g, u)


def ffn_bwd_dx(dz, dgu, w1, f):
    t = dz.shape[0]
    tm = _row_tile(t, 1024)

    def body(dz_ref, d_ref, w_ref, dx_ref, acc):
        k = pl.program_id(1)

        @pl.when(k == 0)
        def _():
            acc[...] = ALPHA * dz_ref[...]

        acc[...] += _dot(d_ref[...], w_ref[...], NT)

        @pl.when(k == 7)
        def _():
            dx_ref[...] = acc[...]

    row = pl.BlockSpec((tm, D), lambda i, k: (i, 0))
    return pl.pallas_call(
        body, name=f"ffn_bwd_dx{f}", grid=(t // tm, 8),
        in_specs=[row, pl.BlockSpec((None, tm, FS), lambda i, k: (k, i, 0)),
                  pl.BlockSpec((None, None, D, FS), lambda i, k: (f, k, 0, 0))],
        out_specs=row, out_shape=SDS((t, D), F32),
        scratch_shapes=[pltpu.VMEM((tm, D), F32)],
        compiler_params=_cp("parallel", "arbitrary"),
    )(dz, dgu, w1)


def _mm_tn(name, a, a_spec, b, b_spec, out_sds, out_spec, grid):
    def body(a_ref, b_ref, o_ref):
        o_ref[...] = _dot(a_ref[...], b_ref[...], TN).astype(o_ref.dtype)

    return pl.pallas_call(
        body, name=name, grid=grid, in_specs=[a_spec, b_spec], out_specs=out_spec, out_shape=out_sds,
        compiler_params=_cp(*(["parallel"] * len(grid))),
    )(a, b)


def ffn_dw1(xb, dgu, f):
    t = xb.shape[0]
    return _mm_tn(f"ffn_dw1_{f}", xb, pl.BlockSpec((t, D), lambda j: (0, 0)),
                  dgu, pl.BlockSpec((None, t, FS), lambda j: (j, 0, 0)),
                  SDS((8, D, FS), BF16), pl.BlockSpec((None, D, FS), lambda j: (j, 0, 0)), (8,))


def ffn_dw2(a, df, f):
    t = df.shape[0]
    return _mm_tn(f"ffn_dw2_{f}", a, pl.BlockSpec((None, t, FS), lambda j: (j, 0, 0)),
                  df, pl.BlockSpec((t, D), lambda j: (0, 0)),
                  SDS((D_FF, D), BF16), pl.BlockSpec((FS, D), lambda j: (j, 0)), (4,))


def mm_tn_square(name, a, b):
    t = a.shape[0]
    return _mm_tn(name, a, pl.BlockSpec((t, 512), lambda i: (0, i)),
                  b, pl.BlockSpec((t, D), lambda i: (0, 0)),
                  SDS((D, D), BF16), pl.BlockSpec((512, D), lambda i: (i, 0)), (2,))


def mix_proj(xb, win):
    t = xb.shape[0]
    tm = _row_tile(t, 1024)

    def body(x_ref, w_ref, o_ref):
        o_ref[...] = _dot(x_ref[...], w_ref[...], NN).astype(BF16)

    return pl.pallas_call(
        body, name="mix_proj", grid=(7, t // tm),
        in_specs=[pl.BlockSpec((tm, D), lambda n, i: (i, 0)), pl.BlockSpec((D, D), lambda n, i: (0, n))],
        out_specs=pl.BlockSpec((tm, D), lambda n, i: (i, n)), out_shape=SDS((t, D_IN), BF16),
        compiler_params=_cp("parallel", "parallel"),
    )(xb, win)


def _pcol(tm, k):
    return pl.BlockSpec((tm, D), lambda i: (i, k))


def _prev_halo(tm, k):
    return pl.BlockSpec((8, D), lambda i: (jnp.maximum(i * (tm // 8) - 1, 0), k))


def _prev_halo16(tm, k):
    return pl.BlockSpec((16, D), lambda i: (jnp.maximum(i * (tm // 16) - 1, 0), k))


def _next_halo(tm, t, k):
    return pl.BlockSpec((8, D), lambda i: (jnp.minimum((i + 1) * (tm // 8), t // 8 - 1), k))


def _full(shape):
    nd = len(shape)
    return pl.BlockSpec(shape, lambda i: (0,) * nd)


def mix_pre(p, small, rcb, ap, rgw):
    t = p.shape[0]
    tm = _row_tile(t, 256)

    def body(bg_ref, cg_ref, v_ref, xr_ref, cgh_ref, vh_ref, xrh_ref, sm_ref, rcb_ref, ap_ref, rgw_ref,
             ca_ref, pa_ref, xc_ref, xcb_ref, gi_ref, gr_ref, h_ref, ext1, ext2, a_s, b_s, carry):
        i = pl.program_id(0)
        first = i == 0
        cv = cg_ref[...].astype(F32) * v_ref[...].astype(F32)
        ext1[0:16, :] = jnp.where(first, 0.0, cgh_ref[...].astype(F32) * vh_ref[...].astype(F32))
        ext1[16:, :] = cv
        xr = xr_ref[...].astype(F32)
        ext2[0:16, :] = jnp.where(first, 0.0, xrh_ref[...].astype(F32))
        ext2[16:, :] = xr
        ca = (sm_ref[R_SC:R_SC + 1, :] * ext1[pl.ds(14, tm), :] + sm_ref[R_SC + 1:R_SC + 2, :] * ext1[pl.ds(15, tm), :]
              + sm_ref[R_SC + 2:R_SC + 3, :] * cv)
        ca_ref[...] = ca.astype(BF16)
        pa_ref[...] = (bg_ref[...].astype(F32) * ca).astype(BF16)
        xc = (sm_ref[R_RC:R_RC + 1, :] * ext2[pl.ds(13, tm), :] + sm_ref[R_RC + 1:R_RC + 2, :] * ext2[pl.ds(14, tm), :]
              + sm_ref[R_RC + 2:R_RC + 3, :] * ext2[pl.ds(15, tm), :] + sm_ref[R_RC + 3:R_RC + 4, :] * xr
              + rcb_ref[...])
        xc_ref[...] = xc
        xcb = xc.astype(BF16)
        xcb_ref[...] = xcb
        g0, g1 = [], []
        for h in range(HEADS):
            xh = xcb[:, h * HD:(h + 1) * HD]
            g0.append(_dot(xh, rgw_ref[0, h], NN))
            g1.append(_dot(xh, rgw_ref[1, h], NN))
        gi = _sigmoid(jnp.concatenate(g0, axis=1) + sm_ref[R_RGB:R_RGB + 1, :])
        gr = _sigmoid(jnp.concatenate(g1, axis=1) + sm_ref[R_RGB + 1:R_RGB + 2, :])
        gi_ref[...] = gi
        gr_ref[...] = gr
        la = (-LRU_C) * gr * _softplus(-ap_ref[...])
        a_s[...] = jnp.exp(la)
        row = lax.broadcasted_iota(jnp.int32, (tm, D), 0) + i * tm
        mult = jnp.where(row == 0, 1.0, jnp.sqrt(_neg_expm1(2.0 * la)))
        b_s[...] = xc * gi * mult

        @pl.when(first)
        def _():
            carry[...] = jnp.zeros_like(carry)

        carry[...] = _scan_tile(a_s, b_s, a_s, carry[...], tm, reverse=False)
        h_ref[...] = a_s[...].astype(BF16)

    row = pl.BlockSpec((tm, D), lambda i: (i, 0))
    f32o, b16o = SDS((t, D), F32), SDS((t, D), BF16)
    ext, tile = pltpu.VMEM((tm + 16, D), F32), pltpu.VMEM((tm, D), F32)
    return pl.pallas_call(
        body, name="mix_pre", grid=(t // tm,),
        in_specs=[_pcol(tm, 0), _pcol(tm, 1), _pcol(tm, 2), _pcol(tm, 3),
                  _prev_halo16(tm, 1), _prev_halo16(tm, 2), _prev_halo16(tm, 3),
                  _full((SMALL_ROWS, D)), _full((1, D)), _full((1, D)), _full((2, HEADS, HD, HD))],
        out_specs=(row,) * 7, out_shape=(b16o, b16o, f32o, b16o, f32o, f32o, b16o),
        scratch_shapes=[ext, ext, tile, tile, pltpu.VMEM((8, D), F32)],
        compiler_params=_cp("arbitrary"),
    )(p, p, p, p, p, p, p, small, rcb, ap, rgw)


def _scan_tile(a_ref, b_ref, o_ref, carry, tm, reverse):
    width = a_ref.shape[1]
    ng = tm // 8
    row8 = lax.broadcasted_iota(jnp.int32, (8, width), 0)

    def step(g, c):
        r = pl.multiple_of((ng - 1 - g if reverse else g) * 8, 8)
        aa = a_ref[pl.ds(r, 8), :]
        bb = b_ref[pl.ds(r, 8), :]
        for s in (1, 2, 4):
            if reverse:
                keep, shift = row8 < 8 - s, 8 - s
            else:
                keep, shift = row8 >= s, s
            a_sh = jnp.where(keep, pltpu.roll(aa, shift, 0), 1.0)
            b_sh = jnp.where(keep, pltpu.roll(bb, shift, 0), 0.0)
            bb = aa * b_sh + bb
            aa = aa * a_sh
        o = aa * c + bb
        o_ref[pl.ds(r, 8), :] = o
        edge = o[0:1, :] if reverse else o[7:8, :]
        return jnp.broadcast_to(edge, (8, width))

    return lax.fori_loop(0, ng, step, carry)


def mix_out(pa, h, p, bm, w3, xf, small):
    t = xf.shape[0]
    tm = _row_tile(t, 256)

    def body(pa_ref, h_ref, yr_ref, gla_ref, glb_ref, bma_ref, bmb_ref, w_ref, x_ref, sm_ref,
             pb_ref, ya_ref, yb_ref, m_ref, xo_ref, xb_ref, xh_ref, rs_ref):
        ge, _ = _gelu(yr_ref[...].astype(F32))
        pb = (h_ref[...].astype(F32) * ge).astype(BF16)
        pb_ref[...] = pb
        ya = _dot(pa_ref[...], w_ref[0], NN)
        yb = _dot(pb, w_ref[1], NN)
        ya_ref[...] = ya.astype(BF16)
        yb_ref[...] = yb.astype(BF16)
        ga = _sigmoid_t(gla_ref[...].astype(F32) + bma_ref[...])
        gb = _sigmoid_t(glb_ref[...].astype(F32) + bmb_ref[...])
        m = (ga * ya + gb * yb).astype(BF16)
        m_ref[...] = m
        z = ALPHA * x_ref[...] + _dot(m, w_ref[2], NN)
        y, xh, rstd = _ln_fwd(z, sm_ref[R_LNG + 1:R_LNG + 2, :], sm_ref[R_LNB + 1:R_LNB + 2, :])
        xo_ref[...] = y
        xb_ref[...] = y.astype(BF16)
        xh_ref[...] = xh
        rs_ref[...] = rstd

    row = pl.BlockSpec((tm, D), lambda i: (i, 0))
    f32o, b16o = SDS((t, D), F32), SDS((t, D), BF16)
    return pl.pallas_call(
        body, name="mix_out", grid=(t // tm,),
        in_specs=[row, row, _pcol(tm, 4), _pcol(tm, 5), _pcol(tm, 6),
                  pl.BlockSpec((1, D), lambda i: (0, 0)), pl.BlockSpec((1, D), lambda i: (0, 1)),
                  _full((3, D, D)), row, _full((SMALL_ROWS, D))],
        out_specs=(row,) * 7 + (pl.BlockSpec((tm, 1), lambda i: (i, 0)),),
        out_shape=(b16o, b16o, b16o, b16o, f32o, b16o, f32o, SDS((t, 1), F32)),
        compiler_params=_cp("parallel"),
    )(pa, h, p, p, p, bm, bm, w3, xf, small)


def mixb_head(dxn, xh, rstd, small, w3, p, bm, ya, yb, ca, h):
    t = dxn.shape[0]
    tm = _row_tile(t, 256)

    def body(dy_ref, xh_ref, rs_ref, sm_ref, w_ref, bg_ref, yr_ref, gla_ref, glb_ref, bma_ref, bmb_ref,
             ya_ref, yb_ref, ca_ref, h_ref,
             dz_ref, dzb_ref, dya_ref, dyb_ref, dbg_ref, dca_ref, dh_ref, dphi_ref, dln_ref, dbm_ref):
        i = pl.program_id(0)
        dy = dy_ref[...]
        xhat = xh_ref[...]
        dz = _ln_bwd(dy, xhat, rs_ref[...], sm_ref[R_LNG + 1:R_LNG + 2, :])

        @pl.when(i == 0)
        def _():
            dln_ref[...] = jnp.zeros_like(dln_ref)
            dbm_ref[...] = jnp.zeros_like(dbm_ref)

        dln_ref[0:1, :] += jnp.sum(dy * xhat, axis=0, keepdims=True)
        dln_ref[1:2, :] += jnp.sum(dy, axis=0, keepdims=True)
        dz_ref[...] = dz
        dzb = dz.astype(BF16)
        dzb_ref[...] = dzb
        dm = _dot(dzb, w_ref[2], NT)
        ga = _sigmoid_t(gla_ref[...].astype(F32) + bma_ref[...])
        gb = _sigmoid_t(glb_ref[...].astype(F32) + bmb_ref[...])
        dya = (dm * ga).astype(BF16)
        dyb = (dm * gb).astype(BF16)
        dya_ref[...] = dya
        dyb_ref[...] = dyb
        dgla = dm * ya_ref[...].astype(F32) * ga * (1.0 - ga)
        dglb = dm * yb_ref[...].astype(F32) * gb * (1.0 - gb)
        dbm_ref[0:1, :] += jnp.sum(dgla, axis=0, keepdims=True)
        dbm_ref[1:2, :] += jnp.sum(dglb, axis=0, keepdims=True)
        dphi_ref[:, D:2 * D] = dgla.astype(BF16)
        dphi_ref[:, 2 * D:3 * D] = dglb.astype(BF16)
        dpa = _dot(dya, w_ref[0], NT)
        dpb = _dot(dyb, w_ref[1], NT)
        dbg_ref[...] = (dpa * ca_ref[...].astype(F32)).astype(BF16)
        dca_ref[...] = dpa * bg_ref[...].astype(F32)
        yr = yr_ref[...].astype(F32)
        ge, th = _gelu(yr)
        dh_ref[...] = (dpb * ge).astype(BF16)
        dphi_ref[:, 0:D] = (dpb * h_ref[...].astype(F32) * _gelu_grad(yr, th)).astype(BF16)

    row = pl.BlockSpec((tm, D), lambda i: (i, 0))
    f32o, b16o = SDS((t, D), F32), SDS((t, D), BF16)
    acc2 = pl.BlockSpec((2, D), lambda i: (0, 0))
    return pl.pallas_call(
        body, name="mixb_head", grid=(t // tm,),
        in_specs=[row, row, pl.BlockSpec((tm, 1), lambda i: (i, 0)), _full((SMALL_ROWS, D)), _full((3, D, D)),
                  _pcol(tm, 0), _pcol(tm, 4), _pcol(tm, 5), _pcol(tm, 6),
                  pl.BlockSpec((1, D), lambda i: (0, 0)), pl.BlockSpec((1, D), lambda i: (0, 1)),
                  row, row, row, row],
        out_specs=(row,) * 7 + (pl.BlockSpec((tm, 3 * D), lambda i: (i, 0)), acc2, acc2),
        out_shape=(f32o, b16o, b16o, b16o, b16o, f32o, b16o, SDS((t, 3 * D), BF16), SDS((2, D), F32), SDS((2, D), F32)),
        compiler_params=_cp("arbitrary"),
    )(dxn, xh, rstd, small, w3, p, p, p, p, bm, bm, ya, yb, ca, h)


def mixb_rec(dh, gr, gi, h, xc, ap, rgw):
    t = dh.shape[0]
    tm = _row_tile(t, 256)
    nt = t // tm

    def body(dh_ref, gr_ref, gi_ref, h_ref, hh_ref, xc_ref, ap_ref, rgw_ref, dg_ref, dxc_ref, red_ref,
             ext, ext_a, c_s, lam_s, lam_c, a_c):
        i = nt - 1 - pl.program_id(0)
        first = i == 0

        @pl.when(pl.program_id(0) == 0)
        def _():
            red_ref[...] = jnp.zeros_like(red_ref)
            lam_c[...] = jnp.zeros_like(lam_c)
            a_c[...] = jnp.zeros_like(a_c)

        ext[0:8, :] = jnp.where(first, 0.0, hh_ref[...].astype(F32)[8:16, :])
        ext[8:, :] = h_ref[...].astype(F32)
        hprev = ext[pl.ds(7, tm), :]
        lam_s[...] = dh_ref[...].astype(F32)
        gr = gr_ref[...]
        gi = gi_ref[...]
        xc = xc_ref[...]
        ap = ap_ref[...]
        sp = _softplus(-ap)
        la = (-LRU_C) * gr * sp
        a = jnp.exp(la)
        ext_a[0:tm, :] = a
        ext_a[tm:, :] = a_c[...]
        c_s[...] = ext_a[pl.ds(1, tm), :]
        lam_c[...] = _scan_tile(c_s, lam_s, lam_s, lam_c[...], tm, reverse=True)
        a_c[...] = jnp.broadcast_to(a[0:1, :], (8, D))
        lam = lam_s[...]
        row = lax.broadcasted_iota(jnp.int32, (tm, D), 0) + i * tm
        start = row == 0
        mult = jnp.where(start, 1.0, jnp.sqrt(_neg_expm1(2.0 * la)))
        dmult = jnp.where(start, 0.0, lam * xc * gi)
        dla = lam * hprev * a - dmult * a * a / mult
        dg1 = (-LRU_C) * sp * dla * gr * (1.0 - gr)
        dg0 = lam * xc * mult * gi * (1.0 - gi)
        dsp = jnp.sum((-LRU_C) * gr * dla, axis=0, keepdims=True)
        red_ref[0:1, :] += jnp.sum(dg0, axis=0, keepdims=True)
        red_ref[1:2, :] += jnp.sum(dg1, axis=0, keepdims=True)
        red_ref[2:3, :] += -dsp * _sigmoid(-ap)
        dg0b = dg0.astype(BF16)
        dg1b = dg1.astype(BF16)
        dg_ref[0] = dg0b
        dg_ref[1] = dg1b
        parts = []
        for hd in range(HEADS):
            sl = slice(hd * HD, (hd + 1) * HD)
            parts.append(_dot(dg0b[:, sl], rgw_ref[0, hd], NT) + _dot(dg1b[:, sl], rgw_ref[1, hd], NT))
        dxc = lam * gi * mult + jnp.concatenate(parts, axis=1)
        dxc_ref[...] = dxc
        red_ref[3:4, :] += jnp.sum(dxc, axis=0, keepdims=True)

    row = pl.BlockSpec((tm, D), lambda i: (nt - 1 - i, 0))
    halo = pl.BlockSpec((16, D), lambda i: (jnp.maximum((nt - 1 - i) * (tm // 16) - 1, 0), 0))
    ext, tile, edge = pltpu.VMEM((tm + 8, D), F32), pltpu.VMEM((tm, D), F32), pltpu.VMEM((8, D), F32)
    return pl.pallas_call(
        body, name="mixb_rec", grid=(nt,),
        in_specs=[row, row, row, row, halo, row, _full((1, D)), _full((2, HEADS, HD, HD))],
        out_specs=(pl.BlockSpec((2, tm, D), lambda i: (0, nt - 1 - i, 0)), row, pl.BlockSpec((8, D), lambda i: (0, 0))),
        out_shape=(SDS((2, t, D), BF16), SDS((t, D), F32), SDS((8, D), F32)),
        scratch_shapes=[ext, ext, tile, tile, edge, edge],
        compiler_params=_cp("arbitrary"),
    )(dh, gr, gi, h, h, xc, ap, rgw)


def mixb_conv(dca, dxc, dbg, p, small):
    t = dca.shape[0]
    tm = _row_tile(t, 256)
    nt = t // tm

    def body(dca_ref, dcan_ref, dxc_ref, dxcn_ref, dbg_ref, cg_ref, v_ref, xr_ref, cgh_ref, vh_ref, xrh_ref, sm_ref,
             dplo_ref, dxr_ref, red_ref, e_dca, e_dxc, e_cv, e_xr):
        i = pl.program_id(0)
        first = i == 0
        last = i == nt - 1

        @pl.when(first)
        def _():
            red_ref[...] = jnp.zeros_like(red_ref)

        dca = dca_ref[...]
        dxc = dxc_ref[...]
        e_dca[0:tm, :] = dca
        e_dca[tm:, :] = jnp.where(last, 0.0, dcan_ref[...])
        e_dxc[0:tm, :] = dxc
        e_dxc[tm:, :] = jnp.where(last, 0.0, dxcn_ref[...])
        cg = cg_ref[...].astype(F32)
        v = v_ref[...].astype(F32)
        xr = xr_ref[...].astype(F32)
        e_cv[0:8, :] = jnp.where(first, 0.0, (cgh_ref[...].astype(F32) * vh_ref[...].astype(F32))[8:16, :])
        e_cv[8:, :] = cg * v
        e_xr[0:8, :] = jnp.where(first, 0.0, xrh_ref[...].astype(F32)[8:16, :])
        e_xr[8:, :] = xr
        dcv = (sm_ref[R_SC + 2:R_SC + 3, :] * dca + sm_ref[R_SC + 1:R_SC + 2, :] * e_dca[pl.ds(1, tm), :]
               + sm_ref[R_SC:R_SC + 1, :] * e_dca[pl.ds(2, tm), :])
        dplo_ref[:, 0:D] = dbg_ref[...]
        dplo_ref[:, D:2 * D] = (dcv * v).astype(BF16)
        dplo_ref[:, 2 * D:3 * D] = (dcv * cg).astype(BF16)
        dxr = (sm_ref[R_RC + 3:R_RC + 4, :] * dxc + sm_ref[R_RC + 2:R_RC + 3, :] * e_dxc[pl.ds(1, tm), :]
               + sm_ref[R_RC + 1:R_RC + 2, :] * e_dxc[pl.ds(2, tm), :] + sm_ref[R_RC:R_RC + 1, :] * e_dxc[pl.ds(3, tm), :])
        dxr_ref[...] = dxr.astype(BF16)
        for k in range(3):
            red_ref[R_SC + k:R_SC + k + 1, :] += jnp.sum(dca * e_cv[pl.ds(6 + k, tm), :], axis=0, keepdims=True)
        for k in range(4):
            red_ref[R_RC + k:R_RC + k + 1, :] += jnp.sum(dxc * e_xr[pl.ds(5 + k, tm), :], axis=0, keepdims=True)

    row = pl.BlockSpec((tm, D), lambda i: (i, 0))
    ext = pltpu.VMEM((tm + 8, D), F32)
    return pl.pallas_call(
        body, name="mixb_conv", grid=(nt,),
        in_specs=[row, _next_halo(tm, t, 0), row, _next_halo(tm, t, 0), row,
                  _pcol(tm, 1), _pcol(tm, 2), _pcol(tm, 3), _prev_halo16(tm, 1), _prev_halo16(tm, 2), _prev_halo16(tm, 3),
                  _full((SMALL_ROWS, D))],
        out_specs=(pl.BlockSpec((tm, 3 * D), lambda i: (i, 0)), row, pl.BlockSpec((8, D), lambda i: (0, 0))),
        out_shape=(SDS((t, 3 * D), BF16), SDS((t, D), BF16), SDS((8, D), F32)),
        scratch_shapes=[ext, ext, ext, ext],
        compiler_params=_cp("arbitrary"),
    )(dca, dca, dxc, dxc, dbg, p, p, p, p, p, p, small)


def mixb_dx(dz, dplo, dxr, dphi, win):
    t = dz.shape[0]
    tm = _row_tile(t, 1024)

    def body(dz_ref, lo_ref, xr_ref, hi_ref, w_ref, dx_ref, acc):
        k = pl.program_id(1)

        @pl.when(k == 0)
        def _():
            acc[...] = ALPHA * dz_ref[...]

        @pl.when(k < 3)
        def _():
            acc[...] += _dot(lo_ref[...], w_ref[...], NT)

        @pl.when(k == 3)
        def _():
            acc[...] += _dot(xr_ref[...], w_ref[...], NT)

        @pl.when(k > 3)
        def _():
            acc[...] += _dot(hi_ref[...], w_ref[...], NT)

        @pl.when(k == 6)
        def _():
            dx_ref[...] = acc[...]

    row = pl.BlockSpec((tm, D), lambda i, k: (i, 0))
    return pl.pallas_call(
        body, name="mixb_dx", grid=(t // tm, 7),
        in_specs=[row, pl.BlockSpec((tm, D), lambda i, k: (i, jnp.minimum(k, 2))), row,
                  pl.BlockSpec((tm, D), lambda i, k: (i, jnp.clip(k - 4, 0, 2))),
                  pl.BlockSpec((D, D), lambda i, k: (0, k))],
        out_specs=row, out_shape=SDS((t, D), F32),
        scratch_shapes=[pltpu.VMEM((tm, D), F32)],
        compiler_params=_cp("parallel", "arbitrary"),
    )(dz, dplo, dxr, dphi, win)


def mixb_dwin(xb, dplo, dxr, dphi):
    t = xb.shape[0]
    tk = _row_tile(t, 2048)
    nk = t // tk

    def body(x_ref, lo_ref, xr_ref, hi_ref, o_ref, acc):
        n = pl.program_id(0)
        k = pl.program_id(1)

        @pl.when(k == 0)
        def _():
            acc[...] = jnp.zeros_like(acc)

        @pl.when(n < 3)
        def _():
            acc[...] += _dot(x_ref[...], lo_ref[...], TN)

        @pl.when(n == 3)
        def _():
            acc[...] += _dot(x_ref[...], xr_ref[...], TN)

        @pl.when(n > 3)
        def _():
            acc[...] += _dot(x_ref[...], hi_ref[...], TN)

        @pl.when(k == nk - 1)
        def _():
            o_ref[...] = acc[...].astype(BF16)

    return pl.pallas_call(
        body, name="mixb_dwin", grid=(7, nk),
        in_specs=[pl.BlockSpec((tk, D), lambda n, k: (k, 0)),
                  pl.BlockSpec((tk, D), lambda n, k: (jnp.where(n < 3, k, 0), jnp.minimum(n, 2))),
                  pl.BlockSpec((tk, D), lambda n, k: (jnp.where(n == 3, k, 0), 0)),
                  pl.BlockSpec((tk, D), lambda n, k: (jnp.where(n > 3, k, 0), jnp.clip(n - 4, 0, 2)))],
        out_specs=pl.BlockSpec((D, D), lambda n, k: (0, n)), out_shape=SDS((D, D_IN), BF16),
        scratch_shapes=[pltpu.VMEM((D, D), F32)],
        compiler_params=_cp("parallel", "arbitrary"),
    )(xb, dplo, dxr, dphi)


def mixb_drgw(xcb, dg):
    t = xcb.shape[0]
    return _mm_tn("mixb_drgw", xcb, pl.BlockSpec((t, HD), lambda g, h: (0, h)),
                  dg, pl.BlockSpec((None, t, HD), lambda g, h: (g, 0, h)),
                  SDS((2, HEADS, HD, HD), BF16), pl.BlockSpec((None, None, HD, HD), lambda g, h: (g, h, 0, 0)),
                  (2, HEADS))


def loss_head(y, tgt):
    t = y.shape[0]
    tm = _row_tile(t, 512)

    def body(y_ref, t_ref, dy_ref, l_ref):
        i = pl.program_id(0)
        e = y_ref[...] - t_ref[...]
        dy_ref[...] = e * (1.0 / D)

        @pl.when(i == 0)
        def _():
            l_ref[...] = jnp.zeros_like(l_ref)

        l_ref[...] += 0.5 * jnp.sum(jnp.mean(e * e, axis=-1, keepdims=True), axis=0, keepdims=True)

    row = pl.BlockSpec((tm, D), lambda i: (i, 0))
    return pl.pallas_call(
        body, name="loss_head", grid=(t // tm,), in_specs=[row, row],
        out_specs=(row, pl.BlockSpec((8, 128), lambda i: (0, 0))),
        out_shape=(SDS((t, D), F32), SDS((8, 128), F32)),
        compiler_params=_cp("arbitrary"),
    )(y, tgt)


def _adamw(w, g, m, v):
    m = ADAM_B1 * m + (1.0 - ADAM_B1) * g
    v = ADAM_B2 * v + (1.0 - ADAM_B2) * (g * g)
    m_hat = m / (1.0 - ADAM_B1 ** ADAM_STEP)
    v_hat = v / (1.0 - ADAM_B2 ** ADAM_STEP)
    delta = -ADAM_LR * (m_hat / (jnp.sqrt(v_hat) + ADAM_EPS) + ADAM_WD * w)
    return delta, m, v


def adam_big(name, w, m, v, parts, rows, lanes, blk_off, tr, l0, prev=None, after=()):
    nr = rows // tr
    nl = len(parts)

    def body(w_ref, m_ref, v_ref, *rest):
        g_ref, d_ref, mo_ref, vo_ref = rest[-4:]
        l = pl.program_id(0)
        for ll in range(nl):
            pr = rest[ll]

            @pl.when(l == ll)
            def _():
                g = pr[0].astype(F32)
                for s in range(1, N_DEV):
                    g = g + pr[s].astype(F32)
                g_ref[...] = g
                d, mn, vn = _adamw(w_ref[...], g, m_ref[...], v_ref[...])
                d_ref[...] = d
                mo_ref[...] = mn
                vo_ref[...] = vn

    blk = pl.BlockSpec((None, tr, lanes), lambda l, r: (l + l0, r, 0))

    def part_spec(ll):
        return pl.BlockSpec((N_DEV, tr, lanes), lambda l, r: (0, jnp.where(l == ll, r, 0) + blk_off, 0))

    out = SDS(w.shape, F32)
    extra = [] if prev is None else list(prev)
    return pl.pallas_call(
        body, name=name, grid=(nl, nr),
        in_specs=[blk, blk, blk] + [part_spec(ll) for ll in range(nl)] + [ANY] * (len(extra) + len(after)),
        out_specs=(blk,) * 4, out_shape=(out,) * 4,
        input_output_aliases={3 + nl + i: i for i in range(len(extra))},
        compiler_params=_cp("parallel", "parallel"),
    )(w, m, v, *parts, *extra, *after)


def small_update(me1, land_sh, land_rep, sharded, replicated):
    ns, nr = len(sharded), len(replicated)

    def body(me_ref, lsh, lrep, *refs):
        ins, outs = refs[:3 * (ns + nr)], refs[3 * (ns + nr):]

        def total(read):
            g = read(0)
            for s in range(1, N_DEV):
                g = g + read(s)
            return g

        def update(k, g, sl):
            w_ref, m_ref, v_ref = ins[3 * k:3 * k + 3]
            d, mn, vn = _adamw(w_ref[sl], g, m_ref[sl], v_ref[sl])
            for o, val in zip(outs[4 * k:4 * k + 4], (g, d, mn, vn)):
                o[sl] = val

        for k, (r0, n) in enumerate(GS_ROWS):
            update(k, total(lambda s: lsh[s, :, r0:r0 + n, :]), (slice(None),) * 3)
        lo, hi = (slice(None), slice(0, D)), (slice(None), slice(D, 2 * D))
        update(ns, total(lambda s: lrep[s, 0:4, :]), lo)
        update(ns, total(lambda s: lrep[s, 4:8, :]), hi)
        update(ns + 1, total(lambda s: lrep[s, 8:12, :]), (slice(None),) * 2)
        update(ns + 2, total(lambda s: lrep[s, 12:16, :]), (slice(None),) * 2)

    def whole(a):
        nd = a.ndim
        return pl.BlockSpec(a.shape, lambda i, me: (0,) * nd)

    params = [a for wmv in list(sharded) + list(replicated) for a in wmv]
    out_shape = tuple(SDS(wmv[0].shape, F32) for wmv in list(sharded) + list(replicated) for _ in range(4))
    res = pl.pallas_call(
        body, name="small_update", out_shape=out_shape,
        grid_spec=pltpu.PrefetchScalarGridSpec(
            num_scalar_prefetch=1, grid=(1,),
            in_specs=[pl.BlockSpec((N_DEV, DEPTH, 16, 128), lambda i, me: (0, 0, 0, me[0])), whole(land_rep)]
            + [whole(a) for a in params],
            out_specs=tuple(pl.BlockSpec(s.shape, lambda i, me, nd=len(s.shape): (0,) * nd) for s in out_shape)),
        compiler_params=_cp("arbitrary"),
    )(me1, land_sh, land_rep, *params)
    return [list(res[4 * k:4 * k + 4]) for k in range(ns + nr)]


def _ffn_forward(xf, xb, gw, f, s, between=None):
    g, u, a = ffn_up(xb, gw["w1"], f)
    if between is not None:
        a = between(a)
    xo, xob, xh, rs = ffn_down_ln(a, gw["w2"], f, xf, gw["small"], s)
    return (xo, xob), dict(xb=xb, g=g, u=u, a=a, xh=xh, rs=rs)


def _ffn_backward(dxn, sv, gw, f, s):
    dz, df, dgu, dln = ffn_bwd_gates(dxn, sv["xh"], sv["rs"], gw["small"], s, gw["w2"], f, sv["g"], sv["u"])
    dx = ffn_bwd_dx(dz, dgu, gw["w1"], f)
    dw1 = ffn_dw1(sv["xb"], dgu, f)
    dw2 = ffn_dw2(sv["a"], df, f)
    return dx, dw1, dw2, dln


def _mixer_forward(xf, xb, gw, rcb, ap, bm):
    p = mix_proj(xb, gw["win"])
    ca, pa, xc, xcb, gi, gr, h = mix_pre(p, gw["small"], rcb, ap, gw["rgw"])
    pb, ya, yb, m, xo, xob, xh, rs = mix_out(pa, h, p, bm, gw["w3"], xf, gw["small"])
    sv = dict(xb=xb, p=p, ca=ca, pa=pa, xc=xc, xcb=xcb, gi=gi, gr=gr, h=h, pb=pb, ya=ya, yb=yb, m=m, xh=xh, rs=rs)
    return (xo, xob), sv


def _mixer_backward(dxn, sv, gw, rcb, ap, bm):
    dz, dzb, dya, dyb, dbg, dca, dh, dphi, dln, dbm = mixb_head(
        dxn, sv["xh"], sv["rs"], gw["small"], gw["w3"], sv["p"], bm, sv["ya"], sv["yb"], sv["ca"], sv["h"])
    dg, dxc, red_rec = mixb_rec(dh, sv["gr"], sv["gi"], sv["h"], sv["xc"], ap, gw["rgw"])
    dplo, dxr, red_conv = mixb_conv(dca, dxc, dbg, sv["p"], gw["small"])
    dx = mixb_dx(dz, dplo, dxr, dphi, gw["win"])
    dwin = mixb_dwin(sv["xb"], dplo, dxr, dphi)
    dwo = mm_tn_square("mixb_dwo", sv["m"], dzb)
    dwoc = mm_tn_square("mixb_dwoc", sv["pa"], dya)
    dwor = mm_tn_square("mixb_dwor", sv["pb"], dyb)
    drgw = mixb_drgw(sv["xcb"], dg)
    return dx, dict(dwin=dwin, dwoc=dwoc, dwor=dwor, dwo=dwo, drgw=drgw), dln, dbm, red_rec, red_conv


def kernel(x, w_in, b_merge, sc_w, rc_w, rc_b, rg_w, rg_b, a_param, w_out_conv, w_out_rnn, w_o, ffn_w1, ffn_w2, ln_g, ln_b, loss_target, m_w_in, m_b_merge, m_sc_w, m_rc_w, m_rc_b, m_rg_w, m_rg_b, m_a_param, m_w_out_conv, m_w_out_rnn, m_w_o, m_ffn_w1, m_ffn_w2, m_ln_g, m_ln_b, v_w_in, v_b_merge, v_sc_w, v_rc_w, v_rc_b, v_rg_w, v_rg_b, v_a_param, v_w_out_conv, v_w_out_rnn, v_w_o, v_ffn_w1, v_ffn_w2, v_ln_g, v_ln_b):
    t = x.shape[1]
    me = _me()

    def rows(parts, total):
        out, off = None, 0
        for part in parts:
            r = part.shape[-2]
            pad = [(0, 0)] * (part.ndim - 2) + [(off, total - off - r), (0, 0)]
            padded = jnp.pad(part, pad)
            out = padded if out is None else out + padded
            off += r
        return out

    small = rows([sc_w, rc_w, rg_b, ln_g, ln_b], SMALL_ROWS)

    me1 = jnp.reshape(me, (1,)).astype(jnp.int32)

    def layer_shards(l, small_now):
        return gather_prep(l, me1, w_in, w_out_conv, w_out_rnn, w_o, ffn_w1, ffn_w2, rg_w, small_now)

    def as_weights(lands):
        return dict(zip(("win", "w3", "w1", "w2", "rgw", "small"), lands))

    xf = x.reshape(t, D)
    xb = xf.astype(BF16)

    early_at, rest_at = [2, 3, 5], [0, 1, 2, 3, 4]

    def ffn_slots(f, o_w1, o_w2, p):
        return [o_w1.at[f, p], o_w2.at[f, _slab(p, W2S, 16), :]]

    def early_slots(lands, p):
        return ffn_slots(0, lands[0], lands[1], p) + [lands[2].at[:, _slab(p, 128, 128)]]

    def rest_slots(lands, p):
        o_win, o_w3, o_w1, o_w2, o_rgw = lands
        return ([o_win.at[:, _slab(p, WIN_S, 128)], o_w3.at[:, _slab(p, 128, 128), :]] + ffn_slots(1, o_w1, o_w2, p)
                + [o_rgw.at[:, :, _slab(p, 32, 32), :]])

    early = ChipGather("gather0a", 3, 3, 3, lambda refs: [refs[0].at[0], refs[1].at[0], refs[2]], early_slots)
    rest = ChipGather("gather0b", 5, 5, 5, lambda refs: [refs[0], refs[1], refs[2].at[1], refs[3].at[1], refs[4]],
                      rest_slots)
    shards, own = layer_shards(0, small)

    def step(method, at, *args):
        res = method(*args[:-1], [shards[i] for i in at], [own[i] for i in at], args[-1])
        for i, s_new, o_new in zip(at, res[-3], res[-2]):
            shards[i], own[i] = s_new, o_new
        return tuple(res[:-3]) + (res[-1],)

    sems_a, thru = step(early.first, early_at, [xb, small])
    sems_b, thru = step(rest.first, rest_at, thru)
    relay_a, thru = step(early.relay, early_at, sems_a, thru)
    ((xb, small),) = step(early.last, early_at, sems_a, relay_a, thru)
    cur, s0 = _ffn_forward(xf, xb, as_weights(own), 0, 0)
    relay_b, thru = step(rest.relay, rest_at, sems_b, list(cur) + [small])
    (thru,) = step(rest.last, rest_at, sems_b, relay_b, thru)
    cur, small = thru[:2], thru[2]

    gathers = [ChipGather(f"gather{l}") for l in range(DEPTH)]
    gws, saved = [as_weights(own)], []
    for l in range(DEPTH):
        gw = gws[l]
        nxt = l + 1 < DEPTH
        if nxt:
            shards, own = layer_shards(l + 1, small)
            sems, shards, own, thru = gathers[l + 1].first(shards, own, list(cur) + [small])
            cur, small = thru[:2], thru[2]
        rcb, ap, bm = rc_b[l][None], a_param[l][None], b_merge[l][None]
        if l > 0:
            cur, s0 = _ffn_forward(cur[0], cur[1], gw, 0, 0)
        cur, s1 = _mixer_forward(cur[0], cur[1], gw, rcb, ap, bm)
        if nxt and l > 0:
            relay_sems, shards, own, cur = gathers[l + 1].relay(sems, shards, own, cur)
        if nxt and l == 0:
            def late_relay(a):
                nonlocal relay_sems, shards, own
                relay_sems, shards, own, (a,) = gathers[1].relay(sems, shards, own, [a])
                return a

            cur, s2 = _ffn_forward(cur[0], cur[1], gw, 1, 2, late_relay)
        else:
            cur, s2 = _ffn_forward(cur[0], cur[1], gw, 1, 2)
        saved.append((s0, s1, s2))
        if nxt:
            _, own, cur = gathers[l + 1].last(sems, relay_sems, shards, own, cur)
            gws.append(as_weights(own))

    dy, loss_tile = loss_head(cur[0], loss_target.reshape(t, D))
    loss = lax.psum(loss_tile[0, 0], ("x", "y", "c"))

    lands = [dict() for _ in range(DEPTH)]
    gsmall, grep = [None] * DEPTH, [None] * DEPTH
    flights = {}

    def launch(key, kinds, grads, dy):
        ex, prep = grad_scatter(f"scatter{key}", kinds)
        fl = ex.start(grads, prep(me1, grads), [] if dy is None else [dy])
        flights[key] = (ex, fl)
        return None if dy is None else fl[4][0]

    def landed(key, thru):
        ex, fl = flights.pop(key)
        return ex.wait(fl[0], fl[1], fl[2], fl[3], thru)

    ffn_b, mixer, ffn_a = ("w1b", "w2b"), ("w_in", "w3", "rgw"), ("w1a", "w2a")
    SPLIT = (0, 1)
    for l in reversed(range(DEPTH)):
        gw = gws[l]
        rcb, ap, bm = rc_b[l][None], a_param[l][None], b_merge[l][None]
        s0, s1, s2 = saved[l]
        dy, dw1b, dw2b, dln2 = _ffn_backward(dy, s2, gw, 1, 2)
        if l in SPLIT:
            dy = launch(f"{l}b", ["ffn"], [dw1b, dw2b], dy)
        dy, dmix, dln1, dbm, red_rec, red_conv = _mixer_backward(dy, s1, gw, rcb, ap, bm)
        mixer_grads = [dmix["dwin"], dmix["dwoc"], dmix["dwor"], dmix["dwo"], dmix["drgw"]]
        if l in SPLIT:
            dy = launch(f"{l}m", ["mixer"], mixer_grads, dy)
        dy, dw1a, dw2a, dln0 = _ffn_backward(dy, s0, gw, 0, 0)
        if l + 1 < DEPTH and l + 1 not in SPLIT:
            got, (dy,) = landed(str(l + 1), [dy])
            lands[l + 1] = dict(zip(ffn_b + mixer + ffn_a, got))
        if l == 0:
            lands[DEPTH - 1]["w_in"] = launch("0a", ["ffn"], [dw1a, dw2a], lands[DEPTH - 1]["w_in"])
        elif l in SPLIT:
            dy = launch(f"{l}a", ["ffn"], [dw1a, dw2a], dy)
        else:
            dy = launch(str(l), ["ffn", "mixer", "ffn"], [dw1b, dw2b] + mixer_grads + [dw1a, dw2a], dy)
        gsmall[l] = rows([red_conv[0:8], red_rec[0:2], dln0[0:1], dln1[0:1], dln2[0:1], dln0[1:2], dln1[1:2],
                          dln2[1:2]], 16)
        grep[l] = [dbm[0:1], dbm[1:2], red_rec[3:4], red_rec[2:3]]
    grad_x = dy.reshape(1, t, D)

    g_rep = rows([grep[l][kind] for kind in range(4) for l in range(DEPTH)], 16)
    land_sh, land_rep = gather_small_grads(jnp.stack(gsmall, axis=0), g_rep)
    small_names = ("sc_w", "rc_w", "rg_b", "ln_g", "ln_b", "b_merge", "rc_b", "a_param")
    small_res = small_update(
        me1, land_sh, land_rep,
        [(sc_w, m_sc_w, v_sc_w), (rc_w, m_rc_w, v_rc_w), (rg_b, m_rg_b, v_rg_b), (ln_g, m_ln_g, v_ln_g),
         (ln_b, m_ln_b, v_ln_b)],
        [(b_merge, m_b_merge, v_b_merge), (rc_b, m_rc_b, v_rc_b), (a_param, m_a_param, v_a_param)])

    def parts_of(layers):
        ls = [lands[l] for l in layers]
        return dict(w_in=[a["w_in"] for a in ls],
                    w3=[a["w3"].reshape(N_DEV, 3 * 128, D) for a in ls],
                    rgw=[a["rgw"].reshape(N_DEV, 2 * HEADS * 32, HD) for a in ls],
                    w1=[a[k] for a in ls for k in ("w1a", "w1b")],
                    w2=[a[k] for a in ls for k in ("w2a", "w2b")])

    families = [("w_in", w_in, m_w_in, v_w_in, "w_in", 1, D, WIN_S, 0, 128),
                ("w_out_conv", w_out_conv, m_w_out_conv, v_w_out_conv, "w3", 1, 128, D, 0, 128),
                ("w_out_rnn", w_out_rnn, m_w_out_rnn, v_w_out_rnn, "w3", 1, 128, D, 1, 128),
                ("w_o", w_o, m_w_o, v_w_o, "w3", 1, 128, D, 2, 128),
                ("ffn_w1", ffn_w1, m_ffn_w1, v_ffn_w1, "w1", 2, D, FS, 0, 256),
                ("ffn_w2", ffn_w2, m_ffn_w2, v_ffn_w2, "w2", 2, W2S, D, 0, W2S // 2),
                ("rg_w", rg_w, m_rg_w, v_rg_w, "rgw", 1, 2 * HEADS * 32, HD, 0, 256)]

    def adam_pass(tag, layers, prev):
        parts = parts_of(layers)
        outs = {}
        for name, w, m, v, fam, per, nrow, lanes, blk_off, tr in families:
            r3 = lambda a: a.reshape(DEPTH * per, nrow, lanes)
            outs[name] = adam_big(f"adam_{name}_{tag}", r3(w), r3(m), r3(v), parts[fam], nrow, lanes, blk_off, tr,
                                  per * layers[0], None if prev is None else prev[name],
                                  (dy,) if prev is None else ())
        return outs

    def through(outs, key, extra):
        thru = [outs[f[0]][0] for f in families] + list(extra)
        got, thru = landed(key, thru)
        outs = {f[0]: [thru[i]] + list(outs[f[0]][1:]) for i, f in enumerate(families)}
        return got, outs, thru[len(families):]

    done = adam_pass("top", list(range(2, DEPTH)), None)
    for l, tag in ((1, "second"), (0, "first")):
        for part, names in (("b", ffn_b), ("m", mixer), ("a", ffn_a)):
            got, done, _ = through(done, f"{l}{part}", [])
            lands[l].update(zip(names, got))
        done = adam_pass(tag, [l], done)
    res = {f[0]: [o.reshape(f[1].shape) for o in done[f[0]]] for f in families}

    res.update(zip(small_names, small_res))

    names = ["w_in", "b_merge", "sc_w", "rc_w", "rc_b", "rg_w", "rg_b", "a_param", "w_out_conv", "w_out_rnn", "w_o",
             "ffn_w1", "ffn_w2", "ln_g", "ln_b"]
    out = [loss, grad_x]
    for k in range(4):
        out += [res[n][k] for n in names]
    return tuple(out)
```

```python
import functools

import jax
import jax.numpy as jnp
from jax import lax
from jax.experimental import pallas as pl
from jax.experimental.pallas import tpu as pltpu

F32 = jnp.float32
BF16 = jnp.bfloat16
SDS = jax.ShapeDtypeStruct

N_DEV = 8
DEPTH = 4
D = 1024
D_FF = 2816
FS = D_FF // 4
W2S = D_FF // 8
D_IN = 7 * D
WIN_S = D_IN // 8
HEADS = 4
HD = D // HEADS
LRU_C = 8.0
ALPHA = (2.0 * DEPTH) ** 0.25
LN_EPS = 1e-5
ADAM_LR, ADAM_B1, ADAM_B2, ADAM_EPS, ADAM_WD, ADAM_STEP = 0.001, 0.9, 0.999, 1e-08, 0.01, 10

R_SC, R_RC, R_RGB, R_LNG, R_LNB = 0, 3, 7, 9, 12
SMALL_ROWS = 16
GS_ROWS = ((0, 3), (3, 4), (8, 2), (10, 3), (13, 3))

NN = ((1,), (0,))
NT = ((1,), (1,))
TN = ((0,), (0,))
MESH = pl.DeviceIdType.MESH
ANY = pl.BlockSpec(memory_space=pl.ANY)
VMEM_LIMIT = 52 * 1024 * 1024


def _dot(a, b, dims):
    return lax.dot_general(a, b, (dims, ((), ())), preferred_element_type=F32)


def _cp(*sem):
    return pltpu.CompilerParams(dimension_semantics=sem, vmem_limit_bytes=VMEM_LIMIT)


def _sigmoid(x):
    return 1.0 / (1.0 + jnp.exp(-x))


def _sigmoid_t(x):
    return 0.5 * jnp.tanh(0.5 * x) + 0.5


def _log1p(e):
    u = 1.0 + e
    return jnp.where(u == 1.0, e, jnp.log(u) * e / jnp.where(u == 1.0, 1.0, u - 1.0))


def _softplus(x):
    return jnp.maximum(x, 0.0) + _log1p(jnp.exp(-jnp.abs(x)))


def _neg_expm1(x):
    u = jnp.exp(x)
    um1 = u - 1.0
    safe = jnp.logical_and(u != 1.0, um1 != -1.0)
    r = um1 * x / jnp.where(safe, jnp.log(jnp.where(safe, u, 0.5)), 1.0)
    return -jnp.where(u == 1.0, x, jnp.where(um1 == -1.0, -1.0, r))


def _gelu(y):
    c = 0.7978845608028654
    t = jnp.tanh(c * (y + 0.044715 * y * y * y))
    return 0.5 * y * (1.0 + t), t


def _gelu_grad(y, t):
    c = 0.7978845608028654
    return 0.5 * (1.0 + t) + 0.5 * y * (1.0 - t * t) * c * (1.0 + 3.0 * 0.044715 * y * y)


def _ln_fwd(z, g, b):
    mu = jnp.mean(z, axis=-1, keepdims=True)
    zc = z - mu
    var = jnp.mean(zc * zc, axis=-1, keepdims=True)
    rstd = lax.rsqrt(var + LN_EPS)
    xh = zc * rstd
    return xh * g + b, xh, rstd


def _ln_bwd(dy, xh, rstd, g):
    dxh = dy * g
    m1 = jnp.mean(dxh, axis=-1, keepdims=True)
    m2 = jnp.mean(dxh * xh, axis=-1, keepdims=True)
    return rstd * (dxh - m1 - xh * m2)


def _row_tile(t, want):
    return min(want, t)


def _me():
    return 4 * lax.axis_index("x") + 2 * lax.axis_index("y") + lax.axis_index("c")


def _coords(p):
    return (p // 4, (p // 2) % 2, p % 2)


def _all_to_all_copies(srcs_of, dsts_of, waits, send_sems, recv_sems, loc_sems):
    me = _me()
    n = len(waits)
    own_src, own_dst = srcs_of(me), dsts_of(me)
    local = [pltpu.make_async_copy(own_src[k], own_dst[k], loc_sems.at[k]) for k in range(n)]
    for cp in local:
        cp.start()
    for d in range(1, N_DEV):
        p = (me + d) % N_DEV
        src, dst = srcs_of(p), dsts_of(me)
        for k in range(n):
            pltpu.make_async_remote_copy(
                src_ref=src[k], dst_ref=dst[k], send_sem=send_sems.at[waits[k][1]],
                recv_sem=recv_sems.at[waits[k][1]], device_id=_coords(p), device_id_type=MESH).start()
    done = set()
    for k in range(n):
        ref, s = waits[k]
        if s in done:
            continue
        done.add(s)
        pltpu.make_async_remote_copy(
            src_ref=ref, dst_ref=ref, send_sem=send_sems.at[s], recv_sem=recv_sems.at[s],
            device_id=_coords(me), device_id_type=MESH).wait()
    for cp in local:
        cp.wait()


HBM = pl.BlockSpec(memory_space=pltpu.HBM)
SEM = pl.BlockSpec(memory_space=pltpu.SEMAPHORE)
EFFECT = pltpu.SideEffectType.DATAFLOW_SIDE_EFFECTING


def _in_hbm(a):
    return pltpu.with_memory_space_constraint(a, pltpu.HBM)


class Exchange:
    def __init__(self, name, src_of, dst_of, sem_of, span_of):
        self.name, self.src_of, self.dst_of, self.sem_of, self.span_of = name, src_of, dst_of, sem_of, span_of
        self.nsem = max(sem_of) + 1

    def start(self, srcs, lands, thru):
        n, m = len(srcs), len(lands)
        ops = list(srcs) + list(lands) + list(thru)

        def body(*refs):
            src_refs, land_refs = refs[:n], refs[n:n + m]
            send_sems, recv_sems = refs[len(ops)], refs[len(ops) + 1]
            me = _me()
            for dd in range(1, N_DEV):
                p = (me + dd) % N_DEV
                s, d = self.src_of(src_refs, p), self.dst_of(land_refs, me)
                for k in range(len(self.sem_of)):
                    pltpu.make_async_remote_copy(
                        src_ref=s[k], dst_ref=d[k], send_sem=send_sems.at[self.sem_of[k]],
                        recv_sem=recv_sems.at[self.sem_of[k]], device_id=_coords(p), device_id_type=MESH).start()

        sem = pltpu.SemaphoreType.DMA((self.nsem,))
        res = pl.pallas_call(
            body, name=self.name + "_start",
            out_shape=(sem, sem) + tuple(pltpu.HBM(a.shape, a.dtype) for a in ops),
            in_specs=[HBM] * len(ops), out_specs=(SEM, SEM) + (HBM,) * len(ops),
            input_output_aliases={i: 2 + i for i in range(len(ops))},
            compiler_params=pltpu.CompilerParams(has_side_effects=EFFECT),
        )(*[_in_hbm(a) for a in ops])
        return res[0], res[1], res[2:2 + n], res[2 + n:2 + n + m], list(res[2 + n + m:])

    def wait(self, send_sems, recv_sems, srcs, lands, thru):
        n, m = len(srcs), len(lands)
        ops = list(srcs) + list(lands) + list(thru)

        def body(*refs):
            land_refs = refs[n:n + m]
            ssem, rsem = refs[len(ops)], refs[len(ops) + 1]
            me = _me()
            spans = self.span_of(land_refs)
            for s in range(self.nsem):
                cp = pltpu.make_async_remote_copy(
                    src_ref=spans[s], dst_ref=spans[s], send_sem=ssem.at[s], recv_sem=rsem.at[s],
                    device_id=_coords(me), device_id_type=MESH)
                cp.wait_send()
                cp.wait_recv()

        res = pl.pallas_call(
            body, name=self.name + "_wait",
            out_shape=tuple(pltpu.HBM(a.shape, a.dtype) for a in ops),
            in_specs=[HBM] * len(ops) + [SEM, SEM], out_specs=(HBM,) * len(ops),
            input_output_aliases={i: i for i in range(len(ops))},
            compiler_params=pltpu.CompilerParams(has_side_effects=EFFECT),
        )(*ops, send_sems, recv_sems)
        return res[n:n + m], list(res[n + m:])


class ChipGather:
    def __init__(self, name, nsrc=6, nland=6, ncopy=6, views=None, slots=None):
        self.name, self.nsrc, self.nland, self.ncopy = name, nsrc, nland, ncopy
        self.views = views if views is not None else list
        self.slots = slots if slots is not None else self.layer_slots

    @staticmethod
    def layer_slots(lands, p):
        o_win, o_w3, o_w1, o_w2, o_rgw, o_sm = lands
        return [o_win.at[:, pl.ds(pl.multiple_of(p * WIN_S, 128), WIN_S)],
                o_w3.at[:, pl.ds(pl.multiple_of(p * 128, 128), 128), :],
                o_w1.at[:, p],
                o_w2.at[:, pl.ds(pl.multiple_of(p * W2S, 16), W2S), :],
                o_rgw.at[:, :, pl.ds(pl.multiple_of(p * 32, 32), 32), :],
                o_sm.at[:, pl.ds(pl.multiple_of(p * 128, 128), 128)]]

    @staticmethod
    def _places():
        x, y, c = lax.axis_index("x"), lax.axis_index("y"), lax.axis_index("c")
        chips = [(1 - x, y), (x, 1 - y), (1 - x, 1 - y)]
        return (x, y, c), (x, y, 1 - c), chips

    @staticmethod
    def _index(place):
        return 4 * place[0] + 2 * place[1] + place[2]

    def _call(self, tag, body, ops, sems_in, sems_out):
        nops = len(ops)
        sem = pltpu.SemaphoreType.DMA((max(sems_out, 1),))
        res = pl.pallas_call(
            body, name=f"{self.name}_{tag}",
            out_shape=((sem, sem) if sems_out else ()) + tuple(pltpu.HBM(a.shape, a.dtype) for a in ops),
            in_specs=[HBM] * nops + [SEM] * len(sems_in),
            out_specs=((SEM, SEM) if sems_out else ()) + (HBM,) * nops,
            input_output_aliases={i: (2 if sems_out else 0) + i for i in range(nops)},
            compiler_params=pltpu.CompilerParams(has_side_effects=EFFECT),
        )(*[_in_hbm(a) for a in ops], *sems_in)
        return res

    def first(self, srcs, lands, thru):
        ns, nl, n = self.nsrc, self.nland, self.ncopy
        ops = list(srcs) + list(lands) + list(thru)

        def body(*refs):
            src_refs, land_refs = self.views(refs[:ns]), refs[ns:ns + nl]
            send_sems, recv_sems = refs[len(ops)], refs[len(ops) + 1]
            me, sibling, chips = self._places()
            dst = self.slots(land_refs, self._index(me))
            targets = [sibling] + [(cx, cy, me[2]) for cx, cy in chips]
            for r, to in enumerate(targets):
                for k in range(n):
                    pltpu.make_async_remote_copy(
                        src_ref=src_refs[k], dst_ref=dst[k], send_sem=send_sems.at[n *r + k],
                        recv_sem=recv_sems.at[n *r + k], device_id=to, device_id_type=MESH).start()

        res = self._call("first", body, ops, [], 4 * n)
        return (res[0], res[1]), list(res[2:2 + ns]), list(res[2 + ns:2 + ns + nl]), list(res[2 + ns + nl:])

    def relay(self, sems, srcs, lands, thru):
        ns, nl, n = self.nsrc, self.nland, self.ncopy
        ops = list(srcs) + list(lands) + list(thru)

        def wait_body(*refs):
            land_refs = refs[ns:ns + nl]
            ssem, rsem = refs[len(ops)], refs[len(ops) + 1]
            me, sibling, chips = self._places()
            for j, (cx, cy) in enumerate(chips):
                got = self.slots(land_refs, self._index((cx, cy, me[2])))
                for k in range(n):
                    pltpu.make_async_remote_copy(
                        src_ref=got[k], dst_ref=got[k], send_sem=ssem.at[n *(1 + j) + k], recv_sem=rsem.at[n *(1 + j) + k],
                        device_id=me, device_id_type=MESH).wait_recv()

        ops = list(self._call("landed", wait_body, ops, list(sems), 0))

        def start_body(*refs):
            land_refs = refs[ns:ns + nl]
            ssem, rsem = refs[len(ops)], refs[len(ops) + 1]
            me, sibling, chips = self._places()
            for j, (cx, cy) in enumerate(chips):
                got = self.slots(land_refs, self._index((cx, cy, me[2])))
                for k in range(n):
                    pltpu.make_async_remote_copy(
                        src_ref=got[k], dst_ref=got[k], send_sem=ssem.at[n *j + k], recv_sem=rsem.at[n *j + k],
                        device_id=sibling, device_id_type=MESH).start()

        res = self._call("relay", start_body, ops, [], 3 * n)
        return (res[0], res[1]), list(res[2:2 + ns]), list(res[2 + ns:2 + ns + nl]), list(res[2 + ns + nl:])

    def last(self, sems, relay_sems, srcs, lands, thru):
        ns, nl, n = self.nsrc, self.nland, self.ncopy
        ops = list(srcs) + list(lands) + list(thru)

        def body(*refs):
            src_refs, land_refs = self.views(refs[:ns]), refs[ns:ns + nl]
            ssem, rsem, ssem2, rsem2 = refs[len(ops):len(ops) + 4]
            me, sibling, chips = self._places()
            got = self.slots(land_refs, self._index(sibling))
            for k in range(n):
                pltpu.make_async_remote_copy(
                    src_ref=got[k], dst_ref=got[k], send_sem=ssem.at[k], recv_sem=rsem.at[k],
                    device_id=me, device_id_type=MESH).wait_recv()
            for j, (cx, cy) in enumerate(chips):
                got = self.slots(land_refs, self._index((cx, cy, 1 - me[2])))
                for k in range(n):
                    pltpu.make_async_remote_copy(
                        src_ref=got[k], dst_ref=got[k], send_sem=ssem2.at[n *j + k], recv_sem=rsem2.at[n *j + k],
                        device_id=me, device_id_type=MESH).wait_recv()
            for sem_s, sem_r, count in ((ssem, rsem, 4), (ssem2, rsem2, 3)):
                for r in range(count):
                    for k in range(n):
                        pltpu.make_async_remote_copy(
                            src_ref=src_refs[k], dst_ref=src_refs[k], send_sem=sem_s.at[n *r + k],
                            recv_sem=sem_r.at[n *r + k], device_id=me, device_id_type=MESH).wait_send()

        res = self._call("last", body, ops, list(sems) + list(relay_sems), 0)
        return list(res[:ns]), list(res[ns:ns + nl]), list(res[ns + nl:])


GATHER_SHAPES = (SDS((D, D_IN), BF16), SDS((3, D, D), BF16), SDS((2, N_DEV, D, FS), BF16),
                 SDS((2, D_FF, D), BF16), SDS((2, HEADS, HD, HD), BF16), SDS((SMALL_ROWS, D), F32))


def _slab(p, n, align):
    return pl.ds(pl.multiple_of(p * n, align), n)


class GradGroup:
    def __init__(self, nsrc, lands, src_of, dst_of, sem_of, in_specs, out_specs, copy):
        self.nsrc, self.lands, self.src_of, self.dst_of, self.sem_of = nsrc, lands, src_of, dst_of, sem_of
        self.in_specs, self.out_specs, self.copy = in_specs, out_specs, copy


def _mixer_group():
    hd = D // 2

    def src_of(refs, p):
        dwin, dwoc, dwor, dwo, drgw = refs
        r3 = _slab(p, 128, 128)
        return [dwin.at[:, _slab(p, WIN_S, 128)], dwoc.at[r3, :], dwor.at[r3, :], dwo.at[r3, :],
                drgw.at[:, :, _slab(p, 32, 32), :]]

    def dst_of(lands, p):
        l_win, l_w3, l_rgw = lands
        return [l_win.at[p], l_w3.at[p, 0], l_w3.at[p, 1], l_w3.at[p, 2], l_rgw.at[p]]

    def copy(srcs, lands):
        lands[0][...] = srcs[0][...]
        for k in range(3):
            lands[1][k] = srcs[1 + k][...]
        lands[2][...] = srcs[4][...]

    three = pl.BlockSpec((64, D), lambda i, me: (2 * me[0] + i, 0))
    return GradGroup(
        5, (SDS((N_DEV, D, WIN_S), BF16), SDS((N_DEV, 3, 128, D), BF16), SDS((N_DEV, 2, HEADS, 32, HD), BF16)),
        src_of, dst_of, [0, 1, 1, 1, 2],
        [pl.BlockSpec((hd, WIN_S), lambda i, me: (i, me[0])), three, three, three,
         pl.BlockSpec((2, HEADS, 16, HD), lambda i, me: (0, 0, 2 * me[0] + i, 0))],
        [pl.BlockSpec((None, hd, WIN_S), lambda i, me: (me[0], i, 0)),
         pl.BlockSpec((None, 3, 64, D), lambda i, me: (me[0], 0, i, 0)),
         pl.BlockSpec((None, 2, HEADS, 16, HD), lambda i, me: (me[0], 0, 0, i, 0))],
        copy)


def _ffn_group():
    hd, hw = D // 2, W2S // 2

    def src_of(refs, p):
        return [refs[0].at[p], refs[1].at[_slab(p, W2S, 16), :]]

    def dst_of(lands, p):
        return [lands[0].at[p], lands[1].at[p]]

    def copy(srcs, lands):
        lands[0][...] = srcs[0][...]
        lands[1][...] = srcs[1][...]

    return GradGroup(
        2, (SDS((N_DEV, D, FS), BF16), SDS((N_DEV, W2S, D), BF16)), src_of, dst_of, [0, 1],
        [pl.BlockSpec((None, hd, FS), lambda i, me: (me[0], i, 0)), pl.BlockSpec((hw, D), lambda i, me: (2 * me[0] + i, 0))],
        [pl.BlockSpec((None, hd, FS), lambda i, me: (me[0], i, 0)), pl.BlockSpec((None, hw, D), lambda i, me: (me[0], i, 0))],
        copy)


def grad_scatter(name, kinds):
    groups = [_mixer_group() if k == "mixer" else _ffn_group() for k in kinds]

    def per_group(refs, counts, fn):
        out, i = [], 0
        for g, c in zip(groups, counts):
            out += fn(g, refs[i:i + c])
            i += c
        return out

    nsrcs = [g.nsrc for g in groups]
    nlands = [len(g.lands) for g in groups]
    sem_of, off = [], 0
    for g in groups:
        sem_of += [off + s for s in g.sem_of]
        off += len(g.lands)

    ex = Exchange(name,
                  lambda refs, p: per_group(refs, nsrcs, lambda g, r: g.src_of(r, p)),
                  lambda lands, p: per_group(lands, nlands, lambda g, r: g.dst_of(r, p)),
                  sem_of, lambda lands: [a.at[pl.ds(0, 7)] for a in lands])

    def prep(me1, grads):
        nsrc = sum(nsrcs)

        def body(me_ref, *refs):
            srcs, lands = refs[:nsrc], refs[nsrc:]
            i = j = 0
            for g in groups:
                g.copy(srcs[i:i + g.nsrc], lands[j:j + len(g.lands)])
                i += g.nsrc
                j += len(g.lands)

        return list(pl.pallas_call(
            body, name=name + "_prep", out_shape=tuple(s for g in groups for s in g.lands),
            grid_spec=pltpu.PrefetchScalarGridSpec(
                num_scalar_prefetch=1, grid=(2,), in_specs=[s for g in groups for s in g.in_specs],
                out_specs=tuple(s for g in groups for s in g.out_specs)),
            compiler_params=_cp("arbitrary"),
        )(me1, *grads))

    return ex, prep


def gather_prep(l, me1, w_in, w_out_conv, w_out_rnn, w_o, ffn_w1, ffn_w2, rg_w, small):
    hd, hw = D // 2, W2S // 2

    def body(me_ref, win, woc, wor, wo, w1, w2, rgw, sm, c_win, c_w3, c_w1, c_w2, c_rgw, c_sm,
             o_win, o_w3, o_w1, o_w2, o_rgw, o_sm):
        a = win[...].astype(BF16)
        c_win[...] = a
        o_win[...] = a
        for k, r in enumerate((woc, wor, wo)):
            b = r[...].astype(BF16)
            c_w3[k] = b
            o_w3[k] = b
        for src, comp, own in ((w1, c_w1, o_w1), (w2, c_w2, o_w2), (rgw, c_rgw, o_rgw)):
            b = src[...].astype(BF16)
            comp[...] = b
            own[...] = b
        c_sm[...] = sm[...]
        o_sm[...] = sm[...]

    three = pl.BlockSpec((None, 64, D), lambda i, me: (l, i, 0))
    in_specs = [pl.BlockSpec((None, hd, WIN_S), lambda i, me: (l, i, 0)), three, three, three,
                pl.BlockSpec((None, 2, hd, FS), lambda i, me: (l, 0, i, 0)),
                pl.BlockSpec((None, 2, hw, D), lambda i, me: (l, 0, i, 0)),
                pl.BlockSpec((None, 2, HEADS, 16, HD), lambda i, me: (l, 0, 0, i, 0)),
                pl.BlockSpec((None, 8, 128), lambda i, me: (l, i, 0))]
    out_specs = (pl.BlockSpec((hd, WIN_S), lambda i, me: (i, 0)), pl.BlockSpec((3, 64, D), lambda i, me: (0, i, 0)),
                 pl.BlockSpec((2, hd, FS), lambda i, me: (0, i, 0)), pl.BlockSpec((2, hw, D), lambda i, me: (0, i, 0)),
                 pl.BlockSpec((2, HEADS, 16, HD), lambda i, me: (0, 0, i, 0)), pl.BlockSpec((8, 128), lambda i, me: (i, 0)),
                 pl.BlockSpec((hd, WIN_S), lambda i, me: (i, me[0])),
                 pl.BlockSpec((3, 64, D), lambda i, me: (0, 2 * me[0] + i, 0)),
                 pl.BlockSpec((2, None, hd, FS), lambda i, me: (0, me[0], i, 0)),
                 pl.BlockSpec((2, hw, D), lambda i, me: (0, 2 * me[0] + i, 0)),
                 pl.BlockSpec((2, HEADS, 16, HD), lambda i, me: (0, 0, 2 * me[0] + i, 0)),
                 pl.BlockSpec((8, 128), lambda i, me: (i, me[0])))
    compact = (SDS((D, WIN_S), BF16), SDS((3, 128, D), BF16), SDS((2, D, FS), BF16), SDS((2, W2S, D), BF16),
               SDS((2, HEADS, 32, HD), BF16), SDS((SMALL_ROWS, 128), F32))
    res = pl.pallas_call(
        body, name=f"gather_prep{l}", out_shape=compact + GATHER_SHAPES,
        grid_spec=pltpu.PrefetchScalarGridSpec(num_scalar_prefetch=1, grid=(2,), in_specs=in_specs, out_specs=out_specs),
        compiler_params=_cp("arbitrary"),
    )(me1, w_in, w_out_conv, w_out_rnn, w_o, ffn_w1, ffn_w2, rg_w, small)
    return list(res[:6]), list(res[6:])


def gather_small_grads(g_sharded, g_replicated):
    def body(ga, gb, la, lb, send_sems, recv_sems, loc_sems):
        _all_to_all_copies(lambda p: [ga, gb], lambda p: [la.at[p], lb.at[p]],
                           [(la.at[pl.ds(0, 7)], 0), (lb.at[pl.ds(0, 7)], 1)], send_sems, recv_sems, loc_sems)

    return pl.pallas_call(
        body, name="gather_small_grads",
        out_shape=(SDS((N_DEV,) + g_sharded.shape, F32), SDS((N_DEV,) + g_replicated.shape, F32)),
        in_specs=[ANY, ANY], out_specs=(ANY, ANY),
        scratch_shapes=[pltpu.SemaphoreType.DMA((2,)), pltpu.SemaphoreType.DMA((2,)), pltpu.SemaphoreType.DMA((2,))],
        compiler_params=pltpu.CompilerParams(has_side_effects=True),
    )(g_sharded, g_replicated)


def ffn_up(xb, w1, f):
    t = xb.shape[0]
    tm = _row_tile(t, 2048)

    def body(x_ref, wg_ref, wu_ref, g_ref, u_ref, a_ref):
        x = x_ref[...]
        g = _dot(x, wg_ref[...], NN)
        u = _dot(x, wu_ref[...], NN)
        g_ref[...] = g.astype(BF16)
        u_ref[...] = u.astype(BF16)
        a_ref[...] = (g * _sigmoid_t(g) * u).astype(BF16)

    out = pl.BlockSpec((None, tm, FS), lambda j, i: (j, i, 0))
    return pl.pallas_call(
        body, name=f"ffn_up{f}", grid=(4, t // tm),
        in_specs=[pl.BlockSpec((tm, D), lambda j, i: (i, 0)),
                  pl.BlockSpec((None, None, D, FS), lambda j, i: (f, j, 0, 0)),
                  pl.BlockSpec((None, None, D, FS), lambda j, i: (f, j + 4, 0, 0))],
        out_specs=(out, out, out), out_shape=(SDS((4, t, FS), BF16),) * 3,
        compiler_params=_cp("parallel", "parallel"),
    )(xb, w1, w1)


def ffn_down_ln(a, w2, f, xf, small, s):
    t = xf.shape[0]
    tm = _row_tile(t, 512)

    def body(a_ref, w_ref, x_ref, sm_ref, xo_ref, xb_ref, xh_ref, rs_ref):
        acc = _dot(a_ref[0], w_ref[0:FS, :], NN)
        for j in range(1, 4):
            acc = acc + _dot(a_ref[j], w_ref[j * FS:(j + 1) * FS, :], NN)
        z = ALPHA * x_ref[...] + 0.5 * acc
        y, xh, rstd = _ln_fwd(z, sm_ref[R_LNG + s:R_LNG + s + 1, :], sm_ref[R_LNB + s:R_LNB + s + 1, :])
        xo_ref[...] = y
        xb_ref[...] = y.astype(BF16)
        xh_ref[...] = xh
        rs_ref[...] = rstd

    row = pl.BlockSpec((tm, D), lambda i: (i, 0))
    return pl.pallas_call(
        body, name=f"ffn_down_ln{f}", grid=(t // tm,),
        in_specs=[pl.BlockSpec((4, tm, FS), lambda i: (0, i, 0)),
                  pl.BlockSpec((None, D_FF, D), lambda i: (f, 0, 0)),
                  row, pl.BlockSpec((SMALL_ROWS, D), lambda i: (0, 0))],
        out_specs=(row, row, row, pl.BlockSpec((tm, 1), lambda i: (i, 0))),
        out_shape=(SDS((t, D), F32), SDS((t, D), BF16), SDS((t, D), F32), SDS((t, 1), F32)),
        compiler_params=_cp("parallel"),
    )(a, w2, xf, small)


def ffn_bwd_gates(dxn, xh, rstd, small, s, w2, f, g, u):
    t = dxn.shape[0]
    tm = _row_tile(t, 256)

    def body(dy_ref, xh_ref, rs_ref, sm_ref, w_ref, g_ref, u_ref, dz_ref, df_ref, dgu_ref, dln_ref):
        i = pl.program_id(0)
        dy = dy_ref[...]
        xhat = xh_ref[...]
        dz = _ln_bwd(dy, xhat, rs_ref[...], sm_ref[R_LNG + s:R_LNG + s + 1, :])

        @pl.when(i == 0)
        def _():
            dln_ref[...] = jnp.zeros_like(dln_ref)

        dln_ref[0:1, :] += jnp.sum(dy * xhat, axis=0, keepdims=True)
        dln_ref[1:2, :] += jnp.sum(dy, axis=0, keepdims=True)
        dz_ref[...] = dz
        df = (0.5 * dz).astype(BF16)
        df_ref[...] = df
        for j in range(4):
            da = _dot(df, w_ref[j * FS:(j + 1) * FS, :], NT)
            gg = g_ref[j].astype(F32)
            uu = u_ref[j].astype(F32)
            sg = _sigmoid_t(gg)
            dgu_ref[j] = (da * uu * (sg * (1.0 + gg * (1.0 - sg)))).astype(BF16)
            dgu_ref[j + 4] = (da * (gg * sg)).astype(BF16)

    row = pl.BlockSpec((tm, D), lambda i: (i, 0))
    gu = pl.BlockSpec((4, tm, FS), lambda i: (0, i, 0))
    return pl.pallas_call(
        body, name=f"ffn_bwd_gates{f}", grid=(t // tm,),
        in_specs=[row, row, pl.BlockSpec((tm, 1), lambda i: (i, 0)), pl.BlockSpec((SMALL_ROWS, D), lambda i: (0, 0)),
                  pl.BlockSpec((None, D_FF, D), lambda i: (f, 0, 0)), gu, gu],
        out_specs=(row, row, pl.BlockSpec((8, tm, FS), lambda i: (0, i, 0)), pl.BlockSpec((2, D), lambda i: (0, 0))),
        out_shape=(SDS((t, D), F32), SDS((t, D), BF16), SDS((8, t, FS), BF16), SDS((2, D), F32)),
        compiler_params=_cp("arbitrary"),
    )(dxn, xh, rstd, small, w2, g, u)


def ffn_bwd_dx(dz, dgu, w1, f):
    t = dz.shape[0]
    tm = _row_tile(t, 1024)

    def body(dz_ref, d_ref, w_ref, dx_ref, acc):
        k = pl.program_id(1)

        @pl.when(k == 0)
        def _():
            acc[...] = ALPHA * dz_ref[...]

        acc[...] += _dot(d_ref[...], w_ref[...], NT)

        @pl.when(k == 7)
        def _():
            dx_ref[...] = acc[...]

    row = pl.BlockSpec((tm, D), lambda i, k: (i, 0))
    return pl.pallas_call(
        body, name=f"ffn_bwd_dx{f}", grid=(t // tm, 8),
        in_specs=[row, pl.BlockSpec((None, tm, FS), lambda i, k: (k, i, 0)),
                  pl.BlockSpec((None, None, D, FS), lambda i, k: (f, k, 0, 0))],
        out_specs=row, out_shape=SDS((t, D), F32),
        scratch_shapes=[pltpu.VMEM((tm, D), F32)],
        compiler_params=_cp("parallel", "arbitrary"),
    )(dz, dgu, w1)


def _mm_tn(name, a, a_spec, b, b_spec, out_sds, out_spec, grid):
    def body(a_ref, b_ref, o_ref):
        o_ref[...] = _dot(a_ref[...], b_ref[...], TN).astype(o_ref.dtype)

    return pl.pallas_call(
        body, name=name, grid=grid, in_specs=[a_spec, b_spec], out_specs=out_spec, out_shape=out_sds,
        compiler_params=_cp(*(["parallel"] * len(grid))),
    )(a, b)


def ffn_dw1(xb, dgu, f):
    t = xb.shape[0]
    return _mm_tn(f"ffn_dw1_{f}", xb, pl.BlockSpec((t, D), lambda j: (0, 0)),
                  dgu, pl.BlockSpec((None, t, FS), lambda j: (j, 0, 0)),
                  SDS((8, D, FS), BF16), pl.BlockSpec((None, D, FS), lambda j: (j, 0, 0)), (8,))


def ffn_dw2(a, df, f):
    t = df.shape[0]
    return _mm_tn(f"ffn_dw2_{f}", a, pl.BlockSpec((None, t, FS), lambda j: (j, 0, 0)),
                  df, pl.BlockSpec((t, D), lambda j: (0, 0)),
                  SDS((D_FF, D), BF16), pl.BlockSpec((FS, D), lambda j: (j, 0)), (4,))


def mm_tn_square(name, a, b):
    t = a.shape[0]
    return _mm_tn(name, a, pl.BlockSpec((t, 512), lambda i: (0, i)),
                  b, pl.BlockSpec((t, D), lambda i: (0, 0)),
                  SDS((D, D), BF16), pl.BlockSpec((512, D), lambda i: (i, 0)), (2,))


def mix_proj(xb, win):
    t = xb.shape[0]
    tm = _row_tile(t, 2048)

    def body(x_ref, w_ref, o_ref):
        o_ref[...] = _dot(x_ref[...], w_ref[...], NN).astype(BF16)

    return pl.pallas_call(
        body, name="mix_proj", grid=(7, t // tm),
        in_specs=[pl.BlockSpec((tm, D), lambda n, i: (i, 0)), pl.BlockSpec((D, D), lambda n, i: (0, n))],
        out_specs=pl.BlockSpec((tm, D), lambda n, i: (i, n)), out_shape=SDS((t, D_IN), BF16),
        compiler_params=_cp("parallel", "parallel"),
    )(xb, win)


def _pcol(tm, k):
    return pl.BlockSpec((tm, D), lambda i: (i, k))


def _prev_halo(tm, k):
    return pl.BlockSpec((8, D), lambda i: (jnp.maximum(i * (tm // 8) - 1, 0), k))


def _prev_halo16(tm, k):
    return pl.BlockSpec((16, D), lambda i: (jnp.maximum(i * (tm // 16) - 1, 0), k))


def _next_halo(tm, t, k):
    return pl.BlockSpec((8, D), lambda i: (jnp.minimum((i + 1) * (tm // 8), t // 8 - 1), k))


def _full(shape):
    nd = len(shape)
    return pl.BlockSpec(shape, lambda i: (0,) * nd)


def mix_pre(p, small, rcb, ap, rgw):
    t = p.shape[0]
    tm = _row_tile(t, 256)

    def body(bg_ref, cg_ref, v_ref, xr_ref, cgh_ref, vh_ref, xrh_ref, sm_ref, rcb_ref, ap_ref, rgw_ref,
             ca_ref, pa_ref, xc_ref, xcb_ref, gi_ref, gr_ref, h_ref, ext1, ext2, a_s, b_s, carry):
        i = pl.program_id(0)
        first = i == 0
        cv = cg_ref[...].astype(F32) * v_ref[...].astype(F32)
        ext1[0:16, :] = jnp.where(first, 0.0, cgh_ref[...].astype(F32) * vh_ref[...].astype(F32))
        ext1[16:, :] = cv
        xr = xr_ref[...].astype(F32)
        ext2[0:16, :] = jnp.where(first, 0.0, xrh_ref[...].astype(F32))
        ext2[16:, :] = xr
        ca = (sm_ref[R_SC:R_SC + 1, :] * ext1[pl.ds(14, tm), :] + sm_ref[R_SC + 1:R_SC + 2, :] * ext1[pl.ds(15, tm), :]
              + sm_ref[R_SC + 2:R_SC + 3, :] * cv)
        ca_ref[...] = ca.astype(BF16)
        pa_ref[...] = (bg_ref[...].astype(F32) * ca).astype(BF16)
        xc = (sm_ref[R_RC:R_RC + 1, :] * ext2[pl.ds(13, tm), :] + sm_ref[R_RC + 1:R_RC + 2, :] * ext2[pl.ds(14, tm), :]
              + sm_ref[R_RC + 2:R_RC + 3, :] * ext2[pl.ds(15, tm), :] + sm_ref[R_RC + 3:R_RC + 4, :] * xr
              + rcb_ref[...])
        xc_ref[...] = xc
        xcb = xc.astype(BF16)
        xcb_ref[...] = xcb
        g0, g1 = [], []
        for h in range(HEADS):
            xh = xcb[:, h * HD:(h + 1) * HD]
            g0.append(_dot(xh, rgw_ref[0, h], NN))
            g1.append(_dot(xh, rgw_ref[1, h], NN))
        gi = _sigmoid(jnp.concatenate(g0, axis=1) + sm_ref[R_RGB:R_RGB + 1, :])
        gr = _sigmoid(jnp.concatenate(g1, axis=1) + sm_ref[R_RGB + 1:R_RGB + 2, :])
        gi_ref[...] = gi
        gr_ref[...] = gr
        la = (-LRU_C) * gr * _softplus(-ap_ref[...])
        a_s[...] = jnp.exp(la)
        row = lax.broadcasted_iota(jnp.int32, (tm, D), 0) + i * tm
        mult = jnp.where(row == 0, 1.0, jnp.sqrt(_neg_expm1(2.0 * la)))
        b_s[...] = xc * gi * mult

        @pl.when(first)
        def _():
            carry[...] = jnp.zeros_like(carry)

        carry[...] = _scan_tile(a_s, b_s, a_s, carry[...], tm, reverse=False)
        h_ref[...] = a_s[...].astype(BF16)

    row = pl.BlockSpec((tm, D), lambda i: (i, 0))
    f32o, b16o = SDS((t, D), F32), SDS((t, D), BF16)
    ext, tile = pltpu.VMEM((tm + 16, D), F32), pltpu.VMEM((tm, D), F32)
    return pl.pallas_call(
        body, name="mix_pre", grid=(t // tm,),
        in_specs=[_pcol(tm, 0), _pcol(tm, 1), _pcol(tm, 2), _pcol(tm, 3),
                  _prev_halo16(tm, 1), _prev_halo16(tm, 2), _prev_halo16(tm, 3),
                  _full((SMALL_ROWS, D)), _full((1, D)), _full((1, D)), _full((2, HEADS, HD, HD))],
        out_specs=(row,) * 7, out_shape=(b16o, b16o, f32o, b16o, f32o, f32o, b16o),
        scratch_shapes=[ext, ext, tile, tile, pltpu.VMEM((8, D), F32)],
        compiler_params=_cp("arbitrary"),
    )(p, p, p, p, p, p, p, small, rcb, ap, rgw)


def _scan_tile(a_ref, b_ref, o_ref, carry, tm, reverse):
    width = a_ref.shape[1]
    ng = tm // 8
    row8 = lax.broadcasted_iota(jnp.int32, (8, width), 0)

    def step(g, c):
        r = pl.multiple_of((ng - 1 - g if reverse else g) * 8, 8)
        aa = a_ref[pl.ds(r, 8), :]
        bb = b_ref[pl.ds(r, 8), :]
        for s in (1, 2, 4):
            if reverse:
                keep, shift = row8 < 8 - s, 8 - s
            else:
                keep, shift = row8 >= s, s
            a_sh = jnp.where(keep, pltpu.roll(aa, shift, 0), 1.0)
            b_sh = jnp.where(keep, pltpu.roll(bb, shift, 0), 0.0)
            bb = aa * b_sh + bb
            aa = aa * a_sh
        o = aa * c + bb
        o_ref[pl.ds(r, 8), :] = o
        edge = o[0:1, :] if reverse else o[7:8, :]
        return jnp.broadcast_to(edge, (8, width))

    return lax.fori_loop(0, ng, step, carry)


def mix_out(pa, h, p, bm, w3, xf, small):
    t = xf.shape[0]
    tm = _row_tile(t, 256)

    def body(pa_ref, h_ref, yr_ref, gla_ref, glb_ref, bma_ref, bmb_ref, w_ref, x_ref, sm_ref,
             pb_ref, ya_ref, yb_ref, m_ref, xo_ref, xb_ref, xh_ref, rs_ref):
        ge, _ = _gelu(yr_ref[...].astype(F32))
        pb = (h_ref[...].astype(F32) * ge).astype(BF16)
        pb_ref[...] = pb
        ya = _dot(pa_ref[...], w_ref[0], NN)
        yb = _dot(pb, w_ref[1], NN)
        ya_ref[...] = ya.astype(BF16)
        yb_ref[...] = yb.astype(BF16)
        ga = _sigmoid_t(gla_ref[...].astype(F32) + bma_ref[...])
        gb = _sigmoid_t(glb_ref[...].astype(F32) + bmb_ref[...])
        m = (ga * ya + gb * yb).astype(BF16)
        m_ref[...] = m
        z = ALPHA * x_ref[...] + _dot(m, w_ref[2], NN)
        y, xh, rstd = _ln_fwd(z, sm_ref[R_LNG + 1:R_LNG + 2, :], sm_ref[R_LNB + 1:R_LNB + 2, :])
        xo_ref[...] = y
        xb_ref[...] = y.astype(BF16)
        xh_ref[...] = xh
        rs_ref[...] = rstd

    row = pl.BlockSpec((tm, D), lambda i: (i, 0))
    f32o, b16o = SDS((t, D), F32), SDS((t, D), BF16)
    return pl.pallas_call(
        body, name="mix_out", grid=(t // tm,),
        in_specs=[row, row, _pcol(tm, 4), _pcol(tm, 5), _pcol(tm, 6),
                  pl.BlockSpec((1, D), lambda i: (0, 0)), pl.BlockSpec((1, D), lambda i: (0, 1)),
                  _full((3, D, D)), row, _full((SMALL_ROWS, D))],
        out_specs=(row,) * 7 + (pl.BlockSpec((tm, 1), lambda i: (i, 0)),),
        out_shape=(b16o, b16o, b16o, b16o, f32o, b16o, f32o, SDS((t, 1), F32)),
        compiler_params=_cp("parallel"),
    )(pa, h, p, p, p, bm, bm, w3, xf, small)


def mixb_head(dxn, xh, rstd, small, w3, p, bm, ya, yb, ca, h):
    t = dxn.shape[0]
    tm = _row_tile(t, 256)

    def body(dy_ref, xh_ref, rs_ref, sm_ref, w_ref, bg_ref, yr_ref, gla_ref, glb_ref, bma_ref, bmb_ref,
             ya_ref, yb_ref, ca_ref, h_ref,
             dz_ref, dzb_ref, dya_ref, dyb_ref, dbg_ref, dca_ref, dh_ref, dphi_ref, dln_ref, dbm_ref):
        i = pl.program_id(0)
        dy = dy_ref[...]
        xhat = xh_ref[...]
        dz = _ln_bwd(dy, xhat, rs_ref[...], sm_ref[R_LNG + 1:R_LNG + 2, :])

        @pl.when(i == 0)
        def _():
            dln_ref[...] = jnp.zeros_like(dln_ref)
            dbm_ref[...] = jnp.zeros_like(dbm_ref)

        dln_ref[0:1, :] += jnp.sum(dy * xhat, axis=0, keepdims=True)
        dln_ref[1:2, :] += jnp.sum(dy, axis=0, keepdims=True)
        dz_ref[...] = dz
        dzb = dz.astype(BF16)
        dzb_ref[...] = dzb
        dm = _dot(dzb, w_ref[2], NT)
        ga = _sigmoid_t(gla_ref[...].astype(F32) + bma_ref[...])
        gb = _sigmoid_t(glb_ref[...].astype(F32) + bmb_ref[...])
        dya = (dm * ga).astype(BF16)
        dyb = (dm * gb).astype(BF16)
        dya_ref[...] = dya
        dyb_ref[...] = dyb
        dgla = dm * ya_ref[...].astype(F32) * ga * (1.0 - ga)
        dglb = dm * yb_ref[...].astype(F32) * gb * (1.0 - gb)
        dbm_ref[0:1, :] += jnp.sum(dgla, axis=0, keepdims=True)
        dbm_ref[1:2, :] += jnp.sum(dglb, axis=0, keepdims=True)
        dphi_ref[:, D:2 * D] = dgla.astype(BF16)
        dphi_ref[:, 2 * D:3 * D] = dglb.astype(BF16)
        dpa = _dot(dya, w_ref[0], NT)
        dpb = _dot(dyb, w_ref[1], NT)
        dbg_ref[...] = (dpa * ca_ref[...].astype(F32)).astype(BF16)
        dca_ref[...] = dpa * bg_ref[...].astype(F32)
        yr = yr_ref[...].astype(F32)
        ge, th = _gelu(yr)
        dh_ref[...] = (dpb * ge).astype(BF16)
        dphi_ref[:, 0:D] = (dpb * h_ref[...].astype(F32) * _gelu_grad(yr, th)).astype(BF16)

    row = pl.BlockSpec((tm, D), lambda i: (i, 0))
    f32o, b16o = SDS((t, D), F32), SDS((t, D), BF16)
    acc2 = pl.BlockSpec((2, D), lambda i: (0, 0))
    return pl.pallas_call(
        body, name="mixb_head", grid=(t // tm,),
        in_specs=[row, row, pl.BlockSpec((tm, 1), lambda i: (i, 0)), _full((SMALL_ROWS, D)), _full((3, D, D)),
                  _pcol(tm, 0), _pcol(tm, 4), _pcol(tm, 5), _pcol(tm, 6),
                  pl.BlockSpec((1, D), lambda i: (0, 0)), pl.BlockSpec((1, D), lambda i: (0, 1)),
                  row, row, row, row],
        out_specs=(row,) * 7 + (pl.BlockSpec((tm, 3 * D), lambda i: (i, 0)), acc2, acc2),
        out_shape=(f32o, b16o, b16o, b16o, b16o, f32o, b16o, SDS((t, 3 * D), BF16), SDS((2, D), F32), SDS((2, D), F32)),
        compiler_params=_cp("arbitrary"),
    )(dxn, xh, rstd, small, w3, p, p, p, p, bm, bm, ya, yb, ca, h)


def mixb_rec(dh, gr, gi, h, xc, ap, rgw):
    t = dh.shape[0]
    tm = _row_tile(t, 256)
    nt = t // tm

    def body(dh_ref, gr_ref, gi_ref, h_ref, hh_ref, xc_ref, ap_ref, rgw_ref, dg_ref, dxc_ref, red_ref,
             ext, ext_a, c_s, lam_s, lam_c, a_c):
        i = nt - 1 - pl.program_id(0)
        first = i == 0

        @pl.when(pl.program_id(0) == 0)
        def _():
            red_ref[...] = jnp.zeros_like(red_ref)
            lam_c[...] = jnp.zeros_like(lam_c)
            a_c[...] = jnp.zeros_like(a_c)

        ext[0:8, :] = jnp.where(first, 0.0, hh_ref[...].astype(F32)[8:16, :])
        ext[8:, :] = h_ref[...].astype(F32)
        hprev = ext[pl.ds(7, tm), :]
        lam_s[...] = dh_ref[...].astype(F32)
        gr = gr_ref[...]
        gi = gi_ref[...]
        xc = xc_ref[...]
        ap = ap_ref[...]
        sp = _softplus(-ap)
        la = (-LRU_C) * gr * sp
        a = jnp.exp(la)
        ext_a[0:tm, :] = a
        ext_a[tm:, :] = a_c[...]
        c_s[...] = ext_a[pl.ds(1, tm), :]
        lam_c[...] = _scan_tile(c_s, lam_s, lam_s, lam_c[...], tm, reverse=True)
        a_c[...] = jnp.broadcast_to(a[0:1, :], (8, D))
        lam = lam_s[...]
        row = lax.broadcasted_iota(jnp.int32, (tm, D), 0) + i * tm
        start = row == 0
        mult = jnp.where(start, 1.0, jnp.sqrt(_neg_expm1(2.0 * la)))
        dmult = jnp.where(start, 0.0, lam * xc * gi)
        dla = lam * hprev * a - dmult * a * a / mult
        dg1 = (-LRU_C) * sp * dla * gr * (1.0 - gr)
        dg0 = lam * xc * mult * gi * (1.0 - gi)
        dsp = jnp.sum((-LRU_C) * gr * dla, axis=0, keepdims=True)
        red_ref[0:1, :] += jnp.sum(dg0, axis=0, keepdims=True)
        red_ref[1:2, :] += jnp.sum(dg1, axis=0, keepdims=True)
        red_ref[2:3, :] += -dsp * _sigmoid(-ap)
        dg0b = dg0.astype(BF16)
        dg1b = dg1.astype(BF16)
        dg_ref[0] = dg0b
        dg_ref[1] = dg1b
        parts = []
        for hd in range(HEADS):
            sl = slice(hd * HD, (hd + 1) * HD)
            parts.append(_dot(dg0b[:, sl], rgw_ref[0, hd], NT) + _dot(dg1b[:, sl], rgw_ref[1, hd], NT))
        dxc = lam * gi * mult + jnp.concatenate(parts, axis=1)
        dxc_ref[...] = dxc
        red_ref[3:4, :] += jnp.sum(dxc, axis=0, keepdims=True)

    row = pl.BlockSpec((tm, D), lambda i: (nt - 1 - i, 0))
    halo = pl.BlockSpec((16, D), lambda i: (jnp.maximum((nt - 1 - i) * (tm // 16) - 1, 0), 0))
    ext, tile, edge = pltpu.VMEM((tm + 8, D), F32), pltpu.VMEM((tm, D), F32), pltpu.VMEM((8, D), F32)
    return pl.pallas_call(
        body, name="mixb_rec", grid=(nt,),
        in_specs=[row, row, row, row, halo, row, _full((1, D)), _full((2, HEADS, HD, HD))],
        out_specs=(pl.BlockSpec((2, tm, D), lambda i: (0, nt - 1 - i, 0)), row, pl.BlockSpec((8, D), lambda i: (0, 0))),
        out_shape=(SDS((2, t, D), BF16), SDS((t, D), F32), SDS((8, D), F32)),
        scratch_shapes=[ext, ext, tile, tile, edge, edge],
        compiler_params=_cp("arbitrary"),
    )(dh, gr, gi, h, h, xc, ap, rgw)


def mixb_conv(dca, dxc, dbg, p, small):
    t = dca.shape[0]
    tm = _row_tile(t, 256)
    nt = t // tm

    def body(dca_ref, dcan_ref, dxc_ref, dxcn_ref, dbg_ref, cg_ref, v_ref, xr_ref, cgh_ref, vh_ref, xrh_ref, sm_ref,
             dplo_ref, dxr_ref, red_ref, e_dca, e_dxc, e_cv, e_xr):
        i = pl.program_id(0)
        first = i == 0
        last = i == nt - 1

        @pl.when(first)
        def _():
            red_ref[...] = jnp.zeros_like(red_ref)

        dca = dca_ref[...]
        dxc = dxc_ref[...]
        e_dca[0:tm, :] = dca
        e_dca[tm:, :] = jnp.where(last, 0.0, dcan_ref[...])
        e_dxc[0:tm, :] = dxc
        e_dxc[tm:, :] = jnp.where(last, 0.0, dxcn_ref[...])
        cg = cg_ref[...].astype(F32)
        v = v_ref[...].astype(F32)
        xr = xr_ref[...].astype(F32)
        e_cv[0:8, :] = jnp.where(first, 0.0, (cgh_ref[...].astype(F32) * vh_ref[...].astype(F32))[8:16, :])
        e_cv[8:, :] = cg * v
        e_xr[0:8, :] = jnp.where(first, 0.0, xrh_ref[...].astype(F32)[8:16, :])
        e_xr[8:, :] = xr
        dcv = (sm_ref[R_SC + 2:R_SC + 3, :] * dca + sm_ref[R_SC + 1:R_SC + 2, :] * e_dca[pl.ds(1, tm), :]
               + sm_ref[R_SC:R_SC + 1, :] * e_dca[pl.ds(2, tm), :])
        dplo_ref[:, 0:D] = dbg_ref[...]
        dplo_ref[:, D:2 * D] = (dcv * v).astype(BF16)
        dplo_ref[:, 2 * D:3 * D] = (dcv * cg).astype(BF16)
        dxr = (sm_ref[R_RC + 3:R_RC + 4, :] * dxc + sm_ref[R_RC + 2:R_RC + 3, :] * e_dxc[pl.ds(1, tm), :]
               + sm_ref[R_RC + 1:R_RC + 2, :] * e_dxc[pl.ds(2, tm), :] + sm_ref[R_RC:R_RC + 1, :] * e_dxc[pl.ds(3, tm), :])
        dxr_ref[...] = dxr.astype(BF16)
        for k in range(3):
            red_ref[R_SC + k:R_SC + k + 1, :] += jnp.sum(dca * e_cv[pl.ds(6 + k, tm), :], axis=0, keepdims=True)
        for k in range(4):
            red_ref[R_RC + k:R_RC + k + 1, :] += jnp.sum(dxc * e_xr[pl.ds(5 + k, tm), :], axis=0, keepdims=True)

    row = pl.BlockSpec((tm, D), lambda i: (i, 0))
    ext = pltpu.VMEM((tm + 8, D), F32)
    return pl.pallas_call(
        body, name="mixb_conv", grid=(nt,),
        in_specs=[row, _next_halo(tm, t, 0), row, _next_halo(tm, t, 0), row,
                  _pcol(tm, 1), _pcol(tm, 2), _pcol(tm, 3), _prev_halo16(tm, 1), _prev_halo16(tm, 2), _prev_halo16(tm, 3),
                  _full((SMALL_ROWS, D))],
        out_specs=(pl.BlockSpec((tm, 3 * D), lambda i: (i, 0)), row, pl.BlockSpec((8, D), lambda i: (0, 0))),
        out_shape=(SDS((t, 3 * D), BF16), SDS((t, D), BF16), SDS((8, D), F32)),
        scratch_shapes=[ext, ext, ext, ext],
        compiler_params=_cp("arbitrary"),
    )(dca, dca, dxc, dxc, dbg, p, p, p, p, p, p, small)


def mixb_dx(dz, dplo, dxr, dphi, win):
    t = dz.shape[0]
    tm = _row_tile(t, 1024)

    def body(dz_ref, lo_ref, xr_ref, hi_ref, w_ref, dx_ref, acc):
        k = pl.program_id(1)

        @pl.when(k == 0)
        def _():
            acc[...] = ALPHA * dz_ref[...]

        @pl.when(k < 3)
        def _():
            acc[...] += _dot(lo_ref[...], w_ref[...], NT)

        @pl.when(k == 3)
        def _():
            acc[...] += _dot(xr_ref[...], w_ref[...], NT)

        @pl.when(k > 3)
        def _():
            acc[...] += _dot(hi_ref[...], w_ref[...], NT)

        @pl.when(k == 6)
        def _():
            dx_ref[...] = acc[...]

    row = pl.BlockSpec((tm, D), lambda i, k: (i, 0))
    return pl.pallas_call(
        body, name="mixb_dx", grid=(t // tm, 7),
        in_specs=[row, pl.BlockSpec((tm, D), lambda i, k: (i, jnp.minimum(k, 2))), row,
                  pl.BlockSpec((tm, D), lambda i, k: (i, jnp.clip(k - 4, 0, 2))),
                  pl.BlockSpec((D, D), lambda i, k: (0, k))],
        out_specs=row, out_shape=SDS((t, D), F32),
        scratch_shapes=[pltpu.VMEM((tm, D), F32)],
        compiler_params=_cp("parallel", "arbitrary"),
    )(dz, dplo, dxr, dphi, win)


def mixb_dwin(xb, dplo, dxr, dphi):
    t = xb.shape[0]
    tk = _row_tile(t, 2048)
    nk = t // tk

    def body(x_ref, lo_ref, xr_ref, hi_ref, o_ref, acc):
        n = pl.program_id(0)
        k = pl.program_id(1)

        @pl.when(k == 0)
        def _():
            acc[...] = jnp.zeros_like(acc)

        @pl.when(n < 3)
        def _():
            acc[...] += _dot(x_ref[...], lo_ref[...], TN)

        @pl.when(n == 3)
        def _():
            acc[...] += _dot(x_ref[...], xr_ref[...], TN)

        @pl.when(n > 3)
        def _():
            acc[...] += _dot(x_ref[...], hi_ref[...], TN)

        @pl.when(k == nk - 1)
        def _():
            o_ref[...] = acc[...].astype(BF16)

    return pl.pallas_call(
        body, name="mixb_dwin", grid=(7, nk),
        in_specs=[pl.BlockSpec((tk, D), lambda n, k: (k, 0)),
                  pl.BlockSpec((tk, D), lambda n, k: (jnp.where(n < 3, k, 0), jnp.minimum(n, 2))),
                  pl.BlockSpec((tk, D), lambda n, k: (jnp.where(n == 3, k, 0), 0)),
                  pl.BlockSpec((tk, D), lambda n, k: (jnp.where(n > 3, k, 0), jnp.clip(n - 4, 0, 2)))],
        out_specs=pl.BlockSpec((D, D), lambda n, k: (0, n)), out_shape=SDS((D, D_IN), BF16),
        scratch_shapes=[pltpu.VMEM((D, D), F32)],
        compiler_params=_cp("parallel", "arbitrary"),
    )(xb, dplo, dxr, dphi)


def mixb_drgw(xcb, dg):
    t = xcb.shape[0]
    return _mm_tn("mixb_drgw", xcb, pl.BlockSpec((t, HD), lambda g, h: (0, h)),
                  dg, pl.BlockSpec((None, t, HD), lambda g, h: (g, 0, h)),
                  SDS((2, HEADS, HD, HD), BF16), pl.BlockSpec((None, None, HD, HD), lambda g, h: (g, h, 0, 0)),
                  (2, HEADS))


def loss_head(y, tgt):
    t = y.shape[0]
    tm = _row_tile(t, 512)

    def body(y_ref, t_ref, dy_ref, l_ref):
        i = pl.program_id(0)
        e = y_ref[...] - t_ref[...]
        dy_ref[...] = e * (1.0 / D)

        @pl.when(i == 0)
        def _():
            l_ref[...] = jnp.zeros_like(l_ref)

        l_ref[...] += 0.5 * jnp.sum(jnp.mean(e * e, axis=-1, keepdims=True), axis=0, keepdims=True)

    row = pl.BlockSpec((tm, D), lambda i: (i, 0))
    return pl.pallas_call(
        body, name="loss_head", grid=(t // tm,), in_specs=[row, row],
        out_specs=(row, pl.BlockSpec((8, 128), lambda i: (0, 0))),
        out_shape=(SDS((t, D), F32), SDS((8, 128), F32)),
        compiler_params=_cp("arbitrary"),
    )(y, tgt)


def _adamw(w, g, m, v):
    m = ADAM_B1 * m + (1.0 - ADAM_B1) * g
    v = ADAM_B2 * v + (1.0 - ADAM_B2) * (g * g)
    m_hat = m / (1.0 - ADAM_B1 ** ADAM_STEP)
    v_hat = v / (1.0 - ADAM_B2 ** ADAM_STEP)
    delta = -ADAM_LR * (m_hat / (jnp.sqrt(v_hat) + ADAM_EPS) + ADAM_WD * w)
    return delta, m, v


def adam_big(name, w, m, v, parts, rows, lanes, blk_off, tr, l0, prev=None, after=()):
    nr = rows // tr
    nl = len(parts)

    def body(w_ref, m_ref, v_ref, *rest):
        g_ref, d_ref, mo_ref, vo_ref = rest[-4:]
        l = pl.program_id(0)
        for ll in range(nl):
            pr = rest[ll]

            @pl.when(l == ll)
            def _():
                g = pr[0].astype(F32)
                for s in range(1, N_DEV):
                    g = g + pr[s].astype(F32)
                g_ref[...] = g
                d, mn, vn = _adamw(w_ref[...], g, m_ref[...], v_ref[...])
                d_ref[...] = d
                mo_ref[...] = mn
                vo_ref[...] = vn

    blk = pl.BlockSpec((None, tr, lanes), lambda l, r: (l + l0, r, 0))

    def part_spec(ll):
        return pl.BlockSpec((N_DEV, tr, lanes), lambda l, r: (0, jnp.where(l == ll, r, 0) + blk_off, 0))

    out = SDS(w.shape, F32)
    extra = [] if prev is None else list(prev)
    return pl.pallas_call(
        body, name=name, grid=(nl, nr),
        in_specs=[blk, blk, blk] + [part_spec(ll) for ll in range(nl)] + [ANY] * (len(extra) + len(after)),
        out_specs=(blk,) * 4, out_shape=(out,) * 4,
        input_output_aliases={3 + nl + i: i for i in range(len(extra))},
        compiler_params=_cp("parallel", "parallel"),
    )(w, m, v, *parts, *extra, *after)


def small_update(me1, land_sh, land_rep, sharded, replicated):
    ns, nr = len(sharded), len(replicated)

    def body(me_ref, lsh, lrep, *refs):
        ins, outs = refs[:3 * (ns + nr)], refs[3 * (ns + nr):]

        def total(read):
            g = read(0)
            for s in range(1, N_DEV):
                g = g + read(s)
            return g

        def update(k, g, sl):
            w_ref, m_ref, v_ref = ins[3 * k:3 * k + 3]
            d, mn, vn = _adamw(w_ref[sl], g, m_ref[sl], v_ref[sl])
            for o, val in zip(outs[4 * k:4 * k + 4], (g, d, mn, vn)):
                o[sl] = val

        for k, (r0, n) in enumerate(GS_ROWS):
            update(k, total(lambda s: lsh[s, :, r0:r0 + n, :]), (slice(None),) * 3)
        lo, hi = (slice(None), slice(0, D)), (slice(None), slice(D, 2 * D))
        update(ns, total(lambda s: lrep[s, 0:4, :]), lo)
        update(ns, total(lambda s: lrep[s, 4:8, :]), hi)
        update(ns + 1, total(lambda s: lrep[s, 8:12, :]), (slice(None),) * 2)
        update(ns + 2, total(lambda s: lrep[s, 12:16, :]), (slice(None),) * 2)

    def whole(a):
        nd = a.ndim
        return pl.BlockSpec(a.shape, lambda i, me: (0,) * nd)

    params = [a for wmv in list(sharded) + list(replicated) for a in wmv]
    out_shape = tuple(SDS(wmv[0].shape, F32) for wmv in list(sharded) + list(replicated) for _ in range(4))
    res = pl.pallas_call(
        body, name="small_update", out_shape=out_shape,
        grid_spec=pltpu.PrefetchScalarGridSpec(
            num_scalar_prefetch=1, grid=(1,),
            in_specs=[pl.BlockSpec((N_DEV, DEPTH, 16, 128), lambda i, me: (0, 0, 0, me[0])), whole(land_rep)]
            + [whole(a) for a in params],
            out_specs=tuple(pl.BlockSpec(s.shape, lambda i, me, nd=len(s.shape): (0,) * nd) for s in out_shape)),
        compiler_params=_cp("arbitrary"),
    )(me1, land_sh, land_rep, *params)
    return [list(res[4 * k:4 * k + 4]) for k in range(ns + nr)]


def _ffn_forward(xf, xb, gw, f, s, between=None):
    g, u, a = ffn_up(xb, gw["w1"], f)
    if between is not None:
        a = between(a)
    xo, xob, xh, rs = ffn_down_ln(a, gw["w2"], f, xf, gw["small"], s)
    return (xo, xob), dict(xb=xb, g=g, u=u, a=a, xh=xh, rs=rs)


def _ffn_backward(dxn, sv, gw, f, s):
    dz, df, dgu, dln = ffn_bwd_gates(dxn, sv["xh"], sv["rs"], gw["small"], s, gw["w2"], f, sv["g"], sv["u"])
    dx = ffn_bwd_dx(dz, dgu, gw["w1"], f)
    dw1 = ffn_dw1(sv["xb"], dgu, f)
    dw2 = ffn_dw2(sv["a"], df, f)
    return dx, dw1, dw2, dln


def _mixer_forward(xf, xb, gw, rcb, ap, bm):
    p = mix_proj(xb, gw["win"])
    ca, pa, xc, xcb, gi, gr, h = mix_pre(p, gw["small"], rcb, ap, gw["rgw"])
    pb, ya, yb, m, xo, xob, xh, rs = mix_out(pa, h, p, bm, gw["w3"], xf, gw["small"])
    sv = dict(xb=xb, p=p, ca=ca, pa=pa, xc=xc, xcb=xcb, gi=gi, gr=gr, h=h, pb=pb, ya=ya, yb=yb, m=m, xh=xh, rs=rs)
    return (xo, xob), sv


def _mixer_backward(dxn, sv, gw, rcb, ap, bm):
    dz, dzb, dya, dyb, dbg, dca, dh, dphi, dln, dbm = mixb_head(
        dxn, sv["xh"], sv["rs"], gw["small"], gw["w3"], sv["p"], bm, sv["ya"], sv["yb"], sv["ca"], sv["h"])
    dg, dxc, red_rec = mixb_rec(dh, sv["gr"], sv["gi"], sv["h"], sv["xc"], ap, gw["rgw"])
    dplo, dxr, red_conv = mixb_conv(dca, dxc, dbg, sv["p"], gw["small"])
    dx = mixb_dx(dz, dplo, dxr, dphi, gw["win"])
    dwin = mixb_dwin(sv["xb"], dplo, dxr, dphi)
    dwo = mm_tn_square("mixb_dwo", sv["m"], dzb)
    dwoc = mm_tn_square("mixb_dwoc", sv["pa"], dya)
    dwor = mm_tn_square("mixb_dwor", sv["pb"], dyb)
    drgw = mixb_drgw(sv["xcb"], dg)
    return dx, dict(dwin=dwin, dwoc=dwoc, dwor=dwor, dwo=dwo, drgw=drgw), dln, dbm, red_rec, red_conv


def kernel(x, w_in, b_merge, sc_w, rc_w, rc_b, rg_w, rg_b, a_param, w_out_conv, w_out_rnn, w_o, ffn_w1, ffn_w2, ln_g, ln_b, loss_target, m_w_in, m_b_merge, m_sc_w, m_rc_w, m_rc_b, m_rg_w, m_rg_b, m_a_param, m_w_out_conv, m_w_out_rnn, m_w_o, m_ffn_w1, m_ffn_w2, m_ln_g, m_ln_b, v_w_in, v_b_merge, v_sc_w, v_rc_w, v_rc_b, v_rg_w, v_rg_b, v_a_param, v_w_out_conv, v_w_out_rnn, v_w_o, v_ffn_w1, v_ffn_w2, v_ln_g, v_ln_b):
    t = x.shape[1]
    me = _me()

    def rows(parts, total):
        out, off = None, 0
        for part in parts:
            r = part.shape[-2]
            pad = [(0, 0)] * (part.ndim - 2) + [(off, total - off - r), (0, 0)]
            padded = jnp.pad(part, pad)
            out = padded if out is None else out + padded
            off += r
        return out

    small = rows([sc_w, rc_w, rg_b, ln_g, ln_b], SMALL_ROWS)

    me1 = jnp.reshape(me, (1,)).astype(jnp.int32)

    def layer_shards(l, small_now):
        return gather_prep(l, me1, w_in, w_out_conv, w_out_rnn, w_o, ffn_w1, ffn_w2, rg_w, small_now)

    def as_weights(lands):
        return dict(zip(("win", "w3", "w1", "w2", "rgw", "small"), lands))

    xf = x.reshape(t, D)
    xb = xf.astype(BF16)

    early_at, rest_at = [2, 3, 5], [0, 1, 2, 3, 4]

    def ffn_slots(f, o_w1, o_w2, p):
        return [o_w1.at[f, p], o_w2.at[f, _slab(p, W2S, 16), :]]

    def early_slots(lands, p):
        return ffn_slots(0, lands[0], lands[1], p) + [lands[2].at[:, _slab(p, 128, 128)]]

    def rest_slots(lands, p):
        o_win, o_w3, o_w1, o_w2, o_rgw = lands
        return ([o_win.at[:, _slab(p, WIN_S, 128)], o_w3.at[:, _slab(p, 128, 128), :]] + ffn_slots(1, o_w1, o_w2, p)
                + [o_rgw.at[:, :, _slab(p, 32, 32), :]])

    early = ChipGather("gather0a", 3, 3, 3, lambda refs: [refs[0].at[0], refs[1].at[0], refs[2]], early_slots)
    rest = ChipGather("gather0b", 5, 5, 5, lambda refs: [refs[0], refs[1], refs[2].at[1], refs[3].at[1], refs[4]],
                      rest_slots)
    shards, own = layer_shards(0, small)

    def step(method, at, *args):
        res = method(*args[:-1], [shards[i] for i in at], [own[i] for i in at], args[-1])
        for i, s_new, o_new in zip(at, res[-3], res[-2]):
            shards[i], own[i] = s_new, o_new
        return tuple(res[:-3]) + (res[-1],)

    sems_a, thru = step(early.first, early_at, [xb, small])
    sems_b, thru = step(rest.first, rest_at, thru)
    relay_a, thru = step(early.relay, early_at, sems_a, thru)
    ((xb, small),) = step(early.last, early_at, sems_a, relay_a, thru)
    cur, s0 = _ffn_forward(xf, xb, as_weights(own), 0, 0)
    relay_b, thru = step(rest.relay, rest_at, sems_b, list(cur) + [small])
    (thru,) = step(rest.last, rest_at, sems_b, relay_b, thru)
    cur, small = thru[:2], thru[2]

    gathers = [ChipGather(f"gather{l}") for l in range(DEPTH)]
    gws, saved = [as_weights(own)], []
    for l in range(DEPTH):
        gw = gws[l]
        nxt = l + 1 < DEPTH
        if nxt:
            shards, own = layer_shards(l + 1, small)
            sems, shards, own, thru = gathers[l + 1].first(shards, own, list(cur) + [small])
            cur, small = thru[:2], thru[2]
        rcb, ap, bm = rc_b[l][None], a_param[l][None], b_merge[l][None]
        if l > 0:
            cur, s0 = _ffn_forward(cur[0], cur[1], gw, 0, 0)
        cur, s1 = _mixer_forward(cur[0], cur[1], gw, rcb, ap, bm)
        if nxt and l > 0:
            relay_sems, shards, own, cur = gathers[l + 1].relay(sems, shards, own, cur)
        if nxt and l == 0:
            def late_relay(a):
                nonlocal relay_sems, shards, own
                relay_sems, shards, own, (a,) = gathers[1].relay(sems, shards, own, [a])
                return a

            cur, s2 = _ffn_forward(cur[0], cur[1], gw, 1, 2, late_relay)
        else:
            cur, s2 = _ffn_forward(cur[0], cur[1], gw, 1, 2)
        saved.append((s0, s1, s2))
        if nxt:
            _, own, cur = gathers[l + 1].last(sems, relay_sems, shards, own, cur)
            gws.append(as_weights(own))

    dy, loss_tile = loss_head(cur[0], loss_target.reshape(t, D))
    loss = lax.psum(loss_tile[0, 0], ("x", "y", "c"))

    lands = [dict() for _ in range(DEPTH)]
    gsmall, grep = [None] * DEPTH, [None] * DEPTH
    flights = {}

    def launch(key, kinds, grads, dy):
        ex, prep = grad_scatter(f"scatter{key}", kinds)
        fl = ex.start(grads, prep(me1, grads), [] if dy is None else [dy])
        flights[key] = (ex, fl)
        return None if dy is None else fl[4][0]

    def landed(key, thru):
        ex, fl = flights.pop(key)
        return ex.wait(fl[0], fl[1], fl[2], fl[3], thru)

    ffn_b, mixer, ffn_a = ("w1b", "w2b"), ("w_in", "w3", "rgw"), ("w1a", "w2a")
    SPLIT = (0, 1)
    for l in reversed(range(DEPTH)):
        gw = gws[l]
        rcb, ap, bm = rc_b[l][None], a_param[l][None], b_merge[l][None]
        s0, s1, s2 = saved[l]
        dy, dw1b, dw2b, dln2 = _ffn_backward(dy, s2, gw, 1, 2)
        if l in SPLIT:
            dy = launch(f"{l}b", ["ffn"], [dw1b, dw2b], dy)
        dy, dmix, dln1, dbm, red_rec, red_conv = _mixer_backward(dy, s1, gw, rcb, ap, bm)
        mixer_grads = [dmix["dwin"], dmix["dwoc"], dmix["dwor"], dmix["dwo"], dmix["drgw"]]
        if l in SPLIT:
            dy = launch(f"{l}m", ["mixer"], mixer_grads, dy)
        dy, dw1a, dw2a, dln0 = _ffn_backward(dy, s0, gw, 0, 0)
        if l + 1 < DEPTH and l + 1 not in SPLIT:
            got, (dy,) = landed(str(l + 1), [dy])
            lands[l + 1] = dict(zip(ffn_b + mixer + ffn_a, got))
        if l == 0:
            lands[DEPTH - 1]["w_in"] = launch("0a", ["ffn"], [dw1a, dw2a], lands[DEPTH - 1]["w_in"])
        elif l in SPLIT:
            dy = launch(f"{l}a", ["ffn"], [dw1a, dw2a], dy)
        else:
            dy = launch(str(l), ["ffn", "mixer", "ffn"], [dw1b, dw2b] + mixer_grads + [dw1a, dw2a], dy)
        gsmall[l] = rows([red_conv[0:8], red_rec[0:2], dln0[0:1], dln1[0:1], dln2[0:1], dln0[1:2], dln1[1:2],
                          dln2[1:2]], 16)
        grep[l] = [dbm[0:1], dbm[1:2], red_rec[3:4], red_rec[2:3]]
    grad_x = dy.reshape(1, t, D)

    g_rep = rows([grep[l][kind] for kind in range(4) for l in range(DEPTH)], 16)
    land_sh, land_rep = gather_small_grads(jnp.stack(gsmall, axis=0), g_rep)
    small_names = ("sc_w", "rc_w", "rg_b", "ln_g", "ln_b", "b_merge", "rc_b", "a_param")
    small_res = small_update(
        me1, land_sh, land_rep,
        [(sc_w, m_sc_w, v_sc_w), (rc_w, m_rc_w, v_rc_w), (rg_b, m_rg_b, v_rg_b), (ln_g, m_ln_g, v_ln_g),
         (ln_b, m_ln_b, v_ln_b)],
        [(b_merge, m_b_merge, v_b_merge), (rc_b, m_rc_b, v_rc_b), (a_param, m_a_param, v_a_param)])

    def parts_of(layers):
        ls = [lands[l] for l in layers]
        return dict(w_in=[a["w_in"] for a in ls],
                    w3=[a["w3"].reshape(N_DEV, 3 * 128, D) for a in ls],
                    rgw=[a["rgw"].reshape(N_DEV, 2 * HEADS * 32, HD) for a in ls],
                    w1=[a[k] for a in ls for k in ("w1a", "w1b")],
                    w2=[a[k] for a in ls for k in ("w2a", "w2b")])

    families = [("w_in", w_in, m_w_in, v_w_in, "w_in", 1, D, WIN_S, 0, 128),
                ("w_out_conv", w_out_conv, m_w_out_conv, v_w_out_conv, "w3", 1, 128, D, 0, 128),
                ("w_out_rnn", w_out_rnn, m_w_out_rnn, v_w_out_rnn, "w3", 1, 128, D, 1, 128),
                ("w_o", w_o, m_w_o, v_w_o, "w3", 1, 128, D, 2, 128),
                ("ffn_w1", ffn_w1, m_ffn_w1, v_ffn_w1, "w1", 2, D, FS, 0, 256),
                ("ffn_w2", ffn_w2, m_ffn_w2, v_ffn_w2, "w2", 2, W2S, D, 0, W2S // 2),
                ("rg_w", rg_w, m_rg_w, v_rg_w, "rgw", 1, 2 * HEADS * 32, HD, 0, 256)]

    def adam_pass(tag, layers, prev):
        parts = parts_of(layers)
        outs = {}
        for name, w, m, v, fam, per, nrow, lanes, blk_off, tr in families:
            r3 = lambda a: a.reshape(DEPTH * per, nrow, lanes)
            outs[name] = adam_big(f"adam_{name}_{tag}", r3(w), r3(m), r3(v), parts[fam], nrow, lanes, blk_off, tr,
                                  per * layers[0], None if prev is None else prev[name],
                                  (dy,) if prev is None else ())
        return outs

    def through(outs, key, extra):
        thru = [outs[f[0]][0] for f in families] + list(extra)
        got, thru = landed(key, thru)
        outs = {f[0]: [thru[i]] + list(outs[f[0]][1:]) for i, f in enumerate(families)}
        return got, outs, thru[len(families):]

    done = adam_pass("top", list(range(2, DEPTH)), None)
    for l, tag in ((1, "second"), (0, "first")):
        for part, names in (("b", ffn_b), ("m", mixer), ("a", ffn_a)):
            got, done, _ = through(done, f"{l}{part}", [])
            lands[l].update(zip(names, got))
        done = adam_pass(tag, [l], done)
    res = {f[0]: [o.reshape(f[1].shape) for o in done[f[0]]] for f in families}

    res.update(zip(small_names, small_res))

    names = ["w_in", "b_merge", "sc_w", "rc_w", "rc_b", "rg_w", "rg_b", "a_param", "w_out_conv", "w_out_rnn", "w_o",
             "ffn_w1", "ffn_w2", "ln_g", "ln_b"]
    out = [loss, grad_x]
    for k in range(4):
        out += [res[n][k] for n in names]
    return tuple(out)
```

```python
import functools

import jax
import jax.numpy as jnp
from jax import lax
from jax.experimental import pallas as pl
from jax.experimental.pallas import tpu as pltpu

F32 = jnp.float32
BF16 = jnp.bfloat16
SDS = jax.ShapeDtypeStruct

N_DEV = 8
DEPTH = 4
D = 1024
D_FF = 2816
FS = D_FF // 4
W2S = D_FF // 8
D_IN = 7 * D
WIN_S = D_IN // 8
HEADS = 4
HD = D // HEADS
LRU_C = 8.0
ALPHA = (2.0 * DEPTH) ** 0.25
LN_EPS = 1e-5
ADAM_LR, ADAM_B1, ADAM_B2, ADAM_EPS, ADAM_WD, ADAM_STEP = 0.001, 0.9, 0.999, 1e-08, 0.01, 10

R_SC, R_RC, R_RGB, R_LNG, R_LNB = 0, 3, 7, 9, 12
SMALL_ROWS = 16
GS_ROWS = ((0, 3), (3, 4), (8, 2), (10, 3), (13, 3))

NN = ((1,), (0,))
NT = ((1,), (1,))
TN = ((0,), (0,))
MESH = pl.DeviceIdType.MESH
ANY = pl.BlockSpec(memory_space=pl.ANY)
VMEM_LIMIT = 52 * 1024 * 1024


def _dot(a, b, dims):
    return lax.dot_general(a, b, (dims, ((), ())), preferred_element_type=F32)


def _cp(*sem):
    return pltpu.CompilerParams(dimension_semantics=sem, vmem_limit_bytes=VMEM_LIMIT)


def _sigmoid(x):
    return 1.0 / (1.0 + jnp.exp(-x))


def _sigmoid_t(x):
    return 0.5 * jnp.tanh(0.5 * x) + 0.5


def _log1p(e):
    u = 1.0 + e
    return jnp.where(u == 1.0, e, jnp.log(u) * e / jnp.where(u == 1.0, 1.0, u - 1.0))


def _softplus(x):
    return jnp.maximum(x, 0.0) + _log1p(jnp.exp(-jnp.abs(x)))


def _neg_expm1(x):
    u = jnp.exp(x)
    um1 = u - 1.0
    safe = jnp.logical_and(u != 1.0, um1 != -1.0)
    r = um1 * x / jnp.where(safe, jnp.log(jnp.where(safe, u, 0.5)), 1.0)
    return -jnp.where(u == 1.0, x, jnp.where(um1 == -1.0, -1.0, r))


def _gelu(y):
    c = 0.7978845608028654
    t = jnp.tanh(c * (y + 0.044715 * y * y * y))
    return 0.5 * y * (1.0 + t), t


def _gelu_grad(y, t):
    c = 0.7978845608028654
    return 0.5 * (1.0 + t) + 0.5 * y * (1.0 - t * t) * c * (1.0 + 3.0 * 0.044715 * y * y)


def _ln_fwd(z, g, b):
    mu = jnp.mean(z, axis=-1, keepdims=True)
    zc = z - mu
    var = jnp.mean(zc * zc, axis=-1, keepdims=True)
    rstd = lax.rsqrt(var + LN_EPS)
    xh = zc * rstd
    return xh * g + b, xh, rstd


def _ln_bwd(dy, xh, rstd, g):
    dxh = dy * g
    m1 = jnp.mean(dxh, axis=-1, keepdims=True)
    m2 = jnp.mean(dxh * xh, axis=-1, keepdims=True)
    return rstd * (dxh - m1 - xh * m2)


def _row_tile(t, want):
    return min(want, t)


def _me():
    return 4 * lax.axis_index("x") + 2 * lax.axis_index("y") + lax.axis_index("c")


def _coords(p):
    return (p // 4, (p // 2) % 2, p % 2)


def _all_to_all_copies(srcs_of, dsts_of, waits, send_sems, recv_sems, loc_sems):
    me = _me()
    n = len(waits)
    own_src, own_dst = srcs_of(me), dsts_of(me)
    local = [pltpu.make_async_copy(own_src[k], own_dst[k], loc_sems.at[k]) for k in range(n)]
    for cp in local:
        cp.start()
    for d in range(1, N_DEV):
        p = (me + d) % N_DEV
        src, dst = srcs_of(p), dsts_of(me)
        for k in range(n):
            pltpu.make_async_remote_copy(
                src_ref=src[k], dst_ref=dst[k], send_sem=send_sems.at[waits[k][1]],
                recv_sem=recv_sems.at[waits[k][1]], device_id=_coords(p), device_id_type=MESH).start()
    done = set()
    for k in range(n):
        ref, s = waits[k]
        if s in done:
            continue
        done.add(s)
        pltpu.make_async_remote_copy(
            src_ref=ref, dst_ref=ref, send_sem=send_sems.at[s], recv_sem=recv_sems.at[s],
            device_id=_coords(me), device_id_type=MESH).wait()
    for cp in local:
        cp.wait()


HBM = pl.BlockSpec(memory_space=pltpu.HBM)
SEM = pl.BlockSpec(memory_space=pltpu.SEMAPHORE)
EFFECT = pltpu.SideEffectType.DATAFLOW_SIDE_EFFECTING


def _in_hbm(a):
    return pltpu.with_memory_space_constraint(a, pltpu.HBM)


class Exchange:
    def __init__(self, name, src_of, dst_of, sem_of, span_of):
        self.name, self.src_of, self.dst_of, self.sem_of, self.span_of = name, src_of, dst_of, sem_of, span_of
        self.nsem = max(sem_of) + 1

    def start(self, srcs, lands, thru):
        n, m = len(srcs), len(lands)
        ops = list(srcs) + list(lands) + list(thru)

        def body(*refs):
            src_refs, land_refs = refs[:n], refs[n:n + m]
            send_sems, recv_sems = refs[len(ops)], refs[len(ops) + 1]
            me = _me()
            for dd in range(1, N_DEV):
                p = (me + dd) % N_DEV
                s, d = self.src_of(src_refs, p), self.dst_of(land_refs, me)
                for k in range(len(self.sem_of)):
                    pltpu.make_async_remote_copy(
                        src_ref=s[k], dst_ref=d[k], send_sem=send_sems.at[self.sem_of[k]],
                        recv_sem=recv_sems.at[self.sem_of[k]], device_id=_coords(p), device_id_type=MESH).start()

        sem = pltpu.SemaphoreType.DMA((self.nsem,))
        res = pl.pallas_call(
            body, name=self.name + "_start",
            out_shape=(sem, sem) + tuple(pltpu.HBM(a.shape, a.dtype) for a in ops),
            in_specs=[HBM] * len(ops), out_specs=(SEM, SEM) + (HBM,) * len(ops),
            input_output_aliases={i: 2 + i for i in range(len(ops))},
            compiler_params=pltpu.CompilerParams(has_side_effects=EFFECT),
        )(*[_in_hbm(a) for a in ops])
        return res[0], res[1], res[2:2 + n], res[2 + n:2 + n + m], list(res[2 + n + m:])

    def wait(self, send_sems, recv_sems, srcs, lands, thru):
        n, m = len(srcs), len(lands)
        ops = list(srcs) + list(lands) + list(thru)

        def body(*refs):
            land_refs = refs[n:n + m]
            ssem, rsem = refs[len(ops)], refs[len(ops) + 1]
            me = _me()
            spans = self.span_of(land_refs)
            for s in range(self.nsem):
                cp = pltpu.make_async_remote_copy(
                    src_ref=spans[s], dst_ref=spans[s], send_sem=ssem.at[s], recv_sem=rsem.at[s],
                    device_id=_coords(me), device_id_type=MESH)
                cp.wait_send()
                cp.wait_recv()

        res = pl.pallas_call(
            body, name=self.name + "_wait",
            out_shape=tuple(pltpu.HBM(a.shape, a.dtype) for a in ops),
            in_specs=[HBM] * len(ops) + [SEM, SEM], out_specs=(HBM,) * len(ops),
            input_output_aliases={i: i for i in range(len(ops))},
            compiler_params=pltpu.CompilerParams(has_side_effects=EFFECT),
        )(*ops, send_sems, recv_sems)
        return res[n:n + m], list(res[n + m:])


class ChipGather:
    def __init__(self, name, nsrc=6, nland=6, ncopy=6, views=None, slots=None):
        self.name, self.nsrc, self.nland, self.ncopy = name, nsrc, nland, ncopy
        self.views = views if views is not None else list
        self.slots = slots if slots is not None else self.layer_slots

    @staticmethod
    def layer_slots(lands, p):
        o_win, o_w3, o_w1, o_w2, o_rgw, o_sm = lands
        return [o_win.at[:, pl.ds(pl.multiple_of(p * WIN_S, 128), WIN_S)],
                o_w3.at[:, pl.ds(pl.multiple_of(p * 128, 128), 128), :],
                o_w1.at[:, p],
                o_w2.at[:, pl.ds(pl.multiple_of(p * W2S, 16), W2S), :],
                o_rgw.at[:, :, pl.ds(pl.multiple_of(p * 32, 32), 32), :],
                o_sm.at[:, pl.ds(pl.multiple_of(p * 128, 128), 128)]]

    @staticmethod
    def _places():
        x, y, c = lax.axis_index("x"), lax.axis_index("y"), lax.axis_index("c")
        chips = [(1 - x, y), (x, 1 - y), (1 - x, 1 - y)]
        return (x, y, c), (x, y, 1 - c), chips

    @staticmethod
    def _index(place):
        return 4 * place[0] + 2 * place[1] + place[2]

    def _call(self, tag, body, ops, sems_in, sems_out):
        nops = len(ops)
        sem = pltpu.SemaphoreType.DMA((max(sems_out, 1),))
        res = pl.pallas_call(
            body, name=f"{self.name}_{tag}",
            out_shape=((sem, sem) if sems_out else ()) + tuple(pltpu.HBM(a.shape, a.dtype) for a in ops),
            in_specs=[HBM] * nops + [SEM] * len(sems_in),
            out_specs=((SEM, SEM) if sems_out else ()) + (HBM,) * nops,
            input_output_aliases={i: (2 if sems_out else 0) + i for i in range(nops)},
            compiler_params=pltpu.CompilerParams(has_side_effects=EFFECT),
        )(*[_in_hbm(a) for a in ops], *sems_in)
        return res

    def first(self, srcs, lands, thru):
        ns, nl, n = self.nsrc, self.nland, self.ncopy
        ops = list(srcs) + list(lands) + list(thru)

        def body(*refs):
            src_refs, land_refs = self.views(refs[:ns]), refs[ns:ns + nl]
            send_sems, recv_sems = refs[len(ops)], refs[len(ops) + 1]
            me, sibling, chips = self._places()
            dst = self.slots(land_refs, self._index(me))
            targets = [sibling] + [(cx, cy, me[2]) for cx, cy in chips]
            for r, to in enumerate(targets):
                for k in range(n):
                    pltpu.make_async_remote_copy(
                        src_ref=src_refs[k], dst_ref=dst[k], send_sem=send_sems.at[n *r + k],
                        recv_sem=recv_sems.at[n *r + k], device_id=to, device_id_type=MESH).start()

        res = self._call("first", body, ops, [], 4 * n)
        return (res[0], res[1]), list(res[2:2 + ns]), list(res[2 + ns:2 + ns + nl]), list(res[2 + ns + nl:])

    def relay(self, sems, srcs, lands, thru):
        ns, nl, n = self.nsrc, self.nland, self.ncopy
        ops = list(srcs) + list(lands) + list(thru)

        def wait_body(*refs):
            land_refs = refs[ns:ns + nl]
            ssem, rsem = refs[len(ops)], refs[len(ops) + 1]
            me, sibling, chips = self._places()
            for j, (cx, cy) in enumerate(chips):
                got = self.slots(land_refs, self._index((cx, cy, me[2])))
                for k in range(n):
                    pltpu.make_async_remote_copy(
                        src_ref=got[k], dst_ref=got[k], send_sem=ssem.at[n *(1 + j) + k], recv_sem=rsem.at[n *(1 + j) + k],
                        device_id=me, device_id_type=MESH).wait_recv()

        ops = list(self._call("landed", wait_body, ops, list(sems), 0))

        def start_body(*refs):
            land_refs = refs[ns:ns + nl]
            ssem, rsem = refs[len(ops)], refs[len(ops) + 1]
            me, sibling, chips = self._places()
            for j, (cx, cy) in enumerate(chips):
                got = self.slots(land_refs, self._index((cx, cy, me[2])))
                for k in range(n):
                    pltpu.make_async_remote_copy(
                        src_ref=got[k], dst_ref=got[k], send_sem=ssem.at[n *j + k], recv_sem=rsem.at[n *j + k],
                        device_id=sibling, device_id_type=MESH).start()

        res = self._call("relay", start_body, ops, [], 3 * n)
        return (res[0], res[1]), list(res[2:2 + ns]), list(res[2 + ns:2 + ns + nl]), list(res[2 + ns + nl:])

    def last(self, sems, relay_sems, srcs, lands, thru):
        ns, nl, n = self.nsrc, self.nland, self.ncopy
        ops = list(srcs) + list(lands) + list(thru)

        def body(*refs):
            src_refs, land_refs = self.views(refs[:ns]), refs[ns:ns + nl]
            ssem, rsem, ssem2, rsem2 = refs[len(ops):len(ops) + 4]
            me, sibling, chips = self._places()
            got = self.slots(land_refs, self._index(sibling))
            for k in range(n):
                pltpu.make_async_remote_copy(
                    src_ref=got[k], dst_ref=got[k], send_sem=ssem.at[k], recv_sem=rsem.at[k],
                    device_id=me, device_id_type=MESH).wait_recv()
            for j, (cx, cy) in enumerate(chips):
                got = self.slots(land_refs, self._index((cx, cy, 1 - me[2])))
                for k in range(n):
                    pltpu.make_async_remote_copy(
                        src_ref=got[k], dst_ref=got[k], send_sem=ssem2.at[n *j + k], recv_sem=rsem2.at[n *j + k],
                        device_id=me, device_id_type=MESH).wait_recv()
            for sem_s, sem_r, count in ((ssem, rsem, 4), (ssem2, rsem2, 3)):
                for r in range(count):
                    for k in range(n):
                        pltpu.make_async_remote_copy(
                            src_ref=src_refs[k], dst_ref=src_refs[k], send_sem=sem_s.at[n *r + k],
                            recv_sem=sem_r.at[n *r + k], device_id=me, device_id_type=MESH).wait_send()

        res = self._call("last", body, ops, list(sems) + list(relay_sems), 0)
        return list(res[:ns]), list(res[ns:ns + nl]), list(res[ns + nl:])


GATHER_SHAPES = (SDS((D, D_IN), BF16), SDS((3, D, D), BF16), SDS((2, N_DEV, D, FS), BF16),
                 SDS((2, D_FF, D), BF16), SDS((2, HEADS, HD, HD), BF16), SDS((SMALL_ROWS, D), F32))


def _slab(p, n, align):
    return pl.ds(pl.multiple_of(p * n, align), n)


class GradGroup:
    def __init__(self, nsrc, lands, src_of, dst_of, sem_of, in_specs, out_specs, copy):
        self.nsrc, self.lands, self.src_of, self.dst_of, self.sem_of = nsrc, lands, src_of, dst_of, sem_of
        self.in_specs, self.out_specs, self.copy = in_specs, out_specs, copy


def _mixer_group():
    hd = D // 2

    def src_of(refs, p):
        dwin, dwoc, dwor, dwo, drgw = refs
        r3 = _slab(p, 128, 128)
        return [dwin.at[:, _slab(p, WIN_S, 128)], dwoc.at[r3, :], dwor.at[r3, :], dwo.at[r3, :],
                drgw.at[:, :, _slab(p, 32, 32), :]]

    def dst_of(lands, p):
        l_win, l_w3, l_rgw = lands
        return [l_win.at[p], l_w3.at[p, 0], l_w3.at[p, 1], l_w3.at[p, 2], l_rgw.at[p]]

    def copy(srcs, lands):
        lands[0][...] = srcs[0][...]
        for k in range(3):
            lands[1][k] = srcs[1 + k][...]
        lands[2][...] = srcs[4][...]

    three = pl.BlockSpec((64, D), lambda i, me: (2 * me[0] + i, 0))
    return GradGroup(
        5, (SDS((N_DEV, D, WIN_S), BF16), SDS((N_DEV, 3, 128, D), BF16), SDS((N_DEV, 2, HEADS, 32, HD), BF16)),
        src_of, dst_of, [0, 1, 1, 1, 2],
        [pl.BlockSpec((hd, WIN_S), lambda i, me: (i, me[0])), three, three, three,
         pl.BlockSpec((2, HEADS, 16, HD), lambda i, me: (0, 0, 2 * me[0] + i, 0))],
        [pl.BlockSpec((None, hd, WIN_S), lambda i, me: (me[0], i, 0)),
         pl.BlockSpec((None, 3, 64, D), lambda i, me: (me[0], 0, i, 0)),
         pl.BlockSpec((None, 2, HEADS, 16, HD), lambda i, me: (me[0], 0, 0, i, 0))],
        copy)


def _ffn_group():
    hd, hw = D // 2, W2S // 2

    def src_of(refs, p):
        return [refs[0].at[p], refs[1].at[_slab(p, W2S, 16), :]]

    def dst_of(lands, p):
        return [lands[0].at[p], lands[1].at[p]]

    def copy(srcs, lands):
        lands[0][...] = srcs[0][...]
        lands[1][...] = srcs[1][...]

    return GradGroup(
        2, (SDS((N_DEV, D, FS), BF16), SDS((N_DEV, W2S, D), BF16)), src_of, dst_of, [0, 1],
        [pl.BlockSpec((None, hd, FS), lambda i, me: (me[0], i, 0)), pl.BlockSpec((hw, D), lambda i, me: (2 * me[0] + i, 0))],
        [pl.BlockSpec((None, hd, FS), lambda i, me: (me[0], i, 0)), pl.BlockSpec((None, hw, D), lambda i, me: (me[0], i, 0))],
        copy)


def grad_scatter(name, kinds):
    groups = [_mixer_group() if k == "mixer" else _ffn_group() for k in kinds]

    def per_group(refs, counts, fn):
        out, i = [], 0
        for g, c in zip(groups, counts):
            out += fn(g, refs[i:i + c])
            i += c
        return out

    nsrcs = [g.nsrc for g in groups]
    nlands = [len(g.lands) for g in groups]
    sem_of, off = [], 0
    for g in groups:
        sem_of += [off + s for s in g.sem_of]
        off += len(g.lands)

    ex = Exchange(name,
                  lambda refs, p: per_group(refs, nsrcs, lambda g, r: g.src_of(r, p)),
                  lambda lands, p: per_group(lands, nlands, lambda g, r: g.dst_of(r, p)),
                  sem_of, lambda lands: [a.at[pl.ds(0, 7)] for a in lands])

    def prep(me1, grads):
        nsrc = sum(nsrcs)

        def body(me_ref, *refs):
            srcs, lands = refs[:nsrc], refs[nsrc:]
            i = j = 0
            for g in groups:
                g.copy(srcs[i:i + g.nsrc], lands[j:j + len(g.lands)])
                i += g.nsrc
                j += len(g.lands)

        return list(pl.pallas_call(
            body, name=name + "_prep", out_shape=tuple(s for g in groups for s in g.lands),
            grid_spec=pltpu.PrefetchScalarGridSpec(
                num_scalar_prefetch=1, grid=(2,), in_specs=[s for g in groups for s in g.in_specs],
                out_specs=tuple(s for g in groups for s in g.out_specs)),
            compiler_params=_cp("arbitrary"),
        )(me1, *grads))

    return ex, prep


def gather_prep(l, me1, w_in, w_out_conv, w_out_rnn, w_o, ffn_w1, ffn_w2, rg_w, small):
    hd, hw = D // 2, W2S // 2

    def body(me_ref, win, woc, wor, wo, w1, w2, rgw, sm, c_win, c_w3, c_w1, c_w2, c_rgw, c_sm,
             o_win, o_w3, o_w1, o_w2, o_rgw, o_sm):
        a = win[...].astype(BF16)
        c_win[...] = a
        o_win[...] = a
        for k, r in enumerate((woc, wor, wo)):
            b = r[...].astype(BF16)
            c_w3[k] = b
            o_w3[k] = b
        for src, comp, own in ((w1, c_w1, o_w1), (w2, c_w2, o_w2), (rgw, c_rgw, o_rgw)):
            b = src[...].astype(BF16)
            comp[...] = b
            own[...] = b
        c_sm[...] = sm[...]
        o_sm[...] = sm[...]

    three = pl.BlockSpec((None, 64, D), lambda i, me: (l, i, 0))
    in_specs = [pl.BlockSpec((None, hd, WIN_S), lambda i, me: (l, i, 0)), three, three, three,
                pl.BlockSpec((None, 2, hd, FS), lambda i, me: (l, 0, i, 0)),
                pl.BlockSpec((None, 2, hw, D), lambda i, me: (l, 0, i, 0)),
                pl.BlockSpec((None, 2, HEADS, 16, HD), lambda i, me: (l, 0, 0, i, 0)),
                pl.BlockSpec((None, 8, 128), lambda i, me: (l, i, 0))]
    out_specs = (pl.BlockSpec((hd, WIN_S), lambda i, me: (i, 0)), pl.BlockSpec((3, 64, D), lambda i, me: (0, i, 0)),
                 pl.BlockSpec((2, hd, FS), lambda i, me: (0, i, 0)), pl.BlockSpec((2, hw, D), lambda i, me: (0, i, 0)),
                 pl.BlockSpec((2, HEADS, 16, HD), lambda i, me: (0, 0, i, 0)), pl.BlockSpec((8, 128), lambda i, me: (i, 0)),
                 pl.BlockSpec((hd, WIN_S), lambda i, me: (i, me[0])),
                 pl.BlockSpec((3, 64, D), lambda i, me: (0, 2 * me[0] + i, 0)),
                 pl.BlockSpec((2, None, hd, FS), lambda i, me: (0, me[0], i, 0)),
                 pl.BlockSpec((2, hw, D), lambda i, me: (0, 2 * me[0] + i, 0)),
                 pl.BlockSpec((2, HEADS, 16, HD), lambda i, me: (0, 0, 2 * me[0] + i, 0)),
                 pl.BlockSpec((8, 128), lambda i, me: (i, me[0])))
    compact = (SDS((D, WIN_S), BF16), SDS((3, 128, D), BF16), SDS((2, D, FS), BF16), SDS((2, W2S, D), BF16),
               SDS((2, HEADS, 32, HD), BF16), SDS((SMALL_ROWS, 128), F32))
    res = pl.pallas_call(
        body, name=f"gather_prep{l}", out_shape=compact + GATHER_SHAPES,
        grid_spec=pltpu.PrefetchScalarGridSpec(num_scalar_prefetch=1, grid=(2,), in_specs=in_specs, out_specs=out_specs),
        compiler_params=_cp("arbitrary"),
    )(me1, w_in, w_out_conv, w_out_rnn, w_o, ffn_w1, ffn_w2, rg_w, small)
    return list(res[:6]), list(res[6:])


def gather_small_grads(g_sharded, g_replicated):
    def body(ga, gb, la, lb, send_sems, recv_sems, loc_sems):
        _all_to_all_copies(lambda p: [ga, gb], lambda p: [la.at[p], lb.at[p]],
                           [(la.at[pl.ds(0, 7)], 0), (lb.at[pl.ds(0, 7)], 1)], send_sems, recv_sems, loc_sems)

    return pl.pallas_call(
        body, name="gather_small_grads",
        out_shape=(SDS((N_DEV,) + g_sharded.shape, F32), SDS((N_DEV,) + g_replicated.shape, F32)),
        in_specs=[ANY, ANY], out_specs=(ANY, ANY),
        scratch_shapes=[pltpu.SemaphoreType.DMA((2,)), pltpu.SemaphoreType.DMA((2,)), pltpu.SemaphoreType.DMA((2,))],
        compiler_params=pltpu.CompilerParams(has_side_effects=True),
    )(g_sharded, g_replicated)


def ffn_up(xb, w1, f):
    t = xb.shape[0]
    tm = _row_tile(t, 2048)

    def body(x_ref, wg_ref, wu_ref, g_ref, u_ref, a_ref):
        x = x_ref[...]
        g = _dot(x, wg_ref[...], NN)
        u = _dot(x, wu_ref[...], NN)
        g_ref[...] = g.astype(BF16)
        u_ref[...] = u.astype(BF16)
        a_ref[...] = (g * _sigmoid_t(g) * u).astype(BF16)

    out = pl.BlockSpec((None, tm, FS), lambda j, i: (j, i, 0))
    return pl.pallas_call(
        body, name=f"ffn_up{f}", grid=(4, t // tm),
        in_specs=[pl.BlockSpec((tm, D), lambda j, i: (i, 0)),
                  pl.BlockSpec((None, None, D, FS), lambda j, i: (f, j, 0, 0)),
                  pl.BlockSpec((None, None, D, FS), lambda j, i: (f, j + 4, 0, 0))],
        out_specs=(out, out, out), out_shape=(SDS((4, t, FS), BF16),) * 3,
        compiler_params=_cp("parallel", "parallel"),
    )(xb, w1, w1)


def ffn_down_ln(a, w2, f, xf, small, s):
    t = xf.shape[0]
    tm = _row_tile(t, 512)

    def body(a_ref, w_ref, x_ref, sm_ref, xo_ref, xb_ref, xh_ref, rs_ref):
        acc = _dot(a_ref[0], w_ref[0:FS, :], NN)
        for j in range(1, 4):
            acc = acc + _dot(a_ref[j], w_ref[j * FS:(j + 1) * FS, :], NN)
        z = ALPHA * x_ref[...] + 0.5 * acc
        y, xh, rstd = _ln_fwd(z, sm_ref[R_LNG + s:R_LNG + s + 1, :], sm_ref[R_LNB + s:R_LNB + s + 1, :])
        xo_ref[...] = y
        xb_ref[...] = y.astype(BF16)
        xh_ref[...] = xh
        rs_ref[...] = rstd

    row = pl.BlockSpec((tm, D), lambda i: (i, 0))
    return pl.pallas_call(
        body, name=f"ffn_down_ln{f}", grid=(t // tm,),
        in_specs=[pl.BlockSpec((4, tm, FS), lambda i: (0, i, 0)),
                  pl.BlockSpec((None, D_FF, D), lambda i: (f, 0, 0)),
                  row, pl.BlockSpec((SMALL_ROWS, D), lambda i: (0, 0))],
        out_specs=(row, row, row, pl.BlockSpec((tm, 1), lambda i: (i, 0))),
        out_shape=(SDS((t, D), F32), SDS((t, D), BF16), SDS((t, D), F32), SDS((t, 1), F32)),
        compiler_params=_cp("parallel"),
    )(a, w2, xf, small)


def ffn_bwd_gates(dxn, xh, rstd, small, s, w2, f, g, u):
    t = dxn.shape[0]
    tm = _row_tile(t, 256)

    def body(dy_ref, xh_ref, rs_ref, sm_ref, w_ref, g_ref, u_ref, dz_ref, df_ref, dgu_ref, dln_ref, da_s):
        i = pl.program_id(0)
        dy = dy_ref[...]
        xhat = xh_ref[...]
        dz = _ln_bwd(dy, xhat, rs_ref[...], sm_ref[R_LNG + s:R_LNG + s + 1, :])

        @pl.when(i == 0)
        def _():
            dln_ref[...] = jnp.zeros_like(dln_ref)

        dln_ref[0:1, :] += jnp.sum(dy * xhat, axis=0, keepdims=True)
        dln_ref[1:2, :] += jnp.sum(dy, axis=0, keepdims=True)
        dz_ref[...] = dz
        df = (0.5 * dz).astype(BF16)
        df_ref[...] = df
        for j in range(4):
            da_s[...] = _dot(df, w_ref[j * FS:(j + 1) * FS, :], NT)

            def rows_chunk(r, carry, j=j):
                rows = pl.ds(pl.multiple_of(r * 16, 16), 16)
                da = da_s[rows, :]
                gg = g_ref[j, rows, :].astype(F32)
                uu = u_ref[j, rows, :].astype(F32)
                sg = _sigmoid_t(gg)
                dgu_ref[j, rows, :] = (da * uu * (sg * (1.0 + gg * (1.0 - sg)))).astype(BF16)
                dgu_ref[j + 4, rows, :] = (da * (gg * sg)).astype(BF16)
                return carry

            lax.fori_loop(0, tm // 16, rows_chunk, 0, unroll=2)

    row = pl.BlockSpec((tm, D), lambda i: (i, 0))
    gu = pl.BlockSpec((4, tm, FS), lambda i: (0, i, 0))
    return pl.pallas_call(
        body, name=f"ffn_bwd_gates{f}", grid=(t // tm,),
        in_specs=[row, row, pl.BlockSpec((tm, 1), lambda i: (i, 0)), pl.BlockSpec((SMALL_ROWS, D), lambda i: (0, 0)),
                  pl.BlockSpec((None, D_FF, D), lambda i: (f, 0, 0)), gu, gu],
        out_specs=(row, row, pl.BlockSpec((8, tm, FS), lambda i: (0, i, 0)), pl.BlockSpec((2, D), lambda i: (0, 0))),
        out_shape=(SDS((t, D), F32), SDS((t, D), BF16), SDS((8, t, FS), BF16), SDS((2, D), F32)),
        scratch_shapes=[pltpu.VMEM((tm, FS), F32)],
        compiler_params=_cp("arbitrary"),
    )(dxn, xh, rstd, small, w2, g, u)


def ffn_bwd_dx(dz, dgu, w1, f):
    t = dz.shape[0]
    tm = _row_tile(t, 1024)

    def body(dz_ref, d_ref, w_ref, dx_ref, acc):
        k = pl.program_id(1)

        @pl.when(k == 0)
        def _():
            acc[...] = ALPHA * dz_ref[...]

        acc[...] += _dot(d_ref[...], w_ref[...], NT)

        @pl.when(k == 7)
        def _():
            dx_ref[...] = acc[...]

    row = pl.BlockSpec((tm, D), lambda i, k: (i, 0))
    return pl.pallas_call(
        body, name=f"ffn_bwd_dx{f}", grid=(t // tm, 8),
        in_specs=[row, pl.BlockSpec((None, tm, FS), lambda i, k: (k, i, 0)),
                  pl.BlockSpec((None, None, D, FS), lambda i, k: (f, k, 0, 0))],
        out_specs=row, out_shape=SDS((t, D), F32),
        scratch_shapes=[pltpu.VMEM((tm, D), F32)],
        compiler_params=_cp("parallel", "arbitrary"),
    )(dz, dgu, w1)


def _mm_tn(name, a, a_spec, b, b_spec, out_sds, out_spec, grid):
    def body(a_ref, b_ref, o_ref):
        o_ref[...] = _dot(a_ref[...], b_ref[...], TN).astype(o_ref.dtype)

    return pl.pallas_call(
        body, name=name, grid=grid, in_specs=[a_spec, b_spec], out_specs=out_spec, out_shape=out_sds,
        compiler_params=_cp(*(["parallel"] * len(grid))),
    )(a, b)


def ffn_dw1(xb, dgu, f):
    t = xb.shape[0]
    return _mm_tn(f"ffn_dw1_{f}", xb, pl.BlockSpec((t, D), lambda j: (0, 0)),
                  dgu, pl.BlockSpec((None, t, FS), lambda j: (j, 0, 0)),
                  SDS((8, D, FS), BF16), pl.BlockSpec((None, D, FS), lambda j: (j, 0, 0)), (8,))


def ffn_dw2(a, df, f):
    t = df.shape[0]
    return _mm_tn(f"ffn_dw2_{f}", a, pl.BlockSpec((None, t, FS), lambda j: (j, 0, 0)),
                  df, pl.BlockSpec((t, D), lambda j: (0, 0)),
                  SDS((D_FF, D), BF16), pl.BlockSpec((FS, D), lambda j: (j, 0)), (4,))


def mm_tn_square(name, a, b):
    t = a.shape[0]
    return _mm_tn(name, a, pl.BlockSpec((t, 512), lambda i: (0, i)),
                  b, pl.BlockSpec((t, D), lambda i: (0, 0)),
                  SDS((D, D), BF16), pl.BlockSpec((512, D), lambda i: (i, 0)), (2,))


def mix_proj(xb, win):
    t = xb.shape[0]
    tm = _row_tile(t, 2048)

    def body(x_ref, w_ref, o_ref):
        o_ref[...] = _dot(x_ref[...], w_ref[...], NN).astype(BF16)

    return pl.pallas_call(
        body, name="mix_proj", grid=(7, t // tm),
        in_specs=[pl.BlockSpec((tm, D), lambda n, i: (i, 0)), pl.BlockSpec((D, D), lambda n, i: (0, n))],
        out_specs=pl.BlockSpec((tm, D), lambda n, i: (i, n)), out_shape=SDS((t, D_IN), BF16),
        compiler_params=_cp("parallel", "parallel"),
    )(xb, win)


def _pcol(tm, k):
    return pl.BlockSpec((tm, D), lambda i: (i, k))


def _prev_halo(tm, k):
    return pl.BlockSpec((8, D), lambda i: (jnp.maximum(i * (tm // 8) - 1, 0), k))


def _prev_halo16(tm, k):
    return pl.BlockSpec((16, D), lambda i: (jnp.maximum(i * (tm // 16) - 1, 0), k))


def _next_halo(tm, t, k):
    return pl.BlockSpec((8, D), lambda i: (jnp.minimum((i + 1) * (tm // 8), t // 8 - 1), k))


def _full(shape):
    nd = len(shape)
    return pl.BlockSpec(shape, lambda i: (0,) * nd)


def mix_pre(p, small, rcb, ap, rgw):
    t = p.shape[0]
    tm = _row_tile(t, 256)

    def body(bg_ref, cg_ref, v_ref, xr_ref, cgh_ref, vh_ref, xrh_ref, sm_ref, rcb_ref, ap_ref, rgw_ref,
             ca_ref, pa_ref, xc_ref, xcb_ref, gi_ref, gr_ref, h_ref, ext1, ext2, a_s, b_s, carry):
        i = pl.program_id(0)
        first = i == 0
        cv = cg_ref[...].astype(F32) * v_ref[...].astype(F32)
        ext1[0:16, :] = jnp.where(first, 0.0, cgh_ref[...].astype(F32) * vh_ref[...].astype(F32))
        ext1[16:, :] = cv
        xr = xr_ref[...].astype(F32)
        ext2[0:16, :] = jnp.where(first, 0.0, xrh_ref[...].astype(F32))
        ext2[16:, :] = xr
        ca = (sm_ref[R_SC:R_SC + 1, :] * ext1[pl.ds(14, tm), :] + sm_ref[R_SC + 1:R_SC + 2, :] * ext1[pl.ds(15, tm), :]
              + sm_ref[R_SC + 2:R_SC + 3, :] * cv)
        ca_ref[...] = ca.astype(BF16)
        pa_ref[...] = (bg_ref[...].astype(F32) * ca).astype(BF16)
        xc = (sm_ref[R_RC:R_RC + 1, :] * ext2[pl.ds(13, tm), :] + sm_ref[R_RC + 1:R_RC + 2, :] * ext2[pl.ds(14, tm), :]
              + sm_ref[R_RC + 2:R_RC + 3, :] * ext2[pl.ds(15, tm), :] + sm_ref[R_RC + 3:R_RC + 4, :] * xr
              + rcb_ref[...])
        xc_ref[...] = xc
        xcb = xc.astype(BF16)
        xcb_ref[...] = xcb
        g0, g1 = [], []
        for h in range(HEADS):
            xh = xcb[:, h * HD:(h + 1) * HD]
            g0.append(_dot(xh, rgw_ref[0, h], NN))
            g1.append(_dot(xh, rgw_ref[1, h], NN))
        gi = _sigmoid(jnp.concatenate(g0, axis=1) + sm_ref[R_RGB:R_RGB + 1, :])
        gr = _sigmoid(jnp.concatenate(g1, axis=1) + sm_ref[R_RGB + 1:R_RGB + 2, :])
        gi_ref[...] = gi
        gr_ref[...] = gr
        la = (-LRU_C) * gr * _softplus(-ap_ref[...])
        a_s[...] = jnp.exp(la)
        row = lax.broadcasted_iota(jnp.int32, (tm, D), 0) + i * tm
        mult = jnp.where(row == 0, 1.0, jnp.sqrt(_neg_expm1(2.0 * la)))
        b_s[...] = xc * gi * mult

        @pl.when(first)
        def _():
            carry[...] = jnp.zeros_like(carry)

        carry[...] = _scan_tile(a_s, b_s, a_s, carry[...], tm, reverse=False)
        h_ref[...] = a_s[...].astype(BF16)

    row = pl.BlockSpec((tm, D), lambda i: (i, 0))
    f32o, b16o = SDS((t, D), F32), SDS((t, D), BF16)
    ext, tile = pltpu.VMEM((tm + 16, D), F32), pltpu.VMEM((tm, D), F32)
    return pl.pallas_call(
        body, name="mix_pre", grid=(t // tm,),
        in_specs=[_pcol(tm, 0), _pcol(tm, 1), _pcol(tm, 2), _pcol(tm, 3),
                  _prev_halo16(tm, 1), _prev_halo16(tm, 2), _prev_halo16(tm, 3),
                  _full((SMALL_ROWS, D)), _full((1, D)), _full((1, D)), _full((2, HEADS, HD, HD))],
        out_specs=(row,) * 7, out_shape=(b16o, b16o, f32o, b16o, f32o, f32o, b16o),
        scratch_shapes=[ext, ext, tile, tile, pltpu.VMEM((8, D), F32)],
        compiler_params=_cp("arbitrary"),
    )(p, p, p, p, p, p, p, small, rcb, ap, rgw)


def _scan_tile(a_ref, b_ref, o_ref, carry, tm, reverse):
    width = a_ref.shape[1]
    ng = tm // 8
    row8 = lax.broadcasted_iota(jnp.int32, (8, width), 0)

    def step(g, c):
        r = pl.multiple_of((ng - 1 - g if reverse else g) * 8, 8)
        aa = a_ref[pl.ds(r, 8), :]
        bb = b_ref[pl.ds(r, 8), :]
        for s in (1, 2, 4):
            if reverse:
                keep, shift = row8 < 8 - s, 8 - s
            else:
                keep, shift = row8 >= s, s
            a_sh = jnp.where(keep, pltpu.roll(aa, shift, 0), 1.0)
            b_sh = jnp.where(keep, pltpu.roll(bb, shift, 0), 0.0)
            bb = aa * b_sh + bb
            aa = aa * a_sh
        o = aa * c + bb
        o_ref[pl.ds(r, 8), :] = o
        edge = o[0:1, :] if reverse else o[7:8, :]
        return jnp.broadcast_to(edge, (8, width))

    return lax.fori_loop(0, ng, step, carry)


def mix_out(pa, h, p, bm, w3, xf, small):
    t = xf.shape[0]
    tm = _row_tile(t, 256)

    def body(pa_ref, h_ref, yr_ref, gla_ref, glb_ref, bma_ref, bmb_ref, w_ref, x_ref, sm_ref,
             pb_ref, ya_ref, yb_ref, m_ref, xo_ref, xb_ref, xh_ref, rs_ref):
        ge, _ = _gelu(yr_ref[...].astype(F32))
        pb = (h_ref[...].astype(F32) * ge).astype(BF16)
        pb_ref[...] = pb
        ya = _dot(pa_ref[...], w_ref[0], NN)
        yb = _dot(pb, w_ref[1], NN)
        ya_ref[...] = ya.astype(BF16)
        yb_ref[...] = yb.astype(BF16)
        ga = _sigmoid_t(gla_ref[...].astype(F32) + bma_ref[...])
        gb = _sigmoid_t(glb_ref[...].astype(F32) + bmb_ref[...])
        m = (ga * ya + gb * yb).astype(BF16)
        m_ref[...] = m
        z = ALPHA * x_ref[...] + _dot(m, w_ref[2], NN)
        y, xh, rstd = _ln_fwd(z, sm_ref[R_LNG + 1:R_LNG + 2, :], sm_ref[R_LNB + 1:R_LNB + 2, :])
        xo_ref[...] = y
        xb_ref[...] = y.astype(BF16)
        xh_ref[...] = xh
        rs_ref[...] = rstd

    row = pl.BlockSpec((tm, D), lambda i: (i, 0))
    f32o, b16o = SDS((t, D), F32), SDS((t, D), BF16)
    return pl.pallas_call(
        body, name="mix_out", grid=(t // tm,),
        in_specs=[row, row, _pcol(tm, 4), _pcol(tm, 5), _pcol(tm, 6),
                  pl.BlockSpec((1, D), lambda i: (0, 0)), pl.BlockSpec((1, D), lambda i: (0, 1)),
                  _full((3, D, D)), row, _full((SMALL_ROWS, D))],
        out_specs=(row,) * 7 + (pl.BlockSpec((tm, 1), lambda i: (i, 0)),),
        out_shape=(b16o, b16o, b16o, b16o, f32o, b16o, f32o, SDS((t, 1), F32)),
        compiler_params=_cp("parallel"),
    )(pa, h, p, p, p, bm, bm, w3, xf, small)


def mixb_head(dxn, xh, rstd, small, w3, p, bm, ya, yb, ca, h):
    t = dxn.shape[0]
    tm = _row_tile(t, 256)

    def body(dy_ref, xh_ref, rs_ref, sm_ref, w_ref, bg_ref, yr_ref, gla_ref, glb_ref, bma_ref, bmb_ref,
             ya_ref, yb_ref, ca_ref, h_ref,
             dz_ref, dzb_ref, dya_ref, dyb_ref, dbg_ref, dca_ref, dh_ref, dphi_ref, dln_ref, dbm_ref):
        i = pl.program_id(0)
        dy = dy_ref[...]
        xhat = xh_ref[...]
        dz = _ln_bwd(dy, xhat, rs_ref[...], sm_ref[R_LNG + 1:R_LNG + 2, :])

        @pl.when(i == 0)
        def _():
            dln_ref[...] = jnp.zeros_like(dln_ref)
            dbm_ref[...] = jnp.zeros_like(dbm_ref)

        dln_ref[0:1, :] += jnp.sum(dy * xhat, axis=0, keepdims=True)
        dln_ref[1:2, :] += jnp.sum(dy, axis=0, keepdims=True)
        dz_ref[...] = dz
        dzb = dz.astype(BF16)
        dzb_ref[...] = dzb
        dm = _dot(dzb, w_ref[2], NT)
        ga = _sigmoid_t(gla_ref[...].astype(F32) + bma_ref[...])
        gb = _sigmoid_t(glb_ref[...].astype(F32) + bmb_ref[...])
        dya = (dm * ga).astype(BF16)
        dyb = (dm * gb).astype(BF16)
        dya_ref[...] = dya
        dyb_ref[...] = dyb
        dgla = dm * ya_ref[...].astype(F32) * ga * (1.0 - ga)
        dglb = dm * yb_ref[...].astype(F32) * gb * (1.0 - gb)
        dbm_ref[0:1, :] += jnp.sum(dgla, axis=0, keepdims=True)
        dbm_ref[1:2, :] += jnp.sum(dglb, axis=0, keepdims=True)
        dphi_ref[:, D:2 * D] = dgla.astype(BF16)
        dphi_ref[:, 2 * D:3 * D] = dglb.astype(BF16)
        dpa = _dot(dya, w_ref[0], NT)
        dpb = _dot(dyb, w_ref[1], NT)
        dbg_ref[...] = (dpa * ca_ref[...].astype(F32)).astype(BF16)
        dca_ref[...] = dpa * bg_ref[...].astype(F32)
        yr = yr_ref[...].astype(F32)
        ge, th = _gelu(yr)
        dh_ref[...] = (dpb * ge).astype(BF16)
        dphi_ref[:, 0:D] = (dpb * h_ref[...].astype(F32) * _gelu_grad(yr, th)).astype(BF16)

    row = pl.BlockSpec((tm, D), lambda i: (i, 0))
    f32o, b16o = SDS((t, D), F32), SDS((t, D), BF16)
    acc2 = pl.BlockSpec((2, D), lambda i: (0, 0))
    return pl.pallas_call(
        body, name="mixb_head", grid=(t // tm,),
        in_specs=[row, row, pl.BlockSpec((tm, 1), lambda i: (i, 0)), _full((SMALL_ROWS, D)), _full((3, D, D)),
                  _pcol(tm, 0), _pcol(tm, 4), _pcol(tm, 5), _pcol(tm, 6),
                  pl.BlockSpec((1, D), lambda i: (0, 0)), pl.BlockSpec((1, D), lambda i: (0, 1)),
                  row, row, row, row],
        out_specs=(row,) * 7 + (pl.BlockSpec((tm, 3 * D), lambda i: (i, 0)), acc2, acc2),
        out_shape=(f32o, b16o, b16o, b16o, b16o, f32o, b16o, SDS((t, 3 * D), BF16), SDS((2, D), F32), SDS((2, D), F32)),
        compiler_params=_cp("arbitrary"),
    )(dxn, xh, rstd, small, w3, p, p, p, p, bm, bm, ya, yb, ca, h)


def mixb_rec(dh, gr, gi, h, xc, ap, rgw):
    t = dh.shape[0]
    tm = _row_tile(t, 256)
    nt = t // tm

    def body(dh_ref, gr_ref, gi_ref, h_ref, hh_ref, xc_ref, ap_ref, rgw_ref, dg_ref, dxc_ref, red_ref,
             ext, ext_a, c_s, lam_s, lam_c, a_c):
        i = nt - 1 - pl.program_id(0)
        first = i == 0

        @pl.when(pl.program_id(0) == 0)
        def _():
            red_ref[...] = jnp.zeros_like(red_ref)
            lam_c[...] = jnp.zeros_like(lam_c)
            a_c[...] = jnp.zeros_like(a_c)

        ext[0:8, :] = jnp.where(first, 0.0, hh_ref[...].astype(F32)[8:16, :])
        ext[8:, :] = h_ref[...].astype(F32)
        hprev = ext[pl.ds(7, tm), :]
        lam_s[...] = dh_ref[...].astype(F32)
        gr = gr_ref[...]
        gi = gi_ref[...]
        xc = xc_ref[...]
        ap = ap_ref[...]
        sp = _softplus(-ap)
        la = (-LRU_C) * gr * sp
        a = jnp.exp(la)
        ext_a[0:tm, :] = a
        ext_a[tm:, :] = a_c[...]
        c_s[...] = ext_a[pl.ds(1, tm), :]
        lam_c[...] = _scan_tile(c_s, lam_s, lam_s, lam_c[...], tm, reverse=True)
        a_c[...] = jnp.broadcast_to(a[0:1, :], (8, D))
        lam = lam_s[...]
        row = lax.broadcasted_iota(jnp.int32, (tm, D), 0) + i * tm
        start = row == 0
        mult = jnp.where(start, 1.0, jnp.sqrt(_neg_expm1(2.0 * la)))
        dmult = jnp.where(start, 0.0, lam * xc * gi)
        dla = lam * hprev * a - dmult * a * a / mult
        dg1 = (-LRU_C) * sp * dla * gr * (1.0 - gr)
        dg0 = lam * xc * mult * gi * (1.0 - gi)
        dsp = jnp.sum((-LRU_C) * gr * dla, axis=0, keepdims=True)
        red_ref[0:1, :] += jnp.sum(dg0, axis=0, keepdims=True)
        red_ref[1:2, :] += jnp.sum(dg1, axis=0, keepdims=True)
        red_ref[2:3, :] += -dsp * _sigmoid(-ap)
        dg0b = dg0.astype(BF16)
        dg1b = dg1.astype(BF16)
        dg_ref[0] = dg0b
        dg_ref[1] = dg1b
        parts = []
        for hd in range(HEADS):
            sl = slice(hd * HD, (hd + 1) * HD)
            parts.append(_dot(dg0b[:, sl], rgw_ref[0, hd], NT) + _dot(dg1b[:, sl], rgw_ref[1, hd], NT))
        dxc = lam * gi * mult + jnp.concatenate(parts, axis=1)
        dxc_ref[...] = dxc
        red_ref[3:4, :] += jnp.sum(dxc, axis=0, keepdims=True)

    row = pl.BlockSpec((tm, D), lambda i: (nt - 1 - i, 0))
    halo = pl.BlockSpec((16, D), lambda i: (jnp.maximum((nt - 1 - i) * (tm // 16) - 1, 0), 0))
    ext, tile, edge = pltpu.VMEM((tm + 8, D), F32), pltpu.VMEM((tm, D), F32), pltpu.VMEM((8, D), F32)
    return pl.pallas_call(
        body, name="mixb_rec", grid=(nt,),
        in_specs=[row, row, row, row, halo, row, _full((1, D)), _full((2, HEADS, HD, HD))],
        out_specs=(pl.BlockSpec((2, tm, D), lambda i: (0, nt - 1 - i, 0)), row, pl.BlockSpec((8, D), lambda i: (0, 0))),
        out_shape=(SDS((2, t, D), BF16), SDS((t, D), F32), SDS((8, D), F32)),
        scratch_shapes=[ext, ext, tile, tile, edge, edge],
        compiler_params=_cp("arbitrary"),
    )(dh, gr, gi, h, h, xc, ap, rgw)


def mixb_conv(dca, dxc, dbg, p, small):
    t = dca.shape[0]
    tm = _row_tile(t, 256)
    nt = t // tm

    def body(dca_ref, dcan_ref, dxc_ref, dxcn_ref, dbg_ref, cg_ref, v_ref, xr_ref, cgh_ref, vh_ref, xrh_ref, sm_ref,
             dplo_ref, dxr_ref, red_ref, e_dca, e_dxc, e_cv, e_xr):
        i = pl.program_id(0)
        first = i == 0
        last = i == nt - 1

        @pl.when(first)
        def _():
            red_ref[...] = jnp.zeros_like(red_ref)

        dca = dca_ref[...]
        dxc = dxc_ref[...]
        e_dca[0:tm, :] = dca
        e_dca[tm:, :] = jnp.where(last, 0.0, dcan_ref[...])
        e_dxc[0:tm, :] = dxc
        e_dxc[tm:, :] = jnp.where(last, 0.0, dxcn_ref[...])
        cg = cg_ref[...].astype(F32)
        v = v_ref[...].astype(F32)
        xr = xr_ref[...].astype(F32)
        e_cv[0:8, :] = jnp.where(first, 0.0, (cgh_ref[...].astype(F32) * vh_ref[...].astype(F32))[8:16, :])
        e_cv[8:, :] = cg * v
        e_xr[0:8, :] = jnp.where(first, 0.0, xrh_ref[...].astype(F32)[8:16, :])
        e_xr[8:, :] = xr
        dcv = (sm_ref[R_SC + 2:R_SC + 3, :] * dca + sm_ref[R_SC + 1:R_SC + 2, :] * e_dca[pl.ds(1, tm), :]
               + sm_ref[R_SC:R_SC + 1, :] * e_dca[pl.ds(2, tm), :])
        dplo_ref[:, 0:D] = dbg_ref[...]
        dplo_ref[:, D:2 * D] = (dcv * v).astype(BF16)
        dplo_ref[:, 2 * D:3 * D] = (dcv * cg).astype(BF16)
        dxr = (sm_ref[R_RC + 3:R_RC + 4, :] * dxc + sm_ref[R_RC + 2:R_RC + 3, :] * e_dxc[pl.ds(1, tm), :]
               + sm_ref[R_RC + 1:R_RC + 2, :] * e_dxc[pl.ds(2, tm), :] + sm_ref[R_RC:R_RC + 1, :] * e_dxc[pl.ds(3, tm), :])
        dxr_ref[...] = dxr.astype(BF16)
        for k in range(3):
            red_ref[R_SC + k:R_SC + k + 1, :] += jnp.sum(dca * e_cv[pl.ds(6 + k, tm), :], axis=0, keepdims=True)
        for k in range(4):
            red_ref[R_RC + k:R_RC + k + 1, :] += jnp.sum(dxc * e_xr[pl.ds(5 + k, tm), :], axis=0, keepdims=True)

    row = pl.BlockSpec((tm, D), lambda i: (i, 0))
    ext = pltpu.VMEM((tm + 8, D), F32)
    return pl.pallas_call(
        body, name="mixb_conv", grid=(nt,),
        in_specs=[row, _next_halo(tm, t, 0), row, _next_halo(tm, t, 0), row,
                  _pcol(tm, 1), _pcol(tm, 2), _pcol(tm, 3), _prev_halo16(tm, 1), _prev_halo16(tm, 2), _prev_halo16(tm, 3),
                  _full((SMALL_ROWS, D))],
        out_specs=(pl.BlockSpec((tm, 3 * D), lambda i: (i, 0)), row, pl.BlockSpec((8, D), lambda i: (0, 0))),
        out_shape=(SDS((t, 3 * D), BF16), SDS((t, D), BF16), SDS((8, D), F32)),
        scratch_shapes=[ext, ext, ext, ext],
        compiler_params=_cp("arbitrary"),
    )(dca, dca, dxc, dxc, dbg, p, p, p, p, p, p, small)


def mixb_dx(dz, dplo, dxr, dphi, win):
    t = dz.shape[0]
    tm = _row_tile(t, 1024)

    def body(dz_ref, lo_ref, xr_ref, hi_ref, w_ref, dx_ref, acc):
        k = pl.program_id(1)

        @pl.when(k == 0)
        def _():
            acc[...] = ALPHA * dz_ref[...]

        @pl.when(k < 3)
        def _():
            acc[...] += _dot(lo_ref[...], w_ref[...], NT)

        @pl.when(k == 3)
        def _():
            acc[...] += _dot(xr_ref[...], w_ref[...], NT)

        @pl.when(k > 3)
        def _():
            acc[...] += _dot(hi_ref[...], w_ref[...], NT)

        @pl.when(k == 6)
        def _():
            dx_ref[...] = acc[...]

    row = pl.BlockSpec((tm, D), lambda i, k: (i, 0))
    return pl.pallas_call(
        body, name="mixb_dx", grid=(t // tm, 7),
        in_specs=[row, pl.BlockSpec((tm, D), lambda i, k: (i, jnp.minimum(k, 2))), row,
                  pl.BlockSpec((tm, D), lambda i, k: (i, jnp.clip(k - 4, 0, 2))),
                  pl.BlockSpec((D, D), lambda i, k: (0, k))],
        out_specs=row, out_shape=SDS((t, D), F32),
        scratch_shapes=[pltpu.VMEM((tm, D), F32)],
        compiler_params=_cp("parallel", "arbitrary"),
    )(dz, dplo, dxr, dphi, win)


def mixb_dwin(xb, dplo, dxr, dphi):
    t = xb.shape[0]
    tk = _row_tile(t, 2048)
    nk = t // tk

    def body(x_ref, lo_ref, xr_ref, hi_ref, o_ref, acc):
        n = pl.program_id(0)
        k = pl.program_id(1)

        @pl.when(k == 0)
        def _():
            acc[...] = jnp.zeros_like(acc)

        @pl.when(n < 3)
        def _():
            acc[...] += _dot(x_ref[...], lo_ref[...], TN)

        @pl.when(n == 3)
        def _():
            acc[...] += _dot(x_ref[...], xr_ref[...], TN)

        @pl.when(n > 3)
        def _():
            acc[...] += _dot(x_ref[...], hi_ref[...], TN)

        @pl.when(k == nk - 1)
        def _():
            o_ref[...] = acc[...].astype(BF16)

    return pl.pallas_call(
        body, name="mixb_dwin", grid=(7, nk),
        in_specs=[pl.BlockSpec((tk, D), lambda n, k: (k, 0)),
                  pl.BlockSpec((tk, D), lambda n, k: (jnp.where(n < 3, k, 0), jnp.minimum(n, 2))),
                  pl.BlockSpec((tk, D), lambda n, k: (jnp.where(n == 3, k, 0), 0)),
                  pl.BlockSpec((tk, D), lambda n, k: (jnp.where(n > 3, k, 0), jnp.clip(n - 4, 0, 2)))],
        out_specs=pl.BlockSpec((D, D), lambda n, k: (0, n)), out_shape=SDS((D, D_IN), BF16),
        scratch_shapes=[pltpu.VMEM((D, D), F32)],
        compiler_params=_cp("parallel", "arbitrary"),
    )(xb, dplo, dxr, dphi)


def mixb_drgw(xcb, dg):
    t = xcb.shape[0]
    return _mm_tn("mixb_drgw", xcb, pl.BlockSpec((t, HD), lambda g, h: (0, h)),
                  dg, pl.BlockSpec((None, t, HD), lambda g, h: (g, 0, h)),
                  SDS((2, HEADS, HD, HD), BF16), pl.BlockSpec((None, None, HD, HD), lambda g, h: (g, h, 0, 0)),
                  (2, HEADS))


def loss_head(y, tgt):
    t = y.shape[0]
    tm = _row_tile(t, 512)

    def body(y_ref, t_ref, dy_ref, l_ref):
        i = pl.program_id(0)
        e = y_ref[...] - t_ref[...]
        dy_ref[...] = e * (1.0 / D)

        @pl.when(i == 0)
        def _():
            l_ref[...] = jnp.zeros_like(l_ref)

        l_ref[...] += 0.5 * jnp.sum(jnp.mean(e * e, axis=-1, keepdims=True), axis=0, keepdims=True)

    row = pl.BlockSpec((tm, D), lambda i: (i, 0))
    return pl.pallas_call(
        body, name="loss_head", grid=(t // tm,), in_specs=[row, row],
        out_specs=(row, pl.BlockSpec((8, 128), lambda i: (0, 0))),
        out_shape=(SDS((t, D), F32), SDS((8, 128), F32)),
        compiler_params=_cp("arbitrary"),
    )(y, tgt)


def _adamw(w, g, m, v):
    m = ADAM_B1 * m + (1.0 - ADAM_B1) * g
    v = ADAM_B2 * v + (1.0 - ADAM_B2) * (g * g)
    m_hat = m / (1.0 - ADAM_B1 ** ADAM_STEP)
    v_hat = v / (1.0 - ADAM_B2 ** ADAM_STEP)
    delta = -ADAM_LR * (m_hat / (jnp.sqrt(v_hat) + ADAM_EPS) + ADAM_WD * w)
    return delta, m, v


def adam_big(name, w, m, v, parts, rows, lanes, blk_off, tr, l0, prev=None, after=()):
    nr = rows // tr
    nl = len(parts)

    def body(w_ref, m_ref, v_ref, *rest):
        g_ref, d_ref, mo_ref, vo_ref = rest[-4:]
        l = pl.program_id(0)
        for ll in range(nl):
            pr = rest[ll]

            @pl.when(l == ll)
            def _():
                g = pr[0].astype(F32)
                for s in range(1, N_DEV):
                    g = g + pr[s].astype(F32)
                g_ref[...] = g
                d, mn, vn = _adamw(w_ref[...], g, m_ref[...], v_ref[...])
                d_ref[...] = d
                mo_ref[...] = mn
                vo_ref[...] = vn

    blk = pl.BlockSpec((None, tr, lanes), lambda l, r: (l + l0, r, 0))

    def part_spec(ll):
        return pl.BlockSpec((N_DEV, tr, lanes), lambda l, r: (0, jnp.where(l == ll, r, 0) + blk_off, 0))

    out = SDS(w.shape, F32)
    extra = [] if prev is None else list(prev)
    return pl.pallas_call(
        body, name=name, grid=(nl, nr),
        in_specs=[blk, blk, blk] + [part_spec(ll) for ll in range(nl)] + [ANY] * (len(extra) + len(after)),
        out_specs=(blk,) * 4, out_shape=(out,) * 4,
        input_output_aliases={3 + nl + i: i for i in range(len(extra))},
        compiler_params=_cp("parallel", "parallel"),
    )(w, m, v, *parts, *extra, *after)


def small_update(me1, land_sh, land_rep, sharded, replicated):
    ns, nr = len(sharded), len(replicated)

    def body(me_ref, lsh, lrep, *refs):
        ins, outs = refs[:3 * (ns + nr)], refs[3 * (ns + nr):]

        def total(read):
            g = read(0)
            for s in range(1, N_DEV):
                g = g + read(s)
            return g

        def update(k, g, sl):
            w_ref, m_ref, v_ref = ins[3 * k:3 * k + 3]
            d, mn, vn = _adamw(w_ref[sl], g, m_ref[sl], v_ref[sl])
            for o, val in zip(outs[4 * k:4 * k + 4], (g, d, mn, vn)):
                o[sl] = val

        for k, (r0, n) in enumerate(GS_ROWS):
            update(k, total(lambda s: lsh[s, :, r0:r0 + n, :]), (slice(None),) * 3)
        lo, hi = (slice(None), slice(0, D)), (slice(None), slice(D, 2 * D))
        update(ns, total(lambda s: lrep[s, 0:4, :]), lo)
        update(ns, total(lambda s: lrep[s, 4:8, :]), hi)
        update(ns + 1, total(lambda s: lrep[s, 8:12, :]), (slice(None),) * 2)
        update(ns + 2, total(lambda s: lrep[s, 12:16, :]), (slice(None),) * 2)

    def whole(a):
        nd = a.ndim
        return pl.BlockSpec(a.shape, lambda i, me: (0,) * nd)

    params = [a for wmv in list(sharded) + list(replicated) for a in wmv]
    out_shape = tuple(SDS(wmv[0].shape, F32) for wmv in list(sharded) + list(replicated) for _ in range(4))
    res = pl.pallas_call(
        body, name="small_update", out_shape=out_shape,
        grid_spec=pltpu.PrefetchScalarGridSpec(
            num_scalar_prefetch=1, grid=(1,),
            in_specs=[pl.BlockSpec((N_DEV, DEPTH, 16, 128), lambda i, me: (0, 0, 0, me[0])), whole(land_rep)]
            + [whole(a) for a in params],
            out_specs=tuple(pl.BlockSpec(s.shape, lambda i, me, nd=len(s.shape): (0,) * nd) for s in out_shape)),
        compiler_params=_cp("arbitrary"),
    )(me1, land_sh, land_rep, *params)
    return [list(res[4 * k:4 * k + 4]) for k in range(ns + nr)]


def _ffn_forward(xf, xb, gw, f, s, between=None):
    g, u, a = ffn_up(xb, gw["w1"], f)
    if between is not None:
        a = between(a)
    xo, xob, xh, rs = ffn_down_ln(a, gw["w2"], f, xf, gw["small"], s)
    return (xo, xob), dict(xb=xb, g=g, u=u, a=a, xh=xh, rs=rs)


def _ffn_backward(dxn, sv, gw, f, s):
    dz, df, dgu, dln = ffn_bwd_gates(dxn, sv["xh"], sv["rs"], gw["small"], s, gw["w2"], f, sv["g"], sv["u"])
    dx = ffn_bwd_dx(dz, dgu, gw["w1"], f)
    dw1 = ffn_dw1(sv["xb"], dgu, f)
    dw2 = ffn_dw2(sv["a"], df, f)
    return dx, dw1, dw2, dln


def _mixer_forward(xf, xb, gw, rcb, ap, bm):
    p = mix_proj(xb, gw["win"])
    ca, pa, xc, xcb, gi, gr, h = mix_pre(p, gw["small"], rcb, ap, gw["rgw"])
    pb, ya, yb, m, xo, xob, xh, rs = mix_out(pa, h, p, bm, gw["w3"], xf, gw["small"])
    sv = dict(xb=xb, p=p, ca=ca, pa=pa, xc=xc, xcb=xcb, gi=gi, gr=gr, h=h, pb=pb, ya=ya, yb=yb, m=m, xh=xh, rs=rs)
    return (xo, xob), sv


def _mixer_backward(dxn, sv, gw, rcb, ap, bm):
    dz, dzb, dya, dyb, dbg, dca, dh, dphi, dln, dbm = mixb_head(
        dxn, sv["xh"], sv["rs"], gw["small"], gw["w3"], sv["p"], bm, sv["ya"], sv["yb"], sv["ca"], sv["h"])
    dg, dxc, red_rec = mixb_rec(dh, sv["gr"], sv["gi"], sv["h"], sv["xc"], ap, gw["rgw"])
    dplo, dxr, red_conv = mixb_conv(dca, dxc, dbg, sv["p"], gw["small"])
    dx = mixb_dx(dz, dplo, dxr, dphi, gw["win"])
    dwin = mixb_dwin(sv["xb"], dplo, dxr, dphi)
    dwo = mm_tn_square("mixb_dwo", sv["m"], dzb)
    dwoc = mm_tn_square("mixb_dwoc", sv["pa"], dya)
    dwor = mm_tn_square("mixb_dwor", sv["pb"], dyb)
    drgw = mixb_drgw(sv["xcb"], dg)
    return dx, dict(dwin=dwin, dwoc=dwoc, dwor=dwor, dwo=dwo, drgw=drgw), dln, dbm, red_rec, red_conv


def kernel(x, w_in, b_merge, sc_w, rc_w, rc_b, rg_w, rg_b, a_param, w_out_conv, w_out_rnn, w_o, ffn_w1, ffn_w2, ln_g, ln_b, loss_target, m_w_in, m_b_merge, m_sc_w, m_rc_w, m_rc_b, m_rg_w, m_rg_b, m_a_param, m_w_out_conv, m_w_out_rnn, m_w_o, m_ffn_w1, m_ffn_w2, m_ln_g, m_ln_b, v_w_in, v_b_merge, v_sc_w, v_rc_w, v_rc_b, v_rg_w, v_rg_b, v_a_param, v_w_out_conv, v_w_out_rnn, v_w_o, v_ffn_w1, v_ffn_w2, v_ln_g, v_ln_b):
    t = x.shape[1]
    me = _me()

    def rows(parts, total):
        out, off = None, 0
        for part in parts:
            r = part.shape[-2]
            pad = [(0, 0)] * (part.ndim - 2) + [(off, total - off - r), (0, 0)]
            padded = jnp.pad(part, pad)
            out = padded if out is None else out + padded
            off += r
        return out

    small = rows([sc_w, rc_w, rg_b, ln_g, ln_b], SMALL_ROWS)

    me1 = jnp.reshape(me, (1,)).astype(jnp.int32)

    def layer_shards(l, small_now):
        return gather_prep(l, me1, w_in, w_out_conv, w_out_rnn, w_o, ffn_w1, ffn_w2, rg_w, small_now)

    def as_weights(lands):
        return dict(zip(("win", "w3", "w1", "w2", "rgw", "small"), lands))

    xf = x.reshape(t, D)
    xb = xf.astype(BF16)

    early_at, rest_at = [2, 3, 5], [0, 1, 2, 3, 4]

    def ffn_slots(f, o_w1, o_w2, p):
        return [o_w1.at[f, p], o_w2.at[f, _slab(p, W2S, 16), :]]

    def early_slots(lands, p):
        return ffn_slots(0, lands[0], lands[1], p) + [lands[2].at[:, _slab(p, 128, 128)]]

    def rest_slots(lands, p):
        o_win, o_w3, o_w1, o_w2, o_rgw = lands
        return ([o_win.at[:, _slab(p, WIN_S, 128)], o_w3.at[:, _slab(p, 128, 128), :]] + ffn_slots(1, o_w1, o_w2, p)
                + [o_rgw.at[:, :, _slab(p, 32, 32), :]])

    early = ChipGather("gather0a", 3, 3, 3, lambda refs: [refs[0].at[0], refs[1].at[0], refs[2]], early_slots)
    rest = ChipGather("gather0b", 5, 5, 5, lambda refs: [refs[0], refs[1], refs[2].at[1], refs[3].at[1], refs[4]],
                      rest_slots)
    shards, own = layer_shards(0, small)

    def step(method, at, *args):
        res = method(*args[:-1], [shards[i] for i in at], [own[i] for i in at], args[-1])
        for i, s_new, o_new in zip(at, res[-3], res[-2]):
            shards[i], own[i] = s_new, o_new
        return tuple(res[:-3]) + (res[-1],)

    sems_a, thru = step(early.first, early_at, [xb, small])
    sems_b, thru = step(rest.first, rest_at, thru)
    relay_a, thru = step(early.relay, early_at, sems_a, thru)
    ((xb, small),) = step(early.last, early_at, sems_a, relay_a, thru)
    cur, s0 = _ffn_forward(xf, xb, as_weights(own), 0, 0)
    relay_b, thru = step(rest.relay, rest_at, sems_b, list(cur) + [small])
    (thru,) = step(rest.last, rest_at, sems_b, relay_b, thru)
    cur, small = thru[:2], thru[2]

    gathers = [ChipGather(f"gather{l}") for l in range(DEPTH)]
    gws, saved = [as_weights(own)], []
    for l in range(DEPTH):
        gw = gws[l]
        nxt = l + 1 < DEPTH
        if nxt:
            shards, own = layer_shards(l + 1, small)
            sems, shards, own, thru = gathers[l + 1].first(shards, own, list(cur) + [small])
            cur, small = thru[:2], thru[2]
        rcb, ap, bm = rc_b[l][None], a_param[l][None], b_merge[l][None]
        if l > 0:
            cur, s0 = _ffn_forward(cur[0], cur[1], gw, 0, 0)
        cur, s1 = _mixer_forward(cur[0], cur[1], gw, rcb, ap, bm)
        if nxt and l > 0:
            relay_sems, shards, own, cur = gathers[l + 1].relay(sems, shards, own, cur)
        if nxt and l == 0:
            def late_relay(a):
                nonlocal relay_sems, shards, own
                relay_sems, shards, own, (a,) = gathers[1].relay(sems, shards, own, [a])
                return a

            cur, s2 = _ffn_forward(cur[0], cur[1], gw, 1, 2, late_relay)
        else:
            cur, s2 = _ffn_forward(cur[0], cur[1], gw, 1, 2)
        saved.append((s0, s1, s2))
        if nxt:
            _, own, cur = gathers[l + 1].last(sems, relay_sems, shards, own, cur)
            gws.append(as_weights(own))

    dy, loss_tile = loss_head(cur[0], loss_target.reshape(t, D))
    loss = lax.psum(loss_tile[0, 0], ("x", "y", "c"))

    lands = [dict() for _ in range(DEPTH)]
    gsmall, grep = [None] * DEPTH, [None] * DEPTH
    flights = {}

    def launch(key, kinds, grads, dy):
        ex, prep = grad_scatter(f"scatter{key}", kinds)
        fl = ex.start(grads, prep(me1, grads), [] if dy is None else [dy])
        flights[key] = (ex, fl)
        return None if dy is None else fl[4][0]

    def landed(key, thru):
        ex, fl = flights.pop(key)
        return ex.wait(fl[0], fl[1], fl[2], fl[3], thru)

    ffn_b, mixer, ffn_a = ("w1b", "w2b"), ("w_in", "w3", "rgw"), ("w1a", "w2a")
    SPLIT = (0, 1)
    for l in reversed(range(DEPTH)):
        gw = gws[l]
        rcb, ap, bm = rc_b[l][None], a_param[l][None], b_merge[l][None]
        s0, s1, s2 = saved[l]
        dy, dw1b, dw2b, dln2 = _ffn_backward(dy, s2, gw, 1, 2)
        if l in SPLIT:
            dy = launch(f"{l}b", ["ffn"], [dw1b, dw2b], dy)
        dy, dmix, dln1, dbm, red_rec, red_conv = _mixer_backward(dy, s1, gw, rcb, ap, bm)
        mixer_grads = [dmix["dwin"], dmix["dwoc"], dmix["dwor"], dmix["dwo"], dmix["drgw"]]
        if l in SPLIT:
            dy = launch(f"{l}m", ["mixer"], mixer_grads, dy)
        dy, dw1a, dw2a, dln0 = _ffn_backward(dy, s0, gw, 0, 0)
        if l + 1 < DEPTH and l + 1 not in SPLIT:
            got, (dy,) = landed(str(l + 1), [dy])
            lands[l + 1] = dict(zip(ffn_b + mixer + ffn_a, got))
        if l == 0:
            lands[DEPTH - 1]["w_in"] = launch("0a", ["ffn"], [dw1a, dw2a], lands[DEPTH - 1]["w_in"])
        elif l in SPLIT:
            dy = launch(f"{l}a", ["ffn"], [dw1a, dw2a], dy)
        else:
            dy = launch(str(l), ["ffn", "mixer", "ffn"], [dw1b, dw2b] + mixer_grads + [dw1a, dw2a], dy)
        gsmall[l] = rows([red_conv[0:8], red_rec[0:2], dln0[0:1], dln1[0:1], dln2[0:1], dln0[1:2], dln1[1:2],
                          dln2[1:2]], 16)
        grep[l] = [dbm[0:1], dbm[1:2], red_rec[3:4], red_rec[2:3]]
    grad_x = dy.reshape(1, t, D)

    g_rep = rows([grep[l][kind] for kind in range(4) for l in range(DEPTH)], 16)
    land_sh, land_rep = gather_small_grads(jnp.stack(gsmall, axis=0), g_rep)
    small_names = ("sc_w", "rc_w", "rg_b", "ln_g", "ln_b", "b_merge", "rc_b", "a_param")
    small_res = small_update(
        me1, land_sh, land_rep,
        [(sc_w, m_sc_w, v_sc_w), (rc_w, m_rc_w, v_rc_w), (rg_b, m_rg_b, v_rg_b), (ln_g, m_ln_g, v_ln_g),
         (ln_b, m_ln_b, v_ln_b)],
        [(b_merge, m_b_merge, v_b_merge), (rc_b, m_rc_b, v_rc_b), (a_param, m_a_param, v_a_param)])

    def parts_of(layers):
        ls = [lands[l] for l in layers]
        return dict(w_in=[a["w_in"] for a in ls],
                    w3=[a["w3"].reshape(N_DEV, 3 * 128, D) for a in ls],
                    rgw=[a["rgw"].reshape(N_DEV, 2 * HEADS * 32, HD) for a in ls],
                    w1=[a[k] for a in ls for k in ("w1a", "w1b")],
                    w2=[a[k] for a in ls for k in ("w2a", "w2b")])

    families = [("w_in", w_in, m_w_in, v_w_in, "w_in", 1, D, WIN_S, 0, 128),
                ("w_out_conv", w_out_conv, m_w_out_conv, v_w_out_conv, "w3", 1, 128, D, 0, 128),
                ("w_out_rnn", w_out_rnn, m_w_out_rnn, v_w_out_rnn, "w3", 1, 128, D, 1, 128),
                ("w_o", w_o, m_w_o, v_w_o, "w3", 1, 128, D, 2, 128),
                ("ffn_w1", ffn_w1, m_ffn_w1, v_ffn_w1, "w1", 2, D, FS, 0, 256),
                ("ffn_w2", ffn_w2, m_ffn_w2, v_ffn_w2, "w2", 2, W2S, D, 0, W2S // 2),
                ("rg_w", rg_w, m_rg_w, v_rg_w, "rgw", 1, 2 * HEADS * 32, HD, 0, 256)]

    def adam_pass(tag, layers, prev):
        parts = parts_of(layers)
        outs = {}
        for name, w, m, v, fam, per, nrow, lanes, blk_off, tr in families:
            r3 = lambda a: a.reshape(DEPTH * per, nrow, lanes)
            outs[name] = adam_big(f"adam_{name}_{tag}", r3(w), r3(m), r3(v), parts[fam], nrow, lanes, blk_off, tr,
                                  per * layers[0], None if prev is None else prev[name],
                                  (dy,) if prev is None else ())
        return outs

    def through(outs, key, extra):
        thru = [outs[f[0]][0] for f in families] + list(extra)
        got, thru = landed(key, thru)
        outs = {f[0]: [thru[i]] + list(outs[f[0]][1:]) for i, f in enumerate(families)}
        return got, outs, thru[len(families):]

    done = adam_pass("top", list(range(2, DEPTH)), None)
    for l, tag in ((1, "second"), (0, "first")):
        for part, names in (("b", ffn_b), ("m", mixer), ("a", ffn_a)):
            got, done, _ = through(done, f"{l}{part}", [])
            lands[l].update(zip(names, got))
        done = adam_pass(tag, [l], done)
    res = {f[0]: [o.reshape(f[1].shape) for o in done[f[0]]] for f in families}

    res.update(zip(small_names, small_res))

    names = ["w_in", "b_merge", "sc_w", "rc_w", "rc_b", "rg_w", "rg_b", "a_param", "w_out_conv", "w_out_rnn", "w_o",
             "ffn_w1", "ffn_w2", "ln_g", "ln_b"]
    out = [loss, grad_x]
    for k in range(4):
        out += [res[n][k] for n in names]
    return tuple(out)
```
